```python
import jax, jax.numpy as jnp
from jax import lax
import numpy as np

D_MODEL = 1024
BATCH = 8
SEQ = 16384
DEPTH = 1

CHUNK = 64
LEFT_CHUNKS = 8
HEAD_DIM = 64
N_HEADS_A = 8
N_HEADS_B = 8
WIDTH_A = N_HEADS_A * HEAD_DIM
WIDTH_B = N_HEADS_B * HEAD_DIM
MAX_REL = 128
N_REL = 2 * MAX_REL + 1
QUERY_BLOCK = 128
D_FF = 2816
CONV_WIDTH = 3
IN_WIDTH = 3 * WIDTH_A + 3 * WIDTH_B + 2 * D_MODEL
EPS = 1e-6

kernel_name = "hybrid_chunked_stickbreaking_convglu_block"


def rms_norm(x, gain):
    xf = x.astype(jnp.float32)
    y = xf * lax.rsqrt(jnp.mean(xf * xf, axis=-1, keepdims=True) + EPS)
    return (y * gain.astype(jnp.float32)).astype(x.dtype)


def chunked_attention(q, k, v, q_gain, k_gain, rel_bias):
    b, s, h, dh = q.shape
    nc = s // CHUNK
    band = (LEFT_CHUNKS + 1) * CHUNK
    q = rms_norm(q, q_gain)
    k = rms_norm(k, k_gain)
    qc = q.reshape(b, nc, CHUNK, h, dh)
    pad = ((0, 0), (LEFT_CHUNKS * CHUNK, 0), (0, 0), (0, 0))
    kp = jnp.pad(k, pad).reshape(b, nc + LEFT_CHUNKS, CHUNK, h, dh)
    vp = jnp.pad(v, pad).reshape(b, nc + LEFT_CHUNKS, CHUNK, h, dh)
    kb = jnp.concatenate([kp[:, i:i + nc] for i in range(LEFT_CHUNKS + 1)], axis=2)
    vb = jnp.concatenate([vp[:, i:i + nc] for i in range(LEFT_CHUNKS + 1)], axis=2)
    scores = jnp.einsum('bnqhd,bnkhd->bnhqk', qc, kb).astype(jnp.float32) * (dh ** -0.5)
    r = jnp.arange(CHUNK)[:, None]
    key_off = jnp.arange(band)[None, :] - LEFT_CHUNKS * CHUNK
    rel = r - key_off
    idx = jnp.clip(rel, -MAX_REL, MAX_REL) + MAX_REL
    bias = rel_bias[:, idx].astype(jnp.float32)
    valid = (jnp.arange(nc)[:, None] * CHUNK + key_off) >= 0
    scores = jnp.where(valid[None, :, None, None, :], scores + bias[None, None], -jnp.inf)
    probs = jax.nn.softmax(scores, axis=-1).astype(v.dtype)
    out = jnp.einsum('bnhqk,bnkhd->bnqhd', probs, vb)
    return out.reshape(b, s, h * dh)


def stick_breaking_attention(q, k, v):
    b, s, h, dh = q.shape
    nb = s // QUERY_BLOCK
    kh = k.transpose(0, 2, 1, 3)
    vh = v.transpose(0, 2, 1, 3)
    qblocks = q.transpose(0, 2, 1, 3).reshape(b, h, nb, QUERY_BLOCK, dh).transpose(2, 0, 1, 3, 4)
    key_pos = jnp.arange(s)

    def block(args):
        qb, blk = args
        z = jnp.einsum('bhqd,bhkd->bhqk', qb, kh).astype(jnp.float32) * (dh ** -0.5)
        q_pos = blk * QUERY_BLOCK + jnp.arange(QUERY_BLOCK)
        causal = key_pos[None, :] < q_pos[:, None]
        log_keep = jnp.where(causal, -jax.nn.softplus(z), 0.0)
        log_after = lax.cumsum(log_keep, axis=3, reverse=True) - log_keep
        w = jnp.where(causal, jnp.exp(jax.nn.log_sigmoid(z) + log_after), 0.0)
        return jnp.einsum('bhqk,bhkd->bhqd', w.astype(vh.dtype), vh)

    out = lax.map(block, (qblocks, jnp.arange(nb)))
    return out.transpose(1, 0, 3, 2, 4).reshape(b, s, h * dh)


def conv_glu(x, w_up, conv_w, conv_b, w_down):
    hid = x @ w_up
    hid = lax.conv_general_dilated(
        hid, conv_w[:, None, :], window_strides=(1,),
        padding=((CONV_WIDTH - 1, 0),),
        dimension_numbers=('NWC', 'WIO', 'NWC'),
        feature_group_count=2 * D_FF) + conv_b
    gate, up = hid[..., :D_FF], hid[..., D_FF:]
    return (jax.nn.silu(gate) * up) @ w_down


def _fwd_setup_inputs(seed: int = 0) -> dict:
    key = jax.random.key(seed)
    ks = jax.random.split(key, 16)
    f32 = jnp.float32
    nrm = lambda k, shape, scale: jax.random.normal(k, shape, f32) * scale
    return {
        "x": nrm(ks[0], (BATCH, SEQ, D_MODEL), 1.0),
        "norm1_g": 1.0 + nrm(ks[1], (DEPTH, D_MODEL), 0.02),
        "w_in": nrm(ks[2], (DEPTH, D_MODEL, IN_WIDTH), D_MODEL ** -0.5),
        "q_norm_g": 1.0 + nrm(ks[3], (DEPTH, HEAD_DIM), 0.02),
        "k_norm_g": 1.0 + nrm(ks[4], (DEPTH, HEAD_DIM), 0.02),
        "rel_bias": nrm(ks[5], (DEPTH, N_HEADS_A, N_REL), 0.2),
        "w_branch_a": nrm(ks[6], (DEPTH, WIDTH_A, D_MODEL), WIDTH_A ** -0.5),
        "w_branch_b": nrm(ks[7], (DEPTH, WIDTH_B, D_MODEL), WIDTH_B ** -0.5),
        "w_out": nrm(ks[8], (DEPTH, D_MODEL, D_MODEL), D_MODEL ** -0.5),
        "norm2_g": 1.0 + nrm(ks[9], (DEPTH, D_MODEL), 0.02),
        "w_ffn_up": nrm(ks[10], (DEPTH, D_MODEL, 2 * D_FF), D_MODEL ** -0.5),
        "ffn_conv_w": nrm(ks[11], (DEPTH, CONV_WIDTH, 2 * D_FF), CONV_WIDTH ** -0.5),
        "ffn_conv_b": nrm(ks[12], (DEPTH, 2 * D_FF), 0.01),
        "w_ffn_down": nrm(ks[13], (DEPTH, D_FF, D_MODEL), D_FF ** -0.5),
    }


def _fwd_reference(x, norm1_g, w_in, q_norm_g, k_norm_g, rel_bias, w_branch_a, w_branch_b,
              w_out, norm2_g, w_ffn_up, ffn_conv_w, ffn_conv_b, w_ffn_down):
    b, s, _ = x.shape
    o1 = 3 * WIDTH_A
    o2 = o1 + 3 * WIDTH_B
    for l in range(DEPTH):
        hn = rms_norm(x, norm1_g[l])
        proj = hn @ w_in[l]
        qa, ka, va = [proj[..., i * WIDTH_A:(i + 1) * WIDTH_A].reshape(b, s, N_HEADS_A, HEAD_DIM)
                      for i in range(3)]
        qb, kb, vb = [proj[..., o1 + i * WIDTH_B:o1 + (i + 1) * WIDTH_B].reshape(b, s, N_HEADS_B, HEAD_DIM)
                      for i in range(3)]
        gate_a = proj[..., o2:o2 + D_MODEL]
        gate_b = proj[..., o2 + D_MODEL:o2 + 2 * D_MODEL]
        out_a = chunked_attention(qa, ka, va, q_norm_g[l], k_norm_g[l], rel_bias[l])
        out_b = stick_breaking_attention(qb, kb, vb)
        mixed = (jax.nn.sigmoid(gate_a) * (out_a @ w_branch_a[l])
                 + jax.nn.sigmoid(gate_b) * (out_b @ w_branch_b[l]))
        x = x + mixed @ w_out[l]
        x = x + conv_glu(rms_norm(x, norm2_g[l]), w_ffn_up[l], ffn_conv_w[l],
                         ffn_conv_b[l], w_ffn_down[l])
    return x


import jax as _jax
import jax.numpy as _jnp

TWIN_FORMAT = 'train_step'
FWD_PARAMS = ['x', 'norm1_g', 'w_in', 'q_norm_g', 'k_norm_g', 'rel_bias', 'w_branch_a', 'w_branch_b', 'w_out', 'norm2_g', 'w_ffn_up', 'ffn_conv_w', 'ffn_conv_b', 'w_ffn_down']
TWIN_WEIGHTS = ['norm1_g', 'w_in', 'q_norm_g', 'k_norm_g', 'rel_bias', 'w_branch_a', 'w_branch_b', 'w_out', 'norm2_g', 'w_ffn_up', 'ffn_conv_w', 'ffn_conv_b', 'w_ffn_down']
TWIN_DIFF_INPUT = 'x'
TWIN_INPUTS = ['x', 'norm1_g', 'w_in', 'q_norm_g', 'k_norm_g', 'rel_bias', 'w_branch_a', 'w_branch_b', 'w_out', 'norm2_g', 'w_ffn_up', 'ffn_conv_w', 'ffn_conv_b', 'w_ffn_down', 'loss_target', 'm_norm1_g', 'm_w_in', 'm_q_norm_g', 'm_k_norm_g', 'm_rel_bias', 'm_w_branch_a', 'm_w_branch_b', 'm_w_out', 'm_norm2_g', 'm_w_ffn_up', 'm_ffn_conv_w', 'm_ffn_conv_b', 'm_w_ffn_down', 'v_norm1_g', 'v_w_in', 'v_q_norm_g', 'v_k_norm_g', 'v_rel_bias', 'v_w_branch_a', 'v_w_branch_b', 'v_w_out', 'v_norm2_g', 'v_w_ffn_up', 'v_ffn_conv_w', 'v_ffn_conv_b', 'v_w_ffn_down']
TWIN_OUTPUTS = ['loss', 'grad_x', 'grad_norm1_g', 'grad_w_in', 'grad_q_norm_g', 'grad_k_norm_g', 'grad_rel_bias', 'grad_w_branch_a', 'grad_w_branch_b', 'grad_w_out', 'grad_norm2_g', 'grad_w_ffn_up', 'grad_ffn_conv_w', 'grad_ffn_conv_b', 'grad_w_ffn_down', 'delta_norm1_g', 'delta_w_in', 'delta_q_norm_g', 'delta_k_norm_g', 'delta_rel_bias', 'delta_w_branch_a', 'delta_w_branch_b', 'delta_w_out', 'delta_norm2_g', 'delta_w_ffn_up', 'delta_ffn_conv_w', 'delta_ffn_conv_b', 'delta_w_ffn_down', 'new_m_norm1_g', 'new_m_w_in', 'new_m_q_norm_g', 'new_m_k_norm_g', 'new_m_rel_bias', 'new_m_w_branch_a', 'new_m_w_branch_b', 'new_m_w_out', 'new_m_norm2_g', 'new_m_w_ffn_up', 'new_m_ffn_conv_w', 'new_m_ffn_conv_b', 'new_m_w_ffn_down', 'new_v_norm1_g', 'new_v_w_in', 'new_v_q_norm_g', 'new_v_k_norm_g', 'new_v_rel_bias', 'new_v_w_branch_a', 'new_v_w_branch_b', 'new_v_w_out', 'new_v_norm2_g', 'new_v_w_ffn_up', 'new_v_ffn_conv_w', 'new_v_ffn_conv_b', 'new_v_w_ffn_down']
TWIN_LEAF_KINDS = {'loss': 'loss', 'grad_x': 'grad_x', 'grad_norm1_g': 'grad_w', 'grad_w_in': 'grad_w', 'grad_q_norm_g': 'grad_w', 'grad_k_norm_g': 'grad_w', 'grad_rel_bias': 'grad_w', 'grad_w_branch_a': 'grad_w', 'grad_w_branch_b': 'grad_w', 'grad_w_out': 'grad_w', 'grad_norm2_g': 'grad_w', 'grad_w_ffn_up': 'grad_w', 'grad_ffn_conv_w': 'grad_w', 'grad_ffn_conv_b': 'grad_w', 'grad_w_ffn_down': 'grad_w', 'delta_norm1_g': 'delta_w', 'delta_w_in': 'delta_w', 'delta_q_norm_g': 'delta_w', 'delta_k_norm_g': 'delta_w', 'delta_rel_bias': 'delta_w', 'delta_w_branch_a': 'delta_w', 'delta_w_branch_b': 'delta_w', 'delta_w_out': 'delta_w', 'delta_norm2_g': 'delta_w', 'delta_w_ffn_up': 'delta_w', 'delta_ffn_conv_w': 'delta_w', 'delta_ffn_conv_b': 'delta_w', 'delta_w_ffn_down': 'delta_w', 'new_m_norm1_g': 'new_m', 'new_m_w_in': 'new_m', 'new_m_q_norm_g': 'new_m', 'new_m_k_norm_g': 'new_m', 'new_m_rel_bias': 'new_m', 'new_m_w_branch_a': 'new_m', 'new_m_w_branch_b': 'new_m', 'new_m_w_out': 'new_m', 'new_m_norm2_g': 'new_m', 'new_m_w_ffn_up': 'new_m', 'new_m_ffn_conv_w': 'new_m', 'new_m_ffn_conv_b': 'new_m', 'new_m_w_ffn_down': 'new_m', 'new_v_norm1_g': 'new_v', 'new_v_w_in': 'new_v', 'new_v_q_norm_g': 'new_v', 'new_v_k_norm_g': 'new_v', 'new_v_rel_bias': 'new_v', 'new_v_w_branch_a': 'new_v', 'new_v_w_branch_b': 'new_v', 'new_v_w_out': 'new_v', 'new_v_norm2_g': 'new_v', 'new_v_w_ffn_up': 'new_v', 'new_v_ffn_conv_w': 'new_v', 'new_v_ffn_conv_b': 'new_v', 'new_v_w_ffn_down': 'new_v'}


def _forward(args):
    return _fwd_reference(*[args[k] for k in FWD_PARAMS])


def _output_shape():
    def fwd():
        inp = _fwd_setup_inputs(0)
        return _fwd_reference(*[inp[k] for k in FWD_PARAMS])
    out = _jax.eval_shape(fwd)
    return out.shape, out.dtype

N_MICROBATCH = 1
ADAM_LR = 0.001
ADAM_B1 = 0.9
ADAM_B2 = 0.999
ADAM_EPS = 1e-08
ADAM_WD = 0.01
ADAM_STEP = 10
PER_EXAMPLE_BATCH_AXIS = {'x': 0, 'loss_target': 0}
SHARED_INPUTS = []
_WEIGHT_DTYPES = {'norm1_g': _jnp.float32, 'w_in': _jnp.float32, 'q_norm_g': _jnp.float32, 'k_norm_g': _jnp.float32, 'rel_bias': _jnp.float32, 'w_branch_a': _jnp.float32, 'w_branch_b': _jnp.float32, 'w_out': _jnp.float32, 'norm2_g': _jnp.float32, 'w_ffn_up': _jnp.float32, 'ffn_conv_w': _jnp.float32, 'ffn_conv_b': _jnp.float32, 'w_ffn_down': _jnp.float32}
MOMENT_SCALE = {'norm1_g': 2.105040e+01, 'w_in': 2.657718e-01, 'q_norm_g': 1.334186e+00, 'k_norm_g': 1.338956e+00, 'rel_bias': 5.198041e-02, 'w_branch_a': 1.045176e-01, 'w_branch_b': 6.478543e-01, 'w_out': 5.609727e-01, 'norm2_g': 1.081539e+02, 'w_ffn_up': 6.456323e-01, 'ffn_conv_w': 1.481697e+01, 'ffn_conv_b': 1.313727e+01, 'w_ffn_down': 1.074774e+00}


def _to_microbatches(a, axis):
    t = _jnp.moveaxis(a, axis, 0)
    t = t.reshape((N_MICROBATCH, t.shape[0] // N_MICROBATCH) + t.shape[1:])
    return _jnp.moveaxis(t, 1, axis + 1)


def setup_inputs(seed: int = 0) -> dict:
    inp = _fwd_setup_inputs(seed)
    key = _jax.random.fold_in(_jax.random.key(seed), 7919)
    shape, _ = _output_shape()
    out = dict(inp)
    out["loss_target"] = _jax.random.normal(_jax.random.fold_in(key, 0), shape, _jnp.float32)
    for i, name in enumerate(TWIN_WEIGHTS):
        w = inp[name].astype(_jnp.float32)
        if MOMENT_SCALE is None:
            s = _jnp.sqrt(_jnp.mean(_jnp.square(w)) + 1e-30)
        else:
            s = MOMENT_SCALE[name]
        km, kv = _jax.random.split(_jax.random.fold_in(key, i + 1))
        out[name] = w
        out["m_" + name] = s * _jax.random.normal(km, w.shape, _jnp.float32)
        out["v_" + name] = (s * s) * _jax.random.uniform(kv, w.shape, _jnp.float32, 0.5, 1.5)
    if N_MICROBATCH > 1:
        for name, axis in PER_EXAMPLE_BATCH_AXIS.items():
            out[name] = _to_microbatches(out[name], axis)
    return {'x': out['x'], 'norm1_g': out['norm1_g'], 'w_in': out['w_in'], 'q_norm_g': out['q_norm_g'], 'k_norm_g': out['k_norm_g'], 'rel_bias': out['rel_bias'], 'w_branch_a': out['w_branch_a'], 'w_branch_b': out['w_branch_b'], 'w_out': out['w_out'], 'norm2_g': out['norm2_g'], 'w_ffn_up': out['w_ffn_up'], 'ffn_conv_w': out['ffn_conv_w'], 'ffn_conv_b': out['ffn_conv_b'], 'w_ffn_down': out['w_ffn_down'], 'loss_target': out['loss_target'], 'm_norm1_g': out['m_norm1_g'], 'm_w_in': out['m_w_in'], 'm_q_norm_g': out['m_q_norm_g'], 'm_k_norm_g': out['m_k_norm_g'], 'm_rel_bias': out['m_rel_bias'], 'm_w_branch_a': out['m_w_branch_a'], 'm_w_branch_b': out['m_w_branch_b'], 'm_w_out': out['m_w_out'], 'm_norm2_g': out['m_norm2_g'], 'm_w_ffn_up': out['m_w_ffn_up'], 'm_ffn_conv_w': out['m_ffn_conv_w'], 'm_ffn_conv_b': out['m_ffn_conv_b'], 'm_w_ffn_down': out['m_w_ffn_down'], 'v_norm1_g': out['v_norm1_g'], 'v_w_in': out['v_w_in'], 'v_q_norm_g': out['v_q_norm_g'], 'v_k_norm_g': out['v_k_norm_g'], 'v_rel_bias': out['v_rel_bias'], 'v_w_branch_a': out['v_w_branch_a'], 'v_w_branch_b': out['v_w_branch_b'], 'v_w_out': out['v_w_out'], 'v_norm2_g': out['v_norm2_g'], 'v_w_ffn_up': out['v_w_ffn_up'], 'v_ffn_conv_w': out['v_ffn_conv_w'], 'v_ffn_conv_b': out['v_ffn_conv_b'], 'v_w_ffn_down': out['v_w_ffn_down']}


def _loss(weights, diff, rest, loss_target):
    with _jax.named_scope("forward"):
        args = {**rest, TWIN_DIFF_INPUT: diff, **{k: w.astype(_WEIGHT_DTYPES[k]) for k, w in weights.items()}}
        y = _forward(args)
    with _jax.named_scope("loss_head"):
        err = _jnp.square(y.astype(_jnp.float32) - loss_target)
        return 0.5 * _jnp.sum(_jnp.mean(err, axis=-1)) if err.ndim else 0.5 * err


def _adamw(w, g, m, v):
    m = ADAM_B1 * m + (1.0 - ADAM_B1) * g
    v = ADAM_B2 * v + (1.0 - ADAM_B2) * _jnp.square(g)
    m_hat = m / (1.0 - ADAM_B1 ** ADAM_STEP)
    v_hat = v / (1.0 - ADAM_B2 ** ADAM_STEP)
    delta = -ADAM_LR * (m_hat / (_jnp.sqrt(v_hat) + ADAM_EPS) + ADAM_WD * w)
    return delta, m, v


def reference(x, norm1_g, w_in, q_norm_g, k_norm_g, rel_bias, w_branch_a, w_branch_b, w_out, norm2_g, w_ffn_up, ffn_conv_w, ffn_conv_b, w_ffn_down, loss_target, m_norm1_g, m_w_in, m_q_norm_g, m_k_norm_g, m_rel_bias, m_w_branch_a, m_w_branch_b, m_w_out, m_norm2_g, m_w_ffn_up, m_ffn_conv_w, m_ffn_conv_b, m_w_ffn_down, v_norm1_g, v_w_in, v_q_norm_g, v_k_norm_g, v_rel_bias, v_w_branch_a, v_w_branch_b, v_w_out, v_norm2_g, v_w_ffn_up, v_ffn_conv_w, v_ffn_conv_b, v_w_ffn_down):
    given = dict(x=x, norm1_g=norm1_g, w_in=w_in, q_norm_g=q_norm_g, k_norm_g=k_norm_g, rel_bias=rel_bias, w_branch_a=w_branch_a, w_branch_b=w_branch_b, w_out=w_out, norm2_g=norm2_g, w_ffn_up=w_ffn_up, ffn_conv_w=ffn_conv_w, ffn_conv_b=ffn_conv_b, w_ffn_down=w_ffn_down, loss_target=loss_target, m_norm1_g=m_norm1_g, m_w_in=m_w_in, m_q_norm_g=m_q_norm_g, m_k_norm_g=m_k_norm_g, m_rel_bias=m_rel_bias, m_w_branch_a=m_w_branch_a, m_w_branch_b=m_w_branch_b, m_w_out=m_w_out, m_norm2_g=m_norm2_g, m_w_ffn_up=m_w_ffn_up, m_ffn_conv_w=m_ffn_conv_w, m_ffn_conv_b=m_ffn_conv_b, m_w_ffn_down=m_w_ffn_down, v_norm1_g=v_norm1_g, v_w_in=v_w_in, v_q_norm_g=v_q_norm_g, v_k_norm_g=v_k_norm_g, v_rel_bias=v_rel_bias, v_w_branch_a=v_w_branch_a, v_w_branch_b=v_w_branch_b, v_w_out=v_w_out, v_norm2_g=v_norm2_g, v_w_ffn_up=v_w_ffn_up, v_ffn_conv_w=v_ffn_conv_w, v_ffn_conv_b=v_ffn_conv_b, v_w_ffn_down=v_w_ffn_down)
    weights = {n: given[n] for n in TWIN_WEIGHTS}
    shared = {n: given[n] for n in SHARED_INPUTS}
    per_example = {n: given[n] for n in ['x']}
    grad_fn = _jax.value_and_grad(_loss, argnums=(0, 1))

    def one_microbatch(ex, loss_target):
        ex = dict(ex)
        diff = ex.pop(TWIN_DIFF_INPUT)
        return grad_fn(weights, diff, {**shared, **ex}, loss_target)

    if N_MICROBATCH == 1:
        loss, (grad_w, grad_x) = one_microbatch(per_example, given["loss_target"])
    else:
        def body(carry, xs):
            loss_sum, grad_sum = carry
            l_k, (gw_k, gx_k) = one_microbatch(xs[0], xs[1])
            with _jax.named_scope("update"):
                return (loss_sum + l_k, _jax.tree.map(_jnp.add, grad_sum, gw_k)), gx_k

        init = (_jnp.zeros((), _jnp.float32), _jax.tree.map(_jnp.zeros_like, weights))
        (loss, grad_w), grad_x = _jax.lax.scan(body, init, (per_example, given["loss_target"]))
    with _jax.named_scope("update"):
        delta_w, new_m, new_v = {}, {}, {}
        for n in TWIN_WEIGHTS:
            delta_w[n], new_m[n], new_v[n] = _adamw(weights[n], grad_w[n], given["m_" + n], given["v_" + n])
    return (loss, grad_x, *[grad_w[n] for n in TWIN_WEIGHTS], *[delta_w[n] for n in TWIN_WEIGHTS],
            *[new_m[n] for n in TWIN_WEIGHTS], *[new_v[n] for n in TWIN_WEIGHTS])
```

```python
import functools

import jax
import jax.numpy as jnp
from jax import lax
from jax.experimental import pallas as pl
from jax.experimental.pallas import tpu as pltpu

F32 = jnp.float32
BF16 = jnp.bfloat16
MESH = pl.DeviceIdType.MESH

D_MODEL = 1024
HEAD_DIM = 64
N_HEADS = 8
WIDTH = N_HEADS * HEAD_DIM
CHUNK = 64
LEFT_CHUNKS = 8
MAX_REL = 128
N_REL = 2 * MAX_REL + 1
D_FF = 2816
EPS = 1e-6
NEG = -1e30

ADAM_LR = 0.001
ADAM_B1 = 0.9
ADAM_B2 = 0.999
ADAM_EPS = 1e-08
ADAM_WD = 0.01
ADAM_STEP = 10

N_CHIPS = 4
N_DEV = 8
LANES = 128
PAIR = 2 * HEAD_DIM
BQ = 256
BAND = LEFT_CHUNKS * CHUNK
KWIN = BAND + BQ
VMEM_LIMIT = 56 * 1024 * 1024
PACK_COLS = 1024
SMALL_ROWS = 16

NN = (((1,), (0,)), ((), ()))
NT = (((1,), (1,)), ((), ()))
TN = (((0,), (0,)), ((), ()))


def _cparams(sem=None):
    if sem is None:
        return pltpu.CompilerParams(vmem_limit_bytes=VMEM_LIMIT)
    return pltpu.CompilerParams(dimension_semantics=sem, vmem_limit_bytes=VMEM_LIMIT)


def _pick(n, cands):
    for c in cands:
        if n % c == 0:
            return c
    raise ValueError(f"no block for {n}")


def _dot(a, b, dn):
    return lax.dot_general(a, b, dn, preferred_element_type=F32)


def _sigmoid(x):
    return 1.0 / (1.0 + jnp.exp(-x))


def _split_bf16(x):
    hi = x.astype(BF16)
    lo = (x - hi.astype(F32)).astype(BF16)
    return hi, lo


def _matmul(a, b, mode, out_dtype, name, residual=None):
    if mode == "nn":
        (M, K), N = a.shape, b.shape[1]
    elif mode == "nt":
        (M, K), N = a.shape, b.shape[0]
    else:
        (K, M), N = a.shape, b.shape[1]
    if mode == "tn":
        bm = _pick(M, (512, 256, 128))
        bn = _pick(N, (512, 256, 128))
        bk = _pick(K, (1024, 512))
    else:
        bm = _pick(M, (1024, 512)) if K <= 1024 else 512
        bn = _pick(N, (512, 256, 128))
        bk = K if K <= D_FF else _pick(K, (1024, 512))
    nk = K // bk
    dn = {"nn": NN, "nt": NT, "tn": TN}[mode]
    a_spec = {"nn": pl.BlockSpec((bm, bk), lambda i, j, k: (i, k)),
              "nt": pl.BlockSpec((bm, bk), lambda i, j, k: (i, k)),
              "tn": pl.BlockSpec((bk, bm), lambda i, j, k: (k, i))}[mode]
    b_spec = {"nn": pl.BlockSpec((bk, bn), lambda i, j, k: (k, j)),
              "nt": pl.BlockSpec((bn, bk), lambda i, j, k: (j, k)),
              "tn": pl.BlockSpec((bk, bn), lambda i, j, k: (k, j))}[mode]
    o_spec = pl.BlockSpec((bm, bn), lambda i, j, k: (i, j))
    has_res = residual is not None

    def body(*refs):
        if has_res:
            a_ref, b_ref, r_ref, o_ref, acc_ref = refs
        else:
            a_ref, b_ref, o_ref, acc_ref = refs
        k = pl.program_id(2)
        part = _dot(a_ref[...], b_ref[...], dn)

        def finish(total):
            if has_res:
                total = total + r_ref[...]
            o_ref[...] = total.astype(out_dtype)

        if nk == 1:
            finish(part)
        else:
            @pl.when(k == 0)
            def _():
                acc_ref[...] = part

            @pl.when(k > 0)
            def _():
                acc_ref[...] += part

            @pl.when(k == nk - 1)
            def _():
                finish(acc_ref[...])

    in_specs = [a_spec, b_spec] + ([o_spec] if has_res else [])
    args = (a, b) + ((residual,) if has_res else ())
    return pl.pallas_call(
        body, name=name,
        grid=(M // bm, N // bn, nk),
        in_specs=in_specs, out_specs=o_spec,
        out_shape=jax.ShapeDtypeStruct((M, N), out_dtype),
        scratch_shapes=[pltpu.VMEM((bm, bn) if nk > 1 else (8, LANES), F32)],
        compiler_params=_cparams(("parallel", "parallel", "arbitrary")),
    )(*args)


ROWS = 512


def _row_spec(cols, bm=ROWS):
    return pl.BlockSpec((bm, cols), lambda i: (i, 0))


def _full_spec(shape):
    return pl.BlockSpec(shape, lambda i: (0,) * len(shape))


def _colsum8(t):
    return jnp.sum(t.reshape(t.shape[0] // 8, 8, t.shape[1]), axis=0)


def _rms_fwd(x, g, name):
    S, D = x.shape

    def body(x_ref, g_ref, o_ref):
        xv = x_ref[...]
        r = lax.rsqrt(jnp.mean(xv * xv, axis=-1, keepdims=True) + EPS)
        o_ref[...] = (xv * r * g_ref[...]).astype(BF16)

    return pl.pallas_call(
        body, name=name, grid=(S // ROWS,),
        in_specs=[_row_spec(D), _full_spec((1, D))], out_specs=_row_spec(D),
        out_shape=jax.ShapeDtypeStruct((S, D), BF16),
        compiler_params=_cparams(("parallel",)),
    )(x, g)


def _rms_bwd(x, g, dy, dres, name):
    S, D = x.shape
    nt = S // ROWS

    def body(x_ref, g_ref, dy_ref, dres_ref, dx_ref, dxb_ref, dg_ref, acc_ref):
        i = pl.program_id(0)
        xv, dyv = x_ref[...], dy_ref[...]
        r = lax.rsqrt(jnp.mean(xv * xv, axis=-1, keepdims=True) + EPS)
        xr = xv * r
        u = dyv * g_ref[...]
        dx = r * u - xr * (r * r) * jnp.mean(xv * u, axis=-1, keepdims=True) + dres_ref[...]
        dx_ref[...] = dx
        dxb_ref[...] = dx.astype(BF16)
        part = _colsum8(dyv * xr)

        @pl.when(i == 0)
        def _():
            acc_ref[...] = part

        @pl.when(i > 0)
        def _():
            acc_ref[...] += part

        @pl.when(i == nt - 1)
        def _():
            dg_ref[...] = jnp.sum(acc_ref[...], axis=0, keepdims=True)

    return pl.pallas_call(
        body, name=name, grid=(nt,),
        in_specs=[_row_spec(D), _full_spec((1, D)), _row_spec(D), _row_spec(D)],
        out_specs=[_row_spec(D), _row_spec(D), _full_spec((1, D))],
        out_shape=[jax.ShapeDtypeStruct((S, D), F32), jax.ShapeDtypeStruct((S, D), BF16),
                   jax.ShapeDtypeStruct((1, D), F32)],
        scratch_shapes=[pltpu.VMEM((8, D), F32)],
        compiler_params=_cparams(("arbitrary",)),
    )(x, g, dy, dres)


def _head_mean(t, blockdiag):
    hi, lo = _split_bf16(t)
    return (_dot(hi, blockdiag, NN) + _dot(lo, blockdiag, NN)) * (1.0 / HEAD_DIM)


def _blockdiag():
    r = lax.broadcasted_iota(jnp.int32, (WIDTH, WIDTH), 0) // HEAD_DIM
    c = lax.broadcasted_iota(jnp.int32, (WIDTH, WIDTH), 1) // HEAD_DIM
    return jnp.where(r == c, 1.0, 0.0).astype(BF16)


def _qknorm_fwd(qk, gq, gk, name):
    S = qk.shape[0]

    def body(qk_ref, gq_ref, gk_ref, q_ref, k_ref):
        bd = _blockdiag()
        for part, g_ref, o_ref, scale in ((0, gq_ref, q_ref, HEAD_DIM ** -0.5), (1, gk_ref, k_ref, 1.0)):
            t = qk_ref[:, part * WIDTH:(part + 1) * WIDTH]
            r = lax.rsqrt(_head_mean(t * t, bd) + EPS)
            o_ref[...] = (t * r * g_ref[...] * scale).astype(BF16)

    return pl.pallas_call(
        body, name=name, grid=(S // ROWS,),
        in_specs=[_row_spec(2 * WIDTH), _full_spec((1, WIDTH)), _full_spec((1, WIDTH))],
        out_specs=[_row_spec(WIDTH), _row_spec(WIDTH)],
        out_shape=[jax.ShapeDtypeStruct((S, WIDTH), BF16)] * 2,
        compiler_params=_cparams(("parallel",)),
    )(qk, gq, gk)


def _qknorm_bwd(qk, gq, gk, dqn, dkn, name):
    S = qk.shape[0]
    nt = S // ROWS

    def body(qk_ref, gq_ref, gk_ref, dqn_ref, dkn_ref, dq_ref, dk_ref, dgq_ref, dgk_ref, accq_ref, acck_ref):
        i = pl.program_id(0)
        bd = _blockdiag()
        for part, g_ref, dn_ref, o_ref, dg_ref, acc_ref, scale in (
                (0, gq_ref, dqn_ref, dq_ref, dgq_ref, accq_ref, HEAD_DIM ** -0.5),
                (1, gk_ref, dkn_ref, dk_ref, dgk_ref, acck_ref, 1.0)):
            t = qk_ref[:, part * WIDTH:(part + 1) * WIDTH]
            dn = dn_ref[...] * scale
            r = lax.rsqrt(_head_mean(t * t, bd) + EPS)
            u = dn * g_ref[...]
            dt = r * u - t * (r * r * r) * _head_mean(t * u, bd)
            o_ref[...] = dt.astype(BF16)
            psum = _colsum8(dn * t * r)

            @pl.when(i == 0)
            def _():
                acc_ref[...] = psum

            @pl.when(i > 0)
            def _():
                acc_ref[...] += psum

            @pl.when(i == nt - 1)
            def _():
                dg_ref[...] = jnp.sum(acc_ref[...], axis=0, keepdims=True)

    return pl.pallas_call(
        body, name=name, grid=(nt,),
        in_specs=[_row_spec(2 * WIDTH), _full_spec((1, WIDTH)), _full_spec((1, WIDTH)),
                  _row_spec(WIDTH), _row_spec(WIDTH)],
        out_specs=[_row_spec(WIDTH), _row_spec(WIDTH), _full_spec((1, WIDTH)), _full_spec((1, WIDTH))],
        out_shape=[jax.ShapeDtypeStruct((S, WIDTH), BF16)] * 2 + [jax.ShapeDtypeStruct((1, WIDTH), F32)] * 2,
        scratch_shapes=[pltpu.VMEM((8, WIDTH), F32)] * 2,
        compiler_params=_cparams(("arbitrary",)),
    )(qk, gq, gk, dqn, dkn)


def _mix_fwd(ga, gb, ya, yb, name):
    S, D = ga.shape

    def body(ga_ref, gb_ref, ya_ref, yb_ref, o_ref):
        o_ref[...] = (_sigmoid(ga_ref[...]) * ya_ref[...] + _sigmoid(gb_ref[...]) * yb_ref[...]).astype(BF16)

    return pl.pallas_call(
        body, name=name, grid=(S // ROWS,),
        in_specs=[_row_spec(D)] * 4, out_specs=_row_spec(D),
        out_shape=jax.ShapeDtypeStruct((S, D), BF16),
        compiler_params=_cparams(("parallel",)),
    )(ga, gb, ya, yb)


def _mix_bwd(dm, ga, gb, ya, yb, name):
    S, D = ga.shape

    def body(dm_ref, ga_ref, gb_ref, ya_ref, yb_ref, dga_ref, dgb_ref, dya_ref, dyb_ref):
        dmv = dm_ref[...]
        for g_ref, y_ref, dg_ref, dy_ref in ((ga_ref, ya_ref, dga_ref, dya_ref), (gb_ref, yb_ref, dgb_ref, dyb_ref)):
            s = _sigmoid(g_ref[...])
            dy_ref[...] = (dmv * s).astype(BF16)
            dg_ref[...] = (dmv * y_ref[...] * s * (1.0 - s)).astype(BF16)

    return pl.pallas_call(
        body, name=name, grid=(S // ROWS,),
        in_specs=[_row_spec(D)] * 5, out_specs=[_row_spec(D)] * 4,
        out_shape=[jax.ShapeDtypeStruct((S, D), BF16)] * 4,
        compiler_params=_cparams(("parallel",)),
    )(dm, ga, gb, ya, yb)


def _loss_head(y, target, name):
    S, D = y.shape
    nt = S // ROWS

    def body(y_ref, t_ref, dy_ref, dyb_ref, p_ref):
        err = y_ref[...] - t_ref[...]
        dy = err * (1.0 / D)
        dy_ref[...] = dy
        dyb_ref[...] = dy.astype(BF16)
        sq = _colsum8(err * err)
        acc = sq[:, 0:LANES]
        for k in range(1, D // LANES):
            acc = acc + sq[:, k * LANES:(k + 1) * LANES]
        p_ref[...] = acc

    return pl.pallas_call(
        body, name=name, grid=(nt,),
        in_specs=[_row_spec(D)] * 2,
        out_specs=[_row_spec(D), _row_spec(D), pl.BlockSpec((8, LANES), lambda i: (i, 0))],
        out_shape=[jax.ShapeDtypeStruct((S, D), F32), jax.ShapeDtypeStruct((S, D), BF16),
                   jax.ShapeDtypeStruct((nt * 8, LANES), F32)],
        compiler_params=_cparams(("parallel",)),
    )(y, target)


CONV_COLS = 256
HALO = 16


def _conv_taps(xe, cw, cb):
    return cw[0:1] * pltpu.roll(xe, 2, 0) + cw[1:2] * pltpu.roll(xe, 1, 0) + cw[2:3] * xe + cb


def _convglu_fwd(hg, hu, cwg, cwu, cbg, cbu, name):
    S = hg.shape[0]
    hb = ROWS // HALO
    main = pl.BlockSpec((ROWS, CONV_COLS), lambda c, i: (i, c))
    prev = pl.BlockSpec((HALO, CONV_COLS), lambda c, i: (jnp.maximum(i * hb - 1, 0), c))
    wspec = pl.BlockSpec((3, CONV_COLS), lambda c, i: (0, c))
    bspec = pl.BlockSpec((1, CONV_COLS), lambda c, i: (0, c))

    def body(hg_ref, hgp_ref, hu_ref, hup_ref, cwg_ref, cwu_ref, cbg_ref, cbu_ref, o_ref):
        i = pl.program_id(1)
        keep = (i > 0).astype(F32)

        def conv(h_ref, hp_ref, cw_ref, cb_ref):
            xe = jnp.concatenate([hp_ref[...].astype(F32) * keep, h_ref[...].astype(F32)], axis=0)
            return _conv_taps(xe, cw_ref[...], cb_ref[...])[HALO:, :]

        gate = conv(hg_ref, hgp_ref, cwg_ref, cbg_ref)
        up = conv(hu_ref, hup_ref, cwu_ref, cbu_ref)
        o_ref[...] = (gate * _sigmoid(gate) * up).astype(BF16)

    return pl.pallas_call(
        body, name=name, grid=(D_FF // CONV_COLS, S // ROWS),
        in_specs=[main, prev, main, prev, wspec, wspec, bspec, bspec], out_specs=main,
        out_shape=jax.ShapeDtypeStruct((S, D_FF), BF16),
        compiler_params=_cparams(("parallel", "parallel")),
    )(hg, hg, hu, hu, cwg, cwu, cbg, cbu)


def _convglu_bwd(hg, hu, dact, cwg, cwu, cbg, cbu, name):
    S = hg.shape[0]
    nt = S // ROWS
    hb = ROWS // HALO
    main = pl.BlockSpec((ROWS, CONV_COLS), lambda c, i: (i, c))
    prev = pl.BlockSpec((HALO, CONV_COLS), lambda c, i: (jnp.maximum(i * hb - 1, 0), c))
    nxt = pl.BlockSpec((HALO, CONV_COLS), lambda c, i: (jnp.minimum((i + 1) * hb, nt * hb - 1), c))
    wspec = pl.BlockSpec((3, CONV_COLS), lambda c, i: (0, c))
    bspec = pl.BlockSpec((1, CONV_COLS), lambda c, i: (0, c))

    def body(hg_ref, hgp_ref, hgn_ref, hu_ref, hup_ref, hun_ref, da_ref, dan_ref,
             cwg_ref, cwu_ref, cbg_ref, cbu_ref,
             dhg_ref, dhu_ref, dcwg_ref, dcwu_ref, dcbg_ref, dcbu_ref):
        i = pl.program_id(1)
        kp = (i > 0).astype(F32)
        kn = (i < nt - 1).astype(F32)

        def ext(h_ref, hp_ref, hn_ref):
            return jnp.concatenate([hp_ref[...].astype(F32) * kp, h_ref[...].astype(F32),
                                    hn_ref[...].astype(F32) * kn], axis=0)

        xg = ext(hg_ref, hgp_ref, hgn_ref)
        xu = ext(hu_ref, hup_ref, hun_ref)
        gate = _conv_taps(xg, cwg_ref[...], cbg_ref[...])[HALO:, :]
        up = _conv_taps(xu, cwu_ref[...], cbu_ref[...])[HALO:, :]
        da = jnp.concatenate([da_ref[...].astype(F32), dan_ref[...].astype(F32) * kn], axis=0)
        sg = _sigmoid(gate)
        dgate = da * up * sg * (1.0 + gate * (1.0 - sg))
        dup = da * gate * sg
        n_ext = ROWS + HALO

        @pl.when(i == 0)
        def _():
            for ref in (dcwg_ref, dcwu_ref, dcbg_ref, dcbu_ref):
                ref[...] = jnp.zeros_like(ref)

        for d, cw_ref, xe, dh_ref, dcw_ref, dcb_ref in ((dgate, cwg_ref, xg, dhg_ref, dcwg_ref, dcbg_ref),
                                                       (dup, cwu_ref, xu, dhu_ref, dcwu_ref, dcbu_ref)):
            cw = cw_ref[...]
            dh = cw[2:3] * d + cw[1:2] * pltpu.roll(d, n_ext - 1, 0) + cw[0:1] * pltpu.roll(d, n_ext - 2, 0)
            dh_ref[...] = dh[:ROWS, :].astype(BF16)
            dc = d[:ROWS, :]
            for t in range(3):
                shifted = xe if t == 2 else pltpu.roll(xe, 2 - t, 0)
                dcw_ref[t:t + 1, :] += jnp.sum(dc * shifted[HALO:HALO + ROWS, :], axis=0, keepdims=True)
            dcb_ref[...] += jnp.sum(dc, axis=0, keepdims=True)

    return pl.pallas_call(
        body, name=name, grid=(D_FF // CONV_COLS, nt),
        in_specs=[main, prev, nxt, main, prev, nxt, main, nxt, wspec, wspec, bspec, bspec],
        out_specs=[main, main, wspec, wspec, bspec, bspec],
        out_shape=[jax.ShapeDtypeStruct((S, D_FF), BF16)] * 2 + [jax.ShapeDtypeStruct((3, D_FF), F32)] * 2
        + [jax.ShapeDtypeStruct((1, D_FF), F32)] * 2,
        compiler_params=_cparams(("parallel", "arbitrary")),
    )(hg, hg, hg, hu, hu, hu, dact, dact, cwg, cwu, cbg, cbu)


def _bias_index():
    qi = lax.broadcasted_iota(jnp.int32, (BQ, KWIN), 0)
    kj = lax.broadcasted_iota(jnp.int32, (BQ, KWIN), 1)
    idx = jnp.clip(qi - kj + BAND, -MAX_REL, MAX_REL) + MAX_REL
    qc = qi // CHUNK
    kc = kj // CHUNK - LEFT_CHUNKS
    valid = (kc <= qc) & (kc >= qc - LEFT_CHUNKS)
    return idx, valid


IDX_LO = MAX_REL - (CHUNK - 1)


def _bias_table(rel_bias, name):
    def body(rb_ref, o_ref):
        h = pl.program_id(0)
        idx, valid = _bias_index()

        def step(t, acc):
            return jnp.where(idx == t, rb_ref[h, t], acc)

        acc = lax.fori_loop(IDX_LO, N_REL, step, jnp.zeros((BQ, KWIN), F32))
        o_ref[0] = jnp.where(valid, acc, NEG)

    return pl.pallas_call(
        body, name=name, grid=(N_HEADS,),
        in_specs=[pl.BlockSpec(memory_space=pltpu.SMEM)],
        out_specs=pl.BlockSpec((1, BQ, KWIN), lambda h: (h, 0, 0)),
        out_shape=jax.ShapeDtypeStruct((N_HEADS, BQ, KWIN), F32),
        compiler_params=_cparams(("parallel",)),
    )(rel_bias)


REL_PAD = 384


def _bias_table_bwd(dtab, name):
    def body(d_ref, o_ref):
        idx, _ = _bias_index()
        dv = d_ref[0]
        lane = lax.broadcasted_iota(jnp.int32, (1, REL_PAD), 1)

        def step(t, acc):
            s = jnp.sum(jnp.where(idx == t, dv, 0.0))
            return jnp.where(lane == t, s, acc)

        o_ref[0] = lax.fori_loop(IDX_LO, N_REL, step, jnp.zeros((1, REL_PAD), F32))

    return pl.pallas_call(
        body, name=name, grid=(N_HEADS,),
        in_specs=[pl.BlockSpec((1, BQ, KWIN), lambda h: (h, 0, 0))],
        out_specs=pl.BlockSpec((1, 1, REL_PAD), lambda h: (h, 0, 0)),
        out_shape=jax.ShapeDtypeStruct((N_HEADS, 1, REL_PAD), F32),
        compiler_params=_cparams(("parallel",)),
    )(dtab)


def _head_masks():
    lane = lax.broadcasted_iota(jnp.int32, (1, PAIR), 1)
    return [lane // HEAD_DIM == h for h in range(2)]


def _ca_window_specs(nq):
    return [pl.BlockSpec((BQ, PAIR), functools.partial(
        lambda p, i, d: (jnp.clip(i - 2 + d, 0, nq - 1), p), d=d)) for d in range(3)]


def _ca_scores(qm, kc, tab_h, i):
    col = lax.broadcasted_iota(jnp.int32, (1, KWIN), 1)
    in_seq = col + (i - 2) * BQ >= 0
    return jnp.where(in_seq, _dot(qm, kc, NT) + tab_h, NEG)


def _ca_fwd(qn, kn, v, tab, name):
    S = qn.shape[0]
    nq = S // BQ
    qspec = pl.BlockSpec((BQ, PAIR), lambda p, i: (i, p))
    tspec = pl.BlockSpec((2, BQ, KWIN), lambda p, i: (p, 0, 0))

    def body(q_ref, k0, k1, k2, v0, v1, v2, tab_ref, o_ref):
        i = pl.program_id(1)
        kc = jnp.concatenate([k0[...], k1[...], k2[...]], axis=0)
        vc = jnp.concatenate([v0[...], v1[...], v2[...]], axis=0)
        qv = q_ref[...]
        acc = jnp.zeros((BQ, PAIR), F32)
        for h, m in enumerate(_head_masks()):
            s = _ca_scores(jnp.where(m, qv, 0), kc, tab_ref[h], i)
            p = jnp.exp(s - jnp.max(s, axis=-1, keepdims=True))
            l = jnp.sum(p, axis=-1, keepdims=True)
            acc = acc + _dot(p.astype(BF16), jnp.where(m, vc, 0), NN) / l
        o_ref[...] = acc.astype(BF16)

    win = _ca_window_specs(nq)
    return pl.pallas_call(
        body, name=name, grid=(WIDTH // PAIR, nq),
        in_specs=[qspec] + win + win + [tspec], out_specs=qspec,
        out_shape=jax.ShapeDtypeStruct((S, WIDTH), BF16),
        compiler_params=_cparams(("parallel", "parallel")),
    )(qn, kn, kn, kn, v, v, v, tab)


def _ca_bwd(qn, kn, v, do, tab, name):
    S = qn.shape[0]
    nq = S // BQ
    qspec = pl.BlockSpec((BQ, PAIR), lambda p, i: (jnp.minimum(i, nq - 1), p))
    kout = pl.BlockSpec((BQ, PAIR), lambda p, i: (jnp.clip(i - 2, 0, nq - 1), p))
    tspec = pl.BlockSpec((2, BQ, KWIN), lambda p, i: (p, 0, 0))

    def body(q_ref, do_ref, k0, k1, k2, v0, v1, v2, tab_ref,
             dq_ref, dk_ref, dv_ref, dtab_ref, dk_acc, dv_acc):
        i = pl.program_id(1)

        @pl.when(i == 0)
        def _():
            dk_acc[...] = jnp.zeros_like(dk_acc)
            dv_acc[...] = jnp.zeros_like(dv_acc)
            dtab_ref[...] = jnp.zeros_like(dtab_ref)

        @pl.when(i < nq)
        def _():
            kc = jnp.concatenate([k0[...], k1[...], k2[...]], axis=0)
            vc = jnp.concatenate([v0[...], v1[...], v2[...]], axis=0)
            qv, dov = q_ref[...], do_ref[...]
            dq = jnp.zeros((BQ, PAIR), F32)
            dkc = jnp.zeros((KWIN, PAIR), F32)
            dvc = jnp.zeros((KWIN, PAIR), F32)
            for h, m in enumerate(_head_masks()):
                qm, dom = jnp.where(m, qv, 0), jnp.where(m, dov, 0)
                s = _ca_scores(qm, kc, tab_ref[h], i)
                p = jnp.exp(s - jnp.max(s, axis=-1, keepdims=True))
                p = p / jnp.sum(p, axis=-1, keepdims=True)
                dp = _dot(dom, vc, NT)
                ds = p * (dp - jnp.sum(p * dp, axis=-1, keepdims=True))
                dtab_ref[h] += ds
                dsb, pb = ds.astype(BF16), p.astype(BF16)
                dq = dq + _dot(dsb, jnp.where(m, kc, 0), NN)
                dkc = dkc + _dot(dsb, qm, TN)
                dvc = dvc + _dot(pb, dom, TN)
            dq_ref[...] = dq
            for d in range(3):
                slot = (i + 1 + d) % 3
                dk_acc[slot] += dkc[d * BQ:(d + 1) * BQ]
                dv_acc[slot] += dvc[d * BQ:(d + 1) * BQ]

        @pl.when(i >= 2)
        def _():
            slot = (i + 1) % 3
            dk_ref[...] = dk_acc[slot]
            dv_ref[...] = dv_acc[slot].astype(BF16)
            dk_acc[slot] = jnp.zeros((BQ, PAIR), F32)
            dv_acc[slot] = jnp.zeros((BQ, PAIR), F32)

    win = _ca_window_specs(nq)
    return pl.pallas_call(
        body, name=name, grid=(WIDTH // PAIR, nq + 2),
        in_specs=[qspec, qspec] + win + win + [tspec],
        out_specs=[qspec, kout, kout, tspec],
        out_shape=[jax.ShapeDtypeStruct((S, WIDTH), F32), jax.ShapeDtypeStruct((S, WIDTH), F32),
                   jax.ShapeDtypeStruct((S, WIDTH), BF16), jax.ShapeDtypeStruct((N_HEADS, BQ, KWIN), F32)],
        scratch_shapes=[pltpu.VMEM((3, BQ, PAIR), F32)] * 2,
        compiler_params=_cparams(("parallel", "arbitrary")),
    )(qn, do, kn, kn, kn, v, v, v, tab)


def _sb_consts():
    r = lax.broadcasted_iota(jnp.int32, (BQ, BQ), 0)
    c = lax.broadcasted_iota(jnp.int32, (BQ, BQ), 1)
    after = jnp.where(r > c, 1.0, 0.0).astype(BF16)
    causal = c < r
    return after, causal


def _suffix_sum(t, after):
    hi, lo = _split_bf16(t)
    return _dot(hi, after, NN) + _dot(lo, after, NN)


def _sb_logs(z):
    e = jnp.exp(-jnp.abs(z))
    l = jnp.log(1.0 + e)
    return -(jnp.maximum(z, 0.0) + l), jnp.minimum(z, 0.0) - l, e


def _sb_fwd(q, k, v, name):
    S = q.shape[0]
    nq = S // BQ
    qspec = pl.BlockSpec((BQ, PAIR), lambda p, i: (i, p))
    kspec = pl.BlockSpec((S, PAIR), lambda p, i: (0, p))

    def body(q_ref, k_ref, v_ref, o_ref, of_ref):
        i = pl.program_id(1)
        after, causal = _sb_consts()
        qv = q_ref[...] * HEAD_DIM ** -0.5
        out = jnp.zeros((BQ, PAIR), F32)
        for m in _head_masks():
            qm = jnp.where(m, qv, 0)

            def tile(j, carry, acc, diag):
                rows = pl.ds(pl.multiple_of(j * BQ, BQ), BQ)
                kb, vb = k_ref[rows, :], v_ref[rows, :]
                log_keep, log_beta, _ = _sb_logs(_dot(qm, kb, NT))
                if diag:
                    log_keep = jnp.where(causal, log_keep, 0.0)
                w = jnp.exp(log_beta + carry + _suffix_sum(log_keep, after))
                if diag:
                    w = jnp.where(causal, w, 0.0)
                acc = acc + _dot(w.astype(BF16), jnp.where(m, vb, 0), NN)
                return carry + jnp.sum(log_keep, axis=-1, keepdims=True), acc

            carry, out = tile(i, jnp.zeros((BQ, 1), F32), out, True)
            carry, out = lax.fori_loop(0, i, lambda t, c: tile(i - 1 - t, c[0], c[1], False), (carry, out))
        o_ref[...] = out.astype(BF16)
        of_ref[...] = out

    return pl.pallas_call(
        body, name=name, grid=(WIDTH // PAIR, nq),
        in_specs=[qspec, kspec, kspec], out_specs=[qspec, qspec],
        out_shape=[jax.ShapeDtypeStruct((S, WIDTH), BF16), jax.ShapeDtypeStruct((S, WIDTH), F32)],
        compiler_params=_cparams(("parallel", "arbitrary")),
    )(q, k, v)


def _sb_bwd(q, k, v, o, do, name):
    S = q.shape[0]
    nq = S // BQ
    qspec = pl.BlockSpec((BQ, PAIR), lambda p, i: (i, p))
    kspec = pl.BlockSpec((S, PAIR), lambda p, i: (0, p))

    def body(q_ref, o_ref, do_ref, k_ref, v_ref, dq_ref, dk_ref, dv_ref, dk_acc, dv_acc):
        i = pl.program_id(1)

        @pl.when(i == 0)
        def _():
            dk_acc[...] = jnp.zeros_like(dk_acc)
            dv_acc[...] = jnp.zeros_like(dv_acc)

        after, causal = _sb_consts()
        qv, dov = q_ref[...] * HEAD_DIM ** -0.5, do_ref[...]
        od = o_ref[...] * dov.astype(F32)
        dq = jnp.zeros((BQ, PAIR), F32)
        for m in _head_masks():
            qm, dom = jnp.where(m, qv, 0), jnp.where(m, dov, 0)
            total = jnp.sum(jnp.where(m, od, 0.0), axis=-1, keepdims=True)

            def tile(j, c_keep, c_g, dq, diag):
                rows = pl.ds(pl.multiple_of(j * BQ, BQ), BQ)
                kb, vb = k_ref[rows, :], v_ref[rows, :]
                z = _dot(qm, kb, NT)
                log_keep, log_beta, e = _sb_logs(z)
                beta = jnp.where(z >= 0.0, 1.0, e) / (1.0 + e)
                if diag:
                    log_keep = jnp.where(causal, log_keep, 0.0)
                w = jnp.exp(log_beta + c_keep + _suffix_sum(log_keep, after))
                if diag:
                    w = jnp.where(causal, w, 0.0)
                wb = w.astype(BF16)
                g = wb.astype(F32) * _dot(dom, vb, NT)
                before = total - (c_g + _suffix_sum(g, after) + g)
                dz = g * (1.0 - beta) - before * beta
                if diag:
                    dz = jnp.where(causal, dz, 0.0)
                dzb = dz.astype(BF16)
                dq = dq + _dot(dzb, jnp.where(m, kb, 0), NN)
                dk_acc[rows, :] += _dot(dzb, qm, TN)
                dv_acc[rows, :] += _dot(wb, dom, TN)
                return (c_keep + jnp.sum(log_keep, axis=-1, keepdims=True),
                        c_g + jnp.sum(g, axis=-1, keepdims=True), dq)

            zero = jnp.zeros((BQ, 1), F32)
            c_keep, c_g, dq = tile(i, zero, zero, dq, True)
            _, _, dq = lax.fori_loop(0, i, lambda t, c: tile(i - 1 - t, c[0], c[1], c[2], False), (c_keep, c_g, dq))
        dq_ref[...] = (dq * HEAD_DIM ** -0.5).astype(BF16)

        @pl.when(i == nq - 1)
        def _():
            dk_ref[...] = dk_acc[...].astype(BF16)
            dv_ref[...] = dv_acc[...].astype(BF16)

    return pl.pallas_call(
        body, name=name, grid=(WIDTH // PAIR, nq),
        in_specs=[qspec, qspec, qspec, kspec, kspec], out_specs=[qspec, kspec, kspec],
        out_shape=[jax.ShapeDtypeStruct((S, WIDTH), BF16)] * 3,
        scratch_shapes=[pltpu.VMEM((S, PAIR), F32)] * 2,
        compiler_params=_cparams(("parallel", "arbitrary")),
    )(q, o, do, k, v)


ANY = pl.BlockSpec(memory_space=pl.ANY)


def _place():
    return lax.axis_index("x"), lax.axis_index("y"), lax.axis_index("c")


def _other_chips(x, y):
    return [(2 * px + py, (px, py)) for px, py in ((1 - x, y), (x, 1 - y), (1 - x, 1 - y))]


def _remote(src, dst, sems, k, to):
    return pltpu.make_async_remote_copy(src_ref=src, dst_ref=dst, send_sem=sems[0].at[k], recv_sem=sems[1].at[k],
                                        device_id=to, device_id_type=MESH)


def _gather_weights(wp, name):
    R = wp.shape[0]
    Rh = R // 2

    def body(wp_ref, out_ref, send_sems, recv_sems, local_sem):
        x, y, c = _place()
        me = 2 * x + y
        sems = (send_sems, recv_sems)
        half = pl.ds(c * Rh, Rh)
        mine = pltpu.make_async_copy(wp_ref, out_ref.at[me], local_sem)
        mine.start()
        chips = _other_chips(x, y)
        first = [_remote(wp_ref.at[half], out_ref.at[me, half], sems, k, (*xy, c)) for k, (_, xy) in enumerate(chips)]
        for cp in first:
            cp.start()
        passed = []
        for k, (chip, xy) in enumerate(chips):
            landed = out_ref.at[chip, half]
            _remote(landed, landed, sems, k, (*xy, c)).wait_recv()
            cp = _remote(landed, landed, sems, 3 + k, (x, y, 1 - c))
            cp.start()
            passed.append(cp)
        other = pl.ds((1 - c) * Rh, Rh)
        for k, (chip, xy) in enumerate(chips):
            landed = out_ref.at[chip, other]
            _remote(landed, landed, sems, 3 + k, (x, y, 1 - c)).wait_recv()
        for cp in first + passed:
            cp.wait_send()
        mine.wait()

    return pl.pallas_call(
        body, name=name, in_specs=[ANY], out_specs=ANY,
        out_shape=jax.ShapeDtypeStruct((N_CHIPS, R, PACK_COLS), wp.dtype),
        scratch_shapes=[pltpu.SemaphoreType.DMA((6,)), pltpu.SemaphoreType.DMA((6,)), pltpu.SemaphoreType.DMA(())],
    )(wp)


def _exchange_cores(g, small, name):
    R = g.shape[1]
    Rh = R // 2

    def body(g_ref, small_ref, mine_ref, sib_ref, all_ref, send_sems, recv_sems, local_sems):
        x, y, c = _place()
        me = 4 * x + 2 * y + c
        sems = (send_sems, recv_sems)
        keep = pltpu.make_async_copy(g_ref.at[:, pl.ds(c * Rh, Rh), :], mine_ref, local_sems.at[0])
        own = pltpu.make_async_copy(small_ref, all_ref.at[me], local_sems.at[1])
        keep.start()
        own.start()
        copies = [_remote(g_ref.at[:, pl.ds((1 - c) * Rh, Rh), :], sib_ref, sems, 0, (x, y, 1 - c))]
        k = 1
        for fx in (0, 1):
            for fy in (0, 1):
                for fc in (0, 1):
                    if fx or fy or fc:
                        to = (1 - x if fx else x, 1 - y if fy else y, 1 - c if fc else c)
                        copies.append(_remote(small_ref, all_ref.at[me], sems, k, to))
                        k += 1
        for cp in copies:
            cp.start()
        for cp in copies:
            cp.wait_recv()
        for cp in copies:
            cp.wait_send()
        keep.wait()
        own.wait()

    return pl.pallas_call(
        body, name=name, in_specs=[ANY, ANY], out_specs=[ANY, ANY, ANY],
        out_shape=[jax.ShapeDtypeStruct((N_CHIPS, Rh, PACK_COLS), F32)] * 2
        + [jax.ShapeDtypeStruct((N_DEV, SMALL_ROWS, PACK_COLS), F32)],
        scratch_shapes=[pltpu.SemaphoreType.DMA((8,)), pltpu.SemaphoreType.DMA((8,)), pltpu.SemaphoreType.DMA((2,))],
    )(g, small)


def _exchange_chips(p, name):
    def body(p_ref, out_ref, send_sems, recv_sems, local_sem):
        x, y, c = _place()
        me = 2 * x + y
        sems = (send_sems, recv_sems)
        own = pltpu.make_async_copy(p_ref.at[me], out_ref.at[me], local_sem)
        own.start()
        copies = [_remote(p_ref.at[chip], out_ref.at[me], sems, k, (*xy, c))
                  for k, (chip, xy) in enumerate(_other_chips(x, y))]
        for cp in copies:
            cp.start()
        for k, (chip, xy) in enumerate(_other_chips(x, y)):
            _remote(p_ref.at[chip], out_ref.at[chip], sems, k, (*xy, c)).wait_recv()
        for cp in copies:
            cp.wait_send()
        own.wait()

    return pl.pallas_call(
        body, name=name, in_specs=[ANY], out_specs=ANY,
        out_shape=jax.ShapeDtypeStruct(p.shape, p.dtype),
        scratch_shapes=[pltpu.SemaphoreType.DMA((3,)), pltpu.SemaphoreType.DMA((3,)), pltpu.SemaphoreType.DMA(())],
    )(p)


def _share_halves(gh, name):
    Rh = gh.shape[0]

    def body(gh_ref, out_ref, send_sems, recv_sems, local_sem):
        x, y, c = _place()
        sems = (send_sems, recv_sems)
        own = pltpu.make_async_copy(gh_ref, out_ref.at[pl.ds(c * Rh, Rh)], local_sem)
        own.start()
        cp = _remote(gh_ref, out_ref.at[pl.ds(c * Rh, Rh)], sems, 0, (x, y, 1 - c))
        cp.start()
        other = out_ref.at[pl.ds((1 - c) * Rh, Rh)]
        _remote(other, other, sems, 0, (x, y, 1 - c)).wait_recv()
        cp.wait_send()
        own.wait()

    return pl.pallas_call(
        body, name=name, in_specs=[ANY], out_specs=ANY,
        out_shape=jax.ShapeDtypeStruct((2 * Rh, PACK_COLS), gh.dtype),
        scratch_shapes=[pltpu.SemaphoreType.DMA((1,)), pltpu.SemaphoreType.DMA((1,)), pltpu.SemaphoreType.DMA(())],
    )(gh)


PACK_ROWS_CAP = 128


def _row_block(rows):
    return max(b for b in range(8, PACK_ROWS_CAP + 1, 8) if rows % b == 0)


def _add2(a, b, name):
    n, Rh, C = a.shape
    rows = _row_block(Rh)
    spec = pl.BlockSpec((1, rows, C), lambda j, i: (j, i, 0))

    def body(a_ref, b_ref, o_ref):
        o_ref[...] = a_ref[...] + b_ref[...]

    return pl.pallas_call(
        body, name=name, grid=(n, Rh // rows), in_specs=[spec, spec], out_specs=spec,
        out_shape=jax.ShapeDtypeStruct(a.shape, F32),
        compiler_params=_cparams(("parallel", "parallel")),
    )(a, b)


def _sum_leading(a, name):
    n, R, C = a.shape
    rows = _row_block(R)

    def body(a_ref, o_ref):
        acc = a_ref[0]
        for j in range(1, n):
            acc = acc + a_ref[j]
        o_ref[...] = acc

    return pl.pallas_call(
        body, name=name, grid=(R // rows,),
        in_specs=[pl.BlockSpec((n, rows, C), lambda i: (0, i, 0))],
        out_specs=pl.BlockSpec((rows, C), lambda i: (i, 0)),
        out_shape=jax.ShapeDtypeStruct((R, C), F32),
        compiler_params=_cparams(("parallel",)),
    )(a)


def _adamw(w, g, m, v, name):
    R, C = w.shape
    rows = _row_block(R)
    spec = pl.BlockSpec((rows, C), lambda i: (i, 0))

    def body(w_ref, g_ref, m_ref, v_ref, d_ref, mo_ref, vo_ref):
        gv = g_ref[...]
        mn = ADAM_B1 * m_ref[...] + (1.0 - ADAM_B1) * gv
        vn = ADAM_B2 * v_ref[...] + (1.0 - ADAM_B2) * (gv * gv)
        m_hat = mn / (1.0 - ADAM_B1 ** ADAM_STEP)
        v_hat = vn / (1.0 - ADAM_B2 ** ADAM_STEP)
        d_ref[...] = -ADAM_LR * (m_hat / (jnp.sqrt(v_hat) + ADAM_EPS) + ADAM_WD * w_ref[...])
        mo_ref[...] = mn
        vo_ref[...] = vn

    return pl.pallas_call(
        body, name=name, grid=(R // rows,), in_specs=[spec] * 4, out_specs=[spec] * 3,
        out_shape=[jax.ShapeDtypeStruct((R, C), F32)] * 3,
        compiler_params=_cparams(("parallel",)),
    )(w, g, m, v)


BIG = ("w_in", "w_branch_a", "w_branch_b", "w_out", "w_ffn_up", "w_ffn_down", "ffn_conv_w")
BIG_SHARD = {"w_in": (D_MODEL, 1280), "w_branch_a": (WIDTH, 256), "w_branch_b": (WIDTH, 256),
             "w_out": (256, D_MODEL), "w_ffn_up": (D_MODEL, 1408), "w_ffn_down": (704, D_MODEL),
             "ffn_conv_w": (3, 1408)}
BIG_COL_SHARDED = {"w_in": True, "w_branch_a": True, "w_branch_b": True, "w_out": False,
                   "w_ffn_up": True, "w_ffn_down": False, "ffn_conv_w": True}
CONV_W_ROWS = 32


def _pack_rows(name):
    r, c = BIG_SHARD[name]
    return CONV_W_ROWS if name == "ffn_conv_w" else r * c // PACK_COLS


def _rows_of(flat, rows):
    return jnp.pad(flat, (0, rows * PACK_COLS - flat.shape[0])).reshape(rows, PACK_COLS)


def _pack_shards(shards, dtype=F32):
    parts = []
    for n in BIG:
        if n == "ffn_conv_w":
            flat = shards[n].reshape(-1)
            if dtype == BF16:
                flat = lax.bitcast_convert_type(flat, BF16).reshape(-1)
            parts.append(_rows_of(flat, CONV_W_ROWS))
        else:
            parts.append(shards[n].reshape(_pack_rows(n), PACK_COLS).astype(dtype))
    return jnp.concatenate(parts, axis=0)


def _unpack_shards(packed):
    out, r0 = {}, 0
    for n in BIG:
        r, c = BIG_SHARD[n]
        out[n] = packed[r0:r0 + _pack_rows(n)].reshape(-1)[:r * c].reshape(r, c)
        r0 += _pack_rows(n)
    return out


def _unpack_full(gathered):
    out, r0 = {}, 0
    for n in BIG:
        r, c = BIG_SHARD[n]
        part = gathered[:, r0:r0 + _pack_rows(n)].reshape(N_CHIPS, -1)
        if n == "ffn_conv_w":
            part = lax.bitcast_convert_type(part[:, :2 * r * c].reshape(N_CHIPS, r * c, 2), F32)
        parts = part[:, :r * c].reshape(N_CHIPS, r, c)
        out[n] = jnp.concatenate(list(parts), axis=1 if BIG_COL_SHARDED[n] else 0)
        r0 += _pack_rows(n)
    return out


def _pack_full(full):
    slabs = []
    for j in range(N_CHIPS):
        shards = {}
        for n in BIG:
            r, c = BIG_SHARD[n]
            shards[n] = full[n][:, j * c:(j + 1) * c] if BIG_COL_SHARDED[n] else full[n][j * r:(j + 1) * r]
        slabs.append(_pack_shards(shards))
    return jnp.stack(slabs)


SMALL = (("norm1_g", D_MODEL), ("q_norm_g", HEAD_DIM), ("k_norm_g", HEAD_DIM), ("rel_bias", N_HEADS * N_REL),
         ("norm2_g", D_MODEL), ("ffn_conv_b", 2 * D_FF))


def _pack_small(vals):
    return _rows_of(jnp.concatenate([vals[n].reshape(-1) for n, _ in SMALL]), SMALL_ROWS)


def _unpack_small(packed):
    flat, out, o = packed.reshape(-1), {}, 0
    for n, sz in SMALL:
        out[n] = flat[o:o + sz]
        o += sz
    return out


def kernel(x, norm1_g, w_in, q_norm_g, k_norm_g, rel_bias, w_branch_a, w_branch_b, w_out, norm2_g, w_ffn_up, ffn_conv_w, ffn_conv_b, w_ffn_down, loss_target, m_norm1_g, m_w_in, m_q_norm_g, m_k_norm_g, m_rel_bias, m_w_branch_a, m_w_branch_b, m_w_out, m_norm2_g, m_w_ffn_up, m_ffn_conv_w, m_ffn_conv_b, m_w_ffn_down, v_norm1_g, v_w_in, v_q_norm_g, v_k_norm_g, v_rel_bias, v_w_branch_a, v_w_branch_b, v_w_out, v_norm2_g, v_w_ffn_up, v_ffn_conv_w, v_ffn_conv_b, v_w_ffn_down):
    w_big = {"w_in": w_in[0], "w_branch_a": w_branch_a[0], "w_branch_b": w_branch_b[0], "w_out": w_out[0],
             "w_ffn_up": w_ffn_up[0], "w_ffn_down": w_ffn_down[0], "ffn_conv_w": ffn_conv_w[0]}
    m_big = {"w_in": m_w_in[0], "w_branch_a": m_w_branch_a[0], "w_branch_b": m_w_branch_b[0], "w_out": m_w_out[0],
             "w_ffn_up": m_w_ffn_up[0], "w_ffn_down": m_w_ffn_down[0], "ffn_conv_w": m_ffn_conv_w[0]}
    v_big = {"w_in": v_w_in[0], "w_branch_a": v_w_branch_a[0], "w_branch_b": v_w_branch_b[0], "w_out": v_w_out[0],
             "w_ffn_up": v_w_ffn_up[0], "w_ffn_down": v_w_ffn_down[0], "ffn_conv_w": v_ffn_conv_w[0]}
    xs, tgt = x[0], loss_target[0]

    full = _unpack_full(_gather_weights(_pack_shards(w_big, BF16), "gather_weights"))
    w_in_f, w_a, w_b, w_o = full["w_in"], full["w_branch_a"], full["w_branch_b"], full["w_out"]
    w_up, w_dn = full["w_ffn_up"], full["w_ffn_down"]
    w_up_g, w_up_u = w_up[:, :D_FF], w_up[:, D_FF:]
    cwg, cwu = full["ffn_conv_w"][:, :D_FF], full["ffn_conv_w"][:, D_FF:]
    cbg, cbu = ffn_conv_b[:, :D_FF], ffn_conv_b[:, D_FF:]

    hn = _rms_fwd(xs, norm1_g, "rms1")
    qk = _matmul(hn, w_in_f[:, :2 * WIDTH], "nn", F32, "proj_qk")
    vqkv = _matmul(hn, w_in_f[:, 2 * WIDTH:6 * WIDTH], "nn", BF16, "proj_vqkv")
    g_a = _matmul(hn, w_in_f[:, 6 * WIDTH:6 * WIDTH + D_MODEL], "nn", F32, "proj_gate_a")
    g_b = _matmul(hn, w_in_f[:, 6 * WIDTH + D_MODEL:], "nn", F32, "proj_gate_b")
    gq = jnp.tile(q_norm_g, (1, N_HEADS))
    gk = jnp.tile(k_norm_g, (1, N_HEADS))
    qa, ka = _qknorm_fwd(qk, gq, gk, "qknorm")
    va, qb, kb, vb = (vqkv[:, k * WIDTH:(k + 1) * WIDTH] for k in range(4))
    tab = _bias_table(rel_bias[0], "bias_table")
    out_a = _ca_fwd(qa, ka, va, tab, "chunk_attn")
    out_b, out_b_f32 = _sb_fwd(qb, kb, vb, "stick_attn")
    y_a = _matmul(out_a, w_a, "nn", F32, "branch_a")
    y_b = _matmul(out_b, w_b, "nn", F32, "branch_b")
    mixed = _mix_fwd(g_a, g_b, y_a, y_b, "mix")
    x2 = _matmul(mixed, w_o, "nn", F32, "out_proj", residual=xs)
    hn2 = _rms_fwd(x2, norm2_g, "rms2")
    hid_g = _matmul(hn2, w_up_g, "nn", BF16, "ffn_up_gate")
    hid_u = _matmul(hn2, w_up_u, "nn", BF16, "ffn_up_up")
    act = _convglu_fwd(hid_g, hid_u, cwg, cwu, cbg, cbu, "convglu")
    y = _matmul(act, w_dn, "nn", F32, "ffn_down", residual=x2)
    dy, dyb, sq = _loss_head(y, tgt, "loss_head")
    loss = lax.psum(0.5 / D_MODEL * jnp.sum(sq), ("x", "y", "c"))

    dact = _matmul(dyb, w_dn, "nt", BF16, "d_act")
    d_w_dn = _matmul(act, dyb, "tn", F32, "d_w_down")
    dhg, dhu, dcwg, dcwu, dcbg, dcbu = _convglu_bwd(hid_g, hid_u, dact, cwg, cwu, cbg, cbu, "convglu_bwd")
    d_w_up = jnp.concatenate([_matmul(hn2, dhg, "tn", F32, "d_w_up_gate"),
                              _matmul(hn2, dhu, "tn", F32, "d_w_up_up")], axis=1)
    dhn2 = _matmul(dhg, w_up_g, "nt", F32, "d_hn2_gate")
    dhn2 = _matmul(dhu, w_up_u, "nt", F32, "d_hn2_up", residual=dhn2)
    dx2, dx2b, d_norm2 = _rms_bwd(x2, norm2_g, dhn2, dy, "rms2_bwd")
    dmixed = _matmul(dx2b, w_o, "nt", F32, "d_mixed")
    d_w_o = _matmul(mixed, dx2b, "tn", F32, "d_w_out")
    dga, dgb, dya, dyb_b = _mix_bwd(dmixed, g_a, g_b, y_a, y_b, "mix_bwd")
    d_w_a = _matmul(out_a, dya, "tn", F32, "d_w_branch_a")
    d_w_b = _matmul(out_b, dyb_b, "tn", F32, "d_w_branch_b")
    do_a = _matmul(dya, w_a, "nt", BF16, "d_out_a")
    do_b = _matmul(dyb_b, w_b, "nt", BF16, "d_out_b")
    dqb, dkb, dvb = _sb_bwd(qb, kb, vb, out_b_f32, do_b, "stick_attn_bwd")
    dqa_n, dka_n, dva, dtab = _ca_bwd(qa, ka, va, do_a, tab, "chunk_attn_bwd")
    d_rel = _bias_table_bwd(dtab, "bias_table_bwd")[:, 0, :N_REL]
    dqa, dka, dgq, dgk = _qknorm_bwd(qk, gq, gk, dqa_n, dka_n, "qknorm_bwd")
    dproj = jnp.concatenate([dqa, dka, dva, dqb, dkb, dvb, dga, dgb], axis=1)
    d_w_in = _matmul(hn, dproj, "tn", F32, "d_w_in")
    dhn = _matmul(dproj, w_in_f, "nt", F32, "d_hn")
    dx, _, d_norm1 = _rms_bwd(xs, norm1_g, dhn, dx2, "rms1_bwd")

    g_pack = _pack_full({"w_in": d_w_in, "w_branch_a": d_w_a, "w_branch_b": d_w_b, "w_out": d_w_o,
                         "w_ffn_up": d_w_up, "w_ffn_down": d_w_dn,
                         "ffn_conv_w": jnp.concatenate([dcwg, dcwu], axis=1)})
    small_g = _pack_small({"norm1_g": d_norm1, "q_norm_g": dgq.reshape(N_HEADS, HEAD_DIM).sum(0),
                           "k_norm_g": dgk.reshape(N_HEADS, HEAD_DIM).sum(0), "rel_bias": d_rel,
                           "norm2_g": d_norm2, "ffn_conv_b": jnp.concatenate([dcbg, dcbu], axis=1)})
    mine, sib, small_all = _exchange_cores(g_pack, small_g, "exchange_cores")
    chip_part = _add2(mine, sib, "sum_cores")
    parts = _exchange_chips(chip_part, "exchange_chips")
    g_half = _sum_leading(parts, "sum_chips")
    g_shard = _share_halves(g_half, "share_halves")
    small_sum = _sum_leading(small_all, "sum_small")

    d_pack, m_pack, v_pack = _adamw(_pack_shards(w_big), g_shard, _pack_shards(m_big), _pack_shards(v_big),
                                    "adamw_big")
    grads = _unpack_shards(g_shard)
    deltas, new_m, new_v = _unpack_shards(d_pack), _unpack_shards(m_pack), _unpack_shards(v_pack)

    shapes = {"norm1_g": norm1_g.shape, "q_norm_g": q_norm_g.shape, "k_norm_g": k_norm_g.shape,
              "rel_bias": rel_bias.shape, "norm2_g": norm2_g.shape, "ffn_conv_b": ffn_conv_b.shape}
    small_w = {"norm1_g": norm1_g, "q_norm_g": q_norm_g, "k_norm_g": k_norm_g, "rel_bias": rel_bias,
               "norm2_g": norm2_g, "ffn_conv_b": ffn_conv_b}
    small_m = {"norm1_g": m_norm1_g, "q_norm_g": m_q_norm_g, "k_norm_g": m_k_norm_g, "rel_bias": m_rel_bias,
               "norm2_g": m_norm2_g, "ffn_conv_b": m_ffn_conv_b}
    small_v = {"norm1_g": v_norm1_g, "q_norm_g": v_q_norm_g, "k_norm_g": v_k_norm_g, "rel_bias": v_rel_bias,
               "norm2_g": v_norm2_g, "ffn_conv_b": v_ffn_conv_b}
    ds, ms, vs = _adamw(_pack_small(small_w), small_sum, _pack_small(small_m), _pack_small(small_v), "adamw_small")
    small_grads, ds, ms, vs = (_unpack_small(t) for t in (small_sum, ds, ms, vs))

    order = ("norm1_g", "w_in", "q_norm_g", "k_norm_g", "rel_bias", "w_branch_a", "w_branch_b", "w_out",
             "norm2_g", "w_ffn_up", "ffn_conv_w", "ffn_conv_b", "w_ffn_down")
    outs = [loss, dx[None]]
    for big, small in ((grads, small_grads), (deltas, ds), (new_m, ms), (new_v, vs)):
        for n in order:
            outs.append(big[n][None] if n in big else small[n].reshape(shapes[n]))
    return tuple(outs)
```

```python
import functools

import jax
import jax.numpy as jnp
from jax import lax
from jax.experimental import pallas as pl
from jax.experimental.pallas import tpu as pltpu

F32 = jnp.float32
BF16 = jnp.bfloat16
MESH = pl.DeviceIdType.MESH

D_MODEL = 1024
HEAD_DIM = 64
N_HEADS = 8
WIDTH = N_HEADS * HEAD_DIM
CHUNK = 64
LEFT_CHUNKS = 8
MAX_REL = 128
N_REL = 2 * MAX_REL + 1
D_FF = 2816
EPS = 1e-6
NEG = -1e30

ADAM_LR = 0.001
ADAM_B1 = 0.9
ADAM_B2 = 0.999
ADAM_EPS = 1e-08
ADAM_WD = 0.01
ADAM_STEP = 10

N_CHIPS = 4
N_DEV = 8
LANES = 128
PAIR = 2 * HEAD_DIM
BQ = 256
BAND = LEFT_CHUNKS * CHUNK
KWIN = BAND + BQ
VMEM_LIMIT = 56 * 1024 * 1024
PACK_COLS = 1024
SMALL_ROWS = 16

NN = (((1,), (0,)), ((), ()))
NT = (((1,), (1,)), ((), ()))
TN = (((0,), (0,)), ((), ()))


def _cparams(sem=None):
    if sem is None:
        return pltpu.CompilerParams(vmem_limit_bytes=VMEM_LIMIT)
    return pltpu.CompilerParams(dimension_semantics=sem, vmem_limit_bytes=VMEM_LIMIT)


def _pick(n, cands):
    for c in cands:
        if n % c == 0:
            return c
    raise ValueError(f"no block for {n}")


def _dot(a, b, dn):
    return lax.dot_general(a, b, dn, preferred_element_type=F32)


def _sigmoid(x):
    return 1.0 / (1.0 + jnp.exp(-x))


def _split_bf16(x):
    hi = x.astype(BF16)
    lo = (x - hi.astype(F32)).astype(BF16)
    return hi, lo


def _matmul(a, b, mode, out_dtype, name, residual=None):
    if mode == "nn":
        (M, K), N = a.shape, b.shape[1]
    elif mode == "nt":
        (M, K), N = a.shape, b.shape[0]
    else:
        (K, M), N = a.shape, b.shape[1]
    if mode == "tn":
        bm = _pick(M, (512, 256, 128))
        bn = _pick(N, (512, 256, 128))
        bk = _pick(K, (1024, 512))
    else:
        bm = _pick(M, (1024, 512)) if K <= 1024 else 512
        bn = _pick(N, (512, 256, 128))
        bk = K if K <= D_FF else _pick(K, (1024, 512))
    nk = K // bk
    dn = {"nn": NN, "nt": NT, "tn": TN}[mode]
    a_spec = {"nn": pl.BlockSpec((bm, bk), lambda i, j, k: (i, k)),
              "nt": pl.BlockSpec((bm, bk), lambda i, j, k: (i, k)),
              "tn": pl.BlockSpec((bk, bm), lambda i, j, k: (k, i))}[mode]
    b_spec = {"nn": pl.BlockSpec((bk, bn), lambda i, j, k: (k, j)),
              "nt": pl.BlockSpec((bn, bk), lambda i, j, k: (j, k)),
              "tn": pl.BlockSpec((bk, bn), lambda i, j, k: (k, j))}[mode]
    o_spec = pl.BlockSpec((bm, bn), lambda i, j, k: (i, j))
    has_res = residual is not None

    def body(*refs):
        if has_res:
            a_ref, b_ref, r_ref, o_ref, acc_ref = refs
        else:
            a_ref, b_ref, o_ref, acc_ref = refs
        k = pl.program_id(2)
        part = _dot(a_ref[...], b_ref[...], dn)

        def finish(total):
            if has_res:
                total = total + r_ref[...]
            o_ref[...] = total.astype(out_dtype)

        if nk == 1:
            finish(part)
        else:
            @pl.when(k == 0)
            def _():
                acc_ref[...] = part

            @pl.when(k > 0)
            def _():
                acc_ref[...] += part

            @pl.when(k == nk - 1)
            def _():
                finish(acc_ref[...])

    in_specs = [a_spec, b_spec] + ([o_spec] if has_res else [])
    args = (a, b) + ((residual,) if has_res else ())
    return pl.pallas_call(
        body, name=name,
        grid=(M // bm, N // bn, nk),
        in_specs=in_specs, out_specs=o_spec,
        out_shape=jax.ShapeDtypeStruct((M, N), out_dtype),
        scratch_shapes=[pltpu.VMEM((bm, bn) if nk > 1 else (8, LANES), F32)],
        compiler_params=_cparams(("parallel", "parallel", "arbitrary")),
    )(*args)


ROWS = 512


def _row_spec(cols, bm=ROWS):
    return pl.BlockSpec((bm, cols), lambda i: (i, 0))


def _full_spec(shape):
    return pl.BlockSpec(shape, lambda i: (0,) * len(shape))


def _colsum8(t):
    return jnp.sum(t.reshape(t.shape[0] // 8, 8, t.shape[1]), axis=0)


def _rms_fwd(x, g, name):
    S, D = x.shape

    def body(x_ref, g_ref, o_ref):
        xv = x_ref[...]
        r = lax.rsqrt(jnp.mean(xv * xv, axis=-1, keepdims=True) + EPS)
        o_ref[...] = (xv * r * g_ref[...]).astype(BF16)

    return pl.pallas_call(
        body, name=name, grid=(S // ROWS,),
        in_specs=[_row_spec(D), _full_spec((1, D))], out_specs=_row_spec(D),
        out_shape=jax.ShapeDtypeStruct((S, D), BF16),
        compiler_params=_cparams(("parallel",)),
    )(x, g)


def _rms_bwd(x, g, dy, dres, name):
    S, D = x.shape
    nt = S // ROWS

    def body(x_ref, g_ref, dy_ref, dres_ref, dx_ref, dxb_ref, dg_ref, acc_ref):
        i = pl.program_id(0)
        xv, dyv = x_ref[...], dy_ref[...]
        r = lax.rsqrt(jnp.mean(xv * xv, axis=-1, keepdims=True) + EPS)
        xr = xv * r
        u = dyv * g_ref[...]
        dx = r * u - xr * (r * r) * jnp.mean(xv * u, axis=-1, keepdims=True) + dres_ref[...]
        dx_ref[...] = dx
        dxb_ref[...] = dx.astype(BF16)
        part = _colsum8(dyv * xr)

        @pl.when(i == 0)
        def _():
            acc_ref[...] = part

        @pl.when(i > 0)
        def _():
            acc_ref[...] += part

        @pl.when(i == nt - 1)
        def _():
            dg_ref[...] = jnp.sum(acc_ref[...], axis=0, keepdims=True)

    return pl.pallas_call(
        body, name=name, grid=(nt,),
        in_specs=[_row_spec(D), _full_spec((1, D)), _row_spec(D), _row_spec(D)],
        out_specs=[_row_spec(D), _row_spec(D), _full_spec((1, D))],
        out_shape=[jax.ShapeDtypeStruct((S, D), F32), jax.ShapeDtypeStruct((S, D), BF16),
                   jax.ShapeDtypeStruct((1, D), F32)],
        scratch_shapes=[pltpu.VMEM((8, D), F32)],
        compiler_params=_cparams(("arbitrary",)),
    )(x, g, dy, dres)


def _head_mean(t, blockdiag):
    hi, lo = _split_bf16(t)
    return (_dot(hi, blockdiag, NN) + _dot(lo, blockdiag, NN)) * (1.0 / HEAD_DIM)


def _blockdiag():
    r = lax.broadcasted_iota(jnp.int32, (WIDTH, WIDTH), 0) // HEAD_DIM
    c = lax.broadcasted_iota(jnp.int32, (WIDTH, WIDTH), 1) // HEAD_DIM
    return jnp.where(r == c, 1.0, 0.0).astype(BF16)


def _qknorm_fwd(qk, gq, gk, name):
    S = qk.shape[0]

    def body(qk_ref, gq_ref, gk_ref, q_ref, k_ref):
        bd = _blockdiag()
        for part, g_ref, o_ref, scale in ((0, gq_ref, q_ref, HEAD_DIM ** -0.5), (1, gk_ref, k_ref, 1.0)):
            t = qk_ref[:, part * WIDTH:(part + 1) * WIDTH]
            r = lax.rsqrt(_head_mean(t * t, bd) + EPS)
            o_ref[...] = (t * r * g_ref[...] * scale).astype(BF16)

    return pl.pallas_call(
        body, name=name, grid=(S // ROWS,),
        in_specs=[_row_spec(2 * WIDTH), _full_spec((1, WIDTH)), _full_spec((1, WIDTH))],
        out_specs=[_row_spec(WIDTH), _row_spec(WIDTH)],
        out_shape=[jax.ShapeDtypeStruct((S, WIDTH), BF16)] * 2,
        compiler_params=_cparams(("parallel",)),
    )(qk, gq, gk)


def _qknorm_bwd(qk, gq, gk, dqn, dkn, name):
    S = qk.shape[0]
    nt = S // ROWS

    def body(qk_ref, gq_ref, gk_ref, dqn_ref, dkn_ref, dq_ref, dk_ref, dgq_ref, dgk_ref, accq_ref, acck_ref):
        i = pl.program_id(0)
        bd = _blockdiag()
        for part, g_ref, dn_ref, o_ref, dg_ref, acc_ref, scale in (
                (0, gq_ref, dqn_ref, dq_ref, dgq_ref, accq_ref, HEAD_DIM ** -0.5),
                (1, gk_ref, dkn_ref, dk_ref, dgk_ref, acck_ref, 1.0)):
            t = qk_ref[:, part * WIDTH:(part + 1) * WIDTH]
            dn = dn_ref[...] * scale
            r = lax.rsqrt(_head_mean(t * t, bd) + EPS)
            u = dn * g_ref[...]
            dt = r * u - t * (r * r * r) * _head_mean(t * u, bd)
            o_ref[...] = dt.astype(BF16)
            psum = _colsum8(dn * t * r)

            @pl.when(i == 0)
            def _():
                acc_ref[...] = psum

            @pl.when(i > 0)
            def _():
                acc_ref[...] += psum

            @pl.when(i == nt - 1)
            def _():
                dg_ref[...] = jnp.sum(acc_ref[...], axis=0, keepdims=True)

    return pl.pallas_call(
        body, name=name, grid=(nt,),
        in_specs=[_row_spec(2 * WIDTH), _full_spec((1, WIDTH)), _full_spec((1, WIDTH)),
                  _row_spec(WIDTH), _row_spec(WIDTH)],
        out_specs=[_row_spec(WIDTH), _row_spec(WIDTH), _full_spec((1, WIDTH)), _full_spec((1, WIDTH))],
        out_shape=[jax.ShapeDtypeStruct((S, WIDTH), BF16)] * 2 + [jax.ShapeDtypeStruct((1, WIDTH), F32)] * 2,
        scratch_shapes=[pltpu.VMEM((8, WIDTH), F32)] * 2,
        compiler_params=_cparams(("arbitrary",)),
    )(qk, gq, gk, dqn, dkn)


def _mix_fwd(ga, gb, ya, yb, name):
    S, D = ga.shape

    def body(ga_ref, gb_ref, ya_ref, yb_ref, o_ref):
        o_ref[...] = (_sigmoid(ga_ref[...]) * ya_ref[...] + _sigmoid(gb_ref[...]) * yb_ref[...]).astype(BF16)

    return pl.pallas_call(
        body, name=name, grid=(S // ROWS,),
        in_specs=[_row_spec(D)] * 4, out_specs=_row_spec(D),
        out_shape=jax.ShapeDtypeStruct((S, D), BF16),
        compiler_params=_cparams(("parallel",)),
    )(ga, gb, ya, yb)


def _mix_bwd(dm, ga, gb, ya, yb, name):
    S, D = ga.shape

    def body(dm_ref, ga_ref, gb_ref, ya_ref, yb_ref, dga_ref, dgb_ref, dya_ref, dyb_ref):
        dmv = dm_ref[...]
        for g_ref, y_ref, dg_ref, dy_ref in ((ga_ref, ya_ref, dga_ref, dya_ref), (gb_ref, yb_ref, dgb_ref, dyb_ref)):
            s = _sigmoid(g_ref[...])
            dy_ref[...] = (dmv * s).astype(BF16)
            dg_ref[...] = (dmv * y_ref[...] * s * (1.0 - s)).astype(BF16)

    return pl.pallas_call(
        body, name=name, grid=(S // ROWS,),
        in_specs=[_row_spec(D)] * 5, out_specs=[_row_spec(D)] * 4,
        out_shape=[jax.ShapeDtypeStruct((S, D), BF16)] * 4,
        compiler_params=_cparams(("parallel",)),
    )(dm, ga, gb, ya, yb)


def _loss_head(y, target, name):
    S, D = y.shape
    nt = S // ROWS

    def body(y_ref, t_ref, dy_ref, dyb_ref, p_ref):
        err = y_ref[...] - t_ref[...]
        dy = err * (1.0 / D)
        dy_ref[...] = dy
        dyb_ref[...] = dy.astype(BF16)
        sq = _colsum8(err * err)
        acc = sq[:, 0:LANES]
        for k in range(1, D // LANES):
            acc = acc + sq[:, k * LANES:(k + 1) * LANES]
        p_ref[...] = acc

    return pl.pallas_call(
        body, name=name, grid=(nt,),
        in_specs=[_row_spec(D)] * 2,
        out_specs=[_row_spec(D), _row_spec(D), pl.BlockSpec((8, LANES), lambda i: (i, 0))],
        out_shape=[jax.ShapeDtypeStruct((S, D), F32), jax.ShapeDtypeStruct((S, D), BF16),
                   jax.ShapeDtypeStruct((nt * 8, LANES), F32)],
        compiler_params=_cparams(("parallel",)),
    )(y, target)


CONV_COLS = 256
HALO = 16


def _conv_taps(xe, cw, cb):
    return cw[0:1] * pltpu.roll(xe, 2, 0) + cw[1:2] * pltpu.roll(xe, 1, 0) + cw[2:3] * xe + cb


def _convglu_fwd(hg, hu, cwg, cwu, cbg, cbu, name):
    S = hg.shape[0]
    hb = ROWS // HALO
    main = pl.BlockSpec((ROWS, CONV_COLS), lambda c, i: (i, c))
    prev = pl.BlockSpec((HALO, CONV_COLS), lambda c, i: (jnp.maximum(i * hb - 1, 0), c))
    wspec = pl.BlockSpec((3, CONV_COLS), lambda c, i: (0, c))
    bspec = pl.BlockSpec((1, CONV_COLS), lambda c, i: (0, c))

    def body(hg_ref, hgp_ref, hu_ref, hup_ref, cwg_ref, cwu_ref, cbg_ref, cbu_ref, o_ref):
        i = pl.program_id(1)
        keep = (i > 0).astype(F32)

        def conv(h_ref, hp_ref, cw_ref, cb_ref):
            xe = jnp.concatenate([hp_ref[...].astype(F32) * keep, h_ref[...].astype(F32)], axis=0)
            return _conv_taps(xe, cw_ref[...], cb_ref[...])[HALO:, :]

        gate = conv(hg_ref, hgp_ref, cwg_ref, cbg_ref)
        up = conv(hu_ref, hup_ref, cwu_ref, cbu_ref)
        o_ref[...] = (gate * _sigmoid(gate) * up).astype(BF16)

    return pl.pallas_call(
        body, name=name, grid=(D_FF // CONV_COLS, S // ROWS),
        in_specs=[main, prev, main, prev, wspec, wspec, bspec, bspec], out_specs=main,
        out_shape=jax.ShapeDtypeStruct((S, D_FF), BF16),
        compiler_params=_cparams(("parallel", "parallel")),
    )(hg, hg, hu, hu, cwg, cwu, cbg, cbu)


def _convglu_bwd(hg, hu, dact, cwg, cwu, cbg, cbu, name):
    S = hg.shape[0]
    nt = S // ROWS
    hb = ROWS // HALO
    main = pl.BlockSpec((ROWS, CONV_COLS), lambda c, i: (i, c))
    prev = pl.BlockSpec((HALO, CONV_COLS), lambda c, i: (jnp.maximum(i * hb - 1, 0), c))
    nxt = pl.BlockSpec((HALO, CONV_COLS), lambda c, i: (jnp.minimum((i + 1) * hb, nt * hb - 1), c))
    wspec = pl.BlockSpec((3, CONV_COLS), lambda c, i: (0, c))
    bspec = pl.BlockSpec((1, CONV_COLS), lambda c, i: (0, c))

    def body(hg_ref, hgp_ref, hgn_ref, hu_ref, hup_ref, hun_ref, da_ref, dan_ref,
             cwg_ref, cwu_ref, cbg_ref, cbu_ref,
             dhg_ref, dhu_ref, dcwg_ref, dcwu_ref, dcbg_ref, dcbu_ref):
        i = pl.program_id(1)
        kp = (i > 0).astype(F32)
        kn = (i < nt - 1).astype(F32)

        def ext(h_ref, hp_ref, hn_ref):
            return jnp.concatenate([hp_ref[...].astype(F32) * kp, h_ref[...].astype(F32),
                                    hn_ref[...].astype(F32) * kn], axis=0)

        xg = ext(hg_ref, hgp_ref, hgn_ref)
        xu = ext(hu_ref, hup_ref, hun_ref)
        gate = _conv_taps(xg, cwg_ref[...], cbg_ref[...])[HALO:, :]
        up = _conv_taps(xu, cwu_ref[...], cbu_ref[...])[HALO:, :]
        da = jnp.concatenate([da_ref[...].astype(F32), dan_ref[...].astype(F32) * kn], axis=0)
        sg = _sigmoid(gate)
        dgate = da * up * sg * (1.0 + gate * (1.0 - sg))
        dup = da * gate * sg
        n_ext = ROWS + HALO

        @pl.when(i == 0)
        def _():
            for ref in (dcwg_ref, dcwu_ref, dcbg_ref, dcbu_ref):
                ref[...] = jnp.zeros_like(ref)

        for d, cw_ref, xe, dh_ref, dcw_ref, dcb_ref in ((dgate, cwg_ref, xg, dhg_ref, dcwg_ref, dcbg_ref),
                                                       (dup, cwu_ref, xu, dhu_ref, dcwu_ref, dcbu_ref)):
            cw = cw_ref[...]
            dh = cw[2:3] * d + cw[1:2] * pltpu.roll(d, n_ext - 1, 0) + cw[0:1] * pltpu.roll(d, n_ext - 2, 0)
            dh_ref[...] = dh[:ROWS, :].astype(BF16)
            dc = d[:ROWS, :]
            for t in range(3):
                shifted = xe if t == 2 else pltpu.roll(xe, 2 - t, 0)
                dcw_ref[t:t + 1, :] += jnp.sum(dc * shifted[HALO:HALO + ROWS, :], axis=0, keepdims=True)
            dcb_ref[...] += jnp.sum(dc, axis=0, keepdims=True)

    return pl.pallas_call(
        body, name=name, grid=(D_FF // CONV_COLS, nt),
        in_specs=[main, prev, nxt, main, prev, nxt, main, nxt, wspec, wspec, bspec, bspec],
        out_specs=[main, main, wspec, wspec, bspec, bspec],
        out_shape=[jax.ShapeDtypeStruct((S, D_FF), BF16)] * 2 + [jax.ShapeDtypeStruct((3, D_FF), F32)] * 2
        + [jax.ShapeDtypeStruct((1, D_FF), F32)] * 2,
        compiler_params=_cparams(("parallel", "arbitrary")),
    )(hg, hg, hg, hu, hu, hu, dact, dact, cwg, cwu, cbg, cbu)


def _bias_index():
    qi = lax.broadcasted_iota(jnp.int32, (BQ, KWIN), 0)
    kj = lax.broadcasted_iota(jnp.int32, (BQ, KWIN), 1)
    idx = jnp.clip(qi - kj + BAND, -MAX_REL, MAX_REL) + MAX_REL
    qc = qi // CHUNK
    kc = kj // CHUNK - LEFT_CHUNKS
    valid = (kc <= qc) & (kc >= qc - LEFT_CHUNKS)
    return idx, valid


IDX_LO = MAX_REL - (CHUNK - 1)


def _bias_table(rel_bias, name):
    def body(rb_ref, o_ref):
        h = pl.program_id(0)
        idx, valid = _bias_index()

        def step(t, acc):
            return jnp.where(idx == t, rb_ref[h, t], acc)

        acc = lax.fori_loop(IDX_LO, N_REL, step, jnp.zeros((BQ, KWIN), F32))
        o_ref[0] = jnp.where(valid, acc, NEG)

    return pl.pallas_call(
        body, name=name, grid=(N_HEADS,),
        in_specs=[pl.BlockSpec(memory_space=pltpu.SMEM)],
        out_specs=pl.BlockSpec((1, BQ, KWIN), lambda h: (h, 0, 0)),
        out_shape=jax.ShapeDtypeStruct((N_HEADS, BQ, KWIN), F32),
        compiler_params=_cparams(("parallel",)),
    )(rel_bias)


REL_PAD = 384


def _bias_table_bwd(dtab, name):
    def body(d_ref, o_ref):
        idx, _ = _bias_index()
        dv = d_ref[0]
        lane = lax.broadcasted_iota(jnp.int32, (1, REL_PAD), 1)

        def step(t, acc):
            s = jnp.sum(jnp.where(idx == t, dv, 0.0))
            return jnp.where(lane == t, s, acc)

        o_ref[0] = lax.fori_loop(IDX_LO, N_REL, step, jnp.zeros((1, REL_PAD), F32))

    return pl.pallas_call(
        body, name=name, grid=(N_HEADS,),
        in_specs=[pl.BlockSpec((1, BQ, KWIN), lambda h: (h, 0, 0))],
        out_specs=pl.BlockSpec((1, 1, REL_PAD), lambda h: (h, 0, 0)),
        out_shape=jax.ShapeDtypeStruct((N_HEADS, 1, REL_PAD), F32),
        compiler_params=_cparams(("parallel",)),
    )(dtab)


def _head_masks():
    lane = lax.broadcasted_iota(jnp.int32, (1, PAIR), 1)
    return [lane // HEAD_DIM == h for h in range(2)]


def _ca_window_specs(nq):
    return [pl.BlockSpec((BQ, PAIR), functools.partial(
        lambda p, i, d: (jnp.clip(i - 2 + d, 0, nq - 1), p), d=d)) for d in range(3)]


def _ca_scores(qm, kc, tab_h, i):
    col = lax.broadcasted_iota(jnp.int32, (1, KWIN), 1)
    in_seq = col + (i - 2) * BQ >= 0
    return jnp.where(in_seq, _dot(qm, kc, NT) + tab_h, NEG)


def _ca_fwd(qn, kn, v, tab, name):
    S = qn.shape[0]
    nq = S // BQ
    qspec = pl.BlockSpec((BQ, PAIR), lambda p, i: (i, p))
    tspec = pl.BlockSpec((2, BQ, KWIN), lambda p, i: (p, 0, 0))

    def body(q_ref, k0, k1, k2, v0, v1, v2, tab_ref, o_ref):
        i = pl.program_id(1)
        kc = jnp.concatenate([k0[...], k1[...], k2[...]], axis=0)
        vc = jnp.concatenate([v0[...], v1[...], v2[...]], axis=0)
        qv = q_ref[...]
        acc = jnp.zeros((BQ, PAIR), F32)
        for h, m in enumerate(_head_masks()):
            s = _ca_scores(jnp.where(m, qv, 0), kc, tab_ref[h], i)
            p = jnp.exp(s - jnp.max(s, axis=-1, keepdims=True))
            l = jnp.sum(p, axis=-1, keepdims=True)
            acc = acc + _dot(p.astype(BF16), jnp.where(m, vc, 0), NN) / l
        o_ref[...] = acc.astype(BF16)

    win = _ca_window_specs(nq)
    return pl.pallas_call(
        body, name=name, grid=(WIDTH // PAIR, nq),
        in_specs=[qspec] + win + win + [tspec], out_specs=qspec,
        out_shape=jax.ShapeDtypeStruct((S, WIDTH), BF16),
        compiler_params=_cparams(("parallel", "parallel")),
    )(qn, kn, kn, kn, v, v, v, tab)


def _ca_bwd(qn, kn, v, do, tab, name):
    S = qn.shape[0]
    nq = S // BQ
    qspec = pl.BlockSpec((BQ, PAIR), lambda p, i: (jnp.minimum(i, nq - 1), p))
    kout = pl.BlockSpec((BQ, PAIR), lambda p, i: (jnp.clip(i - 2, 0, nq - 1), p))
    tspec = pl.BlockSpec((2, BQ, KWIN), lambda p, i: (p, 0, 0))

    def body(q_ref, do_ref, k0, k1, k2, v0, v1, v2, tab_ref,
             dq_ref, dk_ref, dv_ref, dtab_ref, dk_acc, dv_acc):
        i = pl.program_id(1)

        @pl.when(i == 0)
        def _():
            dk_acc[...] = jnp.zeros_like(dk_acc)
            dv_acc[...] = jnp.zeros_like(dv_acc)
            dtab_ref[...] = jnp.zeros_like(dtab_ref)

        @pl.when(i < nq)
        def _():
            kc = jnp.concatenate([k0[...], k1[...], k2[...]], axis=0)
            vc = jnp.concatenate([v0[...], v1[...], v2[...]], axis=0)
            qv, dov = q_ref[...], do_ref[...]
            dq = jnp.zeros((BQ, PAIR), F32)
            dkc = jnp.zeros((KWIN, PAIR), F32)
            dvc = jnp.zeros((KWIN, PAIR), F32)
            for h, m in enumerate(_head_masks()):
                qm, dom = jnp.where(m, qv, 0), jnp.where(m, dov, 0)
                s = _ca_scores(qm, kc, tab_ref[h], i)
                p = jnp.exp(s - jnp.max(s, axis=-1, keepdims=True))
                p = p / jnp.sum(p, axis=-1, keepdims=True)
                dp = _dot(dom, vc, NT)
                ds = p * (dp - jnp.sum(p * dp, axis=-1, keepdims=True))
                dtab_ref[h] += ds
                dsb, pb = ds.astype(BF16), p.astype(BF16)
                dq = dq + _dot(dsb, jnp.where(m, kc, 0), NN)
                dkc = dkc + _dot(dsb, qm, TN)
                dvc = dvc + _dot(pb, dom, TN)
            dq_ref[...] = dq
            for d in range(3):
                slot = (i + 1 + d) % 3
                dk_acc[slot] += dkc[d * BQ:(d + 1) * BQ]
                dv_acc[slot] += dvc[d * BQ:(d + 1) * BQ]

        @pl.when(i >= 2)
        def _():
            slot = (i + 1) % 3
            dk_ref[...] = dk_acc[slot]
            dv_ref[...] = dv_acc[slot].astype(BF16)
            dk_acc[slot] = jnp.zeros((BQ, PAIR), F32)
            dv_acc[slot] = jnp.zeros((BQ, PAIR), F32)

    win = _ca_window_specs(nq)
    return pl.pallas_call(
        body, name=name, grid=(WIDTH // PAIR, nq + 2),
        in_specs=[qspec, qspec] + win + win + [tspec],
        out_specs=[qspec, kout, kout, tspec],
        out_shape=[jax.ShapeDtypeStruct((S, WIDTH), F32), jax.ShapeDtypeStruct((S, WIDTH), F32),
                   jax.ShapeDtypeStruct((S, WIDTH), BF16), jax.ShapeDtypeStruct((N_HEADS, BQ, KWIN), F32)],
        scratch_shapes=[pltpu.VMEM((3, BQ, PAIR), F32)] * 2,
        compiler_params=_cparams(("parallel", "arbitrary")),
    )(qn, do, kn, kn, kn, v, v, v, tab)


def _sb_consts():
    r = lax.broadcasted_iota(jnp.int32, (BQ, BQ), 0)
    c = lax.broadcasted_iota(jnp.int32, (BQ, BQ), 1)
    from_s = jnp.where(r >= c, 1.0, 0.0).astype(BF16)
    causal = c < r
    return from_s, causal


def _suffix_sum(t, from_s):
    hi, lo = _split_bf16(t)
    return _dot(hi, from_s, NN) + _dot(lo, from_s, NN)


def _neg_abs(x):
    bits = lax.bitcast_convert_type(x, jnp.uint32) | jnp.uint32(0x80000000)
    return lax.bitcast_convert_type(bits, F32)


def _sb_log_keep(zn):
    return jnp.minimum(zn, 0.0) - jnp.log(1.0 + jnp.exp(_neg_abs(zn)))


SB_DEAD = 105.0


def _sb_walk(i, tiles, keep_ref):
    tiles([i], True)

    def alive():
        return (jnp.max(keep_ref[...]) > -SB_DEAD).astype(jnp.int32)

    def step(state):
        j, _ = state
        tiles([j], False)
        return j - 1, alive()

    lax.while_loop(lambda state: (state[0] >= 0) & (state[1] > 0), step, (i - 1, alive()))


def _sb_rows(j):
    return pl.ds(pl.multiple_of(j * BQ, BQ), BQ)


def _sb_fwd(q, k, v, name):
    S = q.shape[0]
    nq = S // BQ
    qspec = pl.BlockSpec((BQ, PAIR), lambda p, i: (i, p))
    kspec = pl.BlockSpec((S, PAIR), lambda p, i: (0, p))

    def body(q_ref, k_ref, v_ref, o_ref, of_ref, carry_ref, acc_ref):
        i = pl.program_id(1)
        from_s, causal = _sb_consts()
        masks = _head_masks()
        qn = q_ref[...] * -(HEAD_DIM ** -0.5)
        qms = [jnp.where(m, qn, 0) for m in masks]
        carry_ref[...] = jnp.zeros_like(carry_ref)
        acc_ref[...] = jnp.zeros_like(acc_ref)

        def tiles(js, diag):
            chains = [(n, h) for n in range(len(js)) for h in range(2)]
            kbs = [k_ref[_sb_rows(j), :] for j in js]
            vbs = [v_ref[_sb_rows(j), :] for j in js]
            zn = {c: _dot(qms[c[1]], kbs[c[0]], NT) for c in chains}
            log_keep = {c: _sb_log_keep(zn[c]) for c in chains}
            if diag:
                log_keep = {c: jnp.where(causal, log_keep[c], 0.0) for c in chains}
            split = {c: _split_bf16(log_keep[c]) for c in chains}
            carry = {}
            for h in range(2):
                run = carry_ref[h]
                for n in range(len(js)):
                    carry[(n, h)] = run
                    run = run + jnp.sum(log_keep[(n, h)], axis=-1, keepdims=True)
                carry_ref[h] = run
            suffix = {c: _dot(split[c][0], from_s, NN) + _dot(split[c][1], from_s, NN) for c in chains}
            w = {c: jnp.exp(carry[c] + suffix[c] - zn[c]) for c in chains}
            if diag:
                w = {c: jnp.where(causal, w[c], 0.0) for c in chains}
            for c in chains:
                acc_ref[c[1]] += _dot(w[c].astype(BF16), vbs[c[0]], NN)

        _sb_walk(i, tiles, carry_ref)
        out = jnp.where(masks[0], acc_ref[0], acc_ref[1])
        o_ref[...] = out.astype(BF16)
        of_ref[...] = out

    return pl.pallas_call(
        body, name=name, grid=(WIDTH // PAIR, nq),
        in_specs=[qspec, kspec, kspec], out_specs=[qspec, qspec],
        out_shape=[jax.ShapeDtypeStruct((S, WIDTH), BF16), jax.ShapeDtypeStruct((S, WIDTH), F32)],
        scratch_shapes=[pltpu.VMEM((2, BQ, 1), F32), pltpu.VMEM((2, BQ, PAIR), F32)],
        compiler_params=_cparams(("parallel", "arbitrary")),
    )(q, k, v)


def _sb_bwd(q, k, v, o, do, name):
    S = q.shape[0]
    nq = S // BQ
    qspec = pl.BlockSpec((BQ, PAIR), lambda p, i: (i, p))
    kspec = pl.BlockSpec((S, PAIR), lambda p, i: (0, p))

    def body(q_ref, o_ref, do_ref, k_ref, v_ref, dq_ref, dk_ref, dv_ref, dk_acc, dv_acc, keep_ref, gsum_ref, dq_acc):
        i = pl.program_id(1)

        @pl.when(i == 0)
        def _():
            dk_acc[...] = jnp.zeros_like(dk_acc)
            dv_acc[...] = jnp.zeros_like(dv_acc)

        from_s, causal = _sb_consts()
        masks = _head_masks()
        qn, dov = q_ref[...] * -(HEAD_DIM ** -0.5), do_ref[...]
        od = o_ref[...] * dov.astype(F32)
        qms = [jnp.where(m, qn, 0) for m in masks]
        doms = [jnp.where(m, dov, 0) for m in masks]
        totals = [jnp.sum(jnp.where(m, od, 0.0), axis=-1, keepdims=True) for m in masks]
        for ref in (keep_ref, gsum_ref, dq_acc):
            ref[...] = jnp.zeros_like(ref)

        def running(ref, vals, n_blocks):
            before_chain = {}
            for h in range(2):
                run = ref[h]
                for n in range(n_blocks):
                    before_chain[(n, h)] = run
                    run = run + jnp.sum(vals[(n, h)], axis=-1, keepdims=True)
                ref[h] = run
            return before_chain

        def tiles(js, diag):
            chains = [(n, h) for n in range(len(js)) for h in range(2)]
            kbs = [k_ref[_sb_rows(j), :] for j in js]
            vbs = [v_ref[_sb_rows(j), :] for j in js]
            zn = {c: _dot(qms[c[1]], kbs[c[0]], NT) for c in chains}
            dw = {c: _dot(doms[c[1]], vbs[c[0]], NT) for c in chains}
            log_keep = {c: _sb_log_keep(zn[c]) for c in chains}
            if diag:
                log_keep = {c: jnp.where(causal, log_keep[c], 0.0) for c in chains}
            split = {c: _split_bf16(log_keep[c]) for c in chains}
            kept = running(keep_ref, log_keep, len(js))
            suffix = {c: _dot(split[c][0], from_s, NN) + _dot(split[c][1], from_s, NN) for c in chains}
            w = {c: jnp.exp(kept[c] + suffix[c] - zn[c]) for c in chains}
            if diag:
                w = {c: jnp.where(causal, w[c], 0.0) for c in chains}
            wb = {c: w[c].astype(BF16) for c in chains}
            g = {c: wb[c].astype(F32) * dw[c] for c in chains}
            gsplit = {c: _split_bf16(g[c]) for c in chains}
            gsum = running(gsum_ref, g, len(js))
            gsuffix = {c: _dot(gsplit[c][0], from_s, NN) + _dot(gsplit[c][1], from_s, NN) for c in chains}
            dzb = {}
            for c in chains:
                before = totals[c[1]] - (gsum[c] + gsuffix[c])
                dz = (g[c] + before) * jnp.exp(log_keep[c]) - before
                if diag:
                    dz = jnp.where(causal, dz, 0.0)
                dzb[c] = dz.astype(BF16)
            for c in chains:
                rows = _sb_rows(js[c[0]])
                dq_acc[c[1]] += _dot(dzb[c], kbs[c[0]], NN)
                dk_acc[rows, :] -= _dot(dzb[c], qms[c[1]], TN)
                dv_acc[rows, :] += _dot(wb[c], doms[c[1]], TN)

        _sb_walk(i, tiles, keep_ref)
        dq_ref[...] = (jnp.where(masks[0], dq_acc[0], dq_acc[1]) * HEAD_DIM ** -0.5).astype(BF16)

        @pl.when(i == nq - 1)
        def _():
            dk_ref[...] = dk_acc[...].astype(BF16)
            dv_ref[...] = dv_acc[...].astype(BF16)

    return pl.pallas_call(
        body, name=name, grid=(WIDTH // PAIR, nq),
        in_specs=[qspec, qspec, qspec, kspec, kspec], out_specs=[qspec, kspec, kspec],
        out_shape=[jax.ShapeDtypeStruct((S, WIDTH), BF16)] * 3,
        scratch_shapes=[pltpu.VMEM((S, PAIR), F32)] * 2 + [pltpu.VMEM((2, BQ, 1), F32)] * 2
        + [pltpu.VMEM((2, BQ, PAIR), F32)],
        compiler_params=_cparams(("parallel", "arbitrary")),
    )(q, o, do, k, v)


ANY = pl.BlockSpec(memory_space=pl.ANY)


def _place():
    return lax.axis_index("x"), lax.axis_index("y"), lax.axis_index("c")


def _other_chips(x, y):
    return [(2 * px + py, (px, py)) for px, py in ((1 - x, y), (x, 1 - y), (1 - x, 1 - y))]


def _remote(src, dst, sems, k, to):
    return pltpu.make_async_remote_copy(src_ref=src, dst_ref=dst, send_sem=sems[0].at[k], recv_sem=sems[1].at[k],
                                        device_id=to, device_id_type=MESH)


def _gather_weights(wp, name):
    R = wp.shape[0]
    Rh = R // 2

    def body(wp_ref, out_ref, send_sems, recv_sems, local_sem):
        x, y, c = _place()
        me = 2 * x + y
        sems = (send_sems, recv_sems)
        half = pl.ds(c * Rh, Rh)
        mine = pltpu.make_async_copy(wp_ref, out_ref.at[me], local_sem)
        mine.start()
        chips = _other_chips(x, y)
        first = [_remote(wp_ref.at[half], out_ref.at[me, half], sems, k, (*xy, c)) for k, (_, xy) in enumerate(chips)]
        for cp in first:
            cp.start()
        passed = []
        for k, (chip, xy) in enumerate(chips):
            landed = out_ref.at[chip, half]
            _remote(landed, landed, sems, k, (*xy, c)).wait_recv()
            cp = _remote(landed, landed, sems, 3 + k, (x, y, 1 - c))
            cp.start()
            passed.append(cp)
        other = pl.ds((1 - c) * Rh, Rh)
        for k, (chip, xy) in enumerate(chips):
            landed = out_ref.at[chip, other]
            _remote(landed, landed, sems, 3 + k, (x, y, 1 - c)).wait_recv()
        for cp in first + passed:
            cp.wait_send()
        mine.wait()

    return pl.pallas_call(
        body, name=name, in_specs=[ANY], out_specs=ANY,
        out_shape=jax.ShapeDtypeStruct((N_CHIPS, R, PACK_COLS), wp.dtype),
        scratch_shapes=[pltpu.SemaphoreType.DMA((6,)), pltpu.SemaphoreType.DMA((6,)), pltpu.SemaphoreType.DMA(())],
    )(wp)


def _exchange_cores(g, small, name):
    R = g.shape[1]
    Rh = R // 2

    def body(g_ref, small_ref, mine_ref, sib_ref, all_ref, send_sems, recv_sems, local_sems):
        x, y, c = _place()
        me = 4 * x + 2 * y + c
        sems = (send_sems, recv_sems)
        keep = pltpu.make_async_copy(g_ref.at[:, pl.ds(c * Rh, Rh), :], mine_ref, local_sems.at[0])
        own = pltpu.make_async_copy(small_ref, all_ref.at[me], local_sems.at[1])
        keep.start()
        own.start()
        copies = [_remote(g_ref.at[:, pl.ds((1 - c) * Rh, Rh), :], sib_ref, sems, 0, (x, y, 1 - c))]
        k = 1
        for fx in (0, 1):
            for fy in (0, 1):
                for fc in (0, 1):
                    if fx or fy or fc:
                        to = (1 - x if fx else x, 1 - y if fy else y, 1 - c if fc else c)
                        copies.append(_remote(small_ref, all_ref.at[me], sems, k, to))
                        k += 1
        for cp in copies:
            cp.start()
        for cp in copies:
            cp.wait_recv()
        for cp in copies:
            cp.wait_send()
        keep.wait()
        own.wait()

    return pl.pallas_call(
        body, name=name, in_specs=[ANY, ANY], out_specs=[ANY, ANY, ANY],
        out_shape=[jax.ShapeDtypeStruct((N_CHIPS, Rh, PACK_COLS), F32)] * 2
        + [jax.ShapeDtypeStruct((N_DEV, SMALL_ROWS, PACK_COLS), F32)],
        scratch_shapes=[pltpu.SemaphoreType.DMA((8,)), pltpu.SemaphoreType.DMA((8,)), pltpu.SemaphoreType.DMA((2,))],
    )(g, small)


def _exchange_chips(p, name):
    def body(p_ref, out_ref, send_sems, recv_sems, local_sem):
        x, y, c = _place()
        me = 2 * x + y
        sems = (send_sems, recv_sems)
        own = pltpu.make_async_copy(p_ref.at[me], out_ref.at[me], local_sem)
        own.start()
        copies = [_remote(p_ref.at[chip], out_ref.at[me], sems, k, (*xy, c))
                  for k, (chip, xy) in enumerate(_other_chips(x, y))]
        for cp in copies:
            cp.start()
        for k, (chip, xy) in enumerate(_other_chips(x, y)):
            _remote(p_ref.at[chip], out_ref.at[chip], sems, k, (*xy, c)).wait_recv()
        for cp in copies:
            cp.wait_send()
        own.wait()

    return pl.pallas_call(
        body, name=name, in_specs=[ANY], out_specs=ANY,
        out_shape=jax.ShapeDtypeStruct(p.shape, p.dtype),
        scratch_shapes=[pltpu.SemaphoreType.DMA((3,)), pltpu.SemaphoreType.DMA((3,)), pltpu.SemaphoreType.DMA(())],
    )(p)


def _share_halves(gh, name):
    Rh = gh.shape[0]

    def body(gh_ref, out_ref, send_sems, recv_sems, local_sem):
        x, y, c = _place()
        sems = (send_sems, recv_sems)
        own = pltpu.make_async_copy(gh_ref, out_ref.at[pl.ds(c * Rh, Rh)], local_sem)
        own.start()
        cp = _remote(gh_ref, out_ref.at[pl.ds(c * Rh, Rh)], sems, 0, (x, y, 1 - c))
        cp.start()
        other = out_ref.at[pl.ds((1 - c) * Rh, Rh)]
        _remote(other, other, sems, 0, (x, y, 1 - c)).wait_recv()
        cp.wait_send()
        own.wait()

    return pl.pallas_call(
        body, name=name, in_specs=[ANY], out_specs=ANY,
        out_shape=jax.ShapeDtypeStruct((2 * Rh, PACK_COLS), gh.dtype),
        scratch_shapes=[pltpu.SemaphoreType.DMA((1,)), pltpu.SemaphoreType.DMA((1,)), pltpu.SemaphoreType.DMA(())],
    )(gh)


PACK_ROWS_CAP = 128


def _row_block(rows):
    return max(b for b in range(8, PACK_ROWS_CAP + 1, 8) if rows % b == 0)


def _add2(a, b, name):
    n, Rh, C = a.shape
    rows = _row_block(Rh)
    spec = pl.BlockSpec((1, rows, C), lambda j, i: (j, i, 0))

    def body(a_ref, b_ref, o_ref):
        o_ref[...] = a_ref[...] + b_ref[...]

    return pl.pallas_call(
        body, name=name, grid=(n, Rh // rows), in_specs=[spec, spec], out_specs=spec,
        out_shape=jax.ShapeDtypeStruct(a.shape, F32),
        compiler_params=_cparams(("parallel", "parallel")),
    )(a, b)


def _sum_leading(a, name):
    n, R, C = a.shape
    rows = _row_block(R)

    def body(a_ref, o_ref):
        acc = a_ref[0]
        for j in range(1, n):
            acc = acc + a_ref[j]
        o_ref[...] = acc

    return pl.pallas_call(
        body, name=name, grid=(R // rows,),
        in_specs=[pl.BlockSpec((n, rows, C), lambda i: (0, i, 0))],
        out_specs=pl.BlockSpec((rows, C), lambda i: (i, 0)),
        out_shape=jax.ShapeDtypeStruct((R, C), F32),
        compiler_params=_cparams(("parallel",)),
    )(a)


def _adamw(w, g, m, v, name):
    R, C = w.shape
    rows = _row_block(R)
    spec = pl.BlockSpec((rows, C), lambda i: (i, 0))

    def body(w_ref, g_ref, m_ref, v_ref, d_ref, mo_ref, vo_ref):
        gv = g_ref[...]
        mn = ADAM_B1 * m_ref[...] + (1.0 - ADAM_B1) * gv
        vn = ADAM_B2 * v_ref[...] + (1.0 - ADAM_B2) * (gv * gv)
        m_hat = mn / (1.0 - ADAM_B1 ** ADAM_STEP)
        v_hat = vn / (1.0 - ADAM_B2 ** ADAM_STEP)
        d_ref[...] = -ADAM_LR * (m_hat / (jnp.sqrt(v_hat) + ADAM_EPS) + ADAM_WD * w_ref[...])
        mo_ref[...] = mn
        vo_ref[...] = vn

    return pl.pallas_call(
        body, name=name, grid=(R // rows,), in_specs=[spec] * 4, out_specs=[spec] * 3,
        out_shape=[jax.ShapeDtypeStruct((R, C), F32)] * 3,
        compiler_params=_cparams(("parallel",)),
    )(w, g, m, v)


BIG = ("w_in", "w_branch_a", "w_branch_b", "w_out", "w_ffn_up", "w_ffn_down", "ffn_conv_w")
BIG_SHARD = {"w_in": (D_MODEL, 1280), "w_branch_a": (WIDTH, 256), "w_branch_b": (WIDTH, 256),
             "w_out": (256, D_MODEL), "w_ffn_up": (D_MODEL, 1408), "w_ffn_down": (704, D_MODEL),
             "ffn_conv_w": (3, 1408)}
BIG_COL_SHARDED = {"w_in": True, "w_branch_a": True, "w_branch_b": True, "w_out": False,
                   "w_ffn_up": True, "w_ffn_down": False, "ffn_conv_w": True}
CONV_W_ROWS = 32


def _pack_rows(name):
    r, c = BIG_SHARD[name]
    return CONV_W_ROWS if name == "ffn_conv_w" else r * c // PACK_COLS


def _rows_of(flat, rows):
    return jnp.pad(flat, (0, rows * PACK_COLS - flat.shape[0])).reshape(rows, PACK_COLS)


def _pack_shards(shards, dtype=F32):
    parts = []
    for n in BIG:
        if n == "ffn_conv_w":
            flat = shards[n].reshape(-1)
            if dtype == BF16:
                flat = lax.bitcast_convert_type(flat, BF16).reshape(-1)
            parts.append(_rows_of(flat, CONV_W_ROWS))
        else:
            parts.append(shards[n].reshape(_pack_rows(n), PACK_COLS).astype(dtype))
    return jnp.concatenate(parts, axis=0)


def _unpack_shards(packed):
    out, r0 = {}, 0
    for n in BIG:
        r, c = BIG_SHARD[n]
        out[n] = packed[r0:r0 + _pack_rows(n)].reshape(-1)[:r * c].reshape(r, c)
        r0 += _pack_rows(n)
    return out


def _unpack_full(gathered):
    out, r0 = {}, 0
    for n in BIG:
        r, c = BIG_SHARD[n]
        part = gathered[:, r0:r0 + _pack_rows(n)].reshape(N_CHIPS, -1)
        if n == "ffn_conv_w":
            part = lax.bitcast_convert_type(part[:, :2 * r * c].reshape(N_CHIPS, r * c, 2), F32)
        parts = part[:, :r * c].reshape(N_CHIPS, r, c)
        out[n] = jnp.concatenate(list(parts), axis=1 if BIG_COL_SHARDED[n] else 0)
        r0 += _pack_rows(n)
    return out


def _pack_full(full):
    slabs = []
    for j in range(N_CHIPS):
        shards = {}
        for n in BIG:
            r, c = BIG_SHARD[n]
            shards[n] = full[n][:, j * c:(j + 1) * c] if BIG_COL_SHARDED[n] else full[n][j * r:(j + 1) * r]
        slabs.append(_pack_shards(shards))
    return jnp.stack(slabs)


SMALL = (("norm1_g", D_MODEL), ("q_norm_g", HEAD_DIM), ("k_norm_g", HEAD_DIM), ("rel_bias", N_HEADS * N_REL),
         ("norm2_g", D_MODEL), ("ffn_conv_b", 2 * D_FF))


def _pack_small(vals):
    return _rows_of(jnp.concatenate([vals[n].reshape(-1) for n, _ in SMALL]), SMALL_ROWS)


def _unpack_small(packed):
    flat, out, o = packed.reshape(-1), {}, 0
    for n, sz in SMALL:
        out[n] = flat[o:o + sz]
        o += sz
    return out


def kernel(x, norm1_g, w_in, q_norm_g, k_norm_g, rel_bias, w_branch_a, w_branch_b, w_out, norm2_g, w_ffn_up, ffn_conv_w, ffn_conv_b, w_ffn_down, loss_target, m_norm1_g, m_w_in, m_q_norm_g, m_k_norm_g, m_rel_bias, m_w_branch_a, m_w_branch_b, m_w_out, m_norm2_g, m_w_ffn_up, m_ffn_conv_w, m_ffn_conv_b, m_w_ffn_down, v_norm1_g, v_w_in, v_q_norm_g, v_k_norm_g, v_rel_bias, v_w_branch_a, v_w_branch_b, v_w_out, v_norm2_g, v_w_ffn_up, v_ffn_conv_w, v_ffn_conv_b, v_w_ffn_down):
    w_big = {"w_in": w_in[0], "w_branch_a": w_branch_a[0], "w_branch_b": w_branch_b[0], "w_out": w_out[0],
             "w_ffn_up": w_ffn_up[0], "w_ffn_down": w_ffn_down[0], "ffn_conv_w": ffn_conv_w[0]}
    m_big = {"w_in": m_w_in[0], "w_branch_a": m_w_branch_a[0], "w_branch_b": m_w_branch_b[0], "w_out": m_w_out[0],
             "w_ffn_up": m_w_ffn_up[0], "w_ffn_down": m_w_ffn_down[0], "ffn_conv_w": m_ffn_conv_w[0]}
    v_big = {"w_in": v_w_in[0], "w_branch_a": v_w_branch_a[0], "w_branch_b": v_w_branch_b[0], "w_out": v_w_out[0],
             "w_ffn_up": v_w_ffn_up[0], "w_ffn_down": v_w_ffn_down[0], "ffn_conv_w": v_ffn_conv_w[0]}
    xs, tgt = x[0], loss_target[0]

    full = _unpack_full(_gather_weights(_pack_shards(w_big, BF16), "gather_weights"))
    w_in_f, w_a, w_b, w_o = full["w_in"], full["w_branch_a"], full["w_branch_b"], full["w_out"]
    w_up, w_dn = full["w_ffn_up"], full["w_ffn_down"]
    w_up_g, w_up_u = w_up[:, :D_FF], w_up[:, D_FF:]
    cwg, cwu = full["ffn_conv_w"][:, :D_FF], full["ffn_conv_w"][:, D_FF:]
    cbg, cbu = ffn_conv_b[:, :D_FF], ffn_conv_b[:, D_FF:]

    hn = _rms_fwd(xs, norm1_g, "rms1")
    qk = _matmul(hn, w_in_f[:, :2 * WIDTH], "nn", F32, "proj_qk")
    vqkv = _matmul(hn, w_in_f[:, 2 * WIDTH:6 * WIDTH], "nn", BF16, "proj_vqkv")
    g_a = _matmul(hn, w_in_f[:, 6 * WIDTH:6 * WIDTH + D_MODEL], "nn", F32, "proj_gate_a")
    g_b = _matmul(hn, w_in_f[:, 6 * WIDTH + D_MODEL:], "nn", F32, "proj_gate_b")
    gq = jnp.tile(q_norm_g, (1, N_HEADS))
    gk = jnp.tile(k_norm_g, (1, N_HEADS))
    qa, ka = _qknorm_fwd(qk, gq, gk, "qknorm")
    va, qb, kb, vb = (vqkv[:, k * WIDTH:(k + 1) * WIDTH] for k in range(4))
    tab = _bias_table(rel_bias[0], "bias_table")
    out_a = _ca_fwd(qa, ka, va, tab, "chunk_attn")
    out_b, out_b_f32 = _sb_fwd(qb, kb, vb, "stick_attn")
    y_a = _matmul(out_a, w_a, "nn", F32, "branch_a")
    y_b = _matmul(out_b, w_b, "nn", F32, "branch_b")
    mixed = _mix_fwd(g_a, g_b, y_a, y_b, "mix")
    x2 = _matmul(mixed, w_o, "nn", F32, "out_proj", residual=xs)
    hn2 = _rms_fwd(x2, norm2_g, "rms2")
    hid_g = _matmul(hn2, w_up_g, "nn", BF16, "ffn_up_gate")
    hid_u = _matmul(hn2, w_up_u, "nn", BF16, "ffn_up_up")
    act = _convglu_fwd(hid_g, hid_u, cwg, cwu, cbg, cbu, "convglu")
    y = _matmul(act, w_dn, "nn", F32, "ffn_down", residual=x2)
    dy, dyb, sq = _loss_head(y, tgt, "loss_head")
    loss = lax.psum(0.5 / D_MODEL * jnp.sum(sq), ("x", "y", "c"))

    dact = _matmul(dyb, w_dn, "nt", BF16, "d_act")
    d_w_dn = _matmul(act, dyb, "tn", F32, "d_w_down")
    dhg, dhu, dcwg, dcwu, dcbg, dcbu = _convglu_bwd(hid_g, hid_u, dact, cwg, cwu, cbg, cbu, "convglu_bwd")
    d_w_up = jnp.concatenate([_matmul(hn2, dhg, "tn", F32, "d_w_up_gate"),
                              _matmul(hn2, dhu, "tn", F32, "d_w_up_up")], axis=1)
    dhn2 = _matmul(dhg, w_up_g, "nt", F32, "d_hn2_gate")
    dhn2 = _matmul(dhu, w_up_u, "nt", F32, "d_hn2_up", residual=dhn2)
    dx2, dx2b, d_norm2 = _rms_bwd(x2, norm2_g, dhn2, dy, "rms2_bwd")
    dmixed = _matmul(dx2b, w_o, "nt", F32, "d_mixed")
    d_w_o = _matmul(mixed, dx2b, "tn", F32, "d_w_out")
    dga, dgb, dya, dyb_b = _mix_bwd(dmixed, g_a, g_b, y_a, y_b, "mix_bwd")
    d_w_a = _matmul(out_a, dya, "tn", F32, "d_w_branch_a")
    d_w_b = _matmul(out_b, dyb_b, "tn", F32, "d_w_branch_b")
    do_a = _matmul(dya, w_a, "nt", BF16, "d_out_a")
    do_b = _matmul(dyb_b, w_b, "nt", BF16, "d_out_b")
    dqb, dkb, dvb = _sb_bwd(qb, kb, vb, out_b_f32, do_b, "stick_attn_bwd")
    dqa_n, dka_n, dva, dtab = _ca_bwd(qa, ka, va, do_a, tab, "chunk_attn_bwd")
    d_rel = _bias_table_bwd(dtab, "bias_table_bwd")[:, 0, :N_REL]
    dqa, dka, dgq, dgk = _qknorm_bwd(qk, gq, gk, dqa_n, dka_n, "qknorm_bwd")
    dproj = jnp.concatenate([dqa, dka, dva, dqb, dkb, dvb, dga, dgb], axis=1)
    d_w_in = _matmul(hn, dproj, "tn", F32, "d_w_in")
    dhn = _matmul(dproj, w_in_f, "nt", F32, "d_hn")
    dx, _, d_norm1 = _rms_bwd(xs, norm1_g, dhn, dx2, "rms1_bwd")

    g_pack = _pack_full({"w_in": d_w_in, "w_branch_a": d_w_a, "w_branch_b": d_w_b, "w_out": d_w_o,
                         "w_ffn_up": d_w_up, "w_ffn_down": d_w_dn,
                         "ffn_conv_w": jnp.concatenate([dcwg, dcwu], axis=1)})
    small_g = _pack_small({"norm1_g": d_norm1, "q_norm_g": dgq.reshape(N_HEADS, HEAD_DIM).sum(0),
                           "k_norm_g": dgk.reshape(N_HEADS, HEAD_DIM).sum(0), "rel_bias": d_rel,
                           "norm2_g": d_norm2, "ffn_conv_b": jnp.concatenate([dcbg, dcbu], axis=1)})
    mine, sib, small_all = _exchange_cores(g_pack, small_g, "exchange_cores")
    chip_part = _add2(mine, sib, "sum_cores")
    parts = _exchange_chips(chip_part, "exchange_chips")
    g_half = _sum_leading(parts, "sum_chips")
    g_shard = _share_halves(g_half, "share_halves")
    small_sum = _sum_leading(small_all, "sum_small")

    d_pack, m_pack, v_pack = _adamw(_pack_shards(w_big), g_shard, _pack_shards(m_big), _pack_shards(v_big),
                                    "adamw_big")
    grads = _unpack_shards(g_shard)
    deltas, new_m, new_v = _unpack_shards(d_pack), _unpack_shards(m_pack), _unpack_shards(v_pack)

    shapes = {"norm1_g": norm1_g.shape, "q_norm_g": q_norm_g.shape, "k_norm_g": k_norm_g.shape,
              "rel_bias": rel_bias.shape, "norm2_g": norm2_g.shape, "ffn_conv_b": ffn_conv_b.shape}
    small_w = {"norm1_g": norm1_g, "q_norm_g": q_norm_g, "k_norm_g": k_norm_g, "rel_bias": rel_bias,
               "norm2_g": norm2_g, "ffn_conv_b": ffn_conv_b}
    small_m = {"norm1_g": m_norm1_g, "q_norm_g": m_q_norm_g, "k_norm_g": m_k_norm_g, "rel_bias": m_rel_bias,
               "norm2_g": m_norm2_g, "ffn_conv_b": m_ffn_conv_b}
    small_v = {"norm1_g": v_norm1_g, "q_norm_g": v_q_norm_g, "k_norm_g": v_k_norm_g, "rel_bias": v_rel_bias,
               "norm2_g": v_norm2_g, "ffn_conv_b": v_ffn_conv_b}
    ds, ms, vs = _adamw(_pack_small(small_w), small_sum, _pack_small(small_m), _pack_small(small_v), "adamw_small")
    small_grads, ds, ms, vs = (_unpack_small(t) for t in (small_sum, ds, ms, vs))

    order = ("norm1_g", "w_in", "q_norm_g", "k_norm_g", "rel_bias", "w_branch_a", "w_branch_b", "w_out",
             "norm2_g", "w_ffn_up", "ffn_conv_w", "ffn_conv_b", "w_ffn_down")
    outs = [loss, dx[None]]
    for big, small in ((grads, small_grads), (deltas, ds), (new_m, ms), (new_v, vs)):
        for n in order:
            outs.append(big[n][None] if n in big else small[n].reshape(shapes[n]))
    return tuple(outs)
```

```python
import functools

import jax
import jax.numpy as jnp
from jax import lax
from jax.experimental import pallas as pl
from jax.experimental.pallas import tpu as pltpu

F32 = jnp.float32
BF16 = jnp.bfloat16
MESH = pl.DeviceIdType.MESH

D_MODEL = 1024
HEAD_DIM = 64
N_HEADS = 8
WIDTH = N_HEADS * HEAD_DIM
CHUNK = 64
LEFT_CHUNKS = 8
MAX_REL = 128
N_REL = 2 * MAX_REL + 1
D_FF = 2816
EPS = 1e-6
NEG = -1e30

ADAM_LR = 0.001
ADAM_B1 = 0.9
ADAM_B2 = 0.999
ADAM_EPS = 1e-08
ADAM_WD = 0.01
ADAM_STEP = 10

N_CHIPS = 4
N_DEV = 8
LANES = 128
PAIR = 2 * HEAD_DIM
BQ = 256
BAND = LEFT_CHUNKS * CHUNK
KWIN = BAND + BQ
VMEM_LIMIT = 56 * 1024 * 1024
PACK_COLS = 1024
SMALL_ROWS = 16

NN = (((1,), (0,)), ((), ()))
NT = (((1,), (1,)), ((), ()))
TN = (((0,), (0,)), ((), ()))


def _cparams(sem=None):
    if sem is None:
        return pltpu.CompilerParams(vmem_limit_bytes=VMEM_LIMIT)
    return pltpu.CompilerParams(dimension_semantics=sem, vmem_limit_bytes=VMEM_LIMIT)


def _pick(n, cands):
    for c in cands:
        if n % c == 0:
            return c
    raise ValueError(f"no block for {n}")


def _dot(a, b, dn):
    return lax.dot_general(a, b, dn, preferred_element_type=F32)


def _sigmoid(x):
    return 0.5 * jnp.tanh(0.5 * x) + 0.5


def _split_bf16(x):
    hi = x.astype(BF16)
    lo = (x - hi.astype(F32)).astype(BF16)
    return hi, lo


def _matmul(a, b, mode, out_dtype, name, residual=None):
    if mode == "nn":
        (M, K), N = a.shape, b.shape[1]
        bk = K if K <= D_FF else _pick(K, (1024, 512))
        bm = 512 if D_MODEL < K <= D_FF else _pick(M, (1024, 512))
    else:
        (K, M), N = a.shape, b.shape[1]
        bk = _pick(K, (1024, 512))
        bm = _pick(M, (512, 256, 128))
    bn = _pick(N, (512, 256, 128))
    nk = K // bk
    dn = {"nn": NN, "tn": TN}[mode]
    a_spec = {"nn": pl.BlockSpec((bm, bk), lambda i, j, k: (i, k)),
              "tn": pl.BlockSpec((bk, bm), lambda i, j, k: (k, i))}[mode]
    b_spec = pl.BlockSpec((bk, bn), lambda i, j, k: (k, j))
    o_spec = pl.BlockSpec((bm, bn), lambda i, j, k: (i, j))
    has_res = residual is not None

    def body(*refs):
        if has_res:
            a_ref, b_ref, r_ref, o_ref, acc_ref = refs
        else:
            a_ref, b_ref, o_ref, acc_ref = refs
        k = pl.program_id(2)
        part = _dot(a_ref[...], b_ref[...], dn)

        def finish(total):
            if has_res:
                total = total + r_ref[...]
            o_ref[...] = total.astype(out_dtype)

        if nk == 1:
            finish(part)
        else:
            @pl.when(k == 0)
            def _():
                acc_ref[...] = part

            @pl.when(k > 0)
            def _():
                acc_ref[...] += part

            @pl.when(k == nk - 1)
            def _():
                finish(acc_ref[...])

    in_specs = [a_spec, b_spec] + ([o_spec] if has_res else [])
    args = (a, b) + ((residual,) if has_res else ())
    return pl.pallas_call(
        body, name=name,
        grid=(M // bm, N // bn, nk),
        in_specs=in_specs, out_specs=o_spec,
        out_shape=jax.ShapeDtypeStruct((M, N), out_dtype),
        scratch_shapes=[pltpu.VMEM((bm, bn) if nk > 1 else (8, LANES), F32)],
        compiler_params=_cparams(("parallel", "parallel", "arbitrary")),
    )(*args)


ROWS = 512


def _row_spec(cols, bm=ROWS):
    return pl.BlockSpec((bm, cols), lambda i: (i, 0))


def _full_spec(shape):
    return pl.BlockSpec(shape, lambda i: (0,) * len(shape))


def _colsum8(t):
    return jnp.sum(t.reshape(t.shape[0] // 8, 8, t.shape[1]), axis=0)


def _rms_fwd(x, g, name):
    S, D = x.shape

    def body(x_ref, g_ref, o_ref):
        xv = x_ref[...]
        r = lax.rsqrt(jnp.mean(xv * xv, axis=-1, keepdims=True) + EPS)
        o_ref[...] = (xv * r * g_ref[...]).astype(BF16)

    return pl.pallas_call(
        body, name=name, grid=(S // ROWS,),
        in_specs=[_row_spec(D), _full_spec((1, D))], out_specs=_row_spec(D),
        out_shape=jax.ShapeDtypeStruct((S, D), BF16),
        compiler_params=_cparams(("parallel",)),
    )(x, g)


def _rms_bwd(x, g, dy, dres, name):
    S, D = x.shape
    nt = S // ROWS

    def body(x_ref, g_ref, dy_ref, dres_ref, dx_ref, dxb_ref, dg_ref, acc_ref):
        i = pl.program_id(0)
        xv, dyv = x_ref[...], dy_ref[...]
        r = lax.rsqrt(jnp.mean(xv * xv, axis=-1, keepdims=True) + EPS)
        xr = xv * r
        u = dyv * g_ref[...]
        dx = r * u - xr * (r * r) * jnp.mean(xv * u, axis=-1, keepdims=True) + dres_ref[...]
        dx_ref[...] = dx
        dxb_ref[...] = dx.astype(BF16)
        part = _colsum8(dyv * xr)

        @pl.when(i == 0)
        def _():
            acc_ref[...] = part

        @pl.when(i > 0)
        def _():
            acc_ref[...] += part

        @pl.when(i == nt - 1)
        def _():
            dg_ref[...] = jnp.sum(acc_ref[...], axis=0, keepdims=True)

    return pl.pallas_call(
        body, name=name, grid=(nt,),
        in_specs=[_row_spec(D), _full_spec((1, D)), _row_spec(D), _row_spec(D)],
        out_specs=[_row_spec(D), _row_spec(D), _full_spec((1, D))],
        out_shape=[jax.ShapeDtypeStruct((S, D), F32), jax.ShapeDtypeStruct((S, D), BF16),
                   jax.ShapeDtypeStruct((1, D), F32)],
        scratch_shapes=[pltpu.VMEM((8, D), F32)],
        compiler_params=_cparams(("arbitrary",)),
    )(x, g, dy, dres)


def _head_mean(t, blockdiag):
    hi, lo = _split_bf16(t)
    return (_dot(hi, blockdiag, NN) + _dot(lo, blockdiag, NN)) * (1.0 / HEAD_DIM)


def _blockdiag():
    r = lax.broadcasted_iota(jnp.int32, (WIDTH, WIDTH), 0) // HEAD_DIM
    c = lax.broadcasted_iota(jnp.int32, (WIDTH, WIDTH), 1) // HEAD_DIM
    return jnp.where(r == c, 1.0, 0.0).astype(BF16)


def _qknorm_fwd(qk, gq, gk, name):
    S = qk.shape[0]

    def body(qk_ref, gq_ref, gk_ref, q_ref, k_ref):
        bd = _blockdiag()
        for part, g_ref, o_ref, scale in ((0, gq_ref, q_ref, HEAD_DIM ** -0.5), (1, gk_ref, k_ref, 1.0)):
            t = qk_ref[:, part * WIDTH:(part + 1) * WIDTH]
            r = lax.rsqrt(_head_mean(t * t, bd) + EPS)
            o_ref[...] = (t * r * g_ref[...] * scale).astype(BF16)

    return pl.pallas_call(
        body, name=name, grid=(S // ROWS,),
        in_specs=[_row_spec(2 * WIDTH), _full_spec((1, WIDTH)), _full_spec((1, WIDTH))],
        out_specs=[_row_spec(WIDTH), _row_spec(WIDTH)],
        out_shape=[jax.ShapeDtypeStruct((S, WIDTH), BF16)] * 2,
        compiler_params=_cparams(("parallel",)),
    )(qk, gq, gk)


def _qknorm_bwd(qk, gq, gk, dqn, dkn, name):
    S = qk.shape[0]
    nt = S // ROWS

    def body(qk_ref, gq_ref, gk_ref, dqn_ref, dkn_ref, dq_ref, dk_ref, dgq_ref, dgk_ref, accq_ref, acck_ref):
        i = pl.program_id(0)
        bd = _blockdiag()
        for part, g_ref, dn_ref, o_ref, dg_ref, acc_ref, scale in (
                (0, gq_ref, dqn_ref, dq_ref, dgq_ref, accq_ref, HEAD_DIM ** -0.5),
                (1, gk_ref, dkn_ref, dk_ref, dgk_ref, acck_ref, 1.0)):
            t = qk_ref[:, part * WIDTH:(part + 1) * WIDTH]
            dn = dn_ref[...] * scale
            r = lax.rsqrt(_head_mean(t * t, bd) + EPS)
            u = dn * g_ref[...]
            dt = r * u - t * (r * r * r) * _head_mean(t * u, bd)
            o_ref[...] = dt.astype(BF16)
            psum = _colsum8(dn * t * r)

            @pl.when(i == 0)
            def _():
                acc_ref[...] = psum

            @pl.when(i > 0)
            def _():
                acc_ref[...] += psum

            @pl.when(i == nt - 1)
            def _():
                dg_ref[...] = jnp.sum(acc_ref[...], axis=0, keepdims=True)

    return pl.pallas_call(
        body, name=name, grid=(nt,),
        in_specs=[_row_spec(2 * WIDTH), _full_spec((1, WIDTH)), _full_spec((1, WIDTH)),
                  _row_spec(WIDTH), _row_spec(WIDTH)],
        out_specs=[_row_spec(WIDTH), _row_spec(WIDTH), _full_spec((1, WIDTH)), _full_spec((1, WIDTH))],
        out_shape=[jax.ShapeDtypeStruct((S, WIDTH), BF16)] * 2 + [jax.ShapeDtypeStruct((1, WIDTH), F32)] * 2,
        scratch_shapes=[pltpu.VMEM((8, WIDTH), F32)] * 2,
        compiler_params=_cparams(("arbitrary",)),
    )(qk, gq, gk, dqn, dkn)


def _mix_fwd(ga, gb, ya, yb, name):
    S, D = ga.shape

    def body(ga_ref, gb_ref, ya_ref, yb_ref, o_ref):
        o_ref[...] = (_sigmoid(ga_ref[...]) * ya_ref[...] + _sigmoid(gb_ref[...]) * yb_ref[...]).astype(BF16)

    return pl.pallas_call(
        body, name=name, grid=(S // ROWS,),
        in_specs=[_row_spec(D)] * 4, out_specs=_row_spec(D),
        out_shape=jax.ShapeDtypeStruct((S, D), BF16),
        compiler_params=_cparams(("parallel",)),
    )(ga, gb, ya, yb)


def _mix_bwd(dm, ga, gb, ya, yb, name):
    S, D = ga.shape

    def body(dm_ref, ga_ref, gb_ref, ya_ref, yb_ref, dga_ref, dgb_ref, dya_ref, dyb_ref):
        dmv = dm_ref[...]
        for g_ref, y_ref, dg_ref, dy_ref in ((ga_ref, ya_ref, dga_ref, dya_ref), (gb_ref, yb_ref, dgb_ref, dyb_ref)):
            s = _sigmoid(g_ref[...])
            dy_ref[...] = (dmv * s).astype(BF16)
            dg_ref[...] = (dmv * y_ref[...] * s * (1.0 - s)).astype(BF16)

    return pl.pallas_call(
        body, name=name, grid=(S // ROWS,),
        in_specs=[_row_spec(D)] * 5, out_specs=[_row_spec(D)] * 4,
        out_shape=[jax.ShapeDtypeStruct((S, D), BF16)] * 4,
        compiler_params=_cparams(("parallel",)),
    )(dm, ga, gb, ya, yb)


def _loss_head(y, target, name):
    S, D = y.shape
    nt = S // ROWS

    def body(y_ref, t_ref, dy_ref, dyb_ref, p_ref):
        err = y_ref[...] - t_ref[...]
        dy = err * (1.0 / D)
        dy_ref[...] = dy
        dyb_ref[...] = dy.astype(BF16)
        sq = _colsum8(err * err)
        acc = sq[:, 0:LANES]
        for k in range(1, D // LANES):
            acc = acc + sq[:, k * LANES:(k + 1) * LANES]
        p_ref[...] = acc

    return pl.pallas_call(
        body, name=name, grid=(nt,),
        in_specs=[_row_spec(D)] * 2,
        out_specs=[_row_spec(D), _row_spec(D), pl.BlockSpec((8, LANES), lambda i: (i, 0))],
        out_shape=[jax.ShapeDtypeStruct((S, D), F32), jax.ShapeDtypeStruct((S, D), BF16),
                   jax.ShapeDtypeStruct((nt * 8, LANES), F32)],
        compiler_params=_cparams(("parallel",)),
    )(y, target)


CONV_COLS = 256
HALO = 16


def _conv_taps(xe, cw, cb):
    taps = (pltpu.roll(xe, 2, 0), pltpu.roll(xe, 1, 0), xe)
    return taps, cw[0:1] * taps[0] + cw[1:2] * taps[1] + cw[2:3] * taps[2] + cb


def _conv_specs(nt):
    hb, nb = ROWS // HALO, D_FF // CONV_COLS
    specs = {}
    for part, off in (("gate", 0), ("up", nb)):
        specs[part] = dict(
            main=pl.BlockSpec((ROWS, CONV_COLS), functools.partial(lambda c, i, off: (i, c + off), off=off)),
            prev=pl.BlockSpec((HALO, CONV_COLS),
                              functools.partial(lambda c, i, off: (jnp.maximum(i * hb - 1, 0), c + off), off=off)),
            nxt=pl.BlockSpec((HALO, CONV_COLS),
                             functools.partial(lambda c, i, off: (jnp.minimum((i + 1) * hb, nt * hb - 1), c + off), off=off)),
            w=pl.BlockSpec((3, CONV_COLS), functools.partial(lambda c, i, off: (0, c + off), off=off)),
            b=pl.BlockSpec((1, CONV_COLS), functools.partial(lambda c, i, off: (0, c + off), off=off)))
    return specs


def _convglu_fwd(hid, cw, cb, name):
    S = hid.shape[0]
    sp = _conv_specs(S // ROWS)

    def body(hg_ref, hgp_ref, hu_ref, hup_ref, cwg_ref, cwu_ref, cbg_ref, cbu_ref, o_ref):
        i = pl.program_id(1)
        keep = (i > 0).astype(F32)

        def conv(h_ref, hp_ref, cw_ref, cb_ref):
            xe = jnp.concatenate([hp_ref[...].astype(F32) * keep, h_ref[...].astype(F32)], axis=0)
            return _conv_taps(xe, cw_ref[...], cb_ref[...])[1][HALO:, :]

        gate = conv(hg_ref, hgp_ref, cwg_ref, cbg_ref)
        up = conv(hu_ref, hup_ref, cwu_ref, cbu_ref)
        o_ref[...] = (gate * _sigmoid(gate) * up).astype(BF16)

    g, u = sp["gate"], sp["up"]
    return pl.pallas_call(
        body, name=name, grid=(D_FF // CONV_COLS, S // ROWS),
        in_specs=[g["main"], g["prev"], u["main"], u["prev"], g["w"], u["w"], g["b"], u["b"]], out_specs=g["main"],
        out_shape=jax.ShapeDtypeStruct((S, D_FF), BF16),
        compiler_params=_cparams(("parallel", "parallel")),
    )(hid, hid, hid, hid, cw, cw, cb, cb)


def _convglu_bwd(hid, dact, cw, cb, name):
    S = hid.shape[0]
    nt = S // ROWS
    sp = _conv_specs(nt)

    def body(hg_ref, hgp_ref, hgn_ref, hu_ref, hup_ref, hun_ref, da_ref, dan_ref,
             cwg_ref, cwu_ref, cbg_ref, cbu_ref,
             dhg_ref, dhu_ref, dcwg_ref, dcwu_ref, dcbg_ref, dcbu_ref):
        i = pl.program_id(1)
        kp = (i > 0).astype(F32)
        kn = (i < nt - 1).astype(F32)

        def ext(h_ref, hp_ref, hn_ref):
            return jnp.concatenate([hp_ref[...].astype(F32) * kp, h_ref[...].astype(F32),
                                    hn_ref[...].astype(F32) * kn], axis=0)

        taps_g, gate = _conv_taps(ext(hg_ref, hgp_ref, hgn_ref), cwg_ref[...], cbg_ref[...])
        taps_u, up = _conv_taps(ext(hu_ref, hup_ref, hun_ref), cwu_ref[...], cbu_ref[...])
        gate, up = gate[HALO:, :], up[HALO:, :]
        da = jnp.concatenate([da_ref[...].astype(F32), dan_ref[...].astype(F32) * kn], axis=0)
        sg = _sigmoid(gate)
        dgate = da * up * sg * (1.0 + gate * (1.0 - sg))
        dup = da * gate * sg
        n_ext = ROWS + HALO

        @pl.when(i == 0)
        def _():
            for ref in (dcwg_ref, dcwu_ref, dcbg_ref, dcbu_ref):
                ref[...] = jnp.zeros_like(ref)

        for d, cw_ref, taps, dh_ref, dcw_ref, dcb_ref in ((dgate, cwg_ref, taps_g, dhg_ref, dcwg_ref, dcbg_ref),
                                                         (dup, cwu_ref, taps_u, dhu_ref, dcwu_ref, dcbu_ref)):
            cwv = cw_ref[...]
            dh = cwv[2:3] * d + cwv[1:2] * pltpu.roll(d, n_ext - 1, 0) + cwv[0:1] * pltpu.roll(d, n_ext - 2, 0)
            dh_ref[...] = dh[:ROWS, :].astype(BF16)
            dc = d[:ROWS, :]
            for t in range(3):
                dcw_ref[t:t + 1, :] += jnp.sum(dc * taps[t][HALO:HALO + ROWS, :], axis=0, keepdims=True)
            dcb_ref[...] += jnp.sum(dc, axis=0, keepdims=True)

    g, u = sp["gate"], sp["up"]
    return pl.pallas_call(
        body, name=name, grid=(D_FF // CONV_COLS, nt),
        in_specs=[g["main"], g["prev"], g["nxt"], u["main"], u["prev"], u["nxt"], g["main"], g["nxt"],
                  g["w"], u["w"], g["b"], u["b"]],
        out_specs=[g["main"], g["main"], g["w"], g["w"], g["b"], g["b"]],
        out_shape=[jax.ShapeDtypeStruct((S, D_FF), BF16)] * 2 + [jax.ShapeDtypeStruct((3, D_FF), F32)] * 2
        + [jax.ShapeDtypeStruct((1, D_FF), F32)] * 2,
        compiler_params=_cparams(("parallel", "arbitrary")),
    )(hid, hid, hid, hid, hid, hid, dact, dact, cw, cw, cb, cb)


REL_PAD = 384
DIAG = 1024


def _band_valid():
    qc = lax.broadcasted_iota(jnp.int32, (BQ, KWIN), 0) // CHUNK
    kc = lax.broadcasted_iota(jnp.int32, (BQ, KWIN), 1) // CHUNK - LEFT_CHUNKS
    return (kc <= qc) & (kc >= qc - LEFT_CHUNKS)


def _rel_index(offset):
    return jnp.clip(BAND - offset, -MAX_REL, MAX_REL) + MAX_REL


def _split3(x):
    hi = x.astype(BF16)
    rest = x - hi.astype(F32)
    mid = rest.astype(BF16)
    return hi, mid, (rest - mid.astype(F32)).astype(BF16)


def _bias_table(rel_bias, name):
    def body(rb_ref, o_ref):
        t = lax.broadcasted_iota(jnp.int32, (REL_PAD, DIAG), 0)
        lane = lax.broadcasted_iota(jnp.int32, (REL_PAD, DIAG), 1)
        pick = jnp.where(t == _rel_index(lane - BQ), 1.0, 0.0).astype(BF16)
        base = sum(_dot(piece, pick, NN) for piece in _split3(rb_ref[...]))
        valid = _band_valid()
        for h in range(N_HEADS):
            rows = jnp.broadcast_to(base[h:h + 1], (BQ, DIAG))
            rolled = pltpu.roll(rows, 0, 1, stride=1, stride_axis=0)
            o_ref[h] = jnp.where(valid, rolled[:, BQ:], NEG)

    return pl.pallas_call(
        body, name=name,
        out_shape=jax.ShapeDtypeStruct((N_HEADS, BQ, KWIN), F32),
        compiler_params=_cparams(),
    )(rel_bias)


def _bias_table_bwd(dtab, name):
    def body(d_ref, o_ref, diag_ref):
        r = lax.broadcasted_iota(jnp.int32, (BQ, BQ), 0)
        c = lax.broadcasted_iota(jnp.int32, (BQ, BQ), 1)
        flip = jnp.where(r + c == BQ - 1, 1.0, 0.0).astype(BF16)
        for h in range(N_HEADS):
            flipped = sum(_dot(flip, piece, NN) for piece in _split3(d_ref[h]))
            padded = jnp.concatenate([flipped, jnp.zeros((BQ, DIAG - KWIN), F32)], axis=1)
            rolled = pltpu.roll(padded, DIAG - (BQ - 1), 1, stride=1, stride_axis=0)
            diag_ref[h:h + 1, :] = jnp.sum(rolled, axis=0, keepdims=True)
        lane = lax.broadcasted_iota(jnp.int32, (DIAG, REL_PAD), 0)
        t = lax.broadcasted_iota(jnp.int32, (DIAG, REL_PAD), 1)
        offset = jnp.where(lane < KWIN, lane, lane - DIAG)
        pick = jnp.where(t == _rel_index(offset), 1.0, 0.0).astype(BF16)
        o_ref[...] = sum(_dot(piece, pick, NN) for piece in _split3(diag_ref[...]))

    return pl.pallas_call(
        body, name=name,
        out_shape=jax.ShapeDtypeStruct((N_HEADS, REL_PAD), F32),
        scratch_shapes=[pltpu.VMEM((N_HEADS, DIAG), F32)],
        compiler_params=_cparams(),
    )(dtab)


def _head_masks():
    lane = lax.broadcasted_iota(jnp.int32, (1, PAIR), 1)
    return [lane // HEAD_DIM == h for h in range(2)]


def _ca_window_specs(nq):
    return [pl.BlockSpec((BQ, PAIR), functools.partial(
        lambda p, i, d: (jnp.clip(i - 2 + d, 0, nq - 1), p), d=d)) for d in range(3)]


def _ca_scores(qm, kc, tab_h, i):
    col = lax.broadcasted_iota(jnp.int32, (1, KWIN), 1)
    in_seq = col + (i - 2) * BQ >= 0
    return jnp.where(in_seq, _dot(qm, kc, NT) + tab_h, NEG)


def _ca_fwd(qn, kn, v, tab, name):
    S = qn.shape[0]
    nq = S // BQ
    qspec = pl.BlockSpec((BQ, PAIR), lambda p, i: (i, p))
    tspec = pl.BlockSpec((2, BQ, KWIN), lambda p, i: (p, 0, 0))

    def body(q_ref, k0, k1, k2, v0, v1, v2, tab_ref, o_ref):
        i = pl.program_id(1)
        kc = jnp.concatenate([k0[...], k1[...], k2[...]], axis=0)
        vc = jnp.concatenate([v0[...], v1[...], v2[...]], axis=0)
        qv = q_ref[...]
        acc = jnp.zeros((BQ, PAIR), F32)
        for h, m in enumerate(_head_masks()):
            s = _ca_scores(jnp.where(m, qv, 0), kc, tab_ref[h], i)
            p = jnp.exp(s - jnp.max(s, axis=-1, keepdims=True))
            l = jnp.sum(p, axis=-1, keepdims=True)
            acc = acc + _dot(p.astype(BF16), jnp.where(m, vc, 0), NN) / l
        o_ref[...] = acc.astype(BF16)

    win = _ca_window_specs(nq)
    return pl.pallas_call(
        body, name=name, grid=(WIDTH // PAIR, nq),
        in_specs=[qspec] + win + win + [tspec], out_specs=qspec,
        out_shape=jax.ShapeDtypeStruct((S, WIDTH), BF16),
        compiler_params=_cparams(("parallel", "parallel")),
    )(qn, kn, kn, kn, v, v, v, tab)


def _ca_bwd(qn, kn, v, do, tab, name):
    S = qn.shape[0]
    nq = S // BQ
    qspec = pl.BlockSpec((BQ, PAIR), lambda p, i: (jnp.minimum(i, nq - 1), p))
    kout = pl.BlockSpec((BQ, PAIR), lambda p, i: (jnp.clip(i - 2, 0, nq - 1), p))
    tspec = pl.BlockSpec((2, BQ, KWIN), lambda p, i: (p, 0, 0))

    def body(q_ref, do_ref, k0, k1, k2, v0, v1, v2, tab_ref,
             dq_ref, dk_ref, dv_ref, dtab_ref, dk_acc, dv_acc):
        i = pl.program_id(1)

        @pl.when(i == 0)
        def _():
            dk_acc[...] = jnp.zeros_like(dk_acc)
            dv_acc[...] = jnp.zeros_like(dv_acc)
            dtab_ref[...] = jnp.zeros_like(dtab_ref)

        @pl.when(i < nq)
        def _():
            kc = jnp.concatenate([k0[...], k1[...], k2[...]], axis=0)
            vc = jnp.concatenate([v0[...], v1[...], v2[...]], axis=0)
            qv, dov = q_ref[...], do_ref[...]
            dq = jnp.zeros((BQ, PAIR), F32)
            dkc = jnp.zeros((KWIN, PAIR), F32)
            dvc = jnp.zeros((KWIN, PAIR), F32)
            for h, m in enumerate(_head_masks()):
                qm, dom = jnp.where(m, qv, 0), jnp.where(m, dov, 0)
                s = _ca_scores(qm, kc, tab_ref[h], i)
                p = jnp.exp(s - jnp.max(s, axis=-1, keepdims=True))
                p = p / jnp.sum(p, axis=-1, keepdims=True)
                dp = _dot(dom, vc, NT)
                ds = p * (dp - jnp.sum(p * dp, axis=-1, keepdims=True))
                dtab_ref[h] += ds
                dsb, pb = ds.astype(BF16), p.astype(BF16)
                dq = dq + _dot(dsb, jnp.where(m, kc, 0), NN)
                dkc = dkc + _dot(dsb, qm, TN)
                dvc = dvc + _dot(pb, dom, TN)
            dq_ref[...] = dq
            for d in range(3):
                slot = (i + 1 + d) % 3
                dk_acc[slot] += dkc[d * BQ:(d + 1) * BQ]
                dv_acc[slot] += dvc[d * BQ:(d + 1) * BQ]

        @pl.when(i >= 2)
        def _():
            slot = (i + 1) % 3
            dk_ref[...] = dk_acc[slot]
            dv_ref[...] = dv_acc[slot].astype(BF16)
            dk_acc[slot] = jnp.zeros((BQ, PAIR), F32)
            dv_acc[slot] = jnp.zeros((BQ, PAIR), F32)

    win = _ca_window_specs(nq)
    return pl.pallas_call(
        body, name=name, grid=(WIDTH // PAIR, nq + 2),
        in_specs=[qspec, qspec] + win + win + [tspec],
        out_specs=[qspec, kout, kout, tspec],
        out_shape=[jax.ShapeDtypeStruct((S, WIDTH), F32), jax.ShapeDtypeStruct((S, WIDTH), F32),
                   jax.ShapeDtypeStruct((S, WIDTH), BF16), jax.ShapeDtypeStruct((N_HEADS, BQ, KWIN), F32)],
        scratch_shapes=[pltpu.VMEM((3, BQ, PAIR), F32)] * 2,
        compiler_params=_cparams(("parallel", "arbitrary")),
    )(qn, do, kn, kn, kn, v, v, v, tab)


def _sb_consts():
    r = lax.broadcasted_iota(jnp.int32, (BQ, BQ), 0)
    c = lax.broadcasted_iota(jnp.int32, (BQ, BQ), 1)
    from_s = jnp.where(r >= c, 1.0, 0.0).astype(BF16)
    causal = c < r
    return from_s, causal


def _suffix_sum(t, from_s):
    hi, lo = _split_bf16(t)
    return _dot(hi, from_s, NN) + _dot(lo, from_s, NN)


def _neg_abs(x):
    bits = lax.bitcast_convert_type(x, jnp.uint32) | jnp.uint32(0x80000000)
    return lax.bitcast_convert_type(bits, F32)


def _sb_log_keep(zn):
    return jnp.minimum(zn, 0.0) - jnp.log(1.0 + jnp.exp(_neg_abs(zn)))


SB_DEAD = 105.0


def _sb_walk(i, tiles, keep_ref):
    tiles([i], True)

    def alive():
        return (jnp.max(keep_ref[...]) > -SB_DEAD).astype(jnp.int32)

    def step(state):
        j, _ = state
        tiles([j], False)
        return j - 1, alive()

    lax.while_loop(lambda state: (state[0] >= 0) & (state[1] > 0), step, (i - 1, alive()))


def _sb_rows(j):
    return pl.ds(pl.multiple_of(j * BQ, BQ), BQ)


def _sb_fwd(q, k, v, name):
    S = q.shape[0]
    nq = S // BQ
    qspec = pl.BlockSpec((BQ, PAIR), lambda p, i: (i, p))
    kspec = pl.BlockSpec((S, PAIR), lambda p, i: (0, p))

    def body(q_ref, k_ref, v_ref, o_ref, of_ref, carry_ref, acc_ref):
        i = pl.program_id(1)
        from_s, causal = _sb_consts()
        masks = _head_masks()
        qn = q_ref[...] * -(HEAD_DIM ** -0.5)
        qms = [jnp.where(m, qn, 0) for m in masks]
        carry_ref[...] = jnp.zeros_like(carry_ref)
        acc_ref[...] = jnp.zeros_like(acc_ref)

        def tiles(js, diag):
            chains = [(n, h) for n in range(len(js)) for h in range(2)]
            kbs = [k_ref[_sb_rows(j), :] for j in js]
            vbs = [v_ref[_sb_rows(j), :] for j in js]
            zn = {c: _dot(qms[c[1]], kbs[c[0]], NT) for c in chains}
            log_keep = {c: _sb_log_keep(zn[c]) for c in chains}
            if diag:
                log_keep = {c: jnp.where(causal, log_keep[c], 0.0) for c in chains}
            split = {c: _split_bf16(log_keep[c]) for c in chains}
            carry = {}
            for h in range(2):
                run = carry_ref[h]
                for n in range(len(js)):
                    carry[(n, h)] = run
                    run = run + jnp.sum(log_keep[(n, h)], axis=-1, keepdims=True)
                carry_ref[h] = run
            suffix = {c: _dot(split[c][0], from_s, NN) + _dot(split[c][1], from_s, NN) for c in chains}
            w = {c: jnp.exp(carry[c] + suffix[c] - zn[c]) for c in chains}
            if diag:
                w = {c: jnp.where(causal, w[c], 0.0) for c in chains}
            for c in chains:
                acc_ref[c[1]] += _dot(w[c].astype(BF16), vbs[c[0]], NN)

        _sb_walk(i, tiles, carry_ref)
        out = jnp.where(masks[0], acc_ref[0], acc_ref[1])
        o_ref[...] = out.astype(BF16)
        of_ref[...] = out

    return pl.pallas_call(
        body, name=name, grid=(WIDTH // PAIR, nq),
        in_specs=[qspec, kspec, kspec], out_specs=[qspec, qspec],
        out_shape=[jax.ShapeDtypeStruct((S, WIDTH), BF16), jax.ShapeDtypeStruct((S, WIDTH), F32)],
        scratch_shapes=[pltpu.VMEM((2, BQ, 1), F32), pltpu.VMEM((2, BQ, PAIR), F32)],
        compiler_params=_cparams(("parallel", "arbitrary")),
    )(q, k, v)


def _sb_bwd(q, k, v, o, do, name):
    S = q.shape[0]
    nq = S // BQ
    qspec = pl.BlockSpec((BQ, PAIR), lambda p, i: (i, p))
    kspec = pl.BlockSpec((S, PAIR), lambda p, i: (0, p))

    def body(q_ref, o_ref, do_ref, k_ref, v_ref, dq_ref, dk_ref, dv_ref, dk_acc, dv_acc, keep_ref, gsum_ref, dq_acc):
        i = pl.program_id(1)

        @pl.when(i == 0)
        def _():
            dk_acc[...] = jnp.zeros_like(dk_acc)
            dv_acc[...] = jnp.zeros_like(dv_acc)

        from_s, causal = _sb_consts()
        masks = _head_masks()
        qn, dov = q_ref[...] * -(HEAD_DIM ** -0.5), do_ref[...]
        od = o_ref[...] * dov.astype(F32)
        qms = [jnp.where(m, qn, 0) for m in masks]
        doms = [jnp.where(m, dov, 0) for m in masks]
        totals = [jnp.sum(jnp.where(m, od, 0.0), axis=-1, keepdims=True) for m in masks]
        for ref in (keep_ref, gsum_ref, dq_acc):
            ref[...] = jnp.zeros_like(ref)

        def running(ref, vals, n_blocks):
            before_chain = {}
            for h in range(2):
                run = ref[h]
                for n in range(n_blocks):
                    before_chain[(n, h)] = run
                    run = run + jnp.sum(vals[(n, h)], axis=-1, keepdims=True)
                ref[h] = run
            return before_chain

        def tiles(js, diag):
            chains = [(n, h) for n in range(len(js)) for h in range(2)]
            kbs = [k_ref[_sb_rows(j), :] for j in js]
            vbs = [v_ref[_sb_rows(j), :] for j in js]
            zn = {c: _dot(qms[c[1]], kbs[c[0]], NT) for c in chains}
            dw = {c: _dot(doms[c[1]], vbs[c[0]], NT) for c in chains}
            log_keep = {c: _sb_log_keep(zn[c]) for c in chains}
            if diag:
                log_keep = {c: jnp.where(causal, log_keep[c], 0.0) for c in chains}
            split = {c: _split_bf16(log_keep[c]) for c in chains}
            kept = running(keep_ref, log_keep, len(js))
            suffix = {c: _dot(split[c][0], from_s, NN) + _dot(split[c][1], from_s, NN) for c in chains}
            w = {c: jnp.exp(kept[c] + suffix[c] - zn[c]) for c in chains}
            if diag:
                w = {c: jnp.where(causal, w[c], 0.0) for c in chains}
            wb = {c: w[c].astype(BF16) for c in chains}
            g = {c: wb[c].astype(F32) * dw[c] for c in chains}
            gsplit = {c: _split_bf16(g[c]) for c in chains}
            gsum = running(gsum_ref, g, len(js))
            gsuffix = {c: _dot(gsplit[c][0], from_s, NN) + _dot(gsplit[c][1], from_s, NN) for c in chains}
            dzb = {}
            for c in chains:
                before = totals[c[1]] - (gsum[c] + gsuffix[c])
                dz = (g[c] + before) * jnp.exp(log_keep[c]) - before
                if diag:
                    dz = jnp.where(causal, dz, 0.0)
                dzb[c] = dz.astype(BF16)
            for c in chains:
                rows = _sb_rows(js[c[0]])
                dq_acc[c[1]] += _dot(dzb[c], kbs[c[0]], NN)
                dk_acc[rows, :] -= _dot(dzb[c], qms[c[1]], TN)
                dv_acc[rows, :] += _dot(wb[c], doms[c[1]], TN)

        _sb_walk(i, tiles, keep_ref)
        dq_ref[...] = (jnp.where(masks[0], dq_acc[0], dq_acc[1]) * HEAD_DIM ** -0.5).astype(BF16)

        @pl.when(i == nq - 1)
        def _():
            dk_ref[...] = dk_acc[...].astype(BF16)
            dv_ref[...] = dv_acc[...].astype(BF16)

    return pl.pallas_call(
        body, name=name, grid=(WIDTH // PAIR, nq),
        in_specs=[qspec, qspec, qspec, kspec, kspec], out_specs=[qspec, kspec, kspec],
        out_shape=[jax.ShapeDtypeStruct((S, WIDTH), BF16)] * 3,
        scratch_shapes=[pltpu.VMEM((S, PAIR), F32)] * 2 + [pltpu.VMEM((2, BQ, 1), F32)] * 2
        + [pltpu.VMEM((2, BQ, PAIR), F32)],
        compiler_params=_cparams(("parallel", "arbitrary")),
    )(q, o, do, k, v)


ANY = pl.BlockSpec(memory_space=pl.ANY)


def _place():
    return lax.axis_index("x"), lax.axis_index("y"), lax.axis_index("c")


def _other_chips(x, y):
    return [(2 * px + py, (px, py)) for px, py in ((1 - x, y), (x, 1 - y), (1 - x, 1 - y))]


def _remote(src, dst, sems, k, to):
    return pltpu.make_async_remote_copy(src_ref=src, dst_ref=dst, send_sem=sems[0].at[k], recv_sem=sems[1].at[k],
                                        device_id=to, device_id_type=MESH)


def _gather_weights(wp, name):
    R = wp.shape[0]
    Rh = R // 2

    def body(wp_ref, out_ref, send_sems, recv_sems):
        x, y, c = _place()
        me = 2 * x + y
        sems = (send_sems, recv_sems)
        half = pl.ds(c * Rh, Rh)
        chips = _other_chips(x, y)
        first = [_remote(wp_ref.at[half], out_ref.at[me, half], sems, k, (*xy, c)) for k, (_, xy) in enumerate(chips)]
        for cp in first:
            cp.start()
        passed = []
        for k, (chip, xy) in enumerate(chips):
            landed = out_ref.at[chip, half]
            _remote(landed, landed, sems, k, (*xy, c)).wait_recv()
            cp = _remote(landed, landed, sems, 3 + k, (x, y, 1 - c))
            cp.start()
            passed.append(cp)
        other = pl.ds((1 - c) * Rh, Rh)
        for k, (chip, xy) in enumerate(chips):
            landed = out_ref.at[chip, other]
            _remote(landed, landed, sems, 3 + k, (x, y, 1 - c)).wait_recv()
        for cp in first + passed:
            cp.wait_send()

    return pl.pallas_call(
        body, name=name, in_specs=[ANY], out_specs=ANY,
        out_shape=jax.ShapeDtypeStruct((N_CHIPS, R, PACK_COLS), wp.dtype),
        scratch_shapes=[pltpu.SemaphoreType.DMA((6,)), pltpu.SemaphoreType.DMA((6,))],
    )(wp)


def _exchange_cores(g, small, name):
    R = g.shape[1]
    Rh = R // 2

    def body(g_ref, small_ref, sib_ref, all_ref, send_sems, recv_sems):
        x, y, c = _place()
        me = 4 * x + 2 * y + c
        sems = (send_sems, recv_sems)
        theirs = pl.ds((1 - c) * Rh, Rh)
        copies = [_remote(g_ref.at[j, theirs], sib_ref.at[j], sems, j, (x, y, 1 - c)) for j in range(N_CHIPS)]
        k = N_CHIPS
        for fx in (0, 1):
            for fy in (0, 1):
                for fc in (0, 1):
                    if fx or fy or fc:
                        to = (1 - x if fx else x, 1 - y if fy else y, 1 - c if fc else c)
                        copies.append(_remote(small_ref, all_ref.at[me], sems, k, to))
                        k += 1
        for cp in copies:
            cp.start()
        for cp in copies:
            cp.wait_recv()
        for cp in copies:
            cp.wait_send()

    n_copies = N_CHIPS + N_DEV - 1
    return pl.pallas_call(
        body, name=name, in_specs=[ANY, ANY], out_specs=[ANY, ANY],
        out_shape=[jax.ShapeDtypeStruct((N_CHIPS, Rh, PACK_COLS), F32),
                   jax.ShapeDtypeStruct((N_DEV, SMALL_ROWS, PACK_COLS), F32)],
        scratch_shapes=[pltpu.SemaphoreType.DMA((n_copies,)), pltpu.SemaphoreType.DMA((n_copies,))],
    )(g, small)


def _exchange_chips(p, name):
    def body(p_ref, out_ref, send_sems, recv_sems):
        x, y, c = _place()
        me = 2 * x + y
        sems = (send_sems, recv_sems)
        copies = [_remote(p_ref.at[chip], out_ref.at[me], sems, k, (*xy, c))
                  for k, (chip, xy) in enumerate(_other_chips(x, y))]
        for cp in copies:
            cp.start()
        for k, (chip, xy) in enumerate(_other_chips(x, y)):
            _remote(p_ref.at[chip], out_ref.at[chip], sems, k, (*xy, c)).wait_recv()
        for cp in copies:
            cp.wait_send()

    return pl.pallas_call(
        body, name=name, in_specs=[ANY], out_specs=ANY,
        out_shape=jax.ShapeDtypeStruct(p.shape, p.dtype),
        scratch_shapes=[pltpu.SemaphoreType.DMA((3,)), pltpu.SemaphoreType.DMA((3,))],
    )(p)


SHARE_CHUNKS = 6


def _share_halves(gh, name):
    Rh = gh.shape[0]
    rows = Rh // SHARE_CHUNKS

    def body(gh_ref, out_ref, send_sems, recv_sems):
        x, y, c = _place()
        sems = (send_sems, recv_sems)
        copies = [_remote(gh_ref.at[pl.ds(k * rows, rows)], out_ref.at[pl.ds(k * rows, rows)], sems, k, (x, y, 1 - c))
                  for k in range(SHARE_CHUNKS)]
        for cp in copies:
            cp.start()
        for cp in copies:
            cp.wait_recv()
        for cp in copies:
            cp.wait_send()

    return pl.pallas_call(
        body, name=name, in_specs=[ANY], out_specs=ANY,
        out_shape=jax.ShapeDtypeStruct(gh.shape, gh.dtype),
        scratch_shapes=[pltpu.SemaphoreType.DMA((SHARE_CHUNKS,)), pltpu.SemaphoreType.DMA((SHARE_CHUNKS,))],
    )(gh)


def _row_block(rows, mult=8, cap=512):
    return max(b for b in range(mult, cap + 1, mult) if rows % b == 0)


def _add2(a, b, name):
    n, Rh, C = a.shape
    rows = _row_block(Rh, mult=16, cap=1024)
    spec = pl.BlockSpec((1, rows, C), lambda j, i: (j, i, 0))

    def body(a_ref, b_ref, o_ref):
        o_ref[...] = (a_ref[...] + b_ref[...]).astype(BF16)

    return pl.pallas_call(
        body, name=name, grid=(n, Rh // rows), in_specs=[spec, spec], out_specs=spec,
        out_shape=jax.ShapeDtypeStruct(a.shape, BF16),
        compiler_params=_cparams(("parallel", "parallel")),
    )(a, b)


def _sum_leading(a, name):
    n, R, C = a.shape
    rows = _row_block(R, mult=16 if a.dtype == BF16 else 8, cap=1024 if a.dtype == BF16 else 512)

    def body(a_ref, o_ref):
        acc = a_ref[0].astype(F32)
        for j in range(1, n):
            acc = acc + a_ref[j].astype(F32)
        o_ref[...] = acc

    return pl.pallas_call(
        body, name=name, grid=(R // rows,),
        in_specs=[pl.BlockSpec((n, rows, C), lambda i: (0, i, 0))],
        out_specs=pl.BlockSpec((rows, C), lambda i: (i, 0)),
        out_shape=jax.ShapeDtypeStruct((R, C), F32),
        compiler_params=_cparams(("parallel",)),
    )(a)


def _adamw(w, g, m, v, name):
    R, C = w.shape
    rows = _row_block(R)
    spec = pl.BlockSpec((rows, C), lambda i: (i, 0))

    def body(w_ref, g_ref, m_ref, v_ref, d_ref, mo_ref, vo_ref):
        gv = g_ref[...]
        mn = ADAM_B1 * m_ref[...] + (1.0 - ADAM_B1) * gv
        vn = ADAM_B2 * v_ref[...] + (1.0 - ADAM_B2) * (gv * gv)
        m_hat = mn / (1.0 - ADAM_B1 ** ADAM_STEP)
        v_hat = vn / (1.0 - ADAM_B2 ** ADAM_STEP)
        d_ref[...] = -ADAM_LR * (m_hat / (jnp.sqrt(v_hat) + ADAM_EPS) + ADAM_WD * w_ref[...])
        mo_ref[...] = mn
        vo_ref[...] = vn

    return pl.pallas_call(
        body, name=name, grid=(R // rows,), in_specs=[spec] * 4, out_specs=[spec] * 3,
        out_shape=[jax.ShapeDtypeStruct((R, C), F32)] * 3,
        compiler_params=_cparams(("parallel",)),
    )(w, g, m, v)


BIG = ("w_in", "w_branch_a", "w_branch_b", "w_out", "w_ffn_up", "w_ffn_down", "ffn_conv_w")
BIG_SHARD = {"w_in": (D_MODEL, 1280), "w_branch_a": (WIDTH, 256), "w_branch_b": (WIDTH, 256),
             "w_out": (256, D_MODEL), "w_ffn_up": (D_MODEL, 1408), "w_ffn_down": (704, D_MODEL),
             "ffn_conv_w": (3, 1408)}
BIG_COL_SHARDED = {"w_in": True, "w_branch_a": True, "w_branch_b": True, "w_out": False,
                   "w_ffn_up": True, "w_ffn_down": False, "ffn_conv_w": True}
CONV_W_ROWS = 32


def _pack_rows(name):
    r, c = BIG_SHARD[name]
    return CONV_W_ROWS if name == "ffn_conv_w" else r * c // PACK_COLS


def _rows_of(flat, rows):
    return jnp.pad(flat, (0, rows * PACK_COLS - flat.shape[0])).reshape(rows, PACK_COLS)


def _pack_shards(shards, dtype=F32):
    parts = []
    for n in BIG:
        if n == "ffn_conv_w":
            flat = shards[n].reshape(-1)
            if dtype == BF16:
                flat = lax.bitcast_convert_type(flat, BF16).reshape(-1)
            parts.append(_rows_of(flat, CONV_W_ROWS))
        else:
            parts.append(shards[n].reshape(_pack_rows(n), PACK_COLS).astype(dtype))
    return jnp.concatenate(parts, axis=0)


def _unpack_shards(packed):
    out, r0 = {}, 0
    for n in BIG:
        r, c = BIG_SHARD[n]
        out[n] = packed[r0:r0 + _pack_rows(n)].reshape(-1)[:r * c].reshape(r, c)
        r0 += _pack_rows(n)
    return out


def _unpack_full(gathered):
    out, r0 = {}, 0
    for n in BIG:
        r, c = BIG_SHARD[n]
        part = gathered[:, r0:r0 + _pack_rows(n)].reshape(N_CHIPS, -1)
        if n == "ffn_conv_w":
            part = lax.bitcast_convert_type(part[:, :2 * r * c].reshape(N_CHIPS, r * c, 2), F32)
        parts = part[:, :r * c].reshape(N_CHIPS, r, c)
        out[n] = jnp.concatenate(list(parts), axis=1 if BIG_COL_SHARDED[n] else 0)
        r0 += _pack_rows(n)
    return out


def _pack_full(full):
    slabs = []
    for j in range(N_CHIPS):
        shards = {}
        for n in BIG:
            r, c = BIG_SHARD[n]
            shards[n] = full[n][:, j * c:(j + 1) * c] if BIG_COL_SHARDED[n] else full[n][j * r:(j + 1) * r]
        slabs.append(_pack_shards(shards))
    return jnp.stack(slabs)


SMALL = (("norm1_g", D_MODEL), ("q_norm_g", HEAD_DIM), ("k_norm_g", HEAD_DIM), ("rel_bias", N_HEADS * N_REL),
         ("norm2_g", D_MODEL), ("ffn_conv_b", 2 * D_FF))


def _pack_small(vals):
    return _rows_of(jnp.concatenate([vals[n].reshape(-1) for n, _ in SMALL]), SMALL_ROWS)


def _unpack_small(packed):
    flat, out, o = packed.reshape(-1), {}, 0
    for n, sz in SMALL:
        out[n] = flat[o:o + sz]
        o += sz
    return out


def kernel(x, norm1_g, w_in, q_norm_g, k_norm_g, rel_bias, w_branch_a, w_branch_b, w_out, norm2_g, w_ffn_up, ffn_conv_w, ffn_conv_b, w_ffn_down, loss_target, m_norm1_g, m_w_in, m_q_norm_g, m_k_norm_g, m_rel_bias, m_w_branch_a, m_w_branch_b, m_w_out, m_norm2_g, m_w_ffn_up, m_ffn_conv_w, m_ffn_conv_b, m_w_ffn_down, v_norm1_g, v_w_in, v_q_norm_g, v_k_norm_g, v_rel_bias, v_w_branch_a, v_w_branch_b, v_w_out, v_norm2_g, v_w_ffn_up, v_ffn_conv_w, v_ffn_conv_b, v_w_ffn_down):
    w_big = {"w_in": w_in[0], "w_branch_a": w_branch_a[0], "w_branch_b": w_branch_b[0], "w_out": w_out[0],
             "w_ffn_up": w_ffn_up[0], "w_ffn_down": w_ffn_down[0], "ffn_conv_w": ffn_conv_w[0]}
    m_big = {"w_in": m_w_in[0], "w_branch_a": m_w_branch_a[0], "w_branch_b": m_w_branch_b[0], "w_out": m_w_out[0],
             "w_ffn_up": m_w_ffn_up[0], "w_ffn_down": m_w_ffn_down[0], "ffn_conv_w": m_ffn_conv_w[0]}
    v_big = {"w_in": v_w_in[0], "w_branch_a": v_w_branch_a[0], "w_branch_b": v_w_branch_b[0], "w_out": v_w_out[0],
             "w_ffn_up": v_w_ffn_up[0], "w_ffn_down": v_w_ffn_down[0], "ffn_conv_w": v_ffn_conv_w[0]}
    xs, tgt = x[0], loss_target[0]

    xi, yi, ci = _place()
    chip = 2 * xi + yi
    w_pack_bf = _pack_shards(w_big, BF16)
    gathered = lax.dynamic_update_slice(_gather_weights(w_pack_bf, "gather_weights"), w_pack_bf[None], (chip, 0, 0))
    full = _unpack_full(gathered)
    w_in_f, w_a, w_b, w_o = full["w_in"], full["w_branch_a"], full["w_branch_b"], full["w_out"]
    w_up, w_dn, conv_w = full["w_ffn_up"], full["w_ffn_down"], full["ffn_conv_w"]
    w_in_t, w_a_t, w_b_t, w_o_t, w_up_t, w_dn_t = (w.T for w in (w_in_f, w_a, w_b, w_o, w_up, w_dn))

    hn = _rms_fwd(xs, norm1_g, "rms1")
    qk = _matmul(hn, w_in_f[:, :2 * WIDTH], "nn", F32, "proj_qk")
    vqkv = _matmul(hn, w_in_f[:, 2 * WIDTH:6 * WIDTH], "nn", BF16, "proj_vqkv")
    g_a = _matmul(hn, w_in_f[:, 6 * WIDTH:6 * WIDTH + D_MODEL], "nn", F32, "proj_gate_a")
    g_b = _matmul(hn, w_in_f[:, 6 * WIDTH + D_MODEL:], "nn", F32, "proj_gate_b")
    gq = jnp.tile(q_norm_g, (1, N_HEADS))
    gk = jnp.tile(k_norm_g, (1, N_HEADS))
    qa, ka = _qknorm_fwd(qk, gq, gk, "qknorm")
    va, qb, kb, vb = (vqkv[:, k * WIDTH:(k + 1) * WIDTH] for k in range(4))
    tab = _bias_table(jnp.pad(rel_bias[0], ((0, 0), (0, REL_PAD - N_REL))), "bias_table")
    out_a = _ca_fwd(qa, ka, va, tab, "chunk_attn")
    out_b, out_b_f32 = _sb_fwd(qb, kb, vb, "stick_attn")
    y_a = _matmul(out_a, w_a, "nn", F32, "branch_a")
    y_b = _matmul(out_b, w_b, "nn", F32, "branch_b")
    mixed = _mix_fwd(g_a, g_b, y_a, y_b, "mix")
    x2 = _matmul(mixed, w_o, "nn", F32, "out_proj", residual=xs)
    hn2 = _rms_fwd(x2, norm2_g, "rms2")
    hid = _matmul(hn2, w_up, "nn", BF16, "ffn_up")
    act = _convglu_fwd(hid, conv_w, ffn_conv_b, "convglu")
    y = _matmul(act, w_dn, "nn", F32, "ffn_down", residual=x2)
    dy, dyb, sq = _loss_head(y, tgt, "loss_head")
    loss = lax.psum(0.5 / D_MODEL * jnp.sum(sq), ("x", "y", "c"))

    dact = _matmul(dyb, w_dn_t, "nn", BF16, "d_act")
    d_w_dn = _matmul(act, dyb, "tn", F32, "d_w_down")
    dhg, dhu, dcwg, dcwu, dcbg, dcbu = _convglu_bwd(hid, dact, conv_w, ffn_conv_b, "convglu_bwd")
    d_w_up = jnp.concatenate([_matmul(hn2, dhg, "tn", F32, "d_w_up_gate"),
                              _matmul(hn2, dhu, "tn", F32, "d_w_up_up")], axis=1)
    dhn2 = _matmul(dhg, w_up_t[:D_FF], "nn", F32, "d_hn2_gate")
    dhn2 = _matmul(dhu, w_up_t[D_FF:], "nn", F32, "d_hn2_up", residual=dhn2)
    dx2, dx2b, d_norm2 = _rms_bwd(x2, norm2_g, dhn2, dy, "rms2_bwd")
    dmixed = _matmul(dx2b, w_o_t, "nn", F32, "d_mixed")
    d_w_o = _matmul(mixed, dx2b, "tn", F32, "d_w_out")
    dga, dgb, dya, dyb_b = _mix_bwd(dmixed, g_a, g_b, y_a, y_b, "mix_bwd")
    d_w_a = _matmul(out_a, dya, "tn", F32, "d_w_branch_a")
    d_w_b = _matmul(out_b, dyb_b, "tn", F32, "d_w_branch_b")
    do_a = _matmul(dya, w_a_t, "nn", BF16, "d_out_a")
    do_b = _matmul(dyb_b, w_b_t, "nn", BF16, "d_out_b")
    dqb, dkb, dvb = _sb_bwd(qb, kb, vb, out_b_f32, do_b, "stick_attn_bwd")
    dqa_n, dka_n, dva, dtab = _ca_bwd(qa, ka, va, do_a, tab, "chunk_attn_bwd")
    d_rel = _bias_table_bwd(dtab, "bias_table_bwd")[:, :N_REL]
    dqa, dka, dgq, dgk = _qknorm_bwd(qk, gq, gk, dqa_n, dka_n, "qknorm_bwd")
    dproj = jnp.concatenate([dqa, dka, dva, dqb, dkb, dvb, dga, dgb], axis=1)
    d_w_in = _matmul(hn, dproj, "tn", F32, "d_w_in")
    dhn = _matmul(dproj, w_in_t, "nn", F32, "d_hn")
    dx, _, d_norm1 = _rms_bwd(xs, norm1_g, dhn, dx2, "rms1_bwd")

    g_pack = _pack_full({"w_in": d_w_in, "w_branch_a": d_w_a, "w_branch_b": d_w_b, "w_out": d_w_o,
                         "w_ffn_up": d_w_up, "w_ffn_down": d_w_dn,
                         "ffn_conv_w": jnp.concatenate([dcwg, dcwu], axis=1)})
    small_g = _pack_small({"norm1_g": d_norm1, "q_norm_g": dgq.reshape(N_HEADS, HEAD_DIM).sum(0),
                           "k_norm_g": dgk.reshape(N_HEADS, HEAD_DIM).sum(0), "rel_bias": d_rel,
                           "norm2_g": d_norm2, "ffn_conv_b": jnp.concatenate([dcbg, dcbu], axis=1)})
    half_rows = g_pack.shape[1] // 2
    sib, small_all = _exchange_cores(g_pack, small_g, "exchange_cores")
    mine = lax.dynamic_slice_in_dim(g_pack, ci * half_rows, half_rows, axis=1)
    chip_part = _add2(mine, sib, "sum_cores")
    parts = _exchange_chips(chip_part, "exchange_chips")
    parts = lax.dynamic_update_slice(parts, lax.dynamic_index_in_dim(chip_part, chip, 0), (chip, 0, 0))
    g_half = _sum_leading(parts, "sum_chips")
    g_other = _share_halves(g_half, "share_halves")
    g_shard = jnp.concatenate([jnp.where(ci == 0, g_half, g_other), jnp.where(ci == 0, g_other, g_half)], axis=0)
    small_all = lax.dynamic_update_slice(small_all, small_g[None], (4 * xi + 2 * yi + ci, 0, 0))
    small_sum = _sum_leading(small_all, "sum_small")

    d_pack, m_pack, v_pack = _adamw(_pack_shards(w_big), g_shard, _pack_shards(m_big), _pack_shards(v_big),
                                    "adamw_big")
    grads = _unpack_shards(g_shard)
    deltas, new_m, new_v = _unpack_shards(d_pack), _unpack_shards(m_pack), _unpack_shards(v_pack)

    shapes = {"norm1_g": norm1_g.shape, "q_norm_g": q_norm_g.shape, "k_norm_g": k_norm_g.shape,
              "rel_bias": rel_bias.shape, "norm2_g": norm2_g.shape, "ffn_conv_b": ffn_conv_b.shape}
    small_w = {"norm1_g": norm1_g, "q_norm_g": q_norm_g, "k_norm_g": k_norm_g, "rel_bias": rel_bias,
               "norm2_g": norm2_g, "ffn_conv_b": ffn_conv_b}
    small_m = {"norm1_g": m_norm1_g, "q_norm_g": m_q_norm_g, "k_norm_g": m_k_norm_g, "rel_bias": m_rel_bias,
               "norm2_g": m_norm2_g, "ffn_conv_b": m_ffn_conv_b}
    small_v = {"norm1_g": v_norm1_g, "q_norm_g": v_q_norm_g, "k_norm_g": v_k_norm_g, "rel_bias": v_rel_bias,
               "norm2_g": v_norm2_g, "ffn_conv_b": v_ffn_conv_b}
    ds, ms, vs = _adamw(_pack_small(small_w), small_sum, _pack_small(small_m), _pack_small(small_v), "adamw_small")
    small_grads, ds, ms, vs = (_unpack_small(t) for t in (small_sum, ds, ms, vs))

    order = ("norm1_g", "w_in", "q_norm_g", "k_norm_g", "rel_bias", "w_branch_a", "w_branch_b", "w_out",
             "norm2_g", "w_ffn_up", "ffn_conv_w", "ffn_conv_b", "w_ffn_down")
    outs = [loss, dx[None]]
    for big, small in ((grads, small_grads), (deltas, ds), (new_m, ms), (new_v, vs)):
        for n in order:
            outs.append(big[n][None] if n in big else small[n].reshape(shapes[n]))
    return tuple(outs)
```

```python
import functools

import jax
import jax.numpy as jnp
from jax import lax
from jax.experimental import pallas as pl
from jax.experimental.pallas import tpu as pltpu

F32 = jnp.float32
BF16 = jnp.bfloat16
MESH = pl.DeviceIdType.MESH

D_MODEL = 1024
HEAD_DIM = 64
N_HEADS = 8
WIDTH = N_HEADS * HEAD_DIM
CHUNK = 64
LEFT_CHUNKS = 8
MAX_REL = 128
N_REL = 2 * MAX_REL + 1
D_FF = 2816
EPS = 1e-6
NEG = -1e30

ADAM_LR = 0.001
ADAM_B1 = 0.9
ADAM_B2 = 0.999
ADAM_EPS = 1e-08
ADAM_WD = 0.01
ADAM_STEP = 10

N_CHIPS = 4
N_DEV = 8
LANES = 128
PAIR = 2 * HEAD_DIM
BQ = 256
BAND = LEFT_CHUNKS * CHUNK
KWIN = BAND + BQ
VMEM_LIMIT = 56 * 1024 * 1024
PACK_COLS = 1024
SMALL_ROWS = 16

NN = (((1,), (0,)), ((), ()))
NT = (((1,), (1,)), ((), ()))
TN = (((0,), (0,)), ((), ()))


def _cparams(sem=None):
    if sem is None:
        return pltpu.CompilerParams(vmem_limit_bytes=VMEM_LIMIT)
    return pltpu.CompilerParams(dimension_semantics=sem, vmem_limit_bytes=VMEM_LIMIT)


def _pick(n, cands):
    for c in cands:
        if n % c == 0:
            return c
    raise ValueError(f"no block for {n}")


def _dot(a, b, dn):
    return lax.dot_general(a, b, dn, preferred_element_type=F32)


def _sigmoid(x):
    return 0.5 * jnp.tanh(0.5 * x) + 0.5


def _split_bf16(x):
    hi = x.astype(BF16)
    lo = (x - hi.astype(F32)).astype(BF16)
    return hi, lo


def _matmul(a, b, out_dtype, name, residual=None):
    (M, K), N = a.shape, b.shape[1]
    bk = K if K <= D_FF else _pick(K, (1024, 512))
    bm = 512 if D_MODEL < K <= D_FF else _pick(M, (1024, D_FF // 4, 512, 256, 128))
    bn = _pick(N, (D_FF // 2, 512, 256, 128))
    nk = K // bk
    dn = NN
    a_spec = pl.BlockSpec((bm, bk), lambda i, j, k: (i, k))
    b_spec = pl.BlockSpec((bk, bn), lambda i, j, k: (k, j))
    o_spec = pl.BlockSpec((bm, bn), lambda i, j, k: (i, j))
    has_res = residual is not None

    def body(*refs):
        if has_res:
            a_ref, b_ref, r_ref, o_ref, acc_ref = refs
        else:
            a_ref, b_ref, o_ref, acc_ref = refs
        k = pl.program_id(2)
        part = _dot(a_ref[...], b_ref[...], dn)

        def finish(total):
            if has_res:
                total = total + r_ref[...]
            o_ref[...] = total.astype(out_dtype)

        if nk == 1:
            finish(part)
        else:
            @pl.when(k == 0)
            def _():
                acc_ref[...] = part

            @pl.when(k > 0)
            def _():
                acc_ref[...] += part

            @pl.when(k == nk - 1)
            def _():
                finish(acc_ref[...])

    in_specs = [a_spec, b_spec] + ([o_spec] if has_res else [])
    args = (a, b) + ((residual,) if has_res else ())
    return pl.pallas_call(
        body, name=name,
        grid=(M // bm, N // bn, nk),
        in_specs=in_specs, out_specs=o_spec,
        out_shape=jax.ShapeDtypeStruct((M, N), out_dtype),
        scratch_shapes=[pltpu.VMEM((bm, bn) if nk > 1 else (8, LANES), F32)],
        compiler_params=_cparams(("parallel", "parallel", "arbitrary")),
    )(*args)


ROWS = 512


def _row_spec(cols, bm=ROWS):
    return pl.BlockSpec((bm, cols), lambda i: (i, 0))


def _col_spec(rows, bn=ROWS):
    return pl.BlockSpec((rows, bn), lambda i: (0, i))


def _full_spec(shape):
    return pl.BlockSpec(shape, lambda i: (0,) * len(shape))


def _colsum8(t):
    return jnp.sum(t.reshape(t.shape[0] // 8, 8, t.shape[1]), axis=0)


def _rms_fwd(x, g, name):
    S, D = x.shape

    def body(x_ref, g_ref, o_ref, ot_ref):
        xv = x_ref[...]
        r = lax.rsqrt(jnp.mean(xv * xv, axis=-1, keepdims=True) + EPS)
        y = xv * r * g_ref[...]
        o_ref[...] = y.astype(BF16)
        ot_ref[...] = y.T.astype(BF16)

    return pl.pallas_call(
        body, name=name, grid=(S // ROWS,),
        in_specs=[_row_spec(D), _full_spec((1, D))], out_specs=[_row_spec(D), _col_spec(D)],
        out_shape=[jax.ShapeDtypeStruct((S, D), BF16), jax.ShapeDtypeStruct((D, S), BF16)],
        compiler_params=_cparams(("parallel",)),
    )(x, g)


def _rms_bwd(x, g, dy, dres, name):
    S, D = x.shape
    nt = S // ROWS

    def body(x_ref, g_ref, dy_ref, dres_ref, dx_ref, dxb_ref, dg_ref, acc_ref):
        i = pl.program_id(0)
        xv, dyv = x_ref[...], dy_ref[...]
        r = lax.rsqrt(jnp.mean(xv * xv, axis=-1, keepdims=True) + EPS)
        xr = xv * r
        u = dyv * g_ref[...]
        dx = r * u - xr * (r * r) * jnp.mean(xv * u, axis=-1, keepdims=True) + dres_ref[...]
        dx_ref[...] = dx
        dxb_ref[...] = dx.astype(BF16)
        part = _colsum8(dyv * xr)

        @pl.when(i == 0)
        def _():
            acc_ref[...] = part

        @pl.when(i > 0)
        def _():
            acc_ref[...] += part

        @pl.when(i == nt - 1)
        def _():
            dg_ref[...] = jnp.sum(acc_ref[...], axis=0, keepdims=True)

    return pl.pallas_call(
        body, name=name, grid=(nt,),
        in_specs=[_row_spec(D), _full_spec((1, D)), _row_spec(D), _row_spec(D)],
        out_specs=[_row_spec(D), _row_spec(D), _full_spec((1, D))],
        out_shape=[jax.ShapeDtypeStruct((S, D), F32), jax.ShapeDtypeStruct((S, D), BF16),
                   jax.ShapeDtypeStruct((1, D), F32)],
        scratch_shapes=[pltpu.VMEM((8, D), F32)],
        compiler_params=_cparams(("arbitrary",)),
    )(x, g, dy, dres)


def _head_mean(t, blockdiag):
    hi, lo = _split_bf16(t)
    return (_dot(hi, blockdiag, NN) + _dot(lo, blockdiag, NN)) * (1.0 / HEAD_DIM)


def _blockdiag():
    r = lax.broadcasted_iota(jnp.int32, (WIDTH, WIDTH), 0) // HEAD_DIM
    c = lax.broadcasted_iota(jnp.int32, (WIDTH, WIDTH), 1) // HEAD_DIM
    return jnp.where(r == c, 1.0, 0.0).astype(BF16)


def _qknorm_fwd(qk, gq, gk, name):
    S = qk.shape[0]

    def body(qk_ref, gq_ref, gk_ref, q_ref, k_ref):
        bd = _blockdiag()
        for part, g_ref, o_ref, scale in ((0, gq_ref, q_ref, HEAD_DIM ** -0.5), (1, gk_ref, k_ref, 1.0)):
            t = qk_ref[:, part * WIDTH:(part + 1) * WIDTH]
            r = lax.rsqrt(_head_mean(t * t, bd) + EPS)
            o_ref[...] = (t * r * g_ref[...] * scale).astype(BF16)

    return pl.pallas_call(
        body, name=name, grid=(S // ROWS,),
        in_specs=[_row_spec(2 * WIDTH), _full_spec((1, WIDTH)), _full_spec((1, WIDTH))],
        out_specs=[_row_spec(WIDTH), _row_spec(WIDTH)],
        out_shape=[jax.ShapeDtypeStruct((S, WIDTH), BF16)] * 2,
        compiler_params=_cparams(("parallel",)),
    )(qk, gq, gk)


def _qknorm_bwd(qk, gq, gk, dqn, dkn, name):
    S = qk.shape[0]
    nt = S // ROWS

    def body(qk_ref, gq_ref, gk_ref, dqn_ref, dkn_ref, dq_ref, dk_ref, dgq_ref, dgk_ref, accq_ref, acck_ref):
        i = pl.program_id(0)
        bd = _blockdiag()
        for part, g_ref, dn_ref, o_ref, dg_ref, acc_ref, scale in (
                (0, gq_ref, dqn_ref, dq_ref, dgq_ref, accq_ref, HEAD_DIM ** -0.5),
                (1, gk_ref, dkn_ref, dk_ref, dgk_ref, acck_ref, 1.0)):
            t = qk_ref[:, part * WIDTH:(part + 1) * WIDTH]
            dn = dn_ref[...] * scale
            r = lax.rsqrt(_head_mean(t * t, bd) + EPS)
            u = dn * g_ref[...]
            dt = r * u - t * (r * r * r) * _head_mean(t * u, bd)
            o_ref[...] = dt.astype(BF16)
            psum = _colsum8(dn * t * r)

            @pl.when(i == 0)
            def _():
                acc_ref[...] = psum

            @pl.when(i > 0)
            def _():
                acc_ref[...] += psum

            @pl.when(i == nt - 1)
            def _():
                dg_ref[...] = jnp.sum(acc_ref[...], axis=0, keepdims=True)

    return pl.pallas_call(
        body, name=name, grid=(nt,),
        in_specs=[_row_spec(2 * WIDTH), _full_spec((1, WIDTH)), _full_spec((1, WIDTH)),
                  _row_spec(WIDTH), _row_spec(WIDTH)],
        out_specs=[_row_spec(WIDTH), _row_spec(WIDTH), _full_spec((1, WIDTH)), _full_spec((1, WIDTH))],
        out_shape=[jax.ShapeDtypeStruct((S, WIDTH), BF16)] * 2 + [jax.ShapeDtypeStruct((1, WIDTH), F32)] * 2,
        scratch_shapes=[pltpu.VMEM((8, WIDTH), F32)] * 2,
        compiler_params=_cparams(("arbitrary",)),
    )(qk, gq, gk, dqn, dkn)


def _mix_fwd(ga, gb, ya, yb, name):
    S, D = ga.shape

    def body(ga_ref, gb_ref, ya_ref, yb_ref, o_ref, ot_ref):
        m = _sigmoid(ga_ref[...]) * ya_ref[...] + _sigmoid(gb_ref[...]) * yb_ref[...]
        o_ref[...] = m.astype(BF16)
        ot_ref[...] = m.T.astype(BF16)

    return pl.pallas_call(
        body, name=name, grid=(S // ROWS,),
        in_specs=[_row_spec(D)] * 4, out_specs=[_row_spec(D), _col_spec(D)],
        out_shape=[jax.ShapeDtypeStruct((S, D), BF16), jax.ShapeDtypeStruct((D, S), BF16)],
        compiler_params=_cparams(("parallel",)),
    )(ga, gb, ya, yb)


def _mix_bwd(dm, ga, gb, ya, yb, name):
    S, D = ga.shape

    def body(dm_ref, ga_ref, gb_ref, ya_ref, yb_ref, dga_ref, dgb_ref, dya_ref, dyb_ref):
        dmv = dm_ref[...]
        for g_ref, y_ref, dg_ref, dy_ref in ((ga_ref, ya_ref, dga_ref, dya_ref), (gb_ref, yb_ref, dgb_ref, dyb_ref)):
            s = _sigmoid(g_ref[...])
            dy_ref[...] = (dmv * s).astype(BF16)
            dg_ref[...] = (dmv * y_ref[...] * s * (1.0 - s)).astype(BF16)

    return pl.pallas_call(
        body, name=name, grid=(S // ROWS,),
        in_specs=[_row_spec(D)] * 5, out_specs=[_row_spec(D)] * 4,
        out_shape=[jax.ShapeDtypeStruct((S, D), BF16)] * 4,
        compiler_params=_cparams(("parallel",)),
    )(dm, ga, gb, ya, yb)


def _loss_head(y, target, name):
    S, D = y.shape
    nt = S // ROWS

    def body(y_ref, t_ref, dy_ref, dyb_ref, p_ref):
        err = y_ref[...] - t_ref[...]
        dy = err * (1.0 / D)
        dy_ref[...] = dy
        dyb_ref[...] = dy.astype(BF16)
        sq = _colsum8(err * err)
        acc = sq[:, 0:LANES]
        for k in range(1, D // LANES):
            acc = acc + sq[:, k * LANES:(k + 1) * LANES]
        p_ref[...] = acc

    return pl.pallas_call(
        body, name=name, grid=(nt,),
        in_specs=[_row_spec(D)] * 2,
        out_specs=[_row_spec(D), _row_spec(D), pl.BlockSpec((8, LANES), lambda i: (i, 0))],
        out_shape=[jax.ShapeDtypeStruct((S, D), F32), jax.ShapeDtypeStruct((S, D), BF16),
                   jax.ShapeDtypeStruct((nt * 8, LANES), F32)],
        compiler_params=_cparams(("parallel",)),
    )(y, target)


CONV_COLS = 256
HALO = 16


def _conv_taps(xe, cw, cb):
    taps = (pltpu.roll(xe, 2, 0), pltpu.roll(xe, 1, 0), xe)
    return taps, cw[0:1] * taps[0] + cw[1:2] * taps[1] + cw[2:3] * taps[2] + cb


def _conv_specs(nt):
    hb, nb = ROWS // HALO, D_FF // CONV_COLS
    specs = {}
    for part, off in (("gate", 0), ("up", nb)):
        specs[part] = dict(
            main=pl.BlockSpec((ROWS, CONV_COLS), functools.partial(lambda c, i, off: (i, c + off), off=off)),
            prev=pl.BlockSpec((HALO, CONV_COLS),
                              functools.partial(lambda c, i, off: (jnp.maximum(i * hb - 1, 0), c + off), off=off)),
            nxt=pl.BlockSpec((HALO, CONV_COLS),
                             functools.partial(lambda c, i, off: (jnp.minimum((i + 1) * hb, nt * hb - 1), c + off), off=off)),
            w=pl.BlockSpec((3, CONV_COLS), functools.partial(lambda c, i, off: (0, c + off), off=off)),
            b=pl.BlockSpec((1, CONV_COLS), functools.partial(lambda c, i, off: (0, c + off), off=off)))
    return specs


def _convglu_fwd(hid, cw, cb, name):
    S = hid.shape[0]
    sp = _conv_specs(S // ROWS)

    def body(hg_ref, hgp_ref, hu_ref, hup_ref, cwg_ref, cwu_ref, cbg_ref, cbu_ref, o_ref, ot_ref):
        i = pl.program_id(1)
        keep = (i > 0).astype(F32)

        def conv(h_ref, hp_ref, cw_ref, cb_ref):
            xe = jnp.concatenate([hp_ref[...].astype(F32) * keep, h_ref[...].astype(F32)], axis=0)
            return _conv_taps(xe, cw_ref[...], cb_ref[...])[1][HALO:, :]

        gate = conv(hg_ref, hgp_ref, cwg_ref, cbg_ref)
        up = conv(hu_ref, hup_ref, cwu_ref, cbu_ref)
        act = gate * _sigmoid(gate) * up
        o_ref[...] = act.astype(BF16)
        ot_ref[...] = act.T.astype(BF16)

    g, u = sp["gate"], sp["up"]
    return pl.pallas_call(
        body, name=name, grid=(D_FF // CONV_COLS, S // ROWS),
        in_specs=[g["main"], g["prev"], u["main"], u["prev"], g["w"], u["w"], g["b"], u["b"]],
        out_specs=[g["main"], pl.BlockSpec((CONV_COLS, ROWS), lambda c, i: (c, i))],
        out_shape=[jax.ShapeDtypeStruct((S, D_FF), BF16), jax.ShapeDtypeStruct((D_FF, S), BF16)],
        compiler_params=_cparams(("parallel", "parallel")),
    )(hid, hid, hid, hid, cw, cw, cb, cb)


def _convglu_bwd(hid, dact, cw, cb, name):
    S = hid.shape[0]
    nt = S // ROWS
    sp = _conv_specs(nt)

    def body(hg_ref, hgp_ref, hgn_ref, hu_ref, hup_ref, hun_ref, da_ref, dan_ref,
             cwg_ref, cwu_ref, cbg_ref, cbu_ref,
             dhg_ref, dhu_ref, dcwg_ref, dcwu_ref, dcbg_ref, dcbu_ref):
        i = pl.program_id(1)
        kp = (i > 0).astype(F32)
        kn = (i < nt - 1).astype(F32)

        def ext(h_ref, hp_ref, hn_ref):
            return jnp.concatenate([hp_ref[...].astype(F32) * kp, h_ref[...].astype(F32),
                                    hn_ref[...].astype(F32) * kn], axis=0)

        taps_g, gate = _conv_taps(ext(hg_ref, hgp_ref, hgn_ref), cwg_ref[...], cbg_ref[...])
        taps_u, up = _conv_taps(ext(hu_ref, hup_ref, hun_ref), cwu_ref[...], cbu_ref[...])
        gate, up = gate[HALO:, :], up[HALO:, :]
        da = jnp.concatenate([da_ref[...].astype(F32), dan_ref[...].astype(F32) * kn], axis=0)
        sg = _sigmoid(gate)
        dgate = da * up * sg * (1.0 + gate * (1.0 - sg))
        dup = da * gate * sg
        n_ext = ROWS + HALO

        @pl.when(i == 0)
        def _():
            for ref in (dcwg_ref, dcwu_ref, dcbg_ref, dcbu_ref):
                ref[...] = jnp.zeros_like(ref)

        for d, cw_ref, taps, dh_ref, dcw_ref, dcb_ref in ((dgate, cwg_ref, taps_g, dhg_ref, dcwg_ref, dcbg_ref),
                                                         (dup, cwu_ref, taps_u, dhu_ref, dcwu_ref, dcbu_ref)):
            cwv = cw_ref[...]
            dh = cwv[2:3] * d + cwv[1:2] * pltpu.roll(d, n_ext - 1, 0) + cwv[0:1] * pltpu.roll(d, n_ext - 2, 0)
            dh_ref[...] = dh[:ROWS, :].astype(BF16)
            dc = d[:ROWS, :]
            for t in range(3):
                dcw_ref[t:t + 1, :] += jnp.sum(dc * taps[t][HALO:HALO + ROWS, :], axis=0, keepdims=True)
            dcb_ref[...] += jnp.sum(dc, axis=0, keepdims=True)

    g, u = sp["gate"], sp["up"]
    return pl.pallas_call(
        body, name=name, grid=(D_FF // CONV_COLS, nt),
        in_specs=[g["main"], g["prev"], g["nxt"], u["main"], u["prev"], u["nxt"], g["main"], g["nxt"],
                  g["w"], u["w"], g["b"], u["b"]],
        out_specs=[g["main"], g["main"], g["w"], g["w"], g["b"], g["b"]],
        out_shape=[jax.ShapeDtypeStruct((S, D_FF), BF16)] * 2 + [jax.ShapeDtypeStruct((3, D_FF), F32)] * 2
        + [jax.ShapeDtypeStruct((1, D_FF), F32)] * 2,
        compiler_params=_cparams(("parallel", "arbitrary")),
    )(hid, hid, hid, hid, hid, hid, dact, dact, cw, cw, cb, cb)


REL_PAD = 384
DIAG = 1024


def _band_valid():
    qc = lax.broadcasted_iota(jnp.int32, (BQ, KWIN), 0) // CHUNK
    kc = lax.broadcasted_iota(jnp.int32, (BQ, KWIN), 1) // CHUNK - LEFT_CHUNKS
    return (kc <= qc) & (kc >= qc - LEFT_CHUNKS)


def _rel_index(offset):
    return jnp.clip(BAND - offset, -MAX_REL, MAX_REL) + MAX_REL


def _split3(x):
    hi = x.astype(BF16)
    rest = x - hi.astype(F32)
    mid = rest.astype(BF16)
    return hi, mid, (rest - mid.astype(F32)).astype(BF16)


def _bias_table(rel_bias, name):
    def body(rb_ref, o_ref):
        t = lax.broadcasted_iota(jnp.int32, (REL_PAD, DIAG), 0)
        lane = lax.broadcasted_iota(jnp.int32, (REL_PAD, DIAG), 1)
        pick = jnp.where(t == _rel_index(lane - BQ), 1.0, 0.0).astype(BF16)
        base = sum(_dot(piece, pick, NN) for piece in _split3(rb_ref[...]))
        valid = _band_valid()
        for h in range(N_HEADS):
            rows = jnp.broadcast_to(base[h:h + 1], (BQ, DIAG))
            rolled = pltpu.roll(rows, 0, 1, stride=1, stride_axis=0)
            o_ref[h] = jnp.where(valid, rolled[:, BQ:], NEG)

    return pl.pallas_call(
        body, name=name,
        out_shape=jax.ShapeDtypeStruct((N_HEADS, BQ, KWIN), F32),
        compiler_params=_cparams(),
    )(rel_bias)


def _bias_table_bwd(dtab, name):
    def body(d_ref, o_ref, diag_ref):
        r = lax.broadcasted_iota(jnp.int32, (BQ, BQ), 0)
        c = lax.broadcasted_iota(jnp.int32, (BQ, BQ), 1)
        flip = jnp.where(r + c == BQ - 1, 1.0, 0.0).astype(BF16)
        for h in range(N_HEADS):
            flipped = sum(_dot(flip, piece, NN) for piece in _split3(d_ref[h]))
            padded = jnp.concatenate([flipped, jnp.zeros((BQ, DIAG - KWIN), F32)], axis=1)
            rolled = pltpu.roll(padded, DIAG - (BQ - 1), 1, stride=1, stride_axis=0)
            diag_ref[h:h + 1, :] = jnp.sum(rolled, axis=0, keepdims=True)
        lane = lax.broadcasted_iota(jnp.int32, (DIAG, REL_PAD), 0)
        t = lax.broadcasted_iota(jnp.int32, (DIAG, REL_PAD), 1)
        offset = jnp.where(lane < KWIN, lane, lane - DIAG)
        pick = jnp.where(t == _rel_index(offset), 1.0, 0.0).astype(BF16)
        o_ref[...] = sum(_dot(piece, pick, NN) for piece in _split3(diag_ref[...]))

    return pl.pallas_call(
        body, name=name,
        out_shape=jax.ShapeDtypeStruct((N_HEADS, REL_PAD), F32),
        scratch_shapes=[pltpu.VMEM((N_HEADS, DIAG), F32)],
        compiler_params=_cparams(),
    )(dtab)


def _head_masks():
    lane = lax.broadcasted_iota(jnp.int32, (1, PAIR), 1)
    return [lane // HEAD_DIM == h for h in range(2)]


def _ca_window_specs(nq, col_off=0):
    return [pl.BlockSpec((BQ, PAIR), functools.partial(
        lambda p, i, d: (jnp.clip(i - 2 + d, 0, nq - 1), p + col_off), d=d)) for d in range(3)]


def _softmax_rows(s):
    p = jnp.exp(s - jnp.max(s, axis=-1, keepdims=True))
    return p, jnp.sum(p, axis=-1, keepdims=True)


def _ca_scores(qm, kc, tab_h, i):
    col = lax.broadcasted_iota(jnp.int32, (1, KWIN), 1)
    in_seq = col + (i - 2) * BQ >= 0
    return jnp.where(in_seq, _dot(qm, kc, NT) + tab_h, NEG)


def _ca_fwd(qn, kn, v, tab, name, v_off=0):
    S = qn.shape[0]
    nq = S // BQ
    qspec = pl.BlockSpec((BQ, PAIR), lambda p, i: (i, p))
    tspec = pl.BlockSpec((2, BQ, KWIN), lambda p, i: (p, 0, 0))

    def body(q_ref, k0, k1, k2, v0, v1, v2, tab_ref, o_ref, ot_ref):
        i = pl.program_id(1)
        kc = jnp.concatenate([k0[...], k1[...], k2[...]], axis=0)
        vc = jnp.concatenate([v0[...], v1[...], v2[...]], axis=0)
        qv = q_ref[...]
        masks = _head_masks()
        heads = range(2)
        s = [_ca_scores(jnp.where(masks[h], qv, 0), kc, tab_ref[h], i) for h in heads]
        soft = [_softmax_rows(s[h]) for h in heads]
        o = [_dot(soft[h][0].astype(BF16), vc, NN) / soft[h][1] for h in heads]
        out = jnp.where(masks[0], o[0], o[1])
        o_ref[...] = out.astype(BF16)
        ot_ref[...] = out.T.astype(BF16)

    return pl.pallas_call(
        body, name=name, grid=(WIDTH // PAIR, nq),
        in_specs=[qspec] + _ca_window_specs(nq) + _ca_window_specs(nq, v_off) + [tspec],
        out_specs=[qspec, pl.BlockSpec((PAIR, BQ), lambda p, i: (p, i))],
        out_shape=[jax.ShapeDtypeStruct((S, WIDTH), BF16), jax.ShapeDtypeStruct((WIDTH, S), BF16)],
        compiler_params=_cparams(("parallel", "parallel")),
    )(qn, kn, kn, kn, v, v, v, tab)


def _ca_bwd(qn, kn, v, do, tab, name, v_off=0):
    S = qn.shape[0]
    nq = S // BQ
    qspec = pl.BlockSpec((BQ, PAIR), lambda p, i: (jnp.minimum(i, nq - 1), p))
    kout = pl.BlockSpec((BQ, PAIR), lambda p, i: (jnp.clip(i - 2, 0, nq - 1), p))
    tspec = pl.BlockSpec((2, BQ, KWIN), lambda p, i: (p, 0, 0))

    def body(q_ref, do_ref, k0, k1, k2, v0, v1, v2, tab_ref,
             dq_ref, dk_ref, dv_ref, dtab_ref, dk_acc, dv_acc):
        i = pl.program_id(1)

        @pl.when(i == 0)
        def _():
            dk_acc[...] = jnp.zeros_like(dk_acc)
            dv_acc[...] = jnp.zeros_like(dv_acc)
            dtab_ref[...] = jnp.zeros_like(dtab_ref)

        @pl.when(i < nq)
        def _():
            kc = jnp.concatenate([k0[...], k1[...], k2[...]], axis=0)
            vc = jnp.concatenate([v0[...], v1[...], v2[...]], axis=0)
            qv, dov = q_ref[...], do_ref[...]
            masks = _head_masks()
            heads = range(2)
            qm = [jnp.where(masks[h], qv, 0) for h in heads]
            dom = [jnp.where(masks[h], dov, 0) for h in heads]
            s = [_ca_scores(qm[h], kc, tab_ref[h], i) for h in heads]
            dp = [_dot(dom[h], vc, NT) for h in heads]
            soft = [_softmax_rows(s[h]) for h in heads]
            p = [soft[h][0] / soft[h][1] for h in heads]
            ds = [p[h] * (dp[h] - jnp.sum(p[h] * dp[h], axis=-1, keepdims=True)) for h in heads]
            for h in heads:
                dtab_ref[h] += ds[h]
            dsb = [ds[h].astype(BF16) for h in heads]
            pb = [p[h].astype(BF16) for h in heads]
            dq = [_dot(dsb[h], kc, NN) for h in heads]
            dq_ref[...] = jnp.where(masks[0], dq[0], dq[1])
            dkc = _dot(dsb[0], qm[0], TN) + _dot(dsb[1], qm[1], TN)
            dvc = _dot(pb[0], dom[0], TN) + _dot(pb[1], dom[1], TN)
            for d in range(3):
                slot = (i + 1 + d) % 3
                dk_acc[slot] += dkc[d * BQ:(d + 1) * BQ]
                dv_acc[slot] += dvc[d * BQ:(d + 1) * BQ]

        @pl.when(i >= 2)
        def _():
            slot = (i + 1) % 3
            dk_ref[...] = dk_acc[slot]
            dv_ref[...] = dv_acc[slot].astype(BF16)
            dk_acc[slot] = jnp.zeros((BQ, PAIR), F32)
            dv_acc[slot] = jnp.zeros((BQ, PAIR), F32)

    return pl.pallas_call(
        body, name=name, grid=(WIDTH // PAIR, nq + 2),
        in_specs=[qspec, qspec] + _ca_window_specs(nq) + _ca_window_specs(nq, v_off) + [tspec],
        out_specs=[qspec, kout, kout, tspec],
        out_shape=[jax.ShapeDtypeStruct((S, WIDTH), F32), jax.ShapeDtypeStruct((S, WIDTH), F32),
                   jax.ShapeDtypeStruct((S, WIDTH), BF16), jax.ShapeDtypeStruct((N_HEADS, BQ, KWIN), F32)],
        scratch_shapes=[pltpu.VMEM((3, BQ, PAIR), F32)] * 2,
        compiler_params=_cparams(("parallel", "arbitrary")),
    )(qn, do, kn, kn, kn, v, v, v, tab)


def _sb_consts():
    r = lax.broadcasted_iota(jnp.int32, (BQ, BQ), 0)
    c = lax.broadcasted_iota(jnp.int32, (BQ, BQ), 1)
    from_s = jnp.where(r >= c, 1.0, 0.0).astype(BF16)
    causal = c < r
    return from_s, causal


def _suffix_sum(t, from_s):
    hi, lo = _split_bf16(t)
    return _dot(hi, from_s, NN) + _dot(lo, from_s, NN)


def _neg_abs(x):
    bits = lax.bitcast_convert_type(x, jnp.uint32) | jnp.uint32(0x80000000)
    return lax.bitcast_convert_type(bits, F32)


def _sb_log_keep(zn):
    return jnp.minimum(zn, 0.0) - jnp.log(1.0 + jnp.exp(_neg_abs(zn)))


SB_DEAD = 105.0


def _sb_walk(i, tiles, keep_ref):
    @pl.when(i == 0)
    def _():
        tiles([i], [True])

    @pl.when(i > 0)
    def _():
        tiles([i, i - 1], [True, False])

    def alive():
        return (jnp.max(keep_ref[...]) > -SB_DEAD).astype(jnp.int32)

    def step(state):
        j, _ = state
        tiles([j], [False])
        return j - 1, alive()

    lax.while_loop(lambda state: (state[0] >= 0) & (state[1] > 0), step, (i - 2, alive()))


def _sb_rows(j):
    return pl.ds(pl.multiple_of(j * BQ, BQ), BQ)


def _sb_specs(S, offs):
    def qspec(off=0):
        return pl.BlockSpec((BQ, PAIR), lambda p, i: (i, p + off))

    def kspec(off=0):
        return pl.BlockSpec((S, PAIR), lambda p, i: (0, p + off))

    return qspec, kspec, [qspec(offs[0]), kspec(offs[1]), kspec(offs[2])]


def _sb_fwd(q, k, v, name, offs=(0, 0, 0)):
    S = q.shape[0]
    nq = S // BQ
    qspec, _, qkv_specs = _sb_specs(S, offs)

    def body(q_ref, k_ref, v_ref, o_ref, of_ref, ot_ref, carry_ref, acc_ref):
        i = pl.program_id(1)
        from_s, causal = _sb_consts()
        masks = _head_masks()
        qn = q_ref[...] * -(HEAD_DIM ** -0.5)
        qms = [jnp.where(m, qn, 0) for m in masks]
        carry_ref[...] = jnp.zeros_like(carry_ref)
        acc_ref[...] = jnp.zeros_like(acc_ref)

        def tiles(js, diags):
            chains = [(n, h) for n in range(len(js)) for h in range(2)]
            masked = [c for c in chains if diags[c[0]]]
            kbs = [k_ref[_sb_rows(j), :] for j in js]
            vbs = [v_ref[_sb_rows(j), :] for j in js]
            zn = {c: _dot(qms[c[1]], kbs[c[0]], NT) for c in chains}
            log_keep = {c: _sb_log_keep(zn[c]) for c in chains}
            for c in masked:
                log_keep[c] = jnp.where(causal, log_keep[c], 0.0)
            split = {c: _split_bf16(log_keep[c]) for c in chains}
            carry = {}
            for h in range(2):
                run = carry_ref[h]
                for n in range(len(js)):
                    carry[(n, h)] = run
                    run = run + jnp.sum(log_keep[(n, h)], axis=-1, keepdims=True)
                carry_ref[h] = run
            suffix = {c: _dot(split[c][0], from_s, NN) + _dot(split[c][1], from_s, NN) for c in chains}
            w = {c: jnp.exp(carry[c] + suffix[c] - zn[c]) for c in chains}
            for c in masked:
                w[c] = jnp.where(causal, w[c], 0.0)
            for c in chains:
                acc_ref[c[1]] += _dot(w[c].astype(BF16), vbs[c[0]], NN)

        _sb_walk(i, tiles, carry_ref)
        out = jnp.where(masks[0], acc_ref[0], acc_ref[1])
        o_ref[...] = out.astype(BF16)
        of_ref[...] = out
        ot_ref[...] = out.T.astype(BF16)

    return pl.pallas_call(
        body, name=name, grid=(WIDTH // PAIR, nq),
        in_specs=qkv_specs, out_specs=[qspec(), qspec(), pl.BlockSpec((PAIR, BQ), lambda p, i: (p, i))],
        out_shape=[jax.ShapeDtypeStruct((S, WIDTH), BF16), jax.ShapeDtypeStruct((S, WIDTH), F32),
                   jax.ShapeDtypeStruct((WIDTH, S), BF16)],
        scratch_shapes=[pltpu.VMEM((2, BQ, 1), F32), pltpu.VMEM((2, BQ, PAIR), F32)],
        compiler_params=_cparams(("parallel", "arbitrary")),
    )(q, k, v)


def _sb_bwd(q, k, v, o, do, name, offs=(0, 0, 0)):
    S = q.shape[0]
    nq = S // BQ
    qspec, kspec, qkv_specs = _sb_specs(S, offs)

    def body(q_ref, o_ref, do_ref, k_ref, v_ref, dq_ref, dk_ref, dv_ref, dk_acc, dv_acc, keep_ref, gsum_ref, dq_acc):
        i = pl.program_id(1)

        @pl.when(i == 0)
        def _():
            dk_acc[...] = jnp.zeros_like(dk_acc)
            dv_acc[...] = jnp.zeros_like(dv_acc)

        from_s, causal = _sb_consts()
        masks = _head_masks()
        qn, dov = q_ref[...] * -(HEAD_DIM ** -0.5), do_ref[...]
        od = o_ref[...] * dov.astype(F32)
        qms = [jnp.where(m, qn, 0) for m in masks]
        doms = [jnp.where(m, dov, 0) for m in masks]
        totals = [jnp.sum(jnp.where(m, od, 0.0), axis=-1, keepdims=True) for m in masks]
        for ref in (keep_ref, gsum_ref, dq_acc):
            ref[...] = jnp.zeros_like(ref)

        def running(ref, vals, n_blocks):
            before_chain = {}
            for h in range(2):
                run = ref[h]
                for n in range(n_blocks):
                    before_chain[(n, h)] = run
                    run = run + jnp.sum(vals[(n, h)], axis=-1, keepdims=True)
                ref[h] = run
            return before_chain

        def tiles(js, diags):
            chains = [(n, h) for n in range(len(js)) for h in range(2)]
            masked = [c for c in chains if diags[c[0]]]
            kbs = [k_ref[_sb_rows(j), :] for j in js]
            vbs = [v_ref[_sb_rows(j), :] for j in js]
            zn = {c: _dot(qms[c[1]], kbs[c[0]], NT) for c in chains}
            dw = {c: _dot(doms[c[1]], vbs[c[0]], NT) for c in chains}
            log_keep = {c: _sb_log_keep(zn[c]) for c in chains}
            for c in masked:
                log_keep[c] = jnp.where(causal, log_keep[c], 0.0)
            split = {c: _split_bf16(log_keep[c]) for c in chains}
            kept = running(keep_ref, log_keep, len(js))
            suffix = {c: _dot(split[c][0], from_s, NN) + _dot(split[c][1], from_s, NN) for c in chains}
            w = {c: jnp.exp(kept[c] + suffix[c] - zn[c]) for c in chains}
            for c in masked:
                w[c] = jnp.where(causal, w[c], 0.0)
            wb = {c: w[c].astype(BF16) for c in chains}
            g = {c: wb[c].astype(F32) * dw[c] for c in chains}
            gsplit = {c: _split_bf16(g[c]) for c in chains}
            gsum = running(gsum_ref, g, len(js))
            gsuffix = {c: _dot(gsplit[c][0], from_s, NN) + _dot(gsplit[c][1], from_s, NN) for c in chains}
            dzb = {}
            for c in chains:
                before = totals[c[1]] - (gsum[c] + gsuffix[c])
                dz = (g[c] + before) * jnp.exp(log_keep[c]) - before
                if c in masked:
                    dz = jnp.where(causal, dz, 0.0)
                dzb[c] = dz.astype(BF16)
            for c in chains:
                rows = _sb_rows(js[c[0]])
                dq_acc[c[1]] += _dot(dzb[c], kbs[c[0]], NN)
                dk_acc[rows, :] -= _dot(dzb[c], qms[c[1]], TN)
                dv_acc[rows, :] += _dot(wb[c], doms[c[1]], TN)

        _sb_walk(i, tiles, keep_ref)
        dq_ref[...] = (jnp.where(masks[0], dq_acc[0], dq_acc[1]) * HEAD_DIM ** -0.5).astype(BF16)

        @pl.when(i == nq - 1)
        def _():
            dk_ref[...] = dk_acc[...].astype(BF16)
            dv_ref[...] = dv_acc[...].astype(BF16)

    return pl.pallas_call(
        body, name=name, grid=(WIDTH // PAIR, nq),
        in_specs=[qkv_specs[0], qspec(), qspec(), qkv_specs[1], qkv_specs[2]], out_specs=[qspec(), kspec(), kspec()],
        out_shape=[jax.ShapeDtypeStruct((S, WIDTH), BF16)] * 3,
        scratch_shapes=[pltpu.VMEM((S, PAIR), F32)] * 2 + [pltpu.VMEM((2, BQ, 1), F32)] * 2
        + [pltpu.VMEM((2, BQ, PAIR), F32)],
        compiler_params=_cparams(("parallel", "arbitrary")),
    )(q, o, do, k, v)


ANY = pl.BlockSpec(memory_space=pl.ANY)


def _place():
    return lax.axis_index("x"), lax.axis_index("y"), lax.axis_index("c")


def _other_chips(x, y):
    return [(2 * px + py, (px, py)) for px, py in ((1 - x, y), (x, 1 - y), (1 - x, 1 - y))]


def _remote(src, dst, sems, k, to):
    return pltpu.make_async_remote_copy(src_ref=src, dst_ref=dst, send_sem=sems[0].at[k], recv_sem=sems[1].at[k],
                                        device_id=to, device_id_type=MESH)


def _gather_weights(wp, name):
    R = wp.shape[0]
    Rh = R // 2

    def body(wp_ref, out_ref, send_sems, recv_sems):
        x, y, c = _place()
        me = 2 * x + y
        sems = (send_sems, recv_sems)
        half = pl.ds(c * Rh, Rh)
        chips = _other_chips(x, y)
        first = [_remote(wp_ref.at[half], out_ref.at[me, half], sems, k, (*xy, c)) for k, (_, xy) in enumerate(chips)]
        for cp in first:
            cp.start()
        passed = []
        for k, (chip, xy) in enumerate(chips):
            landed = out_ref.at[chip, half]
            _remote(landed, landed, sems, k, (*xy, c)).wait_recv()
            cp = _remote(landed, landed, sems, 3 + k, (x, y, 1 - c))
            cp.start()
            passed.append(cp)
        other = pl.ds((1 - c) * Rh, Rh)
        for k, (chip, xy) in enumerate(chips):
            landed = out_ref.at[chip, other]
            _remote(landed, landed, sems, 3 + k, (x, y, 1 - c)).wait_recv()
        for cp in first + passed:
            cp.wait_send()

    return pl.pallas_call(
        body, name=name, in_specs=[ANY], out_specs=ANY,
        out_shape=jax.ShapeDtypeStruct((N_CHIPS, R, PACK_COLS), wp.dtype),
        scratch_shapes=[pltpu.SemaphoreType.DMA((6,)), pltpu.SemaphoreType.DMA((6,))],
    )(wp)


def _exchange_cores(g, small, name):
    R = g.shape[1]
    Rh = R // 2

    def body(g_ref, small_ref, sib_ref, all_ref, send_sems, recv_sems):
        x, y, c = _place()
        me = 4 * x + 2 * y + c
        sems = (send_sems, recv_sems)
        theirs = pl.ds((1 - c) * Rh, Rh)
        copies = [_remote(g_ref.at[j, theirs], sib_ref.at[j], sems, j, (x, y, 1 - c)) for j in range(N_CHIPS)]
        k = N_CHIPS
        for fx in (0, 1):
            for fy in (0, 1):
                for fc in (0, 1):
                    if fx or fy or fc:
                        to = (1 - x if fx else x, 1 - y if fy else y, 1 - c if fc else c)
                        copies.append(_remote(small_ref, all_ref.at[me], sems, k, to))
                        k += 1
        for cp in copies:
            cp.start()
        for cp in copies:
            cp.wait_recv()
        for cp in copies:
            cp.wait_send()

    n_copies = N_CHIPS + N_DEV - 1
    return pl.pallas_call(
        body, name=name, in_specs=[ANY, ANY], out_specs=[ANY, ANY],
        out_shape=[jax.ShapeDtypeStruct((N_CHIPS, Rh, PACK_COLS), F32),
                   jax.ShapeDtypeStruct((N_DEV, SMALL_ROWS, PACK_COLS), F32)],
        scratch_shapes=[pltpu.SemaphoreType.DMA((n_copies,)), pltpu.SemaphoreType.DMA((n_copies,))],
    )(g, small)


def _exchange_chips(p, name):
    def body(p_ref, out_ref, send_sems, recv_sems):
        x, y, c = _place()
        me = 2 * x + y
        sems = (send_sems, recv_sems)
        copies = [_remote(p_ref.at[chip], out_ref.at[me], sems, k, (*xy, c))
                  for k, (chip, xy) in enumerate(_other_chips(x, y))]
        for cp in copies:
            cp.start()
        for k, (chip, xy) in enumerate(_other_chips(x, y)):
            _remote(p_ref.at[chip], out_ref.at[chip], sems, k, (*xy, c)).wait_recv()
        for cp in copies:
            cp.wait_send()

    return pl.pallas_call(
        body, name=name, in_specs=[ANY], out_specs=ANY,
        out_shape=jax.ShapeDtypeStruct(p.shape, p.dtype),
        scratch_shapes=[pltpu.SemaphoreType.DMA((3,)), pltpu.SemaphoreType.DMA((3,))],
    )(p)


SHARE_CHUNKS = 6


def _share_halves(gh, name):
    Rh = gh.shape[0]
    rows = Rh // SHARE_CHUNKS

    def body(gh_ref, out_ref, send_sems, recv_sems):
        x, y, c = _place()
        sems = (send_sems, recv_sems)
        copies = [_remote(gh_ref.at[pl.ds(k * rows, rows)], out_ref.at[pl.ds(k * rows, rows)], sems, k, (x, y, 1 - c))
                  for k in range(SHARE_CHUNKS)]
        for cp in copies:
            cp.start()
        for cp in copies:
            cp.wait_recv()
        for cp in copies:
            cp.wait_send()

    return pl.pallas_call(
        body, name=name, in_specs=[ANY], out_specs=ANY,
        out_shape=jax.ShapeDtypeStruct(gh.shape, gh.dtype),
        scratch_shapes=[pltpu.SemaphoreType.DMA((SHARE_CHUNKS,)), pltpu.SemaphoreType.DMA((SHARE_CHUNKS,))],
    )(gh)


def _row_block(rows, mult=8, cap=512):
    return max(b for b in range(mult, cap + 1, mult) if rows % b == 0)


def _add2(a, b, name):
    n, Rh, C = a.shape
    rows = _row_block(Rh, mult=16, cap=1024)
    spec = pl.BlockSpec((1, rows, C), lambda j, i: (j, i, 0))

    def body(a_ref, b_ref, o_ref):
        o_ref[...] = (a_ref[...] + b_ref[...]).astype(BF16)

    return pl.pallas_call(
        body, name=name, grid=(n, Rh // rows), in_specs=[spec, spec], out_specs=spec,
        out_shape=jax.ShapeDtypeStruct(a.shape, BF16),
        compiler_params=_cparams(("parallel", "parallel")),
    )(a, b)


def _sum_leading(a, name):
    n, R, C = a.shape
    rows = _row_block(R, mult=16 if a.dtype == BF16 else 8, cap=1024 if a.dtype == BF16 else 512)

    def body(a_ref, o_ref):
        acc = a_ref[0].astype(F32)
        for j in range(1, n):
            acc = acc + a_ref[j].astype(F32)
        o_ref[...] = acc

    return pl.pallas_call(
        body, name=name, grid=(R // rows,),
        in_specs=[pl.BlockSpec((n, rows, C), lambda i: (0, i, 0))],
        out_specs=pl.BlockSpec((rows, C), lambda i: (i, 0)),
        out_shape=jax.ShapeDtypeStruct((R, C), F32),
        compiler_params=_cparams(("parallel",)),
    )(a)


def _adamw(w, g, m, v, name):
    R, C = w.shape
    rows = _row_block(R)
    spec = pl.BlockSpec((rows, C), lambda i: (i, 0))

    def body(w_ref, g_ref, m_ref, v_ref, d_ref, mo_ref, vo_ref):
        gv = g_ref[...]
        mn = ADAM_B1 * m_ref[...] + (1.0 - ADAM_B1) * gv
        vn = ADAM_B2 * v_ref[...] + (1.0 - ADAM_B2) * (gv * gv)
        m_hat = mn / (1.0 - ADAM_B1 ** ADAM_STEP)
        v_hat = vn / (1.0 - ADAM_B2 ** ADAM_STEP)
        d_ref[...] = -ADAM_LR * (m_hat / (jnp.sqrt(v_hat) + ADAM_EPS) + ADAM_WD * w_ref[...])
        mo_ref[...] = mn
        vo_ref[...] = vn

    return pl.pallas_call(
        body, name=name, grid=(R // rows,), in_specs=[spec] * 4, out_specs=[spec] * 3,
        out_shape=[jax.ShapeDtypeStruct((R, C), F32)] * 3,
        compiler_params=_cparams(("parallel",)),
    )(w, g, m, v)


BIG = ("w_in", "w_branch_a", "w_branch_b", "w_out", "w_ffn_up", "w_ffn_down", "ffn_conv_w")
BIG_SHARD = {"w_in": (D_MODEL, 1280), "w_branch_a": (WIDTH, 256), "w_branch_b": (WIDTH, 256),
             "w_out": (256, D_MODEL), "w_ffn_up": (D_MODEL, 1408), "w_ffn_down": (704, D_MODEL),
             "ffn_conv_w": (3, 1408)}
BIG_COL_SHARDED = {"w_in": True, "w_branch_a": True, "w_branch_b": True, "w_out": False,
                   "w_ffn_up": True, "w_ffn_down": False, "ffn_conv_w": True}
CONV_W_ROWS = 32


def _pack_rows(name):
    r, c = BIG_SHARD[name]
    return CONV_W_ROWS if name == "ffn_conv_w" else r * c // PACK_COLS


def _rows_of(flat, rows):
    return jnp.pad(flat, (0, rows * PACK_COLS - flat.shape[0])).reshape(rows, PACK_COLS)


def _pack_shards(shards, dtype=F32):
    parts = []
    for n in BIG:
        if n == "ffn_conv_w":
            flat = shards[n].reshape(-1)
            if dtype == BF16:
                flat = lax.bitcast_convert_type(flat, BF16).reshape(-1)
            parts.append(_rows_of(flat, CONV_W_ROWS))
        else:
            parts.append(shards[n].reshape(_pack_rows(n), PACK_COLS).astype(dtype))
    return jnp.concatenate(parts, axis=0)


def _unpack_shards(packed):
    out, r0 = {}, 0
    for n in BIG:
        r, c = BIG_SHARD[n]
        out[n] = packed[r0:r0 + _pack_rows(n)].reshape(-1)[:r * c].reshape(r, c)
        r0 += _pack_rows(n)
    return out


def _unpack_full(gathered):
    out, r0 = {}, 0
    for n in BIG:
        r, c = BIG_SHARD[n]
        part = gathered[:, r0:r0 + _pack_rows(n)].reshape(N_CHIPS, -1)
        if n == "ffn_conv_w":
            part = lax.bitcast_convert_type(part[:, :2 * r * c].reshape(N_CHIPS, r * c, 2), F32)
        parts = part[:, :r * c].reshape(N_CHIPS, r, c)
        out[n] = jnp.concatenate(list(parts), axis=1 if BIG_COL_SHARDED[n] else 0)
        r0 += _pack_rows(n)
    return out


def _pack_full(full):
    slabs = []
    for j in range(N_CHIPS):
        shards = {}
        for n in BIG:
            r, c = BIG_SHARD[n]
            shards[n] = full[n][:, j * c:(j + 1) * c] if BIG_COL_SHARDED[n] else full[n][j * r:(j + 1) * r]
        slabs.append(_pack_shards(shards))
    return jnp.stack(slabs)


SMALL = (("norm1_g", D_MODEL), ("q_norm_g", HEAD_DIM), ("k_norm_g", HEAD_DIM), ("rel_bias", N_HEADS * N_REL),
         ("norm2_g", D_MODEL), ("ffn_conv_b", 2 * D_FF))


def _pack_small(vals):
    return _rows_of(jnp.concatenate([vals[n].reshape(-1) for n, _ in SMALL]), SMALL_ROWS)


def _unpack_small(packed):
    flat, out, o = packed.reshape(-1), {}, 0
    for n, sz in SMALL:
        out[n] = flat[o:o + sz]
        o += sz
    return out


def kernel(x, norm1_g, w_in, q_norm_g, k_norm_g, rel_bias, w_branch_a, w_branch_b, w_out, norm2_g, w_ffn_up, ffn_conv_w, ffn_conv_b, w_ffn_down, loss_target, m_norm1_g, m_w_in, m_q_norm_g, m_k_norm_g, m_rel_bias, m_w_branch_a, m_w_branch_b, m_w_out, m_norm2_g, m_w_ffn_up, m_ffn_conv_w, m_ffn_conv_b, m_w_ffn_down, v_norm1_g, v_w_in, v_q_norm_g, v_k_norm_g, v_rel_bias, v_w_branch_a, v_w_branch_b, v_w_out, v_norm2_g, v_w_ffn_up, v_ffn_conv_w, v_ffn_conv_b, v_w_ffn_down):
    w_big = {"w_in": w_in[0], "w_branch_a": w_branch_a[0], "w_branch_b": w_branch_b[0], "w_out": w_out[0],
             "w_ffn_up": w_ffn_up[0], "w_ffn_down": w_ffn_down[0], "ffn_conv_w": ffn_conv_w[0]}
    m_big = {"w_in": m_w_in[0], "w_branch_a": m_w_branch_a[0], "w_branch_b": m_w_branch_b[0], "w_out": m_w_out[0],
             "w_ffn_up": m_w_ffn_up[0], "w_ffn_down": m_w_ffn_down[0], "ffn_conv_w": m_ffn_conv_w[0]}
    v_big = {"w_in": v_w_in[0], "w_branch_a": v_w_branch_a[0], "w_branch_b": v_w_branch_b[0], "w_out": v_w_out[0],
             "w_ffn_up": v_w_ffn_up[0], "w_ffn_down": v_w_ffn_down[0], "ffn_conv_w": v_ffn_conv_w[0]}
    xs, tgt = x[0], loss_target[0]

    xi, yi, ci = _place()
    chip = 2 * xi + yi
    w_pack_bf = _pack_shards(w_big, BF16)
    gathered = lax.dynamic_update_slice(_gather_weights(w_pack_bf, "gather_weights"), w_pack_bf[None], (chip, 0, 0))
    full = _unpack_full(gathered)
    w_in_f, w_a, w_b, w_o = full["w_in"], full["w_branch_a"], full["w_branch_b"], full["w_out"]
    w_up, w_dn, conv_w = full["w_ffn_up"], full["w_ffn_down"], full["ffn_conv_w"]
    w_in_t, w_a_t, w_b_t, w_o_t, w_up_t, w_dn_t = (w.T for w in (w_in_f, w_a, w_b, w_o, w_up, w_dn))

    hn, hn_t = _rms_fwd(xs, norm1_g, "rms1")
    qk = _matmul(hn, w_in_f[:, :2 * WIDTH], F32, "proj_qk")
    vqkv = _matmul(hn, w_in_f[:, 2 * WIDTH:6 * WIDTH], BF16, "proj_vqkv")
    g_a = _matmul(hn, w_in_f[:, 6 * WIDTH:6 * WIDTH + D_MODEL], F32, "proj_gate_a")
    g_b = _matmul(hn, w_in_f[:, 6 * WIDTH + D_MODEL:], F32, "proj_gate_b")
    gq = jnp.tile(q_norm_g, (1, N_HEADS))
    gk = jnp.tile(k_norm_g, (1, N_HEADS))
    qa, ka = _qknorm_fwd(qk, gq, gk, "qknorm")
    per = WIDTH // PAIR
    b_offs = (per, 2 * per, 3 * per)
    tab = _bias_table(jnp.pad(rel_bias[0], ((0, 0), (0, REL_PAD - N_REL))), "bias_table")
    out_a, out_a_t = _ca_fwd(qa, ka, vqkv, tab, "chunk_attn")
    out_b, out_b_f32, out_b_t = _sb_fwd(vqkv, vqkv, vqkv, "stick_attn", b_offs)
    y_a = _matmul(out_a, w_a, F32, "branch_a")
    y_b = _matmul(out_b, w_b, F32, "branch_b")
    mixed, mixed_t = _mix_fwd(g_a, g_b, y_a, y_b, "mix")
    x2 = _matmul(mixed, w_o, F32, "out_proj", residual=xs)
    hn2, hn2_t = _rms_fwd(x2, norm2_g, "rms2")
    hid = _matmul(hn2, w_up, BF16, "ffn_up")
    act, act_t = _convglu_fwd(hid, conv_w, ffn_conv_b, "convglu")
    y = _matmul(act, w_dn, F32, "ffn_down", residual=x2)
    dy, dyb, sq = _loss_head(y, tgt, "loss_head")
    loss = lax.psum(0.5 / D_MODEL * jnp.sum(sq), ("x", "y", "c"))

    dact = _matmul(dyb, w_dn_t, BF16, "d_act")
    d_w_dn = _matmul(act_t, dyb, F32, "d_w_down")
    dhg, dhu, dcwg, dcwu, dcbg, dcbu = _convglu_bwd(hid, dact, conv_w, ffn_conv_b, "convglu_bwd")
    d_w_up = jnp.concatenate([_matmul(hn2_t, dhg, F32, "d_w_up_gate"),
                              _matmul(hn2_t, dhu, F32, "d_w_up_up")], axis=1)
    dhn2 = _matmul(dhg, w_up_t[:D_FF], F32, "d_hn2_gate")
    dhn2 = _matmul(dhu, w_up_t[D_FF:], F32, "d_hn2_up", residual=dhn2)
    dx2, dx2b, d_norm2 = _rms_bwd(x2, norm2_g, dhn2, dy, "rms2_bwd")
    dmixed = _matmul(dx2b, w_o_t, F32, "d_mixed")
    d_w_o = _matmul(mixed_t, dx2b, F32, "d_w_out")
    dga, dgb, dya, dyb_b = _mix_bwd(dmixed, g_a, g_b, y_a, y_b, "mix_bwd")
    d_w_a = _matmul(out_a_t, dya, F32, "d_w_branch_a")
    d_w_b = _matmul(out_b_t, dyb_b, F32, "d_w_branch_b")
    do_a = _matmul(dya, w_a_t, BF16, "d_out_a")
    do_b = _matmul(dyb_b, w_b_t, BF16, "d_out_b")
    dqb, dkb, dvb = _sb_bwd(vqkv, vqkv, vqkv, out_b_f32, do_b, "stick_attn_bwd", b_offs)
    dqa_n, dka_n, dva, dtab = _ca_bwd(qa, ka, vqkv, do_a, tab, "chunk_attn_bwd")
    d_rel = _bias_table_bwd(dtab, "bias_table_bwd")[:, :N_REL]
    dqa, dka, dgq, dgk = _qknorm_bwd(qk, gq, gk, dqa_n, dka_n, "qknorm_bwd")
    dproj = jnp.concatenate([dqa, dka, dva, dqb, dkb, dvb, dga, dgb], axis=1)
    d_w_in = _matmul(hn_t, dproj, F32, "d_w_in")
    dhn = _matmul(dproj, w_in_t, F32, "d_hn")
    dx, _, d_norm1 = _rms_bwd(xs, norm1_g, dhn, dx2, "rms1_bwd")

    g_pack = _pack_full({"w_in": d_w_in, "w_branch_a": d_w_a, "w_branch_b": d_w_b, "w_out": d_w_o,
                         "w_ffn_up": d_w_up, "w_ffn_down": d_w_dn,
                         "ffn_conv_w": jnp.concatenate([dcwg, dcwu], axis=1)})
    small_g = _pack_small({"norm1_g": d_norm1, "q_norm_g": dgq.reshape(N_HEADS, HEAD_DIM).sum(0),
                           "k_norm_g": dgk.reshape(N_HEADS, HEAD_DIM).sum(0), "rel_bias": d_rel,
                           "norm2_g": d_norm2, "ffn_conv_b": jnp.concatenate([dcbg, dcbu], axis=1)})
    half_rows = g_pack.shape[1] // 2
    sib, small_all = _exchange_cores(g_pack, small_g, "exchange_cores")
    mine = lax.dynamic_slice_in_dim(g_pack, ci * half_rows, half_rows, axis=1)
    chip_part = _add2(mine, sib, "sum_cores")
    parts = _exchange_chips(chip_part, "exchange_chips")
    parts = lax.dynamic_update_slice(parts, lax.dynamic_index_in_dim(chip_part, chip, 0), (chip, 0, 0))
    g_half = _sum_leading(parts, "sum_chips")
    g_other = _share_halves(g_half, "share_halves")
    g_shard = jnp.concatenate([jnp.where(ci == 0, g_half, g_other), jnp.where(ci == 0, g_other, g_half)], axis=0)
    small_all = lax.dynamic_update_slice(small_all, small_g[None], (4 * xi + 2 * yi + ci, 0, 0))
    small_sum = _sum_leading(small_all, "sum_small")

    d_pack, m_pack, v_pack = _adamw(_pack_shards(w_big), g_shard, _pack_shards(m_big), _pack_shards(v_big),
                                    "adamw_big")
    grads = _unpack_shards(g_shard)
    deltas, new_m, new_v = _unpack_shards(d_pack), _unpack_shards(m_pack), _unpack_shards(v_pack)

    shapes = {"norm1_g": norm1_g.shape, "q_norm_g": q_norm_g.shape, "k_norm_g": k_norm_g.shape,
              "rel_bias": rel_bias.shape, "norm2_g": norm2_g.shape, "ffn_conv_b": ffn_conv_b.shape}
    small_w = {"norm1_g": norm1_g, "q_norm_g": q_norm_g, "k_norm_g": k_norm_g, "rel_bias": rel_bias,
               "norm2_g": norm2_g, "ffn_conv_b": ffn_conv_b}
    small_m = {"norm1_g": m_norm1_g, "q_norm_g": m_q_norm_g, "k_norm_g": m_k_norm_g, "rel_bias": m_rel_bias,
               "norm2_g": m_norm2_g, "ffn_conv_b": m_ffn_conv_b}
    small_v = {"norm1_g": v_norm1_g, "q_norm_g": v_q_norm_g, "k_norm_g": v_k_norm_g, "rel_bias": v_rel_bias,
               "norm2_g": v_norm2_g, "ffn_conv_b": v_ffn_conv_b}
    ds, ms, vs = _adamw(_pack_small(small_w), small_sum, _pack_small(small_m), _pack_small(small_v), "adamw_small")
    small_grads, ds, ms, vs = (_unpack_small(t) for t in (small_sum, ds, ms, vs))

    order = ("norm1_g", "w_in", "q_norm_g", "k_norm_g", "rel_bias", "w_branch_a", "w_branch_b", "w_out",
             "norm2_g", "w_ffn_up", "ffn_conv_w", "ffn_conv_b", "w_ffn_down")
    outs = [loss, dx[None]]
    for big, small in ((grads, small_grads), (deltas, ds), (new_m, ms), (new_v, vs)):
        for n in order:
            outs.append(big[n][None] if n in big else small[n].reshape(shapes[n]))
    return tuple(outs)
```

```python
import functools

import jax
import jax.numpy as jnp
from jax import lax
from jax.experimental import pallas as pl
from jax.experimental.pallas import tpu as pltpu

F32 = jnp.float32
BF16 = jnp.bfloat16
MESH = pl.DeviceIdType.MESH

D_MODEL = 1024
HEAD_DIM = 64
N_HEADS = 8
WIDTH = N_HEADS * HEAD_DIM
CHUNK = 64
LEFT_CHUNKS = 8
MAX_REL = 128
N_REL = 2 * MAX_REL + 1
D_FF = 2816
EPS = 1e-6
NEG = -1e30

ADAM_LR = 0.001
ADAM_B1 = 0.9
ADAM_B2 = 0.999
ADAM_EPS = 1e-08
ADAM_WD = 0.01
ADAM_STEP = 10

N_CHIPS = 4
N_DEV = 8
LANES = 128
PAIR = 2 * HEAD_DIM
BQ = 256
BAND = LEFT_CHUNKS * CHUNK
KWIN = BAND + BQ
VMEM_LIMIT = 56 * 1024 * 1024
PACK_COLS = 1024

NN = (((1,), (0,)), ((), ()))
NT = (((1,), (1,)), ((), ()))
TN = (((0,), (0,)), ((), ()))


def _cparams(sem=None):
    if sem is None:
        return pltpu.CompilerParams(vmem_limit_bytes=VMEM_LIMIT)
    return pltpu.CompilerParams(dimension_semantics=sem, vmem_limit_bytes=VMEM_LIMIT)


def _pick(n, cands):
    for c in cands:
        if n % c == 0:
            return c
    raise ValueError(f"no block for {n}")


def _dot(a, b, dn):
    return lax.dot_general(a, b, dn, preferred_element_type=F32)


def _sigmoid(x):
    return 0.5 * jnp.tanh(0.5 * x) + 0.5


def _split_bf16(x):
    hi = x.astype(BF16)
    lo = (x - hi.astype(F32)).astype(BF16)
    return hi, lo


def _matmul(a, b, out_dtype, name, residual=None, slabs=None):
    (M, K), N = a.shape, b.shape[1]
    bk = K if K <= D_FF else _pick(K, (1024, 512))
    bm = 512 if D_MODEL < K <= D_FF else _pick(M, (1024, D_FF // 4, 512, 256, 128))
    bn = N // slabs if slabs else _pick(N, (D_FF // 2, 512, 256, 128))
    nk = K // bk
    dn = NN
    a_spec = pl.BlockSpec((bm, bk), lambda i, j, k: (i, k))
    b_spec = pl.BlockSpec((bk, bn), lambda i, j, k: (k, j))
    if slabs:
        o_spec = pl.BlockSpec((None, bm, bn), lambda i, j, k: (j, i, 0))
        out_shape = jax.ShapeDtypeStruct((slabs, M, bn), out_dtype)
    else:
        o_spec = pl.BlockSpec((bm, bn), lambda i, j, k: (i, j))
        out_shape = jax.ShapeDtypeStruct((M, N), out_dtype)
    has_res = residual is not None

    def body(*refs):
        if has_res:
            a_ref, b_ref, r_ref, o_ref, acc_ref = refs
        else:
            a_ref, b_ref, o_ref, acc_ref = refs
        k = pl.program_id(2)
        part = _dot(a_ref[...], b_ref[...], dn)

        def finish(total):
            if has_res:
                total = total + r_ref[...]
            o_ref[...] = total.astype(out_dtype)

        if nk == 1:
            finish(part)
        else:
            @pl.when(k == 0)
            def _():
                acc_ref[...] = part

            @pl.when(k > 0)
            def _():
                acc_ref[...] += part

            @pl.when(k == nk - 1)
            def _():
                finish(acc_ref[...])

    in_specs = [a_spec, b_spec] + ([o_spec] if has_res else [])
    args = (a, b) + ((residual,) if has_res else ())
    return pl.pallas_call(
        body, name=name,
        grid=(M // bm, N // bn, nk),
        in_specs=in_specs, out_specs=o_spec, out_shape=out_shape,
        scratch_shapes=[pltpu.VMEM((bm, bn) if nk > 1 else (8, LANES), F32)],
        compiler_params=_cparams(("parallel", "parallel", "arbitrary")),
    )(*args)


ROWS = 512


def _row_spec(cols, bm=ROWS):
    return pl.BlockSpec((bm, cols), lambda i: (i, 0))


def _col_spec(rows, bn=ROWS):
    return pl.BlockSpec((rows, bn), lambda i: (0, i))


def _full_spec(shape):
    return pl.BlockSpec(shape, lambda i: (0,) * len(shape))


def _colsum8(t):
    return jnp.sum(t.reshape(t.shape[0] // 8, 8, t.shape[1]), axis=0)


def _rms_fwd(x, g, name):
    S, D = x.shape

    def body(x_ref, g_ref, o_ref, ot_ref):
        xv = x_ref[...]
        r = lax.rsqrt(jnp.mean(xv * xv, axis=-1, keepdims=True) + EPS)
        y = xv * r * g_ref[...]
        o_ref[...] = y.astype(BF16)
        ot_ref[...] = y.T.astype(BF16)

    return pl.pallas_call(
        body, name=name, grid=(S // ROWS,),
        in_specs=[_row_spec(D), _full_spec((1, D))], out_specs=[_row_spec(D), _col_spec(D)],
        out_shape=[jax.ShapeDtypeStruct((S, D), BF16), jax.ShapeDtypeStruct((D, S), BF16)],
        compiler_params=_cparams(("parallel",)),
    )(x, g)


def _rms_bwd(x, g, dy, dres, name):
    S, D = x.shape
    nt = S // ROWS

    def body(x_ref, g_ref, dy_ref, dres_ref, dx_ref, dxb_ref, dg_ref, acc_ref):
        i = pl.program_id(0)
        xv, dyv = x_ref[...], dy_ref[...]
        r = lax.rsqrt(jnp.mean(xv * xv, axis=-1, keepdims=True) + EPS)
        xr = xv * r
        u = dyv * g_ref[...]
        dx = r * u - xr * (r * r) * jnp.mean(xv * u, axis=-1, keepdims=True) + dres_ref[...]
        dx_ref[...] = dx
        dxb_ref[...] = dx.astype(BF16)
        part = _colsum8(dyv * xr)

        @pl.when(i == 0)
        def _():
            acc_ref[...] = part

        @pl.when(i > 0)
        def _():
            acc_ref[...] += part

        @pl.when(i == nt - 1)
        def _():
            dg_ref[...] = jnp.sum(acc_ref[...], axis=0, keepdims=True)

    return pl.pallas_call(
        body, name=name, grid=(nt,),
        in_specs=[_row_spec(D), _full_spec((1, D)), _row_spec(D), _row_spec(D)],
        out_specs=[_row_spec(D), _row_spec(D), _full_spec((1, D))],
        out_shape=[jax.ShapeDtypeStruct((S, D), F32), jax.ShapeDtypeStruct((S, D), BF16),
                   jax.ShapeDtypeStruct((1, D), F32)],
        scratch_shapes=[pltpu.VMEM((8, D), F32)],
        compiler_params=_cparams(("arbitrary",)),
    )(x, g, dy, dres)


def _head_mean(t, blockdiag):
    hi, lo = _split_bf16(t)
    return (_dot(hi, blockdiag, NN) + _dot(lo, blockdiag, NN)) * (1.0 / HEAD_DIM)


def _blockdiag():
    r = lax.broadcasted_iota(jnp.int32, (WIDTH, WIDTH), 0) // HEAD_DIM
    c = lax.broadcasted_iota(jnp.int32, (WIDTH, WIDTH), 1) // HEAD_DIM
    return jnp.where(r == c, 1.0, 0.0).astype(BF16)


def _qknorm_fwd(qk, gq, gk, name):
    S = qk.shape[0]

    def body(qk_ref, gq_ref, gk_ref, q_ref, k_ref):
        bd = _blockdiag()
        for part, g_ref, o_ref, scale in ((0, gq_ref, q_ref, HEAD_DIM ** -0.5), (1, gk_ref, k_ref, 1.0)):
            t = qk_ref[:, part * WIDTH:(part + 1) * WIDTH]
            r = lax.rsqrt(_head_mean(t * t, bd) + EPS)
            o_ref[...] = (t * r * g_ref[...] * scale).astype(BF16)

    return pl.pallas_call(
        body, name=name, grid=(S // ROWS,),
        in_specs=[_row_spec(2 * WIDTH), _full_spec((1, WIDTH)), _full_spec((1, WIDTH))],
        out_specs=[_row_spec(WIDTH), _row_spec(WIDTH)],
        out_shape=[jax.ShapeDtypeStruct((S, WIDTH), BF16)] * 2,
        compiler_params=_cparams(("parallel",)),
    )(qk, gq, gk)


def _qknorm_bwd(qk, gq, gk, dqn, dkn, name):
    S = qk.shape[0]
    nt = S // ROWS

    def body(qk_ref, gq_ref, gk_ref, dqn_ref, dkn_ref, dq_ref, dk_ref, dgq_ref, dgk_ref, accq_ref, acck_ref):
        i = pl.program_id(0)
        bd = _blockdiag()
        for part, g_ref, dn_ref, o_ref, dg_ref, acc_ref, scale in (
                (0, gq_ref, dqn_ref, dq_ref, dgq_ref, accq_ref, HEAD_DIM ** -0.5),
                (1, gk_ref, dkn_ref, dk_ref, dgk_ref, acck_ref, 1.0)):
            t = qk_ref[:, part * WIDTH:(part + 1) * WIDTH]
            dn = dn_ref[...] * scale
            r = lax.rsqrt(_head_mean(t * t, bd) + EPS)
            u = dn * g_ref[...]
            dt = r * u - t * (r * r * r) * _head_mean(t * u, bd)
            o_ref[...] = dt.astype(BF16)
            psum = _colsum8(dn * t * r)

            @pl.when(i == 0)
            def _():
                acc_ref[...] = psum

            @pl.when(i > 0)
            def _():
                acc_ref[...] += psum

            @pl.when(i == nt - 1)
            def _():
                dg_ref[...] = jnp.sum(acc_ref[...], axis=0, keepdims=True)

    return pl.pallas_call(
        body, name=name, grid=(nt,),
        in_specs=[_row_spec(2 * WIDTH), _full_spec((1, WIDTH)), _full_spec((1, WIDTH)),
                  _row_spec(WIDTH), _row_spec(WIDTH)],
        out_specs=[_row_spec(WIDTH), _row_spec(WIDTH), _full_spec((1, WIDTH)), _full_spec((1, WIDTH))],
        out_shape=[jax.ShapeDtypeStruct((S, WIDTH), BF16)] * 2 + [jax.ShapeDtypeStruct((1, WIDTH), F32)] * 2,
        scratch_shapes=[pltpu.VMEM((8, WIDTH), F32)] * 2,
        compiler_params=_cparams(("arbitrary",)),
    )(qk, gq, gk, dqn, dkn)


def _mix_fwd(ga, gb, ya, yb, name):
    S, D = ga.shape

    def body(ga_ref, gb_ref, ya_ref, yb_ref, o_ref, ot_ref):
        m = _sigmoid(ga_ref[...]) * ya_ref[...] + _sigmoid(gb_ref[...]) * yb_ref[...]
        o_ref[...] = m.astype(BF16)
        ot_ref[...] = m.T.astype(BF16)

    return pl.pallas_call(
        body, name=name, grid=(S // ROWS,),
        in_specs=[_row_spec(D)] * 4, out_specs=[_row_spec(D), _col_spec(D)],
        out_shape=[jax.ShapeDtypeStruct((S, D), BF16), jax.ShapeDtypeStruct((D, S), BF16)],
        compiler_params=_cparams(("parallel",)),
    )(ga, gb, ya, yb)


def _mix_bwd(dm, ga, gb, ya, yb, name):
    S, D = ga.shape

    def body(dm_ref, ga_ref, gb_ref, ya_ref, yb_ref, dga_ref, dgb_ref, dya_ref, dyb_ref):
        dmv = dm_ref[...]
        for g_ref, y_ref, dg_ref, dy_ref in ((ga_ref, ya_ref, dga_ref, dya_ref), (gb_ref, yb_ref, dgb_ref, dyb_ref)):
            s = _sigmoid(g_ref[...])
            dy_ref[...] = (dmv * s).astype(BF16)
            dg_ref[...] = (dmv * y_ref[...] * s * (1.0 - s)).astype(BF16)

    return pl.pallas_call(
        body, name=name, grid=(S // ROWS,),
        in_specs=[_row_spec(D)] * 5, out_specs=[_row_spec(D)] * 4,
        out_shape=[jax.ShapeDtypeStruct((S, D), BF16)] * 4,
        compiler_params=_cparams(("parallel",)),
    )(dm, ga, gb, ya, yb)


def _loss_head(y, target, name):
    S, D = y.shape
    nt = S // ROWS

    def body(y_ref, t_ref, dy_ref, dyb_ref, p_ref):
        err = y_ref[...] - t_ref[...]
        dy = err * (1.0 / D)
        dy_ref[...] = dy
        dyb_ref[...] = dy.astype(BF16)
        sq = _colsum8(err * err)
        acc = sq[:, 0:LANES]
        for k in range(1, D // LANES):
            acc = acc + sq[:, k * LANES:(k + 1) * LANES]
        p_ref[...] = acc

    return pl.pallas_call(
        body, name=name, grid=(nt,),
        in_specs=[_row_spec(D)] * 2,
        out_specs=[_row_spec(D), _row_spec(D), pl.BlockSpec((8, LANES), lambda i: (i, 0))],
        out_shape=[jax.ShapeDtypeStruct((S, D), F32), jax.ShapeDtypeStruct((S, D), BF16),
                   jax.ShapeDtypeStruct((nt * 8, LANES), F32)],
        compiler_params=_cparams(("parallel",)),
    )(y, target)


CONV_COLS = D_FF // 2
HALO = 16
CONV_CHUNK = 64


def _aligned(start, multiple):
    return start if isinstance(start, int) else pl.multiple_of(start, multiple)


def _conv_taps(xe, cw, cb):
    taps = (pltpu.roll(xe, 2, 0), pltpu.roll(xe, 1, 0), xe)
    return taps, cw[0:1] * taps[0] + cw[1:2] * taps[1] + cw[2:3] * taps[2] + cb


def _conv_specs(nt):
    hb, nb = ROWS // HALO, D_FF // CONV_COLS
    specs = {}
    for part, off in (("gate", 0), ("up", nb)):
        specs[part] = dict(
            main=pl.BlockSpec((ROWS, CONV_COLS), functools.partial(lambda c, i, off: (i, c + off), off=off)),
            prev=pl.BlockSpec((HALO, CONV_COLS),
                              functools.partial(lambda c, i, off: (jnp.maximum(i * hb - 1, 0), c + off), off=off)),
            nxt=pl.BlockSpec((HALO, CONV_COLS),
                             functools.partial(lambda c, i, off: (jnp.minimum((i + 1) * hb, nt * hb - 1), c + off), off=off)),
            w=pl.BlockSpec((3, CONV_COLS), functools.partial(lambda c, i, off: (0, c + off), off=off)),
            b=pl.BlockSpec((1, CONV_COLS), functools.partial(lambda c, i, off: (0, c + off), off=off)))
    return specs


def _convglu_fwd(hid, cw, cb, name):
    S = hid.shape[0]
    sp = _conv_specs(S // ROWS)

    def body(hg_ref, hgp_ref, hu_ref, hup_ref, cwg_ref, cwu_ref, cbg_ref, cbu_ref, o_ref, ot_ref):
        i = pl.program_id(1)
        keep = (i > 0).astype(F32)

        def conv(h_ref, hp_ref, cw_ref, cb_ref):
            xe = jnp.concatenate([hp_ref[...].astype(F32) * keep, h_ref[...].astype(F32)], axis=0)
            return _conv_taps(xe, cw_ref[...], cb_ref[...])[1][HALO:, :]

        gate = conv(hg_ref, hgp_ref, cwg_ref, cbg_ref)
        up = conv(hu_ref, hup_ref, cwu_ref, cbu_ref)
        act = gate * _sigmoid(gate) * up
        o_ref[...] = act.astype(BF16)
        ot_ref[...] = act.T.astype(BF16)

    g, u = sp["gate"], sp["up"]
    return pl.pallas_call(
        body, name=name, grid=(D_FF // CONV_COLS, S // ROWS),
        in_specs=[g["main"], g["prev"], u["main"], u["prev"], g["w"], u["w"], g["b"], u["b"]],
        out_specs=[g["main"], pl.BlockSpec((CONV_COLS, ROWS), lambda c, i: (c, i))],
        out_shape=[jax.ShapeDtypeStruct((S, D_FF), BF16), jax.ShapeDtypeStruct((D_FF, S), BF16)],
        compiler_params=_cparams(("parallel", "parallel")),
    )(hid, hid, hid, hid, cw, cw, cb, cb)


def _convglu_bwd(hid, dact, cw, cb, name):
    S = hid.shape[0]
    nt = S // ROWS
    sp = _conv_specs(nt)

    n_chunks = ROWS // CONV_CHUNK

    def body(hg_ref, hgp_ref, hgn_ref, hu_ref, hup_ref, hun_ref, da_ref, dan_ref,
             cwg_ref, cwu_ref, cbg_ref, cbu_ref,
             dhg_ref, dhu_ref, dcwg_ref, dcwu_ref, dcbg_ref, dcbu_ref, xg_s, xu_s, da_s):
        i = pl.program_id(1)
        kp = (i > 0).astype(F32)
        kn = (i < nt - 1).astype(F32)
        for x_s, h_ref, hp_ref, hn_ref in ((xg_s, hg_ref, hgp_ref, hgn_ref), (xu_s, hu_ref, hup_ref, hun_ref)):
            x_s[0:HALO, :] = hp_ref[...].astype(F32) * kp
            x_s[HALO:HALO + ROWS, :] = h_ref[...].astype(F32)
            x_s[HALO + ROWS:, :] = hn_ref[...].astype(F32) * kn
        da_s[0:ROWS, :] = da_ref[...].astype(F32)
        da_s[ROWS:, :] = dan_ref[...].astype(F32) * kn

        @pl.when(i == 0)
        def _():
            for ref in (dcwg_ref, dcwu_ref, dcbg_ref, dcbu_ref):
                ref[...] = jnp.zeros_like(ref)

        def lane_group(grp, _):
            lanes = pl.ds(pl.multiple_of(grp * LANES, LANES), LANES)
            cwg, cwu, cbg, cbu = cwg_ref[:, lanes], cwu_ref[:, lanes], cbg_ref[:, lanes], cbu_ref[:, lanes]

            def grads(r0, n):
                rows = pl.ds(_aligned(r0 + HALO - 8, 8), n + 8)
                taps_g, gate = _conv_taps(xg_s[rows, lanes], cwg, cbg)
                taps_u, up = _conv_taps(xu_s[rows, lanes], cwu, cbu)
                gate, up = gate[8:], up[8:]
                da = da_s[pl.ds(_aligned(r0, 8), n), lanes]
                sg = _sigmoid(gate)
                return (da * up * sg * (1.0 + gate * (1.0 - sg)), da * gate * sg,
                        [t[8:] for t in taps_g], [t[8:] for t in taps_u])

            def chunk(step, carry):
                below_g, below_u, accs = carry
                r0 = (n_chunks - 1 - step) * CONV_CHUNK
                dg, du, taps_g, taps_u = grads(r0, CONV_CHUNK)
                new_accs = []
                for d, below, cwv, taps, dh_ref, acc in ((dg, below_g, cwg, taps_g, dhg_ref, accs[0]),
                                                        (du, below_u, cwu, taps_u, dhu_ref, accs[1])):
                    ext = jnp.concatenate([d, below], axis=0)
                    n_ext = CONV_CHUNK + 8
                    dh = (cwv[2:3] * d + cwv[1:2] * pltpu.roll(ext, n_ext - 1, 0)[:CONV_CHUNK]
                          + cwv[0:1] * pltpu.roll(ext, n_ext - 2, 0)[:CONV_CHUNK])
                    dh_ref[pl.ds(_aligned(r0, CONV_CHUNK), CONV_CHUNK), lanes] = dh.astype(BF16)
                    new_accs.append(tuple(a + _colsum8(d * tap) for a, tap in zip(acc[:3], taps))
                                    + (acc[3] + _colsum8(d),))
                return dg[0:8], du[0:8], tuple(new_accs)

            below_g, below_u, _, _ = grads(ROWS, 8)
            zero = jnp.zeros((8, LANES), F32)
            _, _, accs = lax.fori_loop(0, n_chunks, chunk, (below_g, below_u, ((zero,) * 4, (zero,) * 4)))
            for acc, dcw_ref, dcb_ref in ((accs[0], dcwg_ref, dcbg_ref), (accs[1], dcwu_ref, dcbu_ref)):
                for t in range(3):
                    dcw_ref[t:t + 1, lanes] += jnp.sum(acc[t], axis=0, keepdims=True)
                dcb_ref[:, lanes] += jnp.sum(acc[3], axis=0, keepdims=True)
            return 0

        lax.fori_loop(0, CONV_COLS // LANES, lane_group, 0)

    g, u = sp["gate"], sp["up"]
    return pl.pallas_call(
        body, name=name, grid=(D_FF // CONV_COLS, nt),
        in_specs=[g["main"], g["prev"], g["nxt"], u["main"], u["prev"], u["nxt"], g["main"], g["nxt"],
                  g["w"], u["w"], g["b"], u["b"]],
        out_specs=[g["main"], g["main"], g["w"], g["w"], g["b"], g["b"]],
        out_shape=[jax.ShapeDtypeStruct((S, D_FF), BF16)] * 2 + [jax.ShapeDtypeStruct((3, D_FF), F32)] * 2
        + [jax.ShapeDtypeStruct((1, D_FF), F32)] * 2,
        scratch_shapes=[pltpu.VMEM((ROWS + 2 * HALO, CONV_COLS), F32)] * 2 + [pltpu.VMEM((ROWS + HALO, CONV_COLS), F32)],
        compiler_params=_cparams(("parallel", "arbitrary")),
    )(hid, hid, hid, hid, hid, hid, dact, dact, cw, cw, cb, cb)


REL_PAD = 384
DIAG = 1024


def _band_valid():
    qc = lax.broadcasted_iota(jnp.int32, (BQ, KWIN), 0) // CHUNK
    kc = lax.broadcasted_iota(jnp.int32, (BQ, KWIN), 1) // CHUNK - LEFT_CHUNKS
    return (kc <= qc) & (kc >= qc - LEFT_CHUNKS)


def _rel_index(offset):
    return jnp.clip(BAND - offset, -MAX_REL, MAX_REL) + MAX_REL


def _split3(x):
    hi = x.astype(BF16)
    rest = x - hi.astype(F32)
    mid = rest.astype(BF16)
    return hi, mid, (rest - mid.astype(F32)).astype(BF16)


def _bias_table(rel_bias, name):
    def body(rb_ref, o_ref):
        t = lax.broadcasted_iota(jnp.int32, (REL_PAD, DIAG), 0)
        lane = lax.broadcasted_iota(jnp.int32, (REL_PAD, DIAG), 1)
        pick = jnp.where(t == _rel_index(lane - BQ), 1.0, 0.0).astype(BF16)
        base = sum(_dot(piece, pick, NN) for piece in _split3(rb_ref[...]))
        valid = _band_valid()
        for h in range(N_HEADS):
            rows = jnp.broadcast_to(base[h:h + 1], (BQ, DIAG))
            rolled = pltpu.roll(rows, 0, 1, stride=1, stride_axis=0)
            o_ref[h] = jnp.where(valid, rolled[:, BQ:], NEG)

    return pl.pallas_call(
        body, name=name,
        out_shape=jax.ShapeDtypeStruct((N_HEADS, BQ, KWIN), F32),
        compiler_params=_cparams(),
    )(rel_bias)


def _bias_table_bwd(dtab, name):
    def body(d_ref, o_ref, diag_ref):
        r = lax.broadcasted_iota(jnp.int32, (BQ, BQ), 0)
        c = lax.broadcasted_iota(jnp.int32, (BQ, BQ), 1)
        flip = jnp.where(r + c == BQ - 1, 1.0, 0.0).astype(BF16)
        for h in range(N_HEADS):
            flipped = sum(_dot(flip, piece, NN) for piece in _split3(d_ref[h]))
            padded = jnp.concatenate([flipped, jnp.zeros((BQ, DIAG - KWIN), F32)], axis=1)
            rolled = pltpu.roll(padded, DIAG - (BQ - 1), 1, stride=1, stride_axis=0)
            diag_ref[h:h + 1, :] = jnp.sum(rolled, axis=0, keepdims=True)
        lane = lax.broadcasted_iota(jnp.int32, (DIAG, REL_PAD), 0)
        t = lax.broadcasted_iota(jnp.int32, (DIAG, REL_PAD), 1)
        offset = jnp.where(lane < KWIN, lane, lane - DIAG)
        pick = jnp.where(t == _rel_index(offset), 1.0, 0.0).astype(BF16)
        o_ref[...] = sum(_dot(piece, pick, NN) for piece in _split3(diag_ref[...]))

    return pl.pallas_call(
        body, name=name,
        out_shape=jax.ShapeDtypeStruct((N_HEADS, REL_PAD), F32),
        scratch_shapes=[pltpu.VMEM((N_HEADS, DIAG), F32)],
        compiler_params=_cparams(),
    )(dtab)


def _head_masks():
    lane = lax.broadcasted_iota(jnp.int32, (1, PAIR), 1)
    return [lane // HEAD_DIM == h for h in range(2)]


def _ca_window_specs(nq, col_off=0):
    return [pl.BlockSpec((BQ, PAIR), functools.partial(
        lambda p, i, d: (jnp.clip(i - 2 + d, 0, nq - 1), p + col_off), d=d)) for d in range(3)]


def _softmax_rows(s):
    p = jnp.exp(s - jnp.max(s, axis=-1, keepdims=True))
    return p, jnp.sum(p, axis=-1, keepdims=True)


def _ca_scores(qm, kc, tab_h, i):
    col = lax.broadcasted_iota(jnp.int32, (1, KWIN), 1)
    in_seq = col + (i - 2) * BQ >= 0
    return jnp.where(in_seq, _dot(qm, kc, NT) + tab_h, NEG)


def _ca_fwd(qn, kn, v, tab, name, v_off=0):
    S = qn.shape[0]
    nq = S // BQ
    qspec = pl.BlockSpec((BQ, PAIR), lambda p, i: (i, p))
    tspec = pl.BlockSpec((2, BQ, KWIN), lambda p, i: (p, 0, 0))

    def body(q_ref, k0, k1, k2, v0, v1, v2, tab_ref, o_ref, ot_ref):
        i = pl.program_id(1)
        kc = jnp.concatenate([k0[...], k1[...], k2[...]], axis=0)
        vc = jnp.concatenate([v0[...], v1[...], v2[...]], axis=0)
        qv = q_ref[...]
        masks = _head_masks()
        heads = range(2)
        s = [_ca_scores(jnp.where(masks[h], qv, 0), kc, tab_ref[h], i) for h in heads]
        soft = [_softmax_rows(s[h]) for h in heads]
        o = [_dot(soft[h][0].astype(BF16), vc, NN) / soft[h][1] for h in heads]
        out = jnp.where(masks[0], o[0], o[1])
        o_ref[...] = out.astype(BF16)
        ot_ref[...] = out.T.astype(BF16)

    return pl.pallas_call(
        body, name=name, grid=(WIDTH // PAIR, nq),
        in_specs=[qspec] + _ca_window_specs(nq) + _ca_window_specs(nq, v_off) + [tspec],
        out_specs=[qspec, pl.BlockSpec((PAIR, BQ), lambda p, i: (p, i))],
        out_shape=[jax.ShapeDtypeStruct((S, WIDTH), BF16), jax.ShapeDtypeStruct((WIDTH, S), BF16)],
        compiler_params=_cparams(("parallel", "parallel")),
    )(qn, kn, kn, kn, v, v, v, tab)


def _ca_bwd(qn, kn, v, do, tab, name, v_off=0):
    S = qn.shape[0]
    nq = S // BQ
    qspec = pl.BlockSpec((BQ, PAIR), lambda p, i: (jnp.minimum(i, nq - 1), p))
    kout = pl.BlockSpec((BQ, PAIR), lambda p, i: (jnp.clip(i - 2, 0, nq - 1), p))
    tspec = pl.BlockSpec((2, BQ, KWIN), lambda p, i: (p, 0, 0))

    def body(q_ref, do_ref, k0, k1, k2, v0, v1, v2, tab_ref,
             dq_ref, dk_ref, dv_ref, dtab_ref, dk_acc, dv_acc):
        i = pl.program_id(1)

        @pl.when(i == 0)
        def _():
            dk_acc[...] = jnp.zeros_like(dk_acc)
            dv_acc[...] = jnp.zeros_like(dv_acc)
            dtab_ref[...] = jnp.zeros_like(dtab_ref)

        @pl.when(i < nq)
        def _():
            kc = jnp.concatenate([k0[...], k1[...], k2[...]], axis=0)
            vc = jnp.concatenate([v0[...], v1[...], v2[...]], axis=0)
            qv, dov = q_ref[...], do_ref[...]
            masks = _head_masks()
            heads = range(2)
            qm = [jnp.where(masks[h], qv, 0) for h in heads]
            dom = [jnp.where(masks[h], dov, 0) for h in heads]
            s = [_ca_scores(qm[h], kc, tab_ref[h], i) for h in heads]
            dp = [_dot(dom[h], vc, NT) for h in heads]
            soft = [_softmax_rows(s[h]) for h in heads]
            p = [soft[h][0] / soft[h][1] for h in heads]
            ds = [p[h] * (dp[h] - jnp.sum(p[h] * dp[h], axis=-1, keepdims=True)) for h in heads]
            for h in heads:
                dtab_ref[h] += ds[h]
            dsb = [ds[h].astype(BF16) for h in heads]
            pb = [p[h].astype(BF16) for h in heads]
            dq = [_dot(dsb[h], kc, NN) for h in heads]
            dq_ref[...] = jnp.where(masks[0], dq[0], dq[1])
            dkc = _dot(dsb[0], qm[0], TN) + _dot(dsb[1], qm[1], TN)
            dvc = _dot(pb[0], dom[0], TN) + _dot(pb[1], dom[1], TN)
            for d in range(3):
                slot = (i + 1 + d) % 3
                dk_acc[slot] += dkc[d * BQ:(d + 1) * BQ]
                dv_acc[slot] += dvc[d * BQ:(d + 1) * BQ]

        @pl.when(i >= 2)
        def _():
            slot = (i + 1) % 3
            dk_ref[...] = dk_acc[slot]
            dv_ref[...] = dv_acc[slot].astype(BF16)
            dk_acc[slot] = jnp.zeros((BQ, PAIR), F32)
            dv_acc[slot] = jnp.zeros((BQ, PAIR), F32)

    return pl.pallas_call(
        body, name=name, grid=(WIDTH // PAIR, nq + 2),
        in_specs=[qspec, qspec] + _ca_window_specs(nq) + _ca_window_specs(nq, v_off) + [tspec],
        out_specs=[qspec, kout, kout, tspec],
        out_shape=[jax.ShapeDtypeStruct((S, WIDTH), F32), jax.ShapeDtypeStruct((S, WIDTH), F32),
                   jax.ShapeDtypeStruct((S, WIDTH), BF16), jax.ShapeDtypeStruct((N_HEADS, BQ, KWIN), F32)],
        scratch_shapes=[pltpu.VMEM((3, BQ, PAIR), F32)] * 2,
        compiler_params=_cparams(("parallel", "arbitrary")),
    )(qn, do, kn, kn, kn, v, v, v, tab)


def _sb_consts():
    r = lax.broadcasted_iota(jnp.int32, (BQ, BQ), 0)
    c = lax.broadcasted_iota(jnp.int32, (BQ, BQ), 1)
    from_s = jnp.where(r >= c, 1.0, 0.0).astype(BF16)
    causal = c < r
    return from_s, causal


def _suffix_sum(t, from_s):
    hi, lo = _split_bf16(t)
    return _dot(hi, from_s, NN) + _dot(lo, from_s, NN)


def _neg_abs(x):
    bits = lax.bitcast_convert_type(x, jnp.uint32) | jnp.uint32(0x80000000)
    return lax.bitcast_convert_type(bits, F32)


def _sb_log_keep(zn):
    return jnp.minimum(zn, 0.0) - jnp.log(1.0 + jnp.exp(_neg_abs(zn)))


SB_DEAD = 105.0


def _sb_walk(i, tiles, keep_ref):
    @pl.when(i == 0)
    def _():
        tiles([i], [True])

    @pl.when(i > 0)
    def _():
        tiles([i, i - 1], [True, False])

    def alive():
        return (jnp.max(keep_ref[...]) > -SB_DEAD).astype(jnp.int32)

    def step(state):
        j, _ = state
        tiles([j], [False])
        return j - 1, alive()

    lax.while_loop(lambda state: (state[0] >= 0) & (state[1] > 0), step, (i - 2, alive()))


def _sb_rows(j):
    return pl.ds(pl.multiple_of(j * BQ, BQ), BQ)


def _sb_specs(S, offs):
    def qspec(off=0):
        return pl.BlockSpec((BQ, PAIR), lambda p, i: (i, p + off))

    def kspec(off=0):
        return pl.BlockSpec((S, PAIR), lambda p, i: (0, p + off))

    return qspec, kspec, [qspec(offs[0]), kspec(offs[1]), kspec(offs[2])]


def _sb_fwd(q, k, v, name, offs=(0, 0, 0)):
    S = q.shape[0]
    nq = S // BQ
    qspec, _, qkv_specs = _sb_specs(S, offs)

    def body(q_ref, k_ref, v_ref, o_ref, of_ref, ot_ref, carry_ref, acc_ref):
        i = pl.program_id(1)
        from_s, causal = _sb_consts()
        masks = _head_masks()
        qn = q_ref[...] * -(HEAD_DIM ** -0.5)
        qms = [jnp.where(m, qn, 0) for m in masks]
        carry_ref[...] = jnp.zeros_like(carry_ref)
        acc_ref[...] = jnp.zeros_like(acc_ref)

        def tiles(js, diags):
            chains = [(n, h) for n in range(len(js)) for h in range(2)]
            masked = [c for c in chains if diags[c[0]]]
            kbs = [k_ref[_sb_rows(j), :] for j in js]
            vbs = [v_ref[_sb_rows(j), :] for j in js]
            zn = {c: _dot(qms[c[1]], kbs[c[0]], NT) for c in chains}
            log_keep = {c: _sb_log_keep(zn[c]) for c in chains}
            for c in masked:
                log_keep[c] = jnp.where(causal, log_keep[c], 0.0)
            split = {c: _split_bf16(log_keep[c]) for c in chains}
            carry = {}
            for h in range(2):
                run = carry_ref[h]
                for n in range(len(js)):
                    carry[(n, h)] = run
                    run = run + jnp.sum(log_keep[(n, h)], axis=-1, keepdims=True)
                carry_ref[h] = run
            suffix = {c: _dot(split[c][0], from_s, NN) + _dot(split[c][1], from_s, NN) for c in chains}
            w = {c: jnp.exp(carry[c] + suffix[c] - zn[c]) for c in chains}
            for c in masked:
                w[c] = jnp.where(causal, w[c], 0.0)
            for c in chains:
                acc_ref[c[1]] += _dot(w[c].astype(BF16), vbs[c[0]], NN)

        _sb_walk(i, tiles, carry_ref)
        out = jnp.where(masks[0], acc_ref[0], acc_ref[1])
        o_ref[...] = out.astype(BF16)
        of_ref[...] = out
        ot_ref[...] = out.T.astype(BF16)

    return pl.pallas_call(
        body, name=name, grid=(WIDTH // PAIR, nq),
        in_specs=qkv_specs, out_specs=[qspec(), qspec(), pl.BlockSpec((PAIR, BQ), lambda p, i: (p, i))],
        out_shape=[jax.ShapeDtypeStruct((S, WIDTH), BF16), jax.ShapeDtypeStruct((S, WIDTH), F32),
                   jax.ShapeDtypeStruct((WIDTH, S), BF16)],
        scratch_shapes=[pltpu.VMEM((2, BQ, 1), F32), pltpu.VMEM((2, BQ, PAIR), F32)],
        compiler_params=_cparams(("parallel", "arbitrary")),
    )(q, k, v)


def _sb_bwd(q, k, v, o, do, name, offs=(0, 0, 0)):
    S = q.shape[0]
    nq = S // BQ
    qspec, kspec, qkv_specs = _sb_specs(S, offs)

    def body(q_ref, o_ref, do_ref, k_ref, v_ref, dq_ref, dk_ref, dv_ref, dk_acc, dv_acc, keep_ref, gsum_ref, dq_acc):
        i = pl.program_id(1)

        @pl.when(i == 0)
        def _():
            dk_acc[...] = jnp.zeros_like(dk_acc)
            dv_acc[...] = jnp.zeros_like(dv_acc)

        from_s, causal = _sb_consts()
        masks = _head_masks()
        qn, dov = q_ref[...] * -(HEAD_DIM ** -0.5), do_ref[...]
        od = o_ref[...] * dov.astype(F32)
        qms = [jnp.where(m, qn, 0) for m in masks]
        doms = [jnp.where(m, dov, 0) for m in masks]
        totals = [jnp.sum(jnp.where(m, od, 0.0), axis=-1, keepdims=True) for m in masks]
        for ref in (keep_ref, gsum_ref, dq_acc):
            ref[...] = jnp.zeros_like(ref)

        def running(ref, vals, n_blocks):
            before_chain = {}
            for h in range(2):
                run = ref[h]
                for n in range(n_blocks):
                    before_chain[(n, h)] = run
                    run = run + jnp.sum(vals[(n, h)], axis=-1, keepdims=True)
                ref[h] = run
            return before_chain

        def tiles(js, diags):
            chains = [(n, h) for n in range(len(js)) for h in range(2)]
            masked = [c for c in chains if diags[c[0]]]
            kbs = [k_ref[_sb_rows(j), :] for j in js]
            vbs = [v_ref[_sb_rows(j), :] for j in js]
            zn = {c: _dot(qms[c[1]], kbs[c[0]], NT) for c in chains}
            dw = {c: _dot(doms[c[1]], vbs[c[0]], NT) for c in chains}
            log_keep = {c: _sb_log_keep(zn[c]) for c in chains}
            for c in masked:
                log_keep[c] = jnp.where(causal, log_keep[c], 0.0)
            split = {c: _split_bf16(log_keep[c]) for c in chains}
            kept = running(keep_ref, log_keep, len(js))
            suffix = {c: _dot(split[c][0], from_s, NN) + _dot(split[c][1], from_s, NN) for c in chains}
            w = {c: jnp.exp(kept[c] + suffix[c] - zn[c]) for c in chains}
            for c in masked:
                w[c] = jnp.where(causal, w[c], 0.0)
            wb = {c: w[c].astype(BF16) for c in chains}
            g = {c: wb[c].astype(F32) * dw[c] for c in chains}
            gsplit = {c: _split_bf16(g[c]) for c in chains}
            gsum = running(gsum_ref, g, len(js))
            gsuffix = {c: _dot(gsplit[c][0], from_s, NN) + _dot(gsplit[c][1], from_s, NN) for c in chains}
            dzb = {}
            for c in chains:
                before = totals[c[1]] - (gsum[c] + gsuffix[c])
                dz = (g[c] + before) * jnp.exp(log_keep[c]) - before
                if c in masked:
                    dz = jnp.where(causal, dz, 0.0)
                dzb[c] = dz.astype(BF16)
            for c in chains:
                rows = _sb_rows(js[c[0]])
                dq_acc[c[1]] += _dot(dzb[c], kbs[c[0]], NN)
                dk_acc[rows, :] -= _dot(dzb[c], qms[c[1]], TN)
                dv_acc[rows, :] += _dot(wb[c], doms[c[1]], TN)

        _sb_walk(i, tiles, keep_ref)
        dq_ref[...] = (jnp.where(masks[0], dq_acc[0], dq_acc[1]) * HEAD_DIM ** -0.5).astype(BF16)

        @pl.when(i == nq - 1)
        def _():
            dk_ref[...] = dk_acc[...].astype(BF16)
            dv_ref[...] = dv_acc[...].astype(BF16)

    return pl.pallas_call(
        body, name=name, grid=(WIDTH // PAIR, nq),
        in_specs=[qkv_specs[0], qspec(), qspec(), qkv_specs[1], qkv_specs[2]], out_specs=[qspec(), kspec(), kspec()],
        out_shape=[jax.ShapeDtypeStruct((S, WIDTH), BF16)] * 3,
        scratch_shapes=[pltpu.VMEM((S, PAIR), F32)] * 2 + [pltpu.VMEM((2, BQ, 1), F32)] * 2
        + [pltpu.VMEM((2, BQ, PAIR), F32)],
        compiler_params=_cparams(("parallel", "arbitrary")),
    )(q, o, do, k, v)


ANY = pl.BlockSpec(memory_space=pl.ANY)


def _place():
    return lax.axis_index("x"), lax.axis_index("y"), lax.axis_index("c")


def _other_chips(x, y):
    return [(2 * px + py, (px, py)) for px, py in ((1 - x, y), (x, 1 - y), (1 - x, 1 - y))]


def _remote(src, dst, sems, k, to):
    return pltpu.make_async_remote_copy(src_ref=src, dst_ref=dst, send_sem=sems[0].at[k], recv_sem=sems[1].at[k],
                                        device_id=to, device_id_type=MESH)


def _gather_weights(ws, extra, name):
    n = len(ws)

    def body(*refs):
        w_refs, e_ref, out_refs, eo_ref = refs[:n], refs[n], refs[n + 1:2 * n + 1], refs[2 * n + 1]
        sems = refs[2 * n + 2:]
        x, y, c = _place()
        me = 2 * x + y
        chips = _other_chips(x, y)

        def halves(ref):
            rh = ref.shape[-2] // 2
            return pl.ds(c * rh, rh), pl.ds((1 - c) * rh, rh)

        first = [_remote(w_ref.at[halves(w_ref)[0]], o_ref.at[me, halves(w_ref)[0]], sems, 6 * a + k, (*xy, c))
                 for a, (w_ref, o_ref) in enumerate(zip(w_refs, out_refs)) for k, (_, xy) in enumerate(chips)]
        first += [_remote(e_ref, eo_ref.at[me], sems, 6 * n + k, (*xy, c)) for k, (_, xy) in enumerate(chips)]
        for cp in first:
            cp.start()
        passed = []
        for a, o_ref in enumerate(out_refs):
            for k, (chip, xy) in enumerate(chips):
                landed = o_ref.at[chip, halves(o_ref)[0]]
                _remote(landed, landed, sems, 6 * a + k, (*xy, c)).wait_recv()
                cp = _remote(landed, landed, sems, 6 * a + 3 + k, (x, y, 1 - c))
                cp.start()
                passed.append(cp)
        for a, o_ref in enumerate(out_refs):
            for k, (chip, xy) in enumerate(chips):
                landed = o_ref.at[chip, halves(o_ref)[1]]
                _remote(landed, landed, sems, 6 * a + 3 + k, (x, y, 1 - c)).wait_recv()
        for k, (chip, xy) in enumerate(chips):
            _remote(e_ref, eo_ref.at[chip], sems, 6 * n + k, (*xy, c)).wait_recv()
        for cp in first + passed:
            cp.wait_send()

    n_copies = 6 * n + 3
    return pl.pallas_call(
        body, name=name, in_specs=[ANY] * (n + 1), out_specs=[ANY] * (n + 1),
        out_shape=[jax.ShapeDtypeStruct((N_CHIPS,) + w.shape, w.dtype) for w in ws]
        + [jax.ShapeDtypeStruct((N_CHIPS,) + extra.shape, extra.dtype)],
        scratch_shapes=[pltpu.SemaphoreType.DMA((n_copies,)), pltpu.SemaphoreType.DMA((n_copies,))],
    )(*ws, extra)


def _exchange_cores(gs, small, name):
    n = len(gs)

    def body(*refs):
        g_refs, small_ref, sib_refs, all_ref = refs[:n], refs[n], refs[n + 1:2 * n + 1], refs[2 * n + 1]
        sems = refs[2 * n + 2:]
        x, y, c = _place()
        me = 4 * x + 2 * y + c
        copies = []
        for a, (g_ref, sib_ref) in enumerate(zip(g_refs, sib_refs)):
            rh = g_ref.shape[1] // 2
            copies.append(_remote(g_ref.at[:, pl.ds((1 - c) * rh, rh), :], sib_ref, sems, a, (x, y, 1 - c)))
        k = n
        for fx in (0, 1):
            for fy in (0, 1):
                for fc in (0, 1):
                    if fx or fy or fc:
                        to = (1 - x if fx else x, 1 - y if fy else y, 1 - c if fc else c)
                        copies.append(_remote(small_ref, all_ref.at[me], sems, k, to))
                        k += 1
        for cp in copies:
            cp.start()
        for cp in copies:
            cp.wait_recv()
        for cp in copies:
            cp.wait_send()

    n_copies = n + N_DEV - 1
    return pl.pallas_call(
        body, name=name, in_specs=[ANY] * (n + 1), out_specs=[ANY] * (n + 1),
        out_shape=[jax.ShapeDtypeStruct((N_CHIPS, g.shape[1] // 2, g.shape[2]), F32) for g in gs]
        + [jax.ShapeDtypeStruct((N_DEV,) + small.shape, F32)],
        scratch_shapes=[pltpu.SemaphoreType.DMA((n_copies,)), pltpu.SemaphoreType.DMA((n_copies,))],
    )(*gs, small)


def _exchange_chips(ps, name):
    n = len(ps)

    def body(*refs):
        p_refs, out_refs, sems = refs[:n], refs[n:2 * n], refs[2 * n:]
        x, y, c = _place()
        me = 2 * x + y
        chips = _other_chips(x, y)
        copies = [_remote(p_ref.at[chip], o_ref.at[me], sems, 3 * a + k, (*xy, c))
                  for a, (p_ref, o_ref) in enumerate(zip(p_refs, out_refs)) for k, (chip, xy) in enumerate(chips)]
        for cp in copies:
            cp.start()
        for a, (p_ref, o_ref) in enumerate(zip(p_refs, out_refs)):
            for k, (chip, xy) in enumerate(chips):
                _remote(p_ref.at[chip], o_ref.at[chip], sems, 3 * a + k, (*xy, c)).wait_recv()
        for cp in copies:
            cp.wait_send()

    return pl.pallas_call(
        body, name=name, in_specs=[ANY] * n, out_specs=[ANY] * n,
        out_shape=[jax.ShapeDtypeStruct(p.shape, p.dtype) for p in ps],
        scratch_shapes=[pltpu.SemaphoreType.DMA((3 * n,)), pltpu.SemaphoreType.DMA((3 * n,))],
    )(*ps)


def _share_halves(ghs, name):
    n = len(ghs)

    def body(*refs):
        gh_refs, out_refs, sems = refs[:n], refs[n:2 * n], refs[2 * n:]
        x, y, c = _place()
        copies = [_remote(gh_ref, o_ref, sems, a, (x, y, 1 - c)) for a, (gh_ref, o_ref) in enumerate(zip(gh_refs, out_refs))]
        for cp in copies:
            cp.start()
        for cp in copies:
            cp.wait_recv()
        for cp in copies:
            cp.wait_send()

    return pl.pallas_call(
        body, name=name, in_specs=[ANY] * n, out_specs=[ANY] * n,
        out_shape=[jax.ShapeDtypeStruct(g.shape, g.dtype) for g in ghs],
        scratch_shapes=[pltpu.SemaphoreType.DMA((n,)), pltpu.SemaphoreType.DMA((n,))],
    )(*ghs)


EW_BLOCK_BYTES = 2 * 1024 * 1024


def _row_block(rows, cols, mult=8):
    fits = [b for b in range(mult, rows + 1, mult) if rows % b == 0 and b * cols * 4 <= EW_BLOCK_BYTES]
    return max(fits) if fits else mult


def _add2(a, b, name):
    R, C = a.shape
    rows = _row_block(R, C, mult=16)
    spec = pl.BlockSpec((rows, C), lambda i: (i, 0))

    def body(a_ref, b_ref, o_ref):
        o_ref[...] = (a_ref[...] + b_ref[...]).astype(BF16)

    return pl.pallas_call(
        body, name=name, grid=(R // rows,), in_specs=[spec, spec], out_specs=spec,
        out_shape=jax.ShapeDtypeStruct(a.shape, BF16),
        compiler_params=_cparams(("parallel",)),
    )(a, b)


def _sum_leading(a, name):
    n, R, C = a.shape
    rows = _row_block(R, n * C, mult=16 if a.dtype == BF16 else 8)

    def body(a_ref, o_ref):
        acc = a_ref[0].astype(F32)
        for j in range(1, n):
            acc = acc + a_ref[j].astype(F32)
        o_ref[...] = acc

    return pl.pallas_call(
        body, name=name, grid=(R // rows,),
        in_specs=[pl.BlockSpec((n, rows, C), lambda i: (0, i, 0))],
        out_specs=pl.BlockSpec((rows, C), lambda i: (i, 0)),
        out_shape=jax.ShapeDtypeStruct((R, C), F32),
        compiler_params=_cparams(("parallel",)),
    )(a)


def _adamw(w, g, m, v, name):
    R, C = w.shape
    rows = _row_block(R, C)
    spec = pl.BlockSpec((rows, C), lambda i: (i, 0))

    def body(w_ref, g_ref, m_ref, v_ref, d_ref, mo_ref, vo_ref):
        gv = g_ref[...]
        mn = ADAM_B1 * m_ref[...] + (1.0 - ADAM_B1) * gv
        vn = ADAM_B2 * v_ref[...] + (1.0 - ADAM_B2) * (gv * gv)
        m_hat = mn / (1.0 - ADAM_B1 ** ADAM_STEP)
        v_hat = vn / (1.0 - ADAM_B2 ** ADAM_STEP)
        d_ref[...] = -ADAM_LR * (m_hat / (jnp.sqrt(v_hat) + ADAM_EPS) + ADAM_WD * w_ref[...])
        mo_ref[...] = mn
        vo_ref[...] = vn

    return pl.pallas_call(
        body, name=name, grid=(R // rows,), in_specs=[spec] * 4, out_specs=[spec] * 3,
        out_shape=[jax.ShapeDtypeStruct((R, C), F32)] * 3,
        compiler_params=_cparams(("parallel",)),
    )(w, g, m, v)


BIG = ("w_in", "w_branch_a", "w_branch_b", "w_out", "w_ffn_up", "w_ffn_down")
COL_SHARDED = {"w_in": True, "w_branch_a": True, "w_branch_b": True, "w_out": False, "w_ffn_up": True,
               "w_ffn_down": False}
CONV_W_COLS = 2 * D_FF // N_CHIPS
SMALL_REPLICATED = (("norm1_g", D_MODEL), ("q_norm_g", HEAD_DIM), ("k_norm_g", HEAD_DIM),
                    ("rel_bias", N_HEADS * N_REL), ("norm2_g", D_MODEL), ("ffn_conv_b", 2 * D_FF))
SMALL_GRADS = SMALL_REPLICATED + (("ffn_conv_w", 3 * 2 * D_FF),)
SMALL_OWN = SMALL_REPLICATED + (("ffn_conv_w", 3 * CONV_W_COLS),)
SMALL_GRAD_ROWS = 32
SMALL_OWN_ROWS = 16


def _whole(name, stacked):
    return jnp.concatenate(list(stacked), axis=1) if COL_SHARDED[name] else stacked.reshape(-1, stacked.shape[2])


def _pack_small(vals, sizes, rows):
    flat = jnp.concatenate([vals[n].reshape(-1) for n, _ in sizes])
    return jnp.pad(flat, (0, rows * PACK_COLS - flat.shape[0])).reshape(rows, PACK_COLS)


def _unpack_small(packed, sizes):
    flat, out, o = packed.reshape(-1), {}, 0
    for n, sz in sizes:
        out[n] = flat[o:o + sz]
        o += sz
    return out


def kernel(x, norm1_g, w_in, q_norm_g, k_norm_g, rel_bias, w_branch_a, w_branch_b, w_out, norm2_g, w_ffn_up, ffn_conv_w, ffn_conv_b, w_ffn_down, loss_target, m_norm1_g, m_w_in, m_q_norm_g, m_k_norm_g, m_rel_bias, m_w_branch_a, m_w_branch_b, m_w_out, m_norm2_g, m_w_ffn_up, m_ffn_conv_w, m_ffn_conv_b, m_w_ffn_down, v_norm1_g, v_w_in, v_q_norm_g, v_k_norm_g, v_rel_bias, v_w_branch_a, v_w_branch_b, v_w_out, v_norm2_g, v_w_ffn_up, v_ffn_conv_w, v_ffn_conv_b, v_w_ffn_down):
    w_big = {"w_in": w_in[0], "w_branch_a": w_branch_a[0], "w_branch_b": w_branch_b[0], "w_out": w_out[0],
             "w_ffn_up": w_ffn_up[0], "w_ffn_down": w_ffn_down[0]}
    m_big = {"w_in": m_w_in[0], "w_branch_a": m_w_branch_a[0], "w_branch_b": m_w_branch_b[0], "w_out": m_w_out[0],
             "w_ffn_up": m_w_ffn_up[0], "w_ffn_down": m_w_ffn_down[0]}
    v_big = {"w_in": v_w_in[0], "w_branch_a": v_w_branch_a[0], "w_branch_b": v_w_branch_b[0], "w_out": v_w_out[0],
             "w_ffn_up": v_w_ffn_up[0], "w_ffn_down": v_w_ffn_down[0]}
    xs, tgt = x[0], loss_target[0]

    xi, yi, ci = _place()
    chip = 2 * xi + yi

    def with_own(stacked, own):
        return lax.dynamic_update_slice(stacked, own[None], (chip,) + (0,) * own.ndim)

    shards_bf = [w_big[n].astype(BF16) for n in BIG]
    conv_own = jnp.pad(ffn_conv_w[0], ((0, 8 - ffn_conv_w.shape[1]), (0, 0)))
    *gathered, conv_all = _gather_weights(shards_bf, conv_own, "gather_weights")
    full = {n: _whole(n, with_own(g, s)) for n, g, s in zip(BIG, gathered, shards_bf)}
    conv_w = jnp.concatenate(list(with_own(conv_all, conv_own)[:, :3]), axis=1)
    w_in_f, w_a, w_b, w_o = full["w_in"], full["w_branch_a"], full["w_branch_b"], full["w_out"]
    w_up, w_dn = full["w_ffn_up"], full["w_ffn_down"]
    w_in_t, w_a_t, w_b_t, w_o_t, w_up_t, w_dn_t = (w.T for w in (w_in_f, w_a, w_b, w_o, w_up, w_dn))

    hn, hn_t = _rms_fwd(xs, norm1_g, "rms1")
    qk = _matmul(hn, w_in_f[:, :2 * WIDTH], F32, "proj_qk")
    vqkv = _matmul(hn, w_in_f[:, 2 * WIDTH:6 * WIDTH], BF16, "proj_vqkv")
    g_a = _matmul(hn, w_in_f[:, 6 * WIDTH:6 * WIDTH + D_MODEL], F32, "proj_gate_a")
    g_b = _matmul(hn, w_in_f[:, 6 * WIDTH + D_MODEL:], F32, "proj_gate_b")
    gq = jnp.tile(q_norm_g, (1, N_HEADS))
    gk = jnp.tile(k_norm_g, (1, N_HEADS))
    qa, ka = _qknorm_fwd(qk, gq, gk, "qknorm")
    per = WIDTH // PAIR
    b_offs = (per, 2 * per, 3 * per)
    tab = _bias_table(jnp.pad(rel_bias[0], ((0, 0), (0, REL_PAD - N_REL))), "bias_table")
    out_a, out_a_t = _ca_fwd(qa, ka, vqkv, tab, "chunk_attn")
    out_b, out_b_f32, out_b_t = _sb_fwd(vqkv, vqkv, vqkv, "stick_attn", b_offs)
    y_a = _matmul(out_a, w_a, F32, "branch_a")
    y_b = _matmul(out_b, w_b, F32, "branch_b")
    mixed, mixed_t = _mix_fwd(g_a, g_b, y_a, y_b, "mix")
    x2 = _matmul(mixed, w_o, F32, "out_proj", residual=xs)
    hn2, hn2_t = _rms_fwd(x2, norm2_g, "rms2")
    hid = _matmul(hn2, w_up, BF16, "ffn_up")
    act, act_t = _convglu_fwd(hid, conv_w, ffn_conv_b, "convglu")
    y = _matmul(act, w_dn, F32, "ffn_down", residual=x2)
    dy, dyb, sq = _loss_head(y, tgt, "loss_head")
    loss = lax.psum(0.5 / D_MODEL * jnp.sum(sq), ("x", "y", "c"))

    dact = _matmul(dyb, w_dn_t, BF16, "d_act")
    d_w_dn = _matmul(act_t, dyb, F32, "d_w_down")
    dhg, dhu, dcwg, dcwu, dcbg, dcbu = _convglu_bwd(hid, dact, conv_w, ffn_conv_b, "convglu_bwd")
    half_chips = N_CHIPS // 2
    d_w_up = jnp.concatenate([_matmul(hn2_t, dhg, F32, "d_w_up_gate", slabs=half_chips),
                              _matmul(hn2_t, dhu, F32, "d_w_up_up", slabs=half_chips)], axis=0)
    dhn2 = _matmul(dhg, w_up_t[:D_FF], F32, "d_hn2_gate")
    dhn2 = _matmul(dhu, w_up_t[D_FF:], F32, "d_hn2_up", residual=dhn2)
    dx2, dx2b, d_norm2 = _rms_bwd(x2, norm2_g, dhn2, dy, "rms2_bwd")
    dmixed = _matmul(dx2b, w_o_t, F32, "d_mixed")
    d_w_o = _matmul(mixed_t, dx2b, F32, "d_w_out")
    dga, dgb, dya, dyb_b = _mix_bwd(dmixed, g_a, g_b, y_a, y_b, "mix_bwd")
    d_w_a = _matmul(out_a_t, dya, F32, "d_w_branch_a", slabs=N_CHIPS)
    d_w_b = _matmul(out_b_t, dyb_b, F32, "d_w_branch_b", slabs=N_CHIPS)
    do_a = _matmul(dya, w_a_t, BF16, "d_out_a")
    do_b = _matmul(dyb_b, w_b_t, BF16, "d_out_b")
    dqb, dkb, dvb = _sb_bwd(vqkv, vqkv, vqkv, out_b_f32, do_b, "stick_attn_bwd", b_offs)
    dqa_n, dka_n, dva, dtab = _ca_bwd(qa, ka, vqkv, do_a, tab, "chunk_attn_bwd")
    d_rel = _bias_table_bwd(dtab, "bias_table_bwd")[:, :N_REL]
    dqa, dka, dgq, dgk = _qknorm_bwd(qk, gq, gk, dqa_n, dka_n, "qknorm_bwd")
    dproj = jnp.concatenate([dqa, dka, dva, dqb, dkb, dvb, dga, dgb], axis=1)
    d_w_in = _matmul(hn_t, dproj, F32, "d_w_in", slabs=N_CHIPS)
    dhn = _matmul(dproj, w_in_t, F32, "d_hn")
    dx, _, d_norm1 = _rms_bwd(xs, norm1_g, dhn, dx2, "rms1_bwd")

    grads_full = {"w_in": d_w_in, "w_branch_a": d_w_a, "w_branch_b": d_w_b, "w_ffn_up": d_w_up,
                  "w_out": d_w_o.reshape(N_CHIPS, -1, D_MODEL), "w_ffn_down": d_w_dn.reshape(N_CHIPS, -1, D_MODEL)}
    gs = [grads_full[n] for n in BIG]
    small_g = _pack_small({"norm1_g": d_norm1, "q_norm_g": dgq.reshape(N_HEADS, HEAD_DIM).sum(0),
                           "k_norm_g": dgk.reshape(N_HEADS, HEAD_DIM).sum(0), "rel_bias": d_rel,
                           "norm2_g": d_norm2, "ffn_conv_b": jnp.concatenate([dcbg, dcbu], axis=1),
                           "ffn_conv_w": jnp.concatenate([dcwg, dcwu], axis=1)}, SMALL_GRADS, SMALL_GRAD_ROWS)
    *sibs, small_all = _exchange_cores(gs, small_g, "exchange_cores")
    chip_parts = []
    for n, g, sib in zip(BIG, gs, sibs):
        rh, cols = sib.shape[1], sib.shape[2]
        mine = lax.dynamic_slice_in_dim(g, ci * rh, rh, axis=1)
        chip_parts.append(_add2(mine.reshape(-1, cols), sib.reshape(-1, cols), "sum_cores_" + n).reshape(sib.shape))
    parts = _exchange_chips(chip_parts, "exchange_chips")
    g_halves = [_sum_leading(with_own(q, lax.dynamic_index_in_dim(p, chip, 0, keepdims=False)), "sum_chips_" + n)
                for n, p, q in zip(BIG, chip_parts, parts)]
    g_others = _share_halves(g_halves, "share_halves")
    grads = {n: jnp.concatenate([jnp.where(ci == 0, mine, other), jnp.where(ci == 0, other, mine)], axis=0)
             for n, mine, other in zip(BIG, g_halves, g_others)}
    small_all = lax.dynamic_update_slice(small_all, small_g[None], (4 * xi + 2 * yi + ci, 0, 0))
    small_sum = _unpack_small(_sum_leading(small_all, "sum_small"), SMALL_GRADS)
    small_sum["ffn_conv_w"] = lax.dynamic_slice_in_dim(small_sum["ffn_conv_w"].reshape(3, 2 * D_FF),
                                                       chip * CONV_W_COLS, CONV_W_COLS, axis=1)

    deltas, new_m, new_v = {}, {}, {}
    for n in BIG:
        deltas[n], new_m[n], new_v[n] = _adamw(w_big[n], grads[n], m_big[n], v_big[n], "adamw_" + n)

    shapes = {"norm1_g": norm1_g.shape, "q_norm_g": q_norm_g.shape, "k_norm_g": k_norm_g.shape,
              "rel_bias": rel_bias.shape, "norm2_g": norm2_g.shape, "ffn_conv_b": ffn_conv_b.shape,
              "ffn_conv_w": ffn_conv_w.shape}
    small_w = {"norm1_g": norm1_g, "q_norm_g": q_norm_g, "k_norm_g": k_norm_g, "rel_bias": rel_bias,
               "norm2_g": norm2_g, "ffn_conv_b": ffn_conv_b, "ffn_conv_w": ffn_conv_w}
    small_m = {"norm1_g": m_norm1_g, "q_norm_g": m_q_norm_g, "k_norm_g": m_k_norm_g, "rel_bias": m_rel_bias,
               "norm2_g": m_norm2_g, "ffn_conv_b": m_ffn_conv_b, "ffn_conv_w": m_ffn_conv_w}
    small_v = {"norm1_g": v_norm1_g, "q_norm_g": v_q_norm_g, "k_norm_g": v_k_norm_g, "rel_bias": v_rel_bias,
               "norm2_g": v_norm2_g, "ffn_conv_b": v_ffn_conv_b, "ffn_conv_w": v_ffn_conv_w}
    ds, ms, vs = _adamw(*(_pack_small(t, SMALL_OWN, SMALL_OWN_ROWS) for t in (small_w, small_sum, small_m, small_v)),
                        "adamw_small")
    small_grads = small_sum
    ds, ms, vs = (_unpack_small(t, SMALL_OWN) for t in (ds, ms, vs))

    order = ("norm1_g", "w_in", "q_norm_g", "k_norm_g", "rel_bias", "w_branch_a", "w_branch_b", "w_out",
             "norm2_g", "w_ffn_up", "ffn_conv_w", "ffn_conv_b", "w_ffn_down")
    outs = [loss, dx[None]]
    for big, small in ((grads, small_grads), (deltas, ds), (new_m, ms), (new_v, vs)):
        for n in order:
            outs.append(big[n][None] if n in big else small[n].reshape(shapes[n]))
    return tuple(outs)
```

```python
import functools

import jax
import jax.numpy as jnp
from jax import lax
from jax.experimental import pallas as pl
from jax.experimental.pallas import tpu as pltpu

F32 = jnp.float32
BF16 = jnp.bfloat16
MESH = pl.DeviceIdType.MESH

D_MODEL = 1024
HEAD_DIM = 64
N_HEADS = 8
WIDTH = N_HEADS * HEAD_DIM
CHUNK = 64
LEFT_CHUNKS = 8
MAX_REL = 128
N_REL = 2 * MAX_REL + 1
D_FF = 2816
EPS = 1e-6
NEG = -1e30

ADAM_LR = 0.001
ADAM_B1 = 0.9
ADAM_B2 = 0.999
ADAM_EPS = 1e-08
ADAM_WD = 0.01
ADAM_STEP = 10

N_CHIPS = 4
N_DEV = 8
LANES = 128
PAIR = 2 * HEAD_DIM
BQ = 256
BAND = LEFT_CHUNKS * CHUNK
KWIN = BAND + BQ
VMEM_LIMIT = 56 * 1024 * 1024
PACK_COLS = 1024

NN = (((1,), (0,)), ((), ()))
NT = (((1,), (1,)), ((), ()))
TN = (((0,), (0,)), ((), ()))


def _cparams(sem=None):
    if sem is None:
        return pltpu.CompilerParams(vmem_limit_bytes=VMEM_LIMIT)
    return pltpu.CompilerParams(dimension_semantics=sem, vmem_limit_bytes=VMEM_LIMIT)


def _pick(n, cands):
    for c in cands:
        if n % c == 0:
            return c
    raise ValueError(f"no block for {n}")


def _dot(a, b, dn):
    return lax.dot_general(a, b, dn, preferred_element_type=F32)


def _sigmoid(x):
    return 0.5 * jnp.tanh(0.5 * x) + 0.5


def _split_bf16(x):
    hi = x.astype(BF16)
    lo = (x - hi.astype(F32)).astype(BF16)
    return hi, lo


def _matmul(a, b, out_dtype, name, residual=None, slabs=None):
    (M, K), N = a.shape, b.shape[1]
    bk = K if K <= D_FF else _pick(K, (1024, 512))
    bm = 512 if D_MODEL < K <= D_FF else _pick(M, (1024, D_FF // 4, 512, 256, 128))
    bn = N // slabs if slabs else _pick(N, (D_FF // 2, 512, 256, 128))
    nk = K // bk
    dn = NN
    a_spec = pl.BlockSpec((bm, bk), lambda i, j, k: (i, k))
    b_spec = pl.BlockSpec((bk, bn), lambda i, j, k: (k, j))
    if slabs:
        o_spec = pl.BlockSpec((None, bm, bn), lambda i, j, k: (j, i, 0))
        out_shape = jax.ShapeDtypeStruct((slabs, M, bn), out_dtype)
    else:
        o_spec = pl.BlockSpec((bm, bn), lambda i, j, k: (i, j))
        out_shape = jax.ShapeDtypeStruct((M, N), out_dtype)
    has_res = residual is not None

    def body(*refs):
        if has_res:
            a_ref, b_ref, r_ref, o_ref, acc_ref = refs
        else:
            a_ref, b_ref, o_ref, acc_ref = refs
        k = pl.program_id(2)
        part = _dot(a_ref[...], b_ref[...], dn)

        def finish(total):
            if has_res:
                total = total + r_ref[...]
            o_ref[...] = total.astype(out_dtype)

        if nk == 1:
            finish(part)
        else:
            @pl.when(k == 0)
            def _():
                acc_ref[...] = part

            @pl.when(k > 0)
            def _():
                acc_ref[...] += part

            @pl.when(k == nk - 1)
            def _():
                finish(acc_ref[...])

    in_specs = [a_spec, b_spec] + ([o_spec] if has_res else [])
    args = (a, b) + ((residual,) if has_res else ())
    return pl.pallas_call(
        body, name=name,
        grid=(M // bm, N // bn, nk),
        in_specs=in_specs, out_specs=o_spec, out_shape=out_shape,
        scratch_shapes=[pltpu.VMEM((bm, bn) if nk > 1 else (8, LANES), F32)],
        compiler_params=_cparams(("parallel", "parallel", "arbitrary")),
    )(*args)


ROWS = 512


def _row_spec(cols, bm=ROWS):
    return pl.BlockSpec((bm, cols), lambda i: (i, 0))


def _col_spec(rows, bn=ROWS):
    return pl.BlockSpec((rows, bn), lambda i: (0, i))


def _full_spec(shape):
    return pl.BlockSpec(shape, lambda i: (0,) * len(shape))


def _colsum8(t):
    return jnp.sum(t.reshape(t.shape[0] // 8, 8, t.shape[1]), axis=0)


def _rms_fwd(x, g, name):
    S, D = x.shape

    def body(x_ref, g_ref, o_ref, ot_ref):
        xv = x_ref[...]
        r = lax.rsqrt(jnp.mean(xv * xv, axis=-1, keepdims=True) + EPS)
        y = xv * r * g_ref[...]
        o_ref[...] = y.astype(BF16)
        ot_ref[...] = y.T.astype(BF16)

    return pl.pallas_call(
        body, name=name, grid=(S // ROWS,),
        in_specs=[_row_spec(D), _full_spec((1, D))], out_specs=[_row_spec(D), _col_spec(D)],
        out_shape=[jax.ShapeDtypeStruct((S, D), BF16), jax.ShapeDtypeStruct((D, S), BF16)],
        compiler_params=_cparams(("parallel",)),
    )(x, g)


def _rms_bwd(x, g, dy, dres, name):
    S, D = x.shape
    nt = S // ROWS

    def body(x_ref, g_ref, dy_ref, dres_ref, dx_ref, dxb_ref, dg_ref, acc_ref):
        i = pl.program_id(0)
        xv, dyv = x_ref[...], dy_ref[...]
        r = lax.rsqrt(jnp.mean(xv * xv, axis=-1, keepdims=True) + EPS)
        xr = xv * r
        u = dyv * g_ref[...]
        dx = r * u - xr * (r * r) * jnp.mean(xv * u, axis=-1, keepdims=True) + dres_ref[...]
        dx_ref[...] = dx
        dxb_ref[...] = dx.astype(BF16)
        part = _colsum8(dyv * xr)

        @pl.when(i == 0)
        def _():
            acc_ref[...] = part

        @pl.when(i > 0)
        def _():
            acc_ref[...] += part

        @pl.when(i == nt - 1)
        def _():
            dg_ref[...] = jnp.sum(acc_ref[...], axis=0, keepdims=True)

    return pl.pallas_call(
        body, name=name, grid=(nt,),
        in_specs=[_row_spec(D), _full_spec((1, D)), _row_spec(D), _row_spec(D)],
        out_specs=[_row_spec(D), _row_spec(D), _full_spec((1, D))],
        out_shape=[jax.ShapeDtypeStruct((S, D), F32), jax.ShapeDtypeStruct((S, D), BF16),
                   jax.ShapeDtypeStruct((1, D), F32)],
        scratch_shapes=[pltpu.VMEM((8, D), F32)],
        compiler_params=_cparams(("arbitrary",)),
    )(x, g, dy, dres)


def _head_mean(t, blockdiag):
    hi, lo = _split_bf16(t)
    return (_dot(hi, blockdiag, NN) + _dot(lo, blockdiag, NN)) * (1.0 / HEAD_DIM)


def _blockdiag():
    r = lax.broadcasted_iota(jnp.int32, (WIDTH, WIDTH), 0) // HEAD_DIM
    c = lax.broadcasted_iota(jnp.int32, (WIDTH, WIDTH), 1) // HEAD_DIM
    return jnp.where(r == c, 1.0, 0.0).astype(BF16)


def _qknorm_fwd(qk, gq, gk, name):
    S = qk.shape[0]

    def body(qk_ref, gq_ref, gk_ref, q_ref, k_ref):
        bd = _blockdiag()
        for part, g_ref, o_ref, scale in ((0, gq_ref, q_ref, HEAD_DIM ** -0.5), (1, gk_ref, k_ref, 1.0)):
            t = qk_ref[:, part * WIDTH:(part + 1) * WIDTH]
            r = lax.rsqrt(_head_mean(t * t, bd) + EPS)
            o_ref[...] = (t * r * g_ref[...] * scale).astype(BF16)

    return pl.pallas_call(
        body, name=name, grid=(S // ROWS,),
        in_specs=[_row_spec(2 * WIDTH), _full_spec((1, WIDTH)), _full_spec((1, WIDTH))],
        out_specs=[_row_spec(WIDTH), _row_spec(WIDTH)],
        out_shape=[jax.ShapeDtypeStruct((S, WIDTH), BF16)] * 2,
        compiler_params=_cparams(("parallel",)),
    )(qk, gq, gk)


def _qknorm_bwd(qk, gq, gk, dqn, dkn, name):
    S = qk.shape[0]
    nt = S // ROWS

    def body(qk_ref, gq_ref, gk_ref, dqn_ref, dkn_ref, dq_ref, dk_ref, dgq_ref, dgk_ref, accq_ref, acck_ref):
        i = pl.program_id(0)
        bd = _blockdiag()
        for part, g_ref, dn_ref, o_ref, dg_ref, acc_ref, scale in (
                (0, gq_ref, dqn_ref, dq_ref, dgq_ref, accq_ref, HEAD_DIM ** -0.5),
                (1, gk_ref, dkn_ref, dk_ref, dgk_ref, acck_ref, 1.0)):
            t = qk_ref[:, part * WIDTH:(part + 1) * WIDTH]
            dn = dn_ref[...] * scale
            r = lax.rsqrt(_head_mean(t * t, bd) + EPS)
            u = dn * g_ref[...]
            dt = r * u - t * (r * r * r) * _head_mean(t * u, bd)
            o_ref[...] = dt.astype(BF16)
            psum = _colsum8(dn * t * r)

            @pl.when(i == 0)
            def _():
                acc_ref[...] = psum

            @pl.when(i > 0)
            def _():
                acc_ref[...] += psum

            @pl.when(i == nt - 1)
            def _():
                dg_ref[...] = jnp.sum(acc_ref[...], axis=0, keepdims=True)

    return pl.pallas_call(
        body, name=name, grid=(nt,),
        in_specs=[_row_spec(2 * WIDTH), _full_spec((1, WIDTH)), _full_spec((1, WIDTH)),
                  _row_spec(WIDTH), _row_spec(WIDTH)],
        out_specs=[_row_spec(WIDTH), _row_spec(WIDTH), _full_spec((1, WIDTH)), _full_spec((1, WIDTH))],
        out_shape=[jax.ShapeDtypeStruct((S, WIDTH), BF16)] * 2 + [jax.ShapeDtypeStruct((1, WIDTH), F32)] * 2,
        scratch_shapes=[pltpu.VMEM((8, WIDTH), F32)] * 2,
        compiler_params=_cparams(("arbitrary",)),
    )(qk, gq, gk, dqn, dkn)


def _mix_fwd(ga, gb, ya, yb, name):
    S, D = ga.shape

    def body(ga_ref, gb_ref, ya_ref, yb_ref, o_ref, ot_ref):
        m = _sigmoid(ga_ref[...]) * ya_ref[...] + _sigmoid(gb_ref[...]) * yb_ref[...]
        o_ref[...] = m.astype(BF16)
        ot_ref[...] = m.T.astype(BF16)

    return pl.pallas_call(
        body, name=name, grid=(S // ROWS,),
        in_specs=[_row_spec(D)] * 4, out_specs=[_row_spec(D), _col_spec(D)],
        out_shape=[jax.ShapeDtypeStruct((S, D), BF16), jax.ShapeDtypeStruct((D, S), BF16)],
        compiler_params=_cparams(("parallel",)),
    )(ga, gb, ya, yb)


def _mix_bwd(dm, ga, gb, ya, yb, name):
    S, D = ga.shape

    def body(dm_ref, ga_ref, gb_ref, ya_ref, yb_ref, dga_ref, dgb_ref, dya_ref, dyb_ref):
        dmv = dm_ref[...]
        for g_ref, y_ref, dg_ref, dy_ref in ((ga_ref, ya_ref, dga_ref, dya_ref), (gb_ref, yb_ref, dgb_ref, dyb_ref)):
            s = _sigmoid(g_ref[...])
            dy_ref[...] = (dmv * s).astype(BF16)
            dg_ref[...] = (dmv * y_ref[...] * s * (1.0 - s)).astype(BF16)

    return pl.pallas_call(
        body, name=name, grid=(S // ROWS,),
        in_specs=[_row_spec(D)] * 5, out_specs=[_row_spec(D)] * 4,
        out_shape=[jax.ShapeDtypeStruct((S, D), BF16)] * 4,
        compiler_params=_cparams(("parallel",)),
    )(dm, ga, gb, ya, yb)


def _loss_head(y, target, name):
    S, D = y.shape
    nt = S // ROWS

    def body(y_ref, t_ref, dy_ref, dyb_ref, p_ref):
        err = y_ref[...] - t_ref[...]
        dy = err * (1.0 / D)
        dy_ref[...] = dy
        dyb_ref[...] = dy.astype(BF16)
        sq = _colsum8(err * err)
        acc = sq[:, 0:LANES]
        for k in range(1, D // LANES):
            acc = acc + sq[:, k * LANES:(k + 1) * LANES]
        p_ref[...] = acc

    return pl.pallas_call(
        body, name=name, grid=(nt,),
        in_specs=[_row_spec(D)] * 2,
        out_specs=[_row_spec(D), _row_spec(D), pl.BlockSpec((8, LANES), lambda i: (i, 0))],
        out_shape=[jax.ShapeDtypeStruct((S, D), F32), jax.ShapeDtypeStruct((S, D), BF16),
                   jax.ShapeDtypeStruct((nt * 8, LANES), F32)],
        compiler_params=_cparams(("parallel",)),
    )(y, target)


CONV_COLS = D_FF // 2
HALO = 16
CONV_CHUNK = 64


def _aligned(start, multiple):
    return start if isinstance(start, int) else pl.multiple_of(start, multiple)


def _conv_taps(xe, cw, cb):
    taps = (pltpu.roll(xe, 2, 0), pltpu.roll(xe, 1, 0), xe)
    return taps, cw[0:1] * taps[0] + cw[1:2] * taps[1] + cw[2:3] * taps[2] + cb


def _conv_specs(nt):
    hb, nb = ROWS // HALO, D_FF // CONV_COLS
    specs = {}
    for part, off in (("gate", 0), ("up", nb)):
        specs[part] = dict(
            main=pl.BlockSpec((ROWS, CONV_COLS), functools.partial(lambda c, i, off: (i, c + off), off=off)),
            prev=pl.BlockSpec((HALO, CONV_COLS),
                              functools.partial(lambda c, i, off: (jnp.maximum(i * hb - 1, 0), c + off), off=off)),
            nxt=pl.BlockSpec((HALO, CONV_COLS),
                             functools.partial(lambda c, i, off: (jnp.minimum((i + 1) * hb, nt * hb - 1), c + off), off=off)),
            w=pl.BlockSpec((3, CONV_COLS), functools.partial(lambda c, i, off: (0, c + off), off=off)),
            b=pl.BlockSpec((1, CONV_COLS), functools.partial(lambda c, i, off: (0, c + off), off=off)))
    return specs


def _convglu_fwd(hid, cw, cb, name):
    S = hid.shape[0]
    sp = _conv_specs(S // ROWS)

    def body(hg_ref, hgp_ref, hu_ref, hup_ref, cwg_ref, cwu_ref, cbg_ref, cbu_ref, o_ref, ot_ref):
        i = pl.program_id(1)
        keep = (i > 0).astype(F32)

        def conv(h_ref, hp_ref, cw_ref, cb_ref):
            xe = jnp.concatenate([hp_ref[...].astype(F32) * keep, h_ref[...].astype(F32)], axis=0)
            return _conv_taps(xe, cw_ref[...], cb_ref[...])[1][HALO:, :]

        gate = conv(hg_ref, hgp_ref, cwg_ref, cbg_ref)
        up = conv(hu_ref, hup_ref, cwu_ref, cbu_ref)
        act = gate * _sigmoid(gate) * up
        o_ref[...] = act.astype(BF16)
        ot_ref[...] = act.T.astype(BF16)

    g, u = sp["gate"], sp["up"]
    return pl.pallas_call(
        body, name=name, grid=(D_FF // CONV_COLS, S // ROWS),
        in_specs=[g["main"], g["prev"], u["main"], u["prev"], g["w"], u["w"], g["b"], u["b"]],
        out_specs=[g["main"], pl.BlockSpec((CONV_COLS, ROWS), lambda c, i: (c, i))],
        out_shape=[jax.ShapeDtypeStruct((S, D_FF), BF16), jax.ShapeDtypeStruct((D_FF, S), BF16)],
        compiler_params=_cparams(("parallel", "parallel")),
    )(hid, hid, hid, hid, cw, cw, cb, cb)


def _convglu_bwd(hid, dact, cw, cb, name):
    S = hid.shape[0]
    nt = S // ROWS
    sp = _conv_specs(nt)

    n_chunks = ROWS // CONV_CHUNK

    def body(hg_ref, hgp_ref, hgn_ref, hu_ref, hup_ref, hun_ref, da_ref, dan_ref,
             cwg_ref, cwu_ref, cbg_ref, cbu_ref,
             dhg_ref, dhu_ref, dcwg_ref, dcwu_ref, dcbg_ref, dcbu_ref, xg_s, xu_s, da_s):
        i = pl.program_id(1)
        kp = (i > 0).astype(F32)
        kn = (i < nt - 1).astype(F32)
        for x_s, h_ref, hp_ref, hn_ref in ((xg_s, hg_ref, hgp_ref, hgn_ref), (xu_s, hu_ref, hup_ref, hun_ref)):
            x_s[0:HALO, :] = hp_ref[...].astype(F32) * kp
            x_s[HALO:HALO + ROWS, :] = h_ref[...].astype(F32)
            x_s[HALO + ROWS:, :] = hn_ref[...].astype(F32) * kn
        da_s[0:ROWS, :] = da_ref[...].astype(F32)
        da_s[ROWS:, :] = dan_ref[...].astype(F32) * kn

        @pl.when(i == 0)
        def _():
            for ref in (dcwg_ref, dcwu_ref, dcbg_ref, dcbu_ref):
                ref[...] = jnp.zeros_like(ref)

        def lane_group(grp, _):
            lanes = pl.ds(pl.multiple_of(grp * LANES, LANES), LANES)
            cwg, cwu, cbg, cbu = cwg_ref[:, lanes], cwu_ref[:, lanes], cbg_ref[:, lanes], cbu_ref[:, lanes]

            def grads(r0, n):
                rows = pl.ds(_aligned(r0 + HALO - 8, 8), n + 8)
                taps_g, gate = _conv_taps(xg_s[rows, lanes], cwg, cbg)
                taps_u, up = _conv_taps(xu_s[rows, lanes], cwu, cbu)
                gate, up = gate[8:], up[8:]
                da = da_s[pl.ds(_aligned(r0, 8), n), lanes]
                sg = _sigmoid(gate)
                return (da * up * sg * (1.0 + gate * (1.0 - sg)), da * gate * sg,
                        [t[8:] for t in taps_g], [t[8:] for t in taps_u])

            def chunk(step, carry):
                below_g, below_u, accs = carry
                r0 = (n_chunks - 1 - step) * CONV_CHUNK
                dg, du, taps_g, taps_u = grads(r0, CONV_CHUNK)
                new_accs = []
                for d, below, cwv, taps, dh_ref, acc in ((dg, below_g, cwg, taps_g, dhg_ref, accs[0]),
                                                        (du, below_u, cwu, taps_u, dhu_ref, accs[1])):
                    ext = jnp.concatenate([d, below], axis=0)
                    n_ext = CONV_CHUNK + 8
                    dh = (cwv[2:3] * d + cwv[1:2] * pltpu.roll(ext, n_ext - 1, 0)[:CONV_CHUNK]
                          + cwv[0:1] * pltpu.roll(ext, n_ext - 2, 0)[:CONV_CHUNK])
                    dh_ref[pl.ds(_aligned(r0, CONV_CHUNK), CONV_CHUNK), lanes] = dh.astype(BF16)
                    new_accs.append(tuple(a + _colsum8(d * tap) for a, tap in zip(acc[:3], taps))
                                    + (acc[3] + _colsum8(d),))
                return dg[0:8], du[0:8], tuple(new_accs)

            below_g, below_u, _, _ = grads(ROWS, 8)
            zero = jnp.zeros((8, LANES), F32)
            _, _, accs = lax.fori_loop(0, n_chunks, chunk, (below_g, below_u, ((zero,) * 4, (zero,) * 4)))
            for acc, dcw_ref, dcb_ref in ((accs[0], dcwg_ref, dcbg_ref), (accs[1], dcwu_ref, dcbu_ref)):
                for t in range(3):
                    dcw_ref[t:t + 1, lanes] += jnp.sum(acc[t], axis=0, keepdims=True)
                dcb_ref[:, lanes] += jnp.sum(acc[3], axis=0, keepdims=True)
            return 0

        lax.fori_loop(0, CONV_COLS // LANES, lane_group, 0)

    g, u = sp["gate"], sp["up"]
    return pl.pallas_call(
        body, name=name, grid=(D_FF // CONV_COLS, nt),
        in_specs=[g["main"], g["prev"], g["nxt"], u["main"], u["prev"], u["nxt"], g["main"], g["nxt"],
                  g["w"], u["w"], g["b"], u["b"]],
        out_specs=[g["main"], g["main"], g["w"], g["w"], g["b"], g["b"]],
        out_shape=[jax.ShapeDtypeStruct((S, D_FF), BF16)] * 2 + [jax.ShapeDtypeStruct((3, D_FF), F32)] * 2
        + [jax.ShapeDtypeStruct((1, D_FF), F32)] * 2,
        scratch_shapes=[pltpu.VMEM((ROWS + 2 * HALO, CONV_COLS), F32)] * 2 + [pltpu.VMEM((ROWS + HALO, CONV_COLS), F32)],
        compiler_params=_cparams(("parallel", "arbitrary")),
    )(hid, hid, hid, hid, hid, hid, dact, dact, cw, cw, cb, cb)


REL_PAD = 384
DIAG = 1024


def _band_valid():
    qc = lax.broadcasted_iota(jnp.int32, (BQ, KWIN), 0) // CHUNK
    kc = lax.broadcasted_iota(jnp.int32, (BQ, KWIN), 1) // CHUNK - LEFT_CHUNKS
    return (kc <= qc) & (kc >= qc - LEFT_CHUNKS)


def _rel_index(offset):
    return jnp.clip(BAND - offset, -MAX_REL, MAX_REL) + MAX_REL


def _split3(x):
    hi = x.astype(BF16)
    rest = x - hi.astype(F32)
    mid = rest.astype(BF16)
    return hi, mid, (rest - mid.astype(F32)).astype(BF16)


def _bias_table(rel_bias, name):
    def body(rb_ref, o_ref):
        t = lax.broadcasted_iota(jnp.int32, (REL_PAD, DIAG), 0)
        lane = lax.broadcasted_iota(jnp.int32, (REL_PAD, DIAG), 1)
        pick = jnp.where(t == _rel_index(lane - BQ), 1.0, 0.0).astype(BF16)
        base = sum(_dot(piece, pick, NN) for piece in _split3(rb_ref[...]))
        valid = _band_valid()
        for h in range(N_HEADS):
            rows = jnp.broadcast_to(base[h:h + 1], (BQ, DIAG))
            rolled = pltpu.roll(rows, 0, 1, stride=1, stride_axis=0)
            o_ref[h] = jnp.where(valid, rolled[:, BQ:], NEG)

    return pl.pallas_call(
        body, name=name,
        out_shape=jax.ShapeDtypeStruct((N_HEADS, BQ, KWIN), F32),
        compiler_params=_cparams(),
    )(rel_bias)


def _bias_table_bwd(dtab, name):
    def body(d_ref, o_ref, diag_ref):
        r = lax.broadcasted_iota(jnp.int32, (BQ, BQ), 0)
        c = lax.broadcasted_iota(jnp.int32, (BQ, BQ), 1)
        flip = jnp.where(r + c == BQ - 1, 1.0, 0.0).astype(BF16)
        for h in range(N_HEADS):
            flipped = sum(_dot(flip, piece, NN) for piece in _split3(d_ref[h]))
            padded = jnp.concatenate([flipped, jnp.zeros((BQ, DIAG - KWIN), F32)], axis=1)
            rolled = pltpu.roll(padded, DIAG - (BQ - 1), 1, stride=1, stride_axis=0)
            diag_ref[h:h + 1, :] = jnp.sum(rolled, axis=0, keepdims=True)
        lane = lax.broadcasted_iota(jnp.int32, (DIAG, REL_PAD), 0)
        t = lax.broadcasted_iota(jnp.int32, (DIAG, REL_PAD), 1)
        offset = jnp.where(lane < KWIN, lane, lane - DIAG)
        pick = jnp.where(t == _rel_index(offset), 1.0, 0.0).astype(BF16)
        o_ref[...] = sum(_dot(piece, pick, NN) for piece in _split3(diag_ref[...]))

    return pl.pallas_call(
        body, name=name,
        out_shape=jax.ShapeDtypeStruct((N_HEADS, REL_PAD), F32),
        scratch_shapes=[pltpu.VMEM((N_HEADS, DIAG), F32)],
        compiler_params=_cparams(),
    )(dtab)


def _head_masks(heads=2):
    lane = lax.broadcasted_iota(jnp.int32, (1, heads * HEAD_DIM), 1)
    return [lane // HEAD_DIM == h for h in range(heads)]


def _own_lanes(masks, vals):
    out = vals[-1]
    for m, val in zip(masks[-2::-1], vals[-2::-1]):
        out = jnp.where(m, val, out)
    return out


CA_HEADS = 4
CA_LANES = CA_HEADS * HEAD_DIM


def _ca_window_specs(nq, col_off=0):
    return [pl.BlockSpec((BQ, CA_LANES), functools.partial(
        lambda p, i, d: (jnp.clip(i - 2 + d, 0, nq - 1), p + col_off), d=d)) for d in range(3)]


def _softmax_rows(s):
    p = jnp.exp(s - jnp.max(s, axis=-1, keepdims=True))
    return p, jnp.sum(p, axis=-1, keepdims=True)


def _ca_scores(qm, kc, tab_h, i):
    col = lax.broadcasted_iota(jnp.int32, (1, KWIN), 1)
    in_seq = col + (i - 2) * BQ >= 0
    return jnp.where(in_seq, _dot(qm, kc, NT) + tab_h, NEG)


def _ca_fwd(qn, kn, v, tab, name, v_off=0):
    S = qn.shape[0]
    nq = S // BQ
    qspec = pl.BlockSpec((BQ, CA_LANES), lambda p, i: (i, p))
    tspec = pl.BlockSpec((CA_HEADS, BQ, KWIN), lambda p, i: (p, 0, 0))

    def body(q_ref, k0, k1, k2, v0, v1, v2, tab_ref, o_ref, ot_ref):
        i = pl.program_id(1)
        kc = jnp.concatenate([k0[...], k1[...], k2[...]], axis=0)
        vc = jnp.concatenate([v0[...], v1[...], v2[...]], axis=0)
        qv = q_ref[...]
        masks = _head_masks(CA_HEADS)
        heads = range(CA_HEADS)
        s = [_ca_scores(jnp.where(masks[h], qv, 0), kc, tab_ref[h], i) for h in heads]
        soft = [_softmax_rows(s[h]) for h in heads]
        o = [_dot(soft[h][0].astype(BF16), vc, NN) / soft[h][1] for h in heads]
        out = _own_lanes(masks, o)
        o_ref[...] = out.astype(BF16)
        ot_ref[...] = out.T.astype(BF16)

    return pl.pallas_call(
        body, name=name, grid=(WIDTH // CA_LANES, nq),
        in_specs=[qspec] + _ca_window_specs(nq) + _ca_window_specs(nq, v_off) + [tspec],
        out_specs=[qspec, pl.BlockSpec((CA_LANES, BQ), lambda p, i: (p, i))],
        out_shape=[jax.ShapeDtypeStruct((S, WIDTH), BF16), jax.ShapeDtypeStruct((WIDTH, S), BF16)],
        compiler_params=_cparams(("parallel", "parallel")),
    )(qn, kn, kn, kn, v, v, v, tab)


def _ca_bwd(qn, kn, v, do, tab, name, v_off=0):
    S = qn.shape[0]
    nq = S // BQ
    qspec = pl.BlockSpec((BQ, CA_LANES), lambda p, i: (jnp.minimum(i, nq - 1), p))
    kout = pl.BlockSpec((BQ, CA_LANES), lambda p, i: (jnp.clip(i - 2, 0, nq - 1), p))
    tspec = pl.BlockSpec((CA_HEADS, BQ, KWIN), lambda p, i: (p, 0, 0))

    def body(q_ref, do_ref, k0, k1, k2, v0, v1, v2, tab_ref,
             dq_ref, dk_ref, dv_ref, dtab_ref, dk_acc, dv_acc):
        i = pl.program_id(1)

        @pl.when(i == 0)
        def _():
            dk_acc[...] = jnp.zeros_like(dk_acc)
            dv_acc[...] = jnp.zeros_like(dv_acc)
            dtab_ref[...] = jnp.zeros_like(dtab_ref)

        @pl.when(i < nq)
        def _():
            kc = jnp.concatenate([k0[...], k1[...], k2[...]], axis=0)
            vc = jnp.concatenate([v0[...], v1[...], v2[...]], axis=0)
            qv, dov = q_ref[...], do_ref[...]
            masks = _head_masks(CA_HEADS)
            heads = range(CA_HEADS)
            qm = [jnp.where(masks[h], qv, 0) for h in heads]
            dom = [jnp.where(masks[h], dov, 0) for h in heads]
            s = [_ca_scores(qm[h], kc, tab_ref[h], i) for h in heads]
            dp = [_dot(dom[h], vc, NT) for h in heads]
            soft = [_softmax_rows(s[h]) for h in heads]
            p = [soft[h][0] / soft[h][1] for h in heads]
            ds = [p[h] * (dp[h] - jnp.sum(p[h] * dp[h], axis=-1, keepdims=True)) for h in heads]
            for h in heads:
                dtab_ref[h] += ds[h]
            dsb = [ds[h].astype(BF16) for h in heads]
            pb = [p[h].astype(BF16) for h in heads]
            dq = [_dot(dsb[h], kc, NN) for h in heads]
            dq_ref[...] = _own_lanes(masks, dq)
            dkc = sum(_dot(dsb[h], qm[h], TN) for h in heads)
            dvc = sum(_dot(pb[h], dom[h], TN) for h in heads)
            for d in range(3):
                slot = (i + 1 + d) % 3
                dk_acc[slot] += dkc[d * BQ:(d + 1) * BQ]
                dv_acc[slot] += dvc[d * BQ:(d + 1) * BQ]

        @pl.when(i >= 2)
        def _():
            slot = (i + 1) % 3
            dk_ref[...] = dk_acc[slot]
            dv_ref[...] = dv_acc[slot].astype(BF16)
            dk_acc[slot] = jnp.zeros((BQ, CA_LANES), F32)
            dv_acc[slot] = jnp.zeros((BQ, CA_LANES), F32)

    return pl.pallas_call(
        body, name=name, grid=(WIDTH // CA_LANES, nq + 2),
        in_specs=[qspec, qspec] + _ca_window_specs(nq) + _ca_window_specs(nq, v_off) + [tspec],
        out_specs=[qspec, kout, kout, tspec],
        out_shape=[jax.ShapeDtypeStruct((S, WIDTH), F32), jax.ShapeDtypeStruct((S, WIDTH), F32),
                   jax.ShapeDtypeStruct((S, WIDTH), BF16), jax.ShapeDtypeStruct((N_HEADS, BQ, KWIN), F32)],
        scratch_shapes=[pltpu.VMEM((3, BQ, CA_LANES), F32)] * 2,
        compiler_params=_cparams(("parallel", "arbitrary")),
    )(qn, do, kn, kn, kn, v, v, v, tab)


def _sb_consts():
    r = lax.broadcasted_iota(jnp.int32, (BQ, BQ), 0)
    c = lax.broadcasted_iota(jnp.int32, (BQ, BQ), 1)
    from_s = jnp.where(r >= c, 1.0, 0.0).astype(BF16)
    causal = c < r
    return from_s, causal


def _suffix_sum(t, from_s):
    hi, lo = _split_bf16(t)
    return _dot(hi, from_s, NN) + _dot(lo, from_s, NN)


def _neg_abs(x):
    bits = lax.bitcast_convert_type(x, jnp.uint32) | jnp.uint32(0x80000000)
    return lax.bitcast_convert_type(bits, F32)


def _sb_log_keep(zn):
    return jnp.minimum(zn, 0.0) - jnp.log(1.0 + jnp.exp(_neg_abs(zn)))


SB_DEAD = 105.0


SB_QB = 2


def _sb_walk(ip, tiles, keep_ref):
    i0 = SB_QB * ip

    @pl.when(ip == 0)
    def _():
        tiles([(0, [0], [True]), (1, [1, 0], [True, False])])

    @pl.when(ip > 0)
    def _():
        tiles([(a, [i0 + a, i0 + a - 1], [True, False]) for a in range(SB_QB)])

    for a in range(SB_QB):
        def alive(a=a):
            return (jnp.max(keep_ref[2 * a:2 * a + 2]) > -SB_DEAD).astype(jnp.int32)

        def step(state, a=a, alive=alive):
            j, _ = state
            tiles([(a, [j], [False])])
            return j - 1, alive()

        lax.while_loop(lambda state: (state[0] >= 0) & (state[1] > 0), step, (i0 + a - 2, alive()))


def _sb_rows(j):
    return pl.ds(pl.multiple_of(j * BQ, BQ), BQ)


def _sb_chains(groups):
    chains = [(a, n, h) for a, js, _ in groups for n in range(len(js)) for h in range(2)]
    block = {(a, n): j for a, js, _ in groups for n, j in enumerate(js)}
    masked = [(a, n, h) for a, _, diags in groups for n, d in enumerate(diags) if d for h in range(2)]
    return chains, block, masked


def _sb_running(ref, vals, groups):
    before_chain = {}
    for a, js, _ in groups:
        for h in range(2):
            run = ref[2 * a + h]
            for n in range(len(js)):
                before_chain[(a, n, h)] = run
                run = run + jnp.sum(vals[(a, n, h)], axis=-1, keepdims=True)
            ref[2 * a + h] = run
    return before_chain


def _sb_specs(S, offs):
    def qspec(off=0):
        return pl.BlockSpec((SB_QB * BQ, PAIR), lambda p, i: (i, p + off))

    def kspec(off=0):
        return pl.BlockSpec((S, PAIR), lambda p, i: (0, p + off), pipeline_mode=pl.Buffered(1))

    return qspec, kspec, [qspec(offs[0]), kspec(offs[1]), kspec(offs[2])]


def _sb_fwd(q, k, v, name, offs=(0, 0, 0)):
    S = q.shape[0]
    steps = S // (SB_QB * BQ)
    qspec, _, qkv_specs = _sb_specs(S, offs)

    def body(q_ref, k_ref, v_ref, o_ref, of_ref, ot_ref, carry_ref, acc_ref):
        ip = pl.program_id(1)
        from_s, causal = _sb_consts()
        masks = _head_masks()
        qn = q_ref[...] * -(HEAD_DIM ** -0.5)
        qms = {(a, h): jnp.where(masks[h], qn[a * BQ:(a + 1) * BQ], 0) for a in range(SB_QB) for h in range(2)}
        carry_ref[...] = jnp.zeros_like(carry_ref)
        acc_ref[...] = jnp.zeros_like(acc_ref)

        def tiles(groups):
            chains, block, masked = _sb_chains(groups)
            kbs = {an: k_ref[_sb_rows(j), :] for an, j in block.items()}
            vbs = {an: v_ref[_sb_rows(j), :] for an, j in block.items()}
            zn = {c: _dot(qms[(c[0], c[2])], kbs[c[:2]], NT) for c in chains}
            log_keep = {c: _sb_log_keep(zn[c]) for c in chains}
            for c in masked:
                log_keep[c] = jnp.where(causal, log_keep[c], 0.0)
            split = {c: _split_bf16(log_keep[c]) for c in chains}
            carry = _sb_running(carry_ref, log_keep, groups)
            suffix = {c: _dot(split[c][0], from_s, NN) + _dot(split[c][1], from_s, NN) for c in chains}
            w = {c: jnp.exp(carry[c] + suffix[c] - zn[c]) for c in chains}
            for c in masked:
                w[c] = jnp.where(causal, w[c], 0.0)
            for c in chains:
                acc_ref[2 * c[0] + c[2]] += _dot(w[c].astype(BF16), vbs[c[:2]], NN)

        _sb_walk(ip, tiles, carry_ref)
        for a in range(SB_QB):
            out = jnp.where(masks[0], acc_ref[2 * a], acc_ref[2 * a + 1])
            o_ref[a * BQ:(a + 1) * BQ, :] = out.astype(BF16)
            of_ref[a * BQ:(a + 1) * BQ, :] = out
            ot_ref[:, a * BQ:(a + 1) * BQ] = out.T.astype(BF16)

    return pl.pallas_call(
        body, name=name, grid=(WIDTH // PAIR, steps),
        in_specs=qkv_specs, out_specs=[qspec(), qspec(), pl.BlockSpec((PAIR, SB_QB * BQ), lambda p, i: (p, i))],
        out_shape=[jax.ShapeDtypeStruct((S, WIDTH), BF16), jax.ShapeDtypeStruct((S, WIDTH), F32),
                   jax.ShapeDtypeStruct((WIDTH, S), BF16)],
        scratch_shapes=[pltpu.VMEM((2 * SB_QB, BQ, 1), F32), pltpu.VMEM((2 * SB_QB, BQ, PAIR), F32)],
        compiler_params=_cparams(("parallel", "arbitrary")),
    )(q, k, v)


def _sb_bwd(q, k, v, o, do, name, offs=(0, 0, 0)):
    S = q.shape[0]
    steps = S // (SB_QB * BQ)
    qspec, kspec, qkv_specs = _sb_specs(S, offs)

    def body(q_ref, o_ref, do_ref, k_ref, v_ref, dq_ref, dk_ref, dv_ref, dk_acc, dv_acc, keep_ref, gsum_ref, dq_acc):
        ip = pl.program_id(1)

        @pl.when(ip == 0)
        def _():
            dk_acc[...] = jnp.zeros_like(dk_acc)
            dv_acc[...] = jnp.zeros_like(dv_acc)

        from_s, causal = _sb_consts()
        masks = _head_masks()
        qn, dov = q_ref[...] * -(HEAD_DIM ** -0.5), do_ref[...]
        od = o_ref[...] * dov.astype(F32)
        lanes = [(a, h) for a in range(SB_QB) for h in range(2)]
        rows_of = {a: slice(a * BQ, (a + 1) * BQ) for a in range(SB_QB)}
        qms = {(a, h): jnp.where(masks[h], qn[rows_of[a]], 0) for a, h in lanes}
        doms = {(a, h): jnp.where(masks[h], dov[rows_of[a]], 0) for a, h in lanes}
        totals = {(a, h): jnp.sum(jnp.where(masks[h], od[rows_of[a]], 0.0), axis=-1, keepdims=True)
                  for a, h in lanes}
        for ref in (keep_ref, gsum_ref, dq_acc):
            ref[...] = jnp.zeros_like(ref)

        def tiles(groups):
            chains, block, masked = _sb_chains(groups)
            kbs = {an: k_ref[_sb_rows(j), :] for an, j in block.items()}
            vbs = {an: v_ref[_sb_rows(j), :] for an, j in block.items()}
            zn = {c: _dot(qms[(c[0], c[2])], kbs[c[:2]], NT) for c in chains}
            dw = {c: _dot(doms[(c[0], c[2])], vbs[c[:2]], NT) for c in chains}
            log_keep = {c: _sb_log_keep(zn[c]) for c in chains}
            for c in masked:
                log_keep[c] = jnp.where(causal, log_keep[c], 0.0)
            split = {c: _split_bf16(log_keep[c]) for c in chains}
            kept = _sb_running(keep_ref, log_keep, groups)
            suffix = {c: _dot(split[c][0], from_s, NN) + _dot(split[c][1], from_s, NN) for c in chains}
            w = {c: jnp.exp(kept[c] + suffix[c] - zn[c]) for c in chains}
            for c in masked:
                w[c] = jnp.where(causal, w[c], 0.0)
            wb = {c: w[c].astype(BF16) for c in chains}
            g = {c: wb[c].astype(F32) * dw[c] for c in chains}
            gsplit = {c: _split_bf16(g[c]) for c in chains}
            gsum = _sb_running(gsum_ref, g, groups)
            gsuffix = {c: _dot(gsplit[c][0], from_s, NN) + _dot(gsplit[c][1], from_s, NN) for c in chains}
            dzb = {}
            for c in chains:
                before = totals[(c[0], c[2])] - (gsum[c] + gsuffix[c])
                dz = (g[c] + before) * jnp.exp(log_keep[c]) - before
                if c in masked:
                    dz = jnp.where(causal, dz, 0.0)
                dzb[c] = dz.astype(BF16)
            for c in chains:
                rows = _sb_rows(block[c[:2]])
                dq_acc[2 * c[0] + c[2]] += _dot(dzb[c], kbs[c[:2]], NN)
                dk_acc[rows, :] -= _dot(dzb[c], qms[(c[0], c[2])], TN)
                dv_acc[rows, :] += _dot(wb[c], doms[(c[0], c[2])], TN)

        _sb_walk(ip, tiles, keep_ref)
        for a in range(SB_QB):
            dq = jnp.where(masks[0], dq_acc[2 * a], dq_acc[2 * a + 1])
            dq_ref[a * BQ:(a + 1) * BQ, :] = (dq * HEAD_DIM ** -0.5).astype(BF16)

        @pl.when(ip == steps - 1)
        def _():
            dk_ref[...] = dk_acc[...].astype(BF16)
            dv_ref[...] = dv_acc[...].astype(BF16)

    return pl.pallas_call(
        body, name=name, grid=(WIDTH // PAIR, steps),
        in_specs=[qkv_specs[0], qspec(), qspec(), qkv_specs[1], qkv_specs[2]], out_specs=[qspec(), kspec(), kspec()],
        out_shape=[jax.ShapeDtypeStruct((S, WIDTH), BF16)] * 3,
        scratch_shapes=[pltpu.VMEM((S, PAIR), F32)] * 2 + [pltpu.VMEM((2 * SB_QB, BQ, 1), F32)] * 2
        + [pltpu.VMEM((2 * SB_QB, BQ, PAIR), F32)],
        compiler_params=_cparams(("parallel", "arbitrary")),
    )(q, o, do, k, v)


ANY = pl.BlockSpec(memory_space=pl.ANY)


def _place():
    return lax.axis_index("x"), lax.axis_index("y"), lax.axis_index("c")


def _other_chips(x, y):
    return [(2 * px + py, (px, py)) for px, py in ((1 - x, y), (x, 1 - y), (1 - x, 1 - y))]


def _remote(src, dst, sems, k, to):
    return pltpu.make_async_remote_copy(src_ref=src, dst_ref=dst, send_sem=sems[0].at[k], recv_sem=sems[1].at[k],
                                        device_id=to, device_id_type=MESH)


def _gather_weights(ws, extra, name):
    n = len(ws)

    def body(*refs):
        w_refs, e_ref, out_refs, eo_ref = refs[:n], refs[n], refs[n + 1:2 * n + 1], refs[2 * n + 1]
        sems = refs[2 * n + 2:]
        x, y, c = _place()
        me = 2 * x + y
        chips = _other_chips(x, y)

        def halves(ref):
            rh = ref.shape[-2] // 2
            return pl.ds(c * rh, rh), pl.ds((1 - c) * rh, rh)

        first = [_remote(w_ref.at[halves(w_ref)[0]], o_ref.at[me, halves(w_ref)[0]], sems, 6 * a + k, (*xy, c))
                 for a, (w_ref, o_ref) in enumerate(zip(w_refs, out_refs)) for k, (_, xy) in enumerate(chips)]
        first += [_remote(e_ref, eo_ref.at[me], sems, 6 * n + k, (*xy, c)) for k, (_, xy) in enumerate(chips)]
        for cp in first:
            cp.start()
        passed = []
        for a, o_ref in enumerate(out_refs):
            for k, (chip, xy) in enumerate(chips):
                landed = o_ref.at[chip, halves(o_ref)[0]]
                _remote(landed, landed, sems, 6 * a + k, (*xy, c)).wait_recv()
                cp = _remote(landed, landed, sems, 6 * a + 3 + k, (x, y, 1 - c))
                cp.start()
                passed.append(cp)
        for a, o_ref in enumerate(out_refs):
            for k, (chip, xy) in enumerate(chips):
                landed = o_ref.at[chip, halves(o_ref)[1]]
                _remote(landed, landed, sems, 6 * a + 3 + k, (x, y, 1 - c)).wait_recv()
        for k, (chip, xy) in enumerate(chips):
            _remote(e_ref, eo_ref.at[chip], sems, 6 * n + k, (*xy, c)).wait_recv()
        for cp in first + passed:
            cp.wait_send()

    n_copies = 6 * n + 3
    return pl.pallas_call(
        body, name=name, in_specs=[ANY] * (n + 1), out_specs=[ANY] * (n + 1),
        out_shape=[jax.ShapeDtypeStruct((N_CHIPS,) + w.shape, w.dtype) for w in ws]
        + [jax.ShapeDtypeStruct((N_CHIPS,) + extra.shape, extra.dtype)],
        scratch_shapes=[pltpu.SemaphoreType.DMA((n_copies,)), pltpu.SemaphoreType.DMA((n_copies,))],
    )(*ws, extra)


def _exchange_cores(gs, small, name):
    n = len(gs)

    def body(*refs):
        g_refs, small_ref, sib_refs, all_ref = refs[:n], refs[n], refs[n + 1:2 * n + 1], refs[2 * n + 1]
        sems = refs[2 * n + 2:]
        x, y, c = _place()
        me = 4 * x + 2 * y + c
        copies = []
        for a, (g_ref, sib_ref) in enumerate(zip(g_refs, sib_refs)):
            rh = g_ref.shape[1] // 2
            copies.append(_remote(g_ref.at[:, pl.ds((1 - c) * rh, rh), :], sib_ref, sems, a, (x, y, 1 - c)))
        k = n
        for fx in (0, 1):
            for fy in (0, 1):
                for fc in (0, 1):
                    if fx or fy or fc:
                        to = (1 - x if fx else x, 1 - y if fy else y, 1 - c if fc else c)
                        copies.append(_remote(small_ref, all_ref.at[me], sems, k, to))
                        k += 1
        for cp in copies:
            cp.start()
        for cp in copies:
            cp.wait_recv()
        for cp in copies:
            cp.wait_send()

    n_copies = n + N_DEV - 1
    return pl.pallas_call(
        body, name=name, in_specs=[ANY] * (n + 1), out_specs=[ANY] * (n + 1),
        out_shape=[jax.ShapeDtypeStruct((N_CHIPS, g.shape[1] // 2, g.shape[2]), F32) for g in gs]
        + [jax.ShapeDtypeStruct((N_DEV,) + small.shape, F32)],
        scratch_shapes=[pltpu.SemaphoreType.DMA((n_copies,)), pltpu.SemaphoreType.DMA((n_copies,))],
    )(*gs, small)


def _exchange_chips(ps, name):
    n = len(ps)

    def body(*refs):
        p_refs, out_refs, sems = refs[:n], refs[n:2 * n], refs[2 * n:]
        x, y, c = _place()
        me = 2 * x + y
        chips = _other_chips(x, y)
        copies = [_remote(p_ref.at[chip], o_ref.at[me], sems, 3 * a + k, (*xy, c))
                  for a, (p_ref, o_ref) in enumerate(zip(p_refs, out_refs)) for k, (chip, xy) in enumerate(chips)]
        for cp in copies:
            cp.start()
        for a, (p_ref, o_ref) in enumerate(zip(p_refs, out_refs)):
            for k, (chip, xy) in enumerate(chips):
                _remote(p_ref.at[chip], o_ref.at[chip], sems, 3 * a + k, (*xy, c)).wait_recv()
        for cp in copies:
            cp.wait_send()

    return pl.pallas_call(
        body, name=name, in_specs=[ANY] * n, out_specs=[ANY] * n,
        out_shape=[jax.ShapeDtypeStruct(p.shape, p.dtype) for p in ps],
        scratch_shapes=[pltpu.SemaphoreType.DMA((3 * n,)), pltpu.SemaphoreType.DMA((3 * n,))],
    )(*ps)


def _share_halves(ghs, name):
    n = len(ghs)

    def body(*refs):
        gh_refs, out_refs, sems = refs[:n], refs[n:2 * n], refs[2 * n:]
        x, y, c = _place()
        copies = [_remote(gh_ref, o_ref, sems, a, (x, y, 1 - c)) for a, (gh_ref, o_ref) in enumerate(zip(gh_refs, out_refs))]
        for cp in copies:
            cp.start()
        for cp in copies:
            cp.wait_recv()
        for cp in copies:
            cp.wait_send()

    return pl.pallas_call(
        body, name=name, in_specs=[ANY] * n, out_specs=[ANY] * n,
        out_shape=[jax.ShapeDtypeStruct(g.shape, g.dtype) for g in ghs],
        scratch_shapes=[pltpu.SemaphoreType.DMA((n,)), pltpu.SemaphoreType.DMA((n,))],
    )(*ghs)


EW_BLOCK_BYTES = 2 * 1024 * 1024


def _row_block(rows, cols, mult=8):
    fits = [b for b in range(mult, rows + 1, mult) if rows % b == 0 and b * cols * 4 <= EW_BLOCK_BYTES]
    return max(fits) if fits else mult


def _add2(a, b, name):
    R, C = a.shape
    rows = _row_block(R, C, mult=16)
    spec = pl.BlockSpec((rows, C), lambda i: (i, 0))

    def body(a_ref, b_ref, o_ref):
        o_ref[...] = (a_ref[...] + b_ref[...]).astype(BF16)

    return pl.pallas_call(
        body, name=name, grid=(R // rows,), in_specs=[spec, spec], out_specs=spec,
        out_shape=jax.ShapeDtypeStruct(a.shape, BF16),
        compiler_params=_cparams(("parallel",)),
    )(a, b)


def _sum_leading(a, name):
    n, R, C = a.shape
    rows = _row_block(R, n * C, mult=16 if a.dtype == BF16 else 8)

    def body(a_ref, o_ref):
        acc = a_ref[0].astype(F32)
        for j in range(1, n):
            acc = acc + a_ref[j].astype(F32)
        o_ref[...] = acc

    return pl.pallas_call(
        body, name=name, grid=(R // rows,),
        in_specs=[pl.BlockSpec((n, rows, C), lambda i: (0, i, 0))],
        out_specs=pl.BlockSpec((rows, C), lambda i: (i, 0)),
        out_shape=jax.ShapeDtypeStruct((R, C), F32),
        compiler_params=_cparams(("parallel",)),
    )(a)


def _adamw(w, g, m, v, name):
    R, C = w.shape
    rows = _row_block(R, C)
    spec = pl.BlockSpec((rows, C), lambda i: (i, 0))

    def body(w_ref, g_ref, m_ref, v_ref, d_ref, mo_ref, vo_ref):
        gv = g_ref[...]
        mn = ADAM_B1 * m_ref[...] + (1.0 - ADAM_B1) * gv
        vn = ADAM_B2 * v_ref[...] + (1.0 - ADAM_B2) * (gv * gv)
        m_hat = mn / (1.0 - ADAM_B1 ** ADAM_STEP)
        v_hat = vn / (1.0 - ADAM_B2 ** ADAM_STEP)
        d_ref[...] = -ADAM_LR * (m_hat / (jnp.sqrt(v_hat) + ADAM_EPS) + ADAM_WD * w_ref[...])
        mo_ref[...] = mn
        vo_ref[...] = vn

    return pl.pallas_call(
        body, name=name, grid=(R // rows,), in_specs=[spec] * 4, out_specs=[spec] * 3,
        out_shape=[jax.ShapeDtypeStruct((R, C), F32)] * 3,
        compiler_params=_cparams(("parallel",)),
    )(w, g, m, v)


BIG = ("w_in", "w_branch_a", "w_branch_b", "w_out", "w_ffn_up", "w_ffn_down")
COL_SHARDED = {"w_in": True, "w_branch_a": True, "w_branch_b": True, "w_out": False, "w_ffn_up": True,
               "w_ffn_down": False}
CONV_W_COLS = 2 * D_FF // N_CHIPS
SMALL_REPLICATED = (("norm1_g", D_MODEL), ("q_norm_g", HEAD_DIM), ("k_norm_g", HEAD_DIM),
                    ("rel_bias", N_HEADS * N_REL), ("norm2_g", D_MODEL), ("ffn_conv_b", 2 * D_FF))
SMALL_GRADS = SMALL_REPLICATED + (("ffn_conv_w", 3 * 2 * D_FF),)
SMALL_OWN = SMALL_REPLICATED + (("ffn_conv_w", 3 * CONV_W_COLS),)
SMALL_GRAD_ROWS = 32
SMALL_OWN_ROWS = 16


def _whole(name, stacked):
    return jnp.concatenate(list(stacked), axis=1) if COL_SHARDED[name] else stacked.reshape(-1, stacked.shape[2])


def _pack_small(vals, sizes, rows):
    flat = jnp.concatenate([vals[n].reshape(-1) for n, _ in sizes])
    return jnp.pad(flat, (0, rows * PACK_COLS - flat.shape[0])).reshape(rows, PACK_COLS)


def _unpack_small(packed, sizes):
    flat, out, o = packed.reshape(-1), {}, 0
    for n, sz in sizes:
        out[n] = flat[o:o + sz]
        o += sz
    return out


def kernel(x, norm1_g, w_in, q_norm_g, k_norm_g, rel_bias, w_branch_a, w_branch_b, w_out, norm2_g, w_ffn_up, ffn_conv_w, ffn_conv_b, w_ffn_down, loss_target, m_norm1_g, m_w_in, m_q_norm_g, m_k_norm_g, m_rel_bias, m_w_branch_a, m_w_branch_b, m_w_out, m_norm2_g, m_w_ffn_up, m_ffn_conv_w, m_ffn_conv_b, m_w_ffn_down, v_norm1_g, v_w_in, v_q_norm_g, v_k_norm_g, v_rel_bias, v_w_branch_a, v_w_branch_b, v_w_out, v_norm2_g, v_w_ffn_up, v_ffn_conv_w, v_ffn_conv_b, v_w_ffn_down):
    w_big = {"w_in": w_in[0], "w_branch_a": w_branch_a[0], "w_branch_b": w_branch_b[0], "w_out": w_out[0],
             "w_ffn_up": w_ffn_up[0], "w_ffn_down": w_ffn_down[0]}
    m_big = {"w_in": m_w_in[0], "w_branch_a": m_w_branch_a[0], "w_branch_b": m_w_branch_b[0], "w_out": m_w_out[0],
             "w_ffn_up": m_w_ffn_up[0], "w_ffn_down": m_w_ffn_down[0]}
    v_big = {"w_in": v_w_in[0], "w_branch_a": v_w_branch_a[0], "w_branch_b": v_w_branch_b[0], "w_out": v_w_out[0],
             "w_ffn_up": v_w_ffn_up[0], "w_ffn_down": v_w_ffn_down[0]}
    xs, tgt = x[0], loss_target[0]

    xi, yi, ci = _place()
    chip = 2 * xi + yi

    def with_own(stacked, own):
        return lax.dynamic_update_slice(stacked, own[None], (chip,) + (0,) * own.ndim)

    shards_bf = [w_big[n].astype(BF16) for n in BIG]
    conv_own = jnp.pad(ffn_conv_w[0], ((0, 8 - ffn_conv_w.shape[1]), (0, 0)))
    *gathered, conv_all = _gather_weights(shards_bf, conv_own, "gather_weights")
    full = {n: _whole(n, with_own(g, s)) for n, g, s in zip(BIG, gathered, shards_bf)}
    conv_w = jnp.concatenate(list(with_own(conv_all, conv_own)[:, :3]), axis=1)
    w_in_f, w_a, w_b, w_o = full["w_in"], full["w_branch_a"], full["w_branch_b"], full["w_out"]
    w_up, w_dn = full["w_ffn_up"], full["w_ffn_down"]
    w_in_t, w_a_t, w_b_t, w_o_t, w_up_t, w_dn_t = (w.T for w in (w_in_f, w_a, w_b, w_o, w_up, w_dn))

    hn, hn_t = _rms_fwd(xs, norm1_g, "rms1")
    qk = _matmul(hn, w_in_f[:, :2 * WIDTH], F32, "proj_qk")
    vqkv = _matmul(hn, w_in_f[:, 2 * WIDTH:6 * WIDTH], BF16, "proj_vqkv")
    g_a = _matmul(hn, w_in_f[:, 6 * WIDTH:6 * WIDTH + D_MODEL], F32, "proj_gate_a")
    g_b = _matmul(hn, w_in_f[:, 6 * WIDTH + D_MODEL:], F32, "proj_gate_b")
    gq = jnp.tile(q_norm_g, (1, N_HEADS))
    gk = jnp.tile(k_norm_g, (1, N_HEADS))
    qa, ka = _qknorm_fwd(qk, gq, gk, "qknorm")
    per = WIDTH // PAIR
    b_offs = (per, 2 * per, 3 * per)
    tab = _bias_table(jnp.pad(rel_bias[0], ((0, 0), (0, REL_PAD - N_REL))), "bias_table")
    out_a, out_a_t = _ca_fwd(qa, ka, vqkv, tab, "chunk_attn")
    out_b, out_b_f32, out_b_t = _sb_fwd(vqkv, vqkv, vqkv, "stick_attn", b_offs)
    y_a = _matmul(out_a, w_a, F32, "branch_a")
    y_b = _matmul(out_b, w_b, F32, "branch_b")
    mixed, mixed_t = _mix_fwd(g_a, g_b, y_a, y_b, "mix")
    x2 = _matmul(mixed, w_o, F32, "out_proj", residual=xs)
    hn2, hn2_t = _rms_fwd(x2, norm2_g, "rms2")
    hid = _matmul(hn2, w_up, BF16, "ffn_up")
    act, act_t = _convglu_fwd(hid, conv_w, ffn_conv_b, "convglu")
    y = _matmul(act, w_dn, F32, "ffn_down", residual=x2)
    dy, dyb, sq = _loss_head(y, tgt, "loss_head")
    loss = lax.psum(0.5 / D_MODEL * jnp.sum(sq), ("x", "y", "c"))

    dact = _matmul(dyb, w_dn_t, BF16, "d_act")
    d_w_dn = _matmul(act_t, dyb, F32, "d_w_down")
    dhg, dhu, dcwg, dcwu, dcbg, dcbu = _convglu_bwd(hid, dact, conv_w, ffn_conv_b, "convglu_bwd")
    half_chips = N_CHIPS // 2
    d_w_up = jnp.concatenate([_matmul(hn2_t, dhg, F32, "d_w_up_gate", slabs=half_chips),
                              _matmul(hn2_t, dhu, F32, "d_w_up_up", slabs=half_chips)], axis=0)
    dhn2 = _matmul(dhg, w_up_t[:D_FF], F32, "d_hn2_gate")
    dhn2 = _matmul(dhu, w_up_t[D_FF:], F32, "d_hn2_up", residual=dhn2)
    dx2, dx2b, d_norm2 = _rms_bwd(x2, norm2_g, dhn2, dy, "rms2_bwd")
    dmixed = _matmul(dx2b, w_o_t, F32, "d_mixed")
    d_w_o = _matmul(mixed_t, dx2b, F32, "d_w_out")
    dga, dgb, dya, dyb_b = _mix_bwd(dmixed, g_a, g_b, y_a, y_b, "mix_bwd")
    d_w_a = _matmul(out_a_t, dya, F32, "d_w_branch_a", slabs=N_CHIPS)
    d_w_b = _matmul(out_b_t, dyb_b, F32, "d_w_branch_b", slabs=N_CHIPS)
    do_a = _matmul(dya, w_a_t, BF16, "d_out_a")
    do_b = _matmul(dyb_b, w_b_t, BF16, "d_out_b")
    dqb, dkb, dvb = _sb_bwd(vqkv, vqkv, vqkv, out_b_f32, do_b, "stick_attn_bwd", b_offs)
    dqa_n, dka_n, dva, dtab = _ca_bwd(qa, ka, vqkv, do_a, tab, "chunk_attn_bwd")
    d_rel = _bias_table_bwd(dtab, "bias_table_bwd")[:, :N_REL]
    dqa, dka, dgq, dgk = _qknorm_bwd(qk, gq, gk, dqa_n, dka_n, "qknorm_bwd")
    dproj = jnp.concatenate([dqa, dka, dva, dqb, dkb, dvb, dga, dgb], axis=1)
    d_w_in = _matmul(hn_t, dproj, F32, "d_w_in", slabs=N_CHIPS)
    dhn = _matmul(dproj, w_in_t, F32, "d_hn")
    dx, _, d_norm1 = _rms_bwd(xs, norm1_g, dhn, dx2, "rms1_bwd")

    grads_full = {"w_in": d_w_in, "w_branch_a": d_w_a, "w_branch_b": d_w_b, "w_ffn_up": d_w_up,
                  "w_out": d_w_o.reshape(N_CHIPS, -1, D_MODEL), "w_ffn_down": d_w_dn.reshape(N_CHIPS, -1, D_MODEL)}
    gs = [grads_full[n] for n in BIG]
    small_g = _pack_small({"norm1_g": d_norm1, "q_norm_g": dgq.reshape(N_HEADS, HEAD_DIM).sum(0),
                           "k_norm_g": dgk.reshape(N_HEADS, HEAD_DIM).sum(0), "rel_bias": d_rel,
                           "norm2_g": d_norm2, "ffn_conv_b": jnp.concatenate([dcbg, dcbu], axis=1),
                           "ffn_conv_w": jnp.concatenate([dcwg, dcwu], axis=1)}, SMALL_GRADS, SMALL_GRAD_ROWS)
    *sibs, small_all = _exchange_cores(gs, small_g, "exchange_cores")
    chip_parts = []
    for n, g, sib in zip(BIG, gs, sibs):
        rh, cols = sib.shape[1], sib.shape[2]
        mine = lax.dynamic_slice_in_dim(g, ci * rh, rh, axis=1)
        chip_parts.append(_add2(mine.reshape(-1, cols), sib.reshape(-1, cols), "sum_cores_" + n).reshape(sib.shape))
    parts = _exchange_chips(chip_parts, "exchange_chips")
    g_halves = [_sum_leading(with_own(q, lax.dynamic_index_in_dim(p, chip, 0, keepdims=False)), "sum_chips_" + n)
                for n, p, q in zip(BIG, chip_parts, parts)]
    g_others = _share_halves(g_halves, "share_halves")
    grads = {n: jnp.concatenate([jnp.where(ci == 0, mine, other), jnp.where(ci == 0, other, mine)], axis=0)
             for n, mine, other in zip(BIG, g_halves, g_others)}
    small_all = lax.dynamic_update_slice(small_all, small_g[None], (4 * xi + 2 * yi + ci, 0, 0))
    small_sum = _unpack_small(_sum_leading(small_all, "sum_small"), SMALL_GRADS)
    small_sum["ffn_conv_w"] = lax.dynamic_slice_in_dim(small_sum["ffn_conv_w"].reshape(3, 2 * D_FF),
                                                       chip * CONV_W_COLS, CONV_W_COLS, axis=1)

    deltas, new_m, new_v = {}, {}, {}
    for n in BIG:
        deltas[n], new_m[n], new_v[n] = _adamw(w_big[n], grads[n], m_big[n], v_big[n], "adamw_" + n)

    shapes = {"norm1_g": norm1_g.shape, "q_norm_g": q_norm_g.shape, "k_norm_g": k_norm_g.shape,
              "rel_bias": rel_bias.shape, "norm2_g": norm2_g.shape, "ffn_conv_b": ffn_conv_b.shape,
              "ffn_conv_w": ffn_conv_w.shape}
    small_w = {"norm1_g": norm1_g, "q_norm_g": q_norm_g, "k_norm_g": k_norm_g, "rel_bias": rel_bias,
               "norm2_g": norm2_g, "ffn_conv_b": ffn_conv_b, "ffn_conv_w": ffn_conv_w}
    small_m = {"norm1_g": m_norm1_g, "q_norm_g": m_q_norm_g, "k_norm_g": m_k_norm_g, "rel_bias": m_rel_bias,
               "norm2_g": m_norm2_g, "ffn_conv_b": m_ffn_conv_b, "ffn_conv_w": m_ffn_conv_w}
    small_v = {"norm1_g": v_norm1_g, "q_norm_g": v_q_norm_g, "k_norm_g": v_k_norm_g, "rel_bias": v_rel_bias,
               "norm2_g": v_norm2_g, "ffn_conv_b": v_ffn_conv_b, "ffn_conv_w": v_ffn_conv_w}
    ds, ms, vs = _adamw(*(_pack_small(t, SMALL_OWN, SMALL_OWN_ROWS) for t in (small_w, small_sum, small_m, small_v)),
                        "adamw_small")
    small_grads = small_sum
    ds, ms, vs = (_unpack_small(t, SMALL_OWN) for t in (ds, ms, vs))

    order = ("norm1_g", "w_in", "q_norm_g", "k_norm_g", "rel_bias", "w_branch_a", "w_branch_b", "w_out",
             "norm2_g", "w_ffn_up", "ffn_conv_w", "ffn_conv_b", "w_ffn_down")
    outs = [loss, dx[None]]
    for big, small in ((grads, small_grads), (deltas, ds), (new_m, ms), (new_v, vs)):
        for n in order:
            outs.append(big[n][None] if n in big else small[n].reshape(shapes[n]))
    return tuple(outs)
```

```python
import functools

import jax
import jax.numpy as jnp
from jax import lax
from jax.experimental import pallas as pl
from jax.experimental.pallas import tpu as pltpu

F32 = jnp.float32
BF16 = jnp.bfloat16
MESH = pl.DeviceIdType.MESH

D_MODEL = 1024
HEAD_DIM = 64
N_HEADS = 8
WIDTH = N_HEADS * HEAD_DIM
CHUNK = 64
LEFT_CHUNKS = 8
MAX_REL = 128
N_REL = 2 * MAX_REL + 1
D_FF = 2816
EPS = 1e-6
NEG = -1e30

ADAM_LR = 0.001
ADAM_B1 = 0.9
ADAM_B2 = 0.999
ADAM_EPS = 1e-08
ADAM_WD = 0.01
ADAM_STEP = 10

N_CHIPS = 4
N_DEV = 8
LANES = 128
PAIR = 2 * HEAD_DIM
BQ = 256
BAND = LEFT_CHUNKS * CHUNK
KWIN = BAND + BQ
VMEM_LIMIT = 56 * 1024 * 1024
PACK_COLS = 1024

NN = (((1,), (0,)), ((), ()))
NT = (((1,), (1,)), ((), ()))
TN = (((0,), (0,)), ((), ()))


def _cparams(sem=None):
    if sem is None:
        return pltpu.CompilerParams(vmem_limit_bytes=VMEM_LIMIT)
    return pltpu.CompilerParams(dimension_semantics=sem, vmem_limit_bytes=VMEM_LIMIT)


def _pick(n, cands):
    for c in cands:
        if n % c == 0:
            return c
    raise ValueError(f"no block for {n}")


def _dot(a, b, dn):
    return lax.dot_general(a, b, dn, preferred_element_type=F32)


def _sigmoid(x):
    return 0.5 * jnp.tanh(0.5 * x) + 0.5


def _split_bf16(x):
    hi = x.astype(BF16)
    lo = (x - hi.astype(F32)).astype(BF16)
    return hi, lo


MM_RESIDENT_BYTES = 12 * 1024 * 1024
MM_TILE_BYTES = 4 * 1024 * 1024


def _matmul(a, b, out_dtype, name, residual=None, slabs=None):
    (M, K), N = a.shape, b.shape[1]
    out_bytes = jnp.dtype(out_dtype).itemsize
    if slabs is None and N <= D_FF and K * N * 2 <= MM_RESIDENT_BYTES:
        bk, bn = K, N
        bm = next(c for c in (1024, 512, 256, 128)
                  if M % c == 0 and c * K * 2 <= MM_TILE_BYTES and c * N * out_bytes <= MM_TILE_BYTES)
        b_spec = pl.BlockSpec((bk, bn), lambda i, j, k: (0, 0), pipeline_mode=pl.Buffered(1))
    else:
        bk = K if K <= D_FF else _pick(K, (1024, 512))
        bm = 512 if D_MODEL < K <= D_FF else _pick(M, (1024, D_FF // 4, 512, 256, 128))
        bn = N // slabs if slabs else _pick(N, (D_FF // 2, 512, 256, 128))
        b_spec = pl.BlockSpec((bk, bn), lambda i, j, k: (k, j))
    nk = K // bk
    dn = NN
    a_spec = pl.BlockSpec((bm, bk), lambda i, j, k: (i, k))
    if slabs:
        o_spec = pl.BlockSpec((None, bm, bn), lambda i, j, k: (j, i, 0))
        out_shape = jax.ShapeDtypeStruct((slabs, M, bn), out_dtype)
    else:
        o_spec = pl.BlockSpec((bm, bn), lambda i, j, k: (i, j))
        out_shape = jax.ShapeDtypeStruct((M, N), out_dtype)
    has_res = residual is not None

    def body(*refs):
        if has_res:
            a_ref, b_ref, r_ref, o_ref, acc_ref = refs
        else:
            a_ref, b_ref, o_ref, acc_ref = refs
        k = pl.program_id(2)
        part = _dot(a_ref[...], b_ref[...], dn)

        def finish(total):
            if has_res:
                total = total + r_ref[...]
            o_ref[...] = total.astype(out_dtype)

        if nk == 1:
            finish(part)
        else:
            @pl.when(k == 0)
            def _():
                acc_ref[...] = part

            @pl.when(k > 0)
            def _():
                acc_ref[...] += part

            @pl.when(k == nk - 1)
            def _():
                finish(acc_ref[...])

    in_specs = [a_spec, b_spec] + ([o_spec] if has_res else [])
    args = (a, b) + ((residual,) if has_res else ())
    return pl.pallas_call(
        body, name=name,
        grid=(M // bm, N // bn, nk),
        in_specs=in_specs, out_specs=o_spec, out_shape=out_shape,
        scratch_shapes=[pltpu.VMEM((bm, bn) if nk > 1 else (8, LANES), F32)],
        compiler_params=_cparams(("parallel", "parallel", "arbitrary")),
    )(*args)


ROWS = 512


def _row_spec(cols, bm=ROWS):
    return pl.BlockSpec((bm, cols), lambda i: (i, 0))


def _col_spec(rows, bn=ROWS):
    return pl.BlockSpec((rows, bn), lambda i: (0, i))


def _full_spec(shape):
    return pl.BlockSpec(shape, lambda i: (0,) * len(shape))


def _colsum8(t):
    return jnp.sum(t.reshape(t.shape[0] // 8, 8, t.shape[1]), axis=0)


def _rms_fwd(x, g, name):
    S, D = x.shape

    def body(x_ref, g_ref, o_ref, ot_ref):
        xv = x_ref[...]
        r = lax.rsqrt(jnp.mean(xv * xv, axis=-1, keepdims=True) + EPS)
        y = xv * r * g_ref[...]
        o_ref[...] = y.astype(BF16)
        ot_ref[...] = y.T.astype(BF16)

    return pl.pallas_call(
        body, name=name, grid=(S // ROWS,),
        in_specs=[_row_spec(D), _full_spec((1, D))], out_specs=[_row_spec(D), _col_spec(D)],
        out_shape=[jax.ShapeDtypeStruct((S, D), BF16), jax.ShapeDtypeStruct((D, S), BF16)],
        compiler_params=_cparams(("parallel",)),
    )(x, g)


def _rms_bwd(x, g, dy, dres, name):
    S, D = x.shape
    nt = S // ROWS

    def body(x_ref, g_ref, dy_ref, dres_ref, dx_ref, dxb_ref, dg_ref, acc_ref):
        i = pl.program_id(0)
        xv, dyv = x_ref[...], dy_ref[...]
        r = lax.rsqrt(jnp.mean(xv * xv, axis=-1, keepdims=True) + EPS)
        xr = xv * r
        u = dyv * g_ref[...]
        dx = r * u - xr * (r * r) * jnp.mean(xv * u, axis=-1, keepdims=True) + dres_ref[...]
        dx_ref[...] = dx
        dxb_ref[...] = dx.astype(BF16)
        part = _colsum8(dyv * xr)

        @pl.when(i == 0)
        def _():
            acc_ref[...] = part

        @pl.when(i > 0)
        def _():
            acc_ref[...] += part

        @pl.when(i == nt - 1)
        def _():
            dg_ref[...] = jnp.sum(acc_ref[...], axis=0, keepdims=True)

    return pl.pallas_call(
        body, name=name, grid=(nt,),
        in_specs=[_row_spec(D), _full_spec((1, D)), _row_spec(D), _row_spec(D)],
        out_specs=[_row_spec(D), _row_spec(D), _full_spec((1, D))],
        out_shape=[jax.ShapeDtypeStruct((S, D), F32), jax.ShapeDtypeStruct((S, D), BF16),
                   jax.ShapeDtypeStruct((1, D), F32)],
        scratch_shapes=[pltpu.VMEM((8, D), F32)],
        compiler_params=_cparams(("arbitrary",)),
    )(x, g, dy, dres)


def _head_mean(t, blockdiag):
    hi, lo = _split_bf16(t)
    return (_dot(hi, blockdiag, NN) + _dot(lo, blockdiag, NN)) * (1.0 / HEAD_DIM)


def _blockdiag():
    r = lax.broadcasted_iota(jnp.int32, (WIDTH, WIDTH), 0) // HEAD_DIM
    c = lax.broadcasted_iota(jnp.int32, (WIDTH, WIDTH), 1) // HEAD_DIM
    return jnp.where(r == c, 1.0, 0.0).astype(BF16)


def _qknorm_fwd(qk, gq, gk, name):
    S = qk.shape[0]

    def body(qk_ref, gq_ref, gk_ref, q_ref, k_ref):
        bd = _blockdiag()
        for part, g_ref, o_ref, scale in ((0, gq_ref, q_ref, HEAD_DIM ** -0.5), (1, gk_ref, k_ref, 1.0)):
            t = qk_ref[:, part * WIDTH:(part + 1) * WIDTH]
            r = lax.rsqrt(_head_mean(t * t, bd) + EPS)
            o_ref[...] = (t * r * g_ref[...] * scale).astype(BF16)

    return pl.pallas_call(
        body, name=name, grid=(S // ROWS,),
        in_specs=[_row_spec(2 * WIDTH), _full_spec((1, WIDTH)), _full_spec((1, WIDTH))],
        out_specs=[_row_spec(WIDTH), _row_spec(WIDTH)],
        out_shape=[jax.ShapeDtypeStruct((S, WIDTH), BF16)] * 2,
        compiler_params=_cparams(("parallel",)),
    )(qk, gq, gk)


def _qknorm_bwd(qk, gq, gk, dqn, dkn, name):
    S = qk.shape[0]
    nt = S // ROWS

    def body(qk_ref, gq_ref, gk_ref, dqn_ref, dkn_ref, dq_ref, dk_ref, dgq_ref, dgk_ref, accq_ref, acck_ref):
        i = pl.program_id(0)
        bd = _blockdiag()
        for part, g_ref, dn_ref, o_ref, dg_ref, acc_ref, scale in (
                (0, gq_ref, dqn_ref, dq_ref, dgq_ref, accq_ref, HEAD_DIM ** -0.5),
                (1, gk_ref, dkn_ref, dk_ref, dgk_ref, acck_ref, 1.0)):
            t = qk_ref[:, part * WIDTH:(part + 1) * WIDTH]
            dn = dn_ref[...] * scale
            r = lax.rsqrt(_head_mean(t * t, bd) + EPS)
            u = dn * g_ref[...]
            dt = r * u - t * (r * r * r) * _head_mean(t * u, bd)
            o_ref[...] = dt.astype(BF16)
            psum = _colsum8(dn * t * r)

            @pl.when(i == 0)
            def _():
                acc_ref[...] = psum

            @pl.when(i > 0)
            def _():
                acc_ref[...] += psum

            @pl.when(i == nt - 1)
            def _():
                dg_ref[...] = jnp.sum(acc_ref[...], axis=0, keepdims=True)

    return pl.pallas_call(
        body, name=name, grid=(nt,),
        in_specs=[_row_spec(2 * WIDTH), _full_spec((1, WIDTH)), _full_spec((1, WIDTH)),
                  _row_spec(WIDTH), _row_spec(WIDTH)],
        out_specs=[_row_spec(WIDTH), _row_spec(WIDTH), _full_spec((1, WIDTH)), _full_spec((1, WIDTH))],
        out_shape=[jax.ShapeDtypeStruct((S, WIDTH), BF16)] * 2 + [jax.ShapeDtypeStruct((1, WIDTH), F32)] * 2,
        scratch_shapes=[pltpu.VMEM((8, WIDTH), F32)] * 2,
        compiler_params=_cparams(("arbitrary",)),
    )(qk, gq, gk, dqn, dkn)


def _mix_fwd(ga, gb, ya, yb, name):
    S, D = ga.shape

    def body(ga_ref, gb_ref, ya_ref, yb_ref, o_ref, ot_ref):
        m = _sigmoid(ga_ref[...]) * ya_ref[...] + _sigmoid(gb_ref[...]) * yb_ref[...]
        o_ref[...] = m.astype(BF16)
        ot_ref[...] = m.T.astype(BF16)

    return pl.pallas_call(
        body, name=name, grid=(S // ROWS,),
        in_specs=[_row_spec(D)] * 4, out_specs=[_row_spec(D), _col_spec(D)],
        out_shape=[jax.ShapeDtypeStruct((S, D), BF16), jax.ShapeDtypeStruct((D, S), BF16)],
        compiler_params=_cparams(("parallel",)),
    )(ga, gb, ya, yb)


def _mix_bwd(dm, ga, gb, ya, yb, name):
    S, D = ga.shape

    def body(dm_ref, ga_ref, gb_ref, ya_ref, yb_ref, dga_ref, dgb_ref, dya_ref, dyb_ref):
        dmv = dm_ref[...]
        for g_ref, y_ref, dg_ref, dy_ref in ((ga_ref, ya_ref, dga_ref, dya_ref), (gb_ref, yb_ref, dgb_ref, dyb_ref)):
            s = _sigmoid(g_ref[...])
            dy_ref[...] = (dmv * s).astype(BF16)
            dg_ref[...] = (dmv * y_ref[...] * s * (1.0 - s)).astype(BF16)

    return pl.pallas_call(
        body, name=name, grid=(S // ROWS,),
        in_specs=[_row_spec(D)] * 5, out_specs=[_row_spec(D)] * 4,
        out_shape=[jax.ShapeDtypeStruct((S, D), BF16)] * 4,
        compiler_params=_cparams(("parallel",)),
    )(dm, ga, gb, ya, yb)


def _loss_head(y, target, name):
    S, D = y.shape
    nt = S // ROWS

    def body(y_ref, t_ref, dy_ref, dyb_ref, p_ref):
        err = y_ref[...] - t_ref[...]
        dy = err * (1.0 / D)
        dy_ref[...] = dy
        dyb_ref[...] = dy.astype(BF16)
        sq = _colsum8(err * err)
        acc = sq[:, 0:LANES]
        for k in range(1, D // LANES):
            acc = acc + sq[:, k * LANES:(k + 1) * LANES]
        p_ref[...] = acc

    return pl.pallas_call(
        body, name=name, grid=(nt,),
        in_specs=[_row_spec(D)] * 2,
        out_specs=[_row_spec(D), _row_spec(D), pl.BlockSpec((8, LANES), lambda i: (i, 0))],
        out_shape=[jax.ShapeDtypeStruct((S, D), F32), jax.ShapeDtypeStruct((S, D), BF16),
                   jax.ShapeDtypeStruct((nt * 8, LANES), F32)],
        compiler_params=_cparams(("parallel",)),
    )(y, target)


CONV_COLS = D_FF // 2
HALO = 16
CONV_CHUNK = 64


def _aligned(start, multiple):
    return start if isinstance(start, int) else pl.multiple_of(start, multiple)


def _conv_taps(xe, cw, cb):
    taps = (pltpu.roll(xe, 2, 0), pltpu.roll(xe, 1, 0), xe)
    return taps, cw[0:1] * taps[0] + cw[1:2] * taps[1] + cw[2:3] * taps[2] + cb


def _conv_specs(nt):
    hb, nb = ROWS // HALO, D_FF // CONV_COLS
    specs = {}
    for part, off in (("gate", 0), ("up", nb)):
        specs[part] = dict(
            main=pl.BlockSpec((ROWS, CONV_COLS), functools.partial(lambda c, i, off: (i, c + off), off=off)),
            prev=pl.BlockSpec((HALO, CONV_COLS),
                              functools.partial(lambda c, i, off: (jnp.maximum(i * hb - 1, 0), c + off), off=off)),
            nxt=pl.BlockSpec((HALO, CONV_COLS),
                             functools.partial(lambda c, i, off: (jnp.minimum((i + 1) * hb, nt * hb - 1), c + off), off=off)),
            w=pl.BlockSpec((3, CONV_COLS), functools.partial(lambda c, i, off: (0, c + off), off=off)),
            b=pl.BlockSpec((1, CONV_COLS), functools.partial(lambda c, i, off: (0, c + off), off=off)))
    return specs


def _convglu_fwd(hid, cw, cb, name):
    S = hid.shape[0]
    sp = _conv_specs(S // ROWS)

    def body(hg_ref, hgp_ref, hu_ref, hup_ref, cwg_ref, cwu_ref, cbg_ref, cbu_ref, o_ref, ot_ref):
        i = pl.program_id(1)
        keep = (i > 0).astype(F32)

        def conv(h_ref, hp_ref, cw_ref, cb_ref):
            xe = jnp.concatenate([hp_ref[...].astype(F32) * keep, h_ref[...].astype(F32)], axis=0)
            return _conv_taps(xe, cw_ref[...], cb_ref[...])[1][HALO:, :]

        gate = conv(hg_ref, hgp_ref, cwg_ref, cbg_ref)
        up = conv(hu_ref, hup_ref, cwu_ref, cbu_ref)
        act = gate * _sigmoid(gate) * up
        o_ref[...] = act.astype(BF16)
        ot_ref[...] = act.T.astype(BF16)

    g, u = sp["gate"], sp["up"]
    return pl.pallas_call(
        body, name=name, grid=(D_FF // CONV_COLS, S // ROWS),
        in_specs=[g["main"], g["prev"], u["main"], u["prev"], g["w"], u["w"], g["b"], u["b"]],
        out_specs=[g["main"], pl.BlockSpec((CONV_COLS, ROWS), lambda c, i: (c, i))],
        out_shape=[jax.ShapeDtypeStruct((S, D_FF), BF16), jax.ShapeDtypeStruct((D_FF, S), BF16)],
        compiler_params=_cparams(("parallel", "parallel")),
    )(hid, hid, hid, hid, cw, cw, cb, cb)


def _convglu_bwd(hid, dact, cw, cb, name):
    S = hid.shape[0]
    nt = S // ROWS
    sp = _conv_specs(nt)

    n_chunks = ROWS // CONV_CHUNK

    def body(hg_ref, hgp_ref, hgn_ref, hu_ref, hup_ref, hun_ref, da_ref, dan_ref,
             cwg_ref, cwu_ref, cbg_ref, cbu_ref,
             dhg_ref, dhu_ref, dcwg_ref, dcwu_ref, dcbg_ref, dcbu_ref, xg_s, xu_s, da_s):
        i = pl.program_id(1)
        kp = (i > 0).astype(F32)
        kn = (i < nt - 1).astype(F32)
        for x_s, h_ref, hp_ref, hn_ref in ((xg_s, hg_ref, hgp_ref, hgn_ref), (xu_s, hu_ref, hup_ref, hun_ref)):
            x_s[0:HALO, :] = hp_ref[...].astype(F32) * kp
            x_s[HALO:HALO + ROWS, :] = h_ref[...].astype(F32)
            x_s[HALO + ROWS:, :] = hn_ref[...].astype(F32) * kn
        da_s[0:ROWS, :] = da_ref[...].astype(F32)
        da_s[ROWS:, :] = dan_ref[...].astype(F32) * kn

        @pl.when(i == 0)
        def _():
            for ref in (dcwg_ref, dcwu_ref, dcbg_ref, dcbu_ref):
                ref[...] = jnp.zeros_like(ref)

        def lane_group(grp, _):
            lanes = pl.ds(pl.multiple_of(grp * LANES, LANES), LANES)
            cwg, cwu, cbg, cbu = cwg_ref[:, lanes], cwu_ref[:, lanes], cbg_ref[:, lanes], cbu_ref[:, lanes]

            def grads(r0, n):
                rows = pl.ds(_aligned(r0 + HALO - 8, 8), n + 8)
                taps_g, gate = _conv_taps(xg_s[rows, lanes], cwg, cbg)
                taps_u, up = _conv_taps(xu_s[rows, lanes], cwu, cbu)
                gate, up = gate[8:], up[8:]
                da = da_s[pl.ds(_aligned(r0, 8), n), lanes]
                sg = _sigmoid(gate)
                return (da * up * sg * (1.0 + gate * (1.0 - sg)), da * gate * sg,
                        [t[8:] for t in taps_g], [t[8:] for t in taps_u])

            def chunk(step, carry):
                below_g, below_u, accs = carry
                r0 = (n_chunks - 1 - step) * CONV_CHUNK
                dg, du, taps_g, taps_u = grads(r0, CONV_CHUNK)
                new_accs = []
                for d, below, cwv, taps, dh_ref, acc in ((dg, below_g, cwg, taps_g, dhg_ref, accs[0]),
                                                        (du, below_u, cwu, taps_u, dhu_ref, accs[1])):
                    ext = jnp.concatenate([d, below], axis=0)
                    n_ext = CONV_CHUNK + 8
                    dh = (cwv[2:3] * d + cwv[1:2] * pltpu.roll(ext, n_ext - 1, 0)[:CONV_CHUNK]
                          + cwv[0:1] * pltpu.roll(ext, n_ext - 2, 0)[:CONV_CHUNK])
                    dh_ref[pl.ds(_aligned(r0, CONV_CHUNK), CONV_CHUNK), lanes] = dh.astype(BF16)
                    new_accs.append(tuple(a + _colsum8(d * tap) for a, tap in zip(acc[:3], taps))
                                    + (acc[3] + _colsum8(d),))
                return dg[0:8], du[0:8], tuple(new_accs)

            below_g, below_u, _, _ = grads(ROWS, 8)
            zero = jnp.zeros((8, LANES), F32)
            _, _, accs = lax.fori_loop(0, n_chunks, chunk, (below_g, below_u, ((zero,) * 4, (zero,) * 4)))
            for acc, dcw_ref, dcb_ref in ((accs[0], dcwg_ref, dcbg_ref), (accs[1], dcwu_ref, dcbu_ref)):
                for t in range(3):
                    dcw_ref[t:t + 1, lanes] += jnp.sum(acc[t], axis=0, keepdims=True)
                dcb_ref[:, lanes] += jnp.sum(acc[3], axis=0, keepdims=True)
            return 0

        lax.fori_loop(0, CONV_COLS // LANES, lane_group, 0)

    g, u = sp["gate"], sp["up"]
    return pl.pallas_call(
        body, name=name, grid=(D_FF // CONV_COLS, nt),
        in_specs=[g["main"], g["prev"], g["nxt"], u["main"], u["prev"], u["nxt"], g["main"], g["nxt"],
                  g["w"], u["w"], g["b"], u["b"]],
        out_specs=[g["main"], g["main"], g["w"], g["w"], g["b"], g["b"]],
        out_shape=[jax.ShapeDtypeStruct((S, D_FF), BF16)] * 2 + [jax.ShapeDtypeStruct((3, D_FF), F32)] * 2
        + [jax.ShapeDtypeStruct((1, D_FF), F32)] * 2,
        scratch_shapes=[pltpu.VMEM((ROWS + 2 * HALO, CONV_COLS), F32)] * 2 + [pltpu.VMEM((ROWS + HALO, CONV_COLS), F32)],
        compiler_params=_cparams(("parallel", "arbitrary")),
    )(hid, hid, hid, hid, hid, hid, dact, dact, cw, cw, cb, cb)


REL_PAD = 384
DIAG = 1024


def _band_valid():
    qc = lax.broadcasted_iota(jnp.int32, (BQ, KWIN), 0) // CHUNK
    kc = lax.broadcasted_iota(jnp.int32, (BQ, KWIN), 1) // CHUNK - LEFT_CHUNKS
    return (kc <= qc) & (kc >= qc - LEFT_CHUNKS)


def _rel_index(offset):
    return jnp.clip(BAND - offset, -MAX_REL, MAX_REL) + MAX_REL


def _split3(x):
    hi = x.astype(BF16)
    rest = x - hi.astype(F32)
    mid = rest.astype(BF16)
    return hi, mid, (rest - mid.astype(F32)).astype(BF16)


def _bias_table(rel_bias, name):
    def body(rb_ref, o_ref):
        t = lax.broadcasted_iota(jnp.int32, (REL_PAD, DIAG), 0)
        lane = lax.broadcasted_iota(jnp.int32, (REL_PAD, DIAG), 1)
        pick = jnp.where(t == _rel_index(lane - BQ), 1.0, 0.0).astype(BF16)
        base = sum(_dot(piece, pick, NN) for piece in _split3(rb_ref[...]))
        valid = _band_valid()
        for h in range(N_HEADS):
            rows = jnp.broadcast_to(base[h:h + 1], (BQ, DIAG))
            rolled = pltpu.roll(rows, 0, 1, stride=1, stride_axis=0)
            o_ref[h] = jnp.where(valid, rolled[:, BQ:], NEG)

    return pl.pallas_call(
        body, name=name,
        out_shape=jax.ShapeDtypeStruct((N_HEADS, BQ, KWIN), F32),
        compiler_params=_cparams(),
    )(rel_bias)


def _bias_table_bwd(dtab, name):
    def body(d_ref, o_ref, diag_ref):
        r = lax.broadcasted_iota(jnp.int32, (BQ, BQ), 0)
        c = lax.broadcasted_iota(jnp.int32, (BQ, BQ), 1)
        flip = jnp.where(r + c == BQ - 1, 1.0, 0.0).astype(BF16)
        for h in range(N_HEADS):
            flipped = sum(_dot(flip, piece, NN) for piece in _split3(d_ref[h]))
            padded = jnp.concatenate([flipped, jnp.zeros((BQ, DIAG - KWIN), F32)], axis=1)
            rolled = pltpu.roll(padded, DIAG - (BQ - 1), 1, stride=1, stride_axis=0)
            diag_ref[h:h + 1, :] = jnp.sum(rolled, axis=0, keepdims=True)
        lane = lax.broadcasted_iota(jnp.int32, (DIAG, REL_PAD), 0)
        t = lax.broadcasted_iota(jnp.int32, (DIAG, REL_PAD), 1)
        offset = jnp.where(lane < KWIN, lane, lane - DIAG)
        pick = jnp.where(t == _rel_index(offset), 1.0, 0.0).astype(BF16)
        o_ref[...] = sum(_dot(piece, pick, NN) for piece in _split3(diag_ref[...]))

    return pl.pallas_call(
        body, name=name,
        out_shape=jax.ShapeDtypeStruct((N_HEADS, REL_PAD), F32),
        scratch_shapes=[pltpu.VMEM((N_HEADS, DIAG), F32)],
        compiler_params=_cparams(),
    )(dtab)


def _head_masks(heads=2):
    lane = lax.broadcasted_iota(jnp.int32, (1, heads * HEAD_DIM), 1)
    return [lane // HEAD_DIM == h for h in range(heads)]


def _own_lanes(masks, vals):
    out = vals[-1]
    for m, val in zip(masks[-2::-1], vals[-2::-1]):
        out = jnp.where(m, val, out)
    return out


CA_HEADS = 4
CA_LANES = CA_HEADS * HEAD_DIM


def _ca_window_specs(nq, col_off=0):
    return [pl.BlockSpec((BQ, CA_LANES), functools.partial(
        lambda p, i, d: (jnp.clip(i - 2 + d, 0, nq - 1), p + col_off), d=d)) for d in range(3)]


def _softmax_rows(s):
    p = jnp.exp(s - jnp.max(s, axis=-1, keepdims=True))
    return p, jnp.sum(p, axis=-1, keepdims=True)


def _ca_scores(qm, kc, tab_h, i):
    col = lax.broadcasted_iota(jnp.int32, (1, KWIN), 1)
    in_seq = col + (i - 2) * BQ >= 0
    return jnp.where(in_seq, _dot(qm, kc, NT) + tab_h, NEG)


def _ca_fwd(qn, kn, v, tab, name, v_off=0):
    S = qn.shape[0]
    nq = S // BQ
    qspec = pl.BlockSpec((BQ, CA_LANES), lambda p, i: (i, p))
    tspec = pl.BlockSpec((CA_HEADS, BQ, KWIN), lambda p, i: (p, 0, 0))

    def body(q_ref, k0, k1, k2, v0, v1, v2, tab_ref, o_ref, ot_ref):
        i = pl.program_id(1)
        kc = jnp.concatenate([k0[...], k1[...], k2[...]], axis=0)
        vc = jnp.concatenate([v0[...], v1[...], v2[...]], axis=0)
        qv = q_ref[...]
        masks = _head_masks(CA_HEADS)
        heads = range(CA_HEADS)
        s = [_ca_scores(jnp.where(masks[h], qv, 0), kc, tab_ref[h], i) for h in heads]
        soft = [_softmax_rows(s[h]) for h in heads]
        o = [_dot(soft[h][0].astype(BF16), vc, NN) / soft[h][1] for h in heads]
        out = _own_lanes(masks, o)
        o_ref[...] = out.astype(BF16)
        ot_ref[...] = out.T.astype(BF16)

    return pl.pallas_call(
        body, name=name, grid=(WIDTH // CA_LANES, nq),
        in_specs=[qspec] + _ca_window_specs(nq) + _ca_window_specs(nq, v_off) + [tspec],
        out_specs=[qspec, pl.BlockSpec((CA_LANES, BQ), lambda p, i: (p, i))],
        out_shape=[jax.ShapeDtypeStruct((S, WIDTH), BF16), jax.ShapeDtypeStruct((WIDTH, S), BF16)],
        compiler_params=_cparams(("parallel", "parallel")),
    )(qn, kn, kn, kn, v, v, v, tab)


def _ca_bwd(qn, kn, v, do, tab, name, v_off=0):
    S = qn.shape[0]
    nq = S // BQ
    qspec = pl.BlockSpec((BQ, CA_LANES), lambda p, i: (jnp.minimum(i, nq - 1), p))
    kout = pl.BlockSpec((BQ, CA_LANES), lambda p, i: (jnp.clip(i - 2, 0, nq - 1), p))
    tspec = pl.BlockSpec((CA_HEADS, BQ, KWIN), lambda p, i: (p, 0, 0))

    def body(q_ref, do_ref, k0, k1, k2, v0, v1, v2, tab_ref,
             dq_ref, dk_ref, dv_ref, dtab_ref, dk_acc, dv_acc):
        i = pl.program_id(1)

        @pl.when(i == 0)
        def _():
            dk_acc[...] = jnp.zeros_like(dk_acc)
            dv_acc[...] = jnp.zeros_like(dv_acc)
            dtab_ref[...] = jnp.zeros_like(dtab_ref)

        @pl.when(i < nq)
        def _():
            kc = jnp.concatenate([k0[...], k1[...], k2[...]], axis=0)
            vc = jnp.concatenate([v0[...], v1[...], v2[...]], axis=0)
            qv, dov = q_ref[...], do_ref[...]
            masks = _head_masks(CA_HEADS)
            heads = range(CA_HEADS)
            qm = [jnp.where(masks[h], qv, 0) for h in heads]
            dom = [jnp.where(masks[h], dov, 0) for h in heads]
            s = [_ca_scores(qm[h], kc, tab_ref[h], i) for h in heads]
            dp = [_dot(dom[h], vc, NT) for h in heads]
            soft = [_softmax_rows(s[h]) for h in heads]
            p = [soft[h][0] / soft[h][1] for h in heads]
            ds = [p[h] * (dp[h] - jnp.sum(p[h] * dp[h], axis=-1, keepdims=True)) for h in heads]
            for h in heads:
                dtab_ref[h] += ds[h]
            dsb = [ds[h].astype(BF16) for h in heads]
            pb = [p[h].astype(BF16) for h in heads]
            dq = [_dot(dsb[h], kc, NN) for h in heads]
            dq_ref[...] = _own_lanes(masks, dq)
            dkc = sum(_dot(dsb[h], qm[h], TN) for h in heads)
            dvc = sum(_dot(pb[h], dom[h], TN) for h in heads)
            for d in range(3):
                slot = (i + 1 + d) % 3
                dk_acc[slot] += dkc[d * BQ:(d + 1) * BQ]
                dv_acc[slot] += dvc[d * BQ:(d + 1) * BQ]

        @pl.when(i >= 2)
        def _():
            slot = (i + 1) % 3
            dk_ref[...] = dk_acc[slot]
            dv_ref[...] = dv_acc[slot].astype(BF16)
            dk_acc[slot] = jnp.zeros((BQ, CA_LANES), F32)
            dv_acc[slot] = jnp.zeros((BQ, CA_LANES), F32)

    return pl.pallas_call(
        body, name=name, grid=(WIDTH // CA_LANES, nq + 2),
        in_specs=[qspec, qspec] + _ca_window_specs(nq) + _ca_window_specs(nq, v_off) + [tspec],
        out_specs=[qspec, kout, kout, tspec],
        out_shape=[jax.ShapeDtypeStruct((S, WIDTH), F32), jax.ShapeDtypeStruct((S, WIDTH), F32),
                   jax.ShapeDtypeStruct((S, WIDTH), BF16), jax.ShapeDtypeStruct((N_HEADS, BQ, KWIN), F32)],
        scratch_shapes=[pltpu.VMEM((3, BQ, CA_LANES), F32)] * 2,
        compiler_params=_cparams(("parallel", "arbitrary")),
    )(qn, do, kn, kn, kn, v, v, v, tab)


def _sb_consts():
    r = lax.broadcasted_iota(jnp.int32, (BQ, BQ), 0)
    c = lax.broadcasted_iota(jnp.int32, (BQ, BQ), 1)
    from_s = jnp.where(r >= c, 1.0, 0.0).astype(BF16)
    causal = c < r
    return from_s, causal


def _suffix_sum(t, from_s):
    hi, lo = _split_bf16(t)
    return _dot(hi, from_s, NN) + _dot(lo, from_s, NN)


def _neg_abs(x):
    bits = lax.bitcast_convert_type(x, jnp.uint32) | jnp.uint32(0x80000000)
    return lax.bitcast_convert_type(bits, F32)


def _sb_log_keep(zn):
    return jnp.minimum(zn, 0.0) - jnp.log(1.0 + jnp.exp(_neg_abs(zn)))


SB_DEAD = 105.0


SB_QB = 2


def _sb_walk(ip, tiles, keep_ref):
    i0 = SB_QB * ip

    @pl.when(ip == 0)
    def _():
        tiles([(0, [0], [True]), (1, [1, 0], [True, False])])

    @pl.when(ip > 0)
    def _():
        tiles([(a, [i0 + a, i0 + a - 1], [True, False]) for a in range(SB_QB)])

    for a in range(SB_QB):
        def alive(a=a):
            return (jnp.max(keep_ref[2 * a:2 * a + 2]) > -SB_DEAD).astype(jnp.int32)

        def step(state, a=a, alive=alive):
            j, _ = state
            tiles([(a, [j], [False])])
            return j - 1, alive()

        lax.while_loop(lambda state: (state[0] >= 0) & (state[1] > 0), step, (i0 + a - 2, alive()))


def _sb_rows(j):
    return pl.ds(pl.multiple_of(j * BQ, BQ), BQ)


def _sb_chains(groups):
    chains = [(a, n, h) for a, js, _ in groups for n in range(len(js)) for h in range(2)]
    block = {(a, n): j for a, js, _ in groups for n, j in enumerate(js)}
    masked = [(a, n, h) for a, _, diags in groups for n, d in enumerate(diags) if d for h in range(2)]
    return chains, block, masked


def _sb_running(ref, vals, groups):
    before_chain = {}
    for a, js, _ in groups:
        for h in range(2):
            run = ref[2 * a + h]
            for n in range(len(js)):
                before_chain[(a, n, h)] = run
                run = run + jnp.sum(vals[(a, n, h)], axis=-1, keepdims=True)
            ref[2 * a + h] = run
    return before_chain


def _sb_specs(S, offs):
    def qspec(off=0):
        return pl.BlockSpec((SB_QB * BQ, PAIR), lambda p, i: (i, p + off))

    def kspec(off=0):
        return pl.BlockSpec((S, PAIR), lambda p, i: (0, p + off), pipeline_mode=pl.Buffered(1))

    return qspec, kspec, [qspec(offs[0]), kspec(offs[1]), kspec(offs[2])]


def _sb_fwd(q, k, v, name, offs=(0, 0, 0)):
    S = q.shape[0]
    steps = S // (SB_QB * BQ)
    qspec, _, qkv_specs = _sb_specs(S, offs)

    def body(q_ref, k_ref, v_ref, o_ref, of_ref, ot_ref, carry_ref, acc_ref):
        ip = pl.program_id(1)
        from_s, causal = _sb_consts()
        masks = _head_masks()
        qn = q_ref[...] * -(HEAD_DIM ** -0.5)
        qms = {(a, h): jnp.where(masks[h], qn[a * BQ:(a + 1) * BQ], 0) for a in range(SB_QB) for h in range(2)}
        carry_ref[...] = jnp.zeros_like(carry_ref)
        acc_ref[...] = jnp.zeros_like(acc_ref)

        def tiles(groups):
            chains, block, masked = _sb_chains(groups)
            kbs = {an: k_ref[_sb_rows(j), :] for an, j in block.items()}
            vbs = {an: v_ref[_sb_rows(j), :] for an, j in block.items()}
            zn = {c: _dot(qms[(c[0], c[2])], kbs[c[:2]], NT) for c in chains}
            log_keep = {c: _sb_log_keep(zn[c]) for c in chains}
            for c in masked:
                log_keep[c] = jnp.where(causal, log_keep[c], 0.0)
            split = {c: _split_bf16(log_keep[c]) for c in chains}
            carry = _sb_running(carry_ref, log_keep, groups)
            suffix = {c: _dot(split[c][0], from_s, NN) + _dot(split[c][1], from_s, NN) for c in chains}
            w = {c: jnp.exp(carry[c] + suffix[c] - zn[c]) for c in chains}
            for c in masked:
                w[c] = jnp.where(causal, w[c], 0.0)
            for c in chains:
                acc_ref[2 * c[0] + c[2]] += _dot(w[c].astype(BF16), vbs[c[:2]], NN)

        _sb_walk(ip, tiles, carry_ref)
        for a in range(SB_QB):
            out = jnp.where(masks[0], acc_ref[2 * a], acc_ref[2 * a + 1])
            o_ref[a * BQ:(a + 1) * BQ, :] = out.astype(BF16)
            of_ref[a * BQ:(a + 1) * BQ, :] = out
            ot_ref[:, a * BQ:(a + 1) * BQ] = out.T.astype(BF16)

    return pl.pallas_call(
        body, name=name, grid=(WIDTH // PAIR, steps),
        in_specs=qkv_specs, out_specs=[qspec(), qspec(), pl.BlockSpec((PAIR, SB_QB * BQ), lambda p, i: (p, i))],
        out_shape=[jax.ShapeDtypeStruct((S, WIDTH), BF16), jax.ShapeDtypeStruct((S, WIDTH), F32),
                   jax.ShapeDtypeStruct((WIDTH, S), BF16)],
        scratch_shapes=[pltpu.VMEM((2 * SB_QB, BQ, 1), F32), pltpu.VMEM((2 * SB_QB, BQ, PAIR), F32)],
        compiler_params=_cparams(("parallel", "arbitrary")),
    )(q, k, v)


def _sb_bwd(q, k, v, o, do, name, offs=(0, 0, 0)):
    S = q.shape[0]
    steps = S // (SB_QB * BQ)
    qspec, kspec, qkv_specs = _sb_specs(S, offs)

    def body(q_ref, o_ref, do_ref, k_ref, v_ref, dq_ref, dk_ref, dv_ref, dk_acc, dv_acc, keep_ref, gsum_ref, dq_acc):
        ip = pl.program_id(1)

        @pl.when(ip == 0)
        def _():
            dk_acc[...] = jnp.zeros_like(dk_acc)
            dv_acc[...] = jnp.zeros_like(dv_acc)

        from_s, causal = _sb_consts()
        masks = _head_masks()
        qn, dov = q_ref[...] * -(HEAD_DIM ** -0.5), do_ref[...]
        od = o_ref[...] * dov.astype(F32)
        lanes = [(a, h) for a in range(SB_QB) for h in range(2)]
        rows_of = {a: slice(a * BQ, (a + 1) * BQ) for a in range(SB_QB)}
        qms = {(a, h): jnp.where(masks[h], qn[rows_of[a]], 0) for a, h in lanes}
        doms = {(a, h): jnp.where(masks[h], dov[rows_of[a]], 0) for a, h in lanes}
        totals = {(a, h): jnp.sum(jnp.where(masks[h], od[rows_of[a]], 0.0), axis=-1, keepdims=True)
                  for a, h in lanes}
        for ref in (keep_ref, gsum_ref, dq_acc):
            ref[...] = jnp.zeros_like(ref)

        def tiles(groups):
            chains, block, masked = _sb_chains(groups)
            kbs = {an: k_ref[_sb_rows(j), :] for an, j in block.items()}
            vbs = {an: v_ref[_sb_rows(j), :] for an, j in block.items()}
            zn = {c: _dot(qms[(c[0], c[2])], kbs[c[:2]], NT) for c in chains}
            dw = {c: _dot(doms[(c[0], c[2])], vbs[c[:2]], NT) for c in chains}
            log_keep = {c: _sb_log_keep(zn[c]) for c in chains}
            for c in masked:
                log_keep[c] = jnp.where(causal, log_keep[c], 0.0)
            split = {c: _split_bf16(log_keep[c]) for c in chains}
            kept = _sb_running(keep_ref, log_keep, groups)
            suffix = {c: _dot(split[c][0], from_s, NN) + _dot(split[c][1], from_s, NN) for c in chains}
            w = {c: jnp.exp(kept[c] + suffix[c] - zn[c]) for c in chains}
            for c in masked:
                w[c] = jnp.where(causal, w[c], 0.0)
            wb = {c: w[c].astype(BF16) for c in chains}
            g = {c: wb[c].astype(F32) * dw[c] for c in chains}
            gsplit = {c: _split_bf16(g[c]) for c in chains}
            gsum = _sb_running(gsum_ref, g, groups)
            gsuffix = {c: _dot(gsplit[c][0], from_s, NN) + _dot(gsplit[c][1], from_s, NN) for c in chains}
            dzb = {}
            for c in chains:
                before = totals[(c[0], c[2])] - (gsum[c] + gsuffix[c])
                dz = (g[c] + before) * jnp.exp(log_keep[c]) - before
                if c in masked:
                    dz = jnp.where(causal, dz, 0.0)
                dzb[c] = dz.astype(BF16)
            for c in chains:
                rows = _sb_rows(block[c[:2]])
                dq_acc[2 * c[0] + c[2]] += _dot(dzb[c], kbs[c[:2]], NN)
                dk_acc[rows, :] -= _dot(dzb[c], qms[(c[0], c[2])], TN)
                dv_acc[rows, :] += _dot(wb[c], doms[(c[0], c[2])], TN)

        _sb_walk(ip, tiles, keep_ref)
        for a in range(SB_QB):
            dq = jnp.where(masks[0], dq_acc[2 * a], dq_acc[2 * a + 1])
            dq_ref[a * BQ:(a + 1) * BQ, :] = (dq * HEAD_DIM ** -0.5).astype(BF16)

        @pl.when(ip == steps - 1)
        def _():
            dk_ref[...] = dk_acc[...].astype(BF16)
            dv_ref[...] = dv_acc[...].astype(BF16)

    return pl.pallas_call(
        body, name=name, grid=(WIDTH // PAIR, steps),
        in_specs=[qkv_specs[0], qspec(), qspec(), qkv_specs[1], qkv_specs[2]], out_specs=[qspec(), kspec(), kspec()],
        out_shape=[jax.ShapeDtypeStruct((S, WIDTH), BF16)] * 3,
        scratch_shapes=[pltpu.VMEM((S, PAIR), F32)] * 2 + [pltpu.VMEM((2 * SB_QB, BQ, 1), F32)] * 2
        + [pltpu.VMEM((2 * SB_QB, BQ, PAIR), F32)],
        compiler_params=_cparams(("parallel", "arbitrary")),
    )(q, o, do, k, v)


ANY = pl.BlockSpec(memory_space=pl.ANY)


def _place():
    return lax.axis_index("x"), lax.axis_index("y"), lax.axis_index("c")


def _other_chips(x, y):
    return [(2 * px + py, (px, py)) for px, py in ((1 - x, y), (x, 1 - y), (1 - x, 1 - y))]


def _remote(src, dst, sems, k, to):
    return pltpu.make_async_remote_copy(src_ref=src, dst_ref=dst, send_sem=sems[0].at[k], recv_sem=sems[1].at[k],
                                        device_id=to, device_id_type=MESH)


def _gather_weights(ws, extra, name):
    n = len(ws)

    def body(*refs):
        w_refs, e_ref, out_refs, eo_ref = refs[:n], refs[n], refs[n + 1:2 * n + 1], refs[2 * n + 1]
        sems = refs[2 * n + 2:]
        x, y, c = _place()
        me = 2 * x + y
        chips = _other_chips(x, y)

        def halves(ref):
            rh = ref.shape[-2] // 2
            return pl.ds(c * rh, rh), pl.ds((1 - c) * rh, rh)

        first = [_remote(w_ref.at[halves(w_ref)[0]], o_ref.at[me, halves(w_ref)[0]], sems, 6 * a + k, (*xy, c))
                 for a, (w_ref, o_ref) in enumerate(zip(w_refs, out_refs)) for k, (_, xy) in enumerate(chips)]
        first += [_remote(e_ref, eo_ref.at[me], sems, 6 * n + k, (*xy, c)) for k, (_, xy) in enumerate(chips)]
        for cp in first:
            cp.start()
        passed = []
        for a, o_ref in enumerate(out_refs):
            for k, (chip, xy) in enumerate(chips):
                landed = o_ref.at[chip, halves(o_ref)[0]]
                _remote(landed, landed, sems, 6 * a + k, (*xy, c)).wait_recv()
                cp = _remote(landed, landed, sems, 6 * a + 3 + k, (x, y, 1 - c))
                cp.start()
                passed.append(cp)
        for a, o_ref in enumerate(out_refs):
            for k, (chip, xy) in enumerate(chips):
                landed = o_ref.at[chip, halves(o_ref)[1]]
                _remote(landed, landed, sems, 6 * a + 3 + k, (x, y, 1 - c)).wait_recv()
        for k, (chip, xy) in enumerate(chips):
            _remote(e_ref, eo_ref.at[chip], sems, 6 * n + k, (*xy, c)).wait_recv()
        for cp in first + passed:
            cp.wait_send()

    n_copies = 6 * n + 3
    return pl.pallas_call(
        body, name=name, in_specs=[ANY] * (n + 1), out_specs=[ANY] * (n + 1),
        out_shape=[jax.ShapeDtypeStruct((N_CHIPS,) + w.shape, w.dtype) for w in ws]
        + [jax.ShapeDtypeStruct((N_CHIPS,) + extra.shape, extra.dtype)],
        scratch_shapes=[pltpu.SemaphoreType.DMA((n_copies,)), pltpu.SemaphoreType.DMA((n_copies,))],
    )(*ws, extra)


def _exchange_cores(gs, small, name):
    n = len(gs)

    def body(*refs):
        g_refs, small_ref, sib_refs, all_ref = refs[:n], refs[n], refs[n + 1:2 * n + 1], refs[2 * n + 1]
        sems = refs[2 * n + 2:]
        x, y, c = _place()
        me = 4 * x + 2 * y + c
        copies = []
        for a, (g_ref, sib_ref) in enumerate(zip(g_refs, sib_refs)):
            rh = g_ref.shape[1] // 2
            copies.append(_remote(g_ref.at[:, pl.ds((1 - c) * rh, rh), :], sib_ref, sems, a, (x, y, 1 - c)))
        k = n
        for fx in (0, 1):
            for fy in (0, 1):
                for fc in (0, 1):
                    if fx or fy or fc:
                        to = (1 - x if fx else x, 1 - y if fy else y, 1 - c if fc else c)
                        copies.append(_remote(small_ref, all_ref.at[me], sems, k, to))
                        k += 1
        for cp in copies:
            cp.start()
        for cp in copies:
            cp.wait_recv()
        for cp in copies:
            cp.wait_send()

    n_copies = n + N_DEV - 1
    return pl.pallas_call(
        body, name=name, in_specs=[ANY] * (n + 1), out_specs=[ANY] * (n + 1),
        out_shape=[jax.ShapeDtypeStruct((N_CHIPS, g.shape[1] // 2, g.shape[2]), F32) for g in gs]
        + [jax.ShapeDtypeStruct((N_DEV,) + small.shape, F32)],
        scratch_shapes=[pltpu.SemaphoreType.DMA((n_copies,)), pltpu.SemaphoreType.DMA((n_copies,))],
    )(*gs, small)


def _exchange_chips(ps, name):
    n = len(ps)

    def body(*refs):
        p_refs, out_refs, sems = refs[:n], refs[n:2 * n], refs[2 * n:]
        x, y, c = _place()
        me = 2 * x + y
        chips = _other_chips(x, y)
        copies = [_remote(p_ref.at[chip], o_ref.at[me], sems, 3 * a + k, (*xy, c))
                  for a, (p_ref, o_ref) in enumerate(zip(p_refs, out_refs)) for k, (chip, xy) in enumerate(chips)]
        for cp in copies:
            cp.start()
        for a, (p_ref, o_ref) in enumerate(zip(p_refs, out_refs)):
            for k, (chip, xy) in enumerate(chips):
                _remote(p_ref.at[chip], o_ref.at[chip], sems, 3 * a + k, (*xy, c)).wait_recv()
        for cp in copies:
            cp.wait_send()

    return pl.pallas_call(
        body, name=name, in_specs=[ANY] * n, out_specs=[ANY] * n,
        out_shape=[jax.ShapeDtypeStruct(p.shape, p.dtype) for p in ps],
        scratch_shapes=[pltpu.SemaphoreType.DMA((3 * n,)), pltpu.SemaphoreType.DMA((3 * n,))],
    )(*ps)


def _share_halves(ghs, name):
    n = len(ghs)

    def body(*refs):
        gh_refs, out_refs, sems = refs[:n], refs[n:2 * n], refs[2 * n:]
        x, y, c = _place()
        copies = [_remote(gh_ref, o_ref, sems, a, (x, y, 1 - c)) for a, (gh_ref, o_ref) in enumerate(zip(gh_refs, out_refs))]
        for cp in copies:
            cp.start()
        for cp in copies:
            cp.wait_recv()
        for cp in copies:
            cp.wait_send()

    return pl.pallas_call(
        body, name=name, in_specs=[ANY] * n, out_specs=[ANY] * n,
        out_shape=[jax.ShapeDtypeStruct(g.shape, g.dtype) for g in ghs],
        scratch_shapes=[pltpu.SemaphoreType.DMA((n,)), pltpu.SemaphoreType.DMA((n,))],
    )(*ghs)


EW_BLOCK_BYTES = 2 * 1024 * 1024


def _row_block(rows, cols, mult=8):
    fits = [b for b in range(mult, rows + 1, mult) if rows % b == 0 and b * cols * 4 <= EW_BLOCK_BYTES]
    return max(fits) if fits else mult


def _add2(a, b, name):
    R, C = a.shape
    rows = _row_block(R, C, mult=16)
    spec = pl.BlockSpec((rows, C), lambda i: (i, 0))

    def body(a_ref, b_ref, o_ref):
        o_ref[...] = (a_ref[...] + b_ref[...]).astype(BF16)

    return pl.pallas_call(
        body, name=name, grid=(R // rows,), in_specs=[spec, spec], out_specs=spec,
        out_shape=jax.ShapeDtypeStruct(a.shape, BF16),
        compiler_params=_cparams(("parallel",)),
    )(a, b)


def _sum_leading(a, name):
    n, R, C = a.shape
    rows = _row_block(R, n * C, mult=16 if a.dtype == BF16 else 8)

    def body(a_ref, o_ref):
        acc = a_ref[0].astype(F32)
        for j in range(1, n):
            acc = acc + a_ref[j].astype(F32)
        o_ref[...] = acc

    return pl.pallas_call(
        body, name=name, grid=(R // rows,),
        in_specs=[pl.BlockSpec((n, rows, C), lambda i: (0, i, 0))],
        out_specs=pl.BlockSpec((rows, C), lambda i: (i, 0)),
        out_shape=jax.ShapeDtypeStruct((R, C), F32),
        compiler_params=_cparams(("parallel",)),
    )(a)


def _adamw(w, g, m, v, name):
    R, C = w.shape
    rows = _row_block(R, C)
    spec = pl.BlockSpec((rows, C), lambda i: (i, 0))

    def body(w_ref, g_ref, m_ref, v_ref, d_ref, mo_ref, vo_ref):
        gv = g_ref[...]
        mn = ADAM_B1 * m_ref[...] + (1.0 - ADAM_B1) * gv
        vn = ADAM_B2 * v_ref[...] + (1.0 - ADAM_B2) * (gv * gv)
        m_hat = mn / (1.0 - ADAM_B1 ** ADAM_STEP)
        v_hat = vn / (1.0 - ADAM_B2 ** ADAM_STEP)
        d_ref[...] = -ADAM_LR * (m_hat / (jnp.sqrt(v_hat) + ADAM_EPS) + ADAM_WD * w_ref[...])
        mo_ref[...] = mn
        vo_ref[...] = vn

    return pl.pallas_call(
        body, name=name, grid=(R // rows,), in_specs=[spec] * 4, out_specs=[spec] * 3,
        out_shape=[jax.ShapeDtypeStruct((R, C), F32)] * 3,
        compiler_params=_cparams(("parallel",)),
    )(w, g, m, v)


BIG = ("w_in", "w_branch_a", "w_branch_b", "w_out", "w_ffn_up", "w_ffn_down")
COL_SHARDED = {"w_in": True, "w_branch_a": True, "w_branch_b": True, "w_out": False, "w_ffn_up": True,
               "w_ffn_down": False}
CONV_W_COLS = 2 * D_FF // N_CHIPS
SMALL_REPLICATED = (("norm1_g", D_MODEL), ("q_norm_g", HEAD_DIM), ("k_norm_g", HEAD_DIM),
                    ("rel_bias", N_HEADS * N_REL), ("norm2_g", D_MODEL), ("ffn_conv_b", 2 * D_FF))
SMALL_GRADS = SMALL_REPLICATED + (("ffn_conv_w", 3 * 2 * D_FF),)
SMALL_OWN = SMALL_REPLICATED + (("ffn_conv_w", 3 * CONV_W_COLS),)
SMALL_GRAD_ROWS = 32
SMALL_OWN_ROWS = 16


def _whole(name, stacked):
    return jnp.concatenate(list(stacked), axis=1) if COL_SHARDED[name] else stacked.reshape(-1, stacked.shape[2])


def _pack_small(vals, sizes, rows):
    flat = jnp.concatenate([vals[n].reshape(-1) for n, _ in sizes])
    return jnp.pad(flat, (0, rows * PACK_COLS - flat.shape[0])).reshape(rows, PACK_COLS)


def _unpack_small(packed, sizes):
    flat, out, o = packed.reshape(-1), {}, 0
    for n, sz in sizes:
        out[n] = flat[o:o + sz]
        o += sz
    return out


def kernel(x, norm1_g, w_in, q_norm_g, k_norm_g, rel_bias, w_branch_a, w_branch_b, w_out, norm2_g, w_ffn_up, ffn_conv_w, ffn_conv_b, w_ffn_down, loss_target, m_norm1_g, m_w_in, m_q_norm_g, m_k_norm_g, m_rel_bias, m_w_branch_a, m_w_branch_b, m_w_out, m_norm2_g, m_w_ffn_up, m_ffn_conv_w, m_ffn_conv_b, m_w_ffn_down, v_norm1_g, v_w_in, v_q_norm_g, v_k_norm_g, v_rel_bias, v_w_branch_a, v_w_branch_b, v_w_out, v_norm2_g, v_w_ffn_up, v_ffn_conv_w, v_ffn_conv_b, v_w_ffn_down):
    w_big = {"w_in": w_in[0], "w_branch_a": w_branch_a[0], "w_branch_b": w_branch_b[0], "w_out": w_out[0],
             "w_ffn_up": w_ffn_up[0], "w_ffn_down": w_ffn_down[0]}
    m_big = {"w_in": m_w_in[0], "w_branch_a": m_w_branch_a[0], "w_branch_b": m_w_branch_b[0], "w_out": m_w_out[0],
             "w_ffn_up": m_w_ffn_up[0], "w_ffn_down": m_w_ffn_down[0]}
    v_big = {"w_in": v_w_in[0], "w_branch_a": v_w_branch_a[0], "w_branch_b": v_w_branch_b[0], "w_out": v_w_out[0],
             "w_ffn_up": v_w_ffn_up[0], "w_ffn_down": v_w_ffn_down[0]}
    xs, tgt = x[0], loss_target[0]

    xi, yi, ci = _place()
    chip = 2 * xi + yi

    def with_own(stacked, own):
        return lax.dynamic_update_slice(stacked, own[None], (chip,) + (0,) * own.ndim)

    shards_bf = [w_big[n].astype(BF16) for n in BIG]
    conv_own = jnp.pad(ffn_conv_w[0], ((0, 8 - ffn_conv_w.shape[1]), (0, 0)))
    *gathered, conv_all = _gather_weights(shards_bf, conv_own, "gather_weights")
    full = {n: _whole(n, with_own(g, s)) for n, g, s in zip(BIG, gathered, shards_bf)}
    conv_w = jnp.concatenate(list(with_own(conv_all, conv_own)[:, :3]), axis=1)
    w_in_f, w_a, w_b, w_o = full["w_in"], full["w_branch_a"], full["w_branch_b"], full["w_out"]
    w_up, w_dn = full["w_ffn_up"], full["w_ffn_down"]
    w_in_t, w_a_t, w_b_t, w_o_t, w_up_t, w_dn_t = (w.T for w in (w_in_f, w_a, w_b, w_o, w_up, w_dn))

    hn, hn_t = _rms_fwd(xs, norm1_g, "rms1")
    qk = _matmul(hn, w_in_f[:, :2 * WIDTH], F32, "proj_qk")
    vqkv = _matmul(hn, w_in_f[:, 2 * WIDTH:6 * WIDTH], BF16, "proj_vqkv")
    g_a = _matmul(hn, w_in_f[:, 6 * WIDTH:6 * WIDTH + D_MODEL], F32, "proj_gate_a")
    g_b = _matmul(hn, w_in_f[:, 6 * WIDTH + D_MODEL:], F32, "proj_gate_b")
    gq = jnp.tile(q_norm_g, (1, N_HEADS))
    gk = jnp.tile(k_norm_g, (1, N_HEADS))
    qa, ka = _qknorm_fwd(qk, gq, gk, "qknorm")
    per = WIDTH // PAIR
    b_offs = (per, 2 * per, 3 * per)
    tab = _bias_table(jnp.pad(rel_bias[0], ((0, 0), (0, REL_PAD - N_REL))), "bias_table")
    out_a, out_a_t = _ca_fwd(qa, ka, vqkv, tab, "chunk_attn")
    out_b, out_b_f32, out_b_t = _sb_fwd(vqkv, vqkv, vqkv, "stick_attn", b_offs)
    y_a = _matmul(out_a, w_a, F32, "branch_a")
    y_b = _matmul(out_b, w_b, F32, "branch_b")
    mixed, mixed_t = _mix_fwd(g_a, g_b, y_a, y_b, "mix")
    x2 = _matmul(mixed, w_o, F32, "out_proj", residual=xs)
    hn2, hn2_t = _rms_fwd(x2, norm2_g, "rms2")
    hid = _matmul(hn2, w_up, BF16, "ffn_up")
    act, act_t = _convglu_fwd(hid, conv_w, ffn_conv_b, "convglu")
    y = _matmul(act, w_dn, F32, "ffn_down", residual=x2)
    dy, dyb, sq = _loss_head(y, tgt, "loss_head")
    loss = lax.psum(0.5 / D_MODEL * jnp.sum(sq), ("x", "y", "c"))

    dact = _matmul(dyb, w_dn_t, BF16, "d_act")
    d_w_dn = _matmul(act_t, dyb, F32, "d_w_down")
    dhg, dhu, dcwg, dcwu, dcbg, dcbu = _convglu_bwd(hid, dact, conv_w, ffn_conv_b, "convglu_bwd")
    half_chips = N_CHIPS // 2
    d_w_up = jnp.concatenate([_matmul(hn2_t, dhg, F32, "d_w_up_gate", slabs=half_chips),
                              _matmul(hn2_t, dhu, F32, "d_w_up_up", slabs=half_chips)], axis=0)
    dhn2 = _matmul(dhg, w_up_t[:D_FF], F32, "d_hn2_gate")
    dhn2 = _matmul(dhu, w_up_t[D_FF:], F32, "d_hn2_up", residual=dhn2)
    dx2, dx2b, d_norm2 = _rms_bwd(x2, norm2_g, dhn2, dy, "rms2_bwd")
    dmixed = _matmul(dx2b, w_o_t, F32, "d_mixed")
    d_w_o = _matmul(mixed_t, dx2b, F32, "d_w_out")
    dga, dgb, dya, dyb_b = _mix_bwd(dmixed, g_a, g_b, y_a, y_b, "mix_bwd")
    d_w_a, d_w_b = (_matmul(o_t, d, F32, nm).reshape(WIDTH, N_CHIPS, -1).transpose(1, 0, 2)
                    for o_t, d, nm in ((out_a_t, dya, "d_w_branch_a"), (out_b_t, dyb_b, "d_w_branch_b")))
    do_a = _matmul(dya, w_a_t, BF16, "d_out_a")
    do_b = _matmul(dyb_b, w_b_t, BF16, "d_out_b")
    dqb, dkb, dvb = _sb_bwd(vqkv, vqkv, vqkv, out_b_f32, do_b, "stick_attn_bwd", b_offs)
    dqa_n, dka_n, dva, dtab = _ca_bwd(qa, ka, vqkv, do_a, tab, "chunk_attn_bwd")
    d_rel = _bias_table_bwd(dtab, "bias_table_bwd")[:, :N_REL]
    dqa, dka, dgq, dgk = _qknorm_bwd(qk, gq, gk, dqa_n, dka_n, "qknorm_bwd")
    dproj = jnp.concatenate([dqa, dka, dva, dqb, dkb, dvb, dga, dgb], axis=1)
    d_w_in = _matmul(hn_t, dproj, F32, "d_w_in", slabs=N_CHIPS)
    dhn = _matmul(dproj, w_in_t, F32, "d_hn")
    dx, _, d_norm1 = _rms_bwd(xs, norm1_g, dhn, dx2, "rms1_bwd")

    grads_full = {"w_in": d_w_in, "w_branch_a": d_w_a, "w_branch_b": d_w_b, "w_ffn_up": d_w_up,
                  "w_out": d_w_o.reshape(N_CHIPS, -1, D_MODEL), "w_ffn_down": d_w_dn.reshape(N_CHIPS, -1, D_MODEL)}
    gs = [grads_full[n] for n in BIG]
    small_g = _pack_small({"norm1_g": d_norm1, "q_norm_g": dgq.reshape(N_HEADS, HEAD_DIM).sum(0),
                           "k_norm_g": dgk.reshape(N_HEADS, HEAD_DIM).sum(0), "rel_bias": d_rel,
                           "norm2_g": d_norm2, "ffn_conv_b": jnp.concatenate([dcbg, dcbu], axis=1),
                           "ffn_conv_w": jnp.concatenate([dcwg, dcwu], axis=1)}, SMALL_GRADS, SMALL_GRAD_ROWS)
    *sibs, small_all = _exchange_cores(gs, small_g, "exchange_cores")
    chip_parts = []
    for n, g, sib in zip(BIG, gs, sibs):
        rh, cols = sib.shape[1], sib.shape[2]
        mine = lax.dynamic_slice_in_dim(g, ci * rh, rh, axis=1)
        chip_parts.append(_add2(mine.reshape(-1, cols), sib.reshape(-1, cols), "sum_cores_" + n).reshape(sib.shape))
    parts = _exchange_chips(chip_parts, "exchange_chips")
    g_halves = [_sum_leading(with_own(q, lax.dynamic_index_in_dim(p, chip, 0, keepdims=False)), "sum_chips_" + n)
                for n, p, q in zip(BIG, chip_parts, parts)]
    g_others = _share_halves(g_halves, "share_halves")
    grads = {n: jnp.concatenate([jnp.where(ci == 0, mine, other), jnp.where(ci == 0, other, mine)], axis=0)
             for n, mine, other in zip(BIG, g_halves, g_others)}
    small_all = lax.dynamic_update_slice(small_all, small_g[None], (4 * xi + 2 * yi + ci, 0, 0))
    small_sum = _unpack_small(_sum_leading(small_all, "sum_small"), SMALL_GRADS)
    small_sum["ffn_conv_w"] = lax.dynamic_slice_in_dim(small_sum["ffn_conv_w"].reshape(3, 2 * D_FF),
                                                       chip * CONV_W_COLS, CONV_W_COLS, axis=1)

    deltas, new_m, new_v = {}, {}, {}
    for n in BIG:
        deltas[n], new_m[n], new_v[n] = _adamw(w_big[n], grads[n], m_big[n], v_big[n], "adamw_" + n)

    shapes = {"norm1_g": norm1_g.shape, "q_norm_g": q_norm_g.shape, "k_norm_g": k_norm_g.shape,
              "rel_bias": rel_bias.shape, "norm2_g": norm2_g.shape, "ffn_conv_b": ffn_conv_b.shape,
              "ffn_conv_w": ffn_conv_w.shape}
    small_w = {"norm1_g": norm1_g, "q_norm_g": q_norm_g, "k_norm_g": k_norm_g, "rel_bias": rel_bias,
               "norm2_g": norm2_g, "ffn_conv_b": ffn_conv_b, "ffn_conv_w": ffn_conv_w}
    small_m = {"norm1_g": m_norm1_g, "q_norm_g": m_q_norm_g, "k_norm_g": m_k_norm_g, "rel_bias": m_rel_bias,
               "norm2_g": m_norm2_g, "ffn_conv_b": m_ffn_conv_b, "ffn_conv_w": m_ffn_conv_w}
    small_v = {"norm1_g": v_norm1_g, "q_norm_g": v_q_norm_g, "k_norm_g": v_k_norm_g, "rel_bias": v_rel_bias,
               "norm2_g": v_norm2_g, "ffn_conv_b": v_ffn_conv_b, "ffn_conv_w": v_ffn_conv_w}
    ds, ms, vs = _adamw(*(_pack_small(t, SMALL_OWN, SMALL_OWN_ROWS) for t in (small_w, small_sum, small_m, small_v)),
                        "adamw_small")
    small_grads = small_sum
    ds, ms, vs = (_unpack_small(t, SMALL_OWN) for t in (ds, ms, vs))

    order = ("norm1_g", "w_in", "q_norm_g", "k_norm_g", "rel_bias", "w_branch_a", "w_branch_b", "w_out",
             "norm2_g", "w_ffn_up", "ffn_conv_w", "ffn_conv_b", "w_ffn_down")
    outs = [loss, dx[None]]
    for big, small in ((grads, small_grads), (deltas, ds), (new_m, ms), (new_v, vs)):
        for n in order:
            outs.append(big[n][None] if n in big else small[n].reshape(shapes[n]))
    return tuple(outs)
```

```python
import functools

import jax
import jax.numpy as jnp
from jax import lax
from jax.experimental import pallas as pl
from jax.experimental.pallas import tpu as pltpu

F32 = jnp.float32
BF16 = jnp.bfloat16
MESH = pl.DeviceIdType.MESH

D_MODEL = 1024
HEAD_DIM = 64
N_HEADS = 8
WIDTH = N_HEADS * HEAD_DIM
CHUNK = 64
LEFT_CHUNKS = 8
MAX_REL = 128
N_REL = 2 * MAX_REL + 1
D_FF = 2816
EPS = 1e-6
NEG = -1e30

ADAM_LR = 0.001
ADAM_B1 = 0.9
ADAM_B2 = 0.999
ADAM_EPS = 1e-08
ADAM_WD = 0.01
ADAM_STEP = 10

N_CHIPS = 4
N_DEV = 8
LANES = 128
PAIR = 2 * HEAD_DIM
BQ = 256
BAND = LEFT_CHUNKS * CHUNK
KWIN = BAND + BQ
VMEM_LIMIT = 56 * 1024 * 1024
PACK_COLS = 1024

NN = (((1,), (0,)), ((), ()))
NT = (((1,), (1,)), ((), ()))
TN = (((0,), (0,)), ((), ()))


def _cparams(sem=None):
    if sem is None:
        return pltpu.CompilerParams(vmem_limit_bytes=VMEM_LIMIT)
    return pltpu.CompilerParams(dimension_semantics=sem, vmem_limit_bytes=VMEM_LIMIT)


def _pick(n, cands):
    for c in cands:
        if n % c == 0:
            return c
    raise ValueError(f"no block for {n}")


def _dot(a, b, dn):
    return lax.dot_general(a, b, dn, preferred_element_type=F32)


def _sigmoid(x):
    return 0.5 * jnp.tanh(0.5 * x) + 0.5


def _split_bf16(x):
    hi = x.astype(BF16)
    lo = (x - hi.astype(F32)).astype(BF16)
    return hi, lo


MM_RESIDENT_BYTES = 12 * 1024 * 1024
MM_TILE_BYTES = 4 * 1024 * 1024


def _matmul(a, b, out_dtype, name, residual=None, slabs=None):
    (M, K), N = a.shape, b.shape[1]
    out_bytes = jnp.dtype(out_dtype).itemsize
    if slabs is None and N <= D_FF and K * N * 2 <= MM_RESIDENT_BYTES:
        bk, bn = K, N
        bm = next(c for c in (1024, 512, 256, 128)
                  if M % c == 0 and c * K * 2 <= MM_TILE_BYTES and c * N * out_bytes <= MM_TILE_BYTES)
        b_spec = pl.BlockSpec((bk, bn), lambda i, j, k: (0, 0), pipeline_mode=pl.Buffered(1))
    else:
        bk = K if K <= D_FF else _pick(K, (1024, 512))
        bm = 512 if D_MODEL < K <= D_FF else _pick(M, (1024, D_FF // 4, 512, 256, 128))
        bn = N // slabs if slabs else _pick(N, (D_FF // 2, 512, 256, 128))
        b_spec = pl.BlockSpec((bk, bn), lambda i, j, k: (k, j))
    nk = K // bk
    dn = NN
    a_spec = pl.BlockSpec((bm, bk), lambda i, j, k: (i, k))
    if slabs:
        o_spec = pl.BlockSpec((None, bm, bn), lambda i, j, k: (j, i, 0))
        out_shape = jax.ShapeDtypeStruct((slabs, M, bn), out_dtype)
    else:
        o_spec = pl.BlockSpec((bm, bn), lambda i, j, k: (i, j))
        out_shape = jax.ShapeDtypeStruct((M, N), out_dtype)
    has_res = residual is not None

    def body(*refs):
        if has_res:
            a_ref, b_ref, r_ref, o_ref, acc_ref = refs
        else:
            a_ref, b_ref, o_ref, acc_ref = refs
        k = pl.program_id(2)
        part = _dot(a_ref[...], b_ref[...], dn)

        def finish(total):
            if has_res:
                total = total + r_ref[...]
            o_ref[...] = total.astype(out_dtype)

        if nk == 1:
            finish(part)
        else:
            @pl.when(k == 0)
            def _():
                acc_ref[...] = part

            @pl.when(k > 0)
            def _():
                acc_ref[...] += part

            @pl.when(k == nk - 1)
            def _():
                finish(acc_ref[...])

    in_specs = [a_spec, b_spec] + ([o_spec] if has_res else [])
    args = (a, b) + ((residual,) if has_res else ())
    return pl.pallas_call(
        body, name=name,
        grid=(M // bm, N // bn, nk),
        in_specs=in_specs, out_specs=o_spec, out_shape=out_shape,
        scratch_shapes=[pltpu.VMEM((bm, bn) if nk > 1 else (8, LANES), F32)],
        compiler_params=_cparams(("parallel", "parallel", "arbitrary")),
    )(*args)


ROWS = 512


def _row_spec(cols, bm=ROWS):
    return pl.BlockSpec((bm, cols), lambda i: (i, 0))


def _col_spec(rows, bn=ROWS):
    return pl.BlockSpec((rows, bn), lambda i: (0, i))


def _full_spec(shape):
    return pl.BlockSpec(shape, lambda i: (0,) * len(shape))


def _colsum8(t):
    return jnp.sum(t.reshape(t.shape[0] // 8, 8, t.shape[1]), axis=0)


def _rms_fwd(x, g, name):
    S, D = x.shape

    def body(x_ref, g_ref, o_ref, ot_ref):
        xv = x_ref[...]
        r = lax.rsqrt(jnp.mean(xv * xv, axis=-1, keepdims=True) + EPS)
        y = xv * r * g_ref[...]
        o_ref[...] = y.astype(BF16)
        ot_ref[...] = y.T.astype(BF16)

    return pl.pallas_call(
        body, name=name, grid=(S // ROWS,),
        in_specs=[_row_spec(D), _full_spec((1, D))], out_specs=[_row_spec(D), _col_spec(D)],
        out_shape=[jax.ShapeDtypeStruct((S, D), BF16), jax.ShapeDtypeStruct((D, S), BF16)],
        compiler_params=_cparams(("parallel",)),
    )(x, g)


def _rms_bwd(x, g, dy, dres, name):
    S, D = x.shape
    nt = S // ROWS

    def body(x_ref, g_ref, dy_ref, dres_ref, dx_ref, dxb_ref, dg_ref, acc_ref):
        i = pl.program_id(0)
        xv, dyv = x_ref[...], dy_ref[...]
        r = lax.rsqrt(jnp.mean(xv * xv, axis=-1, keepdims=True) + EPS)
        xr = xv * r
        u = dyv * g_ref[...]
        dx = r * u - xr * (r * r) * jnp.mean(xv * u, axis=-1, keepdims=True) + dres_ref[...]
        dx_ref[...] = dx
        dxb_ref[...] = dx.astype(BF16)
        part = _colsum8(dyv * xr)

        @pl.when(i == 0)
        def _():
            acc_ref[...] = part

        @pl.when(i > 0)
        def _():
            acc_ref[...] += part

        @pl.when(i == nt - 1)
        def _():
            dg_ref[...] = jnp.sum(acc_ref[...], axis=0, keepdims=True)

    return pl.pallas_call(
        body, name=name, grid=(nt,),
        in_specs=[_row_spec(D), _full_spec((1, D)), _row_spec(D), _row_spec(D)],
        out_specs=[_row_spec(D), _row_spec(D), _full_spec((1, D))],
        out_shape=[jax.ShapeDtypeStruct((S, D), F32), jax.ShapeDtypeStruct((S, D), BF16),
                   jax.ShapeDtypeStruct((1, D), F32)],
        scratch_shapes=[pltpu.VMEM((8, D), F32)],
        compiler_params=_cparams(("arbitrary",)),
    )(x, g, dy, dres)


def _head_mean(t, blockdiag):
    hi, lo = _split_bf16(t)
    return (_dot(hi, blockdiag, NN) + _dot(lo, blockdiag, NN)) * (1.0 / HEAD_DIM)


def _blockdiag():
    r = lax.broadcasted_iota(jnp.int32, (WIDTH, WIDTH), 0) // HEAD_DIM
    c = lax.broadcasted_iota(jnp.int32, (WIDTH, WIDTH), 1) // HEAD_DIM
    return jnp.where(r == c, 1.0, 0.0).astype(BF16)


def _qknorm_fwd(qk, gq, gk, name):
    S = qk.shape[0]

    def body(qk_ref, gq_ref, gk_ref, q_ref, k_ref):
        bd = _blockdiag()
        for part, g_ref, o_ref, scale in ((0, gq_ref, q_ref, HEAD_DIM ** -0.5), (1, gk_ref, k_ref, 1.0)):
            t = qk_ref[:, part * WIDTH:(part + 1) * WIDTH]
            r = lax.rsqrt(_head_mean(t * t, bd) + EPS)
            o_ref[...] = (t * r * g_ref[...] * scale).astype(BF16)

    return pl.pallas_call(
        body, name=name, grid=(S // ROWS,),
        in_specs=[_row_spec(2 * WIDTH), _full_spec((1, WIDTH)), _full_spec((1, WIDTH))],
        out_specs=[_row_spec(WIDTH), _row_spec(WIDTH)],
        out_shape=[jax.ShapeDtypeStruct((S, WIDTH), BF16)] * 2,
        compiler_params=_cparams(("parallel",)),
    )(qk, gq, gk)


def _qknorm_bwd(qk, gq, gk, dqn, dkn, name):
    S = qk.shape[0]
    nt = S // ROWS

    def body(qk_ref, gq_ref, gk_ref, dqn_ref, dkn_ref, dq_ref, dk_ref, dgq_ref, dgk_ref, accq_ref, acck_ref):
        i = pl.program_id(0)
        bd = _blockdiag()
        for part, g_ref, dn_ref, o_ref, dg_ref, acc_ref, scale in (
                (0, gq_ref, dqn_ref, dq_ref, dgq_ref, accq_ref, HEAD_DIM ** -0.5),
                (1, gk_ref, dkn_ref, dk_ref, dgk_ref, acck_ref, 1.0)):
            t = qk_ref[:, part * WIDTH:(part + 1) * WIDTH]
            dn = dn_ref[...] * scale
            r = lax.rsqrt(_head_mean(t * t, bd) + EPS)
            u = dn * g_ref[...]
            dt = r * u - t * (r * r * r) * _head_mean(t * u, bd)
            o_ref[...] = dt.astype(BF16)
            psum = _colsum8(dn * t * r)

            @pl.when(i == 0)
            def _():
                acc_ref[...] = psum

            @pl.when(i > 0)
            def _():
                acc_ref[...] += psum

            @pl.when(i == nt - 1)
            def _():
                dg_ref[...] = jnp.sum(acc_ref[...], axis=0, keepdims=True)

    return pl.pallas_call(
        body, name=name, grid=(nt,),
        in_specs=[_row_spec(2 * WIDTH), _full_spec((1, WIDTH)), _full_spec((1, WIDTH)),
                  _row_spec(WIDTH), _row_spec(WIDTH)],
        out_specs=[_row_spec(WIDTH), _row_spec(WIDTH), _full_spec((1, WIDTH)), _full_spec((1, WIDTH))],
        out_shape=[jax.ShapeDtypeStruct((S, WIDTH), BF16)] * 2 + [jax.ShapeDtypeStruct((1, WIDTH), F32)] * 2,
        scratch_shapes=[pltpu.VMEM((8, WIDTH), F32)] * 2,
        compiler_params=_cparams(("arbitrary",)),
    )(qk, gq, gk, dqn, dkn)


def _mix_fwd(ga, gb, ya, yb, name):
    S, D = ga.shape

    def body(ga_ref, gb_ref, ya_ref, yb_ref, o_ref, ot_ref):
        m = _sigmoid(ga_ref[...]) * ya_ref[...] + _sigmoid(gb_ref[...]) * yb_ref[...]
        o_ref[...] = m.astype(BF16)
        ot_ref[...] = m.T.astype(BF16)

    return pl.pallas_call(
        body, name=name, grid=(S // ROWS,),
        in_specs=[_row_spec(D)] * 4, out_specs=[_row_spec(D), _col_spec(D)],
        out_shape=[jax.ShapeDtypeStruct((S, D), BF16), jax.ShapeDtypeStruct((D, S), BF16)],
        compiler_params=_cparams(("parallel",)),
    )(ga, gb, ya, yb)


def _mix_bwd(dm, ga, gb, ya, yb, name):
    S, D = ga.shape

    def body(dm_ref, ga_ref, gb_ref, ya_ref, yb_ref, dga_ref, dgb_ref, dya_ref, dyb_ref):
        dmv = dm_ref[...]
        for g_ref, y_ref, dg_ref, dy_ref in ((ga_ref, ya_ref, dga_ref, dya_ref), (gb_ref, yb_ref, dgb_ref, dyb_ref)):
            s = _sigmoid(g_ref[...])
            dy_ref[...] = (dmv * s).astype(BF16)
            dg_ref[...] = (dmv * y_ref[...] * s * (1.0 - s)).astype(BF16)

    return pl.pallas_call(
        body, name=name, grid=(S // ROWS,),
        in_specs=[_row_spec(D)] * 5, out_specs=[_row_spec(D)] * 4,
        out_shape=[jax.ShapeDtypeStruct((S, D), BF16)] * 4,
        compiler_params=_cparams(("parallel",)),
    )(dm, ga, gb, ya, yb)


def _loss_head(y, target, name):
    S, D = y.shape
    nt = S // ROWS

    def body(y_ref, t_ref, dy_ref, dyb_ref, p_ref):
        err = y_ref[...] - t_ref[...]
        dy = err * (1.0 / D)
        dy_ref[...] = dy
        dyb_ref[...] = dy.astype(BF16)
        sq = _colsum8(err * err)
        acc = sq[:, 0:LANES]
        for k in range(1, D // LANES):
            acc = acc + sq[:, k * LANES:(k + 1) * LANES]
        p_ref[...] = acc

    return pl.pallas_call(
        body, name=name, grid=(nt,),
        in_specs=[_row_spec(D)] * 2,
        out_specs=[_row_spec(D), _row_spec(D), pl.BlockSpec((8, LANES), lambda i: (i, 0))],
        out_shape=[jax.ShapeDtypeStruct((S, D), F32), jax.ShapeDtypeStruct((S, D), BF16),
                   jax.ShapeDtypeStruct((nt * 8, LANES), F32)],
        compiler_params=_cparams(("parallel",)),
    )(y, target)


CONV_COLS = D_FF // 2
HALO = 16
CONV_CHUNK = 64


def _aligned(start, multiple):
    return start if isinstance(start, int) else pl.multiple_of(start, multiple)


def _conv_taps(xe, cw, cb):
    taps = (pltpu.roll(xe, 2, 0), pltpu.roll(xe, 1, 0), xe)
    return taps, cw[0:1] * taps[0] + cw[1:2] * taps[1] + cw[2:3] * taps[2] + cb


def _conv_specs(nt):
    hb, nb = ROWS // HALO, D_FF // CONV_COLS
    specs = {}
    for part, off in (("gate", 0), ("up", nb)):
        specs[part] = dict(
            main=pl.BlockSpec((ROWS, CONV_COLS), functools.partial(lambda c, i, off: (i, c + off), off=off)),
            prev=pl.BlockSpec((HALO, CONV_COLS),
                              functools.partial(lambda c, i, off: (jnp.maximum(i * hb - 1, 0), c + off), off=off)),
            nxt=pl.BlockSpec((HALO, CONV_COLS),
                             functools.partial(lambda c, i, off: (jnp.minimum((i + 1) * hb, nt * hb - 1), c + off), off=off)),
            w=pl.BlockSpec((3, CONV_COLS), functools.partial(lambda c, i, off: (0, c + off), off=off)),
            b=pl.BlockSpec((1, CONV_COLS), functools.partial(lambda c, i, off: (0, c + off), off=off)))
    return specs


def _convglu_fwd(hid, cw, cb, name):
    S = hid.shape[0]
    sp = _conv_specs(S // ROWS)

    def body(hg_ref, hgp_ref, hu_ref, hup_ref, cwg_ref, cwu_ref, cbg_ref, cbu_ref, o_ref, ot_ref):
        i = pl.program_id(1)
        keep = (i > 0).astype(F32)

        def conv(h_ref, hp_ref, cw_ref, cb_ref):
            xe = jnp.concatenate([hp_ref[...].astype(F32) * keep, h_ref[...].astype(F32)], axis=0)
            return _conv_taps(xe, cw_ref[...], cb_ref[...])[1][HALO:, :]

        gate = conv(hg_ref, hgp_ref, cwg_ref, cbg_ref)
        up = conv(hu_ref, hup_ref, cwu_ref, cbu_ref)
        act = gate * _sigmoid(gate) * up
        o_ref[...] = act.astype(BF16)
        ot_ref[...] = act.T.astype(BF16)

    g, u = sp["gate"], sp["up"]
    return pl.pallas_call(
        body, name=name, grid=(D_FF // CONV_COLS, S // ROWS),
        in_specs=[g["main"], g["prev"], u["main"], u["prev"], g["w"], u["w"], g["b"], u["b"]],
        out_specs=[g["main"], pl.BlockSpec((CONV_COLS, ROWS), lambda c, i: (c, i))],
        out_shape=[jax.ShapeDtypeStruct((S, D_FF), BF16), jax.ShapeDtypeStruct((D_FF, S), BF16)],
        compiler_params=_cparams(("parallel", "parallel")),
    )(hid, hid, hid, hid, cw, cw, cb, cb)


def _convglu_bwd(hid, dact, cw, cb, name):
    S = hid.shape[0]
    nt = S // ROWS
    sp = _conv_specs(nt)

    n_chunks = ROWS // CONV_CHUNK

    def body(hg_ref, hgp_ref, hgn_ref, hu_ref, hup_ref, hun_ref, da_ref, dan_ref,
             cwg_ref, cwu_ref, cbg_ref, cbu_ref,
             dhg_ref, dhu_ref, dcwg_ref, dcwu_ref, dcbg_ref, dcbu_ref, xg_s, xu_s, da_s):
        i = pl.program_id(1)
        kp = (i > 0).astype(F32)
        kn = (i < nt - 1).astype(F32)
        for x_s, h_ref, hp_ref, hn_ref in ((xg_s, hg_ref, hgp_ref, hgn_ref), (xu_s, hu_ref, hup_ref, hun_ref)):
            x_s[0:HALO, :] = hp_ref[...].astype(F32) * kp
            x_s[HALO:HALO + ROWS, :] = h_ref[...].astype(F32)
            x_s[HALO + ROWS:, :] = hn_ref[...].astype(F32) * kn
        da_s[0:ROWS, :] = da_ref[...].astype(F32)
        da_s[ROWS:, :] = dan_ref[...].astype(F32) * kn

        @pl.when(i == 0)
        def _():
            for ref in (dcwg_ref, dcwu_ref, dcbg_ref, dcbu_ref):
                ref[...] = jnp.zeros_like(ref)

        def lane_group(grp, _):
            lanes = pl.ds(pl.multiple_of(grp * LANES, LANES), LANES)
            cwg, cwu, cbg, cbu = cwg_ref[:, lanes], cwu_ref[:, lanes], cbg_ref[:, lanes], cbu_ref[:, lanes]

            def grads(r0, n):
                rows = pl.ds(_aligned(r0 + HALO - 8, 8), n + 8)
                taps_g, gate = _conv_taps(xg_s[rows, lanes], cwg, cbg)
                taps_u, up = _conv_taps(xu_s[rows, lanes], cwu, cbu)
                gate, up = gate[8:], up[8:]
                da = da_s[pl.ds(_aligned(r0, 8), n), lanes]
                sg = _sigmoid(gate)
                return (da * up * sg * (1.0 + gate * (1.0 - sg)), da * gate * sg,
                        [t[8:] for t in taps_g], [t[8:] for t in taps_u])

            def chunk(step, carry):
                below_g, below_u, accs = carry
                r0 = (n_chunks - 1 - step) * CONV_CHUNK
                dg, du, taps_g, taps_u = grads(r0, CONV_CHUNK)
                new_accs = []
                for d, below, cwv, taps, dh_ref, acc in ((dg, below_g, cwg, taps_g, dhg_ref, accs[0]),
                                                        (du, below_u, cwu, taps_u, dhu_ref, accs[1])):
                    ext = jnp.concatenate([d, below], axis=0)
                    n_ext = CONV_CHUNK + 8
                    dh = (cwv[2:3] * d + cwv[1:2] * pltpu.roll(ext, n_ext - 1, 0)[:CONV_CHUNK]
                          + cwv[0:1] * pltpu.roll(ext, n_ext - 2, 0)[:CONV_CHUNK])
                    dh_ref[pl.ds(_aligned(r0, CONV_CHUNK), CONV_CHUNK), lanes] = dh.astype(BF16)
                    new_accs.append(tuple(a + _colsum8(d * tap) for a, tap in zip(acc[:3], taps))
                                    + (acc[3] + _colsum8(d),))
                return dg[0:8], du[0:8], tuple(new_accs)

            below_g, below_u, _, _ = grads(ROWS, 8)
            zero = jnp.zeros((8, LANES), F32)
            _, _, accs = lax.fori_loop(0, n_chunks, chunk, (below_g, below_u, ((zero,) * 4, (zero,) * 4)))
            for acc, dcw_ref, dcb_ref in ((accs[0], dcwg_ref, dcbg_ref), (accs[1], dcwu_ref, dcbu_ref)):
                for t in range(3):
                    dcw_ref[t:t + 1, lanes] += jnp.sum(acc[t], axis=0, keepdims=True)
                dcb_ref[:, lanes] += jnp.sum(acc[3], axis=0, keepdims=True)
            return 0

        lax.fori_loop(0, CONV_COLS // LANES, lane_group, 0)

    g, u = sp["gate"], sp["up"]
    return pl.pallas_call(
        body, name=name, grid=(D_FF // CONV_COLS, nt),
        in_specs=[g["main"], g["prev"], g["nxt"], u["main"], u["prev"], u["nxt"], g["main"], g["nxt"],
                  g["w"], u["w"], g["b"], u["b"]],
        out_specs=[g["main"], g["main"], g["w"], g["w"], g["b"], g["b"]],
        out_shape=[jax.ShapeDtypeStruct((S, D_FF), BF16)] * 2 + [jax.ShapeDtypeStruct((3, D_FF), F32)] * 2
        + [jax.ShapeDtypeStruct((1, D_FF), F32)] * 2,
        scratch_shapes=[pltpu.VMEM((ROWS + 2 * HALO, CONV_COLS), F32)] * 2 + [pltpu.VMEM((ROWS + HALO, CONV_COLS), F32)],
        compiler_params=_cparams(("parallel", "arbitrary")),
    )(hid, hid, hid, hid, hid, hid, dact, dact, cw, cw, cb, cb)


REL_PAD = 384
DIAG = 1024


def _band_valid():
    qc = lax.broadcasted_iota(jnp.int32, (BQ, KWIN), 0) // CHUNK
    kc = lax.broadcasted_iota(jnp.int32, (BQ, KWIN), 1) // CHUNK - LEFT_CHUNKS
    return (kc <= qc) & (kc >= qc - LEFT_CHUNKS)


def _rel_index(offset):
    return jnp.clip(BAND - offset, -MAX_REL, MAX_REL) + MAX_REL


def _split3(x):
    hi = x.astype(BF16)
    rest = x - hi.astype(F32)
    mid = rest.astype(BF16)
    return hi, mid, (rest - mid.astype(F32)).astype(BF16)


def _bias_table(rel_bias, name):
    def body(rb_ref, o_ref):
        t = lax.broadcasted_iota(jnp.int32, (REL_PAD, DIAG), 0)
        lane = lax.broadcasted_iota(jnp.int32, (REL_PAD, DIAG), 1)
        pick = jnp.where(t == _rel_index(lane - BQ), 1.0, 0.0).astype(BF16)
        base = sum(_dot(piece, pick, NN) for piece in _split3(rb_ref[...]))
        valid = _band_valid()
        for h in range(N_HEADS):
            rows = jnp.broadcast_to(base[h:h + 1], (BQ, DIAG))
            rolled = pltpu.roll(rows, 0, 1, stride=1, stride_axis=0)
            o_ref[h] = jnp.where(valid, rolled[:, BQ:], NEG)

    return pl.pallas_call(
        body, name=name,
        out_shape=jax.ShapeDtypeStruct((N_HEADS, BQ, KWIN), F32),
        compiler_params=_cparams(),
    )(rel_bias)


def _bias_table_bwd(dtab, name):
    def body(d_ref, o_ref, diag_ref):
        r = lax.broadcasted_iota(jnp.int32, (BQ, BQ), 0)
        c = lax.broadcasted_iota(jnp.int32, (BQ, BQ), 1)
        flip = jnp.where(r + c == BQ - 1, 1.0, 0.0).astype(BF16)
        for h in range(N_HEADS):
            flipped = sum(_dot(flip, piece, NN) for piece in _split3(d_ref[h]))
            padded = jnp.concatenate([flipped, jnp.zeros((BQ, DIAG - KWIN), F32)], axis=1)
            rolled = pltpu.roll(padded, DIAG - (BQ - 1), 1, stride=1, stride_axis=0)
            diag_ref[h:h + 1, :] = jnp.sum(rolled, axis=0, keepdims=True)
        lane = lax.broadcasted_iota(jnp.int32, (DIAG, REL_PAD), 0)
        t = lax.broadcasted_iota(jnp.int32, (DIAG, REL_PAD), 1)
        offset = jnp.where(lane < KWIN, lane, lane - DIAG)
        pick = jnp.where(t == _rel_index(offset), 1.0, 0.0).astype(BF16)
        o_ref[...] = sum(_dot(piece, pick, NN) for piece in _split3(diag_ref[...]))

    return pl.pallas_call(
        body, name=name,
        out_shape=jax.ShapeDtypeStruct((N_HEADS, REL_PAD), F32),
        scratch_shapes=[pltpu.VMEM((N_HEADS, DIAG), F32)],
        compiler_params=_cparams(),
    )(dtab)


def _head_masks(heads=2):
    lane = lax.broadcasted_iota(jnp.int32, (1, heads * HEAD_DIM), 1)
    return [lane // HEAD_DIM == h for h in range(heads)]


def _own_lanes(masks, vals):
    out = vals[-1]
    for m, val in zip(masks[-2::-1], vals[-2::-1]):
        out = jnp.where(m, val, out)
    return out


CA_HEADS = 4
CA_LANES = CA_HEADS * HEAD_DIM


def _ca_window_specs(nq, col_off=0):
    return [pl.BlockSpec((BQ, CA_LANES), functools.partial(
        lambda p, i, d: (jnp.clip(i - 2 + d, 0, nq - 1), p + col_off), d=d)) for d in range(3)]


def _softmax_rows(s):
    p = jnp.exp(s - jnp.max(s, axis=-1, keepdims=True))
    return p, jnp.sum(p, axis=-1, keepdims=True)


def _ca_scores(qm, kc, tab_h, i):
    col = lax.broadcasted_iota(jnp.int32, (1, KWIN), 1)
    in_seq = col + (i - 2) * BQ >= 0
    return jnp.where(in_seq, _dot(qm, kc, NT) + tab_h, NEG)


def _ca_fwd(qn, kn, v, tab, name, v_off=0, gather=None):
    S = qn.shape[0]
    nq = S // BQ
    groups = WIDTH // CA_LANES
    qspec = pl.BlockSpec((BQ, CA_LANES), lambda p, i: (i, p))
    tspec = pl.BlockSpec((CA_HEADS, BQ, KWIN), lambda p, i: (p, 0, 0))
    n_side = len(gather.arrays) if gather else 0

    def body(*refs):
        q_ref, k0, k1, k2, v0, v1, v2, tab_ref = refs[:8]
        o_ref, ot_ref = refs[8 + n_side:10 + n_side]
        p, i = pl.program_id(0), pl.program_id(1)
        if gather:
            start, forward, finish = gather.steps(refs[8:8 + n_side], refs[10 + n_side:10 + 2 * n_side],
                                                  refs[10 + 2 * n_side:])
            pl.when((p == 0) & (i == 0))(start)
            pl.when((p == groups - 1) & (i == nq // 2))(forward)
        kc = jnp.concatenate([k0[...], k1[...], k2[...]], axis=0)
        vc = jnp.concatenate([v0[...], v1[...], v2[...]], axis=0)
        qv = q_ref[...]
        masks = _head_masks(CA_HEADS)
        heads = range(CA_HEADS)
        s = [_ca_scores(jnp.where(masks[h], qv, 0), kc, tab_ref[h], i) for h in heads]
        soft = [_softmax_rows(s[h]) for h in heads]
        o = [_dot(soft[h][0].astype(BF16), vc, NN) / soft[h][1] for h in heads]
        out = _own_lanes(masks, o)
        o_ref[...] = out.astype(BF16)
        ot_ref[...] = out.T.astype(BF16)
        if gather:
            pl.when((p == groups - 1) & (i == nq - 1))(finish)

    side = gather.arrays if gather else []
    return pl.pallas_call(
        body, name=name, grid=(groups, nq),
        in_specs=[qspec] + _ca_window_specs(nq) + _ca_window_specs(nq, v_off) + [tspec] + [ANY] * n_side,
        out_specs=[qspec, pl.BlockSpec((CA_LANES, BQ), lambda p, i: (p, i))] + [ANY] * n_side,
        out_shape=[jax.ShapeDtypeStruct((S, WIDTH), BF16), jax.ShapeDtypeStruct((WIDTH, S), BF16)]
        + (gather.out_shape if gather else []),
        scratch_shapes=gather.scratch if gather else [],
        compiler_params=_cparams(("arbitrary", "arbitrary") if gather else ("parallel", "parallel")),
    )(qn, kn, kn, kn, v, v, v, tab, *side)


def _ca_bwd(qn, kn, v, do, tab, name, v_off=0):
    S = qn.shape[0]
    nq = S // BQ
    qspec = pl.BlockSpec((BQ, CA_LANES), lambda p, i: (jnp.minimum(i, nq - 1), p))
    kout = pl.BlockSpec((BQ, CA_LANES), lambda p, i: (jnp.clip(i - 2, 0, nq - 1), p))
    tspec = pl.BlockSpec((CA_HEADS, BQ, KWIN), lambda p, i: (p, 0, 0))

    def body(q_ref, do_ref, k0, k1, k2, v0, v1, v2, tab_ref,
             dq_ref, dk_ref, dv_ref, dtab_ref, dk_acc, dv_acc):
        i = pl.program_id(1)

        @pl.when(i == 0)
        def _():
            dk_acc[...] = jnp.zeros_like(dk_acc)
            dv_acc[...] = jnp.zeros_like(dv_acc)
            dtab_ref[...] = jnp.zeros_like(dtab_ref)

        @pl.when(i < nq)
        def _():
            kc = jnp.concatenate([k0[...], k1[...], k2[...]], axis=0)
            vc = jnp.concatenate([v0[...], v1[...], v2[...]], axis=0)
            qv, dov = q_ref[...], do_ref[...]
            masks = _head_masks(CA_HEADS)
            heads = range(CA_HEADS)
            qm = [jnp.where(masks[h], qv, 0) for h in heads]
            dom = [jnp.where(masks[h], dov, 0) for h in heads]
            s = [_ca_scores(qm[h], kc, tab_ref[h], i) for h in heads]
            dp = [_dot(dom[h], vc, NT) for h in heads]
            soft = [_softmax_rows(s[h]) for h in heads]
            p = [soft[h][0] / soft[h][1] for h in heads]
            ds = [p[h] * (dp[h] - jnp.sum(p[h] * dp[h], axis=-1, keepdims=True)) for h in heads]
            for h in heads:
                dtab_ref[h] += ds[h]
            dsb = [ds[h].astype(BF16) for h in heads]
            pb = [p[h].astype(BF16) for h in heads]
            dq = [_dot(dsb[h], kc, NN) for h in heads]
            dq_ref[...] = _own_lanes(masks, dq)
            dkc = sum(_dot(dsb[h], qm[h], TN) for h in heads)
            dvc = sum(_dot(pb[h], dom[h], TN) for h in heads)
            for d in range(3):
                slot = (i + 1 + d) % 3
                dk_acc[slot] += dkc[d * BQ:(d + 1) * BQ]
                dv_acc[slot] += dvc[d * BQ:(d + 1) * BQ]

        @pl.when(i >= 2)
        def _():
            slot = (i + 1) % 3
            dk_ref[...] = dk_acc[slot]
            dv_ref[...] = dv_acc[slot].astype(BF16)
            dk_acc[slot] = jnp.zeros((BQ, CA_LANES), F32)
            dv_acc[slot] = jnp.zeros((BQ, CA_LANES), F32)

    return pl.pallas_call(
        body, name=name, grid=(WIDTH // CA_LANES, nq + 2),
        in_specs=[qspec, qspec] + _ca_window_specs(nq) + _ca_window_specs(nq, v_off) + [tspec],
        out_specs=[qspec, kout, kout, tspec],
        out_shape=[jax.ShapeDtypeStruct((S, WIDTH), F32), jax.ShapeDtypeStruct((S, WIDTH), F32),
                   jax.ShapeDtypeStruct((S, WIDTH), BF16), jax.ShapeDtypeStruct((N_HEADS, BQ, KWIN), F32)],
        scratch_shapes=[pltpu.VMEM((3, BQ, CA_LANES), F32)] * 2,
        compiler_params=_cparams(("parallel", "arbitrary")),
    )(qn, do, kn, kn, kn, v, v, v, tab)


def _sb_consts():
    r = lax.broadcasted_iota(jnp.int32, (BQ, BQ), 0)
    c = lax.broadcasted_iota(jnp.int32, (BQ, BQ), 1)
    from_s = jnp.where(r >= c, 1.0, 0.0).astype(BF16)
    causal = c < r
    return from_s, causal


def _suffix_sum(t, from_s):
    hi, lo = _split_bf16(t)
    return _dot(hi, from_s, NN) + _dot(lo, from_s, NN)


def _neg_abs(x):
    bits = lax.bitcast_convert_type(x, jnp.uint32) | jnp.uint32(0x80000000)
    return lax.bitcast_convert_type(bits, F32)


def _sb_log_keep(zn):
    return jnp.minimum(zn, 0.0) - jnp.log(1.0 + jnp.exp(_neg_abs(zn)))


SB_DEAD = 105.0


SB_QB = 2


def _sb_walk(ip, tiles, keep_ref):
    i0 = SB_QB * ip

    @pl.when(ip == 0)
    def _():
        tiles([(0, [0], [True]), (1, [1, 0], [True, False])])

    @pl.when(ip > 0)
    def _():
        tiles([(a, [i0 + a, i0 + a - 1], [True, False]) for a in range(SB_QB)])

    for a in range(SB_QB):
        def alive(a=a):
            return (jnp.max(keep_ref[2 * a:2 * a + 2]) > -SB_DEAD).astype(jnp.int32)

        def step(state, a=a, alive=alive):
            j, _ = state
            tiles([(a, [j], [False])])
            return j - 1, alive()

        lax.while_loop(lambda state: (state[0] >= 0) & (state[1] > 0), step, (i0 + a - 2, alive()))


def _sb_rows(j):
    return pl.ds(pl.multiple_of(j * BQ, BQ), BQ)


def _sb_chains(groups):
    chains = [(a, n, h) for a, js, _ in groups for n in range(len(js)) for h in range(2)]
    block = {(a, n): j for a, js, _ in groups for n, j in enumerate(js)}
    masked = [(a, n, h) for a, _, diags in groups for n, d in enumerate(diags) if d for h in range(2)]
    return chains, block, masked


def _sb_running(ref, vals, groups):
    before_chain = {}
    for a, js, _ in groups:
        for h in range(2):
            run = ref[2 * a + h]
            for n in range(len(js)):
                before_chain[(a, n, h)] = run
                run = run + jnp.sum(vals[(a, n, h)], axis=-1, keepdims=True)
            ref[2 * a + h] = run
    return before_chain


def _sb_specs(S, offs):
    def qspec(off=0):
        return pl.BlockSpec((SB_QB * BQ, PAIR), lambda p, i: (i, p + off))

    def kspec(off=0):
        return pl.BlockSpec((S, PAIR), lambda p, i: (0, p + off), pipeline_mode=pl.Buffered(1))

    return qspec, kspec, [qspec(offs[0]), kspec(offs[1]), kspec(offs[2])]


def _sb_fwd(q, k, v, name, offs=(0, 0, 0)):
    S = q.shape[0]
    steps = S // (SB_QB * BQ)
    qspec, _, qkv_specs = _sb_specs(S, offs)

    def body(q_ref, k_ref, v_ref, o_ref, of_ref, ot_ref, carry_ref, acc_ref):
        ip = pl.program_id(1)
        from_s, causal = _sb_consts()
        masks = _head_masks()
        qn = q_ref[...] * -(HEAD_DIM ** -0.5)
        qms = {(a, h): jnp.where(masks[h], qn[a * BQ:(a + 1) * BQ], 0) for a in range(SB_QB) for h in range(2)}
        carry_ref[...] = jnp.zeros_like(carry_ref)
        acc_ref[...] = jnp.zeros_like(acc_ref)

        def tiles(groups):
            chains, block, masked = _sb_chains(groups)
            kbs = {an: k_ref[_sb_rows(j), :] for an, j in block.items()}
            vbs = {an: v_ref[_sb_rows(j), :] for an, j in block.items()}
            zn = {c: _dot(qms[(c[0], c[2])], kbs[c[:2]], NT) for c in chains}
            log_keep = {c: _sb_log_keep(zn[c]) for c in chains}
            for c in masked:
                log_keep[c] = jnp.where(causal, log_keep[c], 0.0)
            split = {c: _split_bf16(log_keep[c]) for c in chains}
            carry = _sb_running(carry_ref, log_keep, groups)
            suffix = {c: _dot(split[c][0], from_s, NN) + _dot(split[c][1], from_s, NN) for c in chains}
            w = {c: jnp.exp(carry[c] + suffix[c] - zn[c]) for c in chains}
            for c in masked:
                w[c] = jnp.where(causal, w[c], 0.0)
            for c in chains:
                acc_ref[2 * c[0] + c[2]] += _dot(w[c].astype(BF16), vbs[c[:2]], NN)

        _sb_walk(ip, tiles, carry_ref)
        for a in range(SB_QB):
            out = jnp.where(masks[0], acc_ref[2 * a], acc_ref[2 * a + 1])
            o_ref[a * BQ:(a + 1) * BQ, :] = out.astype(BF16)
            of_ref[a * BQ:(a + 1) * BQ, :] = out
            ot_ref[:, a * BQ:(a + 1) * BQ] = out.T.astype(BF16)

    return pl.pallas_call(
        body, name=name, grid=(WIDTH // PAIR, steps),
        in_specs=qkv_specs, out_specs=[qspec(), qspec(), pl.BlockSpec((PAIR, SB_QB * BQ), lambda p, i: (p, i))],
        out_shape=[jax.ShapeDtypeStruct((S, WIDTH), BF16), jax.ShapeDtypeStruct((S, WIDTH), F32),
                   jax.ShapeDtypeStruct((WIDTH, S), BF16)],
        scratch_shapes=[pltpu.VMEM((2 * SB_QB, BQ, 1), F32), pltpu.VMEM((2 * SB_QB, BQ, PAIR), F32)],
        compiler_params=_cparams(("parallel", "arbitrary")),
    )(q, k, v)


def _sb_bwd(q, k, v, o, do, name, offs=(0, 0, 0), exchange=None):
    S = q.shape[0]
    steps = S // (SB_QB * BQ)
    pairs = WIDTH // PAIR
    qspec, kspec, qkv_specs = _sb_specs(S, offs)
    n_side = len(exchange.arrays) if exchange else 0

    def body(*refs):
        q_ref, o_ref, do_ref, k_ref, v_ref = refs[:5]
        dq_ref, dk_ref, dv_ref = refs[5 + n_side:8 + n_side]
        dk_acc, dv_acc, keep_ref, gsum_ref, dq_acc = refs[8 + 2 * n_side:13 + 2 * n_side]
        ip = pl.program_id(1)
        if exchange:
            start, finish = exchange.steps(refs[5:5 + n_side], refs[8 + n_side:8 + 2 * n_side], refs[13 + 2 * n_side:])
            pl.when((pl.program_id(0) == 0) & (ip == 0))(start)

        @pl.when(ip == 0)
        def _():
            dk_acc[...] = jnp.zeros_like(dk_acc)
            dv_acc[...] = jnp.zeros_like(dv_acc)

        from_s, causal = _sb_consts()
        masks = _head_masks()
        qn, dov = q_ref[...] * -(HEAD_DIM ** -0.5), do_ref[...]
        od = o_ref[...] * dov.astype(F32)
        lanes = [(a, h) for a in range(SB_QB) for h in range(2)]
        rows_of = {a: slice(a * BQ, (a + 1) * BQ) for a in range(SB_QB)}
        qms = {(a, h): jnp.where(masks[h], qn[rows_of[a]], 0) for a, h in lanes}
        doms = {(a, h): jnp.where(masks[h], dov[rows_of[a]], 0) for a, h in lanes}
        totals = {(a, h): jnp.sum(jnp.where(masks[h], od[rows_of[a]], 0.0), axis=-1, keepdims=True)
                  for a, h in lanes}
        for ref in (keep_ref, gsum_ref, dq_acc):
            ref[...] = jnp.zeros_like(ref)

        def tiles(groups):
            chains, block, masked = _sb_chains(groups)
            kbs = {an: k_ref[_sb_rows(j), :] for an, j in block.items()}
            vbs = {an: v_ref[_sb_rows(j), :] for an, j in block.items()}
            zn = {c: _dot(qms[(c[0], c[2])], kbs[c[:2]], NT) for c in chains}
            dw = {c: _dot(doms[(c[0], c[2])], vbs[c[:2]], NT) for c in chains}
            log_keep = {c: _sb_log_keep(zn[c]) for c in chains}
            for c in masked:
                log_keep[c] = jnp.where(causal, log_keep[c], 0.0)
            split = {c: _split_bf16(log_keep[c]) for c in chains}
            kept = _sb_running(keep_ref, log_keep, groups)
            suffix = {c: _dot(split[c][0], from_s, NN) + _dot(split[c][1], from_s, NN) for c in chains}
            w = {c: jnp.exp(kept[c] + suffix[c] - zn[c]) for c in chains}
            for c in masked:
                w[c] = jnp.where(causal, w[c], 0.0)
            wb = {c: w[c].astype(BF16) for c in chains}
            g = {c: wb[c].astype(F32) * dw[c] for c in chains}
            gsplit = {c: _split_bf16(g[c]) for c in chains}
            gsum = _sb_running(gsum_ref, g, groups)
            gsuffix = {c: _dot(gsplit[c][0], from_s, NN) + _dot(gsplit[c][1], from_s, NN) for c in chains}
            dzb = {}
            for c in chains:
                before = totals[(c[0], c[2])] - (gsum[c] + gsuffix[c])
                dz = (g[c] + before) * jnp.exp(log_keep[c]) - before
                if c in masked:
                    dz = jnp.where(causal, dz, 0.0)
                dzb[c] = dz.astype(BF16)
            for c in chains:
                rows = _sb_rows(block[c[:2]])
                dq_acc[2 * c[0] + c[2]] += _dot(dzb[c], kbs[c[:2]], NN)
                dk_acc[rows, :] -= _dot(dzb[c], qms[(c[0], c[2])], TN)
                dv_acc[rows, :] += _dot(wb[c], doms[(c[0], c[2])], TN)

        _sb_walk(ip, tiles, keep_ref)
        for a in range(SB_QB):
            dq = jnp.where(masks[0], dq_acc[2 * a], dq_acc[2 * a + 1])
            dq_ref[a * BQ:(a + 1) * BQ, :] = (dq * HEAD_DIM ** -0.5).astype(BF16)

        @pl.when(ip == steps - 1)
        def _():
            dk_ref[...] = dk_acc[...].astype(BF16)
            dv_ref[...] = dv_acc[...].astype(BF16)

        if exchange:
            pl.when((pl.program_id(0) == pairs - 1) & (ip == steps - 1))(finish)

    side = exchange.arrays if exchange else []
    return pl.pallas_call(
        body, name=name, grid=(pairs, steps),
        in_specs=[qkv_specs[0], qspec(), qspec(), qkv_specs[1], qkv_specs[2]] + [ANY] * n_side,
        out_specs=[qspec(), kspec(), kspec()] + [ANY] * n_side,
        out_shape=[jax.ShapeDtypeStruct((S, WIDTH), BF16)] * 3 + (exchange.out_shape if exchange else []),
        scratch_shapes=[pltpu.VMEM((S, PAIR), F32)] * 2 + [pltpu.VMEM((2 * SB_QB, BQ, 1), F32)] * 2
        + [pltpu.VMEM((2 * SB_QB, BQ, PAIR), F32)] + (exchange.scratch if exchange else []),
        compiler_params=_cparams(("arbitrary", "arbitrary") if exchange else ("parallel", "arbitrary")),
    )(q, o, do, k, v, *side)


ANY = pl.BlockSpec(memory_space=pl.ANY)


def _place():
    return lax.axis_index("x"), lax.axis_index("y"), lax.axis_index("c")


def _other_chips(x, y):
    return [(2 * px + py, (px, py)) for px, py in ((1 - x, y), (x, 1 - y), (1 - x, 1 - y))]


def _remote(src, dst, sems, k, to):
    return pltpu.make_async_remote_copy(src_ref=src, dst_ref=dst, send_sem=sems[0].at[k], recv_sem=sems[1].at[k],
                                        device_id=to, device_id_type=MESH)


class _Gather:
    def __init__(self, ws, extras=()):
        self.n, self.m = len(ws), len(extras)
        self.arrays = list(ws) + list(extras)
        self.n_copies = 6 * self.n + 3 * self.m
        self.out_shape = [jax.ShapeDtypeStruct((N_CHIPS,) + a.shape, a.dtype) for a in self.arrays]
        self.scratch = [pltpu.SemaphoreType.DMA((self.n_copies,)), pltpu.SemaphoreType.DMA((self.n_copies,))]

    def steps(self, in_refs, out_refs, sems):
        n = self.n
        x, y, c = _place()
        me = 2 * x + y
        chips = _other_chips(x, y)
        sibling = (x, y, 1 - c)

        def halves(ref):
            rh = ref.shape[-2] // 2
            return pl.ds(c * rh, rh), pl.ds((1 - c) * rh, rh)

        def first():
            cps = [_remote(w_ref.at[halves(w_ref)[0]], o_ref.at[me, halves(w_ref)[0]], sems, 6 * a + k, (*xy, c))
                   for a, (w_ref, o_ref) in enumerate(zip(in_refs[:n], out_refs[:n])) for k, (_, xy) in enumerate(chips)]
            return cps + [_remote(e_ref, eo_ref.at[me], sems, 6 * n + 3 * b + k, (*xy, c))
                          for b, (e_ref, eo_ref) in enumerate(zip(in_refs[n:], out_refs[n:]))
                          for k, (_, xy) in enumerate(chips)]

        def passed():
            return [_remote(o_ref.at[chip, halves(o_ref)[0]], o_ref.at[chip, halves(o_ref)[0]], sems, 6 * a + 3 + k, sibling)
                    for a, o_ref in enumerate(out_refs[:n]) for k, (chip, _) in enumerate(chips)]

        def start():
            for cp in first():
                cp.start()

        def forward():
            for a, o_ref in enumerate(out_refs[:n]):
                for k, (chip, xy) in enumerate(chips):
                    landed = o_ref.at[chip, halves(o_ref)[0]]
                    _remote(landed, landed, sems, 6 * a + k, (*xy, c)).wait_recv()
            for cp in passed():
                cp.start()

        def finish():
            for a, o_ref in enumerate(out_refs[:n]):
                for k, (chip, _) in enumerate(chips):
                    landed = o_ref.at[chip, halves(o_ref)[1]]
                    _remote(landed, landed, sems, 6 * a + 3 + k, sibling).wait_recv()
            for b, (e_ref, eo_ref) in enumerate(zip(in_refs[n:], out_refs[n:])):
                for k, (chip, xy) in enumerate(chips):
                    _remote(e_ref, eo_ref.at[chip], sems, 6 * n + 3 * b + k, (*xy, c)).wait_recv()
            for cp in first() + passed():
                cp.wait_send()

        return start, forward, finish


def _gather_weights(ws, name):
    plan = _Gather(ws)
    n = len(plan.arrays)

    def body(*refs):
        for step in plan.steps(refs[:n], refs[n:2 * n], refs[2 * n:]):
            step()

    return pl.pallas_call(
        body, name=name, in_specs=[ANY] * n, out_specs=[ANY] * n, out_shape=plan.out_shape,
        scratch_shapes=plan.scratch,
    )(*plan.arrays)


class _ChipExchange:
    def __init__(self, ps):
        self.arrays = list(ps)
        self.out_shape = [jax.ShapeDtypeStruct(p.shape, p.dtype) for p in ps]
        self.scratch = [pltpu.SemaphoreType.DMA((3 * len(ps),)), pltpu.SemaphoreType.DMA((3 * len(ps),))]

    def steps(self, p_refs, out_refs, sems):
        x, y, c = _place()
        me = 2 * x + y
        chips = _other_chips(x, y)

        def copies():
            return [_remote(p_ref.at[chip], o_ref.at[me], sems, 3 * a + k, (*xy, c))
                    for a, (p_ref, o_ref) in enumerate(zip(p_refs, out_refs)) for k, (chip, xy) in enumerate(chips)]

        def start():
            for cp in copies():
                cp.start()

        def finish():
            for a, (p_ref, o_ref) in enumerate(zip(p_refs, out_refs)):
                for k, (chip, xy) in enumerate(chips):
                    _remote(p_ref.at[chip], o_ref.at[chip], sems, 3 * a + k, (*xy, c)).wait_recv()
            for cp in copies():
                cp.wait_send()

        return start, finish


def _exchange_cores(gs, name, small=None):
    n = len(gs)
    m = 0 if small is None else 1

    def body(*refs):
        g_refs, sib_refs = refs[:n], refs[n + m:2 * n + m]
        sems = refs[2 * (n + m):]
        x, y, c = _place()
        me = 4 * x + 2 * y + c
        copies = []
        for a, (g_ref, sib_ref) in enumerate(zip(g_refs, sib_refs)):
            rh = g_ref.shape[1] // 2
            copies.append(_remote(g_ref.at[:, pl.ds((1 - c) * rh, rh), :], sib_ref, sems, a, (x, y, 1 - c)))
        if m:
            small_ref, all_ref = refs[n], refs[2 * n + m]
            k = n
            for fx in (0, 1):
                for fy in (0, 1):
                    for fc in (0, 1):
                        if fx or fy or fc:
                            to = (1 - x if fx else x, 1 - y if fy else y, 1 - c if fc else c)
                            copies.append(_remote(small_ref, all_ref.at[me], sems, k, to))
                            k += 1
        for cp in copies:
            cp.start()
        for cp in copies:
            cp.wait_recv()
        for cp in copies:
            cp.wait_send()

    n_copies = n + m * (N_DEV - 1)
    args = list(gs) + ([small] if m else [])
    return pl.pallas_call(
        body, name=name, in_specs=[ANY] * (n + m), out_specs=[ANY] * (n + m),
        out_shape=[jax.ShapeDtypeStruct((N_CHIPS, g.shape[1] // 2, g.shape[2]), F32) for g in gs]
        + ([jax.ShapeDtypeStruct((N_DEV,) + small.shape, F32)] if m else []),
        scratch_shapes=[pltpu.SemaphoreType.DMA((n_copies,)), pltpu.SemaphoreType.DMA((n_copies,))],
    )(*args)


def _exchange_chips(ps, name):
    plan = _ChipExchange(ps)
    n = len(ps)

    def body(*refs):
        for step in plan.steps(refs[:n], refs[n:2 * n], refs[2 * n:]):
            step()

    return pl.pallas_call(
        body, name=name, in_specs=[ANY] * n, out_specs=[ANY] * n, out_shape=plan.out_shape,
        scratch_shapes=plan.scratch,
    )(*ps)


def _share_halves(ghs, name):
    n = len(ghs)

    def body(*refs):
        gh_refs, out_refs, sems = refs[:n], refs[n:2 * n], refs[2 * n:]
        x, y, c = _place()
        copies = [_remote(gh_ref, o_ref, sems, a, (x, y, 1 - c)) for a, (gh_ref, o_ref) in enumerate(zip(gh_refs, out_refs))]
        for cp in copies:
            cp.start()
        for cp in copies:
            cp.wait_recv()
        for cp in copies:
            cp.wait_send()

    return pl.pallas_call(
        body, name=name, in_specs=[ANY] * n, out_specs=[ANY] * n,
        out_shape=[jax.ShapeDtypeStruct(g.shape, g.dtype) for g in ghs],
        scratch_shapes=[pltpu.SemaphoreType.DMA((n,)), pltpu.SemaphoreType.DMA((n,))],
    )(*ghs)


EW_BLOCK_BYTES = 2 * 1024 * 1024


def _row_block(rows, cols, mult=8):
    fits = [b for b in range(mult, rows + 1, mult) if rows % b == 0 and b * cols * 4 <= EW_BLOCK_BYTES]
    return max(fits) if fits else mult


def _add2(a, b, name):
    R, C = a.shape
    rows = _row_block(R, C, mult=16)
    spec = pl.BlockSpec((rows, C), lambda i: (i, 0))

    def body(a_ref, b_ref, o_ref):
        o_ref[...] = (a_ref[...] + b_ref[...]).astype(BF16)

    return pl.pallas_call(
        body, name=name, grid=(R // rows,), in_specs=[spec, spec], out_specs=spec,
        out_shape=jax.ShapeDtypeStruct(a.shape, BF16),
        compiler_params=_cparams(("parallel",)),
    )(a, b)


def _sum_leading(a, name):
    n, R, C = a.shape
    rows = _row_block(R, n * C, mult=16 if a.dtype == BF16 else 8)

    def body(a_ref, o_ref):
        acc = a_ref[0].astype(F32)
        for j in range(1, n):
            acc = acc + a_ref[j].astype(F32)
        o_ref[...] = acc

    return pl.pallas_call(
        body, name=name, grid=(R // rows,),
        in_specs=[pl.BlockSpec((n, rows, C), lambda i: (0, i, 0))],
        out_specs=pl.BlockSpec((rows, C), lambda i: (i, 0)),
        out_shape=jax.ShapeDtypeStruct((R, C), F32),
        compiler_params=_cparams(("parallel",)),
    )(a)


def _adamw(w, g, m, v, name):
    R, C = w.shape
    rows = _row_block(R, C)
    spec = pl.BlockSpec((rows, C), lambda i: (i, 0))

    def body(w_ref, g_ref, m_ref, v_ref, d_ref, mo_ref, vo_ref):
        gv = g_ref[...]
        mn = ADAM_B1 * m_ref[...] + (1.0 - ADAM_B1) * gv
        vn = ADAM_B2 * v_ref[...] + (1.0 - ADAM_B2) * (gv * gv)
        m_hat = mn / (1.0 - ADAM_B1 ** ADAM_STEP)
        v_hat = vn / (1.0 - ADAM_B2 ** ADAM_STEP)
        d_ref[...] = -ADAM_LR * (m_hat / (jnp.sqrt(v_hat) + ADAM_EPS) + ADAM_WD * w_ref[...])
        mo_ref[...] = mn
        vo_ref[...] = vn

    return pl.pallas_call(
        body, name=name, grid=(R // rows,), in_specs=[spec] * 4, out_specs=[spec] * 3,
        out_shape=[jax.ShapeDtypeStruct((R, C), F32)] * 3,
        compiler_params=_cparams(("parallel",)),
    )(w, g, m, v)


BIG = ("w_in", "w_branch_a", "w_branch_b", "w_out", "w_ffn_up", "w_ffn_down")
COL_SHARDED = {"w_in": True, "w_branch_a": True, "w_branch_b": True, "w_out": False, "w_ffn_up": True,
               "w_ffn_down": False}
CONV_W_COLS = 2 * D_FF // N_CHIPS
SMALL_REPLICATED = (("norm1_g", D_MODEL), ("q_norm_g", HEAD_DIM), ("k_norm_g", HEAD_DIM),
                    ("rel_bias", N_HEADS * N_REL), ("norm2_g", D_MODEL), ("ffn_conv_b", 2 * D_FF))
SMALL_GRADS = SMALL_REPLICATED + (("ffn_conv_w", 3 * 2 * D_FF),)
SMALL_OWN = SMALL_REPLICATED + (("ffn_conv_w", 3 * CONV_W_COLS),)
SMALL_GRAD_ROWS = 32
SMALL_OWN_ROWS = 16


def _whole(name, stacked):
    return jnp.concatenate(list(stacked), axis=1) if COL_SHARDED[name] else stacked.reshape(-1, stacked.shape[2])


def _pack_small(vals, sizes, rows):
    flat = jnp.concatenate([vals[n].reshape(-1) for n, _ in sizes])
    return jnp.pad(flat, (0, rows * PACK_COLS - flat.shape[0])).reshape(rows, PACK_COLS)


def _unpack_small(packed, sizes):
    flat, out, o = packed.reshape(-1), {}, 0
    for n, sz in sizes:
        out[n] = flat[o:o + sz]
        o += sz
    return out


def kernel(x, norm1_g, w_in, q_norm_g, k_norm_g, rel_bias, w_branch_a, w_branch_b, w_out, norm2_g, w_ffn_up, ffn_conv_w, ffn_conv_b, w_ffn_down, loss_target, m_norm1_g, m_w_in, m_q_norm_g, m_k_norm_g, m_rel_bias, m_w_branch_a, m_w_branch_b, m_w_out, m_norm2_g, m_w_ffn_up, m_ffn_conv_w, m_ffn_conv_b, m_w_ffn_down, v_norm1_g, v_w_in, v_q_norm_g, v_k_norm_g, v_rel_bias, v_w_branch_a, v_w_branch_b, v_w_out, v_norm2_g, v_w_ffn_up, v_ffn_conv_w, v_ffn_conv_b, v_w_ffn_down):
    w_big = {"w_in": w_in[0], "w_branch_a": w_branch_a[0], "w_branch_b": w_branch_b[0], "w_out": w_out[0],
             "w_ffn_up": w_ffn_up[0], "w_ffn_down": w_ffn_down[0]}
    m_big = {"w_in": m_w_in[0], "w_branch_a": m_w_branch_a[0], "w_branch_b": m_w_branch_b[0], "w_out": m_w_out[0],
             "w_ffn_up": m_w_ffn_up[0], "w_ffn_down": m_w_ffn_down[0]}
    v_big = {"w_in": v_w_in[0], "w_branch_a": v_w_branch_a[0], "w_branch_b": v_w_branch_b[0], "w_out": v_w_out[0],
             "w_ffn_up": v_w_ffn_up[0], "w_ffn_down": v_w_ffn_down[0]}
    xs, tgt = x[0], loss_target[0]

    xi, yi, ci = _place()
    chip = 2 * xi + yi

    def with_own(stacked, own):
        return lax.dynamic_update_slice(stacked, own[None], (chip,) + (0,) * own.ndim)

    shards_bf = {n: w_big[n].astype(BF16) for n in BIG}
    conv_own = jnp.pad(ffn_conv_w[0], ((0, 8 - ffn_conv_w.shape[1]), (0, 0)))
    later = [n for n in BIG if n != "w_in"]
    (w_in_g,) = _gather_weights([shards_bf["w_in"]], "gather_w_in")
    w_in_f = _whole("w_in", with_own(w_in_g, shards_bf["w_in"]))
    w_in_t = w_in_f.T

    hn, hn_t = _rms_fwd(xs, norm1_g, "rms1")
    qk = _matmul(hn, w_in_f[:, :2 * WIDTH], F32, "proj_qk")
    vqkv = _matmul(hn, w_in_f[:, 2 * WIDTH:6 * WIDTH], BF16, "proj_vqkv")
    g_a = _matmul(hn, w_in_f[:, 6 * WIDTH:6 * WIDTH + D_MODEL], F32, "proj_gate_a")
    g_b = _matmul(hn, w_in_f[:, 6 * WIDTH + D_MODEL:], F32, "proj_gate_b")
    gq = jnp.tile(q_norm_g, (1, N_HEADS))
    gk = jnp.tile(k_norm_g, (1, N_HEADS))
    qa, ka = _qknorm_fwd(qk, gq, gk, "qknorm")
    per = WIDTH // PAIR
    b_offs = (per, 2 * per, 3 * per)
    tab = _bias_table(jnp.pad(rel_bias[0], ((0, 0), (0, REL_PAD - N_REL))), "bias_table")
    out_a, out_a_t, *gathered = _ca_fwd(qa, ka, vqkv, tab, "chunk_attn",
                                        gather=_Gather([shards_bf[n] for n in later], [conv_own]))
    full = {n: _whole(n, with_own(g, shards_bf[n])) for n, g in zip(later, gathered)}
    conv_w = jnp.concatenate(list(with_own(gathered[-1], conv_own)[:, :3]), axis=1)
    w_a, w_b, w_o, w_up, w_dn = (full[n] for n in later)
    w_a_t, w_b_t, w_o_t, w_up_t, w_dn_t = (w.T for w in (w_a, w_b, w_o, w_up, w_dn))
    out_b, out_b_f32, out_b_t = _sb_fwd(vqkv, vqkv, vqkv, "stick_attn", b_offs)
    y_a = _matmul(out_a, w_a, F32, "branch_a")
    y_b = _matmul(out_b, w_b, F32, "branch_b")
    mixed, mixed_t = _mix_fwd(g_a, g_b, y_a, y_b, "mix")
    x2 = _matmul(mixed, w_o, F32, "out_proj", residual=xs)
    hn2, hn2_t = _rms_fwd(x2, norm2_g, "rms2")
    hid = _matmul(hn2, w_up, BF16, "ffn_up")
    act, act_t = _convglu_fwd(hid, conv_w, ffn_conv_b, "convglu")
    y = _matmul(act, w_dn, F32, "ffn_down", residual=x2)
    dy, dyb, sq = _loss_head(y, tgt, "loss_head")
    loss = lax.psum(0.5 / D_MODEL * jnp.sum(sq), ("x", "y", "c"))

    dact = _matmul(dyb, w_dn_t, BF16, "d_act")
    d_w_dn = _matmul(act_t, dyb, F32, "d_w_down")
    dhg, dhu, dcwg, dcwu, dcbg, dcbu = _convglu_bwd(hid, dact, conv_w, ffn_conv_b, "convglu_bwd")
    half_chips = N_CHIPS // 2
    d_w_up = jnp.concatenate([_matmul(hn2_t, dhg, F32, "d_w_up_gate", slabs=half_chips),
                              _matmul(hn2_t, dhu, F32, "d_w_up_up", slabs=half_chips)], axis=0)
    dhn2 = _matmul(dhg, w_up_t[:D_FF], F32, "d_hn2_gate")
    dhn2 = _matmul(dhu, w_up_t[D_FF:], F32, "d_hn2_up", residual=dhn2)
    dx2, dx2b, d_norm2 = _rms_bwd(x2, norm2_g, dhn2, dy, "rms2_bwd")
    dmixed = _matmul(dx2b, w_o_t, F32, "d_mixed")
    d_w_o = _matmul(mixed_t, dx2b, F32, "d_w_out")
    dga, dgb, dya, dyb_b = _mix_bwd(dmixed, g_a, g_b, y_a, y_b, "mix_bwd")
    d_w_a, d_w_b = (_matmul(o_t, d, F32, nm).reshape(WIDTH, N_CHIPS, -1).transpose(1, 0, 2)
                    for o_t, d, nm in ((out_a_t, dya, "d_w_branch_a"), (out_b_t, dyb_b, "d_w_branch_b")))
    do_a = _matmul(dya, w_a_t, BF16, "d_out_a")
    do_b = _matmul(dyb_b, w_b_t, BF16, "d_out_b")

    def core_sums(names, gs, sibs):
        out = {}
        for n, g, sib in zip(names, gs, sibs):
            rh, cols = sib.shape[1], sib.shape[2]
            mine = lax.dynamic_slice_in_dim(g, ci * rh, rh, axis=1)
            out[n] = _add2(mine.reshape(-1, cols), sib.reshape(-1, cols), "sum_cores_" + n).reshape(sib.shape)
        return out

    grads_full = {"w_branch_a": d_w_a, "w_branch_b": d_w_b, "w_ffn_up": d_w_up,
                  "w_out": d_w_o.reshape(N_CHIPS, -1, D_MODEL), "w_ffn_down": d_w_dn.reshape(N_CHIPS, -1, D_MODEL)}
    early = [grads_full[n] for n in later]
    chip_parts = core_sums(later, early, _exchange_cores(early, "exchange_cores_early"))
    dqb, dkb, dvb, *parts_early = _sb_bwd(vqkv, vqkv, vqkv, out_b_f32, do_b, "stick_attn_bwd", b_offs,
                                           exchange=_ChipExchange([chip_parts[n] for n in later]))
    parts = dict(zip(later, parts_early))
    dqa_n, dka_n, dva, dtab = _ca_bwd(qa, ka, vqkv, do_a, tab, "chunk_attn_bwd")
    d_rel = _bias_table_bwd(dtab, "bias_table_bwd")[:, :N_REL]
    dqa, dka, dgq, dgk = _qknorm_bwd(qk, gq, gk, dqa_n, dka_n, "qknorm_bwd")
    dproj = jnp.concatenate([dqa, dka, dva, dqb, dkb, dvb, dga, dgb], axis=1)
    d_w_in = _matmul(hn_t, dproj, F32, "d_w_in", slabs=N_CHIPS)
    dhn = _matmul(dproj, w_in_t, F32, "d_hn")
    dx, _, d_norm1 = _rms_bwd(xs, norm1_g, dhn, dx2, "rms1_bwd")

    small_g = _pack_small({"norm1_g": d_norm1, "q_norm_g": dgq.reshape(N_HEADS, HEAD_DIM).sum(0),
                           "k_norm_g": dgk.reshape(N_HEADS, HEAD_DIM).sum(0), "rel_bias": d_rel,
                           "norm2_g": d_norm2, "ffn_conv_b": jnp.concatenate([dcbg, dcbu], axis=1),
                           "ffn_conv_w": jnp.concatenate([dcwg, dcwu], axis=1)}, SMALL_GRADS, SMALL_GRAD_ROWS)
    sib_in, small_all = _exchange_cores([d_w_in], "exchange_cores", small=small_g)
    chip_parts.update(core_sums(["w_in"], [d_w_in], [sib_in]))
    (parts["w_in"],) = _exchange_chips([chip_parts["w_in"]], "exchange_chips")
    g_halves = [_sum_leading(with_own(parts[n], lax.dynamic_index_in_dim(chip_parts[n], chip, 0, keepdims=False)),
                             "sum_chips_" + n) for n in BIG]
    g_others = _share_halves(g_halves, "share_halves")
    grads = {n: jnp.concatenate([jnp.where(ci == 0, mine, other), jnp.where(ci == 0, other, mine)], axis=0)
             for n, mine, other in zip(BIG, g_halves, g_others)}
    small_all = lax.dynamic_update_slice(small_all, small_g[None], (4 * xi + 2 * yi + ci, 0, 0))
    small_sum = _unpack_small(_sum_leading(small_all, "sum_small"), SMALL_GRADS)
    small_sum["ffn_conv_w"] = lax.dynamic_slice_in_dim(small_sum["ffn_conv_w"].reshape(3, 2 * D_FF),
                                                       chip * CONV_W_COLS, CONV_W_COLS, axis=1)

    deltas, new_m, new_v = {}, {}, {}
    for n in BIG:
        deltas[n], new_m[n], new_v[n] = _adamw(w_big[n], grads[n], m_big[n], v_big[n], "adamw_" + n)

    shapes = {"norm1_g": norm1_g.shape, "q_norm_g": q_norm_g.shape, "k_norm_g": k_norm_g.shape,
              "rel_bias": rel_bias.shape, "norm2_g": norm2_g.shape, "ffn_conv_b": ffn_conv_b.shape,
              "ffn_conv_w": ffn_conv_w.shape}
    small_w = {"norm1_g": norm1_g, "q_norm_g": q_norm_g, "k_norm_g": k_norm_g, "rel_bias": rel_bias,
               "norm2_g": norm2_g, "ffn_conv_b": ffn_conv_b, "ffn_conv_w": ffn_conv_w}
    small_m = {"norm1_g": m_norm1_g, "q_norm_g": m_q_norm_g, "k_norm_g": m_k_norm_g, "rel_bias": m_rel_bias,
               "norm2_g": m_norm2_g, "ffn_conv_b": m_ffn_conv_b, "ffn_conv_w": m_ffn_conv_w}
    small_v = {"norm1_g": v_norm1_g, "q_norm_g": v_q_norm_g, "k_norm_g": v_k_norm_g, "rel_bias": v_rel_bias,
               "norm2_g": v_norm2_g, "ffn_conv_b": v_ffn_conv_b, "ffn_conv_w": v_ffn_conv_w}
    ds, ms, vs = _adamw(*(_pack_small(t, SMALL_OWN, SMALL_OWN_ROWS) for t in (small_w, small_sum, small_m, small_v)),
                        "adamw_small")
    small_grads = small_sum
    ds, ms, vs = (_unpack_small(t, SMALL_OWN) for t in (ds, ms, vs))

    order = ("norm1_g", "w_in", "q_norm_g", "k_norm_g", "rel_bias", "w_branch_a", "w_branch_b", "w_out",
             "norm2_g", "w_ffn_up", "ffn_conv_w", "ffn_conv_b", "w_ffn_down")
    outs = [loss, dx[None]]
    for big, small in ((grads, small_grads), (deltas, ds), (new_m, ms), (new_v, vs)):
        for n in order:
            outs.append(big[n][None] if n in big else small[n].reshape(shapes[n]))
    return tuple(outs)
```

```python
import functools

import jax
import jax.numpy as jnp
from jax import lax
from jax.experimental import pallas as pl
from jax.experimental.pallas import tpu as pltpu

F32 = jnp.float32
BF16 = jnp.bfloat16
MESH = pl.DeviceIdType.MESH

D_MODEL = 1024
HEAD_DIM = 64
N_HEADS = 8
WIDTH = N_HEADS * HEAD_DIM
CHUNK = 64
LEFT_CHUNKS = 8
MAX_REL = 128
N_REL = 2 * MAX_REL + 1
D_FF = 2816
EPS = 1e-6
NEG = -1e30

ADAM_LR = 0.001
ADAM_B1 = 0.9
ADAM_B2 = 0.999
ADAM_EPS = 1e-08
ADAM_WD = 0.01
ADAM_STEP = 10

N_CHIPS = 4
N_DEV = 8
LANES = 128
PAIR = 2 * HEAD_DIM
BQ = 256
BAND = LEFT_CHUNKS * CHUNK
KWIN = BAND + BQ
VMEM_LIMIT = 56 * 1024 * 1024
PACK_COLS = 1024

NN = (((1,), (0,)), ((), ()))
NT = (((1,), (1,)), ((), ()))
TN = (((0,), (0,)), ((), ()))


def _cparams(sem=None):
    if sem is None:
        return pltpu.CompilerParams(vmem_limit_bytes=VMEM_LIMIT)
    return pltpu.CompilerParams(dimension_semantics=sem, vmem_limit_bytes=VMEM_LIMIT)


def _pick(n, cands):
    for c in cands:
        if n % c == 0:
            return c
    raise ValueError(f"no block for {n}")


def _dot(a, b, dn):
    return lax.dot_general(a, b, dn, preferred_element_type=F32)


def _sigmoid(x):
    return 0.5 * jnp.tanh(0.5 * x) + 0.5


def _split_bf16(x):
    hi = x.astype(BF16)
    lo = (x - hi.astype(F32)).astype(BF16)
    return hi, lo


MM_RESIDENT_BYTES = 12 * 1024 * 1024
MM_TILE_BYTES = 4 * 1024 * 1024


def _matmul(a, b, out_dtype, name, residual=None, slabs=None, exchange=None):
    (M, K), N = a.shape, b.shape[1]
    out_bytes = jnp.dtype(out_dtype).itemsize
    if slabs is None and N <= D_FF and K * N * 2 <= MM_RESIDENT_BYTES:
        bk, bn = K, N
        bm = next(c for c in (1024, 512, 256, 128)
                  if M % c == 0 and c * K * 2 <= MM_TILE_BYTES and c * N * out_bytes <= MM_TILE_BYTES)
        b_spec = pl.BlockSpec((bk, bn), lambda i, j, k: (0, 0), pipeline_mode=pl.Buffered(1))
    elif slabs is None and K <= D_FF:
        bk, bm, bn = K, _pick(M, (1024, 512)), _pick(N, (D_FF // 2, 512, 256, 128))
        b_spec = pl.BlockSpec((bk, bn), lambda i, j, k: (k, j))
    else:
        bk = _pick(K, (1024, 512))
        bm = _pick(M, (D_FF // 2, 1024, 512, 256, 128))
        bn = N // slabs if slabs else _pick(N, (D_FF // 2, 1024, 512, 256, 128))
        b_spec = pl.BlockSpec((bk, bn), lambda i, j, k: (k, j))
    nk = K // bk
    dn = NN
    a_spec = pl.BlockSpec((bm, bk), lambda i, j, k: (i, k))
    if slabs:
        o_spec = pl.BlockSpec((None, bm, bn), lambda i, j, k: (j, i, 0))
        out_shape = jax.ShapeDtypeStruct((slabs, M, bn), out_dtype)
    else:
        o_spec = pl.BlockSpec((bm, bn), lambda i, j, k: (i, j))
        out_shape = jax.ShapeDtypeStruct((M, N), out_dtype)
    has_res = residual is not None
    n_in = 3 if has_res else 2
    n_side = len(exchange.arrays) if exchange else 0
    grid = (M // bm, N // bn, nk)

    def body(*refs):
        a_ref, b_ref = refs[:2]
        r_ref = refs[2] if has_res else None
        o_ref, acc_ref = refs[n_in + n_side], refs[n_in + 2 * n_side + 1]
        k = pl.program_id(2)
        if exchange:
            steps = [pl.program_id(d) for d in range(3)]
            start, finish_side = exchange.steps(refs[n_in:n_in + n_side], refs[n_in + n_side + 1:n_in + 2 * n_side + 1],
                                                refs[n_in + 2 * n_side + 2:])
            pl.when((steps[0] == 0) & (steps[1] == 0) & (steps[2] == 0))(start)
        part = _dot(a_ref[...], b_ref[...], dn)

        def finish(total):
            if has_res:
                total = total + r_ref[...]
            o_ref[...] = total.astype(out_dtype)

        if nk == 1:
            finish(part)
        else:
            @pl.when(k == 0)
            def _():
                acc_ref[...] = part

            @pl.when(k > 0)
            def _():
                acc_ref[...] += part

            @pl.when(k == nk - 1)
            def _():
                finish(acc_ref[...])

        if exchange:
            pl.when((steps[0] == grid[0] - 1) & (steps[1] == grid[1] - 1) & (steps[2] == grid[2] - 1))(finish_side)

    side = exchange.arrays if exchange else []
    in_specs = [a_spec, b_spec] + ([o_spec] if has_res else []) + [ANY] * n_side
    args = (a, b) + ((residual,) if has_res else ()) + tuple(side)
    out = pl.pallas_call(
        body, name=name, grid=grid,
        in_specs=in_specs, out_specs=[o_spec] + [ANY] * n_side,
        out_shape=[out_shape] + (exchange.out_shape if exchange else []),
        scratch_shapes=[pltpu.VMEM((bm, bn) if nk > 1 else (8, LANES), F32)] + (exchange.scratch if exchange else []),
        compiler_params=_cparams(("arbitrary",) * 3 if exchange else ("parallel", "parallel", "arbitrary")),
    )(*args)
    return out if exchange else out[0]


ROWS = 512


def _row_spec(cols, bm=ROWS):
    return pl.BlockSpec((bm, cols), lambda i: (i, 0))


def _col_spec(rows, bn=ROWS):
    return pl.BlockSpec((rows, bn), lambda i: (0, i))


def _full_spec(shape):
    return pl.BlockSpec(shape, lambda i: (0,) * len(shape))


def _colsum8(t):
    return jnp.sum(t.reshape(t.shape[0] // 8, 8, t.shape[1]), axis=0)


def _rms_fwd(x, g, name, gather=None):
    S, D = x.shape
    nt = S // ROWS
    n_side = len(gather.arrays) if gather else 0

    def body(*refs):
        x_ref, g_ref = refs[:2]
        o_ref, ot_ref = refs[2 + n_side:4 + n_side]
        i = pl.program_id(0)
        if gather:
            start, forward, finish = gather.steps(refs[2:2 + n_side], refs[4 + n_side:4 + 2 * n_side], refs[4 + 2 * n_side:])
            pl.when(i == 0)(start)
            pl.when(i == 3 * nt // 4)(forward)
        xv = x_ref[...]
        r = lax.rsqrt(jnp.mean(xv * xv, axis=-1, keepdims=True) + EPS)
        y = xv * r * g_ref[...]
        o_ref[...] = y.astype(BF16)
        ot_ref[...] = y.T.astype(BF16)
        if gather:
            pl.when(i == nt - 1)(finish)

    side = gather.arrays if gather else []
    return pl.pallas_call(
        body, name=name, grid=(nt,),
        in_specs=[_row_spec(D), _full_spec((1, D))] + [ANY] * n_side,
        out_specs=[_row_spec(D), _col_spec(D)] + [ANY] * n_side,
        out_shape=[jax.ShapeDtypeStruct((S, D), BF16), jax.ShapeDtypeStruct((D, S), BF16)]
        + (gather.out_shape if gather else []),
        scratch_shapes=gather.scratch if gather else [],
        compiler_params=_cparams(("arbitrary",) if gather else ("parallel",)),
    )(x, g, *side)


def _rms_bwd(x, g, dy, dres, name):
    S, D = x.shape
    nt = S // ROWS

    def body(x_ref, g_ref, dy_ref, dres_ref, dx_ref, dxb_ref, dg_ref, acc_ref):
        i = pl.program_id(0)
        xv, dyv = x_ref[...], dy_ref[...]
        r = lax.rsqrt(jnp.mean(xv * xv, axis=-1, keepdims=True) + EPS)
        xr = xv * r
        u = dyv * g_ref[...]
        dx = r * u - xr * (r * r) * jnp.mean(xv * u, axis=-1, keepdims=True) + dres_ref[...]
        dx_ref[...] = dx
        dxb_ref[...] = dx.astype(BF16)
        part = _colsum8(dyv * xr)

        @pl.when(i == 0)
        def _():
            acc_ref[...] = part

        @pl.when(i > 0)
        def _():
            acc_ref[...] += part

        @pl.when(i == nt - 1)
        def _():
            dg_ref[...] = jnp.sum(acc_ref[...], axis=0, keepdims=True)

    return pl.pallas_call(
        body, name=name, grid=(nt,),
        in_specs=[_row_spec(D), _full_spec((1, D)), _row_spec(D), _row_spec(D)],
        out_specs=[_row_spec(D), _row_spec(D), _full_spec((1, D))],
        out_shape=[jax.ShapeDtypeStruct((S, D), F32), jax.ShapeDtypeStruct((S, D), BF16),
                   jax.ShapeDtypeStruct((1, D), F32)],
        scratch_shapes=[pltpu.VMEM((8, D), F32)],
        compiler_params=_cparams(("arbitrary",)),
    )(x, g, dy, dres)


def _head_mean(t, blockdiag):
    hi, lo = _split_bf16(t)
    return (_dot(hi, blockdiag, NN) + _dot(lo, blockdiag, NN)) * (1.0 / HEAD_DIM)


def _blockdiag():
    r = lax.broadcasted_iota(jnp.int32, (WIDTH, WIDTH), 0) // HEAD_DIM
    c = lax.broadcasted_iota(jnp.int32, (WIDTH, WIDTH), 1) // HEAD_DIM
    return jnp.where(r == c, 1.0, 0.0).astype(BF16)


def _qknorm_fwd(qk, gq, gk, name):
    S = qk.shape[0]

    def body(qk_ref, gq_ref, gk_ref, q_ref, k_ref):
        bd = _blockdiag()
        for part, g_ref, o_ref, scale in ((0, gq_ref, q_ref, HEAD_DIM ** -0.5), (1, gk_ref, k_ref, 1.0)):
            t = qk_ref[:, part * WIDTH:(part + 1) * WIDTH]
            r = lax.rsqrt(_head_mean(t * t, bd) + EPS)
            o_ref[...] = (t * r * g_ref[...] * scale).astype(BF16)

    return pl.pallas_call(
        body, name=name, grid=(S // ROWS,),
        in_specs=[_row_spec(2 * WIDTH), _full_spec((1, WIDTH)), _full_spec((1, WIDTH))],
        out_specs=[_row_spec(WIDTH), _row_spec(WIDTH)],
        out_shape=[jax.ShapeDtypeStruct((S, WIDTH), BF16)] * 2,
        compiler_params=_cparams(("parallel",)),
    )(qk, gq, gk)


def _qknorm_bwd(qk, gq, gk, dqn, dkn, name):
    S = qk.shape[0]
    nt = S // ROWS

    def body(qk_ref, gq_ref, gk_ref, dqn_ref, dkn_ref, dq_ref, dk_ref, dgq_ref, dgk_ref, accq_ref, acck_ref):
        i = pl.program_id(0)
        bd = _blockdiag()
        for part, g_ref, dn_ref, o_ref, dg_ref, acc_ref, scale in (
                (0, gq_ref, dqn_ref, dq_ref, dgq_ref, accq_ref, HEAD_DIM ** -0.5),
                (1, gk_ref, dkn_ref, dk_ref, dgk_ref, acck_ref, 1.0)):
            t = qk_ref[:, part * WIDTH:(part + 1) * WIDTH]
            dn = dn_ref[...] * scale
            r = lax.rsqrt(_head_mean(t * t, bd) + EPS)
            u = dn * g_ref[...]
            dt = r * u - t * (r * r * r) * _head_mean(t * u, bd)
            o_ref[...] = dt.astype(BF16)
            psum = _colsum8(dn * t * r)

            @pl.when(i == 0)
            def _():
                acc_ref[...] = psum

            @pl.when(i > 0)
            def _():
                acc_ref[...] += psum

            @pl.when(i == nt - 1)
            def _():
                dg_ref[...] = jnp.sum(acc_ref[...], axis=0, keepdims=True)

    return pl.pallas_call(
        body, name=name, grid=(nt,),
        in_specs=[_row_spec(2 * WIDTH), _full_spec((1, WIDTH)), _full_spec((1, WIDTH)),
                  _row_spec(WIDTH), _row_spec(WIDTH)],
        out_specs=[_row_spec(WIDTH), _row_spec(WIDTH), _full_spec((1, WIDTH)), _full_spec((1, WIDTH))],
        out_shape=[jax.ShapeDtypeStruct((S, WIDTH), BF16)] * 2 + [jax.ShapeDtypeStruct((1, WIDTH), F32)] * 2,
        scratch_shapes=[pltpu.VMEM((8, WIDTH), F32)] * 2,
        compiler_params=_cparams(("arbitrary",)),
    )(qk, gq, gk, dqn, dkn)


def _mix_fwd(ga, gb, ya, yb, name):
    S, D = ga.shape

    def body(ga_ref, gb_ref, ya_ref, yb_ref, o_ref, ot_ref):
        m = _sigmoid(ga_ref[...]) * ya_ref[...] + _sigmoid(gb_ref[...]) * yb_ref[...]
        o_ref[...] = m.astype(BF16)
        ot_ref[...] = m.T.astype(BF16)

    return pl.pallas_call(
        body, name=name, grid=(S // ROWS,),
        in_specs=[_row_spec(D)] * 4, out_specs=[_row_spec(D), _col_spec(D)],
        out_shape=[jax.ShapeDtypeStruct((S, D), BF16), jax.ShapeDtypeStruct((D, S), BF16)],
        compiler_params=_cparams(("parallel",)),
    )(ga, gb, ya, yb)


def _mix_bwd(dm, ga, gb, ya, yb, name):
    S, D = ga.shape

    def body(dm_ref, ga_ref, gb_ref, ya_ref, yb_ref, dga_ref, dgb_ref, dya_ref, dyb_ref):
        dmv = dm_ref[...]
        for g_ref, y_ref, dg_ref, dy_ref in ((ga_ref, ya_ref, dga_ref, dya_ref), (gb_ref, yb_ref, dgb_ref, dyb_ref)):
            s = _sigmoid(g_ref[...])
            dy_ref[...] = (dmv * s).astype(BF16)
            dg_ref[...] = (dmv * y_ref[...] * s * (1.0 - s)).astype(BF16)

    return pl.pallas_call(
        body, name=name, grid=(S // ROWS,),
        in_specs=[_row_spec(D)] * 5, out_specs=[_row_spec(D)] * 4,
        out_shape=[jax.ShapeDtypeStruct((S, D), BF16)] * 4,
        compiler_params=_cparams(("parallel",)),
    )(dm, ga, gb, ya, yb)


def _loss_head(y, target, name):
    S, D = y.shape
    nt = S // ROWS

    def body(y_ref, t_ref, dy_ref, dyb_ref, p_ref):
        err = y_ref[...] - t_ref[...]
        dy = err * (1.0 / D)
        dy_ref[...] = dy
        dyb_ref[...] = dy.astype(BF16)
        sq = _colsum8(err * err)
        acc = sq[:, 0:LANES]
        for k in range(1, D // LANES):
            acc = acc + sq[:, k * LANES:(k + 1) * LANES]
        p_ref[...] = acc

    return pl.pallas_call(
        body, name=name, grid=(nt,),
        in_specs=[_row_spec(D)] * 2,
        out_specs=[_row_spec(D), _row_spec(D), pl.BlockSpec((8, LANES), lambda i: (i, 0))],
        out_shape=[jax.ShapeDtypeStruct((S, D), F32), jax.ShapeDtypeStruct((S, D), BF16),
                   jax.ShapeDtypeStruct((nt * 8, LANES), F32)],
        compiler_params=_cparams(("parallel",)),
    )(y, target)


CONV_COLS = D_FF // 2
HALO = 16
CONV_CHUNK = 64


def _aligned(start, multiple):
    return start if isinstance(start, int) else pl.multiple_of(start, multiple)


def _conv_taps(xe, cw, cb):
    taps = (pltpu.roll(xe, 2, 0), pltpu.roll(xe, 1, 0), xe)
    return taps, cw[0:1] * taps[0] + cw[1:2] * taps[1] + cw[2:3] * taps[2] + cb


def _conv_specs(nt):
    hb, nb = ROWS // HALO, D_FF // CONV_COLS
    specs = {}
    for part, off in (("gate", 0), ("up", nb)):
        specs[part] = dict(
            main=pl.BlockSpec((ROWS, CONV_COLS), functools.partial(lambda c, i, off: (i, c + off), off=off)),
            prev=pl.BlockSpec((HALO, CONV_COLS),
                              functools.partial(lambda c, i, off: (jnp.maximum(i * hb - 1, 0), c + off), off=off)),
            nxt=pl.BlockSpec((HALO, CONV_COLS),
                             functools.partial(lambda c, i, off: (jnp.minimum((i + 1) * hb, nt * hb - 1), c + off), off=off)),
            w=pl.BlockSpec((3, CONV_COLS), functools.partial(lambda c, i, off: (0, c + off), off=off)),
            b=pl.BlockSpec((1, CONV_COLS), functools.partial(lambda c, i, off: (0, c + off), off=off)))
    return specs


def _convglu_fwd(hid, cw, cb, name):
    S = hid.shape[0]
    sp = _conv_specs(S // ROWS)

    def body(hg_ref, hgp_ref, hu_ref, hup_ref, cwg_ref, cwu_ref, cbg_ref, cbu_ref, o_ref, ot_ref):
        i = pl.program_id(1)
        keep = (i > 0).astype(F32)

        def conv(h_ref, hp_ref, cw_ref, cb_ref):
            xe = jnp.concatenate([hp_ref[...].astype(F32) * keep, h_ref[...].astype(F32)], axis=0)
            return _conv_taps(xe, cw_ref[...], cb_ref[...])[1][HALO:, :]

        gate = conv(hg_ref, hgp_ref, cwg_ref, cbg_ref)
        up = conv(hu_ref, hup_ref, cwu_ref, cbu_ref)
        act = gate * _sigmoid(gate) * up
        o_ref[...] = act.astype(BF16)
        ot_ref[...] = act.T.astype(BF16)

    g, u = sp["gate"], sp["up"]
    return pl.pallas_call(
        body, name=name, grid=(D_FF // CONV_COLS, S // ROWS),
        in_specs=[g["main"], g["prev"], u["main"], u["prev"], g["w"], u["w"], g["b"], u["b"]],
        out_specs=[g["main"], pl.BlockSpec((CONV_COLS, ROWS), lambda c, i: (c, i))],
        out_shape=[jax.ShapeDtypeStruct((S, D_FF), BF16), jax.ShapeDtypeStruct((D_FF, S), BF16)],
        compiler_params=_cparams(("parallel", "parallel")),
    )(hid, hid, hid, hid, cw, cw, cb, cb)


def _convglu_bwd(hid, dact, cw, cb, name):
    S = hid.shape[0]
    nt = S // ROWS
    sp = _conv_specs(nt)

    n_chunks = ROWS // CONV_CHUNK

    def body(hg_ref, hgp_ref, hgn_ref, hu_ref, hup_ref, hun_ref, da_ref, dan_ref,
             cwg_ref, cwu_ref, cbg_ref, cbu_ref,
             dhg_ref, dhu_ref, dcwg_ref, dcwu_ref, dcbg_ref, dcbu_ref, xg_s, xu_s, da_s):
        i = pl.program_id(1)
        kp = (i > 0).astype(F32)
        kn = (i < nt - 1).astype(F32)
        for x_s, h_ref, hp_ref, hn_ref in ((xg_s, hg_ref, hgp_ref, hgn_ref), (xu_s, hu_ref, hup_ref, hun_ref)):
            x_s[0:HALO, :] = hp_ref[...].astype(F32) * kp
            x_s[HALO:HALO + ROWS, :] = h_ref[...].astype(F32)
            x_s[HALO + ROWS:, :] = hn_ref[...].astype(F32) * kn
        da_s[0:ROWS, :] = da_ref[...].astype(F32)
        da_s[ROWS:, :] = dan_ref[...].astype(F32) * kn

        @pl.when(i == 0)
        def _():
            for ref in (dcwg_ref, dcwu_ref, dcbg_ref, dcbu_ref):
                ref[...] = jnp.zeros_like(ref)

        def lane_group(grp, _):
            lanes = pl.ds(pl.multiple_of(grp * LANES, LANES), LANES)
            cwg, cwu, cbg, cbu = cwg_ref[:, lanes], cwu_ref[:, lanes], cbg_ref[:, lanes], cbu_ref[:, lanes]

            def grads(r0, n):
                rows = pl.ds(_aligned(r0 + HALO - 8, 8), n + 8)
                taps_g, gate = _conv_taps(xg_s[rows, lanes], cwg, cbg)
                taps_u, up = _conv_taps(xu_s[rows, lanes], cwu, cbu)
                gate, up = gate[8:], up[8:]
                da = da_s[pl.ds(_aligned(r0, 8), n), lanes]
                sg = _sigmoid(gate)
                return (da * up * sg * (1.0 + gate * (1.0 - sg)), da * gate * sg,
                        [t[8:] for t in taps_g], [t[8:] for t in taps_u])

            def chunk(step, carry):
                below_g, below_u, accs = carry
                r0 = (n_chunks - 1 - step) * CONV_CHUNK
                dg, du, taps_g, taps_u = grads(r0, CONV_CHUNK)
                new_accs = []
                for d, below, cwv, taps, dh_ref, acc in ((dg, below_g, cwg, taps_g, dhg_ref, accs[0]),
                                                        (du, below_u, cwu, taps_u, dhu_ref, accs[1])):
                    ext = jnp.concatenate([d, below], axis=0)
                    n_ext = CONV_CHUNK + 8
                    dh = (cwv[2:3] * d + cwv[1:2] * pltpu.roll(ext, n_ext - 1, 0)[:CONV_CHUNK]
                          + cwv[0:1] * pltpu.roll(ext, n_ext - 2, 0)[:CONV_CHUNK])
                    dh_ref[pl.ds(_aligned(r0, CONV_CHUNK), CONV_CHUNK), lanes] = dh.astype(BF16)
                    new_accs.append(tuple(a + _colsum8(d * tap) for a, tap in zip(acc[:3], taps))
                                    + (acc[3] + _colsum8(d),))
                return dg[0:8], du[0:8], tuple(new_accs)

            below_g, below_u, _, _ = grads(ROWS, 8)
            zero = jnp.zeros((8, LANES), F32)
            _, _, accs = lax.fori_loop(0, n_chunks, chunk, (below_g, below_u, ((zero,) * 4, (zero,) * 4)))
            for acc, dcw_ref, dcb_ref in ((accs[0], dcwg_ref, dcbg_ref), (accs[1], dcwu_ref, dcbu_ref)):
                for t in range(3):
                    dcw_ref[t:t + 1, lanes] += jnp.sum(acc[t], axis=0, keepdims=True)
                dcb_ref[:, lanes] += jnp.sum(acc[3], axis=0, keepdims=True)
            return 0

        lax.fori_loop(0, CONV_COLS // LANES, lane_group, 0)

    g, u = sp["gate"], sp["up"]
    return pl.pallas_call(
        body, name=name, grid=(D_FF // CONV_COLS, nt),
        in_specs=[g["main"], g["prev"], g["nxt"], u["main"], u["prev"], u["nxt"], g["main"], g["nxt"],
                  g["w"], u["w"], g["b"], u["b"]],
        out_specs=[g["main"], g["main"], g["w"], g["w"], g["b"], g["b"]],
        out_shape=[jax.ShapeDtypeStruct((S, D_FF), BF16)] * 2 + [jax.ShapeDtypeStruct((3, D_FF), F32)] * 2
        + [jax.ShapeDtypeStruct((1, D_FF), F32)] * 2,
        scratch_shapes=[pltpu.VMEM((ROWS + 2 * HALO, CONV_COLS), F32)] * 2 + [pltpu.VMEM((ROWS + HALO, CONV_COLS), F32)],
        compiler_params=_cparams(("parallel", "arbitrary")),
    )(hid, hid, hid, hid, hid, hid, dact, dact, cw, cw, cb, cb)


REL_PAD = 384
DIAG = 1024


def _band_valid():
    qc = lax.broadcasted_iota(jnp.int32, (BQ, KWIN), 0) // CHUNK
    kc = lax.broadcasted_iota(jnp.int32, (BQ, KWIN), 1) // CHUNK - LEFT_CHUNKS
    return (kc <= qc) & (kc >= qc - LEFT_CHUNKS)


def _rel_index(offset):
    return jnp.clip(BAND - offset, -MAX_REL, MAX_REL) + MAX_REL


def _split3(x):
    hi = x.astype(BF16)
    rest = x - hi.astype(F32)
    mid = rest.astype(BF16)
    return hi, mid, (rest - mid.astype(F32)).astype(BF16)


def _bias_table(rel_bias, name):
    def body(rb_ref, o_ref):
        t = lax.broadcasted_iota(jnp.int32, (REL_PAD, DIAG), 0)
        lane = lax.broadcasted_iota(jnp.int32, (REL_PAD, DIAG), 1)
        pick = jnp.where(t == _rel_index(lane - BQ), 1.0, 0.0).astype(BF16)
        base = sum(_dot(piece, pick, NN) for piece in _split3(rb_ref[...]))
        valid = _band_valid()
        for h in range(N_HEADS):
            rows = jnp.broadcast_to(base[h:h + 1], (BQ, DIAG))
            rolled = pltpu.roll(rows, 0, 1, stride=1, stride_axis=0)
            o_ref[h] = jnp.where(valid, rolled[:, BQ:], NEG)

    return pl.pallas_call(
        body, name=name,
        out_shape=jax.ShapeDtypeStruct((N_HEADS, BQ, KWIN), F32),
        compiler_params=_cparams(),
    )(rel_bias)


def _bias_table_bwd(dtab, name):
    def body(d_ref, o_ref, diag_ref):
        r = lax.broadcasted_iota(jnp.int32, (BQ, BQ), 0)
        c = lax.broadcasted_iota(jnp.int32, (BQ, BQ), 1)
        flip = jnp.where(r + c == BQ - 1, 1.0, 0.0).astype(BF16)
        for h in range(N_HEADS):
            flipped = sum(_dot(flip, piece, NN) for piece in _split3(d_ref[h]))
            padded = jnp.concatenate([flipped, jnp.zeros((BQ, DIAG - KWIN), F32)], axis=1)
            rolled = pltpu.roll(padded, DIAG - (BQ - 1), 1, stride=1, stride_axis=0)
            diag_ref[h:h + 1, :] = jnp.sum(rolled, axis=0, keepdims=True)
        lane = lax.broadcasted_iota(jnp.int32, (DIAG, REL_PAD), 0)
        t = lax.broadcasted_iota(jnp.int32, (DIAG, REL_PAD), 1)
        offset = jnp.where(lane < KWIN, lane, lane - DIAG)
        pick = jnp.where(t == _rel_index(offset), 1.0, 0.0).astype(BF16)
        o_ref[...] = sum(_dot(piece, pick, NN) for piece in _split3(diag_ref[...]))

    return pl.pallas_call(
        body, name=name,
        out_shape=jax.ShapeDtypeStruct((N_HEADS, REL_PAD), F32),
        scratch_shapes=[pltpu.VMEM((N_HEADS, DIAG), F32)],
        compiler_params=_cparams(),
    )(dtab)


def _head_masks(heads=2):
    lane = lax.broadcasted_iota(jnp.int32, (1, heads * HEAD_DIM), 1)
    return [lane // HEAD_DIM == h for h in range(heads)]


def _own_lanes(masks, vals):
    out = vals[-1]
    for m, val in zip(masks[-2::-1], vals[-2::-1]):
        out = jnp.where(m, val, out)
    return out


CA_HEADS = 4
CA_LANES = CA_HEADS * HEAD_DIM


def _ca_window_specs(nq, col_off=0):
    return [pl.BlockSpec((BQ, CA_LANES), functools.partial(
        lambda p, i, d: (jnp.clip(i - 2 + d, 0, nq - 1), p + col_off), d=d)) for d in range(3)]


def _softmax_rows(s):
    p = jnp.exp(s - jnp.max(s, axis=-1, keepdims=True))
    return p, jnp.sum(p, axis=-1, keepdims=True)


def _ca_scores(qm, kc, tab_h, i):
    col = lax.broadcasted_iota(jnp.int32, (1, KWIN), 1)
    in_seq = col + (i - 2) * BQ >= 0
    return jnp.where(in_seq, _dot(qm, kc, NT) + tab_h, NEG)


def _ca_fwd(qn, kn, v, tab, name, v_off=0, gather=None):
    S = qn.shape[0]
    nq = S // BQ
    groups = WIDTH // CA_LANES
    qspec = pl.BlockSpec((BQ, CA_LANES), lambda p, i: (i, p))
    tspec = pl.BlockSpec((CA_HEADS, BQ, KWIN), lambda p, i: (p, 0, 0))
    n_side = len(gather.arrays) if gather else 0

    def body(*refs):
        q_ref, k0, k1, k2, v0, v1, v2, tab_ref = refs[:8]
        o_ref, ot_ref = refs[8 + n_side:10 + n_side]
        p, i = pl.program_id(0), pl.program_id(1)
        if gather:
            start, forward, finish = gather.steps(refs[8:8 + n_side], refs[10 + n_side:10 + 2 * n_side],
                                                  refs[10 + 2 * n_side:])
            pl.when((p == 0) & (i == 0))(start)
            pl.when((p == groups - 1) & (i == nq // 2))(forward)
        kc = jnp.concatenate([k0[...], k1[...], k2[...]], axis=0)
        vc = jnp.concatenate([v0[...], v1[...], v2[...]], axis=0)
        qv = q_ref[...]
        masks = _head_masks(CA_HEADS)
        heads = range(CA_HEADS)
        s = [_ca_scores(jnp.where(masks[h], qv, 0), kc, tab_ref[h], i) for h in heads]
        soft = [_softmax_rows(s[h]) for h in heads]
        o = [_dot(soft[h][0].astype(BF16), vc, NN) / soft[h][1] for h in heads]
        out = _own_lanes(masks, o)
        o_ref[...] = out.astype(BF16)
        ot_ref[...] = out.T.astype(BF16)
        if gather:
            pl.when((p == groups - 1) & (i == nq - 1))(finish)

    side = gather.arrays if gather else []
    return pl.pallas_call(
        body, name=name, grid=(groups, nq),
        in_specs=[qspec] + _ca_window_specs(nq) + _ca_window_specs(nq, v_off) + [tspec] + [ANY] * n_side,
        out_specs=[qspec, pl.BlockSpec((CA_LANES, BQ), lambda p, i: (p, i))] + [ANY] * n_side,
        out_shape=[jax.ShapeDtypeStruct((S, WIDTH), BF16), jax.ShapeDtypeStruct((WIDTH, S), BF16)]
        + (gather.out_shape if gather else []),
        scratch_shapes=gather.scratch if gather else [],
        compiler_params=_cparams(("arbitrary", "arbitrary") if gather else ("parallel", "parallel")),
    )(qn, kn, kn, kn, v, v, v, tab, *side)


def _ca_bwd(qn, kn, v, do, tab, name, v_off=0):
    S = qn.shape[0]
    nq = S // BQ
    qspec = pl.BlockSpec((BQ, CA_LANES), lambda p, i: (jnp.minimum(i, nq - 1), p))
    kout = pl.BlockSpec((BQ, CA_LANES), lambda p, i: (jnp.clip(i - 2, 0, nq - 1), p))
    tspec = pl.BlockSpec((CA_HEADS, BQ, KWIN), lambda p, i: (p, 0, 0))

    def body(q_ref, do_ref, k0, k1, k2, v0, v1, v2, tab_ref,
             dq_ref, dk_ref, dv_ref, dtab_ref, dk_acc, dv_acc):
        i = pl.program_id(1)

        @pl.when(i == 0)
        def _():
            dk_acc[...] = jnp.zeros_like(dk_acc)
            dv_acc[...] = jnp.zeros_like(dv_acc)
            dtab_ref[...] = jnp.zeros_like(dtab_ref)

        @pl.when(i < nq)
        def _():
            kc = jnp.concatenate([k0[...], k1[...], k2[...]], axis=0)
            vc = jnp.concatenate([v0[...], v1[...], v2[...]], axis=0)
            qv, dov = q_ref[...], do_ref[...]
            masks = _head_masks(CA_HEADS)
            heads = range(CA_HEADS)
            qm = [jnp.where(masks[h], qv, 0) for h in heads]
            dom = [jnp.where(masks[h], dov, 0) for h in heads]
            s = [_ca_scores(qm[h], kc, tab_ref[h], i) for h in heads]
            dp = [_dot(dom[h], vc, NT) for h in heads]
            soft = [_softmax_rows(s[h]) for h in heads]
            p = [soft[h][0] / soft[h][1] for h in heads]
            ds = [p[h] * (dp[h] - jnp.sum(p[h] * dp[h], axis=-1, keepdims=True)) for h in heads]
            for h in heads:
                dtab_ref[h] += ds[h]
            dsb = [ds[h].astype(BF16) for h in heads]
            pb = [p[h].astype(BF16) for h in heads]
            dq = [_dot(dsb[h], kc, NN) for h in heads]
            dq_ref[...] = _own_lanes(masks, dq)
            dkc = sum(_dot(dsb[h], qm[h], TN) for h in heads)
            dvc = sum(_dot(pb[h], dom[h], TN) for h in heads)
            for d in range(3):
                slot = (i + 1 + d) % 3
                dk_acc[slot] += dkc[d * BQ:(d + 1) * BQ]
                dv_acc[slot] += dvc[d * BQ:(d + 1) * BQ]

        @pl.when(i >= 2)
        def _():
            slot = (i + 1) % 3
            dk_ref[...] = dk_acc[slot]
            dv_ref[...] = dv_acc[slot].astype(BF16)
            dk_acc[slot] = jnp.zeros((BQ, CA_LANES), F32)
            dv_acc[slot] = jnp.zeros((BQ, CA_LANES), F32)

    return pl.pallas_call(
        body, name=name, grid=(WIDTH // CA_LANES, nq + 2),
        in_specs=[qspec, qspec] + _ca_window_specs(nq) + _ca_window_specs(nq, v_off) + [tspec],
        out_specs=[qspec, kout, kout, tspec],
        out_shape=[jax.ShapeDtypeStruct((S, WIDTH), F32), jax.ShapeDtypeStruct((S, WIDTH), F32),
                   jax.ShapeDtypeStruct((S, WIDTH), BF16), jax.ShapeDtypeStruct((N_HEADS, BQ, KWIN), F32)],
        scratch_shapes=[pltpu.VMEM((3, BQ, CA_LANES), F32)] * 2,
        compiler_params=_cparams(("parallel", "arbitrary")),
    )(qn, do, kn, kn, kn, v, v, v, tab)


def _sb_consts():
    r = lax.broadcasted_iota(jnp.int32, (BQ, BQ), 0)
    c = lax.broadcasted_iota(jnp.int32, (BQ, BQ), 1)
    from_s = jnp.where(r >= c, 1.0, 0.0).astype(BF16)
    causal = c < r
    return from_s, causal


def _suffix_sum(t, from_s):
    hi, lo = _split_bf16(t)
    return _dot(hi, from_s, NN) + _dot(lo, from_s, NN)


def _neg_abs(x):
    bits = lax.bitcast_convert_type(x, jnp.uint32) | jnp.uint32(0x80000000)
    return lax.bitcast_convert_type(bits, F32)


def _sb_log_keep(zn):
    return jnp.minimum(zn, 0.0) - jnp.log(1.0 + jnp.exp(_neg_abs(zn)))


SB_DEAD = 105.0


SB_QB = 2


def _sb_walk(ip, tiles, keep_ref):
    i0 = SB_QB * ip

    @pl.when(ip == 0)
    def _():
        tiles([(0, [0], [True]), (1, [1, 0], [True, False])])

    @pl.when(ip > 0)
    def _():
        tiles([(a, [i0 + a, i0 + a - 1], [True, False]) for a in range(SB_QB)])

    for a in range(SB_QB):
        def alive(a=a):
            return (jnp.max(keep_ref[2 * a:2 * a + 2]) > -SB_DEAD).astype(jnp.int32)

        def step(state, a=a, alive=alive):
            j, _ = state
            tiles([(a, [j], [False])])
            return j - 1, alive()

        lax.while_loop(lambda state: (state[0] >= 0) & (state[1] > 0), step, (i0 + a - 2, alive()))


def _sb_rows(j):
    return pl.ds(pl.multiple_of(j * BQ, BQ), BQ)


def _sb_chains(groups):
    chains = [(a, n, h) for a, js, _ in groups for n in range(len(js)) for h in range(2)]
    block = {(a, n): j for a, js, _ in groups for n, j in enumerate(js)}
    masked = [(a, n, h) for a, _, diags in groups for n, d in enumerate(diags) if d for h in range(2)]
    return chains, block, masked


def _sb_running(ref, vals, groups):
    before_chain = {}
    for a, js, _ in groups:
        for h in range(2):
            run = ref[2 * a + h]
            for n in range(len(js)):
                before_chain[(a, n, h)] = run
                run = run + jnp.sum(vals[(a, n, h)], axis=-1, keepdims=True)
            ref[2 * a + h] = run
    return before_chain


def _sb_specs(S, offs):
    def qspec(off=0):
        return pl.BlockSpec((SB_QB * BQ, PAIR), lambda p, i: (i, p + off))

    def kspec(off=0):
        return pl.BlockSpec((S, PAIR), lambda p, i: (0, p + off), pipeline_mode=pl.Buffered(1))

    return qspec, kspec, [qspec(offs[0]), kspec(offs[1]), kspec(offs[2])]


def _sb_fwd(q, k, v, name, offs=(0, 0, 0)):
    S = q.shape[0]
    steps = S // (SB_QB * BQ)
    qspec, _, qkv_specs = _sb_specs(S, offs)

    def body(q_ref, k_ref, v_ref, o_ref, of_ref, ot_ref, carry_ref, acc_ref):
        ip = pl.program_id(1)
        from_s, causal = _sb_consts()
        masks = _head_masks()
        qn = q_ref[...] * -(HEAD_DIM ** -0.5)
        qms = {(a, h): jnp.where(masks[h], qn[a * BQ:(a + 1) * BQ], 0) for a in range(SB_QB) for h in range(2)}
        carry_ref[...] = jnp.zeros_like(carry_ref)
        acc_ref[...] = jnp.zeros_like(acc_ref)

        def tiles(groups):
            chains, block, masked = _sb_chains(groups)
            kbs = {an: k_ref[_sb_rows(j), :] for an, j in block.items()}
            vbs = {an: v_ref[_sb_rows(j), :] for an, j in block.items()}
            zn = {c: _dot(qms[(c[0], c[2])], kbs[c[:2]], NT) for c in chains}
            log_keep = {c: _sb_log_keep(zn[c]) for c in chains}
            for c in masked:
                log_keep[c] = jnp.where(causal, log_keep[c], 0.0)
            split = {c: _split_bf16(log_keep[c]) for c in chains}
            carry = _sb_running(carry_ref, log_keep, groups)
            suffix = {c: _dot(split[c][0], from_s, NN) + _dot(split[c][1], from_s, NN) for c in chains}
            w = {c: jnp.exp(carry[c] + suffix[c] - zn[c]) for c in chains}
            for c in masked:
                w[c] = jnp.where(causal, w[c], 0.0)
            for c in chains:
                acc_ref[2 * c[0] + c[2]] += _dot(w[c].astype(BF16), vbs[c[:2]], NN)

        _sb_walk(ip, tiles, carry_ref)
        for a in range(SB_QB):
            out = jnp.where(masks[0], acc_ref[2 * a], acc_ref[2 * a + 1])
            o_ref[a * BQ:(a + 1) * BQ, :] = out.astype(BF16)
            of_ref[a * BQ:(a + 1) * BQ, :] = out
            ot_ref[:, a * BQ:(a + 1) * BQ] = out.T.astype(BF16)

    return pl.pallas_call(
        body, name=name, grid=(WIDTH // PAIR, steps),
        in_specs=qkv_specs, out_specs=[qspec(), qspec(), pl.BlockSpec((PAIR, SB_QB * BQ), lambda p, i: (p, i))],
        out_shape=[jax.ShapeDtypeStruct((S, WIDTH), BF16), jax.ShapeDtypeStruct((S, WIDTH), F32),
                   jax.ShapeDtypeStruct((WIDTH, S), BF16)],
        scratch_shapes=[pltpu.VMEM((2 * SB_QB, BQ, 1), F32), pltpu.VMEM((2 * SB_QB, BQ, PAIR), F32)],
        compiler_params=_cparams(("parallel", "arbitrary")),
    )(q, k, v)


def _sb_bwd(q, k, v, o, do, name, offs=(0, 0, 0), exchange=None):
    S = q.shape[0]
    steps = S // (SB_QB * BQ)
    pairs = WIDTH // PAIR
    qspec, kspec, qkv_specs = _sb_specs(S, offs)
    n_side = len(exchange.arrays) if exchange else 0

    def body(*refs):
        q_ref, o_ref, do_ref, k_ref, v_ref = refs[:5]
        dq_ref, dk_ref, dv_ref = refs[5 + n_side:8 + n_side]
        dk_acc, dv_acc, keep_ref, gsum_ref, dq_acc = refs[8 + 2 * n_side:13 + 2 * n_side]
        ip = pl.program_id(1)
        if exchange:
            start, finish = exchange.steps(refs[5:5 + n_side], refs[8 + n_side:8 + 2 * n_side], refs[13 + 2 * n_side:])
            pl.when((pl.program_id(0) == 0) & (ip == 0))(start)

        @pl.when(ip == 0)
        def _():
            dk_acc[...] = jnp.zeros_like(dk_acc)
            dv_acc[...] = jnp.zeros_like(dv_acc)

        from_s, causal = _sb_consts()
        masks = _head_masks()
        qn, dov = q_ref[...] * -(HEAD_DIM ** -0.5), do_ref[...]
        od = o_ref[...] * dov.astype(F32)
        lanes = [(a, h) for a in range(SB_QB) for h in range(2)]
        rows_of = {a: slice(a * BQ, (a + 1) * BQ) for a in range(SB_QB)}
        qms = {(a, h): jnp.where(masks[h], qn[rows_of[a]], 0) for a, h in lanes}
        doms = {(a, h): jnp.where(masks[h], dov[rows_of[a]], 0) for a, h in lanes}
        totals = {(a, h): jnp.sum(jnp.where(masks[h], od[rows_of[a]], 0.0), axis=-1, keepdims=True)
                  for a, h in lanes}
        for ref in (keep_ref, gsum_ref, dq_acc):
            ref[...] = jnp.zeros_like(ref)

        def tiles(groups):
            chains, block, masked = _sb_chains(groups)
            kbs = {an: k_ref[_sb_rows(j), :] for an, j in block.items()}
            vbs = {an: v_ref[_sb_rows(j), :] for an, j in block.items()}
            zn = {c: _dot(qms[(c[0], c[2])], kbs[c[:2]], NT) for c in chains}
            dw = {c: _dot(doms[(c[0], c[2])], vbs[c[:2]], NT) for c in chains}
            log_keep = {c: _sb_log_keep(zn[c]) for c in chains}
            for c in masked:
                log_keep[c] = jnp.where(causal, log_keep[c], 0.0)
            split = {c: _split_bf16(log_keep[c]) for c in chains}
            kept = _sb_running(keep_ref, log_keep, groups)
            suffix = {c: _dot(split[c][0], from_s, NN) + _dot(split[c][1], from_s, NN) for c in chains}
            w = {c: jnp.exp(kept[c] + suffix[c] - zn[c]) for c in chains}
            for c in masked:
                w[c] = jnp.where(causal, w[c], 0.0)
            wb = {c: w[c].astype(BF16) for c in chains}
            g = {c: wb[c].astype(F32) * dw[c] for c in chains}
            gsplit = {c: _split_bf16(g[c]) for c in chains}
            gsum = _sb_running(gsum_ref, g, groups)
            gsuffix = {c: _dot(gsplit[c][0], from_s, NN) + _dot(gsplit[c][1], from_s, NN) for c in chains}
            dzb = {}
            for c in chains:
                before = totals[(c[0], c[2])] - (gsum[c] + gsuffix[c])
                dz = (g[c] + before) * jnp.exp(log_keep[c]) - before
                if c in masked:
                    dz = jnp.where(causal, dz, 0.0)
                dzb[c] = dz.astype(BF16)
            for c in chains:
                rows = _sb_rows(block[c[:2]])
                dq_acc[2 * c[0] + c[2]] += _dot(dzb[c], kbs[c[:2]], NN)
                dk_acc[rows, :] -= _dot(dzb[c], qms[(c[0], c[2])], TN)
                dv_acc[rows, :] += _dot(wb[c], doms[(c[0], c[2])], TN)

        _sb_walk(ip, tiles, keep_ref)
        for a in range(SB_QB):
            dq = jnp.where(masks[0], dq_acc[2 * a], dq_acc[2 * a + 1])
            dq_ref[a * BQ:(a + 1) * BQ, :] = (dq * HEAD_DIM ** -0.5).astype(BF16)

        @pl.when(ip == steps - 1)
        def _():
            dk_ref[...] = dk_acc[...].astype(BF16)
            dv_ref[...] = dv_acc[...].astype(BF16)

        if exchange:
            pl.when((pl.program_id(0) == pairs - 1) & (ip == steps - 1))(finish)

    side = exchange.arrays if exchange else []
    return pl.pallas_call(
        body, name=name, grid=(pairs, steps),
        in_specs=[qkv_specs[0], qspec(), qspec(), qkv_specs[1], qkv_specs[2]] + [ANY] * n_side,
        out_specs=[qspec(), kspec(), kspec()] + [ANY] * n_side,
        out_shape=[jax.ShapeDtypeStruct((S, WIDTH), BF16)] * 3 + (exchange.out_shape if exchange else []),
        scratch_shapes=[pltpu.VMEM((S, PAIR), F32)] * 2 + [pltpu.VMEM((2 * SB_QB, BQ, 1), F32)] * 2
        + [pltpu.VMEM((2 * SB_QB, BQ, PAIR), F32)] + (exchange.scratch if exchange else []),
        compiler_params=_cparams(("arbitrary", "arbitrary") if exchange else ("parallel", "arbitrary")),
    )(q, o, do, k, v, *side)


ANY = pl.BlockSpec(memory_space=pl.ANY)


def _place():
    return lax.axis_index("x"), lax.axis_index("y"), lax.axis_index("c")


def _other_chips(x, y):
    return [(2 * px + py, (px, py)) for px, py in ((1 - x, y), (x, 1 - y), (1 - x, 1 - y))]


def _remote(src, dst, sems, k, to):
    return pltpu.make_async_remote_copy(src_ref=src, dst_ref=dst, send_sem=sems[0].at[k], recv_sem=sems[1].at[k],
                                        device_id=to, device_id_type=MESH)


class _Gather:
    def __init__(self, ws, extras=()):
        self.n, self.m = len(ws), len(extras)
        self.arrays = list(ws) + list(extras)
        self.n_copies = 6 * self.n + 3 * self.m
        self.out_shape = [jax.ShapeDtypeStruct((N_CHIPS,) + a.shape, a.dtype) for a in self.arrays]
        self.scratch = [pltpu.SemaphoreType.DMA((self.n_copies,)), pltpu.SemaphoreType.DMA((self.n_copies,))]

    def steps(self, in_refs, out_refs, sems):
        n = self.n
        x, y, c = _place()
        me = 2 * x + y
        chips = _other_chips(x, y)
        sibling = (x, y, 1 - c)

        def halves(ref):
            rh = ref.shape[-2] // 2
            return pl.ds(c * rh, rh), pl.ds((1 - c) * rh, rh)

        def first():
            cps = [_remote(w_ref.at[halves(w_ref)[0]], o_ref.at[me, halves(w_ref)[0]], sems, 6 * a + k, (*xy, c))
                   for a, (w_ref, o_ref) in enumerate(zip(in_refs[:n], out_refs[:n])) for k, (_, xy) in enumerate(chips)]
            return cps + [_remote(e_ref, eo_ref.at[me], sems, 6 * n + 3 * b + k, (*xy, c))
                          for b, (e_ref, eo_ref) in enumerate(zip(in_refs[n:], out_refs[n:]))
                          for k, (_, xy) in enumerate(chips)]

        def passed():
            return [_remote(o_ref.at[chip, halves(o_ref)[0]], o_ref.at[chip, halves(o_ref)[0]], sems, 6 * a + 3 + k, sibling)
                    for a, o_ref in enumerate(out_refs[:n]) for k, (chip, _) in enumerate(chips)]

        def start():
            for cp in first():
                cp.start()

        def forward():
            for a, o_ref in enumerate(out_refs[:n]):
                for k, (chip, xy) in enumerate(chips):
                    landed = o_ref.at[chip, halves(o_ref)[0]]
                    _remote(landed, landed, sems, 6 * a + k, (*xy, c)).wait_recv()
            for cp in passed():
                cp.start()

        def finish():
            for a, o_ref in enumerate(out_refs[:n]):
                for k, (chip, _) in enumerate(chips):
                    landed = o_ref.at[chip, halves(o_ref)[1]]
                    _remote(landed, landed, sems, 6 * a + 3 + k, sibling).wait_recv()
            for b, (e_ref, eo_ref) in enumerate(zip(in_refs[n:], out_refs[n:])):
                for k, (chip, xy) in enumerate(chips):
                    _remote(e_ref, eo_ref.at[chip], sems, 6 * n + 3 * b + k, (*xy, c)).wait_recv()
            for cp in first() + passed():
                cp.wait_send()

        return start, forward, finish


class _ChipExchange:
    def __init__(self, ps):
        self.arrays = list(ps)
        self.out_shape = [jax.ShapeDtypeStruct(p.shape, p.dtype) for p in ps]
        self.scratch = [pltpu.SemaphoreType.DMA((3 * len(ps),)), pltpu.SemaphoreType.DMA((3 * len(ps),))]

    def steps(self, p_refs, out_refs, sems):
        x, y, c = _place()
        me = 2 * x + y
        chips = _other_chips(x, y)

        def copies():
            return [_remote(p_ref.at[chip], o_ref.at[me], sems, 3 * a + k, (*xy, c))
                    for a, (p_ref, o_ref) in enumerate(zip(p_refs, out_refs)) for k, (chip, xy) in enumerate(chips)]

        def start():
            for cp in copies():
                cp.start()

        def finish():
            for a, (p_ref, o_ref) in enumerate(zip(p_refs, out_refs)):
                for k, (chip, xy) in enumerate(chips):
                    _remote(p_ref.at[chip], o_ref.at[chip], sems, 3 * a + k, (*xy, c)).wait_recv()
            for cp in copies():
                cp.wait_send()

        return start, finish


def _exchange_cores(gs, name, small=None):
    n = len(gs)
    m = 0 if small is None else 1

    def body(*refs):
        g_refs, sib_refs = refs[:n], refs[n + m:2 * n + m]
        sems = refs[2 * (n + m):]
        x, y, c = _place()
        me = 4 * x + 2 * y + c
        copies = []
        for a, (g_ref, sib_ref) in enumerate(zip(g_refs, sib_refs)):
            rh = g_ref.shape[1] // 2
            copies.append(_remote(g_ref.at[:, pl.ds((1 - c) * rh, rh), :], sib_ref, sems, a, (x, y, 1 - c)))
        if m:
            small_ref, all_ref = refs[n], refs[2 * n + m]
            k = n
            for fx in (0, 1):
                for fy in (0, 1):
                    for fc in (0, 1):
                        if fx or fy or fc:
                            to = (1 - x if fx else x, 1 - y if fy else y, 1 - c if fc else c)
                            copies.append(_remote(small_ref, all_ref.at[me], sems, k, to))
                            k += 1
        for cp in copies:
            cp.start()
        for cp in copies:
            cp.wait_recv()
        for cp in copies:
            cp.wait_send()

    n_copies = n + m * (N_DEV - 1)
    args = list(gs) + ([small] if m else [])
    return pl.pallas_call(
        body, name=name, in_specs=[ANY] * (n + m), out_specs=[ANY] * (n + m),
        out_shape=[jax.ShapeDtypeStruct((N_CHIPS, g.shape[1] // 2, g.shape[2]), F32) for g in gs]
        + ([jax.ShapeDtypeStruct((N_DEV,) + small.shape, F32)] if m else []),
        scratch_shapes=[pltpu.SemaphoreType.DMA((n_copies,)), pltpu.SemaphoreType.DMA((n_copies,))],
    )(*args)


def _share_halves(ghs, name):
    n = len(ghs)

    def body(*refs):
        gh_refs, out_refs, sems = refs[:n], refs[n:2 * n], refs[2 * n:]
        x, y, c = _place()
        copies = [_remote(gh_ref, o_ref, sems, a, (x, y, 1 - c)) for a, (gh_ref, o_ref) in enumerate(zip(gh_refs, out_refs))]
        for cp in copies:
            cp.start()
        for cp in copies:
            cp.wait_recv()
        for cp in copies:
            cp.wait_send()

    return pl.pallas_call(
        body, name=name, in_specs=[ANY] * n, out_specs=[ANY] * n,
        out_shape=[jax.ShapeDtypeStruct(g.shape, g.dtype) for g in ghs],
        scratch_shapes=[pltpu.SemaphoreType.DMA((n,)), pltpu.SemaphoreType.DMA((n,))],
    )(*ghs)


EW_BLOCK_BYTES = 2 * 1024 * 1024


def _row_block(rows, cols, mult=8):
    fits = [b for b in range(mult, rows + 1, mult) if rows % b == 0 and b * cols * 4 <= EW_BLOCK_BYTES]
    return max(fits) if fits else mult


def _add2(a, b, name):
    R, C = a.shape
    rows = _row_block(R, C, mult=16)
    spec = pl.BlockSpec((rows, C), lambda i: (i, 0))

    def body(a_ref, b_ref, o_ref):
        o_ref[...] = (a_ref[...] + b_ref[...]).astype(BF16)

    return pl.pallas_call(
        body, name=name, grid=(R // rows,), in_specs=[spec, spec], out_specs=spec,
        out_shape=jax.ShapeDtypeStruct(a.shape, BF16),
        compiler_params=_cparams(("parallel",)),
    )(a, b)


def _sum_leading(a, name):
    n, R, C = a.shape
    rows = _row_block(R, n * C, mult=16 if a.dtype == BF16 else 8)

    def body(a_ref, o_ref):
        acc = a_ref[0].astype(F32)
        for j in range(1, n):
            acc = acc + a_ref[j].astype(F32)
        o_ref[...] = acc

    return pl.pallas_call(
        body, name=name, grid=(R // rows,),
        in_specs=[pl.BlockSpec((n, rows, C), lambda i: (0, i, 0))],
        out_specs=pl.BlockSpec((rows, C), lambda i: (i, 0)),
        out_shape=jax.ShapeDtypeStruct((R, C), F32),
        compiler_params=_cparams(("parallel",)),
    )(a)


def _adamw(w, g, m, v, name):
    R, C = w.shape
    rows = _row_block(R, C)
    spec = pl.BlockSpec((rows, C), lambda i: (i, 0))

    def body(w_ref, g_ref, m_ref, v_ref, d_ref, mo_ref, vo_ref):
        gv = g_ref[...]
        mn = ADAM_B1 * m_ref[...] + (1.0 - ADAM_B1) * gv
        vn = ADAM_B2 * v_ref[...] + (1.0 - ADAM_B2) * (gv * gv)
        m_hat = mn / (1.0 - ADAM_B1 ** ADAM_STEP)
        v_hat = vn / (1.0 - ADAM_B2 ** ADAM_STEP)
        d_ref[...] = -ADAM_LR * (m_hat / (jnp.sqrt(v_hat) + ADAM_EPS) + ADAM_WD * w_ref[...])
        mo_ref[...] = mn
        vo_ref[...] = vn

    return pl.pallas_call(
        body, name=name, grid=(R // rows,), in_specs=[spec] * 4, out_specs=[spec] * 3,
        out_shape=[jax.ShapeDtypeStruct((R, C), F32)] * 3,
        compiler_params=_cparams(("parallel",)),
    )(w, g, m, v)


BIG = ("w_in", "w_branch_a", "w_branch_b", "w_out", "w_ffn_up", "w_ffn_down")
COL_SHARDED = {"w_in": True, "w_branch_a": True, "w_branch_b": True, "w_out": False, "w_ffn_up": True,
               "w_ffn_down": False}
CONV_W_COLS = 2 * D_FF // N_CHIPS
SMALL_REPLICATED = (("norm1_g", D_MODEL), ("q_norm_g", HEAD_DIM), ("k_norm_g", HEAD_DIM),
                    ("rel_bias", N_HEADS * N_REL), ("norm2_g", D_MODEL), ("ffn_conv_b", 2 * D_FF))
SMALL_GRADS = SMALL_REPLICATED + (("ffn_conv_w", 3 * 2 * D_FF),)
SMALL_OWN = SMALL_REPLICATED + (("ffn_conv_w", 3 * CONV_W_COLS),)
SMALL_GRAD_ROWS = 32
SMALL_OWN_ROWS = 16


def _whole(name, stacked):
    return jnp.concatenate(list(stacked), axis=1) if COL_SHARDED[name] else stacked.reshape(-1, stacked.shape[2])


def _pack_small(vals, sizes, rows):
    flat = jnp.concatenate([vals[n].reshape(-1) for n, _ in sizes])
    return jnp.pad(flat, (0, rows * PACK_COLS - flat.shape[0])).reshape(rows, PACK_COLS)


def _unpack_small(packed, sizes):
    flat, out, o = packed.reshape(-1), {}, 0
    for n, sz in sizes:
        out[n] = flat[o:o + sz]
        o += sz
    return out


def kernel(x, norm1_g, w_in, q_norm_g, k_norm_g, rel_bias, w_branch_a, w_branch_b, w_out, norm2_g, w_ffn_up, ffn_conv_w, ffn_conv_b, w_ffn_down, loss_target, m_norm1_g, m_w_in, m_q_norm_g, m_k_norm_g, m_rel_bias, m_w_branch_a, m_w_branch_b, m_w_out, m_norm2_g, m_w_ffn_up, m_ffn_conv_w, m_ffn_conv_b, m_w_ffn_down, v_norm1_g, v_w_in, v_q_norm_g, v_k_norm_g, v_rel_bias, v_w_branch_a, v_w_branch_b, v_w_out, v_norm2_g, v_w_ffn_up, v_ffn_conv_w, v_ffn_conv_b, v_w_ffn_down):
    w_big = {"w_in": w_in[0], "w_branch_a": w_branch_a[0], "w_branch_b": w_branch_b[0], "w_out": w_out[0],
             "w_ffn_up": w_ffn_up[0], "w_ffn_down": w_ffn_down[0]}
    m_big = {"w_in": m_w_in[0], "w_branch_a": m_w_branch_a[0], "w_branch_b": m_w_branch_b[0], "w_out": m_w_out[0],
             "w_ffn_up": m_w_ffn_up[0], "w_ffn_down": m_w_ffn_down[0]}
    v_big = {"w_in": v_w_in[0], "w_branch_a": v_w_branch_a[0], "w_branch_b": v_w_branch_b[0], "w_out": v_w_out[0],
             "w_ffn_up": v_w_ffn_up[0], "w_ffn_down": v_w_ffn_down[0]}
    xs, tgt = x[0], loss_target[0]

    xi, yi, ci = _place()
    chip = 2 * xi + yi

    def with_own(stacked, own):
        return lax.dynamic_update_slice(stacked, own[None], (chip,) + (0,) * own.ndim)

    shards_bf = {n: w_big[n].astype(BF16) for n in BIG}
    conv_own = jnp.pad(ffn_conv_w[0], ((0, 8 - ffn_conv_w.shape[1]), (0, 0)))
    later = [n for n in BIG if n != "w_in"]

    hn, hn_t, w_in_g = _rms_fwd(xs, norm1_g, "rms1", gather=_Gather([shards_bf["w_in"]]))
    w_in_f = _whole("w_in", with_own(w_in_g, shards_bf["w_in"]))
    w_in_t = w_in_f.T
    qk = _matmul(hn, w_in_f[:, :2 * WIDTH], F32, "proj_qk")
    vqkv = _matmul(hn, w_in_f[:, 2 * WIDTH:6 * WIDTH], BF16, "proj_vqkv")
    g_a = _matmul(hn, w_in_f[:, 6 * WIDTH:6 * WIDTH + D_MODEL], F32, "proj_gate_a")
    g_b = _matmul(hn, w_in_f[:, 6 * WIDTH + D_MODEL:], F32, "proj_gate_b")
    gq = jnp.tile(q_norm_g, (1, N_HEADS))
    gk = jnp.tile(k_norm_g, (1, N_HEADS))
    qa, ka = _qknorm_fwd(qk, gq, gk, "qknorm")
    per = WIDTH // PAIR
    b_offs = (per, 2 * per, 3 * per)
    tab = _bias_table(jnp.pad(rel_bias[0], ((0, 0), (0, REL_PAD - N_REL))), "bias_table")
    out_a, out_a_t, *gathered = _ca_fwd(qa, ka, vqkv, tab, "chunk_attn",
                                        gather=_Gather([shards_bf[n] for n in later], [conv_own]))
    full = {n: _whole(n, with_own(g, shards_bf[n])) for n, g in zip(later, gathered)}
    conv_w = jnp.concatenate(list(with_own(gathered[-1], conv_own)[:, :3]), axis=1)
    w_a, w_b, w_o, w_up, w_dn = (full[n] for n in later)
    w_a_t, w_b_t, w_o_t, w_up_t, w_dn_t = (w.T for w in (w_a, w_b, w_o, w_up, w_dn))
    out_b, out_b_f32, out_b_t = _sb_fwd(vqkv, vqkv, vqkv, "stick_attn", b_offs)
    y_a = _matmul(out_a, w_a, F32, "branch_a")
    y_b = _matmul(out_b, w_b, F32, "branch_b")
    mixed, mixed_t = _mix_fwd(g_a, g_b, y_a, y_b, "mix")
    x2 = _matmul(mixed, w_o, F32, "out_proj", residual=xs)
    hn2, hn2_t = _rms_fwd(x2, norm2_g, "rms2")
    hid = _matmul(hn2, w_up, BF16, "ffn_up")
    act, act_t = _convglu_fwd(hid, conv_w, ffn_conv_b, "convglu")
    y = _matmul(act, w_dn, F32, "ffn_down", residual=x2)
    dy, dyb, sq = _loss_head(y, tgt, "loss_head")
    loss = lax.psum(0.5 / D_MODEL * jnp.sum(sq), ("x", "y", "c"))

    dact = _matmul(dyb, w_dn_t, BF16, "d_act")
    d_w_dn = _matmul(act_t, dyb, F32, "d_w_down")
    dhg, dhu, dcwg, dcwu, dcbg, dcbu = _convglu_bwd(hid, dact, conv_w, ffn_conv_b, "convglu_bwd")
    half_chips = N_CHIPS // 2
    d_w_up = jnp.concatenate([_matmul(hn2_t, dhg, F32, "d_w_up_gate", slabs=half_chips),
                              _matmul(hn2_t, dhu, F32, "d_w_up_up", slabs=half_chips)], axis=0)
    dhn2 = _matmul(dhg, w_up_t[:D_FF], F32, "d_hn2_gate")
    dhn2 = _matmul(dhu, w_up_t[D_FF:], F32, "d_hn2_up", residual=dhn2)
    dx2, dx2b, d_norm2 = _rms_bwd(x2, norm2_g, dhn2, dy, "rms2_bwd")
    dmixed = _matmul(dx2b, w_o_t, F32, "d_mixed")
    d_w_o = _matmul(mixed_t, dx2b, F32, "d_w_out")
    dga, dgb, dya, dyb_b = _mix_bwd(dmixed, g_a, g_b, y_a, y_b, "mix_bwd")
    d_w_a, d_w_b = (_matmul(o_t, d, F32, nm).reshape(WIDTH, N_CHIPS, -1).transpose(1, 0, 2)
                    for o_t, d, nm in ((out_a_t, dya, "d_w_branch_a"), (out_b_t, dyb_b, "d_w_branch_b")))
    do_a = _matmul(dya, w_a_t, BF16, "d_out_a")
    do_b = _matmul(dyb_b, w_b_t, BF16, "d_out_b")

    def core_sums(names, gs, sibs):
        out = {}
        for n, g, sib in zip(names, gs, sibs):
            rh, cols = sib.shape[1], sib.shape[2]
            mine = lax.dynamic_slice_in_dim(g, ci * rh, rh, axis=1)
            out[n] = _add2(mine.reshape(-1, cols), sib.reshape(-1, cols), "sum_cores_" + n).reshape(sib.shape)
        return out

    grads_full = {"w_branch_a": d_w_a, "w_branch_b": d_w_b, "w_ffn_up": d_w_up,
                  "w_out": d_w_o.reshape(N_CHIPS, -1, D_MODEL), "w_ffn_down": d_w_dn.reshape(N_CHIPS, -1, D_MODEL)}
    early = [grads_full[n] for n in later]
    chip_parts = core_sums(later, early, _exchange_cores(early, "exchange_cores_early"))
    dqb, dkb, dvb, *parts_early = _sb_bwd(vqkv, vqkv, vqkv, out_b_f32, do_b, "stick_attn_bwd", b_offs,
                                           exchange=_ChipExchange([chip_parts[n] for n in later]))
    parts = dict(zip(later, parts_early))
    dqa_n, dka_n, dva, dtab = _ca_bwd(qa, ka, vqkv, do_a, tab, "chunk_attn_bwd")
    d_rel = _bias_table_bwd(dtab, "bias_table_bwd")[:, :N_REL]
    dqa, dka, dgq, dgk = _qknorm_bwd(qk, gq, gk, dqa_n, dka_n, "qknorm_bwd")
    dproj = jnp.concatenate([dqa, dka, dva, dqb, dkb, dvb, dga, dgb], axis=1)
    d_w_in = _matmul(hn_t, dproj, F32, "d_w_in", slabs=N_CHIPS)
    chip_parts.update(core_sums(["w_in"], [d_w_in], _exchange_cores([d_w_in], "exchange_cores_w_in")))
    dhn, parts["w_in"] = _matmul(dproj, w_in_t, F32, "d_hn", exchange=_ChipExchange([chip_parts["w_in"]]))
    dx, _, d_norm1 = _rms_bwd(xs, norm1_g, dhn, dx2, "rms1_bwd")

    small_g = _pack_small({"norm1_g": d_norm1, "q_norm_g": dgq.reshape(N_HEADS, HEAD_DIM).sum(0),
                           "k_norm_g": dgk.reshape(N_HEADS, HEAD_DIM).sum(0), "rel_bias": d_rel,
                           "norm2_g": d_norm2, "ffn_conv_b": jnp.concatenate([dcbg, dcbu], axis=1),
                           "ffn_conv_w": jnp.concatenate([dcwg, dcwu], axis=1)}, SMALL_GRADS, SMALL_GRAD_ROWS)
    (small_all,) = _exchange_cores([], "exchange_small", small=small_g)
    g_halves = [_sum_leading(with_own(parts[n], lax.dynamic_index_in_dim(chip_parts[n], chip, 0, keepdims=False)),
                             "sum_chips_" + n) for n in BIG]
    g_others = _share_halves(g_halves, "share_halves")
    grads = {n: jnp.concatenate([jnp.where(ci == 0, mine, other), jnp.where(ci == 0, other, mine)], axis=0)
             for n, mine, other in zip(BIG, g_halves, g_others)}
    small_all = lax.dynamic_update_slice(small_all, small_g[None], (4 * xi + 2 * yi + ci, 0, 0))
    small_sum = _unpack_small(_sum_leading(small_all, "sum_small"), SMALL_GRADS)
    small_sum["ffn_conv_w"] = lax.dynamic_slice_in_dim(small_sum["ffn_conv_w"].reshape(3, 2 * D_FF),
                                                       chip * CONV_W_COLS, CONV_W_COLS, axis=1)

    deltas, new_m, new_v = {}, {}, {}
    for n in BIG:
        deltas[n], new_m[n], new_v[n] = _adamw(w_big[n], grads[n], m_big[n], v_big[n], "adamw_" + n)

    shapes = {"norm1_g": norm1_g.shape, "q_norm_g": q_norm_g.shape, "k_norm_g": k_norm_g.shape,
              "rel_bias": rel_bias.shape, "norm2_g": norm2_g.shape, "ffn_conv_b": ffn_conv_b.shape,
              "ffn_conv_w": ffn_conv_w.shape}
    small_w = {"norm1_g": norm1_g, "q_norm_g": q_norm_g, "k_norm_g": k_norm_g, "rel_bias": rel_bias,
               "norm2_g": norm2_g, "ffn_conv_b": ffn_conv_b, "ffn_conv_w": ffn_conv_w}
    small_m = {"norm1_g": m_norm1_g, "q_norm_g": m_q_norm_g, "k_norm_g": m_k_norm_g, "rel_bias": m_rel_bias,
               "norm2_g": m_norm2_g, "ffn_conv_b": m_ffn_conv_b, "ffn_conv_w": m_ffn_conv_w}
    small_v = {"norm1_g": v_norm1_g, "q_norm_g": v_q_norm_g, "k_norm_g": v_k_norm_g, "rel_bias": v_rel_bias,
               "norm2_g": v_norm2_g, "ffn_conv_b": v_ffn_conv_b, "ffn_conv_w": v_ffn_conv_w}
    ds, ms, vs = _adamw(*(_pack_small(t, SMALL_OWN, SMALL_OWN_ROWS) for t in (small_w, small_sum, small_m, small_v)),
                        "adamw_small")
    small_grads = small_sum
    ds, ms, vs = (_unpack_small(t, SMALL_OWN) for t in (ds, ms, vs))

    order = ("norm1_g", "w_in", "q_norm_g", "k_norm_g", "rel_bias", "w_branch_a", "w_branch_b", "w_out",
             "norm2_g", "w_ffn_up", "ffn_conv_w", "ffn_conv_b", "w_ffn_down")
    outs = [loss, dx[None]]
    for big, small in ((grads, small_grads), (deltas, ds), (new_m, ms), (new_v, vs)):
        for n in order:
            outs.append(big[n][None] if n in big else small[n].reshape(shapes[n]))
    return tuple(outs)
```

```python
import functools

import jax
import jax.numpy as jnp
from jax import lax
from jax.experimental import pallas as pl
from jax.experimental.pallas import tpu as pltpu

F32 = jnp.float32
BF16 = jnp.bfloat16
MESH = pl.DeviceIdType.MESH

D_MODEL = 1024
HEAD_DIM = 64
N_HEADS = 8
WIDTH = N_HEADS * HEAD_DIM
CHUNK = 64
LEFT_CHUNKS = 8
MAX_REL = 128
N_REL = 2 * MAX_REL + 1
D_FF = 2816
EPS = 1e-6
NEG = -1e30

ADAM_LR = 0.001
ADAM_B1 = 0.9
ADAM_B2 = 0.999
ADAM_EPS = 1e-08
ADAM_WD = 0.01
ADAM_STEP = 10

N_CHIPS = 4
N_DEV = 8
LANES = 128
PAIR = 2 * HEAD_DIM
BQ = 256
BAND = LEFT_CHUNKS * CHUNK
KWIN = BAND + BQ
VMEM_LIMIT = 56 * 1024 * 1024
PACK_COLS = 1024

NN = (((1,), (0,)), ((), ()))
NT = (((1,), (1,)), ((), ()))
TN = (((0,), (0,)), ((), ()))


def _cparams(sem=None):
    if sem is None:
        return pltpu.CompilerParams(vmem_limit_bytes=VMEM_LIMIT)
    return pltpu.CompilerParams(dimension_semantics=sem, vmem_limit_bytes=VMEM_LIMIT)


def _pick(n, cands):
    for c in cands:
        if n % c == 0:
            return c
    raise ValueError(f"no block for {n}")


def _dot(a, b, dn):
    return lax.dot_general(a, b, dn, preferred_element_type=F32)


def _sigmoid(x):
    return 0.5 * jnp.tanh(0.5 * x) + 0.5


def _split_bf16(x):
    hi = x.astype(BF16)
    lo = (x - hi.astype(F32)).astype(BF16)
    return hi, lo


MM_RESIDENT_BYTES = 12 * 1024 * 1024
MM_TILE_BYTES = 4 * 1024 * 1024


def _matmul(a, b, out_dtype, name, residual=None, slabs=None, exchange=None):
    (M, K), N = a.shape, b.shape[1]
    out_bytes = jnp.dtype(out_dtype).itemsize
    if slabs is None and N <= D_FF and K * N * 2 <= MM_RESIDENT_BYTES:
        bk, bn = K, N
        bm = next(c for c in (1024, 512, 256, 128)
                  if M % c == 0 and c * K * 2 <= MM_TILE_BYTES and c * N * out_bytes <= MM_TILE_BYTES)
        b_spec = pl.BlockSpec((bk, bn), lambda i, j, k: (0, 0), pipeline_mode=pl.Buffered(1))
    elif slabs is None and K <= D_FF:
        bk, bm, bn = K, _pick(M, (1024, 512)), _pick(N, (D_FF // 2, 512, 256, 128))
        b_spec = pl.BlockSpec((bk, bn), lambda i, j, k: (k, j))
    else:
        bk = _pick(K, (1024, 512))
        bm = _pick(M, (D_FF // 2, 1024, 512, 256, 128))
        bn = N // slabs if slabs else _pick(N, (D_FF // 2, 1024, 512, 256, 128))
        b_spec = pl.BlockSpec((bk, bn), lambda i, j, k: (k, j))
    nk = K // bk
    dn = NN
    a_spec = pl.BlockSpec((bm, bk), lambda i, j, k: (i, k))
    if slabs:
        o_spec = pl.BlockSpec((None, bm, bn), lambda i, j, k: (j, i, 0))
        out_shape = jax.ShapeDtypeStruct((slabs, M, bn), out_dtype)
    else:
        o_spec = pl.BlockSpec((bm, bn), lambda i, j, k: (i, j))
        out_shape = jax.ShapeDtypeStruct((M, N), out_dtype)
    has_res = residual is not None
    n_in = 3 if has_res else 2
    n_side = len(exchange.arrays) if exchange else 0
    grid = (M // bm, N // bn, nk)

    def body(*refs):
        a_ref, b_ref = refs[:2]
        r_ref = refs[2] if has_res else None
        o_ref, acc_ref = refs[n_in + n_side], refs[n_in + 2 * n_side + 1]
        k = pl.program_id(2)
        if exchange:
            steps = [pl.program_id(d) for d in range(3)]
            start, finish_side = exchange.steps(refs[n_in:n_in + n_side], refs[n_in + n_side + 1:n_in + 2 * n_side + 1],
                                                refs[n_in + 2 * n_side + 2:])
            pl.when((steps[0] == 0) & (steps[1] == 0) & (steps[2] == 0))(start)
        part = _dot(a_ref[...], b_ref[...], dn)

        def finish(total):
            if has_res:
                total = total + r_ref[...]
            o_ref[...] = total.astype(out_dtype)

        if nk == 1:
            finish(part)
        else:
            @pl.when(k == 0)
            def _():
                acc_ref[...] = part

            @pl.when(k > 0)
            def _():
                acc_ref[...] += part

            @pl.when(k == nk - 1)
            def _():
                finish(acc_ref[...])

        if exchange:
            pl.when((steps[0] == grid[0] - 1) & (steps[1] == grid[1] - 1) & (steps[2] == grid[2] - 1))(finish_side)

    side = exchange.arrays if exchange else []
    in_specs = [a_spec, b_spec] + ([o_spec] if has_res else []) + [ANY] * n_side
    args = (a, b) + ((residual,) if has_res else ()) + tuple(side)
    out = pl.pallas_call(
        body, name=name, grid=grid,
        in_specs=in_specs, out_specs=[o_spec] + [ANY] * n_side,
        out_shape=[out_shape] + (exchange.out_shape if exchange else []),
        scratch_shapes=[pltpu.VMEM((bm, bn) if nk > 1 else (8, LANES), F32)] + (exchange.scratch if exchange else []),
        compiler_params=_cparams(("arbitrary",) * 3 if exchange else ("parallel", "parallel", "arbitrary")),
    )(*args)
    return out if exchange else out[0]


ROWS = 512


def _row_spec(cols, bm=ROWS):
    return pl.BlockSpec((bm, cols), lambda i: (i, 0))


def _col_spec(rows, bn=ROWS):
    return pl.BlockSpec((rows, bn), lambda i: (0, i))


def _full_spec(shape):
    return pl.BlockSpec(shape, lambda i: (0,) * len(shape))


def _colsum8(t):
    return jnp.sum(t.reshape(t.shape[0] // 8, 8, t.shape[1]), axis=0)


def _rms_fwd(x, g, name, gather=None):
    S, D = x.shape
    nt = S // ROWS
    n_side = len(gather.arrays) if gather else 0

    def body(*refs):
        x_ref, g_ref = refs[:2]
        o_ref, ot_ref = refs[2 + n_side:4 + n_side]
        i = pl.program_id(0)
        if gather:
            start, forward, finish = gather.steps(refs[2:2 + n_side], refs[4 + n_side:4 + 2 * n_side], refs[4 + 2 * n_side:])
            pl.when(i == 0)(start)
            pl.when(i == 3 * nt // 4)(forward)
        xv = x_ref[...]
        r = lax.rsqrt(jnp.mean(xv * xv, axis=-1, keepdims=True) + EPS)
        y = xv * r * g_ref[...]
        o_ref[...] = y.astype(BF16)
        ot_ref[...] = y.T.astype(BF16)
        if gather:
            pl.when(i == nt - 1)(finish)

    side = gather.arrays if gather else []
    return pl.pallas_call(
        body, name=name, grid=(nt,),
        in_specs=[_row_spec(D), _full_spec((1, D))] + [ANY] * n_side,
        out_specs=[_row_spec(D), _col_spec(D)] + [ANY] * n_side,
        out_shape=[jax.ShapeDtypeStruct((S, D), BF16), jax.ShapeDtypeStruct((D, S), BF16)]
        + (gather.out_shape if gather else []),
        scratch_shapes=gather.scratch if gather else [],
        compiler_params=_cparams(("arbitrary",) if gather else ("parallel",)),
    )(x, g, *side)


def _rms_bwd(x, g, dy, dres, name):
    S, D = x.shape
    nt = S // ROWS

    def body(x_ref, g_ref, dy_ref, dres_ref, dx_ref, dxb_ref, dg_ref, acc_ref):
        i = pl.program_id(0)
        xv, dyv = x_ref[...], dy_ref[...]
        r = lax.rsqrt(jnp.mean(xv * xv, axis=-1, keepdims=True) + EPS)
        xr = xv * r
        u = dyv * g_ref[...]
        dx = r * u - xr * (r * r) * jnp.mean(xv * u, axis=-1, keepdims=True) + dres_ref[...]
        dx_ref[...] = dx
        dxb_ref[...] = dx.astype(BF16)
        part = _colsum8(dyv * xr)

        @pl.when(i == 0)
        def _():
            acc_ref[...] = part

        @pl.when(i > 0)
        def _():
            acc_ref[...] += part

        @pl.when(i == nt - 1)
        def _():
            dg_ref[...] = jnp.sum(acc_ref[...], axis=0, keepdims=True)

    return pl.pallas_call(
        body, name=name, grid=(nt,),
        in_specs=[_row_spec(D), _full_spec((1, D)), _row_spec(D), _row_spec(D)],
        out_specs=[_row_spec(D), _row_spec(D), _full_spec((1, D))],
        out_shape=[jax.ShapeDtypeStruct((S, D), F32), jax.ShapeDtypeStruct((S, D), BF16),
                   jax.ShapeDtypeStruct((1, D), F32)],
        scratch_shapes=[pltpu.VMEM((8, D), F32)],
        compiler_params=_cparams(("arbitrary",)),
    )(x, g, dy, dres)


def _head_mean(t, blockdiag):
    hi, lo = _split_bf16(t)
    return (_dot(hi, blockdiag, NN) + _dot(lo, blockdiag, NN)) * (1.0 / HEAD_DIM)


def _blockdiag():
    r = lax.broadcasted_iota(jnp.int32, (WIDTH, WIDTH), 0) // HEAD_DIM
    c = lax.broadcasted_iota(jnp.int32, (WIDTH, WIDTH), 1) // HEAD_DIM
    return jnp.where(r == c, 1.0, 0.0).astype(BF16)


def _qknorm_fwd(qk, gq, gk, name):
    S = qk.shape[0]

    def body(qk_ref, gq_ref, gk_ref, q_ref, k_ref):
        bd = _blockdiag()
        for part, g_ref, o_ref, scale in ((0, gq_ref, q_ref, HEAD_DIM ** -0.5), (1, gk_ref, k_ref, 1.0)):
            t = qk_ref[:, part * WIDTH:(part + 1) * WIDTH]
            r = lax.rsqrt(_head_mean(t * t, bd) + EPS)
            o_ref[...] = (t * r * g_ref[...] * scale).astype(BF16)

    return pl.pallas_call(
        body, name=name, grid=(S // ROWS,),
        in_specs=[_row_spec(2 * WIDTH), _full_spec((1, WIDTH)), _full_spec((1, WIDTH))],
        out_specs=[_row_spec(WIDTH), _row_spec(WIDTH)],
        out_shape=[jax.ShapeDtypeStruct((S, WIDTH), BF16)] * 2,
        compiler_params=_cparams(("parallel",)),
    )(qk, gq, gk)


def _qknorm_bwd(qk, gq, gk, dqn, dkn, name):
    S = qk.shape[0]
    nt = S // ROWS

    def body(qk_ref, gq_ref, gk_ref, dqn_ref, dkn_ref, dq_ref, dk_ref, dgq_ref, dgk_ref, accq_ref, acck_ref):
        i = pl.program_id(0)
        bd = _blockdiag()
        for part, g_ref, dn_ref, o_ref, dg_ref, acc_ref, scale in (
                (0, gq_ref, dqn_ref, dq_ref, dgq_ref, accq_ref, HEAD_DIM ** -0.5),
                (1, gk_ref, dkn_ref, dk_ref, dgk_ref, acck_ref, 1.0)):
            t = qk_ref[:, part * WIDTH:(part + 1) * WIDTH]
            dn = dn_ref[...] * scale
            r = lax.rsqrt(_head_mean(t * t, bd) + EPS)
            u = dn * g_ref[...]
            dt = r * u - t * (r * r * r) * _head_mean(t * u, bd)
            o_ref[...] = dt.astype(BF16)
            psum = _colsum8(dn * t * r)

            @pl.when(i == 0)
            def _():
                acc_ref[...] = psum

            @pl.when(i > 0)
            def _():
                acc_ref[...] += psum

            @pl.when(i == nt - 1)
            def _():
                dg_ref[...] = jnp.sum(acc_ref[...], axis=0, keepdims=True)

    return pl.pallas_call(
        body, name=name, grid=(nt,),
        in_specs=[_row_spec(2 * WIDTH), _full_spec((1, WIDTH)), _full_spec((1, WIDTH)),
                  _row_spec(WIDTH), _row_spec(WIDTH)],
        out_specs=[_row_spec(WIDTH), _row_spec(WIDTH), _full_spec((1, WIDTH)), _full_spec((1, WIDTH))],
        out_shape=[jax.ShapeDtypeStruct((S, WIDTH), BF16)] * 2 + [jax.ShapeDtypeStruct((1, WIDTH), F32)] * 2,
        scratch_shapes=[pltpu.VMEM((8, WIDTH), F32)] * 2,
        compiler_params=_cparams(("arbitrary",)),
    )(qk, gq, gk, dqn, dkn)


def _gate_specs(D):
    return [pl.BlockSpec((ROWS, D), lambda i: (i, 0)), pl.BlockSpec((ROWS, D), lambda i: (i, 1))]


def _mix_fwd(gates, ya, yb, name):
    S, D = ya.shape

    def body(ga_ref, gb_ref, ya_ref, yb_ref, o_ref, ot_ref):
        ga, gb, yav, ybv = (r[...].astype(F32) for r in (ga_ref, gb_ref, ya_ref, yb_ref))
        m = _sigmoid(ga) * yav + _sigmoid(gb) * ybv
        o_ref[...] = m.astype(BF16)
        ot_ref[...] = m.T.astype(BF16)

    return pl.pallas_call(
        body, name=name, grid=(S // ROWS,),
        in_specs=_gate_specs(D) + [_row_spec(D)] * 2, out_specs=[_row_spec(D), _col_spec(D)],
        out_shape=[jax.ShapeDtypeStruct((S, D), BF16), jax.ShapeDtypeStruct((D, S), BF16)],
        compiler_params=_cparams(("parallel",)),
    )(gates, gates, ya, yb)


def _mix_bwd(dm, gates, ya, yb, name):
    S, D = ya.shape

    def body(dm_ref, ga_ref, gb_ref, ya_ref, yb_ref, dg_ref, dya_ref, dyb_ref):
        dmv = dm_ref[...].astype(F32)
        for half, (g_ref, y_ref, dy_ref) in enumerate(((ga_ref, ya_ref, dya_ref), (gb_ref, yb_ref, dyb_ref))):
            s = _sigmoid(g_ref[...].astype(F32))
            dy_ref[...] = (dmv * s).astype(BF16)
            dg_ref[:, half * D:(half + 1) * D] = (dmv * y_ref[...].astype(F32) * s * (1.0 - s)).astype(BF16)

    return pl.pallas_call(
        body, name=name, grid=(S // ROWS,),
        in_specs=[_row_spec(D)] + _gate_specs(D) + [_row_spec(D)] * 2,
        out_specs=[_row_spec(2 * D), _row_spec(D), _row_spec(D)],
        out_shape=[jax.ShapeDtypeStruct((S, 2 * D), BF16)] + [jax.ShapeDtypeStruct((S, D), BF16)] * 2,
        compiler_params=_cparams(("parallel",)),
    )(dm, gates, gates, ya, yb)


def _loss_head(y, target, name):
    S, D = y.shape
    nt = S // ROWS

    def body(y_ref, t_ref, dy_ref, dyb_ref, p_ref):
        err = y_ref[...] - t_ref[...]
        dy = err * (1.0 / D)
        dy_ref[...] = dy
        dyb_ref[...] = dy.astype(BF16)
        sq = _colsum8(err * err)
        acc = sq[:, 0:LANES]
        for k in range(1, D // LANES):
            acc = acc + sq[:, k * LANES:(k + 1) * LANES]
        p_ref[...] = acc

    return pl.pallas_call(
        body, name=name, grid=(nt,),
        in_specs=[_row_spec(D)] * 2,
        out_specs=[_row_spec(D), _row_spec(D), pl.BlockSpec((8, LANES), lambda i: (i, 0))],
        out_shape=[jax.ShapeDtypeStruct((S, D), F32), jax.ShapeDtypeStruct((S, D), BF16),
                   jax.ShapeDtypeStruct((nt * 8, LANES), F32)],
        compiler_params=_cparams(("parallel",)),
    )(y, target)


CONV_COLS = D_FF // 2
HALO = 16
CONV_CHUNK = 64


def _aligned(start, multiple):
    return start if isinstance(start, int) else pl.multiple_of(start, multiple)


def _conv_taps(xe, cw, cb):
    taps = (pltpu.roll(xe, 2, 0), pltpu.roll(xe, 1, 0), xe)
    return taps, cw[0:1] * taps[0] + cw[1:2] * taps[1] + cw[2:3] * taps[2] + cb


def _conv_specs(nt):
    hb, nb = ROWS // HALO, D_FF // CONV_COLS
    specs = {}
    for part, off in (("gate", 0), ("up", nb)):
        specs[part] = dict(
            main=pl.BlockSpec((ROWS, CONV_COLS), functools.partial(lambda c, i, off: (i, c + off), off=off)),
            prev=pl.BlockSpec((HALO, CONV_COLS),
                              functools.partial(lambda c, i, off: (jnp.maximum(i * hb - 1, 0), c + off), off=off)),
            nxt=pl.BlockSpec((HALO, CONV_COLS),
                             functools.partial(lambda c, i, off: (jnp.minimum((i + 1) * hb, nt * hb - 1), c + off), off=off)),
            w=pl.BlockSpec((3, CONV_COLS), functools.partial(lambda c, i, off: (0, c + off), off=off)),
            b=pl.BlockSpec((1, CONV_COLS), functools.partial(lambda c, i, off: (0, c + off), off=off)))
    return specs


def _convglu_fwd(hid, cw, cb, name):
    S = hid.shape[0]
    sp = _conv_specs(S // ROWS)

    def body(hg_ref, hgp_ref, hu_ref, hup_ref, cwg_ref, cwu_ref, cbg_ref, cbu_ref, o_ref, ot_ref):
        i = pl.program_id(1)
        keep = (i > 0).astype(F32)

        def conv(h_ref, hp_ref, cw_ref, cb_ref):
            xe = jnp.concatenate([hp_ref[...].astype(F32) * keep, h_ref[...].astype(F32)], axis=0)
            return _conv_taps(xe, cw_ref[...], cb_ref[...])[1][HALO:, :]

        gate = conv(hg_ref, hgp_ref, cwg_ref, cbg_ref)
        up = conv(hu_ref, hup_ref, cwu_ref, cbu_ref)
        act = gate * _sigmoid(gate) * up
        o_ref[...] = act.astype(BF16)
        ot_ref[...] = act.T.astype(BF16)

    g, u = sp["gate"], sp["up"]
    return pl.pallas_call(
        body, name=name, grid=(D_FF // CONV_COLS, S // ROWS),
        in_specs=[g["main"], g["prev"], u["main"], u["prev"], g["w"], u["w"], g["b"], u["b"]],
        out_specs=[g["main"], pl.BlockSpec((CONV_COLS, ROWS), lambda c, i: (c, i))],
        out_shape=[jax.ShapeDtypeStruct((S, D_FF), BF16), jax.ShapeDtypeStruct((D_FF, S), BF16)],
        compiler_params=_cparams(("parallel", "parallel")),
    )(hid, hid, hid, hid, cw, cw, cb, cb)


def _convglu_bwd(hid, dact, cw, cb, name):
    S = hid.shape[0]
    nt = S // ROWS
    sp = _conv_specs(nt)

    n_chunks = ROWS // CONV_CHUNK

    def body(hg_ref, hgp_ref, hgn_ref, hu_ref, hup_ref, hun_ref, da_ref, dan_ref,
             cwg_ref, cwu_ref, cbg_ref, cbu_ref,
             dhg_ref, dhu_ref, dcwg_ref, dcwu_ref, dcbg_ref, dcbu_ref, xg_s, xu_s, da_s):
        i = pl.program_id(1)
        kp = (i > 0).astype(F32)
        kn = (i < nt - 1).astype(F32)
        for x_s, h_ref, hp_ref, hn_ref in ((xg_s, hg_ref, hgp_ref, hgn_ref), (xu_s, hu_ref, hup_ref, hun_ref)):
            x_s[0:HALO, :] = hp_ref[...].astype(F32) * kp
            x_s[HALO:HALO + ROWS, :] = h_ref[...].astype(F32)
            x_s[HALO + ROWS:, :] = hn_ref[...].astype(F32) * kn
        da_s[0:ROWS, :] = da_ref[...].astype(F32)
        da_s[ROWS:, :] = dan_ref[...].astype(F32) * kn

        @pl.when(i == 0)
        def _():
            for ref in (dcwg_ref, dcwu_ref, dcbg_ref, dcbu_ref):
                ref[...] = jnp.zeros_like(ref)

        def lane_group(grp, _):
            lanes = pl.ds(pl.multiple_of(grp * LANES, LANES), LANES)
            cwg, cwu, cbg, cbu = cwg_ref[:, lanes], cwu_ref[:, lanes], cbg_ref[:, lanes], cbu_ref[:, lanes]

            def grads(r0, n):
                rows = pl.ds(_aligned(r0 + HALO - 8, 8), n + 8)
                taps_g, gate = _conv_taps(xg_s[rows, lanes], cwg, cbg)
                taps_u, up = _conv_taps(xu_s[rows, lanes], cwu, cbu)
                gate, up = gate[8:], up[8:]
                da = da_s[pl.ds(_aligned(r0, 8), n), lanes]
                sg = _sigmoid(gate)
                return (da * up * sg * (1.0 + gate * (1.0 - sg)), da * gate * sg,
                        [t[8:] for t in taps_g], [t[8:] for t in taps_u])

            def chunk(step, carry):
                below_g, below_u, accs = carry
                r0 = (n_chunks - 1 - step) * CONV_CHUNK
                dg, du, taps_g, taps_u = grads(r0, CONV_CHUNK)
                new_accs = []
                for d, below, cwv, taps, dh_ref, acc in ((dg, below_g, cwg, taps_g, dhg_ref, accs[0]),
                                                        (du, below_u, cwu, taps_u, dhu_ref, accs[1])):
                    ext = jnp.concatenate([d, below], axis=0)
                    n_ext = CONV_CHUNK + 8
                    dh = (cwv[2:3] * d + cwv[1:2] * pltpu.roll(ext, n_ext - 1, 0)[:CONV_CHUNK]
                          + cwv[0:1] * pltpu.roll(ext, n_ext - 2, 0)[:CONV_CHUNK])
                    dh_ref[pl.ds(_aligned(r0, CONV_CHUNK), CONV_CHUNK), lanes] = dh.astype(BF16)
                    new_accs.append(tuple(a + _colsum8(d * tap) for a, tap in zip(acc[:3], taps))
                                    + (acc[3] + _colsum8(d),))
                return dg[0:8], du[0:8], tuple(new_accs)

            below_g, below_u, _, _ = grads(ROWS, 8)
            zero = jnp.zeros((8, LANES), F32)
            _, _, accs = lax.fori_loop(0, n_chunks, chunk, (below_g, below_u, ((zero,) * 4, (zero,) * 4)))
            for acc, dcw_ref, dcb_ref in ((accs[0], dcwg_ref, dcbg_ref), (accs[1], dcwu_ref, dcbu_ref)):
                for t in range(3):
                    dcw_ref[t:t + 1, lanes] += jnp.sum(acc[t], axis=0, keepdims=True)
                dcb_ref[:, lanes] += jnp.sum(acc[3], axis=0, keepdims=True)
            return 0

        lax.fori_loop(0, CONV_COLS // LANES, lane_group, 0)

    g, u = sp["gate"], sp["up"]
    return pl.pallas_call(
        body, name=name, grid=(D_FF // CONV_COLS, nt),
        in_specs=[g["main"], g["prev"], g["nxt"], u["main"], u["prev"], u["nxt"], g["main"], g["nxt"],
                  g["w"], u["w"], g["b"], u["b"]],
        out_specs=[g["main"], g["main"], g["w"], g["w"], g["b"], g["b"]],
        out_shape=[jax.ShapeDtypeStruct((S, D_FF), BF16)] * 2 + [jax.ShapeDtypeStruct((3, D_FF), F32)] * 2
        + [jax.ShapeDtypeStruct((1, D_FF), F32)] * 2,
        scratch_shapes=[pltpu.VMEM((ROWS + 2 * HALO, CONV_COLS), F32)] * 2 + [pltpu.VMEM((ROWS + HALO, CONV_COLS), F32)],
        compiler_params=_cparams(("parallel", "arbitrary")),
    )(hid, hid, hid, hid, hid, hid, dact, dact, cw, cw, cb, cb)


REL_PAD = 384
DIAG = 1024


def _band_valid():
    qc = lax.broadcasted_iota(jnp.int32, (BQ, KWIN), 0) // CHUNK
    kc = lax.broadcasted_iota(jnp.int32, (BQ, KWIN), 1) // CHUNK - LEFT_CHUNKS
    return (kc <= qc) & (kc >= qc - LEFT_CHUNKS)


def _rel_index(offset):
    return jnp.clip(BAND - offset, -MAX_REL, MAX_REL) + MAX_REL


def _split3(x):
    hi = x.astype(BF16)
    rest = x - hi.astype(F32)
    mid = rest.astype(BF16)
    return hi, mid, (rest - mid.astype(F32)).astype(BF16)


def _bias_table(rel_bias, name):
    def body(rb_ref, o_ref):
        t = lax.broadcasted_iota(jnp.int32, (REL_PAD, DIAG), 0)
        lane = lax.broadcasted_iota(jnp.int32, (REL_PAD, DIAG), 1)
        pick = jnp.where(t == _rel_index(lane - BQ), 1.0, 0.0).astype(BF16)
        base = sum(_dot(piece, pick, NN) for piece in _split3(rb_ref[...]))
        valid = _band_valid()
        for h in range(N_HEADS):
            rows = jnp.broadcast_to(base[h:h + 1], (BQ, DIAG))
            rolled = pltpu.roll(rows, 0, 1, stride=1, stride_axis=0)
            o_ref[h] = jnp.where(valid, rolled[:, BQ:], NEG)

    return pl.pallas_call(
        body, name=name,
        out_shape=jax.ShapeDtypeStruct((N_HEADS, BQ, KWIN), F32),
        compiler_params=_cparams(),
    )(rel_bias)


def _bias_table_bwd(dtab, name):
    def body(d_ref, o_ref, diag_ref):
        r = lax.broadcasted_iota(jnp.int32, (BQ, BQ), 0)
        c = lax.broadcasted_iota(jnp.int32, (BQ, BQ), 1)
        flip = jnp.where(r + c == BQ - 1, 1.0, 0.0).astype(BF16)
        for h in range(N_HEADS):
            flipped = sum(_dot(flip, piece, NN) for piece in _split3(d_ref[h]))
            padded = jnp.concatenate([flipped, jnp.zeros((BQ, DIAG - KWIN), F32)], axis=1)
            rolled = pltpu.roll(padded, DIAG - (BQ - 1), 1, stride=1, stride_axis=0)
            diag_ref[h:h + 1, :] = jnp.sum(rolled, axis=0, keepdims=True)
        lane = lax.broadcasted_iota(jnp.int32, (DIAG, REL_PAD), 0)
        t = lax.broadcasted_iota(jnp.int32, (DIAG, REL_PAD), 1)
        offset = jnp.where(lane < KWIN, lane, lane - DIAG)
        pick = jnp.where(t == _rel_index(offset), 1.0, 0.0).astype(BF16)
        o_ref[...] = sum(_dot(piece, pick, NN) for piece in _split3(diag_ref[...]))

    return pl.pallas_call(
        body, name=name,
        out_shape=jax.ShapeDtypeStruct((N_HEADS, REL_PAD), F32),
        scratch_shapes=[pltpu.VMEM((N_HEADS, DIAG), F32)],
        compiler_params=_cparams(),
    )(dtab)


def _head_masks(heads=2):
    lane = lax.broadcasted_iota(jnp.int32, (1, heads * HEAD_DIM), 1)
    return [lane // HEAD_DIM == h for h in range(heads)]


def _own_lanes(masks, vals):
    out = vals[-1]
    for m, val in zip(masks[-2::-1], vals[-2::-1]):
        out = jnp.where(m, val, out)
    return out


CA_HEADS = 4
CA_LANES = CA_HEADS * HEAD_DIM


def _ca_window_specs(nq, col_off=0):
    return [pl.BlockSpec((BQ, CA_LANES), functools.partial(
        lambda p, i, d: (jnp.clip(i - 2 + d, 0, nq - 1), p + col_off), d=d)) for d in range(3)]


def _softmax_rows(s):
    p = jnp.exp(s - jnp.max(s, axis=-1, keepdims=True))
    return p, jnp.sum(p, axis=-1, keepdims=True)


def _ca_scores(qm, kc, tab_h, i):
    col = lax.broadcasted_iota(jnp.int32, (1, KWIN), 1)
    in_seq = col + (i - 2) * BQ >= 0
    return jnp.where(in_seq, _dot(qm, kc, NT) + tab_h, NEG)


def _ca_fwd(qn, kn, v, tab, name, v_off=0, gather=None):
    S = qn.shape[0]
    nq = S // BQ
    groups = WIDTH // CA_LANES
    qspec = pl.BlockSpec((BQ, CA_LANES), lambda p, i: (i, p))
    tspec = pl.BlockSpec((CA_HEADS, BQ, KWIN), lambda p, i: (p, 0, 0))
    n_side = len(gather.arrays) if gather else 0

    def body(*refs):
        q_ref, k0, k1, k2, v0, v1, v2, tab_ref = refs[:8]
        o_ref, ot_ref = refs[8 + n_side:10 + n_side]
        p, i = pl.program_id(0), pl.program_id(1)
        if gather:
            start, forward, finish = gather.steps(refs[8:8 + n_side], refs[10 + n_side:10 + 2 * n_side],
                                                  refs[10 + 2 * n_side:])
            pl.when((p == 0) & (i == 0))(start)
            pl.when((p == groups - 1) & (i == nq // 2))(forward)
        kc = jnp.concatenate([k0[...], k1[...], k2[...]], axis=0)
        vc = jnp.concatenate([v0[...], v1[...], v2[...]], axis=0)
        qv = q_ref[...]
        masks = _head_masks(CA_HEADS)
        heads = range(CA_HEADS)
        s = [_ca_scores(jnp.where(masks[h], qv, 0), kc, tab_ref[h], i) for h in heads]
        soft = [_softmax_rows(s[h]) for h in heads]
        o = [_dot(soft[h][0].astype(BF16), vc, NN) / soft[h][1] for h in heads]
        out = _own_lanes(masks, o)
        o_ref[...] = out.astype(BF16)
        ot_ref[...] = out.T.astype(BF16)
        if gather:
            pl.when((p == groups - 1) & (i == nq - 1))(finish)

    side = gather.arrays if gather else []
    return pl.pallas_call(
        body, name=name, grid=(groups, nq),
        in_specs=[qspec] + _ca_window_specs(nq) + _ca_window_specs(nq, v_off) + [tspec] + [ANY] * n_side,
        out_specs=[qspec, pl.BlockSpec((CA_LANES, BQ), lambda p, i: (p, i))] + [ANY] * n_side,
        out_shape=[jax.ShapeDtypeStruct((S, WIDTH), BF16), jax.ShapeDtypeStruct((WIDTH, S), BF16)]
        + (gather.out_shape if gather else []),
        scratch_shapes=gather.scratch if gather else [],
        compiler_params=_cparams(("arbitrary", "arbitrary") if gather else ("parallel", "parallel")),
    )(qn, kn, kn, kn, v, v, v, tab, *side)


def _ca_bwd(qn, kn, v, do, tab, name, v_off=0):
    S = qn.shape[0]
    nq = S // BQ
    qspec = pl.BlockSpec((BQ, CA_LANES), lambda p, i: (jnp.minimum(i, nq - 1), p))
    kout = pl.BlockSpec((BQ, CA_LANES), lambda p, i: (jnp.clip(i - 2, 0, nq - 1), p))
    tspec = pl.BlockSpec((CA_HEADS, BQ, KWIN), lambda p, i: (p, 0, 0))

    def body(q_ref, do_ref, k0, k1, k2, v0, v1, v2, tab_ref,
             dq_ref, dk_ref, dv_ref, dtab_ref, dk_acc, dv_acc):
        i = pl.program_id(1)

        @pl.when(i == 0)
        def _():
            dk_acc[...] = jnp.zeros_like(dk_acc)
            dv_acc[...] = jnp.zeros_like(dv_acc)
            dtab_ref[...] = jnp.zeros_like(dtab_ref)

        @pl.when(i < nq)
        def _():
            kc = jnp.concatenate([k0[...], k1[...], k2[...]], axis=0)
            vc = jnp.concatenate([v0[...], v1[...], v2[...]], axis=0)
            qv, dov = q_ref[...], do_ref[...]
            masks = _head_masks(CA_HEADS)
            heads = range(CA_HEADS)
            qm = [jnp.where(masks[h], qv, 0) for h in heads]
            dom = [jnp.where(masks[h], dov, 0) for h in heads]
            s = [_ca_scores(qm[h], kc, tab_ref[h], i) for h in heads]
            dp = [_dot(dom[h], vc, NT) for h in heads]
            soft = [_softmax_rows(s[h]) for h in heads]
            p = [soft[h][0] / soft[h][1] for h in heads]
            ds = [p[h] * (dp[h] - jnp.sum(p[h] * dp[h], axis=-1, keepdims=True)) for h in heads]
            for h in heads:
                dtab_ref[h] += ds[h]
            dsb = [ds[h].astype(BF16) for h in heads]
            pb = [p[h].astype(BF16) for h in heads]
            dq = [_dot(dsb[h], kc, NN) for h in heads]
            dq_ref[...] = _own_lanes(masks, dq)
            dkc = sum(_dot(dsb[h], qm[h], TN) for h in heads)
            dvc = sum(_dot(pb[h], dom[h], TN) for h in heads)
            for d in range(3):
                slot = (i + 1 + d) % 3
                dk_acc[slot] += dkc[d * BQ:(d + 1) * BQ]
                dv_acc[slot] += dvc[d * BQ:(d + 1) * BQ]

        @pl.when(i >= 2)
        def _():
            slot = (i + 1) % 3
            dk_ref[...] = dk_acc[slot]
            dv_ref[...] = dv_acc[slot].astype(BF16)
            dk_acc[slot] = jnp.zeros((BQ, CA_LANES), F32)
            dv_acc[slot] = jnp.zeros((BQ, CA_LANES), F32)

    return pl.pallas_call(
        body, name=name, grid=(WIDTH // CA_LANES, nq + 2),
        in_specs=[qspec, qspec] + _ca_window_specs(nq) + _ca_window_specs(nq, v_off) + [tspec],
        out_specs=[qspec, kout, kout, tspec],
        out_shape=[jax.ShapeDtypeStruct((S, WIDTH), F32), jax.ShapeDtypeStruct((S, WIDTH), F32),
                   jax.ShapeDtypeStruct((S, WIDTH), BF16), jax.ShapeDtypeStruct((N_HEADS, BQ, KWIN), F32)],
        scratch_shapes=[pltpu.VMEM((3, BQ, CA_LANES), F32)] * 2,
        compiler_params=_cparams(("parallel", "arbitrary")),
    )(qn, do, kn, kn, kn, v, v, v, tab)


def _sb_consts():
    r = lax.broadcasted_iota(jnp.int32, (BQ, BQ), 0)
    c = lax.broadcasted_iota(jnp.int32, (BQ, BQ), 1)
    from_s = jnp.where(r >= c, 1.0, 0.0).astype(BF16)
    causal = c < r
    return from_s, causal


def _suffix_sum(t, from_s):
    hi, lo = _split_bf16(t)
    return _dot(hi, from_s, NN) + _dot(lo, from_s, NN)


def _neg_abs(x):
    bits = lax.bitcast_convert_type(x, jnp.uint32) | jnp.uint32(0x80000000)
    return lax.bitcast_convert_type(bits, F32)


def _sb_log_keep(zn):
    return jnp.minimum(zn, 0.0) - jnp.log(1.0 + jnp.exp(_neg_abs(zn)))


SB_DEAD = 105.0


SB_QB = 2


def _sb_walk(ip, tiles, keep_ref):
    i0 = SB_QB * ip

    @pl.when(ip == 0)
    def _():
        tiles([(0, [0], [True]), (1, [1, 0], [True, False])])

    @pl.when(ip > 0)
    def _():
        tiles([(a, [i0 + a, i0 + a - 1], [True, False]) for a in range(SB_QB)])

    for a in range(SB_QB):
        def alive(a=a):
            return (jnp.max(keep_ref[2 * a:2 * a + 2]) > -SB_DEAD).astype(jnp.int32)

        def step(state, a=a, alive=alive):
            j, _ = state
            tiles([(a, [j], [False])])
            return j - 1, alive()

        lax.while_loop(lambda state: (state[0] >= 0) & (state[1] > 0), step, (i0 + a - 2, alive()))


def _sb_rows(j):
    return pl.ds(pl.multiple_of(j * BQ, BQ), BQ)


def _sb_chains(groups):
    chains = [(a, n, h) for a, js, _ in groups for n in range(len(js)) for h in range(2)]
    block = {(a, n): j for a, js, _ in groups for n, j in enumerate(js)}
    masked = [(a, n, h) for a, _, diags in groups for n, d in enumerate(diags) if d for h in range(2)]
    return chains, block, masked


def _sb_running(ref, vals, groups):
    before_chain = {}
    for a, js, _ in groups:
        for h in range(2):
            run = ref[2 * a + h]
            for n in range(len(js)):
                before_chain[(a, n, h)] = run
                run = run + jnp.sum(vals[(a, n, h)], axis=-1, keepdims=True)
            ref[2 * a + h] = run
    return before_chain


def _sb_specs(S, offs):
    def qspec(off=0):
        return pl.BlockSpec((SB_QB * BQ, PAIR), lambda p, i: (i, p + off))

    def kspec(off=0):
        return pl.BlockSpec((S, PAIR), lambda p, i: (0, p + off), pipeline_mode=pl.Buffered(1))

    return qspec, kspec, [qspec(offs[0]), kspec(offs[1]), kspec(offs[2])]


def _sb_fwd(q, k, v, name, offs=(0, 0, 0)):
    S = q.shape[0]
    steps = S // (SB_QB * BQ)
    qspec, _, qkv_specs = _sb_specs(S, offs)

    def body(q_ref, k_ref, v_ref, o_ref, of_ref, ot_ref, carry_ref, acc_ref):
        ip = pl.program_id(1)
        from_s, causal = _sb_consts()
        masks = _head_masks()
        qn = q_ref[...] * -(HEAD_DIM ** -0.5)
        qms = {(a, h): jnp.where(masks[h], qn[a * BQ:(a + 1) * BQ], 0) for a in range(SB_QB) for h in range(2)}
        carry_ref[...] = jnp.zeros_like(carry_ref)
        acc_ref[...] = jnp.zeros_like(acc_ref)

        def tiles(groups):
            chains, block, masked = _sb_chains(groups)
            kbs = {an: k_ref[_sb_rows(j), :] for an, j in block.items()}
            vbs = {an: v_ref[_sb_rows(j), :] for an, j in block.items()}
            zn = {c: _dot(qms[(c[0], c[2])], kbs[c[:2]], NT) for c in chains}
            log_keep = {c: _sb_log_keep(zn[c]) for c in chains}
            for c in masked:
                log_keep[c] = jnp.where(causal, log_keep[c], 0.0)
            split = {c: _split_bf16(log_keep[c]) for c in chains}
            carry = _sb_running(carry_ref, log_keep, groups)
            suffix = {c: _dot(split[c][0], from_s, NN) + _dot(split[c][1], from_s, NN) for c in chains}
            w = {c: jnp.exp(carry[c] + suffix[c] - zn[c]) for c in chains}
            for c in masked:
                w[c] = jnp.where(causal, w[c], 0.0)
            for c in chains:
                acc_ref[2 * c[0] + c[2]] += _dot(w[c].astype(BF16), vbs[c[:2]], NN)

        _sb_walk(ip, tiles, carry_ref)
        for a in range(SB_QB):
            out = jnp.where(masks[0], acc_ref[2 * a], acc_ref[2 * a + 1])
            o_ref[a * BQ:(a + 1) * BQ, :] = out.astype(BF16)
            of_ref[a * BQ:(a + 1) * BQ, :] = out
            ot_ref[:, a * BQ:(a + 1) * BQ] = out.T.astype(BF16)

    return pl.pallas_call(
        body, name=name, grid=(WIDTH // PAIR, steps),
        in_specs=qkv_specs, out_specs=[qspec(), qspec(), pl.BlockSpec((PAIR, SB_QB * BQ), lambda p, i: (p, i))],
        out_shape=[jax.ShapeDtypeStruct((S, WIDTH), BF16), jax.ShapeDtypeStruct((S, WIDTH), F32),
                   jax.ShapeDtypeStruct((WIDTH, S), BF16)],
        scratch_shapes=[pltpu.VMEM((2 * SB_QB, BQ, 1), F32), pltpu.VMEM((2 * SB_QB, BQ, PAIR), F32)],
        compiler_params=_cparams(("parallel", "arbitrary")),
    )(q, k, v)


def _sb_bwd(q, k, v, o, do, name, offs=(0, 0, 0), exchange=None):
    S = q.shape[0]
    steps = S // (SB_QB * BQ)
    pairs = WIDTH // PAIR
    qspec, kspec, qkv_specs = _sb_specs(S, offs)
    n_side = len(exchange.arrays) if exchange else 0

    def body(*refs):
        q_ref, o_ref, do_ref, k_ref, v_ref = refs[:5]
        dq_ref, dk_ref, dv_ref = refs[5 + n_side:8 + n_side]
        dk_acc, dv_acc, keep_ref, gsum_ref, dq_acc = refs[8 + 2 * n_side:13 + 2 * n_side]
        ip = pl.program_id(1)
        if exchange:
            start, finish = exchange.steps(refs[5:5 + n_side], refs[8 + n_side:8 + 2 * n_side], refs[13 + 2 * n_side:])
            pl.when((pl.program_id(0) == 0) & (ip == 0))(start)

        @pl.when(ip == 0)
        def _():
            dk_acc[...] = jnp.zeros_like(dk_acc)
            dv_acc[...] = jnp.zeros_like(dv_acc)

        from_s, causal = _sb_consts()
        masks = _head_masks()
        qn, dov = q_ref[...] * -(HEAD_DIM ** -0.5), do_ref[...]
        od = o_ref[...] * dov.astype(F32)
        lanes = [(a, h) for a in range(SB_QB) for h in range(2)]
        rows_of = {a: slice(a * BQ, (a + 1) * BQ) for a in range(SB_QB)}
        qms = {(a, h): jnp.where(masks[h], qn[rows_of[a]], 0) for a, h in lanes}
        doms = {(a, h): jnp.where(masks[h], dov[rows_of[a]], 0) for a, h in lanes}
        totals = {(a, h): jnp.sum(jnp.where(masks[h], od[rows_of[a]], 0.0), axis=-1, keepdims=True)
                  for a, h in lanes}
        for ref in (keep_ref, gsum_ref, dq_acc):
            ref[...] = jnp.zeros_like(ref)

        def tiles(groups):
            chains, block, masked = _sb_chains(groups)
            kbs = {an: k_ref[_sb_rows(j), :] for an, j in block.items()}
            vbs = {an: v_ref[_sb_rows(j), :] for an, j in block.items()}
            zn = {c: _dot(qms[(c[0], c[2])], kbs[c[:2]], NT) for c in chains}
            dw = {c: _dot(doms[(c[0], c[2])], vbs[c[:2]], NT) for c in chains}
            log_keep = {c: _sb_log_keep(zn[c]) for c in chains}
            for c in masked:
                log_keep[c] = jnp.where(causal, log_keep[c], 0.0)
            split = {c: _split_bf16(log_keep[c]) for c in chains}
            kept = _sb_running(keep_ref, log_keep, groups)
            suffix = {c: _dot(split[c][0], from_s, NN) + _dot(split[c][1], from_s, NN) for c in chains}
            w = {c: jnp.exp(kept[c] + suffix[c] - zn[c]) for c in chains}
            for c in masked:
                w[c] = jnp.where(causal, w[c], 0.0)
            wb = {c: w[c].astype(BF16) for c in chains}
            g = {c: wb[c].astype(F32) * dw[c] for c in chains}
            gsplit = {c: _split_bf16(g[c]) for c in chains}
            gsum = _sb_running(gsum_ref, g, groups)
            gsuffix = {c: _dot(gsplit[c][0], from_s, NN) + _dot(gsplit[c][1], from_s, NN) for c in chains}
            dzb = {}
            for c in chains:
                before = totals[(c[0], c[2])] - (gsum[c] + gsuffix[c])
                dz = (g[c] + before) * jnp.exp(log_keep[c]) - before
                if c in masked:
                    dz = jnp.where(causal, dz, 0.0)
                dzb[c] = dz.astype(BF16)
            for c in chains:
                rows = _sb_rows(block[c[:2]])
                dq_acc[2 * c[0] + c[2]] += _dot(dzb[c], kbs[c[:2]], NN)
                dk_acc[rows, :] -= _dot(dzb[c], qms[(c[0], c[2])], TN)
                dv_acc[rows, :] += _dot(wb[c], doms[(c[0], c[2])], TN)

        _sb_walk(ip, tiles, keep_ref)
        for a in range(SB_QB):
            dq = jnp.where(masks[0], dq_acc[2 * a], dq_acc[2 * a + 1])
            dq_ref[a * BQ:(a + 1) * BQ, :] = (dq * HEAD_DIM ** -0.5).astype(BF16)

        @pl.when(ip == steps - 1)
        def _():
            dk_ref[...] = dk_acc[...].astype(BF16)
            dv_ref[...] = dv_acc[...].astype(BF16)

        if exchange:
            pl.when((pl.program_id(0) == pairs - 1) & (ip == steps - 1))(finish)

    side = exchange.arrays if exchange else []
    return pl.pallas_call(
        body, name=name, grid=(pairs, steps),
        in_specs=[qkv_specs[0], qspec(), qspec(), qkv_specs[1], qkv_specs[2]] + [ANY] * n_side,
        out_specs=[qspec(), kspec(), kspec()] + [ANY] * n_side,
        out_shape=[jax.ShapeDtypeStruct((S, WIDTH), BF16)] * 3 + (exchange.out_shape if exchange else []),
        scratch_shapes=[pltpu.VMEM((S, PAIR), F32)] * 2 + [pltpu.VMEM((2 * SB_QB, BQ, 1), F32)] * 2
        + [pltpu.VMEM((2 * SB_QB, BQ, PAIR), F32)] + (exchange.scratch if exchange else []),
        compiler_params=_cparams(("arbitrary", "arbitrary") if exchange else ("parallel", "arbitrary")),
    )(q, o, do, k, v, *side)


ANY = pl.BlockSpec(memory_space=pl.ANY)


def _place():
    return lax.axis_index("x"), lax.axis_index("y"), lax.axis_index("c")


def _other_chips(x, y):
    return [(2 * px + py, (px, py)) for px, py in ((1 - x, y), (x, 1 - y), (1 - x, 1 - y))]


def _remote(src, dst, sems, k, to):
    return pltpu.make_async_remote_copy(src_ref=src, dst_ref=dst, send_sem=sems[0].at[k], recv_sem=sems[1].at[k],
                                        device_id=to, device_id_type=MESH)


class _Gather:
    def __init__(self, ws, extras=()):
        self.n, self.m = len(ws), len(extras)
        self.arrays = list(ws) + list(extras)
        self.n_copies = 6 * self.n + 3 * self.m
        self.out_shape = [jax.ShapeDtypeStruct((N_CHIPS,) + a.shape, a.dtype) for a in self.arrays]
        self.scratch = [pltpu.SemaphoreType.DMA((self.n_copies,)), pltpu.SemaphoreType.DMA((self.n_copies,))]

    def steps(self, in_refs, out_refs, sems):
        n = self.n
        x, y, c = _place()
        me = 2 * x + y
        chips = _other_chips(x, y)
        sibling = (x, y, 1 - c)

        def halves(ref):
            rh = ref.shape[-2] // 2
            return pl.ds(c * rh, rh), pl.ds((1 - c) * rh, rh)

        def first():
            cps = [_remote(w_ref.at[halves(w_ref)[0]], o_ref.at[me, halves(w_ref)[0]], sems, 6 * a + k, (*xy, c))
                   for a, (w_ref, o_ref) in enumerate(zip(in_refs[:n], out_refs[:n])) for k, (_, xy) in enumerate(chips)]
            return cps + [_remote(e_ref, eo_ref.at[me], sems, 6 * n + 3 * b + k, (*xy, c))
                          for b, (e_ref, eo_ref) in enumerate(zip(in_refs[n:], out_refs[n:]))
                          for k, (_, xy) in enumerate(chips)]

        def passed():
            return [_remote(o_ref.at[chip, halves(o_ref)[0]], o_ref.at[chip, halves(o_ref)[0]], sems, 6 * a + 3 + k, sibling)
                    for a, o_ref in enumerate(out_refs[:n]) for k, (chip, _) in enumerate(chips)]

        def start():
            for cp in first():
                cp.start()

        def forward():
            for a, o_ref in enumerate(out_refs[:n]):
                for k, (chip, xy) in enumerate(chips):
                    landed = o_ref.at[chip, halves(o_ref)[0]]
                    _remote(landed, landed, sems, 6 * a + k, (*xy, c)).wait_recv()
            for cp in passed():
                cp.start()

        def finish():
            for a, o_ref in enumerate(out_refs[:n]):
                for k, (chip, _) in enumerate(chips):
                    landed = o_ref.at[chip, halves(o_ref)[1]]
                    _remote(landed, landed, sems, 6 * a + 3 + k, sibling).wait_recv()
            for b, (e_ref, eo_ref) in enumerate(zip(in_refs[n:], out_refs[n:])):
                for k, (chip, xy) in enumerate(chips):
                    _remote(e_ref, eo_ref.at[chip], sems, 6 * n + 3 * b + k, (*xy, c)).wait_recv()
            for cp in first() + passed():
                cp.wait_send()

        return start, forward, finish


class _ChipExchange:
    def __init__(self, ps):
        self.arrays = list(ps)
        self.out_shape = [jax.ShapeDtypeStruct(p.shape, p.dtype) for p in ps]
        self.scratch = [pltpu.SemaphoreType.DMA((3 * len(ps),)), pltpu.SemaphoreType.DMA((3 * len(ps),))]

    def steps(self, p_refs, out_refs, sems):
        x, y, c = _place()
        me = 2 * x + y
        chips = _other_chips(x, y)

        def copies():
            return [_remote(p_ref.at[chip], o_ref.at[me], sems, 3 * a + k, (*xy, c))
                    for a, (p_ref, o_ref) in enumerate(zip(p_refs, out_refs)) for k, (chip, xy) in enumerate(chips)]

        def start():
            for cp in copies():
                cp.start()

        def finish():
            for a, (p_ref, o_ref) in enumerate(zip(p_refs, out_refs)):
                for k, (chip, xy) in enumerate(chips):
                    _remote(p_ref.at[chip], o_ref.at[chip], sems, 3 * a + k, (*xy, c)).wait_recv()
            for cp in copies():
                cp.wait_send()

        return start, finish


def _exchange_cores(gs, name, small=None):
    n = len(gs)
    m = 0 if small is None else 1

    def body(*refs):
        g_refs, sib_refs = refs[:n], refs[n + m:2 * n + m]
        sems = refs[2 * (n + m):]
        x, y, c = _place()
        me = 4 * x + 2 * y + c
        copies = []
        for a, (g_ref, sib_ref) in enumerate(zip(g_refs, sib_refs)):
            rh = g_ref.shape[1] // 2
            copies.append(_remote(g_ref.at[:, pl.ds((1 - c) * rh, rh), :], sib_ref, sems, a, (x, y, 1 - c)))
        if m:
            small_ref, all_ref = refs[n], refs[2 * n + m]
            k = n
            for fx in (0, 1):
                for fy in (0, 1):
                    for fc in (0, 1):
                        if fx or fy or fc:
                            to = (1 - x if fx else x, 1 - y if fy else y, 1 - c if fc else c)
                            copies.append(_remote(small_ref, all_ref.at[me], sems, k, to))
                            k += 1
        for cp in copies:
            cp.start()
        for cp in copies:
            cp.wait_recv()
        for cp in copies:
            cp.wait_send()

    n_copies = n + m * (N_DEV - 1)
    args = list(gs) + ([small] if m else [])
    return pl.pallas_call(
        body, name=name, in_specs=[ANY] * (n + m), out_specs=[ANY] * (n + m),
        out_shape=[jax.ShapeDtypeStruct((N_CHIPS, g.shape[1] // 2, g.shape[2]), F32) for g in gs]
        + ([jax.ShapeDtypeStruct((N_DEV,) + small.shape, F32)] if m else []),
        scratch_shapes=[pltpu.SemaphoreType.DMA((n_copies,)), pltpu.SemaphoreType.DMA((n_copies,))],
    )(*args)


def _share_halves(ghs, name):
    n = len(ghs)

    def body(*refs):
        gh_refs, out_refs, sems = refs[:n], refs[n:2 * n], refs[2 * n:]
        x, y, c = _place()
        copies = [_remote(gh_ref, o_ref, sems, a, (x, y, 1 - c)) for a, (gh_ref, o_ref) in enumerate(zip(gh_refs, out_refs))]
        for cp in copies:
            cp.start()
        for cp in copies:
            cp.wait_recv()
        for cp in copies:
            cp.wait_send()

    return pl.pallas_call(
        body, name=name, in_specs=[ANY] * n, out_specs=[ANY] * n,
        out_shape=[jax.ShapeDtypeStruct(g.shape, g.dtype) for g in ghs],
        scratch_shapes=[pltpu.SemaphoreType.DMA((n,)), pltpu.SemaphoreType.DMA((n,))],
    )(*ghs)


EW_BLOCK_BYTES = 2 * 1024 * 1024


def _row_block(rows, cols, mult=8):
    fits = [b for b in range(mult, rows + 1, mult) if rows % b == 0 and b * cols * 4 <= EW_BLOCK_BYTES]
    return max(fits) if fits else mult


def _add2(a, b, name):
    R, C = a.shape
    rows = _row_block(R, C, mult=16)
    spec = pl.BlockSpec((rows, C), lambda i: (i, 0))

    def body(a_ref, b_ref, o_ref):
        o_ref[...] = (a_ref[...] + b_ref[...]).astype(BF16)

    return pl.pallas_call(
        body, name=name, grid=(R // rows,), in_specs=[spec, spec], out_specs=spec,
        out_shape=jax.ShapeDtypeStruct(a.shape, BF16),
        compiler_params=_cparams(("parallel",)),
    )(a, b)


def _sum_leading(a, name):
    n, R, C = a.shape
    rows = _row_block(R, n * C, mult=16 if a.dtype == BF16 else 8)

    def body(a_ref, o_ref):
        acc = a_ref[0].astype(F32)
        for j in range(1, n):
            acc = acc + a_ref[j].astype(F32)
        o_ref[...] = acc

    return pl.pallas_call(
        body, name=name, grid=(R // rows,),
        in_specs=[pl.BlockSpec((n, rows, C), lambda i: (0, i, 0))],
        out_specs=pl.BlockSpec((rows, C), lambda i: (i, 0)),
        out_shape=jax.ShapeDtypeStruct((R, C), F32),
        compiler_params=_cparams(("parallel",)),
    )(a)


def _adamw(w, g, m, v, name):
    R, C = w.shape
    rows = _row_block(R, C)
    spec = pl.BlockSpec((rows, C), lambda i: (i, 0))

    def body(w_ref, g_ref, m_ref, v_ref, d_ref, mo_ref, vo_ref):
        gv = g_ref[...]
        mn = ADAM_B1 * m_ref[...] + (1.0 - ADAM_B1) * gv
        vn = ADAM_B2 * v_ref[...] + (1.0 - ADAM_B2) * (gv * gv)
        m_hat = mn / (1.0 - ADAM_B1 ** ADAM_STEP)
        v_hat = vn / (1.0 - ADAM_B2 ** ADAM_STEP)
        d_ref[...] = -ADAM_LR * (m_hat / (jnp.sqrt(v_hat) + ADAM_EPS) + ADAM_WD * w_ref[...])
        mo_ref[...] = mn
        vo_ref[...] = vn

    return pl.pallas_call(
        body, name=name, grid=(R // rows,), in_specs=[spec] * 4, out_specs=[spec] * 3,
        out_shape=[jax.ShapeDtypeStruct((R, C), F32)] * 3,
        compiler_params=_cparams(("parallel",)),
    )(w, g, m, v)


BIG = ("w_in", "w_branch_a", "w_branch_b", "w_out", "w_ffn_up", "w_ffn_down")
COL_SHARDED = {"w_in": True, "w_branch_a": True, "w_branch_b": True, "w_out": False, "w_ffn_up": True,
               "w_ffn_down": False}
CONV_W_COLS = 2 * D_FF // N_CHIPS
SMALL_REPLICATED = (("norm1_g", D_MODEL), ("q_norm_g", HEAD_DIM), ("k_norm_g", HEAD_DIM),
                    ("rel_bias", N_HEADS * N_REL), ("norm2_g", D_MODEL), ("ffn_conv_b", 2 * D_FF))
SMALL_GRADS = SMALL_REPLICATED + (("ffn_conv_w", 3 * 2 * D_FF),)
SMALL_OWN = SMALL_REPLICATED + (("ffn_conv_w", 3 * CONV_W_COLS),)
SMALL_GRAD_ROWS = 32
SMALL_OWN_ROWS = 16


def _whole(name, stacked):
    return jnp.concatenate(list(stacked), axis=1) if COL_SHARDED[name] else stacked.reshape(-1, stacked.shape[2])


def _pack_small(vals, sizes, rows):
    flat = jnp.concatenate([vals[n].reshape(-1) for n, _ in sizes])
    return jnp.pad(flat, (0, rows * PACK_COLS - flat.shape[0])).reshape(rows, PACK_COLS)


def _unpack_small(packed, sizes):
    flat, out, o = packed.reshape(-1), {}, 0
    for n, sz in sizes:
        out[n] = flat[o:o + sz]
        o += sz
    return out


def kernel(x, norm1_g, w_in, q_norm_g, k_norm_g, rel_bias, w_branch_a, w_branch_b, w_out, norm2_g, w_ffn_up, ffn_conv_w, ffn_conv_b, w_ffn_down, loss_target, m_norm1_g, m_w_in, m_q_norm_g, m_k_norm_g, m_rel_bias, m_w_branch_a, m_w_branch_b, m_w_out, m_norm2_g, m_w_ffn_up, m_ffn_conv_w, m_ffn_conv_b, m_w_ffn_down, v_norm1_g, v_w_in, v_q_norm_g, v_k_norm_g, v_rel_bias, v_w_branch_a, v_w_branch_b, v_w_out, v_norm2_g, v_w_ffn_up, v_ffn_conv_w, v_ffn_conv_b, v_w_ffn_down):
    w_big = {"w_in": w_in[0], "w_branch_a": w_branch_a[0], "w_branch_b": w_branch_b[0], "w_out": w_out[0],
             "w_ffn_up": w_ffn_up[0], "w_ffn_down": w_ffn_down[0]}
    m_big = {"w_in": m_w_in[0], "w_branch_a": m_w_branch_a[0], "w_branch_b": m_w_branch_b[0], "w_out": m_w_out[0],
             "w_ffn_up": m_w_ffn_up[0], "w_ffn_down": m_w_ffn_down[0]}
    v_big = {"w_in": v_w_in[0], "w_branch_a": v_w_branch_a[0], "w_branch_b": v_w_branch_b[0], "w_out": v_w_out[0],
             "w_ffn_up": v_w_ffn_up[0], "w_ffn_down": v_w_ffn_down[0]}
    xs, tgt = x[0], loss_target[0]

    xi, yi, ci = _place()
    chip = 2 * xi + yi

    def with_own(stacked, own):
        return lax.dynamic_update_slice(stacked, own[None], (chip,) + (0,) * own.ndim)

    shards_bf = {n: w_big[n].astype(BF16) for n in BIG}
    conv_own = jnp.pad(ffn_conv_w[0], ((0, 8 - ffn_conv_w.shape[1]), (0, 0)))
    later = [n for n in BIG if n != "w_in"]

    hn, hn_t, w_in_g = _rms_fwd(xs, norm1_g, "rms1", gather=_Gather([shards_bf["w_in"]]))
    w_in_f = _whole("w_in", with_own(w_in_g, shards_bf["w_in"]))
    w_in_t = w_in_f.T
    qk = _matmul(hn, w_in_f[:, :2 * WIDTH], F32, "proj_qk")
    vqkv = _matmul(hn, w_in_f[:, 2 * WIDTH:6 * WIDTH], BF16, "proj_vqkv")
    gates = _matmul(hn, w_in_f[:, 6 * WIDTH:], BF16, "proj_gates")
    gq = jnp.tile(q_norm_g, (1, N_HEADS))
    gk = jnp.tile(k_norm_g, (1, N_HEADS))
    qa, ka = _qknorm_fwd(qk, gq, gk, "qknorm")
    per = WIDTH // PAIR
    b_offs = (per, 2 * per, 3 * per)
    tab = _bias_table(jnp.pad(rel_bias[0], ((0, 0), (0, REL_PAD - N_REL))), "bias_table")
    out_a, out_a_t, *gathered = _ca_fwd(qa, ka, vqkv, tab, "chunk_attn",
                                        gather=_Gather([shards_bf[n] for n in later], [conv_own]))
    full = {n: _whole(n, with_own(g, shards_bf[n])) for n, g in zip(later, gathered)}
    conv_w = jnp.concatenate(list(with_own(gathered[-1], conv_own)[:, :3]), axis=1)
    w_a, w_b, w_o, w_up, w_dn = (full[n] for n in later)
    w_a_t, w_b_t, w_o_t, w_up_t, w_dn_t = (w.T for w in (w_a, w_b, w_o, w_up, w_dn))
    out_b, out_b_f32, out_b_t = _sb_fwd(vqkv, vqkv, vqkv, "stick_attn", b_offs)
    y_a = _matmul(out_a, w_a, BF16, "branch_a")
    y_b = _matmul(out_b, w_b, BF16, "branch_b")
    mixed, mixed_t = _mix_fwd(gates, y_a, y_b, "mix")
    x2 = _matmul(mixed, w_o, F32, "out_proj", residual=xs)
    hn2, hn2_t = _rms_fwd(x2, norm2_g, "rms2")
    hid = _matmul(hn2, w_up, BF16, "ffn_up")
    act, act_t = _convglu_fwd(hid, conv_w, ffn_conv_b, "convglu")
    y = _matmul(act, w_dn, F32, "ffn_down", residual=x2)
    dy, dyb, sq = _loss_head(y, tgt, "loss_head")
    loss = lax.psum(0.5 / D_MODEL * jnp.sum(sq), ("x", "y", "c"))

    dact = _matmul(dyb, w_dn_t, BF16, "d_act")
    d_w_dn = _matmul(act_t, dyb, F32, "d_w_down")
    dhg, dhu, dcwg, dcwu, dcbg, dcbu = _convglu_bwd(hid, dact, conv_w, ffn_conv_b, "convglu_bwd")
    half_chips = N_CHIPS // 2
    d_w_up = jnp.concatenate([_matmul(hn2_t, dhg, F32, "d_w_up_gate", slabs=half_chips),
                              _matmul(hn2_t, dhu, F32, "d_w_up_up", slabs=half_chips)], axis=0)
    dhn2 = _matmul(dhg, w_up_t[:D_FF], F32, "d_hn2_gate")
    dhn2 = _matmul(dhu, w_up_t[D_FF:], F32, "d_hn2_up", residual=dhn2)
    dx2, dx2b, d_norm2 = _rms_bwd(x2, norm2_g, dhn2, dy, "rms2_bwd")
    dmixed = _matmul(dx2b, w_o_t, BF16, "d_mixed")
    d_w_o = _matmul(mixed_t, dx2b, F32, "d_w_out")
    dgates, dya, dyb_b = _mix_bwd(dmixed, gates, y_a, y_b, "mix_bwd")
    d_w_a, d_w_b = (_matmul(o_t, d, F32, nm).reshape(WIDTH, N_CHIPS, -1).transpose(1, 0, 2)
                    for o_t, d, nm in ((out_a_t, dya, "d_w_branch_a"), (out_b_t, dyb_b, "d_w_branch_b")))
    do_a = _matmul(dya, w_a_t, BF16, "d_out_a")
    do_b = _matmul(dyb_b, w_b_t, BF16, "d_out_b")

    def core_sums(names, gs, sibs):
        out = {}
        for n, g, sib in zip(names, gs, sibs):
            rh, cols = sib.shape[1], sib.shape[2]
            mine = lax.dynamic_slice_in_dim(g, ci * rh, rh, axis=1)
            out[n] = _add2(mine.reshape(-1, cols), sib.reshape(-1, cols), "sum_cores_" + n).reshape(sib.shape)
        return out

    grads_full = {"w_branch_a": d_w_a, "w_branch_b": d_w_b, "w_ffn_up": d_w_up,
                  "w_out": d_w_o.reshape(N_CHIPS, -1, D_MODEL), "w_ffn_down": d_w_dn.reshape(N_CHIPS, -1, D_MODEL)}
    early = [grads_full[n] for n in later]
    chip_parts = core_sums(later, early, _exchange_cores(early, "exchange_cores_early"))
    dqb, dkb, dvb, *parts_early = _sb_bwd(vqkv, vqkv, vqkv, out_b_f32, do_b, "stick_attn_bwd", b_offs,
                                           exchange=_ChipExchange([chip_parts[n] for n in later]))
    parts = dict(zip(later, parts_early))
    dqa_n, dka_n, dva, dtab = _ca_bwd(qa, ka, vqkv, do_a, tab, "chunk_attn_bwd")
    d_rel = _bias_table_bwd(dtab, "bias_table_bwd")[:, :N_REL]
    dqa, dka, dgq, dgk = _qknorm_bwd(qk, gq, gk, dqa_n, dka_n, "qknorm_bwd")
    dproj = jnp.concatenate([dqa, dka, dva, dqb, dkb, dvb, dgates], axis=1)
    d_w_in = _matmul(hn_t, dproj, F32, "d_w_in", slabs=N_CHIPS)
    chip_parts.update(core_sums(["w_in"], [d_w_in], _exchange_cores([d_w_in], "exchange_cores_w_in")))
    dhn, parts["w_in"] = _matmul(dproj, w_in_t, F32, "d_hn", exchange=_ChipExchange([chip_parts["w_in"]]))
    dx, _, d_norm1 = _rms_bwd(xs, norm1_g, dhn, dx2, "rms1_bwd")

    small_g = _pack_small({"norm1_g": d_norm1, "q_norm_g": dgq.reshape(N_HEADS, HEAD_DIM).sum(0),
                           "k_norm_g": dgk.reshape(N_HEADS, HEAD_DIM).sum(0), "rel_bias": d_rel,
                           "norm2_g": d_norm2, "ffn_conv_b": jnp.concatenate([dcbg, dcbu], axis=1),
                           "ffn_conv_w": jnp.concatenate([dcwg, dcwu], axis=1)}, SMALL_GRADS, SMALL_GRAD_ROWS)
    (small_all,) = _exchange_cores([], "exchange_small", small=small_g)
    g_halves = [_sum_leading(with_own(parts[n], lax.dynamic_index_in_dim(chip_parts[n], chip, 0, keepdims=False)),
                             "sum_chips_" + n) for n in BIG]
    g_others = _share_halves(g_halves, "share_halves")
    grads = {n: jnp.concatenate([jnp.where(ci == 0, mine, other), jnp.where(ci == 0, other, mine)], axis=0)
             for n, mine, other in zip(BIG, g_halves, g_others)}
    small_all = lax.dynamic_update_slice(small_all, small_g[None], (4 * xi + 2 * yi + ci, 0, 0))
    small_sum = _unpack_small(_sum_leading(small_all, "sum_small"), SMALL_GRADS)
    small_sum["ffn_conv_w"] = lax.dynamic_slice_in_dim(small_sum["ffn_conv_w"].reshape(3, 2 * D_FF),
                                                       chip * CONV_W_COLS, CONV_W_COLS, axis=1)

    deltas, new_m, new_v = {}, {}, {}
    for n in BIG:
        deltas[n], new_m[n], new_v[n] = _adamw(w_big[n], grads[n], m_big[n], v_big[n], "adamw_" + n)

    shapes = {"norm1_g": norm1_g.shape, "q_norm_g": q_norm_g.shape, "k_norm_g": k_norm_g.shape,
              "rel_bias": rel_bias.shape, "norm2_g": norm2_g.shape, "ffn_conv_b": ffn_conv_b.shape,
              "ffn_conv_w": ffn_conv_w.shape}
    small_w = {"norm1_g": norm1_g, "q_norm_g": q_norm_g, "k_norm_g": k_norm_g, "rel_bias": rel_bias,
               "norm2_g": norm2_g, "ffn_conv_b": ffn_conv_b, "ffn_conv_w": ffn_conv_w}
    small_m = {"norm1_g": m_norm1_g, "q_norm_g": m_q_norm_g, "k_norm_g": m_k_norm_g, "rel_bias": m_rel_bias,
               "norm2_g": m_norm2_g, "ffn_conv_b": m_ffn_conv_b, "ffn_conv_w": m_ffn_conv_w}
    small_v = {"norm1_g": v_norm1_g, "q_norm_g": v_q_norm_g, "k_norm_g": v_k_norm_g, "rel_bias": v_rel_bias,
               "norm2_g": v_norm2_g, "ffn_conv_b": v_ffn_conv_b, "ffn_conv_w": v_ffn_conv_w}
    ds, ms, vs = _adamw(*(_pack_small(t, SMALL_OWN, SMALL_OWN_ROWS) for t in (small_w, small_sum, small_m, small_v)),
                        "adamw_small")
    small_grads = small_sum
    ds, ms, vs = (_unpack_small(t, SMALL_OWN) for t in (ds, ms, vs))

    order = ("norm1_g", "w_in", "q_norm_g", "k_norm_g", "rel_bias", "w_branch_a", "w_branch_b", "w_out",
             "norm2_g", "w_ffn_up", "ffn_conv_w", "ffn_conv_b", "w_ffn_down")
    outs = [loss, dx[None]]
    for big, small in ((grads, small_grads), (deltas, ds), (new_m, ms), (new_v, vs)):
        for n in order:
            outs.append(big[n][None] if n in big else small[n].reshape(shapes[n]))
    return tuple(outs)
```

```python
import functools

import jax
import jax.numpy as jnp
from jax import lax
from jax.experimental import pallas as pl
from jax.experimental.pallas import tpu as pltpu

F32 = jnp.float32
BF16 = jnp.bfloat16
MESH = pl.DeviceIdType.MESH

D_MODEL = 1024
HEAD_DIM = 64
N_HEADS = 8
WIDTH = N_HEADS * HEAD_DIM
CHUNK = 64
LEFT_CHUNKS = 8
MAX_REL = 128
N_REL = 2 * MAX_REL + 1
D_FF = 2816
EPS = 1e-6
NEG = -1e30

ADAM_LR = 0.001
ADAM_B1 = 0.9
ADAM_B2 = 0.999
ADAM_EPS = 1e-08
ADAM_WD = 0.01
ADAM_STEP = 10

N_CHIPS = 4
N_DEV = 8
LANES = 128
PAIR = 2 * HEAD_DIM
BQ = 256
BAND = LEFT_CHUNKS * CHUNK
KWIN = BAND + BQ
VMEM_LIMIT = 56 * 1024 * 1024
PACK_COLS = 1024

NN = (((1,), (0,)), ((), ()))
NT = (((1,), (1,)), ((), ()))
TN = (((0,), (0,)), ((), ()))


def _cparams(sem=None):
    if sem is None:
        return pltpu.CompilerParams(vmem_limit_bytes=VMEM_LIMIT)
    return pltpu.CompilerParams(dimension_semantics=sem, vmem_limit_bytes=VMEM_LIMIT)


def _pick(n, cands):
    for c in cands:
        if n % c == 0:
            return c
    raise ValueError(f"no block for {n}")


def _dot(a, b, dn):
    return lax.dot_general(a, b, dn, preferred_element_type=F32)


def _sigmoid(x):
    return 0.5 * jnp.tanh(0.5 * x) + 0.5


def _split_bf16(x):
    hi = x.astype(BF16)
    lo = (x - hi.astype(F32)).astype(BF16)
    return hi, lo


MM_RESIDENT_BYTES = 12 * 1024 * 1024
MM_TILE_BYTES = 4 * 1024 * 1024


def _matmul(a, b, out_dtype, name, residual=None, slabs=None, exchange=None):
    (M, K), N = a.shape, b.shape[1]
    out_bytes = jnp.dtype(out_dtype).itemsize
    if slabs is None and N <= D_FF and K * N * 2 <= MM_RESIDENT_BYTES:
        bk, bn = K, N
        bm = next(c for c in (1024, 512, 256, 128)
                  if M % c == 0 and c * K * 2 <= MM_TILE_BYTES and c * N * out_bytes <= MM_TILE_BYTES)
        b_spec = pl.BlockSpec((bk, bn), lambda i, j, k: (0, 0), pipeline_mode=pl.Buffered(1))
    elif slabs is None and K <= D_FF:
        bk, bm, bn = K, _pick(M, (1024, 512)), _pick(N, (D_FF // 2, 512, 256, 128))
        b_spec = pl.BlockSpec((bk, bn), lambda i, j, k: (k, j))
    else:
        bk = _pick(K, (2048, 1024, 512))
        bm = _pick(M, (D_FF // 2, 1024, 512, 256, 128))
        bn = N // slabs if slabs else _pick(N, (D_FF // 2, 1024, 512, 256, 128))
        b_spec = pl.BlockSpec((bk, bn), lambda i, j, k: (k, j))
    nk = K // bk
    dn = NN
    a_spec = pl.BlockSpec((bm, bk), lambda i, j, k: (i, k))
    if slabs:
        o_spec = pl.BlockSpec((None, bm, bn), lambda i, j, k: (j, i, 0))
        out_shape = jax.ShapeDtypeStruct((slabs, M, bn), out_dtype)
    else:
        o_spec = pl.BlockSpec((bm, bn), lambda i, j, k: (i, j))
        out_shape = jax.ShapeDtypeStruct((M, N), out_dtype)
    has_res = residual is not None
    n_in = 3 if has_res else 2
    n_side = len(exchange.arrays) if exchange else 0
    grid = (M // bm, N // bn, nk)

    def body(*refs):
        a_ref, b_ref = refs[:2]
        r_ref = refs[2] if has_res else None
        o_ref, acc_ref = refs[n_in + n_side], refs[n_in + 2 * n_side + 1]
        k = pl.program_id(2)
        if exchange:
            steps = [pl.program_id(d) for d in range(3)]
            start, finish_side = exchange.steps(refs[n_in:n_in + n_side], refs[n_in + n_side + 1:n_in + 2 * n_side + 1],
                                                refs[n_in + 2 * n_side + 2:])
            pl.when((steps[0] == 0) & (steps[1] == 0) & (steps[2] == 0))(start)
        part = _dot(a_ref[...], b_ref[...], dn)

        def finish(total):
            if has_res:
                total = total + r_ref[...]
            o_ref[...] = total.astype(out_dtype)

        if nk == 1:
            finish(part)
        else:
            @pl.when(k == 0)
            def _():
                acc_ref[...] = part

            @pl.when(k > 0)
            def _():
                acc_ref[...] += part

            @pl.when(k == nk - 1)
            def _():
                finish(acc_ref[...])

        if exchange:
            pl.when((steps[0] == grid[0] - 1) & (steps[1] == grid[1] - 1) & (steps[2] == grid[2] - 1))(finish_side)

    side = exchange.arrays if exchange else []
    in_specs = [a_spec, b_spec] + ([o_spec] if has_res else []) + [ANY] * n_side
    args = (a, b) + ((residual,) if has_res else ()) + tuple(side)
    out = pl.pallas_call(
        body, name=name, grid=grid,
        in_specs=in_specs, out_specs=[o_spec] + [ANY] * n_side,
        out_shape=[out_shape] + (exchange.out_shape if exchange else []),
        scratch_shapes=[pltpu.VMEM((bm, bn) if nk > 1 else (8, LANES), F32)] + (exchange.scratch if exchange else []),
        compiler_params=_cparams(("arbitrary",) * 3 if exchange else ("parallel", "parallel", "arbitrary")),
    )(*args)
    return out if exchange else out[0]


ROWS = 512


def _row_spec(cols, bm=ROWS):
    return pl.BlockSpec((bm, cols), lambda i: (i, 0))


def _col_spec(rows, bn=ROWS):
    return pl.BlockSpec((rows, bn), lambda i: (0, i))


def _full_spec(shape):
    return pl.BlockSpec(shape, lambda i: (0,) * len(shape))


def _colsum8(t):
    return jnp.sum(t.reshape(t.shape[0] // 8, 8, t.shape[1]), axis=0)


def _rms_fwd(x, g, name, gather=None):
    S, D = x.shape
    nt = S // ROWS
    n_side = len(gather.arrays) if gather else 0

    def body(*refs):
        x_ref, g_ref = refs[:2]
        o_ref, ot_ref = refs[2 + n_side:4 + n_side]
        i = pl.program_id(0)
        if gather:
            start, forward, finish = gather.steps(refs[2:2 + n_side], refs[4 + n_side:4 + 2 * n_side], refs[4 + 2 * n_side:])
            pl.when(i == 0)(start)
            pl.when(i == 3 * nt // 4)(forward)
        xv = x_ref[...]
        r = lax.rsqrt(jnp.mean(xv * xv, axis=-1, keepdims=True) + EPS)
        y = xv * r * g_ref[...]
        o_ref[...] = y.astype(BF16)
        ot_ref[...] = y.T.astype(BF16)
        if gather:
            pl.when(i == nt - 1)(finish)

    side = gather.arrays if gather else []
    return pl.pallas_call(
        body, name=name, grid=(nt,),
        in_specs=[_row_spec(D), _full_spec((1, D))] + [ANY] * n_side,
        out_specs=[_row_spec(D), _col_spec(D)] + [ANY] * n_side,
        out_shape=[jax.ShapeDtypeStruct((S, D), BF16), jax.ShapeDtypeStruct((D, S), BF16)]
        + (gather.out_shape if gather else []),
        scratch_shapes=gather.scratch if gather else [],
        compiler_params=_cparams(("arbitrary",) if gather else ("parallel",)),
    )(x, g, *side)


def _rms_bwd(x, g, dy, dres, name):
    S, D = x.shape
    nt = S // ROWS

    def body(x_ref, g_ref, dy_ref, dres_ref, dx_ref, dxb_ref, dg_ref, acc_ref):
        i = pl.program_id(0)
        xv, dyv = x_ref[...], dy_ref[...]
        r = lax.rsqrt(jnp.mean(xv * xv, axis=-1, keepdims=True) + EPS)
        xr = xv * r
        u = dyv * g_ref[...]
        dx = r * u - xr * (r * r) * jnp.mean(xv * u, axis=-1, keepdims=True) + dres_ref[...]
        dx_ref[...] = dx
        dxb_ref[...] = dx.astype(BF16)
        part = _colsum8(dyv * xr)

        @pl.when(i == 0)
        def _():
            acc_ref[...] = part

        @pl.when(i > 0)
        def _():
            acc_ref[...] += part

        @pl.when(i == nt - 1)
        def _():
            dg_ref[...] = jnp.sum(acc_ref[...], axis=0, keepdims=True)

    return pl.pallas_call(
        body, name=name, grid=(nt,),
        in_specs=[_row_spec(D), _full_spec((1, D)), _row_spec(D), _row_spec(D)],
        out_specs=[_row_spec(D), _row_spec(D), _full_spec((1, D))],
        out_shape=[jax.ShapeDtypeStruct((S, D), F32), jax.ShapeDtypeStruct((S, D), BF16),
                   jax.ShapeDtypeStruct((1, D), F32)],
        scratch_shapes=[pltpu.VMEM((8, D), F32)],
        compiler_params=_cparams(("arbitrary",)),
    )(x, g, dy, dres)


def _head_mean(t, blockdiag):
    hi, lo = _split_bf16(t)
    return (_dot(hi, blockdiag, NN) + _dot(lo, blockdiag, NN)) * (1.0 / HEAD_DIM)


def _blockdiag():
    r = lax.broadcasted_iota(jnp.int32, (WIDTH, WIDTH), 0) // HEAD_DIM
    c = lax.broadcasted_iota(jnp.int32, (WIDTH, WIDTH), 1) // HEAD_DIM
    return jnp.where(r == c, 1.0, 0.0).astype(BF16)


def _qknorm_fwd(qk, gq, gk, name):
    S = qk.shape[0]

    def body(qk_ref, gq_ref, gk_ref, q_ref, k_ref):
        bd = _blockdiag()
        for part, g_ref, o_ref, scale in ((0, gq_ref, q_ref, HEAD_DIM ** -0.5), (1, gk_ref, k_ref, 1.0)):
            t = qk_ref[:, part * WIDTH:(part + 1) * WIDTH]
            r = lax.rsqrt(_head_mean(t * t, bd) + EPS)
            o_ref[...] = (t * r * g_ref[...] * scale).astype(BF16)

    return pl.pallas_call(
        body, name=name, grid=(S // ROWS,),
        in_specs=[_row_spec(2 * WIDTH), _full_spec((1, WIDTH)), _full_spec((1, WIDTH))],
        out_specs=[_row_spec(WIDTH), _row_spec(WIDTH)],
        out_shape=[jax.ShapeDtypeStruct((S, WIDTH), BF16)] * 2,
        compiler_params=_cparams(("parallel",)),
    )(qk, gq, gk)


def _qknorm_bwd(qk, gq, gk, dqn, dkn, name):
    S = qk.shape[0]
    nt = S // ROWS

    def body(qk_ref, gq_ref, gk_ref, dqn_ref, dkn_ref, dq_ref, dk_ref, dgq_ref, dgk_ref, accq_ref, acck_ref):
        i = pl.program_id(0)
        bd = _blockdiag()
        for part, g_ref, dn_ref, o_ref, dg_ref, acc_ref, scale in (
                (0, gq_ref, dqn_ref, dq_ref, dgq_ref, accq_ref, HEAD_DIM ** -0.5),
                (1, gk_ref, dkn_ref, dk_ref, dgk_ref, acck_ref, 1.0)):
            t = qk_ref[:, part * WIDTH:(part + 1) * WIDTH]
            dn = dn_ref[...] * scale
            r = lax.rsqrt(_head_mean(t * t, bd) + EPS)
            u = dn * g_ref[...]
            dt = r * u - t * (r * r * r) * _head_mean(t * u, bd)
            o_ref[...] = dt.astype(BF16)
            psum = _colsum8(dn * t * r)

            @pl.when(i == 0)
            def _():
                acc_ref[...] = psum

            @pl.when(i > 0)
            def _():
                acc_ref[...] += psum

            @pl.when(i == nt - 1)
            def _():
                dg_ref[...] = jnp.sum(acc_ref[...], axis=0, keepdims=True)

    return pl.pallas_call(
        body, name=name, grid=(nt,),
        in_specs=[_row_spec(2 * WIDTH), _full_spec((1, WIDTH)), _full_spec((1, WIDTH)),
                  _row_spec(WIDTH), _row_spec(WIDTH)],
        out_specs=[_row_spec(WIDTH), _row_spec(WIDTH), _full_spec((1, WIDTH)), _full_spec((1, WIDTH))],
        out_shape=[jax.ShapeDtypeStruct((S, WIDTH), BF16)] * 2 + [jax.ShapeDtypeStruct((1, WIDTH), F32)] * 2,
        scratch_shapes=[pltpu.VMEM((8, WIDTH), F32)] * 2,
        compiler_params=_cparams(("arbitrary",)),
    )(qk, gq, gk, dqn, dkn)


def _gate_specs(D):
    return [pl.BlockSpec((ROWS, D), lambda i: (i, 0)), pl.BlockSpec((ROWS, D), lambda i: (i, 1))]


def _mix_fwd(gates, ya, yb, name):
    S, D = ya.shape

    def body(ga_ref, gb_ref, ya_ref, yb_ref, o_ref, ot_ref):
        ga, gb, yav, ybv = (r[...].astype(F32) for r in (ga_ref, gb_ref, ya_ref, yb_ref))
        m = _sigmoid(ga) * yav + _sigmoid(gb) * ybv
        o_ref[...] = m.astype(BF16)
        ot_ref[...] = m.T.astype(BF16)

    return pl.pallas_call(
        body, name=name, grid=(S // ROWS,),
        in_specs=_gate_specs(D) + [_row_spec(D)] * 2, out_specs=[_row_spec(D), _col_spec(D)],
        out_shape=[jax.ShapeDtypeStruct((S, D), BF16), jax.ShapeDtypeStruct((D, S), BF16)],
        compiler_params=_cparams(("parallel",)),
    )(gates, gates, ya, yb)


def _mix_bwd(dm, gates, ya, yb, name):
    S, D = ya.shape

    def body(dm_ref, ga_ref, gb_ref, ya_ref, yb_ref, dg_ref, dya_ref, dyb_ref):
        dmv = dm_ref[...].astype(F32)
        for half, (g_ref, y_ref, dy_ref) in enumerate(((ga_ref, ya_ref, dya_ref), (gb_ref, yb_ref, dyb_ref))):
            s = _sigmoid(g_ref[...].astype(F32))
            dy_ref[...] = (dmv * s).astype(BF16)
            dg_ref[:, half * D:(half + 1) * D] = (dmv * y_ref[...].astype(F32) * s * (1.0 - s)).astype(BF16)

    return pl.pallas_call(
        body, name=name, grid=(S // ROWS,),
        in_specs=[_row_spec(D)] + _gate_specs(D) + [_row_spec(D)] * 2,
        out_specs=[_row_spec(2 * D), _row_spec(D), _row_spec(D)],
        out_shape=[jax.ShapeDtypeStruct((S, 2 * D), BF16)] + [jax.ShapeDtypeStruct((S, D), BF16)] * 2,
        compiler_params=_cparams(("parallel",)),
    )(dm, gates, gates, ya, yb)


def _down_and_loss(act, w, x2, target, name):
    (S, K), D = act.shape, w.shape[1]
    nt = S // ROWS

    def body(a_ref, w_ref, x_ref, t_ref, dy_ref, dyb_ref, p_ref):
        err = _dot(a_ref[...], w_ref[...], NN) + x_ref[...] - t_ref[...]
        dy = err * (1.0 / D)
        dy_ref[...] = dy
        dyb_ref[...] = dy.astype(BF16)
        sq = _colsum8(err * err)
        acc = sq[:, 0:LANES]
        for k in range(1, D // LANES):
            acc = acc + sq[:, k * LANES:(k + 1) * LANES]
        p_ref[...] = acc

    return pl.pallas_call(
        body, name=name, grid=(nt,),
        in_specs=[_row_spec(K), pl.BlockSpec((K, D), lambda i: (0, 0), pipeline_mode=pl.Buffered(1)),
                  _row_spec(D), _row_spec(D)],
        out_specs=[_row_spec(D), _row_spec(D), pl.BlockSpec((8, LANES), lambda i: (i, 0))],
        out_shape=[jax.ShapeDtypeStruct((S, D), F32), jax.ShapeDtypeStruct((S, D), BF16),
                   jax.ShapeDtypeStruct((nt * 8, LANES), F32)],
        compiler_params=_cparams(("parallel",)),
    )(act, w, x2, target)


CONV_COLS = D_FF // 2
HALO = 16
CONV_CHUNK = 64


def _aligned(start, multiple):
    return start if isinstance(start, int) else pl.multiple_of(start, multiple)


def _conv_taps(xe, cw, cb):
    taps = (pltpu.roll(xe, 2, 0), pltpu.roll(xe, 1, 0), xe)
    return taps, cw[0:1] * taps[0] + cw[1:2] * taps[1] + cw[2:3] * taps[2] + cb


def _conv_specs(nt):
    hb, nb = ROWS // HALO, D_FF // CONV_COLS
    specs = {}
    for part, off in (("gate", 0), ("up", nb)):
        specs[part] = dict(
            main=pl.BlockSpec((ROWS, CONV_COLS), functools.partial(lambda c, i, off: (i, c + off), off=off)),
            prev=pl.BlockSpec((HALO, CONV_COLS),
                              functools.partial(lambda c, i, off: (jnp.maximum(i * hb - 1, 0), c + off), off=off)),
            nxt=pl.BlockSpec((HALO, CONV_COLS),
                             functools.partial(lambda c, i, off: (jnp.minimum((i + 1) * hb, nt * hb - 1), c + off), off=off)),
            w=pl.BlockSpec((3, CONV_COLS), functools.partial(lambda c, i, off: (0, c + off), off=off)),
            b=pl.BlockSpec((1, CONV_COLS), functools.partial(lambda c, i, off: (0, c + off), off=off)))
    return specs


def _convglu_fwd(hid, cw, cb, name):
    S = hid.shape[0]
    sp = _conv_specs(S // ROWS)

    def body(hg_ref, hgp_ref, hu_ref, hup_ref, cwg_ref, cwu_ref, cbg_ref, cbu_ref, o_ref, ot_ref):
        i = pl.program_id(1)
        keep = (i > 0).astype(F32)

        def conv(h_ref, hp_ref, cw_ref, cb_ref):
            xe = jnp.concatenate([hp_ref[...].astype(F32) * keep, h_ref[...].astype(F32)], axis=0)
            return _conv_taps(xe, cw_ref[...], cb_ref[...])[1][HALO:, :]

        gate = conv(hg_ref, hgp_ref, cwg_ref, cbg_ref)
        up = conv(hu_ref, hup_ref, cwu_ref, cbu_ref)
        act = gate * _sigmoid(gate) * up
        o_ref[...] = act.astype(BF16)
        ot_ref[...] = act.T.astype(BF16)

    g, u = sp["gate"], sp["up"]
    return pl.pallas_call(
        body, name=name, grid=(D_FF // CONV_COLS, S // ROWS),
        in_specs=[g["main"], g["prev"], u["main"], u["prev"], g["w"], u["w"], g["b"], u["b"]],
        out_specs=[g["main"], pl.BlockSpec((CONV_COLS, ROWS), lambda c, i: (c, i))],
        out_shape=[jax.ShapeDtypeStruct((S, D_FF), BF16), jax.ShapeDtypeStruct((D_FF, S), BF16)],
        compiler_params=_cparams(("parallel", "parallel")),
    )(hid, hid, hid, hid, cw, cw, cb, cb)


def _convglu_bwd(hid, dact, cw, cb, name):
    S = hid.shape[0]
    nt = S // ROWS
    sp = _conv_specs(nt)

    n_chunks = ROWS // CONV_CHUNK

    def body(hg_ref, hgp_ref, hgn_ref, hu_ref, hup_ref, hun_ref, da_ref, dan_ref,
             cwg_ref, cwu_ref, cbg_ref, cbu_ref,
             dhg_ref, dhu_ref, dcwg_ref, dcwu_ref, dcbg_ref, dcbu_ref, xg_s, xu_s, da_s):
        i = pl.program_id(1)
        kp = (i > 0).astype(F32)
        kn = (i < nt - 1).astype(F32)
        for x_s, h_ref, hp_ref, hn_ref in ((xg_s, hg_ref, hgp_ref, hgn_ref), (xu_s, hu_ref, hup_ref, hun_ref)):
            x_s[0:HALO, :] = hp_ref[...].astype(F32) * kp
            x_s[HALO:HALO + ROWS, :] = h_ref[...].astype(F32)
            x_s[HALO + ROWS:, :] = hn_ref[...].astype(F32) * kn
        da_s[0:ROWS, :] = da_ref[...].astype(F32)
        da_s[ROWS:, :] = dan_ref[...].astype(F32) * kn

        @pl.when(i == 0)
        def _():
            for ref in (dcwg_ref, dcwu_ref, dcbg_ref, dcbu_ref):
                ref[...] = jnp.zeros_like(ref)

        def lane_group(grp, _):
            lanes = pl.ds(pl.multiple_of(grp * LANES, LANES), LANES)
            cwg, cwu, cbg, cbu = cwg_ref[:, lanes], cwu_ref[:, lanes], cbg_ref[:, lanes], cbu_ref[:, lanes]

            def grads(r0, n):
                rows = pl.ds(_aligned(r0 + HALO - 8, 8), n + 8)
                taps_g, gate = _conv_taps(xg_s[rows, lanes], cwg, cbg)
                taps_u, up = _conv_taps(xu_s[rows, lanes], cwu, cbu)
                gate, up = gate[8:], up[8:]
                da = da_s[pl.ds(_aligned(r0, 8), n), lanes]
                sg = _sigmoid(gate)
                return (da * up * sg * (1.0 + gate * (1.0 - sg)), da * gate * sg,
                        [t[8:] for t in taps_g], [t[8:] for t in taps_u])

            def chunk(step, carry):
                below_g, below_u, accs = carry
                r0 = (n_chunks - 1 - step) * CONV_CHUNK
                dg, du, taps_g, taps_u = grads(r0, CONV_CHUNK)
                new_accs = []
                for d, below, cwv, taps, dh_ref, acc in ((dg, below_g, cwg, taps_g, dhg_ref, accs[0]),
                                                        (du, below_u, cwu, taps_u, dhu_ref, accs[1])):
                    ext = jnp.concatenate([d, below], axis=0)
                    n_ext = CONV_CHUNK + 8
                    dh = (cwv[2:3] * d + cwv[1:2] * pltpu.roll(ext, n_ext - 1, 0)[:CONV_CHUNK]
                          + cwv[0:1] * pltpu.roll(ext, n_ext - 2, 0)[:CONV_CHUNK])
                    dh_ref[pl.ds(_aligned(r0, CONV_CHUNK), CONV_CHUNK), lanes] = dh.astype(BF16)
                    new_accs.append(tuple(a + _colsum8(d * tap) for a, tap in zip(acc[:3], taps))
                                    + (acc[3] + _colsum8(d),))
                return dg[0:8], du[0:8], tuple(new_accs)

            below_g, below_u, _, _ = grads(ROWS, 8)
            zero = jnp.zeros((8, LANES), F32)
            _, _, accs = lax.fori_loop(0, n_chunks, chunk, (below_g, below_u, ((zero,) * 4, (zero,) * 4)))
            for acc, dcw_ref, dcb_ref in ((accs[0], dcwg_ref, dcbg_ref), (accs[1], dcwu_ref, dcbu_ref)):
                for t in range(3):
                    dcw_ref[t:t + 1, lanes] += jnp.sum(acc[t], axis=0, keepdims=True)
                dcb_ref[:, lanes] += jnp.sum(acc[3], axis=0, keepdims=True)
            return 0

        lax.fori_loop(0, CONV_COLS // LANES, lane_group, 0)

    g, u = sp["gate"], sp["up"]
    return pl.pallas_call(
        body, name=name, grid=(D_FF // CONV_COLS, nt),
        in_specs=[g["main"], g["prev"], g["nxt"], u["main"], u["prev"], u["nxt"], g["main"], g["nxt"],
                  g["w"], u["w"], g["b"], u["b"]],
        out_specs=[g["main"], g["main"], g["w"], g["w"], g["b"], g["b"]],
        out_shape=[jax.ShapeDtypeStruct((S, D_FF), BF16)] * 2 + [jax.ShapeDtypeStruct((3, D_FF), F32)] * 2
        + [jax.ShapeDtypeStruct((1, D_FF), F32)] * 2,
        scratch_shapes=[pltpu.VMEM((ROWS + 2 * HALO, CONV_COLS), F32)] * 2 + [pltpu.VMEM((ROWS + HALO, CONV_COLS), F32)],
        compiler_params=_cparams(("parallel", "arbitrary")),
    )(hid, hid, hid, hid, hid, hid, dact, dact, cw, cw, cb, cb)


REL_PAD = 384
DIAG = 1024


def _band_valid():
    qc = lax.broadcasted_iota(jnp.int32, (BQ, KWIN), 0) // CHUNK
    kc = lax.broadcasted_iota(jnp.int32, (BQ, KWIN), 1) // CHUNK - LEFT_CHUNKS
    return (kc <= qc) & (kc >= qc - LEFT_CHUNKS)


def _rel_index(offset):
    return jnp.clip(BAND - offset, -MAX_REL, MAX_REL) + MAX_REL


def _split3(x):
    hi = x.astype(BF16)
    rest = x - hi.astype(F32)
    mid = rest.astype(BF16)
    return hi, mid, (rest - mid.astype(F32)).astype(BF16)


def _bias_table(rel_bias, name):
    def body(rb_ref, o_ref):
        t = lax.broadcasted_iota(jnp.int32, (REL_PAD, DIAG), 0)
        lane = lax.broadcasted_iota(jnp.int32, (REL_PAD, DIAG), 1)
        pick = jnp.where(t == _rel_index(lane - BQ), 1.0, 0.0).astype(BF16)
        base = sum(_dot(piece, pick, NN) for piece in _split3(rb_ref[...]))
        valid = _band_valid()
        for h in range(N_HEADS):
            rows = jnp.broadcast_to(base[h:h + 1], (BQ, DIAG))
            rolled = pltpu.roll(rows, 0, 1, stride=1, stride_axis=0)
            o_ref[h] = jnp.where(valid, rolled[:, BQ:], NEG)

    return pl.pallas_call(
        body, name=name,
        out_shape=jax.ShapeDtypeStruct((N_HEADS, BQ, KWIN), F32),
        compiler_params=_cparams(),
    )(rel_bias)


def _bias_table_bwd(dtab, name):
    def body(d_ref, o_ref, diag_ref):
        r = lax.broadcasted_iota(jnp.int32, (BQ, BQ), 0)
        c = lax.broadcasted_iota(jnp.int32, (BQ, BQ), 1)
        flip = jnp.where(r + c == BQ - 1, 1.0, 0.0).astype(BF16)
        for h in range(N_HEADS):
            flipped = sum(_dot(flip, piece, NN) for piece in _split3(d_ref[h]))
            padded = jnp.concatenate([flipped, jnp.zeros((BQ, DIAG - KWIN), F32)], axis=1)
            rolled = pltpu.roll(padded, DIAG - (BQ - 1), 1, stride=1, stride_axis=0)
            diag_ref[h:h + 1, :] = jnp.sum(rolled, axis=0, keepdims=True)
        lane = lax.broadcasted_iota(jnp.int32, (DIAG, REL_PAD), 0)
        t = lax.broadcasted_iota(jnp.int32, (DIAG, REL_PAD), 1)
        offset = jnp.where(lane < KWIN, lane, lane - DIAG)
        pick = jnp.where(t == _rel_index(offset), 1.0, 0.0).astype(BF16)
        o_ref[...] = sum(_dot(piece, pick, NN) for piece in _split3(diag_ref[...]))

    return pl.pallas_call(
        body, name=name,
        out_shape=jax.ShapeDtypeStruct((N_HEADS, REL_PAD), F32),
        scratch_shapes=[pltpu.VMEM((N_HEADS, DIAG), F32)],
        compiler_params=_cparams(),
    )(dtab)


def _head_masks(heads=2):
    lane = lax.broadcasted_iota(jnp.int32, (1, heads * HEAD_DIM), 1)
    return [lane // HEAD_DIM == h for h in range(heads)]


def _own_lanes(masks, vals):
    out = vals[-1]
    for m, val in zip(masks[-2::-1], vals[-2::-1]):
        out = jnp.where(m, val, out)
    return out


CA_HEADS = 4
CA_LANES = CA_HEADS * HEAD_DIM


def _ca_window_specs(nq, col_off=0):
    return [pl.BlockSpec((BQ, CA_LANES), functools.partial(
        lambda p, i, d: (jnp.clip(i - 2 + d, 0, nq - 1), p + col_off), d=d)) for d in range(3)]


def _softmax_rows(s):
    p = jnp.exp(s - jnp.max(s, axis=-1, keepdims=True))
    return p, jnp.sum(p, axis=-1, keepdims=True)


def _ca_scores(qm, kc, tab_h, i):
    col = lax.broadcasted_iota(jnp.int32, (1, KWIN), 1)
    in_seq = col + (i - 2) * BQ >= 0
    return jnp.where(in_seq, _dot(qm, kc, NT) + tab_h, NEG)


def _ca_fwd(qn, kn, v, tab, name, v_off=0, gather=None):
    S = qn.shape[0]
    nq = S // BQ
    groups = WIDTH // CA_LANES
    qspec = pl.BlockSpec((BQ, CA_LANES), lambda p, i: (i, p))
    tspec = pl.BlockSpec((CA_HEADS, BQ, KWIN), lambda p, i: (p, 0, 0))
    n_side = len(gather.arrays) if gather else 0

    def body(*refs):
        q_ref, k0, k1, k2, v0, v1, v2, tab_ref = refs[:8]
        o_ref, ot_ref = refs[8 + n_side:10 + n_side]
        p, i = pl.program_id(0), pl.program_id(1)
        if gather:
            start, forward, finish = gather.steps(refs[8:8 + n_side], refs[10 + n_side:10 + 2 * n_side],
                                                  refs[10 + 2 * n_side:])
            pl.when((p == 0) & (i == 0))(start)
            pl.when((p == groups - 1) & (i == nq // 2))(forward)
        kc = jnp.concatenate([k0[...], k1[...], k2[...]], axis=0)
        vc = jnp.concatenate([v0[...], v1[...], v2[...]], axis=0)
        qv = q_ref[...]
        masks = _head_masks(CA_HEADS)
        heads = range(CA_HEADS)
        s = [_ca_scores(jnp.where(masks[h], qv, 0), kc, tab_ref[h], i) for h in heads]
        soft = [_softmax_rows(s[h]) for h in heads]
        o = [_dot(soft[h][0].astype(BF16), vc, NN) / soft[h][1] for h in heads]
        out = _own_lanes(masks, o)
        o_ref[...] = out.astype(BF16)
        ot_ref[...] = out.T.astype(BF16)
        if gather:
            pl.when((p == groups - 1) & (i == nq - 1))(finish)

    side = gather.arrays if gather else []
    return pl.pallas_call(
        body, name=name, grid=(groups, nq),
        in_specs=[qspec] + _ca_window_specs(nq) + _ca_window_specs(nq, v_off) + [tspec] + [ANY] * n_side,
        out_specs=[qspec, pl.BlockSpec((CA_LANES, BQ), lambda p, i: (p, i))] + [ANY] * n_side,
        out_shape=[jax.ShapeDtypeStruct((S, WIDTH), BF16), jax.ShapeDtypeStruct((WIDTH, S), BF16)]
        + (gather.out_shape if gather else []),
        scratch_shapes=gather.scratch if gather else [],
        compiler_params=_cparams(("arbitrary", "arbitrary") if gather else ("parallel", "parallel")),
    )(qn, kn, kn, kn, v, v, v, tab, *side)


def _ca_bwd(qn, kn, v, do, tab, name, v_off=0):
    S = qn.shape[0]
    nq = S // BQ
    qspec = pl.BlockSpec((BQ, CA_LANES), lambda p, i: (jnp.minimum(i, nq - 1), p))
    kout = pl.BlockSpec((BQ, CA_LANES), lambda p, i: (jnp.clip(i - 2, 0, nq - 1), p))
    tspec = pl.BlockSpec((CA_HEADS, BQ, KWIN), lambda p, i: (p, 0, 0))

    def body(q_ref, do_ref, k0, k1, k2, v0, v1, v2, tab_ref,
             dq_ref, dk_ref, dv_ref, dtab_ref, dk_acc, dv_acc):
        i = pl.program_id(1)

        @pl.when(i == 0)
        def _():
            dk_acc[...] = jnp.zeros_like(dk_acc)
            dv_acc[...] = jnp.zeros_like(dv_acc)
            dtab_ref[...] = jnp.zeros_like(dtab_ref)

        @pl.when(i < nq)
        def _():
            kc = jnp.concatenate([k0[...], k1[...], k2[...]], axis=0)
            vc = jnp.concatenate([v0[...], v1[...], v2[...]], axis=0)
            qv, dov = q_ref[...], do_ref[...]
            masks = _head_masks(CA_HEADS)
            heads = range(CA_HEADS)
            qm = [jnp.where(masks[h], qv, 0) for h in heads]
            dom = [jnp.where(masks[h], dov, 0) for h in heads]
            s = [_ca_scores(qm[h], kc, tab_ref[h], i) for h in heads]
            dp = [_dot(dom[h], vc, NT) for h in heads]
            soft = [_softmax_rows(s[h]) for h in heads]
            p = [soft[h][0] / soft[h][1] for h in heads]
            ds = [p[h] * (dp[h] - jnp.sum(p[h] * dp[h], axis=-1, keepdims=True)) for h in heads]
            for h in heads:
                dtab_ref[h] += ds[h]
            dsb = [ds[h].astype(BF16) for h in heads]
            pb = [p[h].astype(BF16) for h in heads]
            dq = [_dot(dsb[h], kc, NN) for h in heads]
            dq_ref[...] = _own_lanes(masks, dq)
            dkc = sum(_dot(dsb[h], qm[h], TN) for h in heads)
            dvc = sum(_dot(pb[h], dom[h], TN) for h in heads)
            for d in range(3):
                slot = (i + 1 + d) % 3
                dk_acc[slot] += dkc[d * BQ:(d + 1) * BQ]
                dv_acc[slot] += dvc[d * BQ:(d + 1) * BQ]

        @pl.when(i >= 2)
        def _():
            slot = (i + 1) % 3
            dk_ref[...] = dk_acc[slot]
            dv_ref[...] = dv_acc[slot].astype(BF16)
            dk_acc[slot] = jnp.zeros((BQ, CA_LANES), F32)
            dv_acc[slot] = jnp.zeros((BQ, CA_LANES), F32)

    return pl.pallas_call(
        body, name=name, grid=(WIDTH // CA_LANES, nq + 2),
        in_specs=[qspec, qspec] + _ca_window_specs(nq) + _ca_window_specs(nq, v_off) + [tspec],
        out_specs=[qspec, kout, kout, tspec],
        out_shape=[jax.ShapeDtypeStruct((S, WIDTH), F32), jax.ShapeDtypeStruct((S, WIDTH), F32),
                   jax.ShapeDtypeStruct((S, WIDTH), BF16), jax.ShapeDtypeStruct((N_HEADS, BQ, KWIN), F32)],
        scratch_shapes=[pltpu.VMEM((3, BQ, CA_LANES), F32)] * 2,
        compiler_params=_cparams(("parallel", "arbitrary")),
    )(qn, do, kn, kn, kn, v, v, v, tab)


def _sb_consts():
    r = lax.broadcasted_iota(jnp.int32, (BQ, BQ), 0)
    c = lax.broadcasted_iota(jnp.int32, (BQ, BQ), 1)
    from_s = jnp.where(r >= c, 1.0, 0.0).astype(BF16)
    causal = c < r
    return from_s, causal


def _suffix_sum(t, from_s):
    hi, lo = _split_bf16(t)
    return _dot(hi, from_s, NN) + _dot(lo, from_s, NN)


def _neg_abs(x):
    bits = lax.bitcast_convert_type(x, jnp.uint32) | jnp.uint32(0x80000000)
    return lax.bitcast_convert_type(bits, F32)


def _sb_log_keep(zn):
    return jnp.minimum(zn, 0.0) - jnp.log(1.0 + jnp.exp(_neg_abs(zn)))


SB_DEAD = 105.0


SB_QB = 2


def _sb_walk(ip, tiles, keep_ref):
    i0 = SB_QB * ip

    @pl.when(ip == 0)
    def _():
        tiles([(0, [0], [True]), (1, [1, 0], [True, False])])

    @pl.when(ip > 0)
    def _():
        tiles([(a, [i0 + a, i0 + a - 1], [True, False]) for a in range(SB_QB)])

    for a in range(SB_QB):
        def alive(a=a):
            return (jnp.max(keep_ref[2 * a:2 * a + 2]) > -SB_DEAD).astype(jnp.int32)

        def step(state, a=a, alive=alive):
            j, _ = state
            tiles([(a, [j], [False])])
            return j - 1, alive()

        lax.while_loop(lambda state: (state[0] >= 0) & (state[1] > 0), step, (i0 + a - 2, alive()))


def _sb_rows(j):
    return pl.ds(pl.multiple_of(j * BQ, BQ), BQ)


def _sb_chains(groups):
    chains = [(a, n, h) for a, js, _ in groups for n in range(len(js)) for h in range(2)]
    block = {(a, n): j for a, js, _ in groups for n, j in enumerate(js)}
    masked = [(a, n, h) for a, _, diags in groups for n, d in enumerate(diags) if d for h in range(2)]
    return chains, block, masked


def _sb_running(ref, vals, groups):
    before_chain = {}
    for a, js, _ in groups:
        for h in range(2):
            run = ref[2 * a + h]
            for n in range(len(js)):
                before_chain[(a, n, h)] = run
                run = run + jnp.sum(vals[(a, n, h)], axis=-1, keepdims=True)
            ref[2 * a + h] = run
    return before_chain


def _sb_specs(S, offs):
    def qspec(off=0):
        return pl.BlockSpec((SB_QB * BQ, PAIR), lambda p, i: (i, p + off))

    def kspec(off=0):
        return pl.BlockSpec((S, PAIR), lambda p, i: (0, p + off), pipeline_mode=pl.Buffered(1))

    return qspec, kspec, [qspec(offs[0]), kspec(offs[1]), kspec(offs[2])]


def _sb_fwd(q, k, v, name, offs=(0, 0, 0)):
    S = q.shape[0]
    steps = S // (SB_QB * BQ)
    qspec, _, qkv_specs = _sb_specs(S, offs)

    def body(q_ref, k_ref, v_ref, o_ref, of_ref, ot_ref, carry_ref, acc_ref):
        ip = pl.program_id(1)
        from_s, causal = _sb_consts()
        masks = _head_masks()
        qn = q_ref[...] * -(HEAD_DIM ** -0.5)
        qms = {(a, h): jnp.where(masks[h], qn[a * BQ:(a + 1) * BQ], 0) for a in range(SB_QB) for h in range(2)}
        carry_ref[...] = jnp.zeros_like(carry_ref)
        acc_ref[...] = jnp.zeros_like(acc_ref)

        def tiles(groups):
            chains, block, masked = _sb_chains(groups)
            kbs = {an: k_ref[_sb_rows(j), :] for an, j in block.items()}
            vbs = {an: v_ref[_sb_rows(j), :] for an, j in block.items()}
            zn = {c: _dot(qms[(c[0], c[2])], kbs[c[:2]], NT) for c in chains}
            log_keep = {c: _sb_log_keep(zn[c]) for c in chains}
            for c in masked:
                log_keep[c] = jnp.where(causal, log_keep[c], 0.0)
            split = {c: _split_bf16(log_keep[c]) for c in chains}
            carry = _sb_running(carry_ref, log_keep, groups)
            suffix = {c: _dot(split[c][0], from_s, NN) + _dot(split[c][1], from_s, NN) for c in chains}
            w = {c: jnp.exp(carry[c] + suffix[c] - zn[c]) for c in chains}
            for c in masked:
                w[c] = jnp.where(causal, w[c], 0.0)
            for c in chains:
                acc_ref[2 * c[0] + c[2]] += _dot(w[c].astype(BF16), vbs[c[:2]], NN)

        _sb_walk(ip, tiles, carry_ref)
        for a in range(SB_QB):
            out = jnp.where(masks[0], acc_ref[2 * a], acc_ref[2 * a + 1])
            o_ref[a * BQ:(a + 1) * BQ, :] = out.astype(BF16)
            of_ref[a * BQ:(a + 1) * BQ, :] = out
            ot_ref[:, a * BQ:(a + 1) * BQ] = out.T.astype(BF16)

    return pl.pallas_call(
        body, name=name, grid=(WIDTH // PAIR, steps),
        in_specs=qkv_specs, out_specs=[qspec(), qspec(), pl.BlockSpec((PAIR, SB_QB * BQ), lambda p, i: (p, i))],
        out_shape=[jax.ShapeDtypeStruct((S, WIDTH), BF16), jax.ShapeDtypeStruct((S, WIDTH), F32),
                   jax.ShapeDtypeStruct((WIDTH, S), BF16)],
        scratch_shapes=[pltpu.VMEM((2 * SB_QB, BQ, 1), F32), pltpu.VMEM((2 * SB_QB, BQ, PAIR), F32)],
        compiler_params=_cparams(("parallel", "arbitrary")),
    )(q, k, v)


def _sb_bwd(q, k, v, o, do, name, offs=(0, 0, 0), exchange=None):
    S = q.shape[0]
    steps = S // (SB_QB * BQ)
    pairs = WIDTH // PAIR
    qspec, kspec, qkv_specs = _sb_specs(S, offs)
    n_side = len(exchange.arrays) if exchange else 0

    def body(*refs):
        q_ref, o_ref, do_ref, k_ref, v_ref = refs[:5]
        dq_ref, dk_ref, dv_ref = refs[5 + n_side:8 + n_side]
        dk_acc, dv_acc, keep_ref, gsum_ref, dq_acc = refs[8 + 2 * n_side:13 + 2 * n_side]
        ip = pl.program_id(1)
        if exchange:
            start, finish = exchange.steps(refs[5:5 + n_side], refs[8 + n_side:8 + 2 * n_side], refs[13 + 2 * n_side:])
            pl.when((pl.program_id(0) == 0) & (ip == 0))(start)

        @pl.when(ip == 0)
        def _():
            dk_acc[...] = jnp.zeros_like(dk_acc)
            dv_acc[...] = jnp.zeros_like(dv_acc)

        from_s, causal = _sb_consts()
        masks = _head_masks()
        qn, dov = q_ref[...] * -(HEAD_DIM ** -0.5), do_ref[...]
        od = o_ref[...] * dov.astype(F32)
        lanes = [(a, h) for a in range(SB_QB) for h in range(2)]
        rows_of = {a: slice(a * BQ, (a + 1) * BQ) for a in range(SB_QB)}
        qms = {(a, h): jnp.where(masks[h], qn[rows_of[a]], 0) for a, h in lanes}
        doms = {(a, h): jnp.where(masks[h], dov[rows_of[a]], 0) for a, h in lanes}
        totals = {(a, h): jnp.sum(jnp.where(masks[h], od[rows_of[a]], 0.0), axis=-1, keepdims=True)
                  for a, h in lanes}
        for ref in (keep_ref, gsum_ref, dq_acc):
            ref[...] = jnp.zeros_like(ref)

        def tiles(groups):
            chains, block, masked = _sb_chains(groups)
            kbs = {an: k_ref[_sb_rows(j), :] for an, j in block.items()}
            vbs = {an: v_ref[_sb_rows(j), :] for an, j in block.items()}
            zn = {c: _dot(qms[(c[0], c[2])], kbs[c[:2]], NT) for c in chains}
            dw = {c: _dot(doms[(c[0], c[2])], vbs[c[:2]], NT) for c in chains}
            log_keep = {c: _sb_log_keep(zn[c]) for c in chains}
            for c in masked:
                log_keep[c] = jnp.where(causal, log_keep[c], 0.0)
            split = {c: _split_bf16(log_keep[c]) for c in chains}
            kept = _sb_running(keep_ref, log_keep, groups)
            suffix = {c: _dot(split[c][0], from_s, NN) + _dot(split[c][1], from_s, NN) for c in chains}
            w = {c: jnp.exp(kept[c] + suffix[c] - zn[c]) for c in chains}
            for c in masked:
                w[c] = jnp.where(causal, w[c], 0.0)
            wb = {c: w[c].astype(BF16) for c in chains}
            g = {c: wb[c].astype(F32) * dw[c] for c in chains}
            gsplit = {c: _split_bf16(g[c]) for c in chains}
            gsum = _sb_running(gsum_ref, g, groups)
            gsuffix = {c: _dot(gsplit[c][0], from_s, NN) + _dot(gsplit[c][1], from_s, NN) for c in chains}
            dzb = {}
            for c in chains:
                before = totals[(c[0], c[2])] - (gsum[c] + gsuffix[c])
                dz = (g[c] + before) * jnp.exp(log_keep[c]) - before
                if c in masked:
                    dz = jnp.where(causal, dz, 0.0)
                dzb[c] = dz.astype(BF16)
            for c in chains:
                rows = _sb_rows(block[c[:2]])
                dq_acc[2 * c[0] + c[2]] += _dot(dzb[c], kbs[c[:2]], NN)
                dk_acc[rows, :] -= _dot(dzb[c], qms[(c[0], c[2])], TN)
                dv_acc[rows, :] += _dot(wb[c], doms[(c[0], c[2])], TN)

        _sb_walk(ip, tiles, keep_ref)
        for a in range(SB_QB):
            dq = jnp.where(masks[0], dq_acc[2 * a], dq_acc[2 * a + 1])
            dq_ref[a * BQ:(a + 1) * BQ, :] = (dq * HEAD_DIM ** -0.5).astype(BF16)

        @pl.when(ip == steps - 1)
        def _():
            dk_ref[...] = dk_acc[...].astype(BF16)
            dv_ref[...] = dv_acc[...].astype(BF16)

        if exchange:
            pl.when((pl.program_id(0) == pairs - 1) & (ip == steps - 1))(finish)

    side = exchange.arrays if exchange else []
    return pl.pallas_call(
        body, name=name, grid=(pairs, steps),
        in_specs=[qkv_specs[0], qspec(), qspec(), qkv_specs[1], qkv_specs[2]] + [ANY] * n_side,
        out_specs=[qspec(), kspec(), kspec()] + [ANY] * n_side,
        out_shape=[jax.ShapeDtypeStruct((S, WIDTH), BF16)] * 3 + (exchange.out_shape if exchange else []),
        scratch_shapes=[pltpu.VMEM((S, PAIR), F32)] * 2 + [pltpu.VMEM((2 * SB_QB, BQ, 1), F32)] * 2
        + [pltpu.VMEM((2 * SB_QB, BQ, PAIR), F32)] + (exchange.scratch if exchange else []),
        compiler_params=_cparams(("arbitrary", "arbitrary") if exchange else ("parallel", "arbitrary")),
    )(q, o, do, k, v, *side)


ANY = pl.BlockSpec(memory_space=pl.ANY)


def _place():
    return lax.axis_index("x"), lax.axis_index("y"), lax.axis_index("c")


def _other_chips(x, y):
    return [(2 * px + py, (px, py)) for px, py in ((1 - x, y), (x, 1 - y), (1 - x, 1 - y))]


def _remote(src, dst, sems, k, to):
    return pltpu.make_async_remote_copy(src_ref=src, dst_ref=dst, send_sem=sems[0].at[k], recv_sem=sems[1].at[k],
                                        device_id=to, device_id_type=MESH)


class _Gather:
    def __init__(self, ws, extras=()):
        self.n, self.m = len(ws), len(extras)
        self.arrays = list(ws) + list(extras)
        self.n_copies = 6 * self.n + 3 * self.m
        self.out_shape = [jax.ShapeDtypeStruct((N_CHIPS,) + a.shape, a.dtype) for a in self.arrays]
        self.scratch = [pltpu.SemaphoreType.DMA((self.n_copies,)), pltpu.SemaphoreType.DMA((self.n_copies,))]

    def steps(self, in_refs, out_refs, sems):
        n = self.n
        x, y, c = _place()
        me = 2 * x + y
        chips = _other_chips(x, y)
        sibling = (x, y, 1 - c)

        def halves(ref):
            rh = ref.shape[-2] // 2
            return pl.ds(c * rh, rh), pl.ds((1 - c) * rh, rh)

        def first():
            cps = [_remote(w_ref.at[halves(w_ref)[0]], o_ref.at[me, halves(w_ref)[0]], sems, 6 * a + k, (*xy, c))
                   for a, (w_ref, o_ref) in enumerate(zip(in_refs[:n], out_refs[:n])) for k, (_, xy) in enumerate(chips)]
            return cps + [_remote(e_ref, eo_ref.at[me], sems, 6 * n + 3 * b + k, (*xy, c))
                          for b, (e_ref, eo_ref) in enumerate(zip(in_refs[n:], out_refs[n:]))
                          for k, (_, xy) in enumerate(chips)]

        def passed():
            return [_remote(o_ref.at[chip, halves(o_ref)[0]], o_ref.at[chip, halves(o_ref)[0]], sems, 6 * a + 3 + k, sibling)
                    for a, o_ref in enumerate(out_refs[:n]) for k, (chip, _) in enumerate(chips)]

        def start():
            for cp in first():
                cp.start()

        def forward():
            for a, o_ref in enumerate(out_refs[:n]):
                for k, (chip, xy) in enumerate(chips):
                    landed = o_ref.at[chip, halves(o_ref)[0]]
                    _remote(landed, landed, sems, 6 * a + k, (*xy, c)).wait_recv()
            for cp in passed():
                cp.start()

        def finish():
            for a, o_ref in enumerate(out_refs[:n]):
                for k, (chip, _) in enumerate(chips):
                    landed = o_ref.at[chip, halves(o_ref)[1]]
                    _remote(landed, landed, sems, 6 * a + 3 + k, sibling).wait_recv()
            for b, (e_ref, eo_ref) in enumerate(zip(in_refs[n:], out_refs[n:])):
                for k, (chip, xy) in enumerate(chips):
                    _remote(e_ref, eo_ref.at[chip], sems, 6 * n + 3 * b + k, (*xy, c)).wait_recv()
            for cp in first() + passed():
                cp.wait_send()

        return start, forward, finish


class _ChipExchange:
    def __init__(self, ps):
        self.arrays = list(ps)
        self.out_shape = [jax.ShapeDtypeStruct(p.shape, p.dtype) for p in ps]
        self.scratch = [pltpu.SemaphoreType.DMA((3 * len(ps),)), pltpu.SemaphoreType.DMA((3 * len(ps),))]

    def steps(self, p_refs, out_refs, sems):
        x, y, c = _place()
        me = 2 * x + y
        chips = _other_chips(x, y)

        def copies():
            return [_remote(p_ref.at[chip], o_ref.at[me], sems, 3 * a + k, (*xy, c))
                    for a, (p_ref, o_ref) in enumerate(zip(p_refs, out_refs)) for k, (chip, xy) in enumerate(chips)]

        def start():
            for cp in copies():
                cp.start()

        def finish():
            for a, (p_ref, o_ref) in enumerate(zip(p_refs, out_refs)):
                for k, (chip, xy) in enumerate(chips):
                    _remote(p_ref.at[chip], o_ref.at[chip], sems, 3 * a + k, (*xy, c)).wait_recv()
            for cp in copies():
                cp.wait_send()

        return start, finish


def _exchange_cores(gs, name, small=None):
    n = len(gs)
    m = 0 if small is None else 1

    def body(*refs):
        g_refs, sib_refs = refs[:n], refs[n + m:2 * n + m]
        sems = refs[2 * (n + m):]
        x, y, c = _place()
        me = 4 * x + 2 * y + c
        copies = []
        for a, (g_ref, sib_ref) in enumerate(zip(g_refs, sib_refs)):
            rh = g_ref.shape[1] // 2
            copies.append(_remote(g_ref.at[:, pl.ds((1 - c) * rh, rh), :], sib_ref, sems, a, (x, y, 1 - c)))
        if m:
            small_ref, all_ref = refs[n], refs[2 * n + m]
            k = n
            for fx in (0, 1):
                for fy in (0, 1):
                    for fc in (0, 1):
                        if fx or fy or fc:
                            to = (1 - x if fx else x, 1 - y if fy else y, 1 - c if fc else c)
                            copies.append(_remote(small_ref, all_ref.at[me], sems, k, to))
                            k += 1
        for cp in copies:
            cp.start()
        for cp in copies:
            cp.wait_recv()
        for cp in copies:
            cp.wait_send()

    n_copies = n + m * (N_DEV - 1)
    args = list(gs) + ([small] if m else [])
    return pl.pallas_call(
        body, name=name, in_specs=[ANY] * (n + m), out_specs=[ANY] * (n + m),
        out_shape=[jax.ShapeDtypeStruct((N_CHIPS, g.shape[1] // 2, g.shape[2]), F32) for g in gs]
        + ([jax.ShapeDtypeStruct((N_DEV,) + small.shape, F32)] if m else []),
        scratch_shapes=[pltpu.SemaphoreType.DMA((n_copies,)), pltpu.SemaphoreType.DMA((n_copies,))],
    )(*args)


def _share_halves(ghs, name):
    n = len(ghs)

    def body(*refs):
        gh_refs, out_refs, sems = refs[:n], refs[n:2 * n], refs[2 * n:]
        x, y, c = _place()
        copies = [_remote(gh_ref, o_ref, sems, a, (x, y, 1 - c)) for a, (gh_ref, o_ref) in enumerate(zip(gh_refs, out_refs))]
        for cp in copies:
            cp.start()
        for cp in copies:
            cp.wait_recv()
        for cp in copies:
            cp.wait_send()

    return pl.pallas_call(
        body, name=name, in_specs=[ANY] * n, out_specs=[ANY] * n,
        out_shape=[jax.ShapeDtypeStruct(g.shape, g.dtype) for g in ghs],
        scratch_shapes=[pltpu.SemaphoreType.DMA((n,)), pltpu.SemaphoreType.DMA((n,))],
    )(*ghs)


EW_BLOCK_BYTES = 2 * 1024 * 1024


def _row_block(rows, cols, mult=8):
    fits = [b for b in range(mult, rows + 1, mult) if rows % b == 0 and b * cols * 4 <= EW_BLOCK_BYTES]
    return max(fits) if fits else mult


def _add2(a, b, name):
    R, C = a.shape
    rows = _row_block(R, C, mult=16)
    spec = pl.BlockSpec((rows, C), lambda i: (i, 0))

    def body(a_ref, b_ref, o_ref):
        o_ref[...] = (a_ref[...] + b_ref[...]).astype(BF16)

    return pl.pallas_call(
        body, name=name, grid=(R // rows,), in_specs=[spec, spec], out_specs=spec,
        out_shape=jax.ShapeDtypeStruct(a.shape, BF16),
        compiler_params=_cparams(("parallel",)),
    )(a, b)


def _sum_leading(a, name):
    n, R, C = a.shape
    rows = _row_block(R, n * C, mult=16 if a.dtype == BF16 else 8)

    def body(a_ref, o_ref):
        acc = a_ref[0].astype(F32)
        for j in range(1, n):
            acc = acc + a_ref[j].astype(F32)
        o_ref[...] = acc

    return pl.pallas_call(
        body, name=name, grid=(R // rows,),
        in_specs=[pl.BlockSpec((n, rows, C), lambda i: (0, i, 0))],
        out_specs=pl.BlockSpec((rows, C), lambda i: (i, 0)),
        out_shape=jax.ShapeDtypeStruct((R, C), F32),
        compiler_params=_cparams(("parallel",)),
    )(a)


def _adamw(w, g, m, v, name):
    R, C = w.shape
    rows = _row_block(R, C)
    spec = pl.BlockSpec((rows, C), lambda i: (i, 0))

    def body(w_ref, g_ref, m_ref, v_ref, d_ref, mo_ref, vo_ref):
        gv = g_ref[...]
        mn = ADAM_B1 * m_ref[...] + (1.0 - ADAM_B1) * gv
        vn = ADAM_B2 * v_ref[...] + (1.0 - ADAM_B2) * (gv * gv)
        m_hat = mn / (1.0 - ADAM_B1 ** ADAM_STEP)
        v_hat = vn / (1.0 - ADAM_B2 ** ADAM_STEP)
        d_ref[...] = -ADAM_LR * (m_hat / (jnp.sqrt(v_hat) + ADAM_EPS) + ADAM_WD * w_ref[...])
        mo_ref[...] = mn
        vo_ref[...] = vn

    return pl.pallas_call(
        body, name=name, grid=(R // rows,), in_specs=[spec] * 4, out_specs=[spec] * 3,
        out_shape=[jax.ShapeDtypeStruct((R, C), F32)] * 3,
        compiler_params=_cparams(("parallel",)),
    )(w, g, m, v)


BIG = ("w_in", "w_branch_a", "w_branch_b", "w_out", "w_ffn_up", "w_ffn_down")
COL_SHARDED = {"w_in": True, "w_branch_a": True, "w_branch_b": True, "w_out": False, "w_ffn_up": True,
               "w_ffn_down": False}
CONV_W_COLS = 2 * D_FF // N_CHIPS
SMALL_REPLICATED = (("norm1_g", D_MODEL), ("q_norm_g", HEAD_DIM), ("k_norm_g", HEAD_DIM),
                    ("rel_bias", N_HEADS * N_REL), ("norm2_g", D_MODEL), ("ffn_conv_b", 2 * D_FF))
SMALL_GRADS = SMALL_REPLICATED + (("ffn_conv_w", 3 * 2 * D_FF),)
SMALL_OWN = SMALL_REPLICATED + (("ffn_conv_w", 3 * CONV_W_COLS),)
SMALL_GRAD_ROWS = 32
SMALL_OWN_ROWS = 16


def _whole(name, stacked):
    return jnp.concatenate(list(stacked), axis=1) if COL_SHARDED[name] else stacked.reshape(-1, stacked.shape[2])


def _pack_small(vals, sizes, rows):
    flat = jnp.concatenate([vals[n].reshape(-1) for n, _ in sizes])
    return jnp.pad(flat, (0, rows * PACK_COLS - flat.shape[0])).reshape(rows, PACK_COLS)


def _unpack_small(packed, sizes):
    flat, out, o = packed.reshape(-1), {}, 0
    for n, sz in sizes:
        out[n] = flat[o:o + sz]
        o += sz
    return out


def kernel(x, norm1_g, w_in, q_norm_g, k_norm_g, rel_bias, w_branch_a, w_branch_b, w_out, norm2_g, w_ffn_up, ffn_conv_w, ffn_conv_b, w_ffn_down, loss_target, m_norm1_g, m_w_in, m_q_norm_g, m_k_norm_g, m_rel_bias, m_w_branch_a, m_w_branch_b, m_w_out, m_norm2_g, m_w_ffn_up, m_ffn_conv_w, m_ffn_conv_b, m_w_ffn_down, v_norm1_g, v_w_in, v_q_norm_g, v_k_norm_g, v_rel_bias, v_w_branch_a, v_w_branch_b, v_w_out, v_norm2_g, v_w_ffn_up, v_ffn_conv_w, v_ffn_conv_b, v_w_ffn_down):
    w_big = {"w_in": w_in[0], "w_branch_a": w_branch_a[0], "w_branch_b": w_branch_b[0], "w_out": w_out[0],
             "w_ffn_up": w_ffn_up[0], "w_ffn_down": w_ffn_down[0]}
    m_big = {"w_in": m_w_in[0], "w_branch_a": m_w_branch_a[0], "w_branch_b": m_w_branch_b[0], "w_out": m_w_out[0],
             "w_ffn_up": m_w_ffn_up[0], "w_ffn_down": m_w_ffn_down[0]}
    v_big = {"w_in": v_w_in[0], "w_branch_a": v_w_branch_a[0], "w_branch_b": v_w_branch_b[0], "w_out": v_w_out[0],
             "w_ffn_up": v_w_ffn_up[0], "w_ffn_down": v_w_ffn_down[0]}
    xs, tgt = x[0], loss_target[0]

    xi, yi, ci = _place()
    chip = 2 * xi + yi

    def with_own(stacked, own):
        return lax.dynamic_update_slice(stacked, own[None], (chip,) + (0,) * own.ndim)

    shards_bf = {n: w_big[n].astype(BF16) for n in BIG}
    conv_own = jnp.pad(ffn_conv_w[0], ((0, 8 - ffn_conv_w.shape[1]), (0, 0)))
    later = [n for n in BIG if n != "w_in"]

    hn, hn_t, w_in_g = _rms_fwd(xs, norm1_g, "rms1", gather=_Gather([shards_bf["w_in"]]))
    w_in_f = _whole("w_in", with_own(w_in_g, shards_bf["w_in"]))
    w_in_t = w_in_f.T
    qk = _matmul(hn, w_in_f[:, :2 * WIDTH], F32, "proj_qk")
    vqkv = _matmul(hn, w_in_f[:, 2 * WIDTH:6 * WIDTH], BF16, "proj_vqkv")
    gates = _matmul(hn, w_in_f[:, 6 * WIDTH:], BF16, "proj_gates")
    gq = jnp.tile(q_norm_g, (1, N_HEADS))
    gk = jnp.tile(k_norm_g, (1, N_HEADS))
    qa, ka = _qknorm_fwd(qk, gq, gk, "qknorm")
    per = WIDTH // PAIR
    b_offs = (per, 2 * per, 3 * per)
    tab = _bias_table(jnp.pad(rel_bias[0], ((0, 0), (0, REL_PAD - N_REL))), "bias_table")
    out_a, out_a_t, *gathered = _ca_fwd(qa, ka, vqkv, tab, "chunk_attn",
                                        gather=_Gather([shards_bf[n] for n in later], [conv_own]))
    full = {n: _whole(n, with_own(g, shards_bf[n])) for n, g in zip(later, gathered)}
    conv_w = jnp.concatenate(list(with_own(gathered[-1], conv_own)[:, :3]), axis=1)
    w_a, w_b, w_o, w_up, w_dn = (full[n] for n in later)
    w_a_t, w_b_t, w_o_t, w_up_t, w_dn_t = (w.T for w in (w_a, w_b, w_o, w_up, w_dn))
    out_b, out_b_f32, out_b_t = _sb_fwd(vqkv, vqkv, vqkv, "stick_attn", b_offs)
    y_a = _matmul(out_a, w_a, BF16, "branch_a")
    y_b = _matmul(out_b, w_b, BF16, "branch_b")
    mixed, mixed_t = _mix_fwd(gates, y_a, y_b, "mix")
    x2 = _matmul(mixed, w_o, F32, "out_proj", residual=xs)
    hn2, hn2_t = _rms_fwd(x2, norm2_g, "rms2")
    hid = _matmul(hn2, w_up, BF16, "ffn_up")
    act, act_t = _convglu_fwd(hid, conv_w, ffn_conv_b, "convglu")
    dy, dyb, sq = _down_and_loss(act, w_dn, x2, tgt, "ffn_down_loss")
    loss = lax.psum(0.5 / D_MODEL * jnp.sum(sq), ("x", "y", "c"))

    dact = _matmul(dyb, w_dn_t, BF16, "d_act")
    d_w_dn = _matmul(act_t, dyb, F32, "d_w_down")
    dhg, dhu, dcwg, dcwu, dcbg, dcbu = _convglu_bwd(hid, dact, conv_w, ffn_conv_b, "convglu_bwd")
    half_chips = N_CHIPS // 2
    d_w_up = jnp.concatenate([_matmul(hn2_t, dhg, F32, "d_w_up_gate", slabs=half_chips),
                              _matmul(hn2_t, dhu, F32, "d_w_up_up", slabs=half_chips)], axis=0)
    dhn2 = _matmul(dhg, w_up_t[:D_FF], F32, "d_hn2_gate")
    dhn2 = _matmul(dhu, w_up_t[D_FF:], F32, "d_hn2_up", residual=dhn2)
    dx2, dx2b, d_norm2 = _rms_bwd(x2, norm2_g, dhn2, dy, "rms2_bwd")
    dmixed = _matmul(dx2b, w_o_t, BF16, "d_mixed")
    d_w_o = _matmul(mixed_t, dx2b, F32, "d_w_out")
    dgates, dya, dyb_b = _mix_bwd(dmixed, gates, y_a, y_b, "mix_bwd")
    d_w_a, d_w_b = (_matmul(o_t, d, F32, nm).reshape(WIDTH, N_CHIPS, -1).transpose(1, 0, 2)
                    for o_t, d, nm in ((out_a_t, dya, "d_w_branch_a"), (out_b_t, dyb_b, "d_w_branch_b")))
    do_a = _matmul(dya, w_a_t, BF16, "d_out_a")
    do_b = _matmul(dyb_b, w_b_t, BF16, "d_out_b")

    def core_sums(names, gs, sibs):
        out = {}
        for n, g, sib in zip(names, gs, sibs):
            rh, cols = sib.shape[1], sib.shape[2]
            mine = lax.dynamic_slice_in_dim(g, ci * rh, rh, axis=1)
            out[n] = _add2(mine.reshape(-1, cols), sib.reshape(-1, cols), "sum_cores_" + n).reshape(sib.shape)
        return out

    grads_full = {"w_branch_a": d_w_a, "w_branch_b": d_w_b, "w_ffn_up": d_w_up,
                  "w_out": d_w_o.reshape(N_CHIPS, -1, D_MODEL), "w_ffn_down": d_w_dn.reshape(N_CHIPS, -1, D_MODEL)}
    early = [grads_full[n] for n in later]
    chip_parts = core_sums(later, early, _exchange_cores(early, "exchange_cores_early"))
    dqb, dkb, dvb, *parts_early = _sb_bwd(vqkv, vqkv, vqkv, out_b_f32, do_b, "stick_attn_bwd", b_offs,
                                           exchange=_ChipExchange([chip_parts[n] for n in later]))
    parts = dict(zip(later, parts_early))
    dqa_n, dka_n, dva, dtab = _ca_bwd(qa, ka, vqkv, do_a, tab, "chunk_attn_bwd")
    d_rel = _bias_table_bwd(dtab, "bias_table_bwd")[:, :N_REL]
    dqa, dka, dgq, dgk = _qknorm_bwd(qk, gq, gk, dqa_n, dka_n, "qknorm_bwd")
    dproj = jnp.concatenate([dqa, dka, dva, dqb, dkb, dvb, dgates], axis=1)
    d_w_in = _matmul(hn_t, dproj, F32, "d_w_in", slabs=N_CHIPS)
    chip_parts.update(core_sums(["w_in"], [d_w_in], _exchange_cores([d_w_in], "exchange_cores_w_in")))
    dhn, parts["w_in"] = _matmul(dproj, w_in_t, F32, "d_hn", exchange=_ChipExchange([chip_parts["w_in"]]))
    dx, _, d_norm1 = _rms_bwd(xs, norm1_g, dhn, dx2, "rms1_bwd")

    small_g = _pack_small({"norm1_g": d_norm1, "q_norm_g": dgq.reshape(N_HEADS, HEAD_DIM).sum(0),
                           "k_norm_g": dgk.reshape(N_HEADS, HEAD_DIM).sum(0), "rel_bias": d_rel,
                           "norm2_g": d_norm2, "ffn_conv_b": jnp.concatenate([dcbg, dcbu], axis=1),
                           "ffn_conv_w": jnp.concatenate([dcwg, dcwu], axis=1)}, SMALL_GRADS, SMALL_GRAD_ROWS)
    (small_all,) = _exchange_cores([], "exchange_small", small=small_g)
    g_halves = [_sum_leading(with_own(parts[n], lax.dynamic_index_in_dim(chip_parts[n], chip, 0, keepdims=False)),
                             "sum_chips_" + n) for n in BIG]
    g_others = _share_halves(g_halves, "share_halves")
    grads = {n: jnp.concatenate([jnp.where(ci == 0, mine, other), jnp.where(ci == 0, other, mine)], axis=0)
             for n, mine, other in zip(BIG, g_halves, g_others)}
    small_all = lax.dynamic_update_slice(small_all, small_g[None], (4 * xi + 2 * yi + ci, 0, 0))
    small_sum = _unpack_small(_sum_leading(small_all, "sum_small"), SMALL_GRADS)
    small_sum["ffn_conv_w"] = lax.dynamic_slice_in_dim(small_sum["ffn_conv_w"].reshape(3, 2 * D_FF),
                                                       chip * CONV_W_COLS, CONV_W_COLS, axis=1)

    deltas, new_m, new_v = {}, {}, {}
    for n in BIG:
        deltas[n], new_m[n], new_v[n] = _adamw(w_big[n], grads[n], m_big[n], v_big[n], "adamw_" + n)

    shapes = {"norm1_g": norm1_g.shape, "q_norm_g": q_norm_g.shape, "k_norm_g": k_norm_g.shape,
              "rel_bias": rel_bias.shape, "norm2_g": norm2_g.shape, "ffn_conv_b": ffn_conv_b.shape,
              "ffn_conv_w": ffn_conv_w.shape}
    small_w = {"norm1_g": norm1_g, "q_norm_g": q_norm_g, "k_norm_g": k_norm_g, "rel_bias": rel_bias,
               "norm2_g": norm2_g, "ffn_conv_b": ffn_conv_b, "ffn_conv_w": ffn_conv_w}
    small_m = {"norm1_g": m_norm1_g, "q_norm_g": m_q_norm_g, "k_norm_g": m_k_norm_g, "rel_bias": m_rel_bias,
               "norm2_g": m_norm2_g, "ffn_conv_b": m_ffn_conv_b, "ffn_conv_w": m_ffn_conv_w}
    small_v = {"norm1_g": v_norm1_g, "q_norm_g": v_q_norm_g, "k_norm_g": v_k_norm_g, "rel_bias": v_rel_bias,
               "norm2_g": v_norm2_g, "ffn_conv_b": v_ffn_conv_b, "ffn_conv_w": v_ffn_conv_w}
    ds, ms, vs = _adamw(*(_pack_small(t, SMALL_OWN, SMALL_OWN_ROWS) for t in (small_w, small_sum, small_m, small_v)),
                        "adamw_small")
    small_grads = small_sum
    ds, ms, vs = (_unpack_small(t, SMALL_OWN) for t in (ds, ms, vs))

    order = ("norm1_g", "w_in", "q_norm_g", "k_norm_g", "rel_bias", "w_branch_a", "w_branch_b", "w_out",
             "norm2_g", "w_ffn_up", "ffn_conv_w", "ffn_conv_b", "w_ffn_down")
    outs = [loss, dx[None]]
    for big, small in ((grads, small_grads), (deltas, ds), (new_m, ms), (new_v, vs)):
        for n in order:
            outs.append(big[n][None] if n in big else small[n].reshape(shapes[n]))
    return tuple(outs)
```

```python
import functools

import jax
import jax.numpy as jnp
from jax import lax
from jax.experimental import pallas as pl
from jax.experimental.pallas import tpu as pltpu

F32 = jnp.float32
BF16 = jnp.bfloat16
MESH = pl.DeviceIdType.MESH

D_MODEL = 1024
HEAD_DIM = 64
N_HEADS = 8
WIDTH = N_HEADS * HEAD_DIM
CHUNK = 64
LEFT_CHUNKS = 8
MAX_REL = 128
N_REL = 2 * MAX_REL + 1
D_FF = 2816
EPS = 1e-6
NEG = -1e30

ADAM_LR = 0.001
ADAM_B1 = 0.9
ADAM_B2 = 0.999
ADAM_EPS = 1e-08
ADAM_WD = 0.01
ADAM_STEP = 10

N_CHIPS = 4
N_DEV = 8
LANES = 128
PAIR = 2 * HEAD_DIM
BQ = 256
BAND = LEFT_CHUNKS * CHUNK
KWIN = BAND + BQ
VMEM_LIMIT = 56 * 1024 * 1024
PACK_COLS = 1024

NN = (((1,), (0,)), ((), ()))
NT = (((1,), (1,)), ((), ()))
TN = (((0,), (0,)), ((), ()))


def _cparams(sem=None):
    if sem is None:
        return pltpu.CompilerParams(vmem_limit_bytes=VMEM_LIMIT)
    return pltpu.CompilerParams(dimension_semantics=sem, vmem_limit_bytes=VMEM_LIMIT)


def _pick(n, cands):
    for c in cands:
        if n % c == 0:
            return c
    raise ValueError(f"no block for {n}")


def _dot(a, b, dn):
    return lax.dot_general(a, b, dn, preferred_element_type=F32)


def _sigmoid(x):
    return 0.5 * jnp.tanh(0.5 * x) + 0.5


def _split_bf16(x):
    hi = x.astype(BF16)
    lo = (x - hi.astype(F32)).astype(BF16)
    return hi, lo


MM_RESIDENT_BYTES = 12 * 1024 * 1024
MM_TILE_BYTES = 4 * 1024 * 1024


def _matmul(a, b, out_dtype, name, residual=None, slabs=None):
    (M, K), N = a.shape, b.shape[1]
    out_bytes = jnp.dtype(out_dtype).itemsize
    if slabs is None and N <= D_FF and K * N * 2 <= MM_RESIDENT_BYTES:
        bk, bn = K, N
        bm = next(c for c in (1024, 512, 256, 128)
                  if M % c == 0 and c * K * 2 <= MM_TILE_BYTES and c * N * out_bytes <= MM_TILE_BYTES)
        b_spec = pl.BlockSpec((bk, bn), lambda i, j, k: (0, 0), pipeline_mode=pl.Buffered(1))
    elif slabs is None and K <= D_FF:
        bk, bm, bn = K, _pick(M, (1024, 512)), _pick(N, (D_FF // 2, 512, 256, 128))
        b_spec = pl.BlockSpec((bk, bn), lambda i, j, k: (k, j))
    else:
        bk = _pick(K, (2048, 1024, 512))
        bm = _pick(M, (D_FF // 2, 1024, 512, 256, 128))
        bn = N // slabs if slabs else _pick(N, (D_FF // 2, 1024, 512, 256, 128))
        b_spec = pl.BlockSpec((bk, bn), lambda i, j, k: (k, j))
    nk = K // bk
    dn = NN
    a_spec = pl.BlockSpec((bm, bk), lambda i, j, k: (i, k))
    if slabs:
        o_spec = pl.BlockSpec((None, bm, bn), lambda i, j, k: (j, i, 0))
        out_shape = jax.ShapeDtypeStruct((slabs, M, bn), out_dtype)
    else:
        o_spec = pl.BlockSpec((bm, bn), lambda i, j, k: (i, j))
        out_shape = jax.ShapeDtypeStruct((M, N), out_dtype)
    has_res = residual is not None

    def body(*refs):
        if has_res:
            a_ref, b_ref, r_ref, o_ref, acc_ref = refs
        else:
            a_ref, b_ref, o_ref, acc_ref = refs
        k = pl.program_id(2)
        part = _dot(a_ref[...], b_ref[...], dn)

        def finish(total):
            if has_res:
                total = total + r_ref[...]
            o_ref[...] = total.astype(out_dtype)

        if nk == 1:
            finish(part)
        else:
            @pl.when(k == 0)
            def _():
                acc_ref[...] = part

            @pl.when(k > 0)
            def _():
                acc_ref[...] += part

            @pl.when(k == nk - 1)
            def _():
                finish(acc_ref[...])

    in_specs = [a_spec, b_spec] + ([o_spec] if has_res else [])
    args = (a, b) + ((residual,) if has_res else ())
    return pl.pallas_call(
        body, name=name, grid=(M // bm, N // bn, nk),
        in_specs=in_specs, out_specs=o_spec, out_shape=out_shape,
        scratch_shapes=[pltpu.VMEM((bm, bn) if nk > 1 else (8, LANES), F32)],
        compiler_params=_cparams(("parallel", "parallel", "arbitrary")),
    )(*args)


ROWS = 512


def _row_spec(cols, bm=ROWS):
    return pl.BlockSpec((bm, cols), lambda i: (i, 0))


def _col_spec(rows, bn=ROWS):
    return pl.BlockSpec((rows, bn), lambda i: (0, i))


def _full_spec(shape):
    return pl.BlockSpec(shape, lambda i: (0,) * len(shape))


def _colsum8(t):
    return jnp.sum(t.reshape(t.shape[0] // 8, 8, t.shape[1]), axis=0)


def _rms_fwd(x, g, name, gather=None):
    S, D = x.shape
    nt = S // ROWS
    n_side = len(gather.arrays) if gather else 0

    def body(*refs):
        x_ref, g_ref = refs[:2]
        o_ref, ot_ref = refs[2 + n_side:4 + n_side]
        i = pl.program_id(0)
        if gather:
            start, forward, finish = gather.steps(refs[2:2 + n_side], refs[4 + n_side:4 + 2 * n_side], refs[4 + 2 * n_side:])
            pl.when(i == 0)(start)
            pl.when(i == 3 * nt // 4)(forward)
        xv = x_ref[...]
        r = lax.rsqrt(jnp.mean(xv * xv, axis=-1, keepdims=True) + EPS)
        y = xv * r * g_ref[...]
        o_ref[...] = y.astype(BF16)
        ot_ref[...] = y.T.astype(BF16)
        if gather:
            pl.when(i == nt - 1)(finish)

    side = gather.arrays if gather else []
    return pl.pallas_call(
        body, name=name, grid=(nt,),
        in_specs=[_row_spec(D), _full_spec((1, D))] + [ANY] * n_side,
        out_specs=[_row_spec(D), _col_spec(D)] + [ANY] * n_side,
        out_shape=[jax.ShapeDtypeStruct((S, D), BF16), jax.ShapeDtypeStruct((D, S), BF16)]
        + (gather.out_shape if gather else []),
        scratch_shapes=gather.scratch if gather else [],
        compiler_params=_cparams(("arbitrary",) if gather else ("parallel",)),
    )(x, g, *side)


RMS_BWD_ROWS = 256


def _rms_bwd(pairs, x, g, dres, name, exchange=None):
    S, D = x.shape
    bm = RMS_BWD_ROWS
    nt = S // bm
    n_pairs = len(pairs)
    n_side = len(exchange.arrays) if exchange else 0
    n_in = 2 * n_pairs + 3

    def body(*refs):
        x_ref, g_ref, dres_ref = refs[2 * n_pairs:n_in]
        dx_ref, dxb_ref, dg_ref = refs[n_in + n_side:n_in + n_side + 3]
        acc_ref = refs[n_in + 2 * n_side + 3]
        i = pl.program_id(0)
        if exchange:
            start, finish = exchange.steps(refs[n_in:n_in + n_side], refs[n_in + n_side + 3:n_in + 2 * n_side + 3],
                                           refs[n_in + 2 * n_side + 4:])
            pl.when(i == 0)(start)
        dyv = sum(_dot(refs[2 * p][...], refs[2 * p + 1][...], NN) for p in range(n_pairs))
        xv = x_ref[...]
        r = lax.rsqrt(jnp.mean(xv * xv, axis=-1, keepdims=True) + EPS)
        xr = xv * r
        u = dyv * g_ref[...]
        dx = r * u - xr * (r * r) * jnp.mean(xv * u, axis=-1, keepdims=True) + dres_ref[...]
        dx_ref[...] = dx
        dxb_ref[...] = dx.astype(BF16)
        part = _colsum8(dyv * xr)

        @pl.when(i == 0)
        def _():
            acc_ref[...] = part

        @pl.when(i > 0)
        def _():
            acc_ref[...] += part

        @pl.when(i == nt - 1)
        def _():
            dg_ref[...] = jnp.sum(acc_ref[...], axis=0, keepdims=True)

        if exchange:
            pl.when(i == nt - 1)(finish)

    rows = lambda cols: pl.BlockSpec((bm, cols), lambda i: (i, 0))
    in_specs = []
    for a, b in pairs:
        in_specs += [rows(a.shape[1]), pl.BlockSpec(b.shape, lambda i: (0, 0), pipeline_mode=pl.Buffered(1))]
    side = exchange.arrays if exchange else []
    return pl.pallas_call(
        body, name=name, grid=(nt,),
        in_specs=in_specs + [rows(D), _full_spec((1, D)), rows(D)] + [ANY] * n_side,
        out_specs=[rows(D), rows(D), _full_spec((1, D))] + [ANY] * n_side,
        out_shape=[jax.ShapeDtypeStruct((S, D), F32), jax.ShapeDtypeStruct((S, D), BF16),
                   jax.ShapeDtypeStruct((1, D), F32)] + (exchange.out_shape if exchange else []),
        scratch_shapes=[pltpu.VMEM((8, D), F32)] + (exchange.scratch if exchange else []),
        compiler_params=_cparams(("arbitrary",)),
    )(*[t for pair in pairs for t in pair], x, g, dres, *side)


def _head_mean(t, blockdiag):
    hi, lo = _split_bf16(t)
    return (_dot(hi, blockdiag, NN) + _dot(lo, blockdiag, NN)) * (1.0 / HEAD_DIM)


def _blockdiag():
    r = lax.broadcasted_iota(jnp.int32, (WIDTH, WIDTH), 0) // HEAD_DIM
    c = lax.broadcasted_iota(jnp.int32, (WIDTH, WIDTH), 1) // HEAD_DIM
    return jnp.where(r == c, 1.0, 0.0).astype(BF16)


def _qknorm_fwd(qk, gq, gk, name):
    S = qk.shape[0]

    def body(qk_ref, gq_ref, gk_ref, q_ref, k_ref):
        bd = _blockdiag()
        for part, g_ref, o_ref, scale in ((0, gq_ref, q_ref, HEAD_DIM ** -0.5), (1, gk_ref, k_ref, 1.0)):
            t = qk_ref[:, part * WIDTH:(part + 1) * WIDTH]
            r = lax.rsqrt(_head_mean(t * t, bd) + EPS)
            o_ref[...] = (t * r * g_ref[...] * scale).astype(BF16)

    return pl.pallas_call(
        body, name=name, grid=(S // ROWS,),
        in_specs=[_row_spec(2 * WIDTH), _full_spec((1, WIDTH)), _full_spec((1, WIDTH))],
        out_specs=[_row_spec(WIDTH), _row_spec(WIDTH)],
        out_shape=[jax.ShapeDtypeStruct((S, WIDTH), BF16)] * 2,
        compiler_params=_cparams(("parallel",)),
    )(qk, gq, gk)


def _qknorm_bwd(qk, gq, gk, dqn, dkn, name):
    S = qk.shape[0]
    nt = S // ROWS

    def body(qk_ref, gq_ref, gk_ref, dqn_ref, dkn_ref, dq_ref, dk_ref, dgq_ref, dgk_ref, accq_ref, acck_ref):
        i = pl.program_id(0)
        bd = _blockdiag()
        for part, g_ref, dn_ref, o_ref, dg_ref, acc_ref, scale in (
                (0, gq_ref, dqn_ref, dq_ref, dgq_ref, accq_ref, HEAD_DIM ** -0.5),
                (1, gk_ref, dkn_ref, dk_ref, dgk_ref, acck_ref, 1.0)):
            t = qk_ref[:, part * WIDTH:(part + 1) * WIDTH]
            dn = dn_ref[...] * scale
            r = lax.rsqrt(_head_mean(t * t, bd) + EPS)
            u = dn * g_ref[...]
            dt = r * u - t * (r * r * r) * _head_mean(t * u, bd)
            o_ref[...] = dt.astype(BF16)
            psum = _colsum8(dn * t * r)

            @pl.when(i == 0)
            def _():
                acc_ref[...] = psum

            @pl.when(i > 0)
            def _():
                acc_ref[...] += psum

            @pl.when(i == nt - 1)
            def _():
                dg_ref[...] = jnp.sum(acc_ref[...], axis=0, keepdims=True)

    return pl.pallas_call(
        body, name=name, grid=(nt,),
        in_specs=[_row_spec(2 * WIDTH), _full_spec((1, WIDTH)), _full_spec((1, WIDTH)),
                  _row_spec(WIDTH), _row_spec(WIDTH)],
        out_specs=[_row_spec(WIDTH), _row_spec(WIDTH), _full_spec((1, WIDTH)), _full_spec((1, WIDTH))],
        out_shape=[jax.ShapeDtypeStruct((S, WIDTH), BF16)] * 2 + [jax.ShapeDtypeStruct((1, WIDTH), F32)] * 2,
        scratch_shapes=[pltpu.VMEM((8, WIDTH), F32)] * 2,
        compiler_params=_cparams(("arbitrary",)),
    )(qk, gq, gk, dqn, dkn)


def _gate_specs(D):
    return [pl.BlockSpec((ROWS, D), lambda i: (i, 0)), pl.BlockSpec((ROWS, D), lambda i: (i, 1))]


def _mix_fwd(gates, ya, yb, name):
    S, D = ya.shape

    def body(ga_ref, gb_ref, ya_ref, yb_ref, o_ref, ot_ref):
        ga, gb, yav, ybv = (r[...].astype(F32) for r in (ga_ref, gb_ref, ya_ref, yb_ref))
        m = _sigmoid(ga) * yav + _sigmoid(gb) * ybv
        o_ref[...] = m.astype(BF16)
        ot_ref[...] = m.T.astype(BF16)

    return pl.pallas_call(
        body, name=name, grid=(S // ROWS,),
        in_specs=_gate_specs(D) + [_row_spec(D)] * 2, out_specs=[_row_spec(D), _col_spec(D)],
        out_shape=[jax.ShapeDtypeStruct((S, D), BF16), jax.ShapeDtypeStruct((D, S), BF16)],
        compiler_params=_cparams(("parallel",)),
    )(gates, gates, ya, yb)


def _mix_bwd(dm, gates, ya, yb, name):
    S, D = ya.shape

    def body(dm_ref, ga_ref, gb_ref, ya_ref, yb_ref, dg_ref, dya_ref, dyb_ref):
        dmv = dm_ref[...].astype(F32)
        for half, (g_ref, y_ref, dy_ref) in enumerate(((ga_ref, ya_ref, dya_ref), (gb_ref, yb_ref, dyb_ref))):
            s = _sigmoid(g_ref[...].astype(F32))
            dy_ref[...] = (dmv * s).astype(BF16)
            dg_ref[:, half * D:(half + 1) * D] = (dmv * y_ref[...].astype(F32) * s * (1.0 - s)).astype(BF16)

    return pl.pallas_call(
        body, name=name, grid=(S // ROWS,),
        in_specs=[_row_spec(D)] + _gate_specs(D) + [_row_spec(D)] * 2,
        out_specs=[_row_spec(2 * D), _row_spec(D), _row_spec(D)],
        out_shape=[jax.ShapeDtypeStruct((S, 2 * D), BF16)] + [jax.ShapeDtypeStruct((S, D), BF16)] * 2,
        compiler_params=_cparams(("parallel",)),
    )(dm, gates, gates, ya, yb)


def _down_and_loss(act, w, x2, target, name):
    (S, K), D = act.shape, w.shape[1]
    nt = S // ROWS

    def body(a_ref, w_ref, x_ref, t_ref, dy_ref, dyb_ref, p_ref):
        err = _dot(a_ref[...], w_ref[...], NN) + x_ref[...] - t_ref[...]
        dy = err * (1.0 / D)
        dy_ref[...] = dy
        dyb_ref[...] = dy.astype(BF16)
        sq = _colsum8(err * err)
        acc = sq[:, 0:LANES]
        for k in range(1, D // LANES):
            acc = acc + sq[:, k * LANES:(k + 1) * LANES]
        p_ref[...] = acc

    return pl.pallas_call(
        body, name=name, grid=(nt,),
        in_specs=[_row_spec(K), pl.BlockSpec((K, D), lambda i: (0, 0), pipeline_mode=pl.Buffered(1)),
                  _row_spec(D), _row_spec(D)],
        out_specs=[_row_spec(D), _row_spec(D), pl.BlockSpec((8, LANES), lambda i: (i, 0))],
        out_shape=[jax.ShapeDtypeStruct((S, D), F32), jax.ShapeDtypeStruct((S, D), BF16),
                   jax.ShapeDtypeStruct((nt * 8, LANES), F32)],
        compiler_params=_cparams(("parallel",)),
    )(act, w, x2, target)


CONV_COLS = D_FF // 2
HALO = 16
CONV_CHUNK = 64


def _aligned(start, multiple):
    return start if isinstance(start, int) else pl.multiple_of(start, multiple)


def _conv_taps(xe, cw, cb):
    taps = (pltpu.roll(xe, 2, 0), pltpu.roll(xe, 1, 0), xe)
    return taps, cw[0:1] * taps[0] + cw[1:2] * taps[1] + cw[2:3] * taps[2] + cb


def _conv_specs(nt):
    hb, nb = ROWS // HALO, D_FF // CONV_COLS
    specs = {}
    for part, off in (("gate", 0), ("up", nb)):
        specs[part] = dict(
            main=pl.BlockSpec((ROWS, CONV_COLS), functools.partial(lambda c, i, off: (i, c + off), off=off)),
            prev=pl.BlockSpec((HALO, CONV_COLS),
                              functools.partial(lambda c, i, off: (jnp.maximum(i * hb - 1, 0), c + off), off=off)),
            nxt=pl.BlockSpec((HALO, CONV_COLS),
                             functools.partial(lambda c, i, off: (jnp.minimum((i + 1) * hb, nt * hb - 1), c + off), off=off)),
            w=pl.BlockSpec((3, CONV_COLS), functools.partial(lambda c, i, off: (0, c + off), off=off)),
            b=pl.BlockSpec((1, CONV_COLS), functools.partial(lambda c, i, off: (0, c + off), off=off)))
    return specs


def _convglu_fwd(hid, cw, cb, name):
    S = hid.shape[0]
    sp = _conv_specs(S // ROWS)

    def body(hg_ref, hgp_ref, hu_ref, hup_ref, cwg_ref, cwu_ref, cbg_ref, cbu_ref, o_ref, ot_ref):
        i = pl.program_id(1)
        keep = (i > 0).astype(F32)

        def conv(h_ref, hp_ref, cw_ref, cb_ref):
            xe = jnp.concatenate([hp_ref[...].astype(F32) * keep, h_ref[...].astype(F32)], axis=0)
            return _conv_taps(xe, cw_ref[...], cb_ref[...])[1][HALO:, :]

        gate = conv(hg_ref, hgp_ref, cwg_ref, cbg_ref)
        up = conv(hu_ref, hup_ref, cwu_ref, cbu_ref)
        act = gate * _sigmoid(gate) * up
        o_ref[...] = act.astype(BF16)
        ot_ref[...] = act.T.astype(BF16)

    g, u = sp["gate"], sp["up"]
    return pl.pallas_call(
        body, name=name, grid=(D_FF // CONV_COLS, S // ROWS),
        in_specs=[g["main"], g["prev"], u["main"], u["prev"], g["w"], u["w"], g["b"], u["b"]],
        out_specs=[g["main"], pl.BlockSpec((CONV_COLS, ROWS), lambda c, i: (c, i))],
        out_shape=[jax.ShapeDtypeStruct((S, D_FF), BF16), jax.ShapeDtypeStruct((D_FF, S), BF16)],
        compiler_params=_cparams(("parallel", "parallel")),
    )(hid, hid, hid, hid, cw, cw, cb, cb)


def _convglu_bwd(hid, dact, cw, cb, name):
    S = hid.shape[0]
    nt = S // ROWS
    sp = _conv_specs(nt)

    n_chunks = ROWS // CONV_CHUNK

    def body(hg_ref, hgp_ref, hgn_ref, hu_ref, hup_ref, hun_ref, da_ref, dan_ref,
             cwg_ref, cwu_ref, cbg_ref, cbu_ref,
             dhg_ref, dhu_ref, dcwg_ref, dcwu_ref, dcbg_ref, dcbu_ref, xg_s, xu_s, da_s):
        i = pl.program_id(1)
        kp = (i > 0).astype(F32)
        kn = (i < nt - 1).astype(F32)
        for x_s, h_ref, hp_ref, hn_ref in ((xg_s, hg_ref, hgp_ref, hgn_ref), (xu_s, hu_ref, hup_ref, hun_ref)):
            x_s[0:HALO, :] = hp_ref[...].astype(F32) * kp
            x_s[HALO:HALO + ROWS, :] = h_ref[...].astype(F32)
            x_s[HALO + ROWS:, :] = hn_ref[...].astype(F32) * kn
        da_s[0:ROWS, :] = da_ref[...].astype(F32)
        da_s[ROWS:, :] = dan_ref[...].astype(F32) * kn

        @pl.when(i == 0)
        def _():
            for ref in (dcwg_ref, dcwu_ref, dcbg_ref, dcbu_ref):
                ref[...] = jnp.zeros_like(ref)

        def lane_group(grp, _):
            lanes = pl.ds(pl.multiple_of(grp * LANES, LANES), LANES)
            cwg, cwu, cbg, cbu = cwg_ref[:, lanes], cwu_ref[:, lanes], cbg_ref[:, lanes], cbu_ref[:, lanes]

            def grads(r0, n):
                rows = pl.ds(_aligned(r0 + HALO - 8, 8), n + 8)
                taps_g, gate = _conv_taps(xg_s[rows, lanes], cwg, cbg)
                taps_u, up = _conv_taps(xu_s[rows, lanes], cwu, cbu)
                gate, up = gate[8:], up[8:]
                da = da_s[pl.ds(_aligned(r0, 8), n), lanes]
                sg = _sigmoid(gate)
                return (da * up * sg * (1.0 + gate * (1.0 - sg)), da * gate * sg,
                        [t[8:] for t in taps_g], [t[8:] for t in taps_u])

            def chunk(step, carry):
                below_g, below_u, accs = carry
                r0 = (n_chunks - 1 - step) * CONV_CHUNK
                dg, du, taps_g, taps_u = grads(r0, CONV_CHUNK)
                new_accs = []
                for d, below, cwv, taps, dh_ref, acc in ((dg, below_g, cwg, taps_g, dhg_ref, accs[0]),
                                                        (du, below_u, cwu, taps_u, dhu_ref, accs[1])):
                    ext = jnp.concatenate([d, below], axis=0)
                    n_ext = CONV_CHUNK + 8
                    dh = (cwv[2:3] * d + cwv[1:2] * pltpu.roll(ext, n_ext - 1, 0)[:CONV_CHUNK]
                          + cwv[0:1] * pltpu.roll(ext, n_ext - 2, 0)[:CONV_CHUNK])
                    dh_ref[pl.ds(_aligned(r0, CONV_CHUNK), CONV_CHUNK), lanes] = dh.astype(BF16)
                    new_accs.append(tuple(a + _colsum8(d * tap) for a, tap in zip(acc[:3], taps))
                                    + (acc[3] + _colsum8(d),))
                return dg[0:8], du[0:8], tuple(new_accs)

            below_g, below_u, _, _ = grads(ROWS, 8)
            zero = jnp.zeros((8, LANES), F32)
            _, _, accs = lax.fori_loop(0, n_chunks, chunk, (below_g, below_u, ((zero,) * 4, (zero,) * 4)))
            for acc, dcw_ref, dcb_ref in ((accs[0], dcwg_ref, dcbg_ref), (accs[1], dcwu_ref, dcbu_ref)):
                for t in range(3):
                    dcw_ref[t:t + 1, lanes] += jnp.sum(acc[t], axis=0, keepdims=True)
                dcb_ref[:, lanes] += jnp.sum(acc[3], axis=0, keepdims=True)
            return 0

        lax.fori_loop(0, CONV_COLS // LANES, lane_group, 0)

    g, u = sp["gate"], sp["up"]
    return pl.pallas_call(
        body, name=name, grid=(D_FF // CONV_COLS, nt),
        in_specs=[g["main"], g["prev"], g["nxt"], u["main"], u["prev"], u["nxt"], g["main"], g["nxt"],
                  g["w"], u["w"], g["b"], u["b"]],
        out_specs=[g["main"], g["main"], g["w"], g["w"], g["b"], g["b"]],
        out_shape=[jax.ShapeDtypeStruct((S, D_FF), BF16)] * 2 + [jax.ShapeDtypeStruct((3, D_FF), F32)] * 2
        + [jax.ShapeDtypeStruct((1, D_FF), F32)] * 2,
        scratch_shapes=[pltpu.VMEM((ROWS + 2 * HALO, CONV_COLS), F32)] * 2 + [pltpu.VMEM((ROWS + HALO, CONV_COLS), F32)],
        compiler_params=_cparams(("parallel", "arbitrary")),
    )(hid, hid, hid, hid, hid, hid, dact, dact, cw, cw, cb, cb)


REL_PAD = 384
DIAG = 1024


def _band_valid():
    qc = lax.broadcasted_iota(jnp.int32, (BQ, KWIN), 0) // CHUNK
    kc = lax.broadcasted_iota(jnp.int32, (BQ, KWIN), 1) // CHUNK - LEFT_CHUNKS
    return (kc <= qc) & (kc >= qc - LEFT_CHUNKS)


def _rel_index(offset):
    return jnp.clip(BAND - offset, -MAX_REL, MAX_REL) + MAX_REL


def _split3(x):
    hi = x.astype(BF16)
    rest = x - hi.astype(F32)
    mid = rest.astype(BF16)
    return hi, mid, (rest - mid.astype(F32)).astype(BF16)


def _bias_table(rel_bias, name):
    def body(rb_ref, o_ref):
        t = lax.broadcasted_iota(jnp.int32, (REL_PAD, DIAG), 0)
        lane = lax.broadcasted_iota(jnp.int32, (REL_PAD, DIAG), 1)
        pick = jnp.where(t == _rel_index(lane - BQ), 1.0, 0.0).astype(BF16)
        base = sum(_dot(piece, pick, NN) for piece in _split3(rb_ref[...]))
        valid = _band_valid()
        for h in range(N_HEADS):
            rows = jnp.broadcast_to(base[h:h + 1], (BQ, DIAG))
            rolled = pltpu.roll(rows, 0, 1, stride=1, stride_axis=0)
            o_ref[h] = jnp.where(valid, rolled[:, BQ:], NEG)

    return pl.pallas_call(
        body, name=name,
        out_shape=jax.ShapeDtypeStruct((N_HEADS, BQ, KWIN), F32),
        compiler_params=_cparams(),
    )(rel_bias)


def _bias_table_bwd(dtab, name):
    def body(d_ref, o_ref, diag_ref):
        r = lax.broadcasted_iota(jnp.int32, (BQ, BQ), 0)
        c = lax.broadcasted_iota(jnp.int32, (BQ, BQ), 1)
        flip = jnp.where(r + c == BQ - 1, 1.0, 0.0).astype(BF16)
        for h in range(N_HEADS):
            flipped = sum(_dot(flip, piece, NN) for piece in _split3(d_ref[h]))
            padded = jnp.concatenate([flipped, jnp.zeros((BQ, DIAG - KWIN), F32)], axis=1)
            rolled = pltpu.roll(padded, DIAG - (BQ - 1), 1, stride=1, stride_axis=0)
            diag_ref[h:h + 1, :] = jnp.sum(rolled, axis=0, keepdims=True)
        lane = lax.broadcasted_iota(jnp.int32, (DIAG, REL_PAD), 0)
        t = lax.broadcasted_iota(jnp.int32, (DIAG, REL_PAD), 1)
        offset = jnp.where(lane < KWIN, lane, lane - DIAG)
        pick = jnp.where(t == _rel_index(offset), 1.0, 0.0).astype(BF16)
        o_ref[...] = sum(_dot(piece, pick, NN) for piece in _split3(diag_ref[...]))

    return pl.pallas_call(
        body, name=name,
        out_shape=jax.ShapeDtypeStruct((N_HEADS, REL_PAD), F32),
        scratch_shapes=[pltpu.VMEM((N_HEADS, DIAG), F32)],
        compiler_params=_cparams(),
    )(dtab)


def _head_masks(heads=2):
    lane = lax.broadcasted_iota(jnp.int32, (1, heads * HEAD_DIM), 1)
    return [lane // HEAD_DIM == h for h in range(heads)]


def _own_lanes(masks, vals):
    out = vals[-1]
    for m, val in zip(masks[-2::-1], vals[-2::-1]):
        out = jnp.where(m, val, out)
    return out


CA_HEADS = 4
CA_LANES = CA_HEADS * HEAD_DIM


def _ca_window_specs(nq, col_off=0):
    return [pl.BlockSpec((BQ, CA_LANES), functools.partial(
        lambda p, i, d: (jnp.clip(i - 2 + d, 0, nq - 1), p + col_off), d=d)) for d in range(3)]


def _softmax_rows(s):
    p = jnp.exp(s - jnp.max(s, axis=-1, keepdims=True))
    return p, jnp.sum(p, axis=-1, keepdims=True)


def _ca_scores(qm, kc, tab_h, i):
    col = lax.broadcasted_iota(jnp.int32, (1, KWIN), 1)
    in_seq = col + (i - 2) * BQ >= 0
    return jnp.where(in_seq, _dot(qm, kc, NT) + tab_h, NEG)


def _ca_fwd(qn, kn, v, tab, name, v_off=0, gather=None):
    S = qn.shape[0]
    nq = S // BQ
    groups = WIDTH // CA_LANES
    qspec = pl.BlockSpec((BQ, CA_LANES), lambda p, i: (i, p))
    tspec = pl.BlockSpec((CA_HEADS, BQ, KWIN), lambda p, i: (p, 0, 0))
    n_side = len(gather.arrays) if gather else 0

    def body(*refs):
        q_ref, k0, k1, k2, v0, v1, v2, tab_ref = refs[:8]
        o_ref, ot_ref = refs[8 + n_side:10 + n_side]
        p, i = pl.program_id(0), pl.program_id(1)
        if gather:
            start, forward, finish = gather.steps(refs[8:8 + n_side], refs[10 + n_side:10 + 2 * n_side],
                                                  refs[10 + 2 * n_side:])
            pl.when((p == 0) & (i == 0))(start)
            pl.when((p == groups - 1) & (i == nq // 2))(forward)
        kc = jnp.concatenate([k0[...], k1[...], k2[...]], axis=0)
        vc = jnp.concatenate([v0[...], v1[...], v2[...]], axis=0)
        qv = q_ref[...]
        masks = _head_masks(CA_HEADS)
        heads = range(CA_HEADS)
        s = [_ca_scores(jnp.where(masks[h], qv, 0), kc, tab_ref[h], i) for h in heads]
        soft = [_softmax_rows(s[h]) for h in heads]
        o = [_dot(soft[h][0].astype(BF16), vc, NN) / soft[h][1] for h in heads]
        out = _own_lanes(masks, o)
        o_ref[...] = out.astype(BF16)
        ot_ref[...] = out.T.astype(BF16)
        if gather:
            pl.when((p == groups - 1) & (i == nq - 1))(finish)

    side = gather.arrays if gather else []
    return pl.pallas_call(
        body, name=name, grid=(groups, nq),
        in_specs=[qspec] + _ca_window_specs(nq) + _ca_window_specs(nq, v_off) + [tspec] + [ANY] * n_side,
        out_specs=[qspec, pl.BlockSpec((CA_LANES, BQ), lambda p, i: (p, i))] + [ANY] * n_side,
        out_shape=[jax.ShapeDtypeStruct((S, WIDTH), BF16), jax.ShapeDtypeStruct((WIDTH, S), BF16)]
        + (gather.out_shape if gather else []),
        scratch_shapes=gather.scratch if gather else [],
        compiler_params=_cparams(("arbitrary", "arbitrary") if gather else ("parallel", "parallel")),
    )(qn, kn, kn, kn, v, v, v, tab, *side)


def _ca_bwd(qn, kn, v, do, tab, name, v_off=0):
    S = qn.shape[0]
    nq = S // BQ
    qspec = pl.BlockSpec((BQ, CA_LANES), lambda p, i: (jnp.minimum(i, nq - 1), p))
    kout = pl.BlockSpec((BQ, CA_LANES), lambda p, i: (jnp.clip(i - 2, 0, nq - 1), p))
    tspec = pl.BlockSpec((CA_HEADS, BQ, KWIN), lambda p, i: (p, 0, 0))

    def body(q_ref, do_ref, k0, k1, k2, v0, v1, v2, tab_ref,
             dq_ref, dk_ref, dv_ref, dtab_ref, dk_acc, dv_acc):
        i = pl.program_id(1)

        @pl.when(i == 0)
        def _():
            dk_acc[...] = jnp.zeros_like(dk_acc)
            dv_acc[...] = jnp.zeros_like(dv_acc)
            dtab_ref[...] = jnp.zeros_like(dtab_ref)

        @pl.when(i < nq)
        def _():
            kc = jnp.concatenate([k0[...], k1[...], k2[...]], axis=0)
            vc = jnp.concatenate([v0[...], v1[...], v2[...]], axis=0)
            qv, dov = q_ref[...], do_ref[...]
            masks = _head_masks(CA_HEADS)
            heads = range(CA_HEADS)
            qm = [jnp.where(masks[h], qv, 0) for h in heads]
            dom = [jnp.where(masks[h], dov, 0) for h in heads]
            s = [_ca_scores(qm[h], kc, tab_ref[h], i) for h in heads]
            dp = [_dot(dom[h], vc, NT) for h in heads]
            soft = [_softmax_rows(s[h]) for h in heads]
            p = [soft[h][0] / soft[h][1] for h in heads]
            ds = [p[h] * (dp[h] - jnp.sum(p[h] * dp[h], axis=-1, keepdims=True)) for h in heads]
            for h in heads:
                dtab_ref[h] += ds[h]
            dsb = [ds[h].astype(BF16) for h in heads]
            pb = [p[h].astype(BF16) for h in heads]
            dq = [_dot(dsb[h], kc, NN) for h in heads]
            dq_ref[...] = _own_lanes(masks, dq)
            dkc = sum(_dot(dsb[h], qm[h], TN) for h in heads)
            dvc = sum(_dot(pb[h], dom[h], TN) for h in heads)
            for d in range(3):
                slot = (i + 1 + d) % 3
                dk_acc[slot] += dkc[d * BQ:(d + 1) * BQ]
                dv_acc[slot] += dvc[d * BQ:(d + 1) * BQ]

        @pl.when(i >= 2)
        def _():
            slot = (i + 1) % 3
            dk_ref[...] = dk_acc[slot]
            dv_ref[...] = dv_acc[slot].astype(BF16)
            dk_acc[slot] = jnp.zeros((BQ, CA_LANES), F32)
            dv_acc[slot] = jnp.zeros((BQ, CA_LANES), F32)

    return pl.pallas_call(
        body, name=name, grid=(WIDTH // CA_LANES, nq + 2),
        in_specs=[qspec, qspec] + _ca_window_specs(nq) + _ca_window_specs(nq, v_off) + [tspec],
        out_specs=[qspec, kout, kout, tspec],
        out_shape=[jax.ShapeDtypeStruct((S, WIDTH), F32), jax.ShapeDtypeStruct((S, WIDTH), F32),
                   jax.ShapeDtypeStruct((S, WIDTH), BF16), jax.ShapeDtypeStruct((N_HEADS, BQ, KWIN), F32)],
        scratch_shapes=[pltpu.VMEM((3, BQ, CA_LANES), F32)] * 2,
        compiler_params=_cparams(("parallel", "arbitrary")),
    )(qn, do, kn, kn, kn, v, v, v, tab)


def _sb_consts():
    r = lax.broadcasted_iota(jnp.int32, (BQ, BQ), 0)
    c = lax.broadcasted_iota(jnp.int32, (BQ, BQ), 1)
    from_s = jnp.where(r >= c, 1.0, 0.0).astype(BF16)
    causal = c < r
    return from_s, causal


def _suffix_sum(t, from_s):
    hi, lo = _split_bf16(t)
    return _dot(hi, from_s, NN) + _dot(lo, from_s, NN)


def _neg_abs(x):
    bits = lax.bitcast_convert_type(x, jnp.uint32) | jnp.uint32(0x80000000)
    return lax.bitcast_convert_type(bits, F32)


def _sb_log_keep(zn):
    return jnp.minimum(zn, 0.0) - jnp.log(1.0 + jnp.exp(_neg_abs(zn)))


SB_DEAD = 105.0


SB_QB = 2


def _sb_walk(ip, tiles, keep_ref):
    i0 = SB_QB * ip

    @pl.when(ip == 0)
    def _():
        tiles([(0, [0], [True]), (1, [1, 0], [True, False])])

    @pl.when(ip > 0)
    def _():
        tiles([(a, [i0 + a, i0 + a - 1], [True, False]) for a in range(SB_QB)])

    for a in range(SB_QB):
        def alive(a=a):
            return (jnp.max(keep_ref[2 * a:2 * a + 2]) > -SB_DEAD).astype(jnp.int32)

        def step(state, a=a, alive=alive):
            j, _ = state
            tiles([(a, [j], [False])])
            return j - 1, alive()

        lax.while_loop(lambda state: (state[0] >= 0) & (state[1] > 0), step, (i0 + a - 2, alive()))


def _sb_rows(j):
    return pl.ds(pl.multiple_of(j * BQ, BQ), BQ)


def _sb_chains(groups):
    chains = [(a, n, h) for a, js, _ in groups for n in range(len(js)) for h in range(2)]
    block = {(a, n): j for a, js, _ in groups for n, j in enumerate(js)}
    masked = [(a, n, h) for a, _, diags in groups for n, d in enumerate(diags) if d for h in range(2)]
    return chains, block, masked


def _sb_running(ref, vals, groups):
    before_chain = {}
    for a, js, _ in groups:
        for h in range(2):
            run = ref[2 * a + h]
            for n in range(len(js)):
                before_chain[(a, n, h)] = run
                run = run + jnp.sum(vals[(a, n, h)], axis=-1, keepdims=True)
            ref[2 * a + h] = run
    return before_chain


def _sb_specs(S, offs):
    def qspec(off=0):
        return pl.BlockSpec((SB_QB * BQ, PAIR), lambda p, i: (i, p + off))

    def kspec(off=0):
        return pl.BlockSpec((S, PAIR), lambda p, i: (0, p + off), pipeline_mode=pl.Buffered(1))

    return qspec, kspec, [qspec(offs[0]), kspec(offs[1]), kspec(offs[2])]


def _sb_fwd(q, k, v, name, offs=(0, 0, 0)):
    S = q.shape[0]
    steps = S // (SB_QB * BQ)
    qspec, _, qkv_specs = _sb_specs(S, offs)

    def body(q_ref, k_ref, v_ref, o_ref, of_ref, ot_ref, carry_ref, acc_ref):
        ip = pl.program_id(1)
        from_s, causal = _sb_consts()
        masks = _head_masks()
        qn = q_ref[...] * -(HEAD_DIM ** -0.5)
        qms = {(a, h): jnp.where(masks[h], qn[a * BQ:(a + 1) * BQ], 0) for a in range(SB_QB) for h in range(2)}
        carry_ref[...] = jnp.zeros_like(carry_ref)
        acc_ref[...] = jnp.zeros_like(acc_ref)

        def tiles(groups):
            chains, block, masked = _sb_chains(groups)
            kbs = {an: k_ref[_sb_rows(j), :] for an, j in block.items()}
            vbs = {an: v_ref[_sb_rows(j), :] for an, j in block.items()}
            zn = {c: _dot(qms[(c[0], c[2])], kbs[c[:2]], NT) for c in chains}
            log_keep = {c: _sb_log_keep(zn[c]) for c in chains}
            for c in masked:
                log_keep[c] = jnp.where(causal, log_keep[c], 0.0)
            split = {c: _split_bf16(log_keep[c]) for c in chains}
            carry = _sb_running(carry_ref, log_keep, groups)
            suffix = {c: _dot(split[c][0], from_s, NN) + _dot(split[c][1], from_s, NN) for c in chains}
            w = {c: jnp.exp(carry[c] + suffix[c] - zn[c]) for c in chains}
            for c in masked:
                w[c] = jnp.where(causal, w[c], 0.0)
            for c in chains:
                acc_ref[2 * c[0] + c[2]] += _dot(w[c].astype(BF16), vbs[c[:2]], NN)

        _sb_walk(ip, tiles, carry_ref)
        for a in range(SB_QB):
            out = jnp.where(masks[0], acc_ref[2 * a], acc_ref[2 * a + 1])
            o_ref[a * BQ:(a + 1) * BQ, :] = out.astype(BF16)
            of_ref[a * BQ:(a + 1) * BQ, :] = out
            ot_ref[:, a * BQ:(a + 1) * BQ] = out.T.astype(BF16)

    return pl.pallas_call(
        body, name=name, grid=(WIDTH // PAIR, steps),
        in_specs=qkv_specs, out_specs=[qspec(), qspec(), pl.BlockSpec((PAIR, SB_QB * BQ), lambda p, i: (p, i))],
        out_shape=[jax.ShapeDtypeStruct((S, WIDTH), BF16), jax.ShapeDtypeStruct((S, WIDTH), F32),
                   jax.ShapeDtypeStruct((WIDTH, S), BF16)],
        scratch_shapes=[pltpu.VMEM((2 * SB_QB, BQ, 1), F32), pltpu.VMEM((2 * SB_QB, BQ, PAIR), F32)],
        compiler_params=_cparams(("parallel", "arbitrary")),
    )(q, k, v)


def _sb_bwd(q, k, v, o, do, name, offs=(0, 0, 0), exchange=None):
    S = q.shape[0]
    steps = S // (SB_QB * BQ)
    pairs = WIDTH // PAIR
    qspec, kspec, qkv_specs = _sb_specs(S, offs)
    n_side = len(exchange.arrays) if exchange else 0

    def body(*refs):
        q_ref, o_ref, do_ref, k_ref, v_ref = refs[:5]
        dq_ref, dk_ref, dv_ref = refs[5 + n_side:8 + n_side]
        dk_acc, dv_acc, keep_ref, gsum_ref, dq_acc = refs[8 + 2 * n_side:13 + 2 * n_side]
        ip = pl.program_id(1)
        if exchange:
            start, finish = exchange.steps(refs[5:5 + n_side], refs[8 + n_side:8 + 2 * n_side], refs[13 + 2 * n_side:])
            pl.when((pl.program_id(0) == 0) & (ip == 0))(start)

        @pl.when(ip == 0)
        def _():
            dk_acc[...] = jnp.zeros_like(dk_acc)
            dv_acc[...] = jnp.zeros_like(dv_acc)

        from_s, causal = _sb_consts()
        masks = _head_masks()
        qn, dov = q_ref[...] * -(HEAD_DIM ** -0.5), do_ref[...]
        od = o_ref[...] * dov.astype(F32)
        lanes = [(a, h) for a in range(SB_QB) for h in range(2)]
        rows_of = {a: slice(a * BQ, (a + 1) * BQ) for a in range(SB_QB)}
        qms = {(a, h): jnp.where(masks[h], qn[rows_of[a]], 0) for a, h in lanes}
        doms = {(a, h): jnp.where(masks[h], dov[rows_of[a]], 0) for a, h in lanes}
        totals = {(a, h): jnp.sum(jnp.where(masks[h], od[rows_of[a]], 0.0), axis=-1, keepdims=True)
                  for a, h in lanes}
        for ref in (keep_ref, gsum_ref, dq_acc):
            ref[...] = jnp.zeros_like(ref)

        def tiles(groups):
            chains, block, masked = _sb_chains(groups)
            kbs = {an: k_ref[_sb_rows(j), :] for an, j in block.items()}
            vbs = {an: v_ref[_sb_rows(j), :] for an, j in block.items()}
            zn = {c: _dot(qms[(c[0], c[2])], kbs[c[:2]], NT) for c in chains}
            dw = {c: _dot(doms[(c[0], c[2])], vbs[c[:2]], NT) for c in chains}
            log_keep = {c: _sb_log_keep(zn[c]) for c in chains}
            for c in masked:
                log_keep[c] = jnp.where(causal, log_keep[c], 0.0)
            split = {c: _split_bf16(log_keep[c]) for c in chains}
            kept = _sb_running(keep_ref, log_keep, groups)
            suffix = {c: _dot(split[c][0], from_s, NN) + _dot(split[c][1], from_s, NN) for c in chains}
            w = {c: jnp.exp(kept[c] + suffix[c] - zn[c]) for c in chains}
            for c in masked:
                w[c] = jnp.where(causal, w[c], 0.0)
            wb = {c: w[c].astype(BF16) for c in chains}
            g = {c: wb[c].astype(F32) * dw[c] for c in chains}
            gsplit = {c: _split_bf16(g[c]) for c in chains}
            gsum = _sb_running(gsum_ref, g, groups)
            gsuffix = {c: _dot(gsplit[c][0], from_s, NN) + _dot(gsplit[c][1], from_s, NN) for c in chains}
            dzb = {}
            for c in chains:
                before = totals[(c[0], c[2])] - (gsum[c] + gsuffix[c])
                dz = (g[c] + before) * jnp.exp(log_keep[c]) - before
                if c in masked:
                    dz = jnp.where(causal, dz, 0.0)
                dzb[c] = dz.astype(BF16)
            for c in chains:
                rows = _sb_rows(block[c[:2]])
                dq_acc[2 * c[0] + c[2]] += _dot(dzb[c], kbs[c[:2]], NN)
                dk_acc[rows, :] -= _dot(dzb[c], qms[(c[0], c[2])], TN)
                dv_acc[rows, :] += _dot(wb[c], doms[(c[0], c[2])], TN)

        _sb_walk(ip, tiles, keep_ref)
        for a in range(SB_QB):
            dq = jnp.where(masks[0], dq_acc[2 * a], dq_acc[2 * a + 1])
            dq_ref[a * BQ:(a + 1) * BQ, :] = (dq * HEAD_DIM ** -0.5).astype(BF16)

        @pl.when(ip == steps - 1)
        def _():
            dk_ref[...] = dk_acc[...].astype(BF16)
            dv_ref[...] = dv_acc[...].astype(BF16)

        if exchange:
            pl.when((pl.program_id(0) == pairs - 1) & (ip == steps - 1))(finish)

    side = exchange.arrays if exchange else []
    return pl.pallas_call(
        body, name=name, grid=(pairs, steps),
        in_specs=[qkv_specs[0], qspec(), qspec(), qkv_specs[1], qkv_specs[2]] + [ANY] * n_side,
        out_specs=[qspec(), kspec(), kspec()] + [ANY] * n_side,
        out_shape=[jax.ShapeDtypeStruct((S, WIDTH), BF16)] * 3 + (exchange.out_shape if exchange else []),
        scratch_shapes=[pltpu.VMEM((S, PAIR), F32)] * 2 + [pltpu.VMEM((2 * SB_QB, BQ, 1), F32)] * 2
        + [pltpu.VMEM((2 * SB_QB, BQ, PAIR), F32)] + (exchange.scratch if exchange else []),
        compiler_params=_cparams(("arbitrary", "arbitrary") if exchange else ("parallel", "arbitrary")),
    )(q, o, do, k, v, *side)


ANY = pl.BlockSpec(memory_space=pl.ANY)


def _place():
    return lax.axis_index("x"), lax.axis_index("y"), lax.axis_index("c")


def _other_chips(x, y):
    return [(2 * px + py, (px, py)) for px, py in ((1 - x, y), (x, 1 - y), (1 - x, 1 - y))]


def _remote(src, dst, sems, k, to):
    return pltpu.make_async_remote_copy(src_ref=src, dst_ref=dst, send_sem=sems[0].at[k], recv_sem=sems[1].at[k],
                                        device_id=to, device_id_type=MESH)


class _Gather:
    def __init__(self, ws, extras=()):
        self.n, self.m = len(ws), len(extras)
        self.arrays = list(ws) + list(extras)
        self.n_copies = 6 * self.n + 3 * self.m
        self.out_shape = [jax.ShapeDtypeStruct((N_CHIPS,) + a.shape, a.dtype) for a in self.arrays]
        self.scratch = [pltpu.SemaphoreType.DMA((self.n_copies,)), pltpu.SemaphoreType.DMA((self.n_copies,))]

    def steps(self, in_refs, out_refs, sems):
        n = self.n
        x, y, c = _place()
        me = 2 * x + y
        chips = _other_chips(x, y)
        sibling = (x, y, 1 - c)

        def halves(ref):
            rh = ref.shape[-2] // 2
            return pl.ds(c * rh, rh), pl.ds((1 - c) * rh, rh)

        def first():
            cps = [_remote(w_ref.at[halves(w_ref)[0]], o_ref.at[me, halves(w_ref)[0]], sems, 6 * a + k, (*xy, c))
                   for a, (w_ref, o_ref) in enumerate(zip(in_refs[:n], out_refs[:n])) for k, (_, xy) in enumerate(chips)]
            return cps + [_remote(e_ref, eo_ref.at[me], sems, 6 * n + 3 * b + k, (*xy, c))
                          for b, (e_ref, eo_ref) in enumerate(zip(in_refs[n:], out_refs[n:]))
                          for k, (_, xy) in enumerate(chips)]

        def passed():
            return [_remote(o_ref.at[chip, halves(o_ref)[0]], o_ref.at[chip, halves(o_ref)[0]], sems, 6 * a + 3 + k, sibling)
                    for a, o_ref in enumerate(out_refs[:n]) for k, (chip, _) in enumerate(chips)]

        def start():
            for cp in first():
                cp.start()

        def forward():
            for a, o_ref in enumerate(out_refs[:n]):
                for k, (chip, xy) in enumerate(chips):
                    landed = o_ref.at[chip, halves(o_ref)[0]]
                    _remote(landed, landed, sems, 6 * a + k, (*xy, c)).wait_recv()
            for cp in passed():
                cp.start()

        def finish():
            for a, o_ref in enumerate(out_refs[:n]):
                for k, (chip, _) in enumerate(chips):
                    landed = o_ref.at[chip, halves(o_ref)[1]]
                    _remote(landed, landed, sems, 6 * a + 3 + k, sibling).wait_recv()
            for b, (e_ref, eo_ref) in enumerate(zip(in_refs[n:], out_refs[n:])):
                for k, (chip, xy) in enumerate(chips):
                    _remote(e_ref, eo_ref.at[chip], sems, 6 * n + 3 * b + k, (*xy, c)).wait_recv()
            for cp in first() + passed():
                cp.wait_send()

        return start, forward, finish


class _ChipExchange:
    def __init__(self, ps):
        self.arrays = list(ps)
        self.out_shape = [jax.ShapeDtypeStruct(p.shape, p.dtype) for p in ps]
        self.scratch = [pltpu.SemaphoreType.DMA((3 * len(ps),)), pltpu.SemaphoreType.DMA((3 * len(ps),))]

    def steps(self, p_refs, out_refs, sems):
        x, y, c = _place()
        me = 2 * x + y
        chips = _other_chips(x, y)

        def copies():
            return [_remote(p_ref.at[chip], o_ref.at[me], sems, 3 * a + k, (*xy, c))
                    for a, (p_ref, o_ref) in enumerate(zip(p_refs, out_refs)) for k, (chip, xy) in enumerate(chips)]

        def start():
            for cp in copies():
                cp.start()

        def finish():
            for a, (p_ref, o_ref) in enumerate(zip(p_refs, out_refs)):
                for k, (chip, xy) in enumerate(chips):
                    _remote(p_ref.at[chip], o_ref.at[chip], sems, 3 * a + k, (*xy, c)).wait_recv()
            for cp in copies():
                cp.wait_send()

        return start, finish


def _exchange_cores(gs, name, small=None):
    n = len(gs)
    m = 0 if small is None else 1

    def body(*refs):
        g_refs, sib_refs = refs[:n], refs[n + m:2 * n + m]
        sems = refs[2 * (n + m):]
        x, y, c = _place()
        me = 4 * x + 2 * y + c
        copies = []
        for a, (g_ref, sib_ref) in enumerate(zip(g_refs, sib_refs)):
            rh = g_ref.shape[1] // 2
            copies.append(_remote(g_ref.at[:, pl.ds((1 - c) * rh, rh), :], sib_ref, sems, a, (x, y, 1 - c)))
        if m:
            small_ref, all_ref = refs[n], refs[2 * n + m]
            k = n
            for fx in (0, 1):
                for fy in (0, 1):
                    for fc in (0, 1):
                        if fx or fy or fc:
                            to = (1 - x if fx else x, 1 - y if fy else y, 1 - c if fc else c)
                            copies.append(_remote(small_ref, all_ref.at[me], sems, k, to))
                            k += 1
        for cp in copies:
            cp.start()
        for cp in copies:
            cp.wait_recv()
        for cp in copies:
            cp.wait_send()

    n_copies = n + m * (N_DEV - 1)
    args = list(gs) + ([small] if m else [])
    return pl.pallas_call(
        body, name=name, in_specs=[ANY] * (n + m), out_specs=[ANY] * (n + m),
        out_shape=[jax.ShapeDtypeStruct((N_CHIPS, g.shape[1] // 2, g.shape[2]), F32) for g in gs]
        + ([jax.ShapeDtypeStruct((N_DEV,) + small.shape, F32)] if m else []),
        scratch_shapes=[pltpu.SemaphoreType.DMA((n_copies,)), pltpu.SemaphoreType.DMA((n_copies,))],
    )(*args)


def _share_halves(ghs, name):
    n = len(ghs)

    def body(*refs):
        gh_refs, out_refs, sems = refs[:n], refs[n:2 * n], refs[2 * n:]
        x, y, c = _place()
        copies = [_remote(gh_ref, o_ref, sems, a, (x, y, 1 - c)) for a, (gh_ref, o_ref) in enumerate(zip(gh_refs, out_refs))]
        for cp in copies:
            cp.start()
        for cp in copies:
            cp.wait_recv()
        for cp in copies:
            cp.wait_send()

    return pl.pallas_call(
        body, name=name, in_specs=[ANY] * n, out_specs=[ANY] * n,
        out_shape=[jax.ShapeDtypeStruct(g.shape, g.dtype) for g in ghs],
        scratch_shapes=[pltpu.SemaphoreType.DMA((n,)), pltpu.SemaphoreType.DMA((n,))],
    )(*ghs)


EW_BLOCK_BYTES = 2 * 1024 * 1024


def _row_block(rows, cols, mult=8):
    fits = [b for b in range(mult, rows + 1, mult) if rows % b == 0 and b * cols * 4 <= EW_BLOCK_BYTES]
    return max(fits) if fits else mult


def _add2(a, b, name):
    R, C = a.shape
    rows = _row_block(R, C, mult=16)
    spec = pl.BlockSpec((rows, C), lambda i: (i, 0))

    def body(a_ref, b_ref, o_ref):
        o_ref[...] = (a_ref[...] + b_ref[...]).astype(BF16)

    return pl.pallas_call(
        body, name=name, grid=(R // rows,), in_specs=[spec, spec], out_specs=spec,
        out_shape=jax.ShapeDtypeStruct(a.shape, BF16),
        compiler_params=_cparams(("parallel",)),
    )(a, b)


def _sum_leading(a, name):
    n, R, C = a.shape
    rows = _row_block(R, n * C, mult=16 if a.dtype == BF16 else 8)

    def body(a_ref, o_ref):
        acc = a_ref[0].astype(F32)
        for j in range(1, n):
            acc = acc + a_ref[j].astype(F32)
        o_ref[...] = acc

    return pl.pallas_call(
        body, name=name, grid=(R // rows,),
        in_specs=[pl.BlockSpec((n, rows, C), lambda i: (0, i, 0))],
        out_specs=pl.BlockSpec((rows, C), lambda i: (i, 0)),
        out_shape=jax.ShapeDtypeStruct((R, C), F32),
        compiler_params=_cparams(("parallel",)),
    )(a)


def _adamw(w, g, m, v, name):
    R, C = w.shape
    rows = _row_block(R, C)
    spec = pl.BlockSpec((rows, C), lambda i: (i, 0))

    def body(w_ref, g_ref, m_ref, v_ref, d_ref, mo_ref, vo_ref):
        gv = g_ref[...]
        mn = ADAM_B1 * m_ref[...] + (1.0 - ADAM_B1) * gv
        vn = ADAM_B2 * v_ref[...] + (1.0 - ADAM_B2) * (gv * gv)
        m_hat = mn / (1.0 - ADAM_B1 ** ADAM_STEP)
        v_hat = vn / (1.0 - ADAM_B2 ** ADAM_STEP)
        d_ref[...] = -ADAM_LR * (m_hat / (jnp.sqrt(v_hat) + ADAM_EPS) + ADAM_WD * w_ref[...])
        mo_ref[...] = mn
        vo_ref[...] = vn

    return pl.pallas_call(
        body, name=name, grid=(R // rows,), in_specs=[spec] * 4, out_specs=[spec] * 3,
        out_shape=[jax.ShapeDtypeStruct((R, C), F32)] * 3,
        compiler_params=_cparams(("parallel",)),
    )(w, g, m, v)


BIG = ("w_in", "w_branch_a", "w_branch_b", "w_out", "w_ffn_up", "w_ffn_down")
COL_SHARDED = {"w_in": True, "w_branch_a": True, "w_branch_b": True, "w_out": False, "w_ffn_up": True,
               "w_ffn_down": False}
CONV_W_COLS = 2 * D_FF // N_CHIPS
SMALL_REPLICATED = (("norm1_g", D_MODEL), ("q_norm_g", HEAD_DIM), ("k_norm_g", HEAD_DIM),
                    ("rel_bias", N_HEADS * N_REL), ("norm2_g", D_MODEL), ("ffn_conv_b", 2 * D_FF))
SMALL_GRADS = SMALL_REPLICATED + (("ffn_conv_w", 3 * 2 * D_FF),)
SMALL_OWN = SMALL_REPLICATED + (("ffn_conv_w", 3 * CONV_W_COLS),)
SMALL_GRAD_ROWS = 32
SMALL_OWN_ROWS = 16


def _whole(name, stacked):
    return jnp.concatenate(list(stacked), axis=1) if COL_SHARDED[name] else stacked.reshape(-1, stacked.shape[2])


def _pack_small(vals, sizes, rows):
    flat = jnp.concatenate([vals[n].reshape(-1) for n, _ in sizes])
    return jnp.pad(flat, (0, rows * PACK_COLS - flat.shape[0])).reshape(rows, PACK_COLS)


def _unpack_small(packed, sizes):
    flat, out, o = packed.reshape(-1), {}, 0
    for n, sz in sizes:
        out[n] = flat[o:o + sz]
        o += sz
    return out


def kernel(x, norm1_g, w_in, q_norm_g, k_norm_g, rel_bias, w_branch_a, w_branch_b, w_out, norm2_g, w_ffn_up, ffn_conv_w, ffn_conv_b, w_ffn_down, loss_target, m_norm1_g, m_w_in, m_q_norm_g, m_k_norm_g, m_rel_bias, m_w_branch_a, m_w_branch_b, m_w_out, m_norm2_g, m_w_ffn_up, m_ffn_conv_w, m_ffn_conv_b, m_w_ffn_down, v_norm1_g, v_w_in, v_q_norm_g, v_k_norm_g, v_rel_bias, v_w_branch_a, v_w_branch_b, v_w_out, v_norm2_g, v_w_ffn_up, v_ffn_conv_w, v_ffn_conv_b, v_w_ffn_down):
    w_big = {"w_in": w_in[0], "w_branch_a": w_branch_a[0], "w_branch_b": w_branch_b[0], "w_out": w_out[0],
             "w_ffn_up": w_ffn_up[0], "w_ffn_down": w_ffn_down[0]}
    m_big = {"w_in": m_w_in[0], "w_branch_a": m_w_branch_a[0], "w_branch_b": m_w_branch_b[0], "w_out": m_w_out[0],
             "w_ffn_up": m_w_ffn_up[0], "w_ffn_down": m_w_ffn_down[0]}
    v_big = {"w_in": v_w_in[0], "w_branch_a": v_w_branch_a[0], "w_branch_b": v_w_branch_b[0], "w_out": v_w_out[0],
             "w_ffn_up": v_w_ffn_up[0], "w_ffn_down": v_w_ffn_down[0]}
    xs, tgt = x[0], loss_target[0]

    xi, yi, ci = _place()
    chip = 2 * xi + yi

    def with_own(stacked, own):
        return lax.dynamic_update_slice(stacked, own[None], (chip,) + (0,) * own.ndim)

    shards_bf = {n: w_big[n].astype(BF16) for n in BIG}
    conv_own = jnp.pad(ffn_conv_w[0], ((0, 8 - ffn_conv_w.shape[1]), (0, 0)))
    later = [n for n in BIG if n != "w_in"]

    hn, hn_t, w_in_g = _rms_fwd(xs, norm1_g, "rms1", gather=_Gather([shards_bf["w_in"]]))
    w_in_f = _whole("w_in", with_own(w_in_g, shards_bf["w_in"]))
    w_in_t = w_in_f.T
    qk = _matmul(hn, w_in_f[:, :2 * WIDTH], F32, "proj_qk")
    vqkv = _matmul(hn, w_in_f[:, 2 * WIDTH:6 * WIDTH], BF16, "proj_vqkv")
    gates = _matmul(hn, w_in_f[:, 6 * WIDTH:], BF16, "proj_gates")
    gq = jnp.tile(q_norm_g, (1, N_HEADS))
    gk = jnp.tile(k_norm_g, (1, N_HEADS))
    qa, ka = _qknorm_fwd(qk, gq, gk, "qknorm")
    per = WIDTH // PAIR
    b_offs = (per, 2 * per, 3 * per)
    tab = _bias_table(jnp.pad(rel_bias[0], ((0, 0), (0, REL_PAD - N_REL))), "bias_table")
    out_a, out_a_t, *gathered = _ca_fwd(qa, ka, vqkv, tab, "chunk_attn",
                                        gather=_Gather([shards_bf[n] for n in later], [conv_own]))
    full = {n: _whole(n, with_own(g, shards_bf[n])) for n, g in zip(later, gathered)}
    conv_w = jnp.concatenate(list(with_own(gathered[-1], conv_own)[:, :3]), axis=1)
    w_a, w_b, w_o, w_up, w_dn = (full[n] for n in later)
    w_a_t, w_b_t, w_o_t, w_up_t, w_dn_t = (w.T for w in (w_a, w_b, w_o, w_up, w_dn))
    out_b, out_b_f32, out_b_t = _sb_fwd(vqkv, vqkv, vqkv, "stick_attn", b_offs)
    y_a = _matmul(out_a, w_a, BF16, "branch_a")
    y_b = _matmul(out_b, w_b, BF16, "branch_b")
    mixed, mixed_t = _mix_fwd(gates, y_a, y_b, "mix")
    x2 = _matmul(mixed, w_o, F32, "out_proj", residual=xs)
    hn2, hn2_t = _rms_fwd(x2, norm2_g, "rms2")
    hid = _matmul(hn2, w_up, BF16, "ffn_up")
    act, act_t = _convglu_fwd(hid, conv_w, ffn_conv_b, "convglu")
    dy, dyb, sq = _down_and_loss(act, w_dn, x2, tgt, "ffn_down_loss")
    loss = lax.psum(0.5 / D_MODEL * jnp.sum(sq), ("x", "y", "c"))

    dact = _matmul(dyb, w_dn_t, BF16, "d_act")
    d_w_dn = _matmul(act_t, dyb, F32, "d_w_down")
    dhg, dhu, dcwg, dcwu, dcbg, dcbu = _convglu_bwd(hid, dact, conv_w, ffn_conv_b, "convglu_bwd")
    half_chips = N_CHIPS // 2
    d_w_up = jnp.concatenate([_matmul(hn2_t, dhg, F32, "d_w_up_gate", slabs=half_chips),
                              _matmul(hn2_t, dhu, F32, "d_w_up_up", slabs=half_chips)], axis=0)
    dx2, dx2b, d_norm2 = _rms_bwd([(dhg, w_up_t[:D_FF]), (dhu, w_up_t[D_FF:])], x2, norm2_g, dy, "d_hn2_rms2_bwd")
    dmixed = _matmul(dx2b, w_o_t, BF16, "d_mixed")
    d_w_o = _matmul(mixed_t, dx2b, F32, "d_w_out")
    dgates, dya, dyb_b = _mix_bwd(dmixed, gates, y_a, y_b, "mix_bwd")
    d_w_a, d_w_b = (_matmul(o_t, d, F32, nm).reshape(WIDTH, N_CHIPS, -1).transpose(1, 0, 2)
                    for o_t, d, nm in ((out_a_t, dya, "d_w_branch_a"), (out_b_t, dyb_b, "d_w_branch_b")))
    do_a = _matmul(dya, w_a_t, BF16, "d_out_a")
    do_b = _matmul(dyb_b, w_b_t, BF16, "d_out_b")

    def core_sums(names, gs, sibs):
        out = {}
        for n, g, sib in zip(names, gs, sibs):
            rh, cols = sib.shape[1], sib.shape[2]
            mine = lax.dynamic_slice_in_dim(g, ci * rh, rh, axis=1)
            out[n] = _add2(mine.reshape(-1, cols), sib.reshape(-1, cols), "sum_cores_" + n).reshape(sib.shape)
        return out

    grads_full = {"w_branch_a": d_w_a, "w_branch_b": d_w_b, "w_ffn_up": d_w_up,
                  "w_out": d_w_o.reshape(N_CHIPS, -1, D_MODEL), "w_ffn_down": d_w_dn.reshape(N_CHIPS, -1, D_MODEL)}
    early = [grads_full[n] for n in later]
    chip_parts = core_sums(later, early, _exchange_cores(early, "exchange_cores_early"))
    dqb, dkb, dvb, *parts_early = _sb_bwd(vqkv, vqkv, vqkv, out_b_f32, do_b, "stick_attn_bwd", b_offs,
                                           exchange=_ChipExchange([chip_parts[n] for n in later]))
    parts = dict(zip(later, parts_early))
    dqa_n, dka_n, dva, dtab = _ca_bwd(qa, ka, vqkv, do_a, tab, "chunk_attn_bwd")
    d_rel = _bias_table_bwd(dtab, "bias_table_bwd")[:, :N_REL]
    dqa, dka, dgq, dgk = _qknorm_bwd(qk, gq, gk, dqa_n, dka_n, "qknorm_bwd")
    dproj = jnp.concatenate([dqa, dka, dva, dqb, dkb, dvb, dgates], axis=1)
    d_w_in = _matmul(hn_t, dproj, F32, "d_w_in", slabs=N_CHIPS)
    chip_parts.update(core_sums(["w_in"], [d_w_in], _exchange_cores([d_w_in], "exchange_cores_w_in")))
    dx, _, d_norm1, parts["w_in"] = _rms_bwd([(dproj, w_in_t)], xs, norm1_g, dx2, "d_hn_rms1_bwd",
                                             exchange=_ChipExchange([chip_parts["w_in"]]))

    small_g = _pack_small({"norm1_g": d_norm1, "q_norm_g": dgq.reshape(N_HEADS, HEAD_DIM).sum(0),
                           "k_norm_g": dgk.reshape(N_HEADS, HEAD_DIM).sum(0), "rel_bias": d_rel,
                           "norm2_g": d_norm2, "ffn_conv_b": jnp.concatenate([dcbg, dcbu], axis=1),
                           "ffn_conv_w": jnp.concatenate([dcwg, dcwu], axis=1)}, SMALL_GRADS, SMALL_GRAD_ROWS)
    (small_all,) = _exchange_cores([], "exchange_small", small=small_g)
    g_halves = [_sum_leading(with_own(parts[n], lax.dynamic_index_in_dim(chip_parts[n], chip, 0, keepdims=False)),
                             "sum_chips_" + n) for n in BIG]
    g_others = _share_halves(g_halves, "share_halves")
    grads = {n: jnp.concatenate([jnp.where(ci == 0, mine, other), jnp.where(ci == 0, other, mine)], axis=0)
             for n, mine, other in zip(BIG, g_halves, g_others)}
    small_all = lax.dynamic_update_slice(small_all, small_g[None], (4 * xi + 2 * yi + ci, 0, 0))
    small_sum = _unpack_small(_sum_leading(small_all, "sum_small"), SMALL_GRADS)
    small_sum["ffn_conv_w"] = lax.dynamic_slice_in_dim(small_sum["ffn_conv_w"].reshape(3, 2 * D_FF),
                                                       chip * CONV_W_COLS, CONV_W_COLS, axis=1)

    deltas, new_m, new_v = {}, {}, {}
    for n in BIG:
        deltas[n], new_m[n], new_v[n] = _adamw(w_big[n], grads[n], m_big[n], v_big[n], "adamw_" + n)

    shapes = {"norm1_g": norm1_g.shape, "q_norm_g": q_norm_g.shape, "k_norm_g": k_norm_g.shape,
              "rel_bias": rel_bias.shape, "norm2_g": norm2_g.shape, "ffn_conv_b": ffn_conv_b.shape,
              "ffn_conv_w": ffn_conv_w.shape}
    small_w = {"norm1_g": norm1_g, "q_norm_g": q_norm_g, "k_norm_g": k_norm_g, "rel_bias": rel_bias,
               "norm2_g": norm2_g, "ffn_conv_b": ffn_conv_b, "ffn_conv_w": ffn_conv_w}
    small_m = {"norm1_g": m_norm1_g, "q_norm_g": m_q_norm_g, "k_norm_g": m_k_norm_g, "rel_bias": m_rel_bias,
               "norm2_g": m_norm2_g, "ffn_conv_b": m_ffn_conv_b, "ffn_conv_w": m_ffn_conv_w}
    small_v = {"norm1_g": v_norm1_g, "q_norm_g": v_q_norm_g, "k_norm_g": v_k_norm_g, "rel_bias": v_rel_bias,
               "norm2_g": v_norm2_g, "ffn_conv_b": v_ffn_conv_b, "ffn_conv_w": v_ffn_conv_w}
    ds, ms, vs = _adamw(*(_pack_small(t, SMALL_OWN, SMALL_OWN_ROWS) for t in (small_w, small_sum, small_m, small_v)),
                        "adamw_small")
    small_grads = small_sum
    ds, ms, vs = (_unpack_small(t, SMALL_OWN) for t in (ds, ms, vs))

    order = ("norm1_g", "w_in", "q_norm_g", "k_norm_g", "rel_bias", "w_branch_a", "w_branch_b", "w_out",
             "norm2_g", "w_ffn_up", "ffn_conv_w", "ffn_conv_b", "w_ffn_down")
    outs = [loss, dx[None]]
    for big, small in ((grads, small_grads), (deltas, ds), (new_m, ms), (new_v, vs)):
        for n in order:
            outs.append(big[n][None] if n in big else small[n].reshape(shapes[n]))
    return tuple(outs)
```

```python
import functools

import jax
import jax.numpy as jnp
from jax import lax
from jax.experimental import pallas as pl
from jax.experimental.pallas import tpu as pltpu

F32 = jnp.float32
BF16 = jnp.bfloat16
MESH = pl.DeviceIdType.MESH

D_MODEL = 1024
HEAD_DIM = 64
N_HEADS = 8
WIDTH = N_HEADS * HEAD_DIM
CHUNK = 64
LEFT_CHUNKS = 8
MAX_REL = 128
N_REL = 2 * MAX_REL + 1
D_FF = 2816
EPS = 1e-6
NEG = -1e30

ADAM_LR = 0.001
ADAM_B1 = 0.9
ADAM_B2 = 0.999
ADAM_EPS = 1e-08
ADAM_WD = 0.01
ADAM_STEP = 10

N_CHIPS = 4
N_DEV = 8
LANES = 128
PAIR = 2 * HEAD_DIM
BQ = 256
BAND = LEFT_CHUNKS * CHUNK
KWIN = BAND + BQ
VMEM_LIMIT = 56 * 1024 * 1024
PACK_COLS = 1024

NN = (((1,), (0,)), ((), ()))
NT = (((1,), (1,)), ((), ()))
TN = (((0,), (0,)), ((), ()))


def _cparams(sem=None):
    if sem is None:
        return pltpu.CompilerParams(vmem_limit_bytes=VMEM_LIMIT)
    return pltpu.CompilerParams(dimension_semantics=sem, vmem_limit_bytes=VMEM_LIMIT)


def _pick(n, cands):
    for c in cands:
        if n % c == 0:
            return c
    raise ValueError(f"no block for {n}")


def _dot(a, b, dn):
    return lax.dot_general(a, b, dn, preferred_element_type=F32)


def _sigmoid(x):
    return 0.5 * jnp.tanh(0.5 * x) + 0.5


def _split_bf16(x):
    hi = x.astype(BF16)
    lo = (x - hi.astype(F32)).astype(BF16)
    return hi, lo


MM_RESIDENT_BYTES = 12 * 1024 * 1024
MM_TILE_BYTES = 4 * 1024 * 1024


def _matmul(a, b, out_dtype, name, residual=None, slabs=None):
    (M, K), N = a.shape, b.shape[1]
    out_bytes = jnp.dtype(out_dtype).itemsize
    if slabs is None and N <= D_FF and K * N * 2 <= MM_RESIDENT_BYTES:
        bk, bn = K, N
        bm = next(c for c in (1024, 512, 256, 128)
                  if M % c == 0 and c * K * 2 <= MM_TILE_BYTES and c * N * out_bytes <= MM_TILE_BYTES)
        b_spec = pl.BlockSpec((bk, bn), lambda i, j, k: (0, 0), pipeline_mode=pl.Buffered(1))
    elif slabs is None and K <= D_FF:
        bk, bm, bn = K, _pick(M, (1024, 512)), _pick(N, (D_FF // 2, 512, 256, 128))
        b_spec = pl.BlockSpec((bk, bn), lambda i, j, k: (k, j))
    else:
        bk = _pick(K, (2048, 1024, 512))
        bm = _pick(M, (D_FF // 2, 1024, 512, 256, 128))
        bn = N // slabs if slabs else _pick(N, (D_FF // 2, 1024, 512, 256, 128))
        b_spec = pl.BlockSpec((bk, bn), lambda i, j, k: (k, j))
    nk = K // bk
    dn = NN
    a_spec = pl.BlockSpec((bm, bk), lambda i, j, k: (i, k))
    if slabs:
        o_spec = pl.BlockSpec((None, bm, bn), lambda i, j, k: (j, i, 0))
        out_shape = jax.ShapeDtypeStruct((slabs, M, bn), out_dtype)
    else:
        o_spec = pl.BlockSpec((bm, bn), lambda i, j, k: (i, j))
        out_shape = jax.ShapeDtypeStruct((M, N), out_dtype)
    has_res = residual is not None

    def body(*refs):
        if has_res:
            a_ref, b_ref, r_ref, o_ref, acc_ref = refs
        else:
            a_ref, b_ref, o_ref, acc_ref = refs
        k = pl.program_id(2)
        part = _dot(a_ref[...], b_ref[...], dn)

        def finish(total):
            if has_res:
                total = total + r_ref[...]
            o_ref[...] = total.astype(out_dtype)

        if nk == 1:
            finish(part)
        else:
            @pl.when(k == 0)
            def _():
                acc_ref[...] = part

            @pl.when(k > 0)
            def _():
                acc_ref[...] += part

            @pl.when(k == nk - 1)
            def _():
                finish(acc_ref[...])

    in_specs = [a_spec, b_spec] + ([o_spec] if has_res else [])
    args = (a, b) + ((residual,) if has_res else ())
    return pl.pallas_call(
        body, name=name, grid=(M // bm, N // bn, nk),
        in_specs=in_specs, out_specs=o_spec, out_shape=out_shape,
        scratch_shapes=[pltpu.VMEM((bm, bn) if nk > 1 else (8, LANES), F32)],
        compiler_params=_cparams(("parallel", "parallel", "arbitrary")),
    )(*args)


ROWS = 512


def _row_spec(cols, bm=ROWS):
    return pl.BlockSpec((bm, cols), lambda i: (i, 0))


def _col_spec(rows, bn=ROWS):
    return pl.BlockSpec((rows, bn), lambda i: (0, i))


def _full_spec(shape):
    return pl.BlockSpec(shape, lambda i: (0,) * len(shape))


def _colsum8(t):
    return jnp.sum(t.reshape(t.shape[0] // 8, 8, t.shape[1]), axis=0)


def _rms_fwd(x, g, name, gather=None):
    S, D = x.shape
    nt = S // ROWS
    n_side = len(gather.arrays) if gather else 0

    def body(*refs):
        x_ref, g_ref = refs[:2]
        o_ref, ot_ref = refs[2 + n_side:4 + n_side]
        i = pl.program_id(0)
        if gather:
            start, forward, finish = gather.steps(refs[2:2 + n_side], refs[4 + n_side:4 + 2 * n_side], refs[4 + 2 * n_side:])
            pl.when(i == 0)(start)
            pl.when(i == 3 * nt // 4)(forward)
        xv = x_ref[...]
        r = lax.rsqrt(jnp.mean(xv * xv, axis=-1, keepdims=True) + EPS)
        y = xv * r * g_ref[...]
        o_ref[...] = y.astype(BF16)
        ot_ref[...] = y.T.astype(BF16)
        if gather:
            pl.when(i == nt - 1)(finish)

    side = gather.arrays if gather else []
    return pl.pallas_call(
        body, name=name, grid=(nt,),
        in_specs=[_row_spec(D), _full_spec((1, D))] + [ANY] * n_side,
        out_specs=[_row_spec(D), _col_spec(D)] + [ANY] * n_side,
        out_shape=[jax.ShapeDtypeStruct((S, D), BF16), jax.ShapeDtypeStruct((D, S), BF16)]
        + (gather.out_shape if gather else []),
        scratch_shapes=gather.scratch if gather else [],
        compiler_params=_cparams(("arbitrary",) if gather else ("parallel",)),
    )(x, g, *side)


RMS_BWD_ROWS = 256


def _rms_bwd(pairs, x, g, dres, name, exchange=None):
    S, D = x.shape
    bm = RMS_BWD_ROWS
    nt = S // bm
    n_pairs = len(pairs)
    n_side = len(exchange.arrays) if exchange else 0
    n_in = 2 * n_pairs + 3

    def body(*refs):
        x_ref, g_ref, dres_ref = refs[2 * n_pairs:n_in]
        dx_ref, dxb_ref, dg_ref = refs[n_in + n_side:n_in + n_side + 3]
        acc_ref = refs[n_in + 2 * n_side + 3]
        i = pl.program_id(0)
        if exchange:
            start, finish = exchange.steps(refs[n_in:n_in + n_side], refs[n_in + n_side + 3:n_in + 2 * n_side + 3],
                                           refs[n_in + 2 * n_side + 4:])
            pl.when(i == 0)(start)
        dyv = sum(_dot(refs[2 * p][...], refs[2 * p + 1][...], NN) for p in range(n_pairs))
        xv = x_ref[...]
        r = lax.rsqrt(jnp.mean(xv * xv, axis=-1, keepdims=True) + EPS)
        xr = xv * r
        u = dyv * g_ref[...]
        dx = r * u - xr * (r * r) * jnp.mean(xv * u, axis=-1, keepdims=True) + dres_ref[...]
        dx_ref[...] = dx
        dxb_ref[...] = dx.astype(BF16)
        part = _colsum8(dyv * xr)

        @pl.when(i == 0)
        def _():
            acc_ref[...] = part

        @pl.when(i > 0)
        def _():
            acc_ref[...] += part

        @pl.when(i == nt - 1)
        def _():
            dg_ref[...] = jnp.sum(acc_ref[...], axis=0, keepdims=True)

        if exchange:
            pl.when(i == nt - 1)(finish)

    rows = lambda cols: pl.BlockSpec((bm, cols), lambda i: (i, 0))
    in_specs = []
    for a, b in pairs:
        in_specs += [rows(a.shape[1]), pl.BlockSpec(b.shape, lambda i: (0, 0), pipeline_mode=pl.Buffered(1))]
    side = exchange.arrays if exchange else []
    return pl.pallas_call(
        body, name=name, grid=(nt,),
        in_specs=in_specs + [rows(D), _full_spec((1, D)), rows(D)] + [ANY] * n_side,
        out_specs=[rows(D), rows(D), _full_spec((1, D))] + [ANY] * n_side,
        out_shape=[jax.ShapeDtypeStruct((S, D), F32), jax.ShapeDtypeStruct((S, D), BF16),
                   jax.ShapeDtypeStruct((1, D), F32)] + (exchange.out_shape if exchange else []),
        scratch_shapes=[pltpu.VMEM((8, D), F32)] + (exchange.scratch if exchange else []),
        compiler_params=_cparams(("arbitrary",)),
    )(*[t for pair in pairs for t in pair], x, g, dres, *side)


def _head_mean(t, blockdiag):
    hi, lo = _split_bf16(t)
    return (_dot(hi, blockdiag, NN) + _dot(lo, blockdiag, NN)) * (1.0 / HEAD_DIM)


def _blockdiag():
    r = lax.broadcasted_iota(jnp.int32, (WIDTH, WIDTH), 0) // HEAD_DIM
    c = lax.broadcasted_iota(jnp.int32, (WIDTH, WIDTH), 1) // HEAD_DIM
    return jnp.where(r == c, 1.0, 0.0).astype(BF16)


def _qknorm_fwd(qk, gq, gk, name):
    S = qk.shape[0]

    def body(qk_ref, gq_ref, gk_ref, q_ref, k_ref):
        bd = _blockdiag()
        for part, g_ref, o_ref, scale in ((0, gq_ref, q_ref, HEAD_DIM ** -0.5), (1, gk_ref, k_ref, 1.0)):
            t = qk_ref[:, part * WIDTH:(part + 1) * WIDTH]
            r = lax.rsqrt(_head_mean(t * t, bd) + EPS)
            o_ref[...] = (t * r * g_ref[...] * scale).astype(BF16)

    return pl.pallas_call(
        body, name=name, grid=(S // ROWS,),
        in_specs=[_row_spec(2 * WIDTH), _full_spec((1, WIDTH)), _full_spec((1, WIDTH))],
        out_specs=[_row_spec(WIDTH), _row_spec(WIDTH)],
        out_shape=[jax.ShapeDtypeStruct((S, WIDTH), BF16)] * 2,
        compiler_params=_cparams(("parallel",)),
    )(qk, gq, gk)


def _qknorm_bwd(qk, gq, gk, dqn, dkn, name):
    S = qk.shape[0]
    nt = S // ROWS

    def body(qk_ref, gq_ref, gk_ref, dqn_ref, dkn_ref, dq_ref, dk_ref, dgq_ref, dgk_ref, accq_ref, acck_ref):
        i = pl.program_id(0)
        bd = _blockdiag()
        for part, g_ref, dn_ref, o_ref, dg_ref, acc_ref, scale in (
                (0, gq_ref, dqn_ref, dq_ref, dgq_ref, accq_ref, HEAD_DIM ** -0.5),
                (1, gk_ref, dkn_ref, dk_ref, dgk_ref, acck_ref, 1.0)):
            t = qk_ref[:, part * WIDTH:(part + 1) * WIDTH]
            dn = dn_ref[...] * scale
            r = lax.rsqrt(_head_mean(t * t, bd) + EPS)
            u = dn * g_ref[...]
            dt = r * u - t * (r * r * r) * _head_mean(t * u, bd)
            o_ref[...] = dt.astype(BF16)
            psum = _colsum8(dn * t * r)

            @pl.when(i == 0)
            def _():
                acc_ref[...] = psum

            @pl.when(i > 0)
            def _():
                acc_ref[...] += psum

            @pl.when(i == nt - 1)
            def _():
                dg_ref[...] = jnp.sum(acc_ref[...], axis=0, keepdims=True)

    return pl.pallas_call(
        body, name=name, grid=(nt,),
        in_specs=[_row_spec(2 * WIDTH), _full_spec((1, WIDTH)), _full_spec((1, WIDTH)),
                  _row_spec(WIDTH), _row_spec(WIDTH)],
        out_specs=[_row_spec(WIDTH), _row_spec(WIDTH), _full_spec((1, WIDTH)), _full_spec((1, WIDTH))],
        out_shape=[jax.ShapeDtypeStruct((S, WIDTH), BF16)] * 2 + [jax.ShapeDtypeStruct((1, WIDTH), F32)] * 2,
        scratch_shapes=[pltpu.VMEM((8, WIDTH), F32)] * 2,
        compiler_params=_cparams(("arbitrary",)),
    )(qk, gq, gk, dqn, dkn)


def _gate_specs(D):
    return [pl.BlockSpec((ROWS, D), lambda i: (i, 0)), pl.BlockSpec((ROWS, D), lambda i: (i, 1))]


def _mix_fwd(gates, ya, yb, name):
    S, D = ya.shape

    def body(ga_ref, gb_ref, ya_ref, yb_ref, o_ref, ot_ref):
        ga, gb, yav, ybv = (r[...].astype(F32) for r in (ga_ref, gb_ref, ya_ref, yb_ref))
        m = _sigmoid(ga) * yav + _sigmoid(gb) * ybv
        o_ref[...] = m.astype(BF16)
        ot_ref[...] = m.T.astype(BF16)

    return pl.pallas_call(
        body, name=name, grid=(S // ROWS,),
        in_specs=_gate_specs(D) + [_row_spec(D)] * 2, out_specs=[_row_spec(D), _col_spec(D)],
        out_shape=[jax.ShapeDtypeStruct((S, D), BF16), jax.ShapeDtypeStruct((D, S), BF16)],
        compiler_params=_cparams(("parallel",)),
    )(gates, gates, ya, yb)


def _mix_bwd(dm, gates, ya, yb, name):
    S, D = ya.shape

    def body(dm_ref, ga_ref, gb_ref, ya_ref, yb_ref, dg_ref, dya_ref, dyb_ref):
        dmv = dm_ref[...].astype(F32)
        for half, (g_ref, y_ref, dy_ref) in enumerate(((ga_ref, ya_ref, dya_ref), (gb_ref, yb_ref, dyb_ref))):
            s = _sigmoid(g_ref[...].astype(F32))
            dy_ref[...] = (dmv * s).astype(BF16)
            dg_ref[:, half * D:(half + 1) * D] = (dmv * y_ref[...].astype(F32) * s * (1.0 - s)).astype(BF16)

    return pl.pallas_call(
        body, name=name, grid=(S // ROWS,),
        in_specs=[_row_spec(D)] + _gate_specs(D) + [_row_spec(D)] * 2,
        out_specs=[_row_spec(2 * D), _row_spec(D), _row_spec(D)],
        out_shape=[jax.ShapeDtypeStruct((S, 2 * D), BF16)] + [jax.ShapeDtypeStruct((S, D), BF16)] * 2,
        compiler_params=_cparams(("parallel",)),
    )(dm, gates, gates, ya, yb)


def _down_and_loss(act, w, x2, target, name):
    (S, K), D = act.shape, w.shape[1]
    nt = S // ROWS

    def body(a_ref, w_ref, x_ref, t_ref, dy_ref, dyb_ref, p_ref):
        err = _dot(a_ref[...], w_ref[...], NN) + x_ref[...] - t_ref[...]
        dy = err * (1.0 / D)
        dy_ref[...] = dy
        dyb_ref[...] = dy.astype(BF16)
        sq = _colsum8(err * err)
        acc = sq[:, 0:LANES]
        for k in range(1, D // LANES):
            acc = acc + sq[:, k * LANES:(k + 1) * LANES]
        p_ref[...] = acc

    return pl.pallas_call(
        body, name=name, grid=(nt,),
        in_specs=[_row_spec(K), pl.BlockSpec((K, D), lambda i: (0, 0), pipeline_mode=pl.Buffered(1)),
                  _row_spec(D), _row_spec(D)],
        out_specs=[_row_spec(D), _row_spec(D), pl.BlockSpec((8, LANES), lambda i: (i, 0))],
        out_shape=[jax.ShapeDtypeStruct((S, D), F32), jax.ShapeDtypeStruct((S, D), BF16),
                   jax.ShapeDtypeStruct((nt * 8, LANES), F32)],
        compiler_params=_cparams(("parallel",)),
    )(act, w, x2, target)


CONV_COLS = D_FF // 2
HALO = 16
CONV_CHUNK = 64


def _aligned(start, multiple):
    return start if isinstance(start, int) else pl.multiple_of(start, multiple)


def _conv_taps(xe, cw, cb):
    taps = (pltpu.roll(xe, 2, 0), pltpu.roll(xe, 1, 0), xe)
    return taps, cw[0:1] * taps[0] + cw[1:2] * taps[1] + cw[2:3] * taps[2] + cb


def _conv_specs(nt):
    hb, nb = ROWS // HALO, D_FF // CONV_COLS
    specs = {}
    for part, off in (("gate", 0), ("up", nb)):
        specs[part] = dict(
            main=pl.BlockSpec((ROWS, CONV_COLS), functools.partial(lambda c, i, off: (i, c + off), off=off)),
            prev=pl.BlockSpec((HALO, CONV_COLS),
                              functools.partial(lambda c, i, off: (jnp.maximum(i * hb - 1, 0), c + off), off=off)),
            nxt=pl.BlockSpec((HALO, CONV_COLS),
                             functools.partial(lambda c, i, off: (jnp.minimum((i + 1) * hb, nt * hb - 1), c + off), off=off)),
            w=pl.BlockSpec((3, CONV_COLS), functools.partial(lambda c, i, off: (0, c + off), off=off)),
            b=pl.BlockSpec((1, CONV_COLS), functools.partial(lambda c, i, off: (0, c + off), off=off)))
    return specs


def _convglu_fwd(hid, cw, cb, name):
    S = hid.shape[0]
    sp = _conv_specs(S // ROWS)

    def body(hg_ref, hgp_ref, hu_ref, hup_ref, cwg_ref, cwu_ref, cbg_ref, cbu_ref, o_ref, ot_ref):
        i = pl.program_id(1)
        keep = (i > 0).astype(F32)

        def conv(h_ref, hp_ref, cw_ref, cb_ref):
            xe = jnp.concatenate([hp_ref[...].astype(F32) * keep, h_ref[...].astype(F32)], axis=0)
            return _conv_taps(xe, cw_ref[...], cb_ref[...])[1][HALO:, :]

        gate = conv(hg_ref, hgp_ref, cwg_ref, cbg_ref)
        up = conv(hu_ref, hup_ref, cwu_ref, cbu_ref)
        act = gate * _sigmoid(gate) * up
        o_ref[...] = act.astype(BF16)
        ot_ref[...] = act.T.astype(BF16)

    g, u = sp["gate"], sp["up"]
    return pl.pallas_call(
        body, name=name, grid=(D_FF // CONV_COLS, S // ROWS),
        in_specs=[g["main"], g["prev"], u["main"], u["prev"], g["w"], u["w"], g["b"], u["b"]],
        out_specs=[g["main"], pl.BlockSpec((CONV_COLS, ROWS), lambda c, i: (c, i))],
        out_shape=[jax.ShapeDtypeStruct((S, D_FF), BF16), jax.ShapeDtypeStruct((D_FF, S), BF16)],
        compiler_params=_cparams(("parallel", "parallel")),
    )(hid, hid, hid, hid, cw, cw, cb, cb)


def _convglu_bwd(hid, dact, cw, cb, name):
    S = hid.shape[0]
    nt = S // ROWS
    sp = _conv_specs(nt)

    n_chunks = ROWS // CONV_CHUNK

    def body(hg_ref, hgp_ref, hgn_ref, hu_ref, hup_ref, hun_ref, da_ref, dan_ref,
             cwg_ref, cwu_ref, cbg_ref, cbu_ref,
             dhg_ref, dhu_ref, dcwg_ref, dcwu_ref, dcbg_ref, dcbu_ref, xg_s, xu_s, da_s):
        i = pl.program_id(1)
        kp = (i > 0).astype(F32)
        kn = (i < nt - 1).astype(F32)
        for x_s, h_ref, hp_ref, hn_ref in ((xg_s, hg_ref, hgp_ref, hgn_ref), (xu_s, hu_ref, hup_ref, hun_ref)):
            x_s[0:HALO, :] = hp_ref[...].astype(F32) * kp
            x_s[HALO:HALO + ROWS, :] = h_ref[...].astype(F32)
            x_s[HALO + ROWS:, :] = hn_ref[...].astype(F32) * kn
        da_s[0:ROWS, :] = da_ref[...].astype(F32)
        da_s[ROWS:, :] = dan_ref[...].astype(F32) * kn

        @pl.when(i == 0)
        def _():
            for ref in (dcwg_ref, dcwu_ref, dcbg_ref, dcbu_ref):
                ref[...] = jnp.zeros_like(ref)

        def lane_group(grp, _):
            lanes = pl.ds(pl.multiple_of(grp * LANES, LANES), LANES)
            cwg, cwu, cbg, cbu = cwg_ref[:, lanes], cwu_ref[:, lanes], cbg_ref[:, lanes], cbu_ref[:, lanes]

            def grads(r0, n):
                rows = pl.ds(_aligned(r0 + HALO - 8, 8), n + 8)
                taps_g, gate = _conv_taps(xg_s[rows, lanes], cwg, cbg)
                taps_u, up = _conv_taps(xu_s[rows, lanes], cwu, cbu)
                gate, up = gate[8:], up[8:]
                da = da_s[pl.ds(_aligned(r0, 8), n), lanes]
                sg = _sigmoid(gate)
                return (da * up * sg * (1.0 + gate * (1.0 - sg)), da * gate * sg,
                        [t[8:] for t in taps_g], [t[8:] for t in taps_u])

            def chunk(step, carry):
                below_g, below_u, accs = carry
                r0 = (n_chunks - 1 - step) * CONV_CHUNK
                dg, du, taps_g, taps_u = grads(r0, CONV_CHUNK)
                new_accs = []
                for d, below, cwv, taps, dh_ref, acc in ((dg, below_g, cwg, taps_g, dhg_ref, accs[0]),
                                                        (du, below_u, cwu, taps_u, dhu_ref, accs[1])):
                    ext = jnp.concatenate([d, below], axis=0)
                    n_ext = CONV_CHUNK + 8
                    dh = (cwv[2:3] * d + cwv[1:2] * pltpu.roll(ext, n_ext - 1, 0)[:CONV_CHUNK]
                          + cwv[0:1] * pltpu.roll(ext, n_ext - 2, 0)[:CONV_CHUNK])
                    dh_ref[pl.ds(_aligned(r0, CONV_CHUNK), CONV_CHUNK), lanes] = dh.astype(BF16)
                    new_accs.append(tuple(a + _colsum8(d * tap) for a, tap in zip(acc[:3], taps))
                                    + (acc[3] + _colsum8(d),))
                return dg[0:8], du[0:8], tuple(new_accs)

            below_g, below_u, _, _ = grads(ROWS, 8)
            zero = jnp.zeros((8, LANES), F32)
            _, _, accs = lax.fori_loop(0, n_chunks, chunk, (below_g, below_u, ((zero,) * 4, (zero,) * 4)))
            for acc, dcw_ref, dcb_ref in ((accs[0], dcwg_ref, dcbg_ref), (accs[1], dcwu_ref, dcbu_ref)):
                for t in range(3):
                    dcw_ref[t:t + 1, lanes] += jnp.sum(acc[t], axis=0, keepdims=True)
                dcb_ref[:, lanes] += jnp.sum(acc[3], axis=0, keepdims=True)
            return 0

        lax.fori_loop(0, CONV_COLS // LANES, lane_group, 0)

    g, u = sp["gate"], sp["up"]
    return pl.pallas_call(
        body, name=name, grid=(D_FF // CONV_COLS, nt),
        in_specs=[g["main"], g["prev"], g["nxt"], u["main"], u["prev"], u["nxt"], g["main"], g["nxt"],
                  g["w"], u["w"], g["b"], u["b"]],
        out_specs=[g["main"], g["main"], g["w"], g["w"], g["b"], g["b"]],
        out_shape=[jax.ShapeDtypeStruct((S, D_FF), BF16)] * 2 + [jax.ShapeDtypeStruct((3, D_FF), F32)] * 2
        + [jax.ShapeDtypeStruct((1, D_FF), F32)] * 2,
        scratch_shapes=[pltpu.VMEM((ROWS + 2 * HALO, CONV_COLS), F32)] * 2 + [pltpu.VMEM((ROWS + HALO, CONV_COLS), F32)],
        compiler_params=_cparams(("parallel", "arbitrary")),
    )(hid, hid, hid, hid, hid, hid, dact, dact, cw, cw, cb, cb)


REL_PAD = 384
DIAG = 1024


def _band_valid():
    qc = lax.broadcasted_iota(jnp.int32, (BQ, KWIN), 0) // CHUNK
    kc = lax.broadcasted_iota(jnp.int32, (BQ, KWIN), 1) // CHUNK - LEFT_CHUNKS
    return (kc <= qc) & (kc >= qc - LEFT_CHUNKS)


def _rel_index(offset):
    return jnp.clip(BAND - offset, -MAX_REL, MAX_REL) + MAX_REL


def _split3(x):
    hi = x.astype(BF16)
    rest = x - hi.astype(F32)
    mid = rest.astype(BF16)
    return hi, mid, (rest - mid.astype(F32)).astype(BF16)


def _bias_table(rel_bias, name):
    def body(rb_ref, o_ref):
        t = lax.broadcasted_iota(jnp.int32, (REL_PAD, DIAG), 0)
        lane = lax.broadcasted_iota(jnp.int32, (REL_PAD, DIAG), 1)
        pick = jnp.where(t == _rel_index(lane - BQ), 1.0, 0.0).astype(BF16)
        base = sum(_dot(piece, pick, NN) for piece in _split3(rb_ref[...]))
        valid = _band_valid()
        for h in range(N_HEADS):
            rows = jnp.broadcast_to(base[h:h + 1], (BQ, DIAG))
            rolled = pltpu.roll(rows, 0, 1, stride=1, stride_axis=0)
            o_ref[h] = jnp.where(valid, rolled[:, BQ:], NEG)

    return pl.pallas_call(
        body, name=name,
        out_shape=jax.ShapeDtypeStruct((N_HEADS, BQ, KWIN), F32),
        compiler_params=_cparams(),
    )(rel_bias)


def _bias_table_bwd(dtab, name):
    def body(d_ref, o_ref, diag_ref):
        r = lax.broadcasted_iota(jnp.int32, (BQ, BQ), 0)
        c = lax.broadcasted_iota(jnp.int32, (BQ, BQ), 1)
        flip = jnp.where(r + c == BQ - 1, 1.0, 0.0).astype(BF16)
        for h in range(N_HEADS):
            flipped = sum(_dot(flip, piece, NN) for piece in _split3(d_ref[h]))
            padded = jnp.concatenate([flipped, jnp.zeros((BQ, DIAG - KWIN), F32)], axis=1)
            rolled = pltpu.roll(padded, DIAG - (BQ - 1), 1, stride=1, stride_axis=0)
            diag_ref[h:h + 1, :] = jnp.sum(rolled, axis=0, keepdims=True)
        lane = lax.broadcasted_iota(jnp.int32, (DIAG, REL_PAD), 0)
        t = lax.broadcasted_iota(jnp.int32, (DIAG, REL_PAD), 1)
        offset = jnp.where(lane < KWIN, lane, lane - DIAG)
        pick = jnp.where(t == _rel_index(offset), 1.0, 0.0).astype(BF16)
        o_ref[...] = sum(_dot(piece, pick, NN) for piece in _split3(diag_ref[...]))

    return pl.pallas_call(
        body, name=name,
        out_shape=jax.ShapeDtypeStruct((N_HEADS, REL_PAD), F32),
        scratch_shapes=[pltpu.VMEM((N_HEADS, DIAG), F32)],
        compiler_params=_cparams(),
    )(dtab)


def _head_masks(heads=2):
    lane = lax.broadcasted_iota(jnp.int32, (1, heads * HEAD_DIM), 1)
    return [lane // HEAD_DIM == h for h in range(heads)]


def _own_lanes(masks, vals):
    out = vals[-1]
    for m, val in zip(masks[-2::-1], vals[-2::-1]):
        out = jnp.where(m, val, out)
    return out


CA_HEADS = 4
CA_LANES = CA_HEADS * HEAD_DIM


def _ca_window_specs(nq, col_off=0):
    return [pl.BlockSpec((BQ, CA_LANES), functools.partial(
        lambda p, i, d: (jnp.clip(i - 2 + d, 0, nq - 1), p + col_off), d=d)) for d in range(3)]


def _softmax_rows(s):
    p = jnp.exp(s - jnp.max(s, axis=-1, keepdims=True))
    return p, jnp.sum(p, axis=-1, keepdims=True)


def _ca_scores(qm, kc, tab_h, i):
    col = lax.broadcasted_iota(jnp.int32, (1, KWIN), 1)
    in_seq = col + (i - 2) * BQ >= 0
    return jnp.where(in_seq, _dot(qm, kc, NT) + tab_h, NEG)


def _ca_fwd(qn, kn, v, tab, name, v_off=0, gather=None):
    S = qn.shape[0]
    nq = S // BQ
    groups = WIDTH // CA_LANES
    qspec = pl.BlockSpec((BQ, CA_LANES), lambda p, i: (i, p))
    tspec = pl.BlockSpec((CA_HEADS, BQ, KWIN), lambda p, i: (p, 0, 0))
    n_side = len(gather.arrays) if gather else 0

    def body(*refs):
        q_ref, k0, k1, k2, v0, v1, v2, tab_ref = refs[:8]
        o_ref, ot_ref = refs[8 + n_side:10 + n_side]
        p, i = pl.program_id(0), pl.program_id(1)
        if gather:
            start, forward, finish = gather.steps(refs[8:8 + n_side], refs[10 + n_side:10 + 2 * n_side],
                                                  refs[10 + 2 * n_side:])
            pl.when((p == 0) & (i == 0))(start)
            pl.when((p == groups - 1) & (i == nq // 2))(forward)
        kc = jnp.concatenate([k0[...], k1[...], k2[...]], axis=0)
        vc = jnp.concatenate([v0[...], v1[...], v2[...]], axis=0)
        qv = q_ref[...]
        masks = _head_masks(CA_HEADS)
        heads = range(CA_HEADS)
        s = [_ca_scores(jnp.where(masks[h], qv, 0), kc, tab_ref[h], i) for h in heads]
        soft = [_softmax_rows(s[h]) for h in heads]
        o = [_dot(soft[h][0].astype(BF16), vc, NN) / soft[h][1] for h in heads]
        out = _own_lanes(masks, o)
        o_ref[...] = out.astype(BF16)
        ot_ref[...] = out.T.astype(BF16)
        if gather:
            pl.when((p == groups - 1) & (i == nq - 1))(finish)

    side = gather.arrays if gather else []
    return pl.pallas_call(
        body, name=name, grid=(groups, nq),
        in_specs=[qspec] + _ca_window_specs(nq) + _ca_window_specs(nq, v_off) + [tspec] + [ANY] * n_side,
        out_specs=[qspec, pl.BlockSpec((CA_LANES, BQ), lambda p, i: (p, i))] + [ANY] * n_side,
        out_shape=[jax.ShapeDtypeStruct((S, WIDTH), BF16), jax.ShapeDtypeStruct((WIDTH, S), BF16)]
        + (gather.out_shape if gather else []),
        scratch_shapes=gather.scratch if gather else [],
        compiler_params=_cparams(("arbitrary", "arbitrary") if gather else ("parallel", "parallel")),
    )(qn, kn, kn, kn, v, v, v, tab, *side)


def _ca_bwd(qn, kn, v, do, tab, name, v_off=0):
    S = qn.shape[0]
    nq = S // BQ
    qspec = pl.BlockSpec((BQ, CA_LANES), lambda p, i: (jnp.minimum(i, nq - 1), p))
    kout = pl.BlockSpec((BQ, CA_LANES), lambda p, i: (jnp.clip(i - 2, 0, nq - 1), p))
    tspec = pl.BlockSpec((CA_HEADS, BQ, KWIN), lambda p, i: (p, 0, 0))

    def body(q_ref, do_ref, k0, k1, k2, v0, v1, v2, tab_ref,
             dq_ref, dk_ref, dv_ref, dtab_ref, dk_acc, dv_acc):
        i = pl.program_id(1)

        @pl.when(i == 0)
        def _():
            dk_acc[...] = jnp.zeros_like(dk_acc)
            dv_acc[...] = jnp.zeros_like(dv_acc)
            dtab_ref[...] = jnp.zeros_like(dtab_ref)

        @pl.when(i < nq)
        def _():
            kc = jnp.concatenate([k0[...], k1[...], k2[...]], axis=0)
            vc = jnp.concatenate([v0[...], v1[...], v2[...]], axis=0)
            qv, dov = q_ref[...], do_ref[...]
            masks = _head_masks(CA_HEADS)
            heads = range(CA_HEADS)
            qm = [jnp.where(masks[h], qv, 0) for h in heads]
            dom = [jnp.where(masks[h], dov, 0) for h in heads]
            s = [_ca_scores(qm[h], kc, tab_ref[h], i) for h in heads]
            dp = [_dot(dom[h], vc, NT) for h in heads]
            soft = [_softmax_rows(s[h]) for h in heads]
            p = [soft[h][0] / soft[h][1] for h in heads]
            ds = [p[h] * (dp[h] - jnp.sum(p[h] * dp[h], axis=-1, keepdims=True)) for h in heads]
            for h in heads:
                dtab_ref[h] += ds[h]
            dsb = [ds[h].astype(BF16) for h in heads]
            pb = [p[h].astype(BF16) for h in heads]
            dq = [_dot(dsb[h], kc, NN) for h in heads]
            dq_ref[...] = _own_lanes(masks, dq)
            dkc = sum(_dot(dsb[h], qm[h], TN) for h in heads)
            dvc = sum(_dot(pb[h], dom[h], TN) for h in heads)
            for d in range(3):
                slot = (i + 1 + d) % 3
                dk_acc[slot] += dkc[d * BQ:(d + 1) * BQ]
                dv_acc[slot] += dvc[d * BQ:(d + 1) * BQ]

        @pl.when(i >= 2)
        def _():
            slot = (i + 1) % 3
            dk_ref[...] = dk_acc[slot]
            dv_ref[...] = dv_acc[slot].astype(BF16)
            dk_acc[slot] = jnp.zeros((BQ, CA_LANES), F32)
            dv_acc[slot] = jnp.zeros((BQ, CA_LANES), F32)

    return pl.pallas_call(
        body, name=name, grid=(WIDTH // CA_LANES, nq + 2),
        in_specs=[qspec, qspec] + _ca_window_specs(nq) + _ca_window_specs(nq, v_off) + [tspec],
        out_specs=[qspec, kout, kout, tspec],
        out_shape=[jax.ShapeDtypeStruct((S, WIDTH), F32), jax.ShapeDtypeStruct((S, WIDTH), F32),
                   jax.ShapeDtypeStruct((S, WIDTH), BF16), jax.ShapeDtypeStruct((N_HEADS, BQ, KWIN), F32)],
        scratch_shapes=[pltpu.VMEM((3, BQ, CA_LANES), F32)] * 2,
        compiler_params=_cparams(("parallel", "arbitrary")),
    )(qn, do, kn, kn, kn, v, v, v, tab)


def _sb_consts():
    r = lax.broadcasted_iota(jnp.int32, (BQ, BQ), 0)
    c = lax.broadcasted_iota(jnp.int32, (BQ, BQ), 1)
    from_s = jnp.where(r >= c, 1.0, 0.0).astype(BF16)
    causal = c < r
    return from_s, causal


def _suffix_sum(t, from_s):
    hi, lo = _split_bf16(t)
    return _dot(hi, from_s, NN) + _dot(lo, from_s, NN)


def _neg_abs(x):
    bits = lax.bitcast_convert_type(x, jnp.uint32) | jnp.uint32(0x80000000)
    return lax.bitcast_convert_type(bits, F32)


def _sb_log_keep(zn):
    return jnp.minimum(zn, 0.0) - jnp.log(1.0 + jnp.exp(_neg_abs(zn)))


SB_DEAD = 105.0


SB_QB = 2


def _sb_walk(ip, tiles, keep_ref):
    i0 = SB_QB * ip

    @pl.when(ip == 0)
    def _():
        tiles([(0, [0], [True]), (1, [1, 0], [True, False])])

    @pl.when(ip > 0)
    def _():
        tiles([(a, [i0 + a, i0 + a - 1], [True, False]) for a in range(SB_QB)])

    for a in range(SB_QB):
        def alive(a=a):
            return (jnp.max(keep_ref[2 * a:2 * a + 2]) > -SB_DEAD).astype(jnp.int32)

        def step(state, a=a, alive=alive):
            j, _ = state
            tiles([(a, [j], [False])])
            return j - 1, alive()

        lax.while_loop(lambda state: (state[0] >= 0) & (state[1] > 0), step, (i0 + a - 2, alive()))


def _sb_rows(j):
    return pl.ds(pl.multiple_of(j * BQ, BQ), BQ)


def _sb_chains(groups):
    chains = [(a, n, h) for a, js, _ in groups for n in range(len(js)) for h in range(2)]
    block = {(a, n): j for a, js, _ in groups for n, j in enumerate(js)}
    masked = [(a, n, h) for a, _, diags in groups for n, d in enumerate(diags) if d for h in range(2)]
    return chains, block, masked


def _sb_running(ref, vals, groups):
    before_chain = {}
    for a, js, _ in groups:
        for h in range(2):
            run = ref[2 * a + h]
            for n in range(len(js)):
                before_chain[(a, n, h)] = run
                run = run + jnp.sum(vals[(a, n, h)], axis=-1, keepdims=True)
            ref[2 * a + h] = run
    return before_chain


def _sb_specs(S, offs):
    def qspec(off=0):
        return pl.BlockSpec((SB_QB * BQ, PAIR), lambda p, i: (i, p + off))

    def kspec(off=0):
        return pl.BlockSpec((S, PAIR), lambda p, i: (0, p + off), pipeline_mode=pl.Buffered(1))

    return qspec, kspec, [qspec(offs[0]), kspec(offs[1]), kspec(offs[2])]


def _sb_fwd(q, k, v, name, offs=(0, 0, 0)):
    S = q.shape[0]
    steps = S // (SB_QB * BQ)
    qspec, _, qkv_specs = _sb_specs(S, offs)

    def body(q_ref, k_ref, v_ref, o_ref, of_ref, ot_ref, carry_ref, acc_ref):
        ip = pl.program_id(1)
        from_s, causal = _sb_consts()
        masks = _head_masks()
        qn = q_ref[...] * -(HEAD_DIM ** -0.5)
        qms = {(a, h): jnp.where(masks[h], qn[a * BQ:(a + 1) * BQ], 0) for a in range(SB_QB) for h in range(2)}
        carry_ref[...] = jnp.zeros_like(carry_ref)
        acc_ref[...] = jnp.zeros_like(acc_ref)

        def tiles(groups):
            chains, block, masked = _sb_chains(groups)
            kbs = {an: k_ref[_sb_rows(j), :] for an, j in block.items()}
            vbs = {an: v_ref[_sb_rows(j), :] for an, j in block.items()}
            zn = {c: _dot(qms[(c[0], c[2])], kbs[c[:2]], NT) for c in chains}
            log_keep = {c: _sb_log_keep(zn[c]) for c in chains}
            for c in masked:
                log_keep[c] = jnp.where(causal, log_keep[c], 0.0)
            split = {c: _split_bf16(log_keep[c]) for c in chains}
            carry = _sb_running(carry_ref, log_keep, groups)
            suffix = {c: _dot(split[c][0], from_s, NN) + _dot(split[c][1], from_s, NN) for c in chains}
            w = {c: jnp.exp(carry[c] + suffix[c] - zn[c]) for c in chains}
            for c in masked:
                w[c] = jnp.where(causal, w[c], 0.0)
            for c in chains:
                acc_ref[2 * c[0] + c[2]] += _dot(w[c].astype(BF16), vbs[c[:2]], NN)

        _sb_walk(ip, tiles, carry_ref)
        for a in range(SB_QB):
            out = jnp.where(masks[0], acc_ref[2 * a], acc_ref[2 * a + 1])
            o_ref[a * BQ:(a + 1) * BQ, :] = out.astype(BF16)
            of_ref[a * BQ:(a + 1) * BQ, :] = out
            ot_ref[:, a * BQ:(a + 1) * BQ] = out.T.astype(BF16)

    return pl.pallas_call(
        body, name=name, grid=(WIDTH // PAIR, steps),
        in_specs=qkv_specs, out_specs=[qspec(), qspec(), pl.BlockSpec((PAIR, SB_QB * BQ), lambda p, i: (p, i))],
        out_shape=[jax.ShapeDtypeStruct((S, WIDTH), BF16), jax.ShapeDtypeStruct((S, WIDTH), F32),
                   jax.ShapeDtypeStruct((WIDTH, S), BF16)],
        scratch_shapes=[pltpu.VMEM((2 * SB_QB, BQ, 1), F32), pltpu.VMEM((2 * SB_QB, BQ, PAIR), F32)],
        compiler_params=_cparams(("parallel", "arbitrary")),
    )(q, k, v)


def _sb_bwd(q, k, v, o, do, name, offs=(0, 0, 0), exchange=None):
    S = q.shape[0]
    steps = S // (SB_QB * BQ)
    pairs = WIDTH // PAIR
    qspec, kspec, qkv_specs = _sb_specs(S, offs)
    n_side = len(exchange.arrays) if exchange else 0

    def body(*refs):
        q_ref, o_ref, do_ref, k_ref, v_ref = refs[:5]
        dq_ref, dk_ref, dv_ref = refs[5 + n_side:8 + n_side]
        dk_acc, dv_acc, keep_ref, gsum_ref, dq_acc = refs[8 + 2 * n_side:13 + 2 * n_side]
        ip = pl.program_id(1)
        if exchange:
            start, finish = exchange.steps(refs[5:5 + n_side], refs[8 + n_side:8 + 2 * n_side], refs[13 + 2 * n_side:])
            pl.when((pl.program_id(0) == 0) & (ip == 0))(start)

        @pl.when(ip == 0)
        def _():
            dk_acc[...] = jnp.zeros_like(dk_acc)
            dv_acc[...] = jnp.zeros_like(dv_acc)

        from_s, causal = _sb_consts()
        masks = _head_masks()
        qn, dov = q_ref[...] * -(HEAD_DIM ** -0.5), do_ref[...]
        od = o_ref[...] * dov.astype(F32)
        lanes = [(a, h) for a in range(SB_QB) for h in range(2)]
        rows_of = {a: slice(a * BQ, (a + 1) * BQ) for a in range(SB_QB)}
        qms = {(a, h): jnp.where(masks[h], qn[rows_of[a]], 0) for a, h in lanes}
        doms = {(a, h): jnp.where(masks[h], dov[rows_of[a]], 0) for a, h in lanes}
        totals = {(a, h): jnp.sum(jnp.where(masks[h], od[rows_of[a]], 0.0), axis=-1, keepdims=True)
                  for a, h in lanes}
        for ref in (keep_ref, gsum_ref, dq_acc):
            ref[...] = jnp.zeros_like(ref)

        def tiles(groups):
            chains, block, masked = _sb_chains(groups)
            kbs = {an: k_ref[_sb_rows(j), :] for an, j in block.items()}
            vbs = {an: v_ref[_sb_rows(j), :] for an, j in block.items()}
            zn = {c: _dot(qms[(c[0], c[2])], kbs[c[:2]], NT) for c in chains}
            dw = {c: _dot(doms[(c[0], c[2])], vbs[c[:2]], NT) for c in chains}
            log_keep = {c: _sb_log_keep(zn[c]) for c in chains}
            for c in masked:
                log_keep[c] = jnp.where(causal, log_keep[c], 0.0)
            split = {c: _split_bf16(log_keep[c]) for c in chains}
            kept = _sb_running(keep_ref, log_keep, groups)
            suffix = {c: _dot(split[c][0], from_s, NN) + _dot(split[c][1], from_s, NN) for c in chains}
            w = {c: jnp.exp(kept[c] + suffix[c] - zn[c]) for c in chains}
            for c in masked:
                w[c] = jnp.where(causal, w[c], 0.0)
            wb = {c: w[c].astype(BF16) for c in chains}
            g = {c: wb[c].astype(F32) * dw[c] for c in chains}
            gsplit = {c: _split_bf16(g[c]) for c in chains}
            gsum = _sb_running(gsum_ref, g, groups)
            gsuffix = {c: _dot(gsplit[c][0], from_s, NN) + _dot(gsplit[c][1], from_s, NN) for c in chains}
            dzb = {}
            for c in chains:
                before = totals[(c[0], c[2])] - (gsum[c] + gsuffix[c])
                dz = (g[c] + before) * jnp.exp(log_keep[c]) - before
                if c in masked:
                    dz = jnp.where(causal, dz, 0.0)
                dzb[c] = dz.astype(BF16)
            for c in chains:
                rows = _sb_rows(block[c[:2]])
                dq_acc[2 * c[0] + c[2]] += _dot(dzb[c], kbs[c[:2]], NN)
                dk_acc[rows, :] -= _dot(dzb[c], qms[(c[0], c[2])], TN)
                dv_acc[rows, :] += _dot(wb[c], doms[(c[0], c[2])], TN)

        _sb_walk(ip, tiles, keep_ref)
        for a in range(SB_QB):
            dq = jnp.where(masks[0], dq_acc[2 * a], dq_acc[2 * a + 1])
            dq_ref[a * BQ:(a + 1) * BQ, :] = (dq * HEAD_DIM ** -0.5).astype(BF16)

        @pl.when(ip == steps - 1)
        def _():
            dk_ref[...] = dk_acc[...].astype(BF16)
            dv_ref[...] = dv_acc[...].astype(BF16)

        if exchange:
            pl.when((pl.program_id(0) == pairs - 1) & (ip == steps - 1))(finish)

    side = exchange.arrays if exchange else []
    return pl.pallas_call(
        body, name=name, grid=(pairs, steps),
        in_specs=[qkv_specs[0], qspec(), qspec(), qkv_specs[1], qkv_specs[2]] + [ANY] * n_side,
        out_specs=[qspec(), kspec(), kspec()] + [ANY] * n_side,
        out_shape=[jax.ShapeDtypeStruct((S, WIDTH), BF16)] * 3 + (exchange.out_shape if exchange else []),
        scratch_shapes=[pltpu.VMEM((S, PAIR), F32)] * 2 + [pltpu.VMEM((2 * SB_QB, BQ, 1), F32)] * 2
        + [pltpu.VMEM((2 * SB_QB, BQ, PAIR), F32)] + (exchange.scratch if exchange else []),
        compiler_params=_cparams(("arbitrary", "arbitrary") if exchange else ("parallel", "arbitrary")),
    )(q, o, do, k, v, *side)


ANY = pl.BlockSpec(memory_space=pl.ANY)


def _place():
    return lax.axis_index("x"), lax.axis_index("y"), lax.axis_index("c")


def _other_chips(x, y):
    return [(2 * px + py, (px, py)) for px, py in ((1 - x, y), (x, 1 - y), (1 - x, 1 - y))]


def _remote(src, dst, sems, k, to):
    return pltpu.make_async_remote_copy(src_ref=src, dst_ref=dst, send_sem=sems[0].at[k], recv_sem=sems[1].at[k],
                                        device_id=to, device_id_type=MESH)


class _Gather:
    def __init__(self, ws, extras=()):
        self.n, self.m = len(ws), len(extras)
        self.arrays = list(ws) + list(extras)
        self.n_copies = 6 * self.n + 3 * self.m
        self.out_shape = [jax.ShapeDtypeStruct((N_CHIPS,) + a.shape, a.dtype) for a in self.arrays]
        self.scratch = [pltpu.SemaphoreType.DMA((self.n_copies,)), pltpu.SemaphoreType.DMA((self.n_copies,))]

    def steps(self, in_refs, out_refs, sems):
        n = self.n
        x, y, c = _place()
        me = 2 * x + y
        chips = _other_chips(x, y)
        sibling = (x, y, 1 - c)

        def halves(ref):
            rh = ref.shape[-2] // 2
            return pl.ds(c * rh, rh), pl.ds((1 - c) * rh, rh)

        def first():
            cps = [_remote(w_ref.at[halves(w_ref)[0]], o_ref.at[me, halves(w_ref)[0]], sems, 6 * a + k, (*xy, c))
                   for a, (w_ref, o_ref) in enumerate(zip(in_refs[:n], out_refs[:n])) for k, (_, xy) in enumerate(chips)]
            return cps + [_remote(e_ref, eo_ref.at[me], sems, 6 * n + 3 * b + k, (*xy, c))
                          for b, (e_ref, eo_ref) in enumerate(zip(in_refs[n:], out_refs[n:]))
                          for k, (_, xy) in enumerate(chips)]

        def passed():
            return [_remote(o_ref.at[chip, halves(o_ref)[0]], o_ref.at[chip, halves(o_ref)[0]], sems, 6 * a + 3 + k, sibling)
                    for a, o_ref in enumerate(out_refs[:n]) for k, (chip, _) in enumerate(chips)]

        def start():
            for cp in first():
                cp.start()

        def forward():
            for a, o_ref in enumerate(out_refs[:n]):
                for k, (chip, xy) in enumerate(chips):
                    landed = o_ref.at[chip, halves(o_ref)[0]]
                    _remote(landed, landed, sems, 6 * a + k, (*xy, c)).wait_recv()
            for cp in passed():
                cp.start()

        def finish():
            for a, o_ref in enumerate(out_refs[:n]):
                for k, (chip, _) in enumerate(chips):
                    landed = o_ref.at[chip, halves(o_ref)[1]]
                    _remote(landed, landed, sems, 6 * a + 3 + k, sibling).wait_recv()
            for b, (e_ref, eo_ref) in enumerate(zip(in_refs[n:], out_refs[n:])):
                for k, (chip, xy) in enumerate(chips):
                    _remote(e_ref, eo_ref.at[chip], sems, 6 * n + 3 * b + k, (*xy, c)).wait_recv()
            for cp in first() + passed():
                cp.wait_send()

        return start, forward, finish


class _ChipExchange:
    def __init__(self, ps):
        self.arrays = list(ps)
        self.out_shape = [jax.ShapeDtypeStruct(p.shape, p.dtype) for p in ps]
        self.scratch = [pltpu.SemaphoreType.DMA((3 * len(ps),)), pltpu.SemaphoreType.DMA((3 * len(ps),))]

    def steps(self, p_refs, out_refs, sems):
        x, y, c = _place()
        me = 2 * x + y
        chips = _other_chips(x, y)

        def copies():
            return [_remote(p_ref.at[chip], o_ref.at[me], sems, 3 * a + k, (*xy, c))
                    for a, (p_ref, o_ref) in enumerate(zip(p_refs, out_refs)) for k, (chip, xy) in enumerate(chips)]

        def start():
            for cp in copies():
                cp.start()

        def finish():
            for a, (p_ref, o_ref) in enumerate(zip(p_refs, out_refs)):
                for k, (chip, xy) in enumerate(chips):
                    _remote(p_ref.at[chip], o_ref.at[chip], sems, 3 * a + k, (*xy, c)).wait_recv()
            for cp in copies():
                cp.wait_send()

        return start, finish


def _exchange_cores(gs, name, small=None):
    n = len(gs)
    m = 0 if small is None else 1

    def body(*refs):
        g_refs, sib_refs = refs[:n], refs[n + m:2 * n + m]
        sems = refs[2 * (n + m):]
        x, y, c = _place()
        me = 4 * x + 2 * y + c
        copies = []
        for a, (g_ref, sib_ref) in enumerate(zip(g_refs, sib_refs)):
            rh = g_ref.shape[1] // 2
            copies.append(_remote(g_ref.at[:, pl.ds((1 - c) * rh, rh), :], sib_ref, sems, a, (x, y, 1 - c)))
        if m:
            small_ref, all_ref = refs[n], refs[2 * n + m]
            k = n
            for fx in (0, 1):
                for fy in (0, 1):
                    for fc in (0, 1):
                        if fx or fy or fc:
                            to = (1 - x if fx else x, 1 - y if fy else y, 1 - c if fc else c)
                            copies.append(_remote(small_ref, all_ref.at[me], sems, k, to))
                            k += 1
        for cp in copies:
            cp.start()
        for cp in copies:
            cp.wait_recv()
        for cp in copies:
            cp.wait_send()

    n_copies = n + m * (N_DEV - 1)
    args = list(gs) + ([small] if m else [])
    return pl.pallas_call(
        body, name=name, in_specs=[ANY] * (n + m), out_specs=[ANY] * (n + m),
        out_shape=[jax.ShapeDtypeStruct((N_CHIPS, g.shape[1] // 2, g.shape[2]), F32) for g in gs]
        + ([jax.ShapeDtypeStruct((N_DEV,) + small.shape, F32)] if m else []),
        scratch_shapes=[pltpu.SemaphoreType.DMA((n_copies,)), pltpu.SemaphoreType.DMA((n_copies,))],
    )(*args)


def _share_halves(ghs, name):
    n = len(ghs)

    def body(*refs):
        gh_refs, out_refs, sems = refs[:n], refs[n:2 * n], refs[2 * n:]
        x, y, c = _place()
        copies = [_remote(gh_ref, o_ref, sems, a, (x, y, 1 - c)) for a, (gh_ref, o_ref) in enumerate(zip(gh_refs, out_refs))]
        for cp in copies:
            cp.start()
        for cp in copies:
            cp.wait_recv()
        for cp in copies:
            cp.wait_send()

    return pl.pallas_call(
        body, name=name, in_specs=[ANY] * n, out_specs=[ANY] * n,
        out_shape=[jax.ShapeDtypeStruct(g.shape, g.dtype) for g in ghs],
        scratch_shapes=[pltpu.SemaphoreType.DMA((n,)), pltpu.SemaphoreType.DMA((n,))],
    )(*ghs)


EW_BLOCK_BYTES = 2 * 1024 * 1024


def _row_block(rows, cols, mult=8):
    fits = [b for b in range(mult, rows + 1, mult) if rows % b == 0 and b * cols * 4 <= EW_BLOCK_BYTES]
    return max(fits) if fits else mult


def _add2(a, b, name):
    R, C = a.shape
    rows = _row_block(R, C, mult=16)
    spec = pl.BlockSpec((rows, C), lambda i: (i, 0))

    def body(a_ref, b_ref, o_ref):
        o_ref[...] = (a_ref[...] + b_ref[...]).astype(BF16)

    return pl.pallas_call(
        body, name=name, grid=(R // rows,), in_specs=[spec, spec], out_specs=spec,
        out_shape=jax.ShapeDtypeStruct(a.shape, BF16),
        compiler_params=_cparams(("parallel",)),
    )(a, b)


def _sum_leading(a, name):
    n, R, C = a.shape
    rows = _row_block(R, n * C, mult=16 if a.dtype == BF16 else 8)

    def body(a_ref, o_ref):
        acc = a_ref[0].astype(F32)
        for j in range(1, n):
            acc = acc + a_ref[j].astype(F32)
        o_ref[...] = acc

    return pl.pallas_call(
        body, name=name, grid=(R // rows,),
        in_specs=[pl.BlockSpec((n, rows, C), lambda i: (0, i, 0))],
        out_specs=pl.BlockSpec((rows, C), lambda i: (i, 0)),
        out_shape=jax.ShapeDtypeStruct((R, C), F32),
        compiler_params=_cparams(("parallel",)),
    )(a)


def _adamw(w, g, m, v, name):
    R, C = w.shape
    rows = _row_block(R, C)
    spec = pl.BlockSpec((rows, C), lambda i: (i, 0))

    def body(w_ref, g_ref, m_ref, v_ref, d_ref, mo_ref, vo_ref):
        gv = g_ref[...]
        mn = ADAM_B1 * m_ref[...] + (1.0 - ADAM_B1) * gv
        vn = ADAM_B2 * v_ref[...] + (1.0 - ADAM_B2) * (gv * gv)
        m_hat = mn / (1.0 - ADAM_B1 ** ADAM_STEP)
        v_hat = vn / (1.0 - ADAM_B2 ** ADAM_STEP)
        d_ref[...] = -ADAM_LR * (m_hat / (jnp.sqrt(v_hat) + ADAM_EPS) + ADAM_WD * w_ref[...])
        mo_ref[...] = mn
        vo_ref[...] = vn

    return pl.pallas_call(
        body, name=name, grid=(R // rows,), in_specs=[spec] * 4, out_specs=[spec] * 3,
        out_shape=[jax.ShapeDtypeStruct((R, C), F32)] * 3,
        compiler_params=_cparams(("parallel",)),
    )(w, g, m, v)


BIG = ("w_in", "w_branch_a", "w_branch_b", "w_out", "w_ffn_up", "w_ffn_down")
COL_SHARDED = {"w_in": True, "w_branch_a": True, "w_branch_b": True, "w_out": False, "w_ffn_up": True,
               "w_ffn_down": False}
CONV_W_COLS = 2 * D_FF // N_CHIPS
SMALL_REPLICATED = (("norm1_g", D_MODEL), ("q_norm_g", HEAD_DIM), ("k_norm_g", HEAD_DIM),
                    ("rel_bias", N_HEADS * N_REL), ("norm2_g", D_MODEL), ("ffn_conv_b", 2 * D_FF))
SMALL_GRADS = SMALL_REPLICATED + (("ffn_conv_w", 3 * 2 * D_FF),)
SMALL_OWN = SMALL_REPLICATED + (("ffn_conv_w", 3 * CONV_W_COLS),)
SMALL_GRAD_ROWS = 32
SMALL_OWN_ROWS = 16


def _whole(name, stacked):
    return jnp.concatenate(list(stacked), axis=1) if COL_SHARDED[name] else stacked.reshape(-1, stacked.shape[2])


def _pack_small(vals, sizes, rows):
    flat = jnp.concatenate([vals[n].reshape(-1) for n, _ in sizes])
    return jnp.pad(flat, (0, rows * PACK_COLS - flat.shape[0])).reshape(rows, PACK_COLS)


def _unpack_small(packed, sizes):
    flat, out, o = packed.reshape(-1), {}, 0
    for n, sz in sizes:
        out[n] = flat[o:o + sz]
        o += sz
    return out


def kernel(x, norm1_g, w_in, q_norm_g, k_norm_g, rel_bias, w_branch_a, w_branch_b, w_out, norm2_g, w_ffn_up, ffn_conv_w, ffn_conv_b, w_ffn_down, loss_target, m_norm1_g, m_w_in, m_q_norm_g, m_k_norm_g, m_rel_bias, m_w_branch_a, m_w_branch_b, m_w_out, m_norm2_g, m_w_ffn_up, m_ffn_conv_w, m_ffn_conv_b, m_w_ffn_down, v_norm1_g, v_w_in, v_q_norm_g, v_k_norm_g, v_rel_bias, v_w_branch_a, v_w_branch_b, v_w_out, v_norm2_g, v_w_ffn_up, v_ffn_conv_w, v_ffn_conv_b, v_w_ffn_down):
    w_big = {"w_in": w_in[0], "w_branch_a": w_branch_a[0], "w_branch_b": w_branch_b[0], "w_out": w_out[0],
             "w_ffn_up": w_ffn_up[0], "w_ffn_down": w_ffn_down[0]}
    m_big = {"w_in": m_w_in[0], "w_branch_a": m_w_branch_a[0], "w_branch_b": m_w_branch_b[0], "w_out": m_w_out[0],
             "w_ffn_up": m_w_ffn_up[0], "w_ffn_down": m_w_ffn_down[0]}
    v_big = {"w_in": v_w_in[0], "w_branch_a": v_w_branch_a[0], "w_branch_b": v_w_branch_b[0], "w_out": v_w_out[0],
             "w_ffn_up": v_w_ffn_up[0], "w_ffn_down": v_w_ffn_down[0]}
    xs, tgt = x[0], loss_target[0]

    xi, yi, ci = _place()
    chip = 2 * xi + yi

    def with_own(stacked, own):
        return lax.dynamic_update_slice(stacked, own[None], (chip,) + (0,) * own.ndim)

    shards_bf = {n: w_big[n].astype(BF16) for n in BIG}
    conv_own = jnp.pad(ffn_conv_w[0], ((0, 8 - ffn_conv_w.shape[1]), (0, 0)))
    later = [n for n in BIG if n != "w_in"]

    hn, hn_t, w_in_g = _rms_fwd(xs, norm1_g, "rms1", gather=_Gather([shards_bf["w_in"]]))
    w_in_f = _whole("w_in", with_own(w_in_g, shards_bf["w_in"]))
    w_in_t = w_in_f.T
    qk = _matmul(hn, w_in_f[:, :2 * WIDTH], F32, "proj_qk")
    vqkv = _matmul(hn, w_in_f[:, 2 * WIDTH:6 * WIDTH], BF16, "proj_vqkv")
    gates = _matmul(hn, w_in_f[:, 6 * WIDTH:], BF16, "proj_gates")
    gq = jnp.tile(q_norm_g, (1, N_HEADS))
    gk = jnp.tile(k_norm_g, (1, N_HEADS))
    qa, ka = _qknorm_fwd(qk, gq, gk, "qknorm")
    per = WIDTH // PAIR
    b_offs = (per, 2 * per, 3 * per)
    tab = _bias_table(jnp.pad(rel_bias[0], ((0, 0), (0, REL_PAD - N_REL))), "bias_table")
    out_a, out_a_t, *gathered = _ca_fwd(qa, ka, vqkv, tab, "chunk_attn",
                                        gather=_Gather([shards_bf[n] for n in later], [conv_own]))
    full = {n: _whole(n, with_own(g, shards_bf[n])) for n, g in zip(later, gathered)}
    conv_w = jnp.concatenate(list(with_own(gathered[-1], conv_own)[:, :3]), axis=1)
    w_a, w_b, w_o, w_up, w_dn = (full[n] for n in later)
    w_a_t, w_b_t, w_o_t, w_up_t, w_dn_t = (w.T for w in (w_a, w_b, w_o, w_up, w_dn))
    out_b, out_b_f32, out_b_t = _sb_fwd(vqkv, vqkv, vqkv, "stick_attn", b_offs)
    y_a = _matmul(out_a, w_a, BF16, "branch_a")
    y_b = _matmul(out_b, w_b, BF16, "branch_b")
    mixed, mixed_t = _mix_fwd(gates, y_a, y_b, "mix")
    x2 = _matmul(mixed, w_o, F32, "out_proj", residual=xs)
    hn2, hn2_t = _rms_fwd(x2, norm2_g, "rms2")
    hid = _matmul(hn2, w_up, BF16, "ffn_up")
    act, act_t = _convglu_fwd(hid, conv_w, ffn_conv_b, "convglu")
    dy, dyb, sq = _down_and_loss(act, w_dn, x2, tgt, "ffn_down_loss")
    loss = lax.psum(0.5 / D_MODEL * jnp.sum(sq), ("x", "y", "c"))

    dact = _matmul(dyb, w_dn_t, BF16, "d_act")
    d_w_dn = _matmul(act_t, dyb, F32, "d_w_down")
    dhg, dhu, dcwg, dcwu, dcbg, dcbu = _convglu_bwd(hid, dact, conv_w, ffn_conv_b, "convglu_bwd")
    half_chips = N_CHIPS // 2
    d_w_up = jnp.concatenate([_matmul(hn2_t, dhg, F32, "d_w_up_gate", slabs=half_chips),
                              _matmul(hn2_t, dhu, F32, "d_w_up_up", slabs=half_chips)], axis=0)
    dx2, dx2b, d_norm2 = _rms_bwd([(dhg, w_up_t[:D_FF]), (dhu, w_up_t[D_FF:])], x2, norm2_g, dy, "d_hn2_rms2_bwd")
    dmixed = _matmul(dx2b, w_o_t, BF16, "d_mixed")
    d_w_o = _matmul(mixed_t, dx2b, F32, "d_w_out")
    dgates, dya, dyb_b = _mix_bwd(dmixed, gates, y_a, y_b, "mix_bwd")
    d_w_a, d_w_b = (_matmul(o_t, d, F32, nm).reshape(WIDTH, N_CHIPS, -1).transpose(1, 0, 2)
                    for o_t, d, nm in ((out_a_t, dya, "d_w_branch_a"), (out_b_t, dyb_b, "d_w_branch_b")))
    do_a = _matmul(dya, w_a_t, BF16, "d_out_a")
    do_b = _matmul(dyb_b, w_b_t, BF16, "d_out_b")

    def core_sums(names, gs, sibs):
        out = {}
        for n, g, sib in zip(names, gs, sibs):
            rh, cols = sib.shape[1], sib.shape[2]
            mine = lax.dynamic_slice_in_dim(g, ci * rh, rh, axis=1)
            out[n] = _add2(mine.reshape(-1, cols), sib.reshape(-1, cols), "sum_cores_" + n).reshape(sib.shape)
        return out

    grads_full = {"w_branch_a": d_w_a, "w_branch_b": d_w_b, "w_ffn_up": d_w_up,
                  "w_out": d_w_o.reshape(N_CHIPS, -1, D_MODEL), "w_ffn_down": d_w_dn.reshape(N_CHIPS, -1, D_MODEL)}
    early = [grads_full[n] for n in later]
    chip_parts = core_sums(later, early, _exchange_cores(early, "exchange_cores_early"))
    dqb, dkb, dvb, *parts_early = _sb_bwd(vqkv, vqkv, vqkv, out_b_f32, do_b, "stick_attn_bwd", b_offs,
                                           exchange=_ChipExchange([chip_parts[n] for n in later]))
    parts = dict(zip(later, parts_early))
    dqa_n, dka_n, dva, dtab = _ca_bwd(qa, ka, vqkv, do_a, tab, "chunk_attn_bwd")
    d_rel = _bias_table_bwd(dtab, "bias_table_bwd")[:, :N_REL]
    dqa, dka, dgq, dgk = _qknorm_bwd(qk, gq, gk, dqa_n, dka_n, "qknorm_bwd")
    pieces = (("qa", dqa), ("ka", dka), ("va", dva), ("qb", dqb), ("kb", dkb), ("vb", dvb), ("gates", dgates))
    d_w_in = jnp.concatenate([_matmul(hn_t, d, F32, "d_w_in_" + nm) for nm, d in pieces], axis=1)
    d_w_in = d_w_in.reshape(D_MODEL, N_CHIPS, -1).transpose(1, 0, 2)
    chip_parts.update(core_sums(["w_in"], [d_w_in], _exchange_cores([d_w_in], "exchange_cores_w_in")))
    offsets = [sum(d.shape[1] for _, d in pieces[:k]) for k in range(len(pieces))]
    dx, _, d_norm1, parts["w_in"] = _rms_bwd([(d, w_in_t[o:o + d.shape[1]]) for (_, d), o in zip(pieces, offsets)],
                                             xs, norm1_g, dx2, "d_hn_rms1_bwd",
                                             exchange=_ChipExchange([chip_parts["w_in"]]))

    small_g = _pack_small({"norm1_g": d_norm1, "q_norm_g": dgq.reshape(N_HEADS, HEAD_DIM).sum(0),
                           "k_norm_g": dgk.reshape(N_HEADS, HEAD_DIM).sum(0), "rel_bias": d_rel,
                           "norm2_g": d_norm2, "ffn_conv_b": jnp.concatenate([dcbg, dcbu], axis=1),
                           "ffn_conv_w": jnp.concatenate([dcwg, dcwu], axis=1)}, SMALL_GRADS, SMALL_GRAD_ROWS)
    (small_all,) = _exchange_cores([], "exchange_small", small=small_g)
    g_halves = [_sum_leading(with_own(parts[n], lax.dynamic_index_in_dim(chip_parts[n], chip, 0, keepdims=False)),
                             "sum_chips_" + n) for n in BIG]
    g_others = _share_halves(g_halves, "share_halves")
    grads = {n: jnp.concatenate([jnp.where(ci == 0, mine, other), jnp.where(ci == 0, other, mine)], axis=0)
             for n, mine, other in zip(BIG, g_halves, g_others)}
    small_all = lax.dynamic_update_slice(small_all, small_g[None], (4 * xi + 2 * yi + ci, 0, 0))
    small_sum = _unpack_small(_sum_leading(small_all, "sum_small"), SMALL_GRADS)
    small_sum["ffn_conv_w"] = lax.dynamic_slice_in_dim(small_sum["ffn_conv_w"].reshape(3, 2 * D_FF),
                                                       chip * CONV_W_COLS, CONV_W_COLS, axis=1)

    deltas, new_m, new_v = {}, {}, {}
    for n in BIG:
        deltas[n], new_m[n], new_v[n] = _adamw(w_big[n], grads[n], m_big[n], v_big[n], "adamw_" + n)

    shapes = {"norm1_g": norm1_g.shape, "q_norm_g": q_norm_g.shape, "k_norm_g": k_norm_g.shape,
              "rel_bias": rel_bias.shape, "norm2_g": norm2_g.shape, "ffn_conv_b": ffn_conv_b.shape,
              "ffn_conv_w": ffn_conv_w.shape}
    small_w = {"norm1_g": norm1_g, "q_norm_g": q_norm_g, "k_norm_g": k_norm_g, "rel_bias": rel_bias,
               "norm2_g": norm2_g, "ffn_conv_b": ffn_conv_b, "ffn_conv_w": ffn_conv_w}
    small_m = {"norm1_g": m_norm1_g, "q_norm_g": m_q_norm_g, "k_norm_g": m_k_norm_g, "rel_bias": m_rel_bias,
               "norm2_g": m_norm2_g, "ffn_conv_b": m_ffn_conv_b, "ffn_conv_w": m_ffn_conv_w}
    small_v = {"norm1_g": v_norm1_g, "q_norm_g": v_q_norm_g, "k_norm_g": v_k_norm_g, "rel_bias": v_rel_bias,
               "norm2_g": v_norm2_g, "ffn_conv_b": v_ffn_conv_b, "ffn_conv_w": v_ffn_conv_w}
    ds, ms, vs = _adamw(*(_pack_small(t, SMALL_OWN, SMALL_OWN_ROWS) for t in (small_w, small_sum, small_m, small_v)),
                        "adamw_small")
    small_grads = small_sum
    ds, ms, vs = (_unpack_small(t, SMALL_OWN) for t in (ds, ms, vs))

    order = ("norm1_g", "w_in", "q_norm_g", "k_norm_g", "rel_bias", "w_branch_a", "w_branch_b", "w_out",
             "norm2_g", "w_ffn_up", "ffn_conv_w", "ffn_conv_b", "w_ffn_down")
    outs = [loss, dx[None]]
    for big, small in ((grads, small_grads), (deltas, ds), (new_m, ms), (new_v, vs)):
        for n in order:
            outs.append(big[n][None] if n in big else small[n].reshape(shapes[n]))
    return tuple(outs)
```

```python
import functools

import jax
import jax.numpy as jnp
from jax import lax
from jax.experimental import pallas as pl
from jax.experimental.pallas import tpu as pltpu

F32 = jnp.float32
BF16 = jnp.bfloat16
MESH = pl.DeviceIdType.MESH

D_MODEL = 1024
HEAD_DIM = 64
N_HEADS = 8
WIDTH = N_HEADS * HEAD_DIM
CHUNK = 64
LEFT_CHUNKS = 8
MAX_REL = 128
N_REL = 2 * MAX_REL + 1
D_FF = 2816
EPS = 1e-6
NEG = -1e30

ADAM_LR = 0.001
ADAM_B1 = 0.9
ADAM_B2 = 0.999
ADAM_EPS = 1e-08
ADAM_WD = 0.01
ADAM_STEP = 10

N_CHIPS = 4
N_DEV = 8
LANES = 128
PAIR = 2 * HEAD_DIM
BQ = 256
BAND = LEFT_CHUNKS * CHUNK
KWIN = BAND + BQ
VMEM_LIMIT = 56 * 1024 * 1024
PACK_COLS = 1024

NN = (((1,), (0,)), ((), ()))
NT = (((1,), (1,)), ((), ()))
TN = (((0,), (0,)), ((), ()))


def _cparams(sem=None):
    if sem is None:
        return pltpu.CompilerParams(vmem_limit_bytes=VMEM_LIMIT)
    return pltpu.CompilerParams(dimension_semantics=sem, vmem_limit_bytes=VMEM_LIMIT)


def _pick(n, cands):
    for c in cands:
        if n % c == 0:
            return c
    raise ValueError(f"no block for {n}")


def _dot(a, b, dn):
    return lax.dot_general(a, b, dn, preferred_element_type=F32)


def _sigmoid(x):
    return 0.5 * jnp.tanh(0.5 * x) + 0.5


def _split_bf16(x):
    hi = x.astype(BF16)
    lo = (x - hi.astype(F32)).astype(BF16)
    return hi, lo


MM_RESIDENT_BYTES = 12 * 1024 * 1024
MM_TILE_BYTES = 4 * 1024 * 1024


def _matmul(a, b, out_dtype, name, slabs=None):
    (M, K), N = a.shape, b.shape[1]
    out_bytes = jnp.dtype(out_dtype).itemsize
    if slabs is None and N <= D_FF and K * N * 2 <= MM_RESIDENT_BYTES:
        bk, bn = K, N
        bm = next(c for c in (1024, 512, 256, 128)
                  if M % c == 0 and c * K * 2 <= MM_TILE_BYTES and c * N * out_bytes <= MM_TILE_BYTES)
        b_spec = pl.BlockSpec((bk, bn), lambda i, j, k: (0, 0), pipeline_mode=pl.Buffered(1))
    elif slabs is None and K <= D_FF:
        bk, bm, bn = K, _pick(M, (1024, 512)), _pick(N, (D_FF // 2, 512, 256, 128))
        b_spec = pl.BlockSpec((bk, bn), lambda i, j, k: (k, j))
    else:
        bk = _pick(K, (2048, 1024, 512))
        bm = _pick(M, (D_FF // 2, 1024, 512, 256, 128))
        bn = N // slabs if slabs else _pick(N, (D_FF // 2, 1024, 512, 256, 128))
        b_spec = pl.BlockSpec((bk, bn), lambda i, j, k: (k, j))
    nk = K // bk
    dn = NN
    a_spec = pl.BlockSpec((bm, bk), lambda i, j, k: (i, k))
    if slabs:
        o_spec = pl.BlockSpec((None, bm, bn), lambda i, j, k: (j, i, 0))
        out_shape = jax.ShapeDtypeStruct((slabs, M, bn), out_dtype)
    else:
        o_spec = pl.BlockSpec((bm, bn), lambda i, j, k: (i, j))
        out_shape = jax.ShapeDtypeStruct((M, N), out_dtype)

    def body(a_ref, b_ref, o_ref, acc_ref):
        k = pl.program_id(2)
        part = _dot(a_ref[...], b_ref[...], dn)
        if nk == 1:
            o_ref[...] = part.astype(out_dtype)
        else:
            @pl.when(k == 0)
            def _():
                acc_ref[...] = part

            @pl.when(k > 0)
            def _():
                acc_ref[...] += part

            @pl.when(k == nk - 1)
            def _():
                o_ref[...] = acc_ref[...].astype(out_dtype)

    return pl.pallas_call(
        body, name=name, grid=(M // bm, N // bn, nk),
        in_specs=[a_spec, b_spec], out_specs=o_spec, out_shape=out_shape,
        scratch_shapes=[pltpu.VMEM((bm, bn) if nk > 1 else (8, LANES), F32)],
        compiler_params=_cparams(("parallel", "parallel", "arbitrary")),
    )(a, b)


ROWS = 512


def _row_spec(cols, bm=ROWS):
    return pl.BlockSpec((bm, cols), lambda i: (i, 0))


def _col_spec(rows, bn=ROWS):
    return pl.BlockSpec((rows, bn), lambda i: (0, i))


def _full_spec(shape):
    return pl.BlockSpec(shape, lambda i: (0,) * len(shape))


def _colsum8(t):
    return jnp.sum(t.reshape(t.shape[0] // 8, 8, t.shape[1]), axis=0)


def _proj_rms_fwd(a, w, res, g, name):
    (S, K), D = a.shape, w.shape[1]

    def body(a_ref, w_ref, res_ref, g_ref, x_ref, o_ref, ot_ref):
        xv = _dot(a_ref[...], w_ref[...], NN) + res_ref[...]
        x_ref[...] = xv
        r = lax.rsqrt(jnp.mean(xv * xv, axis=-1, keepdims=True) + EPS)
        y = xv * r * g_ref[...]
        o_ref[...] = y.astype(BF16)
        ot_ref[...] = y.T.astype(BF16)

    return pl.pallas_call(
        body, name=name, grid=(S // ROWS,),
        in_specs=[_row_spec(K), pl.BlockSpec(w.shape, lambda i: (0, 0), pipeline_mode=pl.Buffered(1)),
                  _row_spec(D), _full_spec((1, D))],
        out_specs=[_row_spec(D), _row_spec(D), _col_spec(D)],
        out_shape=[jax.ShapeDtypeStruct((S, D), F32), jax.ShapeDtypeStruct((S, D), BF16),
                   jax.ShapeDtypeStruct((D, S), BF16)],
        compiler_params=_cparams(("parallel",)),
    )(a, w, res, g)


def _rms_fwd(x, g, name, gather=None):
    S, D = x.shape
    nt = S // ROWS
    n_side = len(gather.arrays) if gather else 0

    def body(*refs):
        x_ref, g_ref = refs[:2]
        o_ref, ot_ref = refs[2 + n_side:4 + n_side]
        i = pl.program_id(0)
        if gather:
            start, forward, finish = gather.steps(refs[2:2 + n_side], refs[4 + n_side:4 + 2 * n_side], refs[4 + 2 * n_side:])
            pl.when(i == 0)(start)
            pl.when(i == 3 * nt // 4)(forward)
        xv = x_ref[...]
        r = lax.rsqrt(jnp.mean(xv * xv, axis=-1, keepdims=True) + EPS)
        y = xv * r * g_ref[...]
        o_ref[...] = y.astype(BF16)
        ot_ref[...] = y.T.astype(BF16)
        if gather:
            pl.when(i == nt - 1)(finish)

    side = gather.arrays if gather else []
    return pl.pallas_call(
        body, name=name, grid=(nt,),
        in_specs=[_row_spec(D), _full_spec((1, D))] + [ANY] * n_side,
        out_specs=[_row_spec(D), _col_spec(D)] + [ANY] * n_side,
        out_shape=[jax.ShapeDtypeStruct((S, D), BF16), jax.ShapeDtypeStruct((D, S), BF16)]
        + (gather.out_shape if gather else []),
        scratch_shapes=gather.scratch if gather else [],
        compiler_params=_cparams(("arbitrary",) if gather else ("parallel",)),
    )(x, g, *side)


RMS_BWD_ROWS = 256


def _rms_bwd(pairs, x, g, dres, name, exchange=None):
    S, D = x.shape
    bm = RMS_BWD_ROWS
    nt = S // bm
    n_pairs = len(pairs)
    n_side = len(exchange.arrays) if exchange else 0
    n_in = 2 * n_pairs + 3

    def body(*refs):
        x_ref, g_ref, dres_ref = refs[2 * n_pairs:n_in]
        dx_ref, dxb_ref, dg_ref = refs[n_in + n_side:n_in + n_side + 3]
        acc_ref = refs[n_in + 2 * n_side + 3]
        i = pl.program_id(0)
        if exchange:
            start, finish = exchange.steps(refs[n_in:n_in + n_side], refs[n_in + n_side + 3:n_in + 2 * n_side + 3],
                                           refs[n_in + 2 * n_side + 4:])
            pl.when(i == 0)(start)
        dyv = sum(_dot(refs[2 * p][...], refs[2 * p + 1][...], NN) for p in range(n_pairs))
        xv = x_ref[...]
        r = lax.rsqrt(jnp.mean(xv * xv, axis=-1, keepdims=True) + EPS)
        xr = xv * r
        u = dyv * g_ref[...]
        dx = r * u - xr * (r * r) * jnp.mean(xv * u, axis=-1, keepdims=True) + dres_ref[...]
        dx_ref[...] = dx
        dxb_ref[...] = dx.astype(BF16)
        part = _colsum8(dyv * xr)

        @pl.when(i == 0)
        def _():
            acc_ref[...] = part

        @pl.when(i > 0)
        def _():
            acc_ref[...] += part

        @pl.when(i == nt - 1)
        def _():
            dg_ref[...] = jnp.sum(acc_ref[...], axis=0, keepdims=True)

        if exchange:
            pl.when(i == nt - 1)(finish)

    rows = lambda cols: pl.BlockSpec((bm, cols), lambda i: (i, 0))
    in_specs = []
    for a, b in pairs:
        in_specs += [rows(a.shape[1]), pl.BlockSpec(b.shape, lambda i: (0, 0), pipeline_mode=pl.Buffered(1))]
    side = exchange.arrays if exchange else []
    return pl.pallas_call(
        body, name=name, grid=(nt,),
        in_specs=in_specs + [rows(D), _full_spec((1, D)), rows(D)] + [ANY] * n_side,
        out_specs=[rows(D), rows(D), _full_spec((1, D))] + [ANY] * n_side,
        out_shape=[jax.ShapeDtypeStruct((S, D), F32), jax.ShapeDtypeStruct((S, D), BF16),
                   jax.ShapeDtypeStruct((1, D), F32)] + (exchange.out_shape if exchange else []),
        scratch_shapes=[pltpu.VMEM((8, D), F32)] + (exchange.scratch if exchange else []),
        compiler_params=_cparams(("arbitrary",)),
    )(*[t for pair in pairs for t in pair], x, g, dres, *side)


def _head_mean(t, blockdiag):
    hi, lo = _split_bf16(t)
    return (_dot(hi, blockdiag, NN) + _dot(lo, blockdiag, NN)) * (1.0 / HEAD_DIM)


def _blockdiag():
    r = lax.broadcasted_iota(jnp.int32, (WIDTH, WIDTH), 0) // HEAD_DIM
    c = lax.broadcasted_iota(jnp.int32, (WIDTH, WIDTH), 1) // HEAD_DIM
    return jnp.where(r == c, 1.0, 0.0).astype(BF16)


def _qknorm_fwd(qk, gq, gk, name):
    S = qk.shape[0]

    def body(qk_ref, gq_ref, gk_ref, q_ref, k_ref):
        bd = _blockdiag()
        for part, g_ref, o_ref, scale in ((0, gq_ref, q_ref, HEAD_DIM ** -0.5), (1, gk_ref, k_ref, 1.0)):
            t = qk_ref[:, part * WIDTH:(part + 1) * WIDTH]
            r = lax.rsqrt(_head_mean(t * t, bd) + EPS)
            o_ref[...] = (t * r * g_ref[...] * scale).astype(BF16)

    return pl.pallas_call(
        body, name=name, grid=(S // ROWS,),
        in_specs=[_row_spec(2 * WIDTH), _full_spec((1, WIDTH)), _full_spec((1, WIDTH))],
        out_specs=[_row_spec(WIDTH), _row_spec(WIDTH)],
        out_shape=[jax.ShapeDtypeStruct((S, WIDTH), BF16)] * 2,
        compiler_params=_cparams(("parallel",)),
    )(qk, gq, gk)


def _qknorm_bwd(qk, gq, gk, dqn, dkn, name):
    S = qk.shape[0]
    nt = S // ROWS

    def body(qk_ref, gq_ref, gk_ref, dqn_ref, dkn_ref, dq_ref, dk_ref, dgq_ref, dgk_ref, accq_ref, acck_ref):
        i = pl.program_id(0)
        bd = _blockdiag()
        for part, g_ref, dn_ref, o_ref, dg_ref, acc_ref, scale in (
                (0, gq_ref, dqn_ref, dq_ref, dgq_ref, accq_ref, HEAD_DIM ** -0.5),
                (1, gk_ref, dkn_ref, dk_ref, dgk_ref, acck_ref, 1.0)):
            t = qk_ref[:, part * WIDTH:(part + 1) * WIDTH]
            dn = dn_ref[...] * scale
            r = lax.rsqrt(_head_mean(t * t, bd) + EPS)
            u = dn * g_ref[...]
            dt = r * u - t * (r * r * r) * _head_mean(t * u, bd)
            o_ref[...] = dt.astype(BF16)
            psum = _colsum8(dn * t * r)

            @pl.when(i == 0)
            def _():
                acc_ref[...] = psum

            @pl.when(i > 0)
            def _():
                acc_ref[...] += psum

            @pl.when(i == nt - 1)
            def _():
                dg_ref[...] = jnp.sum(acc_ref[...], axis=0, keepdims=True)

    return pl.pallas_call(
        body, name=name, grid=(nt,),
        in_specs=[_row_spec(2 * WIDTH), _full_spec((1, WIDTH)), _full_spec((1, WIDTH)),
                  _row_spec(WIDTH), _row_spec(WIDTH)],
        out_specs=[_row_spec(WIDTH), _row_spec(WIDTH), _full_spec((1, WIDTH)), _full_spec((1, WIDTH))],
        out_shape=[jax.ShapeDtypeStruct((S, WIDTH), BF16)] * 2 + [jax.ShapeDtypeStruct((1, WIDTH), F32)] * 2,
        scratch_shapes=[pltpu.VMEM((8, WIDTH), F32)] * 2,
        compiler_params=_cparams(("arbitrary",)),
    )(qk, gq, gk, dqn, dkn)


def _gate_specs(D):
    return [pl.BlockSpec((ROWS, D), lambda i: (i, 0)), pl.BlockSpec((ROWS, D), lambda i: (i, 1))]


def _mix_fwd(gates, ya, yb, name):
    S, D = ya.shape

    def body(ga_ref, gb_ref, ya_ref, yb_ref, o_ref, ot_ref):
        ga, gb, yav, ybv = (r[...].astype(F32) for r in (ga_ref, gb_ref, ya_ref, yb_ref))
        m = _sigmoid(ga) * yav + _sigmoid(gb) * ybv
        o_ref[...] = m.astype(BF16)
        ot_ref[...] = m.T.astype(BF16)

    return pl.pallas_call(
        body, name=name, grid=(S // ROWS,),
        in_specs=_gate_specs(D) + [_row_spec(D)] * 2, out_specs=[_row_spec(D), _col_spec(D)],
        out_shape=[jax.ShapeDtypeStruct((S, D), BF16), jax.ShapeDtypeStruct((D, S), BF16)],
        compiler_params=_cparams(("parallel",)),
    )(gates, gates, ya, yb)


def _mix_bwd(dx, w_t, gates, ya, yb, name):
    S, D = ya.shape

    def body(dx_ref, w_ref, ga_ref, gb_ref, ya_ref, yb_ref, dg_ref, dya_ref, dyb_ref):
        dmv = _dot(dx_ref[...], w_ref[...], NN)
        for half, (g_ref, y_ref, dy_ref) in enumerate(((ga_ref, ya_ref, dya_ref), (gb_ref, yb_ref, dyb_ref))):
            s = _sigmoid(g_ref[...].astype(F32))
            dy_ref[...] = (dmv * s).astype(BF16)
            dg_ref[:, half * D:(half + 1) * D] = (dmv * y_ref[...].astype(F32) * s * (1.0 - s)).astype(BF16)

    return pl.pallas_call(
        body, name=name, grid=(S // ROWS,),
        in_specs=[_row_spec(D), pl.BlockSpec(w_t.shape, lambda i: (0, 0), pipeline_mode=pl.Buffered(1))]
        + _gate_specs(D) + [_row_spec(D)] * 2,
        out_specs=[_row_spec(2 * D), _row_spec(D), _row_spec(D)],
        out_shape=[jax.ShapeDtypeStruct((S, 2 * D), BF16)] + [jax.ShapeDtypeStruct((S, D), BF16)] * 2,
        compiler_params=_cparams(("parallel",)),
    )(dx, w_t, gates, gates, ya, yb)


def _down_and_loss(act, w, x2, target, name):
    (S, K), D = act.shape, w.shape[1]
    nt = S // ROWS

    def body(a_ref, w_ref, x_ref, t_ref, dy_ref, dyb_ref, p_ref):
        err = _dot(a_ref[...], w_ref[...], NN) + x_ref[...] - t_ref[...]
        dy = err * (1.0 / D)
        dy_ref[...] = dy
        dyb_ref[...] = dy.astype(BF16)
        sq = _colsum8(err * err)
        acc = sq[:, 0:LANES]
        for k in range(1, D // LANES):
            acc = acc + sq[:, k * LANES:(k + 1) * LANES]
        p_ref[...] = acc

    return pl.pallas_call(
        body, name=name, grid=(nt,),
        in_specs=[_row_spec(K), pl.BlockSpec((K, D), lambda i: (0, 0), pipeline_mode=pl.Buffered(1)),
                  _row_spec(D), _row_spec(D)],
        out_specs=[_row_spec(D), _row_spec(D), pl.BlockSpec((8, LANES), lambda i: (i, 0))],
        out_shape=[jax.ShapeDtypeStruct((S, D), F32), jax.ShapeDtypeStruct((S, D), BF16),
                   jax.ShapeDtypeStruct((nt * 8, LANES), F32)],
        compiler_params=_cparams(("parallel",)),
    )(act, w, x2, target)


CONV_COLS = D_FF // 2
HALO = 16
CONV_CHUNK = 64


def _aligned(start, multiple):
    return start if isinstance(start, int) else pl.multiple_of(start, multiple)


def _conv_taps(xe, cw, cb):
    taps = (pltpu.roll(xe, 2, 0), pltpu.roll(xe, 1, 0), xe)
    return taps, cw[0:1] * taps[0] + cw[1:2] * taps[1] + cw[2:3] * taps[2] + cb


def _conv_specs(nt):
    hb, nb = ROWS // HALO, D_FF // CONV_COLS
    specs = {}
    for part, off in (("gate", 0), ("up", nb)):
        specs[part] = dict(
            main=pl.BlockSpec((ROWS, CONV_COLS), functools.partial(lambda c, i, off: (i, c + off), off=off)),
            prev=pl.BlockSpec((HALO, CONV_COLS),
                              functools.partial(lambda c, i, off: (jnp.maximum(i * hb - 1, 0), c + off), off=off)),
            nxt=pl.BlockSpec((HALO, CONV_COLS),
                             functools.partial(lambda c, i, off: (jnp.minimum((i + 1) * hb, nt * hb - 1), c + off), off=off)),
            w=pl.BlockSpec((3, CONV_COLS), functools.partial(lambda c, i, off: (0, c + off), off=off)),
            b=pl.BlockSpec((1, CONV_COLS), functools.partial(lambda c, i, off: (0, c + off), off=off)))
    return specs


def _convglu_fwd(hid, cw, cb, name):
    S = hid.shape[0]
    sp = _conv_specs(S // ROWS)

    def body(hg_ref, hgp_ref, hu_ref, hup_ref, cwg_ref, cwu_ref, cbg_ref, cbu_ref, o_ref, ot_ref):
        i = pl.program_id(1)
        keep = (i > 0).astype(F32)

        def conv(h_ref, hp_ref, cw_ref, cb_ref):
            xe = jnp.concatenate([hp_ref[...].astype(F32) * keep, h_ref[...].astype(F32)], axis=0)
            return _conv_taps(xe, cw_ref[...], cb_ref[...])[1][HALO:, :]

        gate = conv(hg_ref, hgp_ref, cwg_ref, cbg_ref)
        up = conv(hu_ref, hup_ref, cwu_ref, cbu_ref)
        act = gate * _sigmoid(gate) * up
        o_ref[...] = act.astype(BF16)
        ot_ref[...] = act.T.astype(BF16)

    g, u = sp["gate"], sp["up"]
    return pl.pallas_call(
        body, name=name, grid=(D_FF // CONV_COLS, S // ROWS),
        in_specs=[g["main"], g["prev"], u["main"], u["prev"], g["w"], u["w"], g["b"], u["b"]],
        out_specs=[g["main"], pl.BlockSpec((CONV_COLS, ROWS), lambda c, i: (c, i))],
        out_shape=[jax.ShapeDtypeStruct((S, D_FF), BF16), jax.ShapeDtypeStruct((D_FF, S), BF16)],
        compiler_params=_cparams(("parallel", "parallel")),
    )(hid, hid, hid, hid, cw, cw, cb, cb)


def _convglu_bwd(hid, dact, cw, cb, name):
    S = hid.shape[0]
    nt = S // ROWS
    sp = _conv_specs(nt)

    n_chunks = ROWS // CONV_CHUNK

    def body(hg_ref, hgp_ref, hgn_ref, hu_ref, hup_ref, hun_ref, da_ref, dan_ref,
             cwg_ref, cwu_ref, cbg_ref, cbu_ref,
             dhg_ref, dhu_ref, dcwg_ref, dcwu_ref, dcbg_ref, dcbu_ref, xg_s, xu_s, da_s):
        i = pl.program_id(1)
        kp = (i > 0).astype(F32)
        kn = (i < nt - 1).astype(F32)
        for x_s, h_ref, hp_ref, hn_ref in ((xg_s, hg_ref, hgp_ref, hgn_ref), (xu_s, hu_ref, hup_ref, hun_ref)):
            x_s[0:HALO, :] = hp_ref[...].astype(F32) * kp
            x_s[HALO:HALO + ROWS, :] = h_ref[...].astype(F32)
            x_s[HALO + ROWS:, :] = hn_ref[...].astype(F32) * kn
        da_s[0:ROWS, :] = da_ref[...].astype(F32)
        da_s[ROWS:, :] = dan_ref[...].astype(F32) * kn

        @pl.when(i == 0)
        def _():
            for ref in (dcwg_ref, dcwu_ref, dcbg_ref, dcbu_ref):
                ref[...] = jnp.zeros_like(ref)

        def lane_group(grp, _):
            lanes = pl.ds(pl.multiple_of(grp * LANES, LANES), LANES)
            cwg, cwu, cbg, cbu = cwg_ref[:, lanes], cwu_ref[:, lanes], cbg_ref[:, lanes], cbu_ref[:, lanes]

            def grads(r0, n):
                rows = pl.ds(_aligned(r0 + HALO - 8, 8), n + 8)
                taps_g, gate = _conv_taps(xg_s[rows, lanes], cwg, cbg)
                taps_u, up = _conv_taps(xu_s[rows, lanes], cwu, cbu)
                gate, up = gate[8:], up[8:]
                da = da_s[pl.ds(_aligned(r0, 8), n), lanes]
                sg = _sigmoid(gate)
                return (da * up * sg * (1.0 + gate * (1.0 - sg)), da * gate * sg,
                        [t[8:] for t in taps_g], [t[8:] for t in taps_u])

            def chunk(step, carry):
                below_g, below_u, accs = carry
                r0 = (n_chunks - 1 - step) * CONV_CHUNK
                dg, du, taps_g, taps_u = grads(r0, CONV_CHUNK)
                new_accs = []
                for d, below, cwv, taps, dh_ref, acc in ((dg, below_g, cwg, taps_g, dhg_ref, accs[0]),
                                                        (du, below_u, cwu, taps_u, dhu_ref, accs[1])):
                    ext = jnp.concatenate([d, below], axis=0)
                    n_ext = CONV_CHUNK + 8
                    dh = (cwv[2:3] * d + cwv[1:2] * pltpu.roll(ext, n_ext - 1, 0)[:CONV_CHUNK]
                          + cwv[0:1] * pltpu.roll(ext, n_ext - 2, 0)[:CONV_CHUNK])
                    dh_ref[pl.ds(_aligned(r0, CONV_CHUNK), CONV_CHUNK), lanes] = dh.astype(BF16)
                    new_accs.append(tuple(a + _colsum8(d * tap) for a, tap in zip(acc[:3], taps))
                                    + (acc[3] + _colsum8(d),))
                return dg[0:8], du[0:8], tuple(new_accs)

            below_g, below_u, _, _ = grads(ROWS, 8)
            zero = jnp.zeros((8, LANES), F32)
            _, _, accs = lax.fori_loop(0, n_chunks, chunk, (below_g, below_u, ((zero,) * 4, (zero,) * 4)))
            for acc, dcw_ref, dcb_ref in ((accs[0], dcwg_ref, dcbg_ref), (accs[1], dcwu_ref, dcbu_ref)):
                for t in range(3):
                    dcw_ref[t:t + 1, lanes] += jnp.sum(acc[t], axis=0, keepdims=True)
                dcb_ref[:, lanes] += jnp.sum(acc[3], axis=0, keepdims=True)
            return 0

        lax.fori_loop(0, CONV_COLS // LANES, lane_group, 0)

    g, u = sp["gate"], sp["up"]
    return pl.pallas_call(
        body, name=name, grid=(D_FF // CONV_COLS, nt),
        in_specs=[g["main"], g["prev"], g["nxt"], u["main"], u["prev"], u["nxt"], g["main"], g["nxt"],
                  g["w"], u["w"], g["b"], u["b"]],
        out_specs=[g["main"], g["main"], g["w"], g["w"], g["b"], g["b"]],
        out_shape=[jax.ShapeDtypeStruct((S, D_FF), BF16)] * 2 + [jax.ShapeDtypeStruct((3, D_FF), F32)] * 2
        + [jax.ShapeDtypeStruct((1, D_FF), F32)] * 2,
        scratch_shapes=[pltpu.VMEM((ROWS + 2 * HALO, CONV_COLS), F32)] * 2 + [pltpu.VMEM((ROWS + HALO, CONV_COLS), F32)],
        compiler_params=_cparams(("parallel", "arbitrary")),
    )(hid, hid, hid, hid, hid, hid, dact, dact, cw, cw, cb, cb)


REL_PAD = 384
DIAG = 1024


def _band_valid():
    qc = lax.broadcasted_iota(jnp.int32, (BQ, KWIN), 0) // CHUNK
    kc = lax.broadcasted_iota(jnp.int32, (BQ, KWIN), 1) // CHUNK - LEFT_CHUNKS
    return (kc <= qc) & (kc >= qc - LEFT_CHUNKS)


def _rel_index(offset):
    return jnp.clip(BAND - offset, -MAX_REL, MAX_REL) + MAX_REL


def _split3(x):
    hi = x.astype(BF16)
    rest = x - hi.astype(F32)
    mid = rest.astype(BF16)
    return hi, mid, (rest - mid.astype(F32)).astype(BF16)


def _bias_table(rel_bias, name):
    def body(rb_ref, o_ref):
        t = lax.broadcasted_iota(jnp.int32, (REL_PAD, DIAG), 0)
        lane = lax.broadcasted_iota(jnp.int32, (REL_PAD, DIAG), 1)
        pick = jnp.where(t == _rel_index(lane - BQ), 1.0, 0.0).astype(BF16)
        base = sum(_dot(piece, pick, NN) for piece in _split3(rb_ref[...]))
        valid = _band_valid()
        for h in range(N_HEADS):
            rows = jnp.broadcast_to(base[h:h + 1], (BQ, DIAG))
            rolled = pltpu.roll(rows, 0, 1, stride=1, stride_axis=0)
            o_ref[h] = jnp.where(valid, rolled[:, BQ:], NEG)

    return pl.pallas_call(
        body, name=name,
        out_shape=jax.ShapeDtypeStruct((N_HEADS, BQ, KWIN), F32),
        compiler_params=_cparams(),
    )(rel_bias)


def _bias_table_bwd(dtab, name):
    def body(d_ref, o_ref, diag_ref):
        r = lax.broadcasted_iota(jnp.int32, (BQ, BQ), 0)
        c = lax.broadcasted_iota(jnp.int32, (BQ, BQ), 1)
        flip = jnp.where(r + c == BQ - 1, 1.0, 0.0).astype(BF16)
        for h in range(N_HEADS):
            flipped = sum(_dot(flip, piece, NN) for piece in _split3(d_ref[h]))
            padded = jnp.concatenate([flipped, jnp.zeros((BQ, DIAG - KWIN), F32)], axis=1)
            rolled = pltpu.roll(padded, DIAG - (BQ - 1), 1, stride=1, stride_axis=0)
            diag_ref[h:h + 1, :] = jnp.sum(rolled, axis=0, keepdims=True)
        lane = lax.broadcasted_iota(jnp.int32, (DIAG, REL_PAD), 0)
        t = lax.broadcasted_iota(jnp.int32, (DIAG, REL_PAD), 1)
        offset = jnp.where(lane < KWIN, lane, lane - DIAG)
        pick = jnp.where(t == _rel_index(offset), 1.0, 0.0).astype(BF16)
        o_ref[...] = sum(_dot(piece, pick, NN) for piece in _split3(diag_ref[...]))

    return pl.pallas_call(
        body, name=name,
        out_shape=jax.ShapeDtypeStruct((N_HEADS, REL_PAD), F32),
        scratch_shapes=[pltpu.VMEM((N_HEADS, DIAG), F32)],
        compiler_params=_cparams(),
    )(dtab)


def _head_masks(heads=2):
    lane = lax.broadcasted_iota(jnp.int32, (1, heads * HEAD_DIM), 1)
    return [lane // HEAD_DIM == h for h in range(heads)]


def _own_lanes(masks, vals):
    out = vals[-1]
    for m, val in zip(masks[-2::-1], vals[-2::-1]):
        out = jnp.where(m, val, out)
    return out


CA_HEADS = 4
CA_LANES = CA_HEADS * HEAD_DIM


def _ca_window_specs(nq, col_off=0):
    return [pl.BlockSpec((BQ, CA_LANES), functools.partial(
        lambda p, i, d: (jnp.clip(i - 2 + d, 0, nq - 1), p + col_off), d=d)) for d in range(3)]


def _softmax_rows(s):
    p = jnp.exp(s - jnp.max(s, axis=-1, keepdims=True))
    return p, jnp.sum(p, axis=-1, keepdims=True)


def _ca_scores(qm, kc, tab_h, i):
    col = lax.broadcasted_iota(jnp.int32, (1, KWIN), 1)
    in_seq = col + (i - 2) * BQ >= 0
    return jnp.where(in_seq, _dot(qm, kc, NT) + tab_h, NEG)


def _ca_fwd(qn, kn, v, tab, name, v_off=0, gather=None):
    S = qn.shape[0]
    nq = S // BQ
    groups = WIDTH // CA_LANES
    qspec = pl.BlockSpec((BQ, CA_LANES), lambda p, i: (i, p))
    tspec = pl.BlockSpec((CA_HEADS, BQ, KWIN), lambda p, i: (p, 0, 0))
    n_side = len(gather.arrays) if gather else 0

    def body(*refs):
        q_ref, k0, k1, k2, v0, v1, v2, tab_ref = refs[:8]
        o_ref, ot_ref = refs[8 + n_side:10 + n_side]
        p, i = pl.program_id(0), pl.program_id(1)
        if gather:
            start, forward, finish = gather.steps(refs[8:8 + n_side], refs[10 + n_side:10 + 2 * n_side],
                                                  refs[10 + 2 * n_side:])
            pl.when((p == 0) & (i == 0))(start)
            pl.when((p == groups - 1) & (i == nq // 2))(forward)
        kc = jnp.concatenate([k0[...], k1[...], k2[...]], axis=0)
        vc = jnp.concatenate([v0[...], v1[...], v2[...]], axis=0)
        qv = q_ref[...]
        masks = _head_masks(CA_HEADS)
        heads = range(CA_HEADS)
        s = [_ca_scores(jnp.where(masks[h], qv, 0), kc, tab_ref[h], i) for h in heads]
        soft = [_softmax_rows(s[h]) for h in heads]
        o = [_dot(soft[h][0].astype(BF16), vc, NN) / soft[h][1] for h in heads]
        out = _own_lanes(masks, o)
        o_ref[...] = out.astype(BF16)
        ot_ref[...] = out.T.astype(BF16)
        if gather:
            pl.when((p == groups - 1) & (i == nq - 1))(finish)

    side = gather.arrays if gather else []
    return pl.pallas_call(
        body, name=name, grid=(groups, nq),
        in_specs=[qspec] + _ca_window_specs(nq) + _ca_window_specs(nq, v_off) + [tspec] + [ANY] * n_side,
        out_specs=[qspec, pl.BlockSpec((CA_LANES, BQ), lambda p, i: (p, i))] + [ANY] * n_side,
        out_shape=[jax.ShapeDtypeStruct((S, WIDTH), BF16), jax.ShapeDtypeStruct((WIDTH, S), BF16)]
        + (gather.out_shape if gather else []),
        scratch_shapes=gather.scratch if gather else [],
        compiler_params=_cparams(("arbitrary", "arbitrary") if gather else ("parallel", "parallel")),
    )(qn, kn, kn, kn, v, v, v, tab, *side)


def _ca_bwd(qn, kn, v, do, tab, name, v_off=0):
    S = qn.shape[0]
    nq = S // BQ
    qspec = pl.BlockSpec((BQ, CA_LANES), lambda p, i: (jnp.minimum(i, nq - 1), p))
    kout = pl.BlockSpec((BQ, CA_LANES), lambda p, i: (jnp.clip(i - 2, 0, nq - 1), p))
    tspec = pl.BlockSpec((CA_HEADS, BQ, KWIN), lambda p, i: (p, 0, 0))

    def body(q_ref, do_ref, k0, k1, k2, v0, v1, v2, tab_ref,
             dq_ref, dk_ref, dv_ref, dtab_ref, dk_acc, dv_acc):
        i = pl.program_id(1)

        @pl.when(i == 0)
        def _():
            dk_acc[...] = jnp.zeros_like(dk_acc)
            dv_acc[...] = jnp.zeros_like(dv_acc)
            dtab_ref[...] = jnp.zeros_like(dtab_ref)

        @pl.when(i < nq)
        def _():
            kc = jnp.concatenate([k0[...], k1[...], k2[...]], axis=0)
            vc = jnp.concatenate([v0[...], v1[...], v2[...]], axis=0)
            qv, dov = q_ref[...], do_ref[...]
            masks = _head_masks(CA_HEADS)
            heads = range(CA_HEADS)
            qm = [jnp.where(masks[h], qv, 0) for h in heads]
            dom = [jnp.where(masks[h], dov, 0) for h in heads]
            s = [_ca_scores(qm[h], kc, tab_ref[h], i) for h in heads]
            dp = [_dot(dom[h], vc, NT) for h in heads]
            soft = [_softmax_rows(s[h]) for h in heads]
            p = [soft[h][0] / soft[h][1] for h in heads]
            ds = [p[h] * (dp[h] - jnp.sum(p[h] * dp[h], axis=-1, keepdims=True)) for h in heads]
            for h in heads:
                dtab_ref[h] += ds[h]
            dsb = [ds[h].astype(BF16) for h in heads]
            pb = [p[h].astype(BF16) for h in heads]
            dq = [_dot(dsb[h], kc, NN) for h in heads]
            dq_ref[...] = _own_lanes(masks, dq)
            dkc = sum(_dot(dsb[h], qm[h], TN) for h in heads)
            dvc = sum(_dot(pb[h], dom[h], TN) for h in heads)
            for d in range(3):
                slot = (i + 1 + d) % 3
                dk_acc[slot] += dkc[d * BQ:(d + 1) * BQ]
                dv_acc[slot] += dvc[d * BQ:(d + 1) * BQ]

        @pl.when(i >= 2)
        def _():
            slot = (i + 1) % 3
            dk_ref[...] = dk_acc[slot]
            dv_ref[...] = dv_acc[slot].astype(BF16)
            dk_acc[slot] = jnp.zeros((BQ, CA_LANES), F32)
            dv_acc[slot] = jnp.zeros((BQ, CA_LANES), F32)

    return pl.pallas_call(
        body, name=name, grid=(WIDTH // CA_LANES, nq + 2),
        in_specs=[qspec, qspec] + _ca_window_specs(nq) + _ca_window_specs(nq, v_off) + [tspec],
        out_specs=[qspec, kout, kout, tspec],
        out_shape=[jax.ShapeDtypeStruct((S, WIDTH), F32), jax.ShapeDtypeStruct((S, WIDTH), F32),
                   jax.ShapeDtypeStruct((S, WIDTH), BF16), jax.ShapeDtypeStruct((N_HEADS, BQ, KWIN), F32)],
        scratch_shapes=[pltpu.VMEM((3, BQ, CA_LANES), F32)] * 2,
        compiler_params=_cparams(("parallel", "arbitrary")),
    )(qn, do, kn, kn, kn, v, v, v, tab)


def _sb_consts():
    r = lax.broadcasted_iota(jnp.int32, (BQ, BQ), 0)
    c = lax.broadcasted_iota(jnp.int32, (BQ, BQ), 1)
    from_s = jnp.where(r >= c, 1.0, 0.0).astype(BF16)
    causal = c < r
    return from_s, causal


def _suffix_sum(t, from_s):
    hi, lo = _split_bf16(t)
    return _dot(hi, from_s, NN) + _dot(lo, from_s, NN)


def _neg_abs(x):
    bits = lax.bitcast_convert_type(x, jnp.uint32) | jnp.uint32(0x80000000)
    return lax.bitcast_convert_type(bits, F32)


def _sb_log_keep(zn):
    return jnp.minimum(zn, 0.0) - jnp.log(1.0 + jnp.exp(_neg_abs(zn)))


SB_DEAD = 105.0


SB_QB = 2


def _sb_walk(ip, tiles, keep_ref):
    i0 = SB_QB * ip

    @pl.when(ip == 0)
    def _():
        tiles([(0, [0], [True]), (1, [1, 0], [True, False])])

    @pl.when(ip > 0)
    def _():
        tiles([(a, [i0 + a, i0 + a - 1], [True, False]) for a in range(SB_QB)])

    for a in range(SB_QB):
        def alive(a=a):
            return (jnp.max(keep_ref[2 * a:2 * a + 2]) > -SB_DEAD).astype(jnp.int32)

        def step(state, a=a, alive=alive):
            j, _ = state
            tiles([(a, [j], [False])])
            return j - 1, alive()

        lax.while_loop(lambda state: (state[0] >= 0) & (state[1] > 0), step, (i0 + a - 2, alive()))


def _sb_rows(j):
    return pl.ds(pl.multiple_of(j * BQ, BQ), BQ)


def _sb_chains(groups):
    chains = [(a, n, h) for a, js, _ in groups for n in range(len(js)) for h in range(2)]
    block = {(a, n): j for a, js, _ in groups for n, j in enumerate(js)}
    masked = [(a, n, h) for a, _, diags in groups for n, d in enumerate(diags) if d for h in range(2)]
    return chains, block, masked


def _sb_running(ref, vals, groups):
    before_chain = {}
    for a, js, _ in groups:
        for h in range(2):
            run = ref[2 * a + h]
            for n in range(len(js)):
                before_chain[(a, n, h)] = run
                run = run + jnp.sum(vals[(a, n, h)], axis=-1, keepdims=True)
            ref[2 * a + h] = run
    return before_chain


def _sb_specs(S, offs):
    def qspec(off=0):
        return pl.BlockSpec((SB_QB * BQ, PAIR), lambda p, i: (i, p + off))

    def kspec(off=0):
        return pl.BlockSpec((S, PAIR), lambda p, i: (0, p + off), pipeline_mode=pl.Buffered(1))

    return qspec, kspec, [qspec(offs[0]), kspec(offs[1]), kspec(offs[2])]


def _sb_fwd(q, k, v, name, offs=(0, 0, 0)):
    S = q.shape[0]
    steps = S // (SB_QB * BQ)
    qspec, _, qkv_specs = _sb_specs(S, offs)

    def body(q_ref, k_ref, v_ref, o_ref, of_ref, ot_ref, carry_ref, acc_ref):
        ip = pl.program_id(1)
        from_s, causal = _sb_consts()
        masks = _head_masks()
        qn = q_ref[...] * -(HEAD_DIM ** -0.5)
        qms = {(a, h): jnp.where(masks[h], qn[a * BQ:(a + 1) * BQ], 0) for a in range(SB_QB) for h in range(2)}
        carry_ref[...] = jnp.zeros_like(carry_ref)
        acc_ref[...] = jnp.zeros_like(acc_ref)

        def tiles(groups):
            chains, block, masked = _sb_chains(groups)
            kbs = {an: k_ref[_sb_rows(j), :] for an, j in block.items()}
            vbs = {an: v_ref[_sb_rows(j), :] for an, j in block.items()}
            zn = {c: _dot(qms[(c[0], c[2])], kbs[c[:2]], NT) for c in chains}
            log_keep = {c: _sb_log_keep(zn[c]) for c in chains}
            for c in masked:
                log_keep[c] = jnp.where(causal, log_keep[c], 0.0)
            split = {c: _split_bf16(log_keep[c]) for c in chains}
            carry = _sb_running(carry_ref, log_keep, groups)
            suffix = {c: _dot(split[c][0], from_s, NN) + _dot(split[c][1], from_s, NN) for c in chains}
            w = {c: jnp.exp(carry[c] + suffix[c] - zn[c]) for c in chains}
            for c in masked:
                w[c] = jnp.where(causal, w[c], 0.0)
            for c in chains:
                acc_ref[2 * c[0] + c[2]] += _dot(w[c].astype(BF16), vbs[c[:2]], NN)

        _sb_walk(ip, tiles, carry_ref)
        for a in range(SB_QB):
            out = jnp.where(masks[0], acc_ref[2 * a], acc_ref[2 * a + 1])
            o_ref[a * BQ:(a + 1) * BQ, :] = out.astype(BF16)
            of_ref[a * BQ:(a + 1) * BQ, :] = out
            ot_ref[:, a * BQ:(a + 1) * BQ] = out.T.astype(BF16)

    return pl.pallas_call(
        body, name=name, grid=(WIDTH // PAIR, steps),
        in_specs=qkv_specs, out_specs=[qspec(), qspec(), pl.BlockSpec((PAIR, SB_QB * BQ), lambda p, i: (p, i))],
        out_shape=[jax.ShapeDtypeStruct((S, WIDTH), BF16), jax.ShapeDtypeStruct((S, WIDTH), F32),
                   jax.ShapeDtypeStruct((WIDTH, S), BF16)],
        scratch_shapes=[pltpu.VMEM((2 * SB_QB, BQ, 1), F32), pltpu.VMEM((2 * SB_QB, BQ, PAIR), F32)],
        compiler_params=_cparams(("parallel", "arbitrary")),
    )(q, k, v)


def _sb_bwd(q, k, v, o, do, name, offs=(0, 0, 0), exchange=None):
    S = q.shape[0]
    steps = S // (SB_QB * BQ)
    pairs = WIDTH // PAIR
    qspec, kspec, qkv_specs = _sb_specs(S, offs)
    n_side = len(exchange.arrays) if exchange else 0

    def body(*refs):
        q_ref, o_ref, do_ref, k_ref, v_ref = refs[:5]
        dq_ref, dk_ref, dv_ref = refs[5 + n_side:8 + n_side]
        dk_acc, dv_acc, keep_ref, gsum_ref, dq_acc = refs[8 + 2 * n_side:13 + 2 * n_side]
        ip = pl.program_id(1)
        if exchange:
            start, finish = exchange.steps(refs[5:5 + n_side], refs[8 + n_side:8 + 2 * n_side], refs[13 + 2 * n_side:])
            pl.when((pl.program_id(0) == 0) & (ip == 0))(start)

        @pl.when(ip == 0)
        def _():
            dk_acc[...] = jnp.zeros_like(dk_acc)
            dv_acc[...] = jnp.zeros_like(dv_acc)

        from_s, causal = _sb_consts()
        masks = _head_masks()
        qn, dov = q_ref[...] * -(HEAD_DIM ** -0.5), do_ref[...]
        od = o_ref[...] * dov.astype(F32)
        lanes = [(a, h) for a in range(SB_QB) for h in range(2)]
        rows_of = {a: slice(a * BQ, (a + 1) * BQ) for a in range(SB_QB)}
        qms = {(a, h): jnp.where(masks[h], qn[rows_of[a]], 0) for a, h in lanes}
        doms = {(a, h): jnp.where(masks[h], dov[rows_of[a]], 0) for a, h in lanes}
        totals = {(a, h): jnp.sum(jnp.where(masks[h], od[rows_of[a]], 0.0), axis=-1, keepdims=True)
                  for a, h in lanes}
        for ref in (keep_ref, gsum_ref, dq_acc):
            ref[...] = jnp.zeros_like(ref)

        def tiles(groups):
            chains, block, masked = _sb_chains(groups)
            kbs = {an: k_ref[_sb_rows(j), :] for an, j in block.items()}
            vbs = {an: v_ref[_sb_rows(j), :] for an, j in block.items()}
            zn = {c: _dot(qms[(c[0], c[2])], kbs[c[:2]], NT) for c in chains}
            dw = {c: _dot(doms[(c[0], c[2])], vbs[c[:2]], NT) for c in chains}
            log_keep = {c: _sb_log_keep(zn[c]) for c in chains}
            for c in masked:
                log_keep[c] = jnp.where(causal, log_keep[c], 0.0)
            split = {c: _split_bf16(log_keep[c]) for c in chains}
            kept = _sb_running(keep_ref, log_keep, groups)
            suffix = {c: _dot(split[c][0], from_s, NN) + _dot(split[c][1], from_s, NN) for c in chains}
            w = {c: jnp.exp(kept[c] + suffix[c] - zn[c]) for c in chains}
            for c in masked:
                w[c] = jnp.where(causal, w[c], 0.0)
            wb = {c: w[c].astype(BF16) for c in chains}
            g = {c: wb[c].astype(F32) * dw[c] for c in chains}
            gsplit = {c: _split_bf16(g[c]) for c in chains}
            gsum = _sb_running(gsum_ref, g, groups)
            gsuffix = {c: _dot(gsplit[c][0], from_s, NN) + _dot(gsplit[c][1], from_s, NN) for c in chains}
            dzb = {}
            for c in chains:
                before = totals[(c[0], c[2])] - (gsum[c] + gsuffix[c])
                dz = (g[c] + before) * jnp.exp(log_keep[c]) - before
                if c in masked:
                    dz = jnp.where(causal, dz, 0.0)
                dzb[c] = dz.astype(BF16)
            for c in chains:
                rows = _sb_rows(block[c[:2]])
                dq_acc[2 * c[0] + c[2]] += _dot(dzb[c], kbs[c[:2]], NN)
                dk_acc[rows, :] -= _dot(dzb[c], qms[(c[0], c[2])], TN)
                dv_acc[rows, :] += _dot(wb[c], doms[(c[0], c[2])], TN)

        _sb_walk(ip, tiles, keep_ref)
        for a in range(SB_QB):
            dq = jnp.where(masks[0], dq_acc[2 * a], dq_acc[2 * a + 1])
            dq_ref[a * BQ:(a + 1) * BQ, :] = (dq * HEAD_DIM ** -0.5).astype(BF16)

        @pl.when(ip == steps - 1)
        def _():
            dk_ref[...] = dk_acc[...].astype(BF16)
            dv_ref[...] = dv_acc[...].astype(BF16)

        if exchange:
            pl.when((pl.program_id(0) == pairs - 1) & (ip == steps - 1))(finish)

    side = exchange.arrays if exchange else []
    return pl.pallas_call(
        body, name=name, grid=(pairs, steps),
        in_specs=[qkv_specs[0], qspec(), qspec(), qkv_specs[1], qkv_specs[2]] + [ANY] * n_side,
        out_specs=[qspec(), kspec(), kspec()] + [ANY] * n_side,
        out_shape=[jax.ShapeDtypeStruct((S, WIDTH), BF16)] * 3 + (exchange.out_shape if exchange else []),
        scratch_shapes=[pltpu.VMEM((S, PAIR), F32)] * 2 + [pltpu.VMEM((2 * SB_QB, BQ, 1), F32)] * 2
        + [pltpu.VMEM((2 * SB_QB, BQ, PAIR), F32)] + (exchange.scratch if exchange else []),
        compiler_params=_cparams(("arbitrary", "arbitrary") if exchange else ("parallel", "arbitrary")),
    )(q, o, do, k, v, *side)


ANY = pl.BlockSpec(memory_space=pl.ANY)


def _place():
    return lax.axis_index("x"), lax.axis_index("y"), lax.axis_index("c")


def _other_chips(x, y):
    return [(2 * px + py, (px, py)) for px, py in ((1 - x, y), (x, 1 - y), (1 - x, 1 - y))]


def _remote(src, dst, sems, k, to):
    return pltpu.make_async_remote_copy(src_ref=src, dst_ref=dst, send_sem=sems[0].at[k], recv_sem=sems[1].at[k],
                                        device_id=to, device_id_type=MESH)


class _Gather:
    def __init__(self, ws, extras=()):
        self.n, self.m = len(ws), len(extras)
        self.arrays = list(ws) + list(extras)
        self.n_copies = 6 * self.n + 3 * self.m
        self.out_shape = [jax.ShapeDtypeStruct((N_CHIPS,) + a.shape, a.dtype) for a in self.arrays]
        self.scratch = [pltpu.SemaphoreType.DMA((self.n_copies,)), pltpu.SemaphoreType.DMA((self.n_copies,))]

    def steps(self, in_refs, out_refs, sems):
        n = self.n
        x, y, c = _place()
        me = 2 * x + y
        chips = _other_chips(x, y)
        sibling = (x, y, 1 - c)

        def halves(ref):
            rh = ref.shape[-2] // 2
            return pl.ds(c * rh, rh), pl.ds((1 - c) * rh, rh)

        def first():
            cps = [_remote(w_ref.at[halves(w_ref)[0]], o_ref.at[me, halves(w_ref)[0]], sems, 6 * a + k, (*xy, c))
                   for a, (w_ref, o_ref) in enumerate(zip(in_refs[:n], out_refs[:n])) for k, (_, xy) in enumerate(chips)]
            return cps + [_remote(e_ref, eo_ref.at[me], sems, 6 * n + 3 * b + k, (*xy, c))
                          for b, (e_ref, eo_ref) in enumerate(zip(in_refs[n:], out_refs[n:]))
                          for k, (_, xy) in enumerate(chips)]

        def passed():
            return [_remote(o_ref.at[chip, halves(o_ref)[0]], o_ref.at[chip, halves(o_ref)[0]], sems, 6 * a + 3 + k, sibling)
                    for a, o_ref in enumerate(out_refs[:n]) for k, (chip, _) in enumerate(chips)]

        def start():
            for cp in first():
                cp.start()

        def forward():
            for a, o_ref in enumerate(out_refs[:n]):
                for k, (chip, xy) in enumerate(chips):
                    landed = o_ref.at[chip, halves(o_ref)[0]]
                    _remote(landed, landed, sems, 6 * a + k, (*xy, c)).wait_recv()
            for cp in passed():
                cp.start()

        def finish():
            for a, o_ref in enumerate(out_refs[:n]):
                for k, (chip, _) in enumerate(chips):
                    landed = o_ref.at[chip, halves(o_ref)[1]]
                    _remote(landed, landed, sems, 6 * a + 3 + k, sibling).wait_recv()
            for b, (e_ref, eo_ref) in enumerate(zip(in_refs[n:], out_refs[n:])):
                for k, (chip, xy) in enumerate(chips):
                    _remote(e_ref, eo_ref.at[chip], sems, 6 * n + 3 * b + k, (*xy, c)).wait_recv()
            for cp in first() + passed():
                cp.wait_send()

        return start, forward, finish


class _ChipExchange:
    def __init__(self, ps):
        self.arrays = list(ps)
        self.out_shape = [jax.ShapeDtypeStruct(p.shape, p.dtype) for p in ps]
        self.scratch = [pltpu.SemaphoreType.DMA((3 * len(ps),)), pltpu.SemaphoreType.DMA((3 * len(ps),))]

    def steps(self, p_refs, out_refs, sems):
        x, y, c = _place()
        me = 2 * x + y
        chips = _other_chips(x, y)

        def copies():
            return [_remote(p_ref.at[chip], o_ref.at[me], sems, 3 * a + k, (*xy, c))
                    for a, (p_ref, o_ref) in enumerate(zip(p_refs, out_refs)) for k, (chip, xy) in enumerate(chips)]

        def start():
            for cp in copies():
                cp.start()

        def finish():
            for a, (p_ref, o_ref) in enumerate(zip(p_refs, out_refs)):
                for k, (chip, xy) in enumerate(chips):
                    _remote(p_ref.at[chip], o_ref.at[chip], sems, 3 * a + k, (*xy, c)).wait_recv()
            for cp in copies():
                cp.wait_send()

        return start, finish


def _exchange_cores(gs, name, small=None):
    n = len(gs)
    m = 0 if small is None else 1

    def body(*refs):
        g_refs, sib_refs = refs[:n], refs[n + m:2 * n + m]
        sems = refs[2 * (n + m):]
        x, y, c = _place()
        me = 4 * x + 2 * y + c
        copies = []
        for a, (g_ref, sib_ref) in enumerate(zip(g_refs, sib_refs)):
            rh = g_ref.shape[1] // 2
            copies.append(_remote(g_ref.at[:, pl.ds((1 - c) * rh, rh), :], sib_ref, sems, a, (x, y, 1 - c)))
        if m:
            small_ref, all_ref = refs[n], refs[2 * n + m]
            k = n
            for fx in (0, 1):
                for fy in (0, 1):
                    for fc in (0, 1):
                        if fx or fy or fc:
                            to = (1 - x if fx else x, 1 - y if fy else y, 1 - c if fc else c)
                            copies.append(_remote(small_ref, all_ref.at[me], sems, k, to))
                            k += 1
        for cp in copies:
            cp.start()
        for cp in copies:
            cp.wait_recv()
        for cp in copies:
            cp.wait_send()

    n_copies = n + m * (N_DEV - 1)
    args = list(gs) + ([small] if m else [])
    return pl.pallas_call(
        body, name=name, in_specs=[ANY] * (n + m), out_specs=[ANY] * (n + m),
        out_shape=[jax.ShapeDtypeStruct((N_CHIPS, g.shape[1] // 2, g.shape[2]), F32) for g in gs]
        + ([jax.ShapeDtypeStruct((N_DEV,) + small.shape, F32)] if m else []),
        scratch_shapes=[pltpu.SemaphoreType.DMA((n_copies,)), pltpu.SemaphoreType.DMA((n_copies,))],
    )(*args)


def _share_halves(ghs, name):
    n = len(ghs)

    def body(*refs):
        gh_refs, out_refs, sems = refs[:n], refs[n:2 * n], refs[2 * n:]
        x, y, c = _place()
        copies = [_remote(gh_ref, o_ref, sems, a, (x, y, 1 - c)) for a, (gh_ref, o_ref) in enumerate(zip(gh_refs, out_refs))]
        for cp in copies:
            cp.start()
        for cp in copies:
            cp.wait_recv()
        for cp in copies:
            cp.wait_send()

    return pl.pallas_call(
        body, name=name, in_specs=[ANY] * n, out_specs=[ANY] * n,
        out_shape=[jax.ShapeDtypeStruct(g.shape, g.dtype) for g in ghs],
        scratch_shapes=[pltpu.SemaphoreType.DMA((n,)), pltpu.SemaphoreType.DMA((n,))],
    )(*ghs)


EW_BLOCK_BYTES = 2 * 1024 * 1024


def _row_block(rows, cols, mult=8):
    fits = [b for b in range(mult, rows + 1, mult) if rows % b == 0 and b * cols * 4 <= EW_BLOCK_BYTES]
    return max(fits) if fits else mult


def _add2(a, b, name):
    R, C = a.shape
    rows = _row_block(R, C, mult=16)
    spec = pl.BlockSpec((rows, C), lambda i: (i, 0))

    def body(a_ref, b_ref, o_ref):
        o_ref[...] = (a_ref[...] + b_ref[...]).astype(BF16)

    return pl.pallas_call(
        body, name=name, grid=(R // rows,), in_specs=[spec, spec], out_specs=spec,
        out_shape=jax.ShapeDtypeStruct(a.shape, BF16),
        compiler_params=_cparams(("parallel",)),
    )(a, b)


def _sum_leading(a, name):
    n, R, C = a.shape
    rows = _row_block(R, n * C, mult=16 if a.dtype == BF16 else 8)

    def body(a_ref, o_ref):
        acc = a_ref[0].astype(F32)
        for j in range(1, n):
            acc = acc + a_ref[j].astype(F32)
        o_ref[...] = acc

    return pl.pallas_call(
        body, name=name, grid=(R // rows,),
        in_specs=[pl.BlockSpec((n, rows, C), lambda i: (0, i, 0))],
        out_specs=pl.BlockSpec((rows, C), lambda i: (i, 0)),
        out_shape=jax.ShapeDtypeStruct((R, C), F32),
        compiler_params=_cparams(("parallel",)),
    )(a)


def _adamw(w, g, m, v, name):
    R, C = w.shape
    rows = _row_block(R, C)
    spec = pl.BlockSpec((rows, C), lambda i: (i, 0))

    def body(w_ref, g_ref, m_ref, v_ref, d_ref, mo_ref, vo_ref):
        gv = g_ref[...]
        mn = ADAM_B1 * m_ref[...] + (1.0 - ADAM_B1) * gv
        vn = ADAM_B2 * v_ref[...] + (1.0 - ADAM_B2) * (gv * gv)
        m_hat = mn / (1.0 - ADAM_B1 ** ADAM_STEP)
        v_hat = vn / (1.0 - ADAM_B2 ** ADAM_STEP)
        d_ref[...] = -ADAM_LR * (m_hat / (jnp.sqrt(v_hat) + ADAM_EPS) + ADAM_WD * w_ref[...])
        mo_ref[...] = mn
        vo_ref[...] = vn

    return pl.pallas_call(
        body, name=name, grid=(R // rows,), in_specs=[spec] * 4, out_specs=[spec] * 3,
        out_shape=[jax.ShapeDtypeStruct((R, C), F32)] * 3,
        compiler_params=_cparams(("parallel",)),
    )(w, g, m, v)


BIG = ("w_in", "w_branch_a", "w_branch_b", "w_out", "w_ffn_up", "w_ffn_down")
COL_SHARDED = {"w_in": True, "w_branch_a": True, "w_branch_b": True, "w_out": False, "w_ffn_up": True,
               "w_ffn_down": False}
CONV_W_COLS = 2 * D_FF // N_CHIPS
SMALL_REPLICATED = (("norm1_g", D_MODEL), ("q_norm_g", HEAD_DIM), ("k_norm_g", HEAD_DIM),
                    ("rel_bias", N_HEADS * N_REL), ("norm2_g", D_MODEL), ("ffn_conv_b", 2 * D_FF))
SMALL_GRADS = SMALL_REPLICATED + (("ffn_conv_w", 3 * 2 * D_FF),)
SMALL_OWN = SMALL_REPLICATED + (("ffn_conv_w", 3 * CONV_W_COLS),)
SMALL_GRAD_ROWS = 32
SMALL_OWN_ROWS = 16


def _whole(name, stacked):
    return jnp.concatenate(list(stacked), axis=1) if COL_SHARDED[name] else stacked.reshape(-1, stacked.shape[2])


def _pack_small(vals, sizes, rows):
    flat = jnp.concatenate([vals[n].reshape(-1) for n, _ in sizes])
    return jnp.pad(flat, (0, rows * PACK_COLS - flat.shape[0])).reshape(rows, PACK_COLS)


def _unpack_small(packed, sizes):
    flat, out, o = packed.reshape(-1), {}, 0
    for n, sz in sizes:
        out[n] = flat[o:o + sz]
        o += sz
    return out


def kernel(x, norm1_g, w_in, q_norm_g, k_norm_g, rel_bias, w_branch_a, w_branch_b, w_out, norm2_g, w_ffn_up, ffn_conv_w, ffn_conv_b, w_ffn_down, loss_target, m_norm1_g, m_w_in, m_q_norm_g, m_k_norm_g, m_rel_bias, m_w_branch_a, m_w_branch_b, m_w_out, m_norm2_g, m_w_ffn_up, m_ffn_conv_w, m_ffn_conv_b, m_w_ffn_down, v_norm1_g, v_w_in, v_q_norm_g, v_k_norm_g, v_rel_bias, v_w_branch_a, v_w_branch_b, v_w_out, v_norm2_g, v_w_ffn_up, v_ffn_conv_w, v_ffn_conv_b, v_w_ffn_down):
    w_big = {"w_in": w_in[0], "w_branch_a": w_branch_a[0], "w_branch_b": w_branch_b[0], "w_out": w_out[0],
             "w_ffn_up": w_ffn_up[0], "w_ffn_down": w_ffn_down[0]}
    m_big = {"w_in": m_w_in[0], "w_branch_a": m_w_branch_a[0], "w_branch_b": m_w_branch_b[0], "w_out": m_w_out[0],
             "w_ffn_up": m_w_ffn_up[0], "w_ffn_down": m_w_ffn_down[0]}
    v_big = {"w_in": v_w_in[0], "w_branch_a": v_w_branch_a[0], "w_branch_b": v_w_branch_b[0], "w_out": v_w_out[0],
             "w_ffn_up": v_w_ffn_up[0], "w_ffn_down": v_w_ffn_down[0]}
    xs, tgt = x[0], loss_target[0]

    xi, yi, ci = _place()
    chip = 2 * xi + yi

    def with_own(stacked, own):
        return lax.dynamic_update_slice(stacked, own[None], (chip,) + (0,) * own.ndim)

    shards_bf = {n: w_big[n].astype(BF16) for n in BIG}
    conv_own = jnp.pad(ffn_conv_w[0], ((0, 8 - ffn_conv_w.shape[1]), (0, 0)))
    later = [n for n in BIG if n != "w_in"]

    hn, hn_t, w_in_g = _rms_fwd(xs, norm1_g, "rms1", gather=_Gather([shards_bf["w_in"]]))
    w_in_f = _whole("w_in", with_own(w_in_g, shards_bf["w_in"]))
    w_in_t = w_in_f.T
    qk = _matmul(hn, w_in_f[:, :2 * WIDTH], F32, "proj_qk")
    vqkv = _matmul(hn, w_in_f[:, 2 * WIDTH:6 * WIDTH], BF16, "proj_vqkv")
    gates = _matmul(hn, w_in_f[:, 6 * WIDTH:], BF16, "proj_gates")
    gq = jnp.tile(q_norm_g, (1, N_HEADS))
    gk = jnp.tile(k_norm_g, (1, N_HEADS))
    qa, ka = _qknorm_fwd(qk, gq, gk, "qknorm")
    per = WIDTH // PAIR
    b_offs = (per, 2 * per, 3 * per)
    tab = _bias_table(jnp.pad(rel_bias[0], ((0, 0), (0, REL_PAD - N_REL))), "bias_table")
    out_a, out_a_t, *gathered = _ca_fwd(qa, ka, vqkv, tab, "chunk_attn",
                                        gather=_Gather([shards_bf[n] for n in later], [conv_own]))
    full = {n: _whole(n, with_own(g, shards_bf[n])) for n, g in zip(later, gathered)}
    conv_w = jnp.concatenate(list(with_own(gathered[-1], conv_own)[:, :3]), axis=1)
    w_a, w_b, w_o, w_up, w_dn = (full[n] for n in later)
    w_a_t, w_b_t, w_o_t, w_up_t, w_dn_t = (w.T for w in (w_a, w_b, w_o, w_up, w_dn))
    out_b, out_b_f32, out_b_t = _sb_fwd(vqkv, vqkv, vqkv, "stick_attn", b_offs)
    y_a = _matmul(out_a, w_a, BF16, "branch_a")
    y_b = _matmul(out_b, w_b, BF16, "branch_b")
    mixed, mixed_t = _mix_fwd(gates, y_a, y_b, "mix")
    x2, hn2, hn2_t = _proj_rms_fwd(mixed, w_o, xs, norm2_g, "out_proj_rms2")
    hid = _matmul(hn2, w_up, BF16, "ffn_up")
    act, act_t = _convglu_fwd(hid, conv_w, ffn_conv_b, "convglu")
    dy, dyb, sq = _down_and_loss(act, w_dn, x2, tgt, "ffn_down_loss")
    loss = lax.psum(0.5 / D_MODEL * jnp.sum(sq), ("x", "y", "c"))

    dact = _matmul(dyb, w_dn_t, BF16, "d_act")
    d_w_dn = _matmul(act_t, dyb, F32, "d_w_down")
    dhg, dhu, dcwg, dcwu, dcbg, dcbu = _convglu_bwd(hid, dact, conv_w, ffn_conv_b, "convglu_bwd")
    half_chips = N_CHIPS // 2
    d_w_up = jnp.concatenate([_matmul(hn2_t, dhg, F32, "d_w_up_gate", slabs=half_chips),
                              _matmul(hn2_t, dhu, F32, "d_w_up_up", slabs=half_chips)], axis=0)
    dx2, dx2b, d_norm2 = _rms_bwd([(dhg, w_up_t[:D_FF]), (dhu, w_up_t[D_FF:])], x2, norm2_g, dy, "d_hn2_rms2_bwd")
    d_w_o = _matmul(mixed_t, dx2b, F32, "d_w_out")
    dgates, dya, dyb_b = _mix_bwd(dx2b, w_o_t, gates, y_a, y_b, "d_mixed_mix_bwd")
    d_w_a, d_w_b = (_matmul(o_t, d, F32, nm).reshape(WIDTH, N_CHIPS, -1).transpose(1, 0, 2)
                    for o_t, d, nm in ((out_a_t, dya, "d_w_branch_a"), (out_b_t, dyb_b, "d_w_branch_b")))
    do_a = _matmul(dya, w_a_t, BF16, "d_out_a")
    do_b = _matmul(dyb_b, w_b_t, BF16, "d_out_b")

    def core_sums(names, gs, sibs):
        out = {}
        for n, g, sib in zip(names, gs, sibs):
            rh, cols = sib.shape[1], sib.shape[2]
            mine = lax.dynamic_slice_in_dim(g, ci * rh, rh, axis=1)
            out[n] = _add2(mine.reshape(-1, cols), sib.reshape(-1, cols), "sum_cores_" + n).reshape(sib.shape)
        return out

    grads_full = {"w_branch_a": d_w_a, "w_branch_b": d_w_b, "w_ffn_up": d_w_up,
                  "w_out": d_w_o.reshape(N_CHIPS, -1, D_MODEL), "w_ffn_down": d_w_dn.reshape(N_CHIPS, -1, D_MODEL)}
    early = [grads_full[n] for n in later]
    chip_parts = core_sums(later, early, _exchange_cores(early, "exchange_cores_early"))
    dqb, dkb, dvb, *parts_early = _sb_bwd(vqkv, vqkv, vqkv, out_b_f32, do_b, "stick_attn_bwd", b_offs,
                                           exchange=_ChipExchange([chip_parts[n] for n in later]))
    parts = dict(zip(later, parts_early))
    dqa_n, dka_n, dva, dtab = _ca_bwd(qa, ka, vqkv, do_a, tab, "chunk_attn_bwd")
    d_rel = _bias_table_bwd(dtab, "bias_table_bwd")[:, :N_REL]
    dqa, dka, dgq, dgk = _qknorm_bwd(qk, gq, gk, dqa_n, dka_n, "qknorm_bwd")
    pieces = (("qa", dqa), ("ka", dka), ("va", dva), ("qb", dqb), ("kb", dkb), ("vb", dvb), ("gates", dgates))
    d_w_in = jnp.concatenate([_matmul(hn_t, d, F32, "d_w_in_" + nm) for nm, d in pieces], axis=1)
    d_w_in = d_w_in.reshape(D_MODEL, N_CHIPS, -1).transpose(1, 0, 2)
    chip_parts.update(core_sums(["w_in"], [d_w_in], _exchange_cores([d_w_in], "exchange_cores_w_in")))
    offsets = [sum(d.shape[1] for _, d in pieces[:k]) for k in range(len(pieces))]
    dx, _, d_norm1, parts["w_in"] = _rms_bwd([(d, w_in_t[o:o + d.shape[1]]) for (_, d), o in zip(pieces, offsets)],
                                             xs, norm1_g, dx2, "d_hn_rms1_bwd",
                                             exchange=_ChipExchange([chip_parts["w_in"]]))

    small_g = _pack_small({"norm1_g": d_norm1, "q_norm_g": dgq.reshape(N_HEADS, HEAD_DIM).sum(0),
                           "k_norm_g": dgk.reshape(N_HEADS, HEAD_DIM).sum(0), "rel_bias": d_rel,
                           "norm2_g": d_norm2, "ffn_conv_b": jnp.concatenate([dcbg, dcbu], axis=1),
                           "ffn_conv_w": jnp.concatenate([dcwg, dcwu], axis=1)}, SMALL_GRADS, SMALL_GRAD_ROWS)
    (small_all,) = _exchange_cores([], "exchange_small", small=small_g)
    g_halves = [_sum_leading(with_own(parts[n], lax.dynamic_index_in_dim(chip_parts[n], chip, 0, keepdims=False)),
                             "sum_chips_" + n) for n in BIG]
    g_others = _share_halves(g_halves, "share_halves")
    grads = {n: jnp.concatenate([jnp.where(ci == 0, mine, other), jnp.where(ci == 0, other, mine)], axis=0)
             for n, mine, other in zip(BIG, g_halves, g_others)}
    small_all = lax.dynamic_update_slice(small_all, small_g[None], (4 * xi + 2 * yi + ci, 0, 0))
    small_sum = _unpack_small(_sum_leading(small_all, "sum_small"), SMALL_GRADS)
    small_sum["ffn_conv_w"] = lax.dynamic_slice_in_dim(small_sum["ffn_conv_w"].reshape(3, 2 * D_FF),
                                                       chip * CONV_W_COLS, CONV_W_COLS, axis=1)

    deltas, new_m, new_v = {}, {}, {}
    for n in BIG:
        deltas[n], new_m[n], new_v[n] = _adamw(w_big[n], grads[n], m_big[n], v_big[n], "adamw_" + n)

    shapes = {"norm1_g": norm1_g.shape, "q_norm_g": q_norm_g.shape, "k_norm_g": k_norm_g.shape,
              "rel_bias": rel_bias.shape, "norm2_g": norm2_g.shape, "ffn_conv_b": ffn_conv_b.shape,
              "ffn_conv_w": ffn_conv_w.shape}
    small_w = {"norm1_g": norm1_g, "q_norm_g": q_norm_g, "k_norm_g": k_norm_g, "rel_bias": rel_bias,
               "norm2_g": norm2_g, "ffn_conv_b": ffn_conv_b, "ffn_conv_w": ffn_conv_w}
    small_m = {"norm1_g": m_norm1_g, "q_norm_g": m_q_norm_g, "k_norm_g": m_k_norm_g, "rel_bias": m_rel_bias,
               "norm2_g": m_norm2_g, "ffn_conv_b": m_ffn_conv_b, "ffn_conv_w": m_ffn_conv_w}
    small_v = {"norm1_g": v_norm1_g, "q_norm_g": v_q_norm_g, "k_norm_g": v_k_norm_g, "rel_bias": v_rel_bias,
               "norm2_g": v_norm2_g, "ffn_conv_b": v_ffn_conv_b, "ffn_conv_w": v_ffn_conv_w}
    ds, ms, vs = _adamw(*(_pack_small(t, SMALL_OWN, SMALL_OWN_ROWS) for t in (small_w, small_sum, small_m, small_v)),
                        "adamw_small")
    small_grads = small_sum
    ds, ms, vs = (_unpack_small(t, SMALL_OWN) for t in (ds, ms, vs))

    order = ("norm1_g", "w_in", "q_norm_g", "k_norm_g", "rel_bias", "w_branch_a", "w_branch_b", "w_out",
             "norm2_g", "w_ffn_up", "ffn_conv_w", "ffn_conv_b", "w_ffn_down")
    outs = [loss, dx[None]]
    for big, small in ((grads, small_grads), (deltas, ds), (new_m, ms), (new_v, vs)):
        for n in order:
            outs.append(big[n][None] if n in big else small[n].reshape(shapes[n]))
    return tuple(outs)
```

```python
import functools

import jax
import jax.numpy as jnp
from jax import lax
from jax.experimental import pallas as pl
from jax.experimental.pallas import tpu as pltpu

F32 = jnp.float32
BF16 = jnp.bfloat16
MESH = pl.DeviceIdType.MESH

D_MODEL = 1024
HEAD_DIM = 64
N_HEADS = 8
WIDTH = N_HEADS * HEAD_DIM
CHUNK = 64
LEFT_CHUNKS = 8
MAX_REL = 128
N_REL = 2 * MAX_REL + 1
D_FF = 2816
EPS = 1e-6
NEG = -1e30

ADAM_LR = 0.001
ADAM_B1 = 0.9
ADAM_B2 = 0.999
ADAM_EPS = 1e-08
ADAM_WD = 0.01
ADAM_STEP = 10

N_CHIPS = 4
N_DEV = 8
LANES = 128
PAIR = 2 * HEAD_DIM
BQ = 256
BAND = LEFT_CHUNKS * CHUNK
KWIN = BAND + BQ
VMEM_LIMIT = 56 * 1024 * 1024
PACK_COLS = 1024

NN = (((1,), (0,)), ((), ()))
NT = (((1,), (1,)), ((), ()))
TN = (((0,), (0,)), ((), ()))


def _cparams(sem=None):
    if sem is None:
        return pltpu.CompilerParams(vmem_limit_bytes=VMEM_LIMIT)
    return pltpu.CompilerParams(dimension_semantics=sem, vmem_limit_bytes=VMEM_LIMIT)


def _pick(n, cands):
    for c in cands:
        if n % c == 0:
            return c
    raise ValueError(f"no block for {n}")


def _dot(a, b, dn):
    return lax.dot_general(a, b, dn, preferred_element_type=F32)


def _sigmoid(x):
    return 0.5 * jnp.tanh(0.5 * x) + 0.5


def _split_bf16(x):
    hi = x.astype(BF16)
    lo = (x - hi.astype(F32)).astype(BF16)
    return hi, lo


MM_RESIDENT_BYTES = 12 * 1024 * 1024
MM_TILE_BYTES = 4 * 1024 * 1024


def _matmul(a, b, out_dtype, name, slabs=None):
    (M, K), N = a.shape, b.shape[1]
    out_bytes = jnp.dtype(out_dtype).itemsize
    if slabs is None and N <= D_FF and K * N * 2 <= MM_RESIDENT_BYTES:
        bk, bn = K, N
        bm = next(c for c in (1024, 512, 256, 128)
                  if M % c == 0 and c * K * 2 <= MM_TILE_BYTES and c * N * out_bytes <= MM_TILE_BYTES)
        b_spec = pl.BlockSpec((bk, bn), lambda i, j, k: (0, 0), pipeline_mode=pl.Buffered(1))
    elif slabs is None and K <= D_FF:
        bk, bm, bn = K, _pick(M, (1024, 512)), _pick(N, (D_FF // 2, 512, 256, 128))
        b_spec = pl.BlockSpec((bk, bn), lambda i, j, k: (k, j))
    else:
        bk = _pick(K, (2048, 1024, 512))
        bm = _pick(M, (D_FF // 2, 1024, 512, 256, 128))
        bn = N // slabs if slabs else _pick(N, (D_FF // 2, 1024, 512, 256, 128))
        b_spec = pl.BlockSpec((bk, bn), lambda i, j, k: (k, j))
    nk = K // bk
    dn = NN
    a_spec = pl.BlockSpec((bm, bk), lambda i, j, k: (i, k))
    if slabs:
        o_spec = pl.BlockSpec((None, bm, bn), lambda i, j, k: (j, i, 0))
        out_shape = jax.ShapeDtypeStruct((slabs, M, bn), out_dtype)
    else:
        o_spec = pl.BlockSpec((bm, bn), lambda i, j, k: (i, j))
        out_shape = jax.ShapeDtypeStruct((M, N), out_dtype)

    def body(a_ref, b_ref, o_ref, acc_ref):
        k = pl.program_id(2)
        part = _dot(a_ref[...], b_ref[...], dn)
        if nk == 1:
            o_ref[...] = part.astype(out_dtype)
        else:
            @pl.when(k == 0)
            def _():
                acc_ref[...] = part

            @pl.when(k > 0)
            def _():
                acc_ref[...] += part

            @pl.when(k == nk - 1)
            def _():
                o_ref[...] = acc_ref[...].astype(out_dtype)

    return pl.pallas_call(
        body, name=name, grid=(M // bm, N // bn, nk),
        in_specs=[a_spec, b_spec], out_specs=o_spec, out_shape=out_shape,
        scratch_shapes=[pltpu.VMEM((bm, bn) if nk > 1 else (8, LANES), F32)],
        compiler_params=_cparams(("parallel", "parallel", "arbitrary")),
    )(a, b)


ROWS = 512


def _row_spec(cols, bm=ROWS):
    return pl.BlockSpec((bm, cols), lambda i: (i, 0))


def _col_spec(rows, bn=ROWS):
    return pl.BlockSpec((rows, bn), lambda i: (0, i))


def _full_spec(shape):
    return pl.BlockSpec(shape, lambda i: (0,) * len(shape))


def _colsum8(t):
    return jnp.sum(t.reshape(t.shape[0] // 8, 8, t.shape[1]), axis=0)


def _proj_rms_fwd(a, w, res, g, name):
    (S, K), D = a.shape, w.shape[1]

    def body(a_ref, w_ref, res_ref, g_ref, x_ref, o_ref, ot_ref):
        xv = _dot(a_ref[...], w_ref[...], NN) + res_ref[...]
        x_ref[...] = xv
        r = lax.rsqrt(jnp.mean(xv * xv, axis=-1, keepdims=True) + EPS)
        y = xv * r * g_ref[...]
        o_ref[...] = y.astype(BF16)
        ot_ref[...] = y.T.astype(BF16)

    return pl.pallas_call(
        body, name=name, grid=(S // ROWS,),
        in_specs=[_row_spec(K), pl.BlockSpec(w.shape, lambda i: (0, 0), pipeline_mode=pl.Buffered(1)),
                  _row_spec(D), _full_spec((1, D))],
        out_specs=[_row_spec(D), _row_spec(D), _col_spec(D)],
        out_shape=[jax.ShapeDtypeStruct((S, D), F32), jax.ShapeDtypeStruct((S, D), BF16),
                   jax.ShapeDtypeStruct((D, S), BF16)],
        compiler_params=_cparams(("parallel",)),
    )(a, w, res, g)


def _rms_fwd(x, g, name, gather=None):
    S, D = x.shape
    nt = S // ROWS
    n_side = len(gather.arrays) if gather else 0

    def body(*refs):
        x_ref, g_ref = refs[:2]
        o_ref, ot_ref = refs[2 + n_side:4 + n_side]
        i = pl.program_id(0)
        if gather:
            start, forward, finish = gather.steps(refs[2:2 + n_side], refs[4 + n_side:4 + 2 * n_side], refs[4 + 2 * n_side:])
            pl.when(i == 0)(start)
            pl.when(i == 3 * nt // 4)(forward)
        xv = x_ref[...]
        r = lax.rsqrt(jnp.mean(xv * xv, axis=-1, keepdims=True) + EPS)
        y = xv * r * g_ref[...]
        o_ref[...] = y.astype(BF16)
        ot_ref[...] = y.T.astype(BF16)
        if gather:
            pl.when(i == nt - 1)(finish)

    side = gather.arrays if gather else []
    return pl.pallas_call(
        body, name=name, grid=(nt,),
        in_specs=[_row_spec(D), _full_spec((1, D))] + [ANY] * n_side,
        out_specs=[_row_spec(D), _col_spec(D)] + [ANY] * n_side,
        out_shape=[jax.ShapeDtypeStruct((S, D), BF16), jax.ShapeDtypeStruct((D, S), BF16)]
        + (gather.out_shape if gather else []),
        scratch_shapes=gather.scratch if gather else [],
        compiler_params=_cparams(("arbitrary",) if gather else ("parallel",)),
    )(x, g, *side)


RMS_BWD_ROWS = 256


def _rms_bwd(pairs, x, g, dres, name, exchange=None):
    S, D = x.shape
    bm = RMS_BWD_ROWS
    nt = S // bm
    n_pairs = len(pairs)
    n_side = len(exchange.arrays) if exchange else 0
    n_in = 2 * n_pairs + 3

    def body(*refs):
        x_ref, g_ref, dres_ref = refs[2 * n_pairs:n_in]
        dx_ref, dxb_ref, dg_ref = refs[n_in + n_side:n_in + n_side + 3]
        acc_ref = refs[n_in + 2 * n_side + 3]
        i = pl.program_id(0)
        if exchange:
            start, finish = exchange.steps(refs[n_in:n_in + n_side], refs[n_in + n_side + 3:n_in + 2 * n_side + 3],
                                           refs[n_in + 2 * n_side + 4:])
            pl.when(i == 0)(start)
        dyv = sum(_dot(refs[2 * p][...], refs[2 * p + 1][...], NN) for p in range(n_pairs))
        xv = x_ref[...]
        r = lax.rsqrt(jnp.mean(xv * xv, axis=-1, keepdims=True) + EPS)
        xr = xv * r
        u = dyv * g_ref[...]
        dx = r * u - xr * (r * r) * jnp.mean(xv * u, axis=-1, keepdims=True) + dres_ref[...]
        dx_ref[...] = dx
        dxb_ref[...] = dx.astype(BF16)
        part = _colsum8(dyv * xr)

        @pl.when(i == 0)
        def _():
            acc_ref[...] = part

        @pl.when(i > 0)
        def _():
            acc_ref[...] += part

        @pl.when(i == nt - 1)
        def _():
            dg_ref[...] = jnp.sum(acc_ref[...], axis=0, keepdims=True)

        if exchange:
            pl.when(i == nt - 1)(finish)

    rows = lambda cols: pl.BlockSpec((bm, cols), lambda i: (i, 0))
    in_specs = []
    for a, b in pairs:
        in_specs += [rows(a.shape[1]), pl.BlockSpec(b.shape, lambda i: (0, 0), pipeline_mode=pl.Buffered(1))]
    side = exchange.arrays if exchange else []
    return pl.pallas_call(
        body, name=name, grid=(nt,),
        in_specs=in_specs + [rows(D), _full_spec((1, D)), rows(D)] + [ANY] * n_side,
        out_specs=[rows(D), rows(D), _full_spec((1, D))] + [ANY] * n_side,
        out_shape=[jax.ShapeDtypeStruct((S, D), F32), jax.ShapeDtypeStruct((S, D), BF16),
                   jax.ShapeDtypeStruct((1, D), F32)] + (exchange.out_shape if exchange else []),
        scratch_shapes=[pltpu.VMEM((8, D), F32)] + (exchange.scratch if exchange else []),
        compiler_params=_cparams(("arbitrary",)),
    )(*[t for pair in pairs for t in pair], x, g, dres, *side)


def _head_mean(t, blockdiag):
    hi, lo = _split_bf16(t)
    return (_dot(hi, blockdiag, NN) + _dot(lo, blockdiag, NN)) * (1.0 / HEAD_DIM)


def _blockdiag():
    r = lax.broadcasted_iota(jnp.int32, (WIDTH, WIDTH), 0) // HEAD_DIM
    c = lax.broadcasted_iota(jnp.int32, (WIDTH, WIDTH), 1) // HEAD_DIM
    return jnp.where(r == c, 1.0, 0.0).astype(BF16)


def _proj_qknorm_fwd(hn, w, gq, gk, name):
    S, K = hn.shape

    def body(hn_ref, w_ref, gq_ref, gk_ref, qk_ref, q_ref, k_ref):
        bd = _blockdiag()
        qk = _dot(hn_ref[...], w_ref[...], NN)
        qk_ref[...] = qk
        for part, g_ref, o_ref, scale in ((0, gq_ref, q_ref, HEAD_DIM ** -0.5), (1, gk_ref, k_ref, 1.0)):
            t = qk[:, part * WIDTH:(part + 1) * WIDTH]
            r = lax.rsqrt(_head_mean(t * t, bd) + EPS)
            o_ref[...] = (t * r * g_ref[...] * scale).astype(BF16)

    return pl.pallas_call(
        body, name=name, grid=(S // ROWS,),
        in_specs=[_row_spec(K), pl.BlockSpec(w.shape, lambda i: (0, 0), pipeline_mode=pl.Buffered(1)),
                  _full_spec((1, WIDTH)), _full_spec((1, WIDTH))],
        out_specs=[_row_spec(2 * WIDTH), _row_spec(WIDTH), _row_spec(WIDTH)],
        out_shape=[jax.ShapeDtypeStruct((S, 2 * WIDTH), F32)] + [jax.ShapeDtypeStruct((S, WIDTH), BF16)] * 2,
        compiler_params=_cparams(("parallel",)),
    )(hn, w, gq, gk)


def _qknorm_bwd(qk, gq, gk, dqn, dkn, name):
    S = qk.shape[0]
    nt = S // ROWS

    def body(qk_ref, gq_ref, gk_ref, dqn_ref, dkn_ref, dq_ref, dk_ref, dgq_ref, dgk_ref, accq_ref, acck_ref):
        i = pl.program_id(0)
        bd = _blockdiag()
        for part, g_ref, dn_ref, o_ref, dg_ref, acc_ref, scale in (
                (0, gq_ref, dqn_ref, dq_ref, dgq_ref, accq_ref, HEAD_DIM ** -0.5),
                (1, gk_ref, dkn_ref, dk_ref, dgk_ref, acck_ref, 1.0)):
            t = qk_ref[:, part * WIDTH:(part + 1) * WIDTH]
            dn = dn_ref[...] * scale
            r = lax.rsqrt(_head_mean(t * t, bd) + EPS)
            u = dn * g_ref[...]
            dt = r * u - t * (r * r * r) * _head_mean(t * u, bd)
            o_ref[...] = dt.astype(BF16)
            psum = _colsum8(dn * t * r)

            @pl.when(i == 0)
            def _():
                acc_ref[...] = psum

            @pl.when(i > 0)
            def _():
                acc_ref[...] += psum

            @pl.when(i == nt - 1)
            def _():
                dg_ref[...] = jnp.sum(acc_ref[...], axis=0, keepdims=True)

    return pl.pallas_call(
        body, name=name, grid=(nt,),
        in_specs=[_row_spec(2 * WIDTH), _full_spec((1, WIDTH)), _full_spec((1, WIDTH)),
                  _row_spec(WIDTH), _row_spec(WIDTH)],
        out_specs=[_row_spec(WIDTH), _row_spec(WIDTH), _full_spec((1, WIDTH)), _full_spec((1, WIDTH))],
        out_shape=[jax.ShapeDtypeStruct((S, WIDTH), BF16)] * 2 + [jax.ShapeDtypeStruct((1, WIDTH), F32)] * 2,
        scratch_shapes=[pltpu.VMEM((8, WIDTH), F32)] * 2,
        compiler_params=_cparams(("arbitrary",)),
    )(qk, gq, gk, dqn, dkn)


def _gate_specs(D):
    return [pl.BlockSpec((ROWS, D), lambda i: (i, 0)), pl.BlockSpec((ROWS, D), lambda i: (i, 1))]


def _branch_mix_fwd(out_a, w_a, out_b, w_b, gates, name):
    (S, K), D = out_a.shape, w_a.shape[1]
    wspec = pl.BlockSpec(w_a.shape, lambda i: (0, 0), pipeline_mode=pl.Buffered(1))

    def body(oa_ref, wa_ref, ob_ref, wb_ref, ga_ref, gb_ref, ya_ref, yb_ref, o_ref, ot_ref):
        ya = _dot(oa_ref[...], wa_ref[...], NN)
        yb = _dot(ob_ref[...], wb_ref[...], NN)
        ya_ref[...] = ya.astype(BF16)
        yb_ref[...] = yb.astype(BF16)
        m = _sigmoid(ga_ref[...].astype(F32)) * ya + _sigmoid(gb_ref[...].astype(F32)) * yb
        o_ref[...] = m.astype(BF16)
        ot_ref[...] = m.T.astype(BF16)

    return pl.pallas_call(
        body, name=name, grid=(S // ROWS,),
        in_specs=[_row_spec(K), wspec, _row_spec(K), wspec] + _gate_specs(D),
        out_specs=[_row_spec(D), _row_spec(D), _row_spec(D), _col_spec(D)],
        out_shape=[jax.ShapeDtypeStruct((S, D), BF16)] * 3 + [jax.ShapeDtypeStruct((D, S), BF16)],
        compiler_params=_cparams(("parallel",)),
    )(out_a, w_a, out_b, w_b, gates, gates)


def _mix_bwd(dx, w_t, gates, ya, yb, name):
    S, D = ya.shape

    def body(dx_ref, w_ref, ga_ref, gb_ref, ya_ref, yb_ref, dg_ref, dya_ref, dyb_ref):
        dmv = _dot(dx_ref[...], w_ref[...], NN)
        for half, (g_ref, y_ref, dy_ref) in enumerate(((ga_ref, ya_ref, dya_ref), (gb_ref, yb_ref, dyb_ref))):
            s = _sigmoid(g_ref[...].astype(F32))
            dy_ref[...] = (dmv * s).astype(BF16)
            dg_ref[:, half * D:(half + 1) * D] = (dmv * y_ref[...].astype(F32) * s * (1.0 - s)).astype(BF16)

    return pl.pallas_call(
        body, name=name, grid=(S // ROWS,),
        in_specs=[_row_spec(D), pl.BlockSpec(w_t.shape, lambda i: (0, 0), pipeline_mode=pl.Buffered(1))]
        + _gate_specs(D) + [_row_spec(D)] * 2,
        out_specs=[_row_spec(2 * D), _row_spec(D), _row_spec(D)],
        out_shape=[jax.ShapeDtypeStruct((S, 2 * D), BF16)] + [jax.ShapeDtypeStruct((S, D), BF16)] * 2,
        compiler_params=_cparams(("parallel",)),
    )(dx, w_t, gates, gates, ya, yb)


def _down_and_loss(act, w, x2, target, name):
    (S, K), D = act.shape, w.shape[1]
    nt = S // ROWS

    def body(a_ref, w_ref, x_ref, t_ref, dy_ref, dyb_ref, p_ref):
        err = _dot(a_ref[...], w_ref[...], NN) + x_ref[...] - t_ref[...]
        dy = err * (1.0 / D)
        dy_ref[...] = dy
        dyb_ref[...] = dy.astype(BF16)
        sq = _colsum8(err * err)
        acc = sq[:, 0:LANES]
        for k in range(1, D // LANES):
            acc = acc + sq[:, k * LANES:(k + 1) * LANES]
        p_ref[...] = acc

    return pl.pallas_call(
        body, name=name, grid=(nt,),
        in_specs=[_row_spec(K), pl.BlockSpec((K, D), lambda i: (0, 0), pipeline_mode=pl.Buffered(1)),
                  _row_spec(D), _row_spec(D)],
        out_specs=[_row_spec(D), _row_spec(D), pl.BlockSpec((8, LANES), lambda i: (i, 0))],
        out_shape=[jax.ShapeDtypeStruct((S, D), F32), jax.ShapeDtypeStruct((S, D), BF16),
                   jax.ShapeDtypeStruct((nt * 8, LANES), F32)],
        compiler_params=_cparams(("parallel",)),
    )(act, w, x2, target)


CONV_COLS = D_FF // 2
HALO = 16
CONV_CHUNK = 64


def _aligned(start, multiple):
    return start if isinstance(start, int) else pl.multiple_of(start, multiple)


def _conv_taps(xe, cw, cb):
    taps = (pltpu.roll(xe, 2, 0), pltpu.roll(xe, 1, 0), xe)
    return taps, cw[0:1] * taps[0] + cw[1:2] * taps[1] + cw[2:3] * taps[2] + cb


def _conv_specs(nt):
    hb, nb = ROWS // HALO, D_FF // CONV_COLS
    specs = {}
    for part, off in (("gate", 0), ("up", nb)):
        specs[part] = dict(
            main=pl.BlockSpec((ROWS, CONV_COLS), functools.partial(lambda c, i, off: (i, c + off), off=off)),
            prev=pl.BlockSpec((HALO, CONV_COLS),
                              functools.partial(lambda c, i, off: (jnp.maximum(i * hb - 1, 0), c + off), off=off)),
            nxt=pl.BlockSpec((HALO, CONV_COLS),
                             functools.partial(lambda c, i, off: (jnp.minimum((i + 1) * hb, nt * hb - 1), c + off), off=off)),
            w=pl.BlockSpec((3, CONV_COLS), functools.partial(lambda c, i, off: (0, c + off), off=off)),
            b=pl.BlockSpec((1, CONV_COLS), functools.partial(lambda c, i, off: (0, c + off), off=off)))
    return specs


def _convglu_fwd(hid, cw, cb, name):
    S = hid.shape[0]
    sp = _conv_specs(S // ROWS)

    def body(hg_ref, hgp_ref, hu_ref, hup_ref, cwg_ref, cwu_ref, cbg_ref, cbu_ref, o_ref, ot_ref):
        i = pl.program_id(1)
        keep = (i > 0).astype(F32)

        def conv(h_ref, hp_ref, cw_ref, cb_ref):
            xe = jnp.concatenate([hp_ref[...].astype(F32) * keep, h_ref[...].astype(F32)], axis=0)
            return _conv_taps(xe, cw_ref[...], cb_ref[...])[1][HALO:, :]

        gate = conv(hg_ref, hgp_ref, cwg_ref, cbg_ref)
        up = conv(hu_ref, hup_ref, cwu_ref, cbu_ref)
        act = gate * _sigmoid(gate) * up
        o_ref[...] = act.astype(BF16)
        ot_ref[...] = act.T.astype(BF16)

    g, u = sp["gate"], sp["up"]
    return pl.pallas_call(
        body, name=name, grid=(D_FF // CONV_COLS, S // ROWS),
        in_specs=[g["main"], g["prev"], u["main"], u["prev"], g["w"], u["w"], g["b"], u["b"]],
        out_specs=[g["main"], pl.BlockSpec((CONV_COLS, ROWS), lambda c, i: (c, i))],
        out_shape=[jax.ShapeDtypeStruct((S, D_FF), BF16), jax.ShapeDtypeStruct((D_FF, S), BF16)],
        compiler_params=_cparams(("parallel", "parallel")),
    )(hid, hid, hid, hid, cw, cw, cb, cb)


def _convglu_bwd(hid, dact, cw, cb, name):
    S = hid.shape[0]
    nt = S // ROWS
    sp = _conv_specs(nt)

    n_chunks = ROWS // CONV_CHUNK

    def body(hg_ref, hgp_ref, hgn_ref, hu_ref, hup_ref, hun_ref, da_ref, dan_ref,
             cwg_ref, cwu_ref, cbg_ref, cbu_ref,
             dhg_ref, dhu_ref, dcwg_ref, dcwu_ref, dcbg_ref, dcbu_ref, xg_s, xu_s, da_s):
        i = pl.program_id(1)
        kp = (i > 0).astype(F32)
        kn = (i < nt - 1).astype(F32)
        for x_s, h_ref, hp_ref, hn_ref in ((xg_s, hg_ref, hgp_ref, hgn_ref), (xu_s, hu_ref, hup_ref, hun_ref)):
            x_s[0:HALO, :] = hp_ref[...].astype(F32) * kp
            x_s[HALO:HALO + ROWS, :] = h_ref[...].astype(F32)
            x_s[HALO + ROWS:, :] = hn_ref[...].astype(F32) * kn
        da_s[0:ROWS, :] = da_ref[...].astype(F32)
        da_s[ROWS:, :] = dan_ref[...].astype(F32) * kn

        @pl.when(i == 0)
        def _():
            for ref in (dcwg_ref, dcwu_ref, dcbg_ref, dcbu_ref):
                ref[...] = jnp.zeros_like(ref)

        def lane_group(grp, _):
            lanes = pl.ds(pl.multiple_of(grp * LANES, LANES), LANES)
            cwg, cwu, cbg, cbu = cwg_ref[:, lanes], cwu_ref[:, lanes], cbg_ref[:, lanes], cbu_ref[:, lanes]

            def grads(r0, n):
                rows = pl.ds(_aligned(r0 + HALO - 8, 8), n + 8)
                taps_g, gate = _conv_taps(xg_s[rows, lanes], cwg, cbg)
                taps_u, up = _conv_taps(xu_s[rows, lanes], cwu, cbu)
                gate, up = gate[8:], up[8:]
                da = da_s[pl.ds(_aligned(r0, 8), n), lanes]
                sg = _sigmoid(gate)
                return (da * up * sg * (1.0 + gate * (1.0 - sg)), da * gate * sg,
                        [t[8:] for t in taps_g], [t[8:] for t in taps_u])

            def chunk(step, carry):
                below_g, below_u, accs = carry
                r0 = (n_chunks - 1 - step) * CONV_CHUNK
                dg, du, taps_g, taps_u = grads(r0, CONV_CHUNK)
                new_accs = []
                for d, below, cwv, taps, dh_ref, acc in ((dg, below_g, cwg, taps_g, dhg_ref, accs[0]),
                                                        (du, below_u, cwu, taps_u, dhu_ref, accs[1])):
                    ext = jnp.concatenate([d, below], axis=0)
                    n_ext = CONV_CHUNK + 8
                    dh = (cwv[2:3] * d + cwv[1:2] * pltpu.roll(ext, n_ext - 1, 0)[:CONV_CHUNK]
                          + cwv[0:1] * pltpu.roll(ext, n_ext - 2, 0)[:CONV_CHUNK])
                    dh_ref[pl.ds(_aligned(r0, CONV_CHUNK), CONV_CHUNK), lanes] = dh.astype(BF16)
                    new_accs.append(tuple(a + _colsum8(d * tap) for a, tap in zip(acc[:3], taps))
                                    + (acc[3] + _colsum8(d),))
                return dg[0:8], du[0:8], tuple(new_accs)

            below_g, below_u, _, _ = grads(ROWS, 8)
            zero = jnp.zeros((8, LANES), F32)
            _, _, accs = lax.fori_loop(0, n_chunks, chunk, (below_g, below_u, ((zero,) * 4, (zero,) * 4)))
            for acc, dcw_ref, dcb_ref in ((accs[0], dcwg_ref, dcbg_ref), (accs[1], dcwu_ref, dcbu_ref)):
                for t in range(3):
                    dcw_ref[t:t + 1, lanes] += jnp.sum(acc[t], axis=0, keepdims=True)
                dcb_ref[:, lanes] += jnp.sum(acc[3], axis=0, keepdims=True)
            return 0

        lax.fori_loop(0, CONV_COLS // LANES, lane_group, 0)

    g, u = sp["gate"], sp["up"]
    return pl.pallas_call(
        body, name=name, grid=(D_FF // CONV_COLS, nt),
        in_specs=[g["main"], g["prev"], g["nxt"], u["main"], u["prev"], u["nxt"], g["main"], g["nxt"],
                  g["w"], u["w"], g["b"], u["b"]],
        out_specs=[g["main"], g["main"], g["w"], g["w"], g["b"], g["b"]],
        out_shape=[jax.ShapeDtypeStruct((S, D_FF), BF16)] * 2 + [jax.ShapeDtypeStruct((3, D_FF), F32)] * 2
        + [jax.ShapeDtypeStruct((1, D_FF), F32)] * 2,
        scratch_shapes=[pltpu.VMEM((ROWS + 2 * HALO, CONV_COLS), F32)] * 2 + [pltpu.VMEM((ROWS + HALO, CONV_COLS), F32)],
        compiler_params=_cparams(("parallel", "arbitrary")),
    )(hid, hid, hid, hid, hid, hid, dact, dact, cw, cw, cb, cb)


REL_PAD = 384
DIAG = 1024


def _band_valid():
    qc = lax.broadcasted_iota(jnp.int32, (BQ, KWIN), 0) // CHUNK
    kc = lax.broadcasted_iota(jnp.int32, (BQ, KWIN), 1) // CHUNK - LEFT_CHUNKS
    return (kc <= qc) & (kc >= qc - LEFT_CHUNKS)


def _rel_index(offset):
    return jnp.clip(BAND - offset, -MAX_REL, MAX_REL) + MAX_REL


def _split3(x):
    hi = x.astype(BF16)
    rest = x - hi.astype(F32)
    mid = rest.astype(BF16)
    return hi, mid, (rest - mid.astype(F32)).astype(BF16)


def _bias_table(rel_bias, name):
    def body(rb_ref, o_ref):
        t = lax.broadcasted_iota(jnp.int32, (REL_PAD, DIAG), 0)
        lane = lax.broadcasted_iota(jnp.int32, (REL_PAD, DIAG), 1)
        pick = jnp.where(t == _rel_index(lane - BQ), 1.0, 0.0).astype(BF16)
        base = sum(_dot(piece, pick, NN) for piece in _split3(rb_ref[...]))
        valid = _band_valid()
        for h in range(N_HEADS):
            rows = jnp.broadcast_to(base[h:h + 1], (BQ, DIAG))
            rolled = pltpu.roll(rows, 0, 1, stride=1, stride_axis=0)
            o_ref[h] = jnp.where(valid, rolled[:, BQ:], NEG)

    return pl.pallas_call(
        body, name=name,
        out_shape=jax.ShapeDtypeStruct((N_HEADS, BQ, KWIN), F32),
        compiler_params=_cparams(),
    )(rel_bias)


def _bias_table_bwd(dtab, name):
    def body(d_ref, o_ref, diag_ref):
        r = lax.broadcasted_iota(jnp.int32, (BQ, BQ), 0)
        c = lax.broadcasted_iota(jnp.int32, (BQ, BQ), 1)
        flip = jnp.where(r + c == BQ - 1, 1.0, 0.0).astype(BF16)
        for h in range(N_HEADS):
            flipped = sum(_dot(flip, piece, NN) for piece in _split3(d_ref[h]))
            padded = jnp.concatenate([flipped, jnp.zeros((BQ, DIAG - KWIN), F32)], axis=1)
            rolled = pltpu.roll(padded, DIAG - (BQ - 1), 1, stride=1, stride_axis=0)
            diag_ref[h:h + 1, :] = jnp.sum(rolled, axis=0, keepdims=True)
        lane = lax.broadcasted_iota(jnp.int32, (DIAG, REL_PAD), 0)
        t = lax.broadcasted_iota(jnp.int32, (DIAG, REL_PAD), 1)
        offset = jnp.where(lane < KWIN, lane, lane - DIAG)
        pick = jnp.where(t == _rel_index(offset), 1.0, 0.0).astype(BF16)
        o_ref[...] = sum(_dot(piece, pick, NN) for piece in _split3(diag_ref[...]))

    return pl.pallas_call(
        body, name=name,
        out_shape=jax.ShapeDtypeStruct((N_HEADS, REL_PAD), F32),
        scratch_shapes=[pltpu.VMEM((N_HEADS, DIAG), F32)],
        compiler_params=_cparams(),
    )(dtab)


def _head_masks(heads=2):
    lane = lax.broadcasted_iota(jnp.int32, (1, heads * HEAD_DIM), 1)
    return [lane // HEAD_DIM == h for h in range(heads)]


def _own_lanes(masks, vals):
    out = vals[-1]
    for m, val in zip(masks[-2::-1], vals[-2::-1]):
        out = jnp.where(m, val, out)
    return out


CA_HEADS = 4
CA_LANES = CA_HEADS * HEAD_DIM


def _ca_window_specs(nq, col_off=0):
    return [pl.BlockSpec((BQ, CA_LANES), functools.partial(
        lambda p, i, d: (jnp.clip(i - 2 + d, 0, nq - 1), p + col_off), d=d)) for d in range(3)]


def _softmax_rows(s):
    p = jnp.exp(s - jnp.max(s, axis=-1, keepdims=True))
    return p, jnp.sum(p, axis=-1, keepdims=True)


def _ca_scores(qm, kc, tab_h, i):
    col = lax.broadcasted_iota(jnp.int32, (1, KWIN), 1)
    in_seq = col + (i - 2) * BQ >= 0
    return jnp.where(in_seq, _dot(qm, kc, NT) + tab_h, NEG)


def _ca_fwd(qn, kn, v, tab, name, v_off=0, gather=None):
    S = qn.shape[0]
    nq = S // BQ
    groups = WIDTH // CA_LANES
    qspec = pl.BlockSpec((BQ, CA_LANES), lambda p, i: (i, p))
    tspec = pl.BlockSpec((CA_HEADS, BQ, KWIN), lambda p, i: (p, 0, 0))
    n_side = len(gather.arrays) if gather else 0

    def body(*refs):
        q_ref, k0, k1, k2, v0, v1, v2, tab_ref = refs[:8]
        o_ref, ot_ref = refs[8 + n_side:10 + n_side]
        p, i = pl.program_id(0), pl.program_id(1)
        if gather:
            start, forward, finish = gather.steps(refs[8:8 + n_side], refs[10 + n_side:10 + 2 * n_side],
                                                  refs[10 + 2 * n_side:])
            pl.when((p == 0) & (i == 0))(start)
            pl.when((p == groups - 1) & (i == nq // 2))(forward)
        kc = jnp.concatenate([k0[...], k1[...], k2[...]], axis=0)
        vc = jnp.concatenate([v0[...], v1[...], v2[...]], axis=0)
        qv = q_ref[...]
        masks = _head_masks(CA_HEADS)
        heads = range(CA_HEADS)
        s = [_ca_scores(jnp.where(masks[h], qv, 0), kc, tab_ref[h], i) for h in heads]
        soft = [_softmax_rows(s[h]) for h in heads]
        o = [_dot(soft[h][0].astype(BF16), vc, NN) / soft[h][1] for h in heads]
        out = _own_lanes(masks, o)
        o_ref[...] = out.astype(BF16)
        ot_ref[...] = out.T.astype(BF16)
        if gather:
            pl.when((p == groups - 1) & (i == nq - 1))(finish)

    side = gather.arrays if gather else []
    return pl.pallas_call(
        body, name=name, grid=(groups, nq),
        in_specs=[qspec] + _ca_window_specs(nq) + _ca_window_specs(nq, v_off) + [tspec] + [ANY] * n_side,
        out_specs=[qspec, pl.BlockSpec((CA_LANES, BQ), lambda p, i: (p, i))] + [ANY] * n_side,
        out_shape=[jax.ShapeDtypeStruct((S, WIDTH), BF16), jax.ShapeDtypeStruct((WIDTH, S), BF16)]
        + (gather.out_shape if gather else []),
        scratch_shapes=gather.scratch if gather else [],
        compiler_params=_cparams(("arbitrary", "arbitrary") if gather else ("parallel", "parallel")),
    )(qn, kn, kn, kn, v, v, v, tab, *side)


def _ca_bwd(qn, kn, v, do, tab, name, v_off=0):
    S = qn.shape[0]
    nq = S // BQ
    qspec = pl.BlockSpec((BQ, CA_LANES), lambda p, i: (jnp.minimum(i, nq - 1), p))
    kout = pl.BlockSpec((BQ, CA_LANES), lambda p, i: (jnp.clip(i - 2, 0, nq - 1), p))
    tspec = pl.BlockSpec((CA_HEADS, BQ, KWIN), lambda p, i: (p, 0, 0))

    def body(q_ref, do_ref, k0, k1, k2, v0, v1, v2, tab_ref,
             dq_ref, dk_ref, dv_ref, dtab_ref, dk_acc, dv_acc):
        i = pl.program_id(1)

        @pl.when(i == 0)
        def _():
            dk_acc[...] = jnp.zeros_like(dk_acc)
            dv_acc[...] = jnp.zeros_like(dv_acc)
            dtab_ref[...] = jnp.zeros_like(dtab_ref)

        @pl.when(i < nq)
        def _():
            kc = jnp.concatenate([k0[...], k1[...], k2[...]], axis=0)
            vc = jnp.concatenate([v0[...], v1[...], v2[...]], axis=0)
            qv, dov = q_ref[...], do_ref[...]
            masks = _head_masks(CA_HEADS)
            heads = range(CA_HEADS)
            qm = [jnp.where(masks[h], qv, 0) for h in heads]
            dom = [jnp.where(masks[h], dov, 0) for h in heads]
            s = [_ca_scores(qm[h], kc, tab_ref[h], i) for h in heads]
            dp = [_dot(dom[h], vc, NT) for h in heads]
            soft = [_softmax_rows(s[h]) for h in heads]
            p = [soft[h][0] / soft[h][1] for h in heads]
            ds = [p[h] * (dp[h] - jnp.sum(p[h] * dp[h], axis=-1, keepdims=True)) for h in heads]
            for h in heads:
                dtab_ref[h] += ds[h]
            dsb = [ds[h].astype(BF16) for h in heads]
            pb = [p[h].astype(BF16) for h in heads]
            dq = [_dot(dsb[h], kc, NN) for h in heads]
            dq_ref[...] = _own_lanes(masks, dq)
            dkc = sum(_dot(dsb[h], qm[h], TN) for h in heads)
            dvc = sum(_dot(pb[h], dom[h], TN) for h in heads)
            for d in range(3):
                slot = (i + 1 + d) % 3
                dk_acc[slot] += dkc[d * BQ:(d + 1) * BQ]
                dv_acc[slot] += dvc[d * BQ:(d + 1) * BQ]

        @pl.when(i >= 2)
        def _():
            slot = (i + 1) % 3
            dk_ref[...] = dk_acc[slot]
            dv_ref[...] = dv_acc[slot].astype(BF16)
            dk_acc[slot] = jnp.zeros((BQ, CA_LANES), F32)
            dv_acc[slot] = jnp.zeros((BQ, CA_LANES), F32)

    return pl.pallas_call(
        body, name=name, grid=(WIDTH // CA_LANES, nq + 2),
        in_specs=[qspec, qspec] + _ca_window_specs(nq) + _ca_window_specs(nq, v_off) + [tspec],
        out_specs=[qspec, kout, kout, tspec],
        out_shape=[jax.ShapeDtypeStruct((S, WIDTH), F32), jax.ShapeDtypeStruct((S, WIDTH), F32),
                   jax.ShapeDtypeStruct((S, WIDTH), BF16), jax.ShapeDtypeStruct((N_HEADS, BQ, KWIN), F32)],
        scratch_shapes=[pltpu.VMEM((3, BQ, CA_LANES), F32)] * 2,
        compiler_params=_cparams(("parallel", "arbitrary")),
    )(qn, do, kn, kn, kn, v, v, v, tab)


def _sb_consts():
    r = lax.broadcasted_iota(jnp.int32, (BQ, BQ), 0)
    c = lax.broadcasted_iota(jnp.int32, (BQ, BQ), 1)
    from_s = jnp.where(r >= c, 1.0, 0.0).astype(BF16)
    causal = c < r
    return from_s, causal


def _suffix_sum(t, from_s):
    hi, lo = _split_bf16(t)
    return _dot(hi, from_s, NN) + _dot(lo, from_s, NN)


def _neg_abs(x):
    bits = lax.bitcast_convert_type(x, jnp.uint32) | jnp.uint32(0x80000000)
    return lax.bitcast_convert_type(bits, F32)


def _sb_log_keep(zn):
    return jnp.minimum(zn, 0.0) - jnp.log(1.0 + jnp.exp(_neg_abs(zn)))


SB_DEAD = 105.0


SB_QB = 2


def _sb_walk(ip, tiles, keep_ref):
    i0 = SB_QB * ip

    @pl.when(ip == 0)
    def _():
        tiles([(0, [0], [True]), (1, [1, 0], [True, False])])

    @pl.when(ip > 0)
    def _():
        tiles([(a, [i0 + a, i0 + a - 1], [True, False]) for a in range(SB_QB)])

    for a in range(SB_QB):
        def alive(a=a):
            return (jnp.max(keep_ref[2 * a:2 * a + 2]) > -SB_DEAD).astype(jnp.int32)

        def step(state, a=a, alive=alive):
            j, _ = state
            tiles([(a, [j], [False])])
            return j - 1, alive()

        lax.while_loop(lambda state: (state[0] >= 0) & (state[1] > 0), step, (i0 + a - 2, alive()))


def _sb_rows(j):
    return pl.ds(pl.multiple_of(j * BQ, BQ), BQ)


def _sb_chains(groups):
    chains = [(a, n, h) for a, js, _ in groups for n in range(len(js)) for h in range(2)]
    block = {(a, n): j for a, js, _ in groups for n, j in enumerate(js)}
    masked = [(a, n, h) for a, _, diags in groups for n, d in enumerate(diags) if d for h in range(2)]
    return chains, block, masked


def _sb_running(ref, vals, groups):
    before_chain = {}
    for a, js, _ in groups:
        for h in range(2):
            run = ref[2 * a + h]
            for n in range(len(js)):
                before_chain[(a, n, h)] = run
                run = run + jnp.sum(vals[(a, n, h)], axis=-1, keepdims=True)
            ref[2 * a + h] = run
    return before_chain


def _sb_specs(S, offs):
    def qspec(off=0):
        return pl.BlockSpec((SB_QB * BQ, PAIR), lambda p, i: (i, p + off))

    def kspec(off=0):
        return pl.BlockSpec((S, PAIR), lambda p, i: (0, p + off), pipeline_mode=pl.Buffered(1))

    return qspec, kspec, [qspec(offs[0]), kspec(offs[1]), kspec(offs[2])]


def _sb_fwd(q, k, v, name, offs=(0, 0, 0)):
    S = q.shape[0]
    steps = S // (SB_QB * BQ)
    qspec, _, qkv_specs = _sb_specs(S, offs)

    def body(q_ref, k_ref, v_ref, o_ref, of_ref, ot_ref, carry_ref, acc_ref):
        ip = pl.program_id(1)
        from_s, causal = _sb_consts()
        masks = _head_masks()
        qn = q_ref[...] * -(HEAD_DIM ** -0.5)
        qms = {(a, h): jnp.where(masks[h], qn[a * BQ:(a + 1) * BQ], 0) for a in range(SB_QB) for h in range(2)}
        carry_ref[...] = jnp.zeros_like(carry_ref)
        acc_ref[...] = jnp.zeros_like(acc_ref)

        def tiles(groups):
            chains, block, masked = _sb_chains(groups)
            kbs = {an: k_ref[_sb_rows(j), :] for an, j in block.items()}
            vbs = {an: v_ref[_sb_rows(j), :] for an, j in block.items()}
            zn = {c: _dot(qms[(c[0], c[2])], kbs[c[:2]], NT) for c in chains}
            log_keep = {c: _sb_log_keep(zn[c]) for c in chains}
            for c in masked:
                log_keep[c] = jnp.where(causal, log_keep[c], 0.0)
            split = {c: _split_bf16(log_keep[c]) for c in chains}
            carry = _sb_running(carry_ref, log_keep, groups)
            suffix = {c: _dot(split[c][0], from_s, NN) + _dot(split[c][1], from_s, NN) for c in chains}
            w = {c: jnp.exp(carry[c] + suffix[c] - zn[c]) for c in chains}
            for c in masked:
                w[c] = jnp.where(causal, w[c], 0.0)
            for c in chains:
                acc_ref[2 * c[0] + c[2]] += _dot(w[c].astype(BF16), vbs[c[:2]], NN)

        _sb_walk(ip, tiles, carry_ref)
        for a in range(SB_QB):
            out = jnp.where(masks[0], acc_ref[2 * a], acc_ref[2 * a + 1])
            o_ref[a * BQ:(a + 1) * BQ, :] = out.astype(BF16)
            of_ref[a * BQ:(a + 1) * BQ, :] = out
            ot_ref[:, a * BQ:(a + 1) * BQ] = out.T.astype(BF16)

    return pl.pallas_call(
        body, name=name, grid=(WIDTH // PAIR, steps),
        in_specs=qkv_specs, out_specs=[qspec(), qspec(), pl.BlockSpec((PAIR, SB_QB * BQ), lambda p, i: (p, i))],
        out_shape=[jax.ShapeDtypeStruct((S, WIDTH), BF16), jax.ShapeDtypeStruct((S, WIDTH), F32),
                   jax.ShapeDtypeStruct((WIDTH, S), BF16)],
        scratch_shapes=[pltpu.VMEM((2 * SB_QB, BQ, 1), F32), pltpu.VMEM((2 * SB_QB, BQ, PAIR), F32)],
        compiler_params=_cparams(("parallel", "arbitrary")),
    )(q, k, v)


def _sb_bwd(q, k, v, o, do, name, offs=(0, 0, 0), exchange=None):
    S = q.shape[0]
    steps = S // (SB_QB * BQ)
    pairs = WIDTH // PAIR
    qspec, kspec, qkv_specs = _sb_specs(S, offs)
    n_side = len(exchange.arrays) if exchange else 0

    def body(*refs):
        q_ref, o_ref, do_ref, k_ref, v_ref = refs[:5]
        dq_ref, dk_ref, dv_ref = refs[5 + n_side:8 + n_side]
        dk_acc, dv_acc, keep_ref, gsum_ref, dq_acc = refs[8 + 2 * n_side:13 + 2 * n_side]
        ip = pl.program_id(1)
        if exchange:
            start, finish = exchange.steps(refs[5:5 + n_side], refs[8 + n_side:8 + 2 * n_side], refs[13 + 2 * n_side:])
            pl.when((pl.program_id(0) == 0) & (ip == 0))(start)

        @pl.when(ip == 0)
        def _():
            dk_acc[...] = jnp.zeros_like(dk_acc)
            dv_acc[...] = jnp.zeros_like(dv_acc)

        from_s, causal = _sb_consts()
        masks = _head_masks()
        qn, dov = q_ref[...] * -(HEAD_DIM ** -0.5), do_ref[...]
        od = o_ref[...] * dov.astype(F32)
        lanes = [(a, h) for a in range(SB_QB) for h in range(2)]
        rows_of = {a: slice(a * BQ, (a + 1) * BQ) for a in range(SB_QB)}
        qms = {(a, h): jnp.where(masks[h], qn[rows_of[a]], 0) for a, h in lanes}
        doms = {(a, h): jnp.where(masks[h], dov[rows_of[a]], 0) for a, h in lanes}
        totals = {(a, h): jnp.sum(jnp.where(masks[h], od[rows_of[a]], 0.0), axis=-1, keepdims=True)
                  for a, h in lanes}
        for ref in (keep_ref, gsum_ref, dq_acc):
            ref[...] = jnp.zeros_like(ref)

        def tiles(groups):
            chains, block, masked = _sb_chains(groups)
            kbs = {an: k_ref[_sb_rows(j), :] for an, j in block.items()}
            vbs = {an: v_ref[_sb_rows(j), :] for an, j in block.items()}
            zn = {c: _dot(qms[(c[0], c[2])], kbs[c[:2]], NT) for c in chains}
            dw = {c: _dot(doms[(c[0], c[2])], vbs[c[:2]], NT) for c in chains}
            log_keep = {c: _sb_log_keep(zn[c]) for c in chains}
            for c in masked:
                log_keep[c] = jnp.where(causal, log_keep[c], 0.0)
            split = {c: _split_bf16(log_keep[c]) for c in chains}
            kept = _sb_running(keep_ref, log_keep, groups)
            suffix = {c: _dot(split[c][0], from_s, NN) + _dot(split[c][1], from_s, NN) for c in chains}
            w = {c: jnp.exp(kept[c] + suffix[c] - zn[c]) for c in chains}
            for c in masked:
                w[c] = jnp.where(causal, w[c], 0.0)
            wb = {c: w[c].astype(BF16) for c in chains}
            g = {c: wb[c].astype(F32) * dw[c] for c in chains}
            gsplit = {c: _split_bf16(g[c]) for c in chains}
            gsum = _sb_running(gsum_ref, g, groups)
            gsuffix = {c: _dot(gsplit[c][0], from_s, NN) + _dot(gsplit[c][1], from_s, NN) for c in chains}
            dzb = {}
            for c in chains:
                before = totals[(c[0], c[2])] - (gsum[c] + gsuffix[c])
                dz = (g[c] + before) * jnp.exp(log_keep[c]) - before
                if c in masked:
                    dz = jnp.where(causal, dz, 0.0)
                dzb[c] = dz.astype(BF16)
            for c in chains:
                rows = _sb_rows(block[c[:2]])
                dq_acc[2 * c[0] + c[2]] += _dot(dzb[c], kbs[c[:2]], NN)
                dk_acc[rows, :] -= _dot(dzb[c], qms[(c[0], c[2])], TN)
                dv_acc[rows, :] += _dot(wb[c], doms[(c[0], c[2])], TN)

        _sb_walk(ip, tiles, keep_ref)
        for a in range(SB_QB):
            dq = jnp.where(masks[0], dq_acc[2 * a], dq_acc[2 * a + 1])
            dq_ref[a * BQ:(a + 1) * BQ, :] = (dq * HEAD_DIM ** -0.5).astype(BF16)

        @pl.when(ip == steps - 1)
        def _():
            dk_ref[...] = dk_acc[...].astype(BF16)
            dv_ref[...] = dv_acc[...].astype(BF16)

        if exchange:
            pl.when((pl.program_id(0) == pairs - 1) & (ip == steps - 1))(finish)

    side = exchange.arrays if exchange else []
    return pl.pallas_call(
        body, name=name, grid=(pairs, steps),
        in_specs=[qkv_specs[0], qspec(), qspec(), qkv_specs[1], qkv_specs[2]] + [ANY] * n_side,
        out_specs=[qspec(), kspec(), kspec()] + [ANY] * n_side,
        out_shape=[jax.ShapeDtypeStruct((S, WIDTH), BF16)] * 3 + (exchange.out_shape if exchange else []),
        scratch_shapes=[pltpu.VMEM((S, PAIR), F32)] * 2 + [pltpu.VMEM((2 * SB_QB, BQ, 1), F32)] * 2
        + [pltpu.VMEM((2 * SB_QB, BQ, PAIR), F32)] + (exchange.scratch if exchange else []),
        compiler_params=_cparams(("arbitrary", "arbitrary") if exchange else ("parallel", "arbitrary")),
    )(q, o, do, k, v, *side)


ANY = pl.BlockSpec(memory_space=pl.ANY)


def _place():
    return lax.axis_index("x"), lax.axis_index("y"), lax.axis_index("c")


def _other_chips(x, y):
    return [(2 * px + py, (px, py)) for px, py in ((1 - x, y), (x, 1 - y), (1 - x, 1 - y))]


def _remote(src, dst, sems, k, to):
    return pltpu.make_async_remote_copy(src_ref=src, dst_ref=dst, send_sem=sems[0].at[k], recv_sem=sems[1].at[k],
                                        device_id=to, device_id_type=MESH)


class _Gather:
    def __init__(self, ws, extras=()):
        self.n, self.m = len(ws), len(extras)
        self.arrays = list(ws) + list(extras)
        self.n_copies = 6 * self.n + 3 * self.m
        self.out_shape = [jax.ShapeDtypeStruct((N_CHIPS,) + a.shape, a.dtype) for a in self.arrays]
        self.scratch = [pltpu.SemaphoreType.DMA((self.n_copies,)), pltpu.SemaphoreType.DMA((self.n_copies,))]

    def steps(self, in_refs, out_refs, sems):
        n = self.n
        x, y, c = _place()
        me = 2 * x + y
        chips = _other_chips(x, y)
        sibling = (x, y, 1 - c)

        def halves(ref):
            rh = ref.shape[-2] // 2
            return pl.ds(c * rh, rh), pl.ds((1 - c) * rh, rh)

        def first():
            cps = [_remote(w_ref.at[halves(w_ref)[0]], o_ref.at[me, halves(w_ref)[0]], sems, 6 * a + k, (*xy, c))
                   for a, (w_ref, o_ref) in enumerate(zip(in_refs[:n], out_refs[:n])) for k, (_, xy) in enumerate(chips)]
            return cps + [_remote(e_ref, eo_ref.at[me], sems, 6 * n + 3 * b + k, (*xy, c))
                          for b, (e_ref, eo_ref) in enumerate(zip(in_refs[n:], out_refs[n:]))
                          for k, (_, xy) in enumerate(chips)]

        def passed():
            return [_remote(o_ref.at[chip, halves(o_ref)[0]], o_ref.at[chip, halves(o_ref)[0]], sems, 6 * a + 3 + k, sibling)
                    for a, o_ref in enumerate(out_refs[:n]) for k, (chip, _) in enumerate(chips)]

        def start():
            for cp in first():
                cp.start()

        def forward():
            for a, o_ref in enumerate(out_refs[:n]):
                for k, (chip, xy) in enumerate(chips):
                    landed = o_ref.at[chip, halves(o_ref)[0]]
                    _remote(landed, landed, sems, 6 * a + k, (*xy, c)).wait_recv()
            for cp in passed():
                cp.start()

        def finish():
            for a, o_ref in enumerate(out_refs[:n]):
                for k, (chip, _) in enumerate(chips):
                    landed = o_ref.at[chip, halves(o_ref)[1]]
                    _remote(landed, landed, sems, 6 * a + 3 + k, sibling).wait_recv()
            for b, (e_ref, eo_ref) in enumerate(zip(in_refs[n:], out_refs[n:])):
                for k, (chip, xy) in enumerate(chips):
                    _remote(e_ref, eo_ref.at[chip], sems, 6 * n + 3 * b + k, (*xy, c)).wait_recv()
            for cp in first() + passed():
                cp.wait_send()

        return start, forward, finish


class _ChipExchange:
    def __init__(self, ps):
        self.arrays = list(ps)
        self.out_shape = [jax.ShapeDtypeStruct(p.shape, p.dtype) for p in ps]
        self.scratch = [pltpu.SemaphoreType.DMA((3 * len(ps),)), pltpu.SemaphoreType.DMA((3 * len(ps),))]

    def steps(self, p_refs, out_refs, sems):
        x, y, c = _place()
        me = 2 * x + y
        chips = _other_chips(x, y)

        def copies():
            return [_remote(p_ref.at[chip], o_ref.at[me], sems, 3 * a + k, (*xy, c))
                    for a, (p_ref, o_ref) in enumerate(zip(p_refs, out_refs)) for k, (chip, xy) in enumerate(chips)]

        def start():
            for cp in copies():
                cp.start()

        def finish():
            for a, (p_ref, o_ref) in enumerate(zip(p_refs, out_refs)):
                for k, (chip, xy) in enumerate(chips):
                    _remote(p_ref.at[chip], o_ref.at[chip], sems, 3 * a + k, (*xy, c)).wait_recv()
            for cp in copies():
                cp.wait_send()

        return start, finish


def _exchange_cores(gs, name, small=None):
    n = len(gs)
    m = 0 if small is None else 1

    def body(*refs):
        g_refs, sib_refs = refs[:n], refs[n + m:2 * n + m]
        sems = refs[2 * (n + m):]
        x, y, c = _place()
        me = 4 * x + 2 * y + c
        copies = []
        for a, (g_ref, sib_ref) in enumerate(zip(g_refs, sib_refs)):
            rh = g_ref.shape[1] // 2
            copies.append(_remote(g_ref.at[:, pl.ds((1 - c) * rh, rh), :], sib_ref, sems, a, (x, y, 1 - c)))
        if m:
            small_ref, all_ref = refs[n], refs[2 * n + m]
            k = n
            for fx in (0, 1):
                for fy in (0, 1):
                    for fc in (0, 1):
                        if fx or fy or fc:
                            to = (1 - x if fx else x, 1 - y if fy else y, 1 - c if fc else c)
                            copies.append(_remote(small_ref, all_ref.at[me], sems, k, to))
                            k += 1
        for cp in copies:
            cp.start()
        for cp in copies:
            cp.wait_recv()
        for cp in copies:
            cp.wait_send()

    n_copies = n + m * (N_DEV - 1)
    args = list(gs) + ([small] if m else [])
    return pl.pallas_call(
        body, name=name, in_specs=[ANY] * (n + m), out_specs=[ANY] * (n + m),
        out_shape=[jax.ShapeDtypeStruct((N_CHIPS, g.shape[1] // 2, g.shape[2]), F32) for g in gs]
        + ([jax.ShapeDtypeStruct((N_DEV,) + small.shape, F32)] if m else []),
        scratch_shapes=[pltpu.SemaphoreType.DMA((n_copies,)), pltpu.SemaphoreType.DMA((n_copies,))],
    )(*args)


def _share_halves(ghs, name):
    n = len(ghs)

    def body(*refs):
        gh_refs, out_refs, sems = refs[:n], refs[n:2 * n], refs[2 * n:]
        x, y, c = _place()
        copies = [_remote(gh_ref, o_ref, sems, a, (x, y, 1 - c)) for a, (gh_ref, o_ref) in enumerate(zip(gh_refs, out_refs))]
        for cp in copies:
            cp.start()
        for cp in copies:
            cp.wait_recv()
        for cp in copies:
            cp.wait_send()

    return pl.pallas_call(
        body, name=name, in_specs=[ANY] * n, out_specs=[ANY] * n,
        out_shape=[jax.ShapeDtypeStruct(g.shape, g.dtype) for g in ghs],
        scratch_shapes=[pltpu.SemaphoreType.DMA((n,)), pltpu.SemaphoreType.DMA((n,))],
    )(*ghs)


EW_BLOCK_BYTES = 2 * 1024 * 1024


def _row_block(rows, cols, mult=8):
    fits = [b for b in range(mult, rows + 1, mult) if rows % b == 0 and b * cols * 4 <= EW_BLOCK_BYTES]
    return max(fits) if fits else mult


def _add2(a, b, name):
    R, C = a.shape
    rows = _row_block(R, C, mult=16)
    spec = pl.BlockSpec((rows, C), lambda i: (i, 0))

    def body(a_ref, b_ref, o_ref):
        o_ref[...] = (a_ref[...] + b_ref[...]).astype(BF16)

    return pl.pallas_call(
        body, name=name, grid=(R // rows,), in_specs=[spec, spec], out_specs=spec,
        out_shape=jax.ShapeDtypeStruct(a.shape, BF16),
        compiler_params=_cparams(("parallel",)),
    )(a, b)


def _sum_leading(a, name):
    n, R, C = a.shape
    rows = _row_block(R, n * C, mult=16 if a.dtype == BF16 else 8)

    def body(a_ref, o_ref):
        acc = a_ref[0].astype(F32)
        for j in range(1, n):
            acc = acc + a_ref[j].astype(F32)
        o_ref[...] = acc

    return pl.pallas_call(
        body, name=name, grid=(R // rows,),
        in_specs=[pl.BlockSpec((n, rows, C), lambda i: (0, i, 0))],
        out_specs=pl.BlockSpec((rows, C), lambda i: (i, 0)),
        out_shape=jax.ShapeDtypeStruct((R, C), F32),
        compiler_params=_cparams(("parallel",)),
    )(a)


def _adamw(w, g, m, v, name):
    R, C = w.shape
    rows = _row_block(R, C)
    spec = pl.BlockSpec((rows, C), lambda i: (i, 0))

    def body(w_ref, g_ref, m_ref, v_ref, d_ref, mo_ref, vo_ref):
        gv = g_ref[...]
        mn = ADAM_B1 * m_ref[...] + (1.0 - ADAM_B1) * gv
        vn = ADAM_B2 * v_ref[...] + (1.0 - ADAM_B2) * (gv * gv)
        m_hat = mn / (1.0 - ADAM_B1 ** ADAM_STEP)
        v_hat = vn / (1.0 - ADAM_B2 ** ADAM_STEP)
        d_ref[...] = -ADAM_LR * (m_hat / (jnp.sqrt(v_hat) + ADAM_EPS) + ADAM_WD * w_ref[...])
        mo_ref[...] = mn
        vo_ref[...] = vn

    return pl.pallas_call(
        body, name=name, grid=(R // rows,), in_specs=[spec] * 4, out_specs=[spec] * 3,
        out_shape=[jax.ShapeDtypeStruct((R, C), F32)] * 3,
        compiler_params=_cparams(("parallel",)),
    )(w, g, m, v)


BIG = ("w_in", "w_branch_a", "w_branch_b", "w_out", "w_ffn_up", "w_ffn_down")
COL_SHARDED = {"w_in": True, "w_branch_a": True, "w_branch_b": True, "w_out": False, "w_ffn_up": True,
               "w_ffn_down": False}
CONV_W_COLS = 2 * D_FF // N_CHIPS
SMALL_REPLICATED = (("norm1_g", D_MODEL), ("q_norm_g", HEAD_DIM), ("k_norm_g", HEAD_DIM),
                    ("rel_bias", N_HEADS * N_REL), ("norm2_g", D_MODEL), ("ffn_conv_b", 2 * D_FF))
SMALL_GRADS = SMALL_REPLICATED + (("ffn_conv_w", 3 * 2 * D_FF),)
SMALL_OWN = SMALL_REPLICATED + (("ffn_conv_w", 3 * CONV_W_COLS),)
SMALL_GRAD_ROWS = 32
SMALL_OWN_ROWS = 16


def _whole(name, stacked):
    return jnp.concatenate(list(stacked), axis=1) if COL_SHARDED[name] else stacked.reshape(-1, stacked.shape[2])


def _pack_small(vals, sizes, rows):
    flat = jnp.concatenate([vals[n].reshape(-1) for n, _ in sizes])
    return jnp.pad(flat, (0, rows * PACK_COLS - flat.shape[0])).reshape(rows, PACK_COLS)


def _unpack_small(packed, sizes):
    flat, out, o = packed.reshape(-1), {}, 0
    for n, sz in sizes:
        out[n] = flat[o:o + sz]
        o += sz
    return out


def kernel(x, norm1_g, w_in, q_norm_g, k_norm_g, rel_bias, w_branch_a, w_branch_b, w_out, norm2_g, w_ffn_up, ffn_conv_w, ffn_conv_b, w_ffn_down, loss_target, m_norm1_g, m_w_in, m_q_norm_g, m_k_norm_g, m_rel_bias, m_w_branch_a, m_w_branch_b, m_w_out, m_norm2_g, m_w_ffn_up, m_ffn_conv_w, m_ffn_conv_b, m_w_ffn_down, v_norm1_g, v_w_in, v_q_norm_g, v_k_norm_g, v_rel_bias, v_w_branch_a, v_w_branch_b, v_w_out, v_norm2_g, v_w_ffn_up, v_ffn_conv_w, v_ffn_conv_b, v_w_ffn_down):
    w_big = {"w_in": w_in[0], "w_branch_a": w_branch_a[0], "w_branch_b": w_branch_b[0], "w_out": w_out[0],
             "w_ffn_up": w_ffn_up[0], "w_ffn_down": w_ffn_down[0]}
    m_big = {"w_in": m_w_in[0], "w_branch_a": m_w_branch_a[0], "w_branch_b": m_w_branch_b[0], "w_out": m_w_out[0],
             "w_ffn_up": m_w_ffn_up[0], "w_ffn_down": m_w_ffn_down[0]}
    v_big = {"w_in": v_w_in[0], "w_branch_a": v_w_branch_a[0], "w_branch_b": v_w_branch_b[0], "w_out": v_w_out[0],
             "w_ffn_up": v_w_ffn_up[0], "w_ffn_down": v_w_ffn_down[0]}
    xs, tgt = x[0], loss_target[0]

    xi, yi, ci = _place()
    chip = 2 * xi + yi

    def with_own(stacked, own):
        return lax.dynamic_update_slice(stacked, own[None], (chip,) + (0,) * own.ndim)

    shards_bf = {n: w_big[n].astype(BF16) for n in BIG}
    conv_own = jnp.pad(ffn_conv_w[0], ((0, 8 - ffn_conv_w.shape[1]), (0, 0)))
    later = [n for n in BIG if n != "w_in"]

    hn, hn_t, w_in_g = _rms_fwd(xs, norm1_g, "rms1", gather=_Gather([shards_bf["w_in"]]))
    w_in_f = _whole("w_in", with_own(w_in_g, shards_bf["w_in"]))
    w_in_t = w_in_f.T
    gq = jnp.tile(q_norm_g, (1, N_HEADS))
    gk = jnp.tile(k_norm_g, (1, N_HEADS))
    qk, qa, ka = _proj_qknorm_fwd(hn, w_in_f[:, :2 * WIDTH], gq, gk, "proj_qk_qknorm")
    vqkv = _matmul(hn, w_in_f[:, 2 * WIDTH:6 * WIDTH], BF16, "proj_vqkv")
    gates = _matmul(hn, w_in_f[:, 6 * WIDTH:], BF16, "proj_gates")
    per = WIDTH // PAIR
    b_offs = (per, 2 * per, 3 * per)
    tab = _bias_table(jnp.pad(rel_bias[0], ((0, 0), (0, REL_PAD - N_REL))), "bias_table")
    out_a, out_a_t, *gathered = _ca_fwd(qa, ka, vqkv, tab, "chunk_attn",
                                        gather=_Gather([shards_bf[n] for n in later], [conv_own]))
    full = {n: _whole(n, with_own(g, shards_bf[n])) for n, g in zip(later, gathered)}
    conv_w = jnp.concatenate(list(with_own(gathered[-1], conv_own)[:, :3]), axis=1)
    w_a, w_b, w_o, w_up, w_dn = (full[n] for n in later)
    w_a_t, w_b_t, w_o_t, w_up_t, w_dn_t = (w.T for w in (w_a, w_b, w_o, w_up, w_dn))
    out_b, out_b_f32, out_b_t = _sb_fwd(vqkv, vqkv, vqkv, "stick_attn", b_offs)
    y_a, y_b, mixed, mixed_t = _branch_mix_fwd(out_a, w_a, out_b, w_b, gates, "branch_mix")
    x2, hn2, hn2_t = _proj_rms_fwd(mixed, w_o, xs, norm2_g, "out_proj_rms2")
    hid = _matmul(hn2, w_up, BF16, "ffn_up")
    act, act_t = _convglu_fwd(hid, conv_w, ffn_conv_b, "convglu")
    dy, dyb, sq = _down_and_loss(act, w_dn, x2, tgt, "ffn_down_loss")
    loss = lax.psum(0.5 / D_MODEL * jnp.sum(sq), ("x", "y", "c"))

    dact = _matmul(dyb, w_dn_t, BF16, "d_act")
    d_w_dn = _matmul(act_t, dyb, F32, "d_w_down")
    dhg, dhu, dcwg, dcwu, dcbg, dcbu = _convglu_bwd(hid, dact, conv_w, ffn_conv_b, "convglu_bwd")
    half_chips = N_CHIPS // 2
    d_w_up = jnp.concatenate([_matmul(hn2_t, dhg, F32, "d_w_up_gate", slabs=half_chips),
                              _matmul(hn2_t, dhu, F32, "d_w_up_up", slabs=half_chips)], axis=0)
    dx2, dx2b, d_norm2 = _rms_bwd([(dhg, w_up_t[:D_FF]), (dhu, w_up_t[D_FF:])], x2, norm2_g, dy, "d_hn2_rms2_bwd")
    d_w_o = _matmul(mixed_t, dx2b, F32, "d_w_out")
    dgates, dya, dyb_b = _mix_bwd(dx2b, w_o_t, gates, y_a, y_b, "d_mixed_mix_bwd")
    d_w_a, d_w_b = (_matmul(o_t, d, F32, nm).reshape(WIDTH, N_CHIPS, -1).transpose(1, 0, 2)
                    for o_t, d, nm in ((out_a_t, dya, "d_w_branch_a"), (out_b_t, dyb_b, "d_w_branch_b")))
    do_a = _matmul(dya, w_a_t, BF16, "d_out_a")
    do_b = _matmul(dyb_b, w_b_t, BF16, "d_out_b")

    def core_sums(names, gs, sibs):
        out = {}
        for n, g, sib in zip(names, gs, sibs):
            rh, cols = sib.shape[1], sib.shape[2]
            mine = lax.dynamic_slice_in_dim(g, ci * rh, rh, axis=1)
            out[n] = _add2(mine.reshape(-1, cols), sib.reshape(-1, cols), "sum_cores_" + n).reshape(sib.shape)
        return out

    grads_full = {"w_branch_a": d_w_a, "w_branch_b": d_w_b, "w_ffn_up": d_w_up,
                  "w_out": d_w_o.reshape(N_CHIPS, -1, D_MODEL), "w_ffn_down": d_w_dn.reshape(N_CHIPS, -1, D_MODEL)}
    early = [grads_full[n] for n in later]
    chip_parts = core_sums(later, early, _exchange_cores(early, "exchange_cores_early"))
    dqb, dkb, dvb, *parts_early = _sb_bwd(vqkv, vqkv, vqkv, out_b_f32, do_b, "stick_attn_bwd", b_offs,
                                           exchange=_ChipExchange([chip_parts[n] for n in later]))
    parts = dict(zip(later, parts_early))
    dqa_n, dka_n, dva, dtab = _ca_bwd(qa, ka, vqkv, do_a, tab, "chunk_attn_bwd")
    d_rel = _bias_table_bwd(dtab, "bias_table_bwd")[:, :N_REL]
    dqa, dka, dgq, dgk = _qknorm_bwd(qk, gq, gk, dqa_n, dka_n, "qknorm_bwd")
    pieces = (("qa", dqa), ("ka", dka), ("va", dva), ("qb", dqb), ("kb", dkb), ("vb", dvb), ("gates", dgates))
    d_w_in = jnp.concatenate([_matmul(hn_t, d, F32, "d_w_in_" + nm) for nm, d in pieces], axis=1)
    d_w_in = d_w_in.reshape(D_MODEL, N_CHIPS, -1).transpose(1, 0, 2)
    chip_parts.update(core_sums(["w_in"], [d_w_in], _exchange_cores([d_w_in], "exchange_cores_w_in")))
    offsets = [sum(d.shape[1] for _, d in pieces[:k]) for k in range(len(pieces))]
    dx, _, d_norm1, parts["w_in"] = _rms_bwd([(d, w_in_t[o:o + d.shape[1]]) for (_, d), o in zip(pieces, offsets)],
                                             xs, norm1_g, dx2, "d_hn_rms1_bwd",
                                             exchange=_ChipExchange([chip_parts["w_in"]]))

    small_g = _pack_small({"norm1_g": d_norm1, "q_norm_g": dgq.reshape(N_HEADS, HEAD_DIM).sum(0),
                           "k_norm_g": dgk.reshape(N_HEADS, HEAD_DIM).sum(0), "rel_bias": d_rel,
                           "norm2_g": d_norm2, "ffn_conv_b": jnp.concatenate([dcbg, dcbu], axis=1),
                           "ffn_conv_w": jnp.concatenate([dcwg, dcwu], axis=1)}, SMALL_GRADS, SMALL_GRAD_ROWS)
    (small_all,) = _exchange_cores([], "exchange_small", small=small_g)
    g_halves = [_sum_leading(with_own(parts[n], lax.dynamic_index_in_dim(chip_parts[n], chip, 0, keepdims=False)),
                             "sum_chips_" + n) for n in BIG]
    g_others = _share_halves(g_halves, "share_halves")
    grads = {n: jnp.concatenate([jnp.where(ci == 0, mine, other), jnp.where(ci == 0, other, mine)], axis=0)
             for n, mine, other in zip(BIG, g_halves, g_others)}
    small_all = lax.dynamic_update_slice(small_all, small_g[None], (4 * xi + 2 * yi + ci, 0, 0))
    small_sum = _unpack_small(_sum_leading(small_all, "sum_small"), SMALL_GRADS)
    small_sum["ffn_conv_w"] = lax.dynamic_slice_in_dim(small_sum["ffn_conv_w"].reshape(3, 2 * D_FF),
                                                       chip * CONV_W_COLS, CONV_W_COLS, axis=1)

    deltas, new_m, new_v = {}, {}, {}
    for n in BIG:
        deltas[n], new_m[n], new_v[n] = _adamw(w_big[n], grads[n], m_big[n], v_big[n], "adamw_" + n)

    shapes = {"norm1_g": norm1_g.shape, "q_norm_g": q_norm_g.shape, "k_norm_g": k_norm_g.shape,
              "rel_bias": rel_bias.shape, "norm2_g": norm2_g.shape, "ffn_conv_b": ffn_conv_b.shape,
              "ffn_conv_w": ffn_conv_w.shape}
    small_w = {"norm1_g": norm1_g, "q_norm_g": q_norm_g, "k_norm_g": k_norm_g, "rel_bias": rel_bias,
               "norm2_g": norm2_g, "ffn_conv_b": ffn_conv_b, "ffn_conv_w": ffn_conv_w}
    small_m = {"norm1_g": m_norm1_g, "q_norm_g": m_q_norm_g, "k_norm_g": m_k_norm_g, "rel_bias": m_rel_bias,
               "norm2_g": m_norm2_g, "ffn_conv_b": m_ffn_conv_b, "ffn_conv_w": m_ffn_conv_w}
    small_v = {"norm1_g": v_norm1_g, "q_norm_g": v_q_norm_g, "k_norm_g": v_k_norm_g, "rel_bias": v_rel_bias,
               "norm2_g": v_norm2_g, "ffn_conv_b": v_ffn_conv_b, "ffn_conv_w": v_ffn_conv_w}
    ds, ms, vs = _adamw(*(_pack_small(t, SMALL_OWN, SMALL_OWN_ROWS) for t in (small_w, small_sum, small_m, small_v)),
                        "adamw_small")
    small_grads = small_sum
    ds, ms, vs = (_unpack_small(t, SMALL_OWN) for t in (ds, ms, vs))

    order = ("norm1_g", "w_in", "q_norm_g", "k_norm_g", "rel_bias", "w_branch_a", "w_branch_b", "w_out",
             "norm2_g", "w_ffn_up", "ffn_conv_w", "ffn_conv_b", "w_ffn_down")
    outs = [loss, dx[None]]
    for big, small in ((grads, small_grads), (deltas, ds), (new_m, ms), (new_v, vs)):
        for n in order:
            outs.append(big[n][None] if n in big else small[n].reshape(shapes[n]))
    return tuple(outs)
```

```python
import functools

import jax
import jax.numpy as jnp
from jax import lax
from jax.experimental import pallas as pl
from jax.experimental.pallas import tpu as pltpu

F32 = jnp.float32
BF16 = jnp.bfloat16
MESH = pl.DeviceIdType.MESH

D_MODEL = 1024
HEAD_DIM = 64
N_HEADS = 8
WIDTH = N_HEADS * HEAD_DIM
CHUNK = 64
LEFT_CHUNKS = 8
MAX_REL = 128
N_REL = 2 * MAX_REL + 1
D_FF = 2816
EPS = 1e-6
NEG = -1e30

ADAM_LR = 0.001
ADAM_B1 = 0.9
ADAM_B2 = 0.999
ADAM_EPS = 1e-08
ADAM_WD = 0.01
ADAM_STEP = 10

N_CHIPS = 4
N_DEV = 8
LANES = 128
PAIR = 2 * HEAD_DIM
BQ = 256
BAND = LEFT_CHUNKS * CHUNK
KWIN = BAND + BQ
VMEM_LIMIT = 56 * 1024 * 1024
PACK_COLS = 1024

NN = (((1,), (0,)), ((), ()))
NT = (((1,), (1,)), ((), ()))
TN = (((0,), (0,)), ((), ()))


def _cparams(sem=None):
    if sem is None:
        return pltpu.CompilerParams(vmem_limit_bytes=VMEM_LIMIT)
    return pltpu.CompilerParams(dimension_semantics=sem, vmem_limit_bytes=VMEM_LIMIT)


def _pick(n, cands):
    for c in cands:
        if n % c == 0:
            return c
    raise ValueError(f"no block for {n}")


def _dot(a, b, dn):
    return lax.dot_general(a, b, dn, preferred_element_type=F32)


def _sigmoid(x):
    return 0.5 * jnp.tanh(0.5 * x) + 0.5


def _split_bf16(x):
    hi = x.astype(BF16)
    lo = (x - hi.astype(F32)).astype(BF16)
    return hi, lo


MM_RESIDENT_BYTES = 12 * 1024 * 1024
MM_TILE_BYTES = 4 * 1024 * 1024


def _matmul(a, b, out_dtype, name, slabs=None):
    (M, K), N = a.shape, b.shape[1]
    out_bytes = jnp.dtype(out_dtype).itemsize
    if slabs is None and N <= D_FF and K * N * 2 <= MM_RESIDENT_BYTES:
        bk, bn = K, N
        bm = next(c for c in (1024, 512, 256, 128)
                  if M % c == 0 and c * K * 2 <= MM_TILE_BYTES and c * N * out_bytes <= MM_TILE_BYTES)
        b_spec = pl.BlockSpec((bk, bn), lambda i, j, k: (0, 0), pipeline_mode=pl.Buffered(1))
    elif slabs is None and K <= D_FF:
        bk, bm, bn = K, _pick(M, (1024, 512)), _pick(N, (D_FF // 2, 512, 256, 128))
        b_spec = pl.BlockSpec((bk, bn), lambda i, j, k: (k, j))
    else:
        bk = _pick(K, (2048, 1024, 512))
        bm = _pick(M, (D_FF // 2, 1024, 512, 256, 128))
        bn = N // slabs if slabs else _pick(N, (D_FF // 2, 1024, 512, 256, 128))
        b_spec = pl.BlockSpec((bk, bn), lambda i, j, k: (k, j))
    nk = K // bk
    dn = NN
    a_spec = pl.BlockSpec((bm, bk), lambda i, j, k: (i, k))
    if slabs:
        o_spec = pl.BlockSpec((None, bm, bn), lambda i, j, k: (j, i, 0))
        out_shape = jax.ShapeDtypeStruct((slabs, M, bn), out_dtype)
    else:
        o_spec = pl.BlockSpec((bm, bn), lambda i, j, k: (i, j))
        out_shape = jax.ShapeDtypeStruct((M, N), out_dtype)

    def body(a_ref, b_ref, o_ref, acc_ref):
        k = pl.program_id(2)
        part = _dot(a_ref[...], b_ref[...], dn)
        if nk == 1:
            o_ref[...] = part.astype(out_dtype)
        else:
            @pl.when(k == 0)
            def _():
                acc_ref[...] = part

            @pl.when(k > 0)
            def _():
                acc_ref[...] += part

            @pl.when(k == nk - 1)
            def _():
                o_ref[...] = acc_ref[...].astype(out_dtype)

    return pl.pallas_call(
        body, name=name, grid=(M // bm, N // bn, nk),
        in_specs=[a_spec, b_spec], out_specs=o_spec, out_shape=out_shape,
        scratch_shapes=[pltpu.VMEM((bm, bn) if nk > 1 else (8, LANES), F32)],
        compiler_params=_cparams(("parallel", "parallel", "arbitrary")),
    )(a, b)


ROWS = 512


def _row_spec(cols, bm=ROWS):
    return pl.BlockSpec((bm, cols), lambda i: (i, 0))


def _col_spec(rows, bn=ROWS):
    return pl.BlockSpec((rows, bn), lambda i: (0, i))


def _full_spec(shape):
    return pl.BlockSpec(shape, lambda i: (0,) * len(shape))


def _colsum8(t):
    return jnp.sum(t.reshape(t.shape[0] // 8, 8, t.shape[1]), axis=0)


def _proj_rms_fwd(a, w, res, g, name):
    (S, K), D = a.shape, w.shape[1]

    def body(a_ref, w_ref, res_ref, g_ref, x_ref, o_ref, ot_ref):
        xv = _dot(a_ref[...], w_ref[...], NN) + res_ref[...]
        x_ref[...] = xv
        r = lax.rsqrt(jnp.mean(xv * xv, axis=-1, keepdims=True) + EPS)
        y = xv * r * g_ref[...]
        o_ref[...] = y.astype(BF16)
        ot_ref[...] = y.T.astype(BF16)

    return pl.pallas_call(
        body, name=name, grid=(S // ROWS,),
        in_specs=[_row_spec(K), pl.BlockSpec(w.shape, lambda i: (0, 0), pipeline_mode=pl.Buffered(1)),
                  _row_spec(D), _full_spec((1, D))],
        out_specs=[_row_spec(D), _row_spec(D), _col_spec(D)],
        out_shape=[jax.ShapeDtypeStruct((S, D), F32), jax.ShapeDtypeStruct((S, D), BF16),
                   jax.ShapeDtypeStruct((D, S), BF16)],
        compiler_params=_cparams(("parallel",)),
    )(a, w, res, g)


def _rms_fwd(x, g, name, gather=None):
    S, D = x.shape
    nt = S // ROWS
    n_side = len(gather.arrays) if gather else 0

    def body(*refs):
        x_ref, g_ref = refs[:2]
        o_ref, ot_ref = refs[2 + n_side:4 + n_side]
        i = pl.program_id(0)
        if gather:
            start, forward, finish = gather.steps(refs[2:2 + n_side], refs[4 + n_side:4 + 2 * n_side], refs[4 + 2 * n_side:])
            pl.when(i == 0)(start)
            pl.when(i == 3 * nt // 4)(forward)
        xv = x_ref[...]
        r = lax.rsqrt(jnp.mean(xv * xv, axis=-1, keepdims=True) + EPS)
        y = xv * r * g_ref[...]
        o_ref[...] = y.astype(BF16)
        ot_ref[...] = y.T.astype(BF16)
        if gather:
            pl.when(i == nt - 1)(finish)

    side = gather.arrays if gather else []
    return pl.pallas_call(
        body, name=name, grid=(nt,),
        in_specs=[_row_spec(D), _full_spec((1, D))] + [ANY] * n_side,
        out_specs=[_row_spec(D), _col_spec(D)] + [ANY] * n_side,
        out_shape=[jax.ShapeDtypeStruct((S, D), BF16), jax.ShapeDtypeStruct((D, S), BF16)]
        + (gather.out_shape if gather else []),
        scratch_shapes=gather.scratch if gather else [],
        compiler_params=_cparams(("arbitrary",) if gather else ("parallel",)),
    )(x, g, *side)


RMS_BWD_ROWS = 256


def _rms_bwd(pairs, x, g, dres, name, exchange=None):
    S, D = x.shape
    bm = RMS_BWD_ROWS
    nt = S // bm
    n_pairs = len(pairs)
    n_side = len(exchange.arrays) if exchange else 0
    n_in = 2 * n_pairs + 3

    def body(*refs):
        x_ref, g_ref, dres_ref = refs[2 * n_pairs:n_in]
        dx_ref, dxb_ref, dg_ref = refs[n_in + n_side:n_in + n_side + 3]
        acc_ref = refs[n_in + 2 * n_side + 3]
        i = pl.program_id(0)
        if exchange:
            start, finish = exchange.steps(refs[n_in:n_in + n_side], refs[n_in + n_side + 3:n_in + 2 * n_side + 3],
                                           refs[n_in + 2 * n_side + 4:])
            pl.when(i == 0)(start)
        dyv = sum(_dot(refs[2 * p][...], refs[2 * p + 1][...], NN) for p in range(n_pairs))
        xv = x_ref[...]
        r = lax.rsqrt(jnp.mean(xv * xv, axis=-1, keepdims=True) + EPS)
        xr = xv * r
        u = dyv * g_ref[...]
        dx = r * u - xr * (r * r) * jnp.mean(xv * u, axis=-1, keepdims=True) + dres_ref[...]
        dx_ref[...] = dx
        dxb_ref[...] = dx.astype(BF16)
        part = _colsum8(dyv * xr)

        @pl.when(i == 0)
        def _():
            acc_ref[...] = part

        @pl.when(i > 0)
        def _():
            acc_ref[...] += part

        @pl.when(i == nt - 1)
        def _():
            dg_ref[...] = jnp.sum(acc_ref[...], axis=0, keepdims=True)

        if exchange:
            pl.when(i == nt - 1)(finish)

    rows = lambda cols: pl.BlockSpec((bm, cols), lambda i: (i, 0))
    in_specs = []
    for a, b in pairs:
        in_specs += [rows(a.shape[1]), pl.BlockSpec(b.shape, lambda i: (0, 0), pipeline_mode=pl.Buffered(1))]
    side = exchange.arrays if exchange else []
    return pl.pallas_call(
        body, name=name, grid=(nt,),
        in_specs=in_specs + [rows(D), _full_spec((1, D)), rows(D)] + [ANY] * n_side,
        out_specs=[rows(D), rows(D), _full_spec((1, D))] + [ANY] * n_side,
        out_shape=[jax.ShapeDtypeStruct((S, D), F32), jax.ShapeDtypeStruct((S, D), BF16),
                   jax.ShapeDtypeStruct((1, D), F32)] + (exchange.out_shape if exchange else []),
        scratch_shapes=[pltpu.VMEM((8, D), F32)] + (exchange.scratch if exchange else []),
        compiler_params=_cparams(("arbitrary",)),
    )(*[t for pair in pairs for t in pair], x, g, dres, *side)


def _head_mean(t, blockdiag):
    hi, lo = _split_bf16(t)
    return (_dot(hi, blockdiag, NN) + _dot(lo, blockdiag, NN)) * (1.0 / HEAD_DIM)


def _blockdiag():
    r = lax.broadcasted_iota(jnp.int32, (WIDTH, WIDTH), 0) // HEAD_DIM
    c = lax.broadcasted_iota(jnp.int32, (WIDTH, WIDTH), 1) // HEAD_DIM
    return jnp.where(r == c, 1.0, 0.0).astype(BF16)


def _proj_qknorm_fwd(hn, w, gq, gk, name):
    S, K = hn.shape

    def body(hn_ref, w_ref, gq_ref, gk_ref, qk_ref, q_ref, k_ref):
        bd = _blockdiag()
        qk = _dot(hn_ref[...], w_ref[...], NN)
        qk_ref[...] = qk
        for part, g_ref, o_ref, scale in ((0, gq_ref, q_ref, HEAD_DIM ** -0.5), (1, gk_ref, k_ref, 1.0)):
            t = qk[:, part * WIDTH:(part + 1) * WIDTH]
            r = lax.rsqrt(_head_mean(t * t, bd) + EPS)
            o_ref[...] = (t * r * g_ref[...] * scale).astype(BF16)

    return pl.pallas_call(
        body, name=name, grid=(S // ROWS,),
        in_specs=[_row_spec(K), pl.BlockSpec(w.shape, lambda i: (0, 0), pipeline_mode=pl.Buffered(1)),
                  _full_spec((1, WIDTH)), _full_spec((1, WIDTH))],
        out_specs=[_row_spec(2 * WIDTH), _row_spec(WIDTH), _row_spec(WIDTH)],
        out_shape=[jax.ShapeDtypeStruct((S, 2 * WIDTH), F32)] + [jax.ShapeDtypeStruct((S, WIDTH), BF16)] * 2,
        compiler_params=_cparams(("parallel",)),
    )(hn, w, gq, gk)


def _qknorm_bwd(qk, gq, gk, dqn, dkn, name):
    S = qk.shape[0]
    nt = S // ROWS

    def body(qk_ref, gq_ref, gk_ref, dqn_ref, dkn_ref, dq_ref, dk_ref, dgq_ref, dgk_ref, accq_ref, acck_ref):
        i = pl.program_id(0)
        bd = _blockdiag()
        for part, g_ref, dn_ref, o_ref, dg_ref, acc_ref, scale in (
                (0, gq_ref, dqn_ref, dq_ref, dgq_ref, accq_ref, HEAD_DIM ** -0.5),
                (1, gk_ref, dkn_ref, dk_ref, dgk_ref, acck_ref, 1.0)):
            t = qk_ref[:, part * WIDTH:(part + 1) * WIDTH]
            dn = dn_ref[...] * scale
            r = lax.rsqrt(_head_mean(t * t, bd) + EPS)
            u = dn * g_ref[...]
            dt = r * u - t * (r * r * r) * _head_mean(t * u, bd)
            o_ref[...] = dt.astype(BF16)
            psum = _colsum8(dn * t * r)

            @pl.when(i == 0)
            def _():
                acc_ref[...] = psum

            @pl.when(i > 0)
            def _():
                acc_ref[...] += psum

            @pl.when(i == nt - 1)
            def _():
                dg_ref[...] = jnp.sum(acc_ref[...], axis=0, keepdims=True)

    return pl.pallas_call(
        body, name=name, grid=(nt,),
        in_specs=[_row_spec(2 * WIDTH), _full_spec((1, WIDTH)), _full_spec((1, WIDTH)),
                  _row_spec(WIDTH), _row_spec(WIDTH)],
        out_specs=[_row_spec(WIDTH), _row_spec(WIDTH), _full_spec((1, WIDTH)), _full_spec((1, WIDTH))],
        out_shape=[jax.ShapeDtypeStruct((S, WIDTH), BF16)] * 2 + [jax.ShapeDtypeStruct((1, WIDTH), F32)] * 2,
        scratch_shapes=[pltpu.VMEM((8, WIDTH), F32)] * 2,
        compiler_params=_cparams(("arbitrary",)),
    )(qk, gq, gk, dqn, dkn)


def _gate_specs(D):
    return [pl.BlockSpec((ROWS, D), lambda i: (i, 0)), pl.BlockSpec((ROWS, D), lambda i: (i, 1))]


def _branch_mix_fwd(out_a, w_a, out_b, w_b, gates, name):
    (S, K), D = out_a.shape, w_a.shape[1]
    wspec = pl.BlockSpec(w_a.shape, lambda i: (0, 0), pipeline_mode=pl.Buffered(1))

    def body(oa_ref, wa_ref, ob_ref, wb_ref, ga_ref, gb_ref, ya_ref, yb_ref, o_ref, ot_ref):
        ya = _dot(oa_ref[...], wa_ref[...], NN)
        yb = _dot(ob_ref[...], wb_ref[...], NN)
        ya_ref[...] = ya.astype(BF16)
        yb_ref[...] = yb.astype(BF16)
        m = _sigmoid(ga_ref[...].astype(F32)) * ya + _sigmoid(gb_ref[...].astype(F32)) * yb
        o_ref[...] = m.astype(BF16)
        ot_ref[...] = m.T.astype(BF16)

    return pl.pallas_call(
        body, name=name, grid=(S // ROWS,),
        in_specs=[_row_spec(K), wspec, _row_spec(K), wspec] + _gate_specs(D),
        out_specs=[_row_spec(D), _row_spec(D), _row_spec(D), _col_spec(D)],
        out_shape=[jax.ShapeDtypeStruct((S, D), BF16)] * 3 + [jax.ShapeDtypeStruct((D, S), BF16)],
        compiler_params=_cparams(("parallel",)),
    )(out_a, w_a, out_b, w_b, gates, gates)


def _mix_bwd(dx, w_t, gates, ya, yb, name):
    S, D = ya.shape

    def body(dx_ref, w_ref, ga_ref, gb_ref, ya_ref, yb_ref, dg_ref, dya_ref, dyb_ref):
        dmv = _dot(dx_ref[...], w_ref[...], NN)
        for half, (g_ref, y_ref, dy_ref) in enumerate(((ga_ref, ya_ref, dya_ref), (gb_ref, yb_ref, dyb_ref))):
            s = _sigmoid(g_ref[...].astype(F32))
            dy_ref[...] = (dmv * s).astype(BF16)
            dg_ref[:, half * D:(half + 1) * D] = (dmv * y_ref[...].astype(F32) * s * (1.0 - s)).astype(BF16)

    return pl.pallas_call(
        body, name=name, grid=(S // ROWS,),
        in_specs=[_row_spec(D), pl.BlockSpec(w_t.shape, lambda i: (0, 0), pipeline_mode=pl.Buffered(1))]
        + _gate_specs(D) + [_row_spec(D)] * 2,
        out_specs=[_row_spec(2 * D), _row_spec(D), _row_spec(D)],
        out_shape=[jax.ShapeDtypeStruct((S, 2 * D), BF16)] + [jax.ShapeDtypeStruct((S, D), BF16)] * 2,
        compiler_params=_cparams(("parallel",)),
    )(dx, w_t, gates, gates, ya, yb)


def _down_and_loss(act, w, x2, target, name):
    (S, K), D = act.shape, w.shape[1]
    nt = S // ROWS

    def body(a_ref, w_ref, x_ref, t_ref, dy_ref, dyb_ref, p_ref):
        err = _dot(a_ref[...], w_ref[...], NN) + x_ref[...] - t_ref[...]
        dy = err * (1.0 / D)
        dy_ref[...] = dy
        dyb_ref[...] = dy.astype(BF16)
        sq = _colsum8(err * err)
        acc = sq[:, 0:LANES]
        for k in range(1, D // LANES):
            acc = acc + sq[:, k * LANES:(k + 1) * LANES]
        p_ref[...] = acc

    return pl.pallas_call(
        body, name=name, grid=(nt,),
        in_specs=[_row_spec(K), pl.BlockSpec((K, D), lambda i: (0, 0), pipeline_mode=pl.Buffered(1)),
                  _row_spec(D), _row_spec(D)],
        out_specs=[_row_spec(D), _row_spec(D), pl.BlockSpec((8, LANES), lambda i: (i, 0))],
        out_shape=[jax.ShapeDtypeStruct((S, D), F32), jax.ShapeDtypeStruct((S, D), BF16),
                   jax.ShapeDtypeStruct((nt * 8, LANES), F32)],
        compiler_params=_cparams(("parallel",)),
    )(act, w, x2, target)


CONV_COLS = D_FF // 2
HALO = 16
CONV_CHUNK = 64


def _aligned(start, multiple):
    return start if isinstance(start, int) else pl.multiple_of(start, multiple)


def _conv_taps(xe, cw, cb):
    taps = (pltpu.roll(xe, 2, 0), pltpu.roll(xe, 1, 0), xe)
    return taps, cw[0:1] * taps[0] + cw[1:2] * taps[1] + cw[2:3] * taps[2] + cb


def _conv_specs(nt):
    hb, nb = ROWS // HALO, D_FF // CONV_COLS
    specs = {}
    for part, off in (("gate", 0), ("up", nb)):
        specs[part] = dict(
            main=pl.BlockSpec((ROWS, CONV_COLS), functools.partial(lambda c, i, off: (i, c + off), off=off)),
            prev=pl.BlockSpec((HALO, CONV_COLS),
                              functools.partial(lambda c, i, off: (jnp.maximum(i * hb - 1, 0), c + off), off=off)),
            nxt=pl.BlockSpec((HALO, CONV_COLS),
                             functools.partial(lambda c, i, off: (jnp.minimum((i + 1) * hb, nt * hb - 1), c + off), off=off)),
            w=pl.BlockSpec((3, CONV_COLS), functools.partial(lambda c, i, off: (0, c + off), off=off)),
            b=pl.BlockSpec((1, CONV_COLS), functools.partial(lambda c, i, off: (0, c + off), off=off)))
    return specs


def _convglu_fwd(hid, cw, cb, name):
    S = hid.shape[0]
    sp = _conv_specs(S // ROWS)

    def body(hg_ref, hgp_ref, hu_ref, hup_ref, cwg_ref, cwu_ref, cbg_ref, cbu_ref, o_ref, ot_ref):
        i = pl.program_id(1)
        keep = (i > 0).astype(F32)

        def conv(h_ref, hp_ref, cw_ref, cb_ref):
            xe = jnp.concatenate([hp_ref[...].astype(F32) * keep, h_ref[...].astype(F32)], axis=0)
            return _conv_taps(xe, cw_ref[...], cb_ref[...])[1][HALO:, :]

        gate = conv(hg_ref, hgp_ref, cwg_ref, cbg_ref)
        up = conv(hu_ref, hup_ref, cwu_ref, cbu_ref)
        act = gate * _sigmoid(gate) * up
        o_ref[...] = act.astype(BF16)
        ot_ref[...] = act.T.astype(BF16)

    g, u = sp["gate"], sp["up"]
    return pl.pallas_call(
        body, name=name, grid=(D_FF // CONV_COLS, S // ROWS),
        in_specs=[g["main"], g["prev"], u["main"], u["prev"], g["w"], u["w"], g["b"], u["b"]],
        out_specs=[g["main"], pl.BlockSpec((CONV_COLS, ROWS), lambda c, i: (c, i))],
        out_shape=[jax.ShapeDtypeStruct((S, D_FF), BF16), jax.ShapeDtypeStruct((D_FF, S), BF16)],
        compiler_params=_cparams(("parallel", "parallel")),
    )(hid, hid, hid, hid, cw, cw, cb, cb)


def _convglu_bwd(hid, dact, cw, cb, name):
    S = hid.shape[0]
    nt = S // ROWS
    sp = _conv_specs(nt)

    n_chunks = ROWS // CONV_CHUNK

    def body(hg_ref, hgp_ref, hgn_ref, hu_ref, hup_ref, hun_ref, da_ref, dan_ref,
             cwg_ref, cwu_ref, cbg_ref, cbu_ref,
             dhg_ref, dhu_ref, dcwg_ref, dcwu_ref, dcbg_ref, dcbu_ref, xg_s, xu_s, da_s):
        i = pl.program_id(1)
        kp = (i > 0).astype(F32)
        kn = (i < nt - 1).astype(F32)
        for x_s, h_ref, hp_ref, hn_ref in ((xg_s, hg_ref, hgp_ref, hgn_ref), (xu_s, hu_ref, hup_ref, hun_ref)):
            x_s[0:HALO, :] = hp_ref[...].astype(F32) * kp
            x_s[HALO:HALO + ROWS, :] = h_ref[...].astype(F32)
            x_s[HALO + ROWS:, :] = hn_ref[...].astype(F32) * kn
        da_s[0:ROWS, :] = da_ref[...].astype(F32)
        da_s[ROWS:, :] = dan_ref[...].astype(F32) * kn

        @pl.when(i == 0)
        def _():
            for ref in (dcwg_ref, dcwu_ref, dcbg_ref, dcbu_ref):
                ref[...] = jnp.zeros_like(ref)

        def lane_group(grp, _):
            lanes = pl.ds(pl.multiple_of(grp * LANES, LANES), LANES)
            cwg, cwu, cbg, cbu = cwg_ref[:, lanes], cwu_ref[:, lanes], cbg_ref[:, lanes], cbu_ref[:, lanes]

            def grads(r0, n):
                rows = pl.ds(_aligned(r0 + HALO - 8, 8), n + 8)
                taps_g, gate = _conv_taps(xg_s[rows, lanes], cwg, cbg)
                taps_u, up = _conv_taps(xu_s[rows, lanes], cwu, cbu)
                gate, up = gate[8:], up[8:]
                da = da_s[pl.ds(_aligned(r0, 8), n), lanes]
                sg = _sigmoid(gate)
                return (da * up * sg * (1.0 + gate * (1.0 - sg)), da * gate * sg,
                        [t[8:] for t in taps_g], [t[8:] for t in taps_u])

            def chunk(step, carry):
                below_g, below_u, accs = carry
                r0 = (n_chunks - 1 - step) * CONV_CHUNK
                dg, du, taps_g, taps_u = grads(r0, CONV_CHUNK)
                new_accs = []
                for d, below, cwv, taps, dh_ref, acc in ((dg, below_g, cwg, taps_g, dhg_ref, accs[0]),
                                                        (du, below_u, cwu, taps_u, dhu_ref, accs[1])):
                    ext = jnp.concatenate([d, below], axis=0)
                    n_ext = CONV_CHUNK + 8
                    dh = (cwv[2:3] * d + cwv[1:2] * pltpu.roll(ext, n_ext - 1, 0)[:CONV_CHUNK]
                          + cwv[0:1] * pltpu.roll(ext, n_ext - 2, 0)[:CONV_CHUNK])
                    dh_ref[pl.ds(_aligned(r0, CONV_CHUNK), CONV_CHUNK), lanes] = dh.astype(BF16)
                    new_accs.append(tuple(a + _colsum8(d * tap) for a, tap in zip(acc[:3], taps))
                                    + (acc[3] + _colsum8(d),))
                return dg[0:8], du[0:8], tuple(new_accs)

            below_g, below_u, _, _ = grads(ROWS, 8)
            zero = jnp.zeros((8, LANES), F32)
            _, _, accs = lax.fori_loop(0, n_chunks, chunk, (below_g, below_u, ((zero,) * 4, (zero,) * 4)))
            for acc, dcw_ref, dcb_ref in ((accs[0], dcwg_ref, dcbg_ref), (accs[1], dcwu_ref, dcbu_ref)):
                for t in range(3):
                    dcw_ref[t:t + 1, lanes] += jnp.sum(acc[t], axis=0, keepdims=True)
                dcb_ref[:, lanes] += jnp.sum(acc[3], axis=0, keepdims=True)
            return 0

        lax.fori_loop(0, CONV_COLS // LANES, lane_group, 0)

    g, u = sp["gate"], sp["up"]
    return pl.pallas_call(
        body, name=name, grid=(D_FF // CONV_COLS, nt),
        in_specs=[g["main"], g["prev"], g["nxt"], u["main"], u["prev"], u["nxt"], g["main"], g["nxt"],
                  g["w"], u["w"], g["b"], u["b"]],
        out_specs=[g["main"], g["main"], g["w"], g["w"], g["b"], g["b"]],
        out_shape=[jax.ShapeDtypeStruct((S, D_FF), BF16)] * 2 + [jax.ShapeDtypeStruct((3, D_FF), F32)] * 2
        + [jax.ShapeDtypeStruct((1, D_FF), F32)] * 2,
        scratch_shapes=[pltpu.VMEM((ROWS + 2 * HALO, CONV_COLS), F32)] * 2 + [pltpu.VMEM((ROWS + HALO, CONV_COLS), F32)],
        compiler_params=_cparams(("parallel", "arbitrary")),
    )(hid, hid, hid, hid, hid, hid, dact, dact, cw, cw, cb, cb)


REL_PAD = 384
DIAG = 1024


def _band_valid():
    qc = lax.broadcasted_iota(jnp.int32, (BQ, KWIN), 0) // CHUNK
    kc = lax.broadcasted_iota(jnp.int32, (BQ, KWIN), 1) // CHUNK - LEFT_CHUNKS
    return (kc <= qc) & (kc >= qc - LEFT_CHUNKS)


def _rel_index(offset):
    return jnp.clip(BAND - offset, -MAX_REL, MAX_REL) + MAX_REL


def _split3(x):
    hi = x.astype(BF16)
    rest = x - hi.astype(F32)
    mid = rest.astype(BF16)
    return hi, mid, (rest - mid.astype(F32)).astype(BF16)


def _bias_table(rel_bias, name):
    def body(rb_ref, o_ref):
        t = lax.broadcasted_iota(jnp.int32, (REL_PAD, DIAG), 0)
        lane = lax.broadcasted_iota(jnp.int32, (REL_PAD, DIAG), 1)
        pick = jnp.where(t == _rel_index(lane - BQ), 1.0, 0.0).astype(BF16)
        base = sum(_dot(piece, pick, NN) for piece in _split3(rb_ref[...]))
        valid = _band_valid()
        for h in range(N_HEADS):
            rows = jnp.broadcast_to(base[h:h + 1], (BQ, DIAG))
            rolled = pltpu.roll(rows, 0, 1, stride=1, stride_axis=0)
            o_ref[h] = jnp.where(valid, rolled[:, BQ:], NEG)

    return pl.pallas_call(
        body, name=name,
        out_shape=jax.ShapeDtypeStruct((N_HEADS, BQ, KWIN), F32),
        compiler_params=_cparams(),
    )(rel_bias)


def _bias_table_bwd(dtab, name):
    def body(d_ref, o_ref, diag_ref):
        r = lax.broadcasted_iota(jnp.int32, (BQ, BQ), 0)
        c = lax.broadcasted_iota(jnp.int32, (BQ, BQ), 1)
        flip = jnp.where(r + c == BQ - 1, 1.0, 0.0).astype(BF16)
        for h in range(N_HEADS):
            flipped = sum(_dot(flip, piece, NN) for piece in _split3(d_ref[h]))
            padded = jnp.concatenate([flipped, jnp.zeros((BQ, DIAG - KWIN), F32)], axis=1)
            rolled = pltpu.roll(padded, DIAG - (BQ - 1), 1, stride=1, stride_axis=0)
            diag_ref[h:h + 1, :] = jnp.sum(rolled, axis=0, keepdims=True)
        lane = lax.broadcasted_iota(jnp.int32, (DIAG, REL_PAD), 0)
        t = lax.broadcasted_iota(jnp.int32, (DIAG, REL_PAD), 1)
        offset = jnp.where(lane < KWIN, lane, lane - DIAG)
        pick = jnp.where(t == _rel_index(offset), 1.0, 0.0).astype(BF16)
        o_ref[...] = sum(_dot(piece, pick, NN) for piece in _split3(diag_ref[...]))

    return pl.pallas_call(
        body, name=name,
        out_shape=jax.ShapeDtypeStruct((N_HEADS, REL_PAD), F32),
        scratch_shapes=[pltpu.VMEM((N_HEADS, DIAG), F32)],
        compiler_params=_cparams(),
    )(dtab)


def _head_masks(heads=2):
    lane = lax.broadcasted_iota(jnp.int32, (1, heads * HEAD_DIM), 1)
    return [lane // HEAD_DIM == h for h in range(heads)]


def _own_lanes(masks, vals):
    out = vals[-1]
    for m, val in zip(masks[-2::-1], vals[-2::-1]):
        out = jnp.where(m, val, out)
    return out


CA_HEADS = 4
CA_LANES = CA_HEADS * HEAD_DIM


def _ca_window_specs(nq, col_off=0):
    return [pl.BlockSpec((BQ, CA_LANES), functools.partial(
        lambda p, i, d: (jnp.clip(i - 2 + d, 0, nq - 1), p + col_off), d=d)) for d in range(3)]


def _softmax_rows(s):
    p = jnp.exp(s - jnp.max(s, axis=-1, keepdims=True))
    return p, jnp.sum(p, axis=-1, keepdims=True)


def _ca_scores(qm, kc, tab_h, i):
    col = lax.broadcasted_iota(jnp.int32, (1, KWIN), 1)
    in_seq = col + (i - 2) * BQ >= 0
    return jnp.where(in_seq, _dot(qm, kc, NT) + tab_h, NEG)


def _ca_fwd(qn, kn, v, tab, name, v_off=0, gather=None):
    S = qn.shape[0]
    nq = S // BQ
    groups = WIDTH // CA_LANES
    qspec = pl.BlockSpec((BQ, CA_LANES), lambda p, i: (i, p))
    tspec = pl.BlockSpec((CA_HEADS, BQ, KWIN), lambda p, i: (p, 0, 0))
    n_side = len(gather.arrays) if gather else 0

    def body(*refs):
        q_ref, k0, k1, k2, v0, v1, v2, tab_ref = refs[:8]
        o_ref, ot_ref = refs[8 + n_side:10 + n_side]
        p, i = pl.program_id(0), pl.program_id(1)
        if gather:
            start, forward, finish = gather.steps(refs[8:8 + n_side], refs[10 + n_side:10 + 2 * n_side],
                                                  refs[10 + 2 * n_side:])
            pl.when((p == 0) & (i == 0))(start)
            pl.when((p == groups - 1) & (i == nq // 2))(forward)
        kc = jnp.concatenate([k0[...], k1[...], k2[...]], axis=0)
        vc = jnp.concatenate([v0[...], v1[...], v2[...]], axis=0)
        qv = q_ref[...]
        masks = _head_masks(CA_HEADS)
        heads = range(CA_HEADS)
        s = [_ca_scores(jnp.where(masks[h], qv, 0), kc, tab_ref[h], i) for h in heads]
        soft = [_softmax_rows(s[h]) for h in heads]
        o = [_dot(soft[h][0].astype(BF16), vc, NN) / soft[h][1] for h in heads]
        out = _own_lanes(masks, o)
        o_ref[...] = out.astype(BF16)
        ot_ref[...] = out.T.astype(BF16)
        if gather:
            pl.when((p == groups - 1) & (i == nq - 1))(finish)

    side = gather.arrays if gather else []
    return pl.pallas_call(
        body, name=name, grid=(groups, nq),
        in_specs=[qspec] + _ca_window_specs(nq) + _ca_window_specs(nq, v_off) + [tspec] + [ANY] * n_side,
        out_specs=[qspec, pl.BlockSpec((CA_LANES, BQ), lambda p, i: (p, i))] + [ANY] * n_side,
        out_shape=[jax.ShapeDtypeStruct((S, WIDTH), BF16), jax.ShapeDtypeStruct((WIDTH, S), BF16)]
        + (gather.out_shape if gather else []),
        scratch_shapes=gather.scratch if gather else [],
        compiler_params=_cparams(("arbitrary", "arbitrary") if gather else ("parallel", "parallel")),
    )(qn, kn, kn, kn, v, v, v, tab, *side)


def _ca_bwd(qn, kn, v, do, tab, name, v_off=0):
    S = qn.shape[0]
    nq = S // BQ
    qspec = pl.BlockSpec((BQ, CA_LANES), lambda p, i: (jnp.minimum(i, nq - 1), p))
    kout = pl.BlockSpec((BQ, CA_LANES), lambda p, i: (jnp.clip(i - 2, 0, nq - 1), p))
    tspec = pl.BlockSpec((CA_HEADS, BQ, KWIN), lambda p, i: (p, 0, 0))

    def body(q_ref, do_ref, k0, k1, k2, v0, v1, v2, tab_ref,
             dq_ref, dk_ref, dv_ref, dtab_ref, dk_acc, dv_acc):
        i = pl.program_id(1)

        @pl.when(i == 0)
        def _():
            dk_acc[...] = jnp.zeros_like(dk_acc)
            dv_acc[...] = jnp.zeros_like(dv_acc)
            dtab_ref[...] = jnp.zeros_like(dtab_ref)

        @pl.when(i < nq)
        def _():
            kc = jnp.concatenate([k0[...], k1[...], k2[...]], axis=0)
            vc = jnp.concatenate([v0[...], v1[...], v2[...]], axis=0)
            qv, dov = q_ref[...], do_ref[...]
            masks = _head_masks(CA_HEADS)
            heads = range(CA_HEADS)
            qm = [jnp.where(masks[h], qv, 0) for h in heads]
            dom = [jnp.where(masks[h], dov, 0) for h in heads]
            s = [_ca_scores(qm[h], kc, tab_ref[h], i) for h in heads]
            dp = [_dot(dom[h], vc, NT) for h in heads]
            soft = [_softmax_rows(s[h]) for h in heads]
            p = [soft[h][0] / soft[h][1] for h in heads]
            ds = [p[h] * (dp[h] - jnp.sum(p[h] * dp[h], axis=-1, keepdims=True)) for h in heads]
            for h in heads:
                dtab_ref[h] += ds[h]
            dsb = [ds[h].astype(BF16) for h in heads]
            pb = [p[h].astype(BF16) for h in heads]
            dq = [_dot(dsb[h], kc, NN) for h in heads]
            dq_ref[...] = _own_lanes(masks, dq)
            dkc = sum(_dot(dsb[h], qm[h], TN) for h in heads)
            dvc = sum(_dot(pb[h], dom[h], TN) for h in heads)
            for d in range(3):
                slot = (i + 1 + d) % 3
                dk_acc[slot] += dkc[d * BQ:(d + 1) * BQ]
                dv_acc[slot] += dvc[d * BQ:(d + 1) * BQ]

        @pl.when(i >= 2)
        def _():
            slot = (i + 1) % 3
            dk_ref[...] = dk_acc[slot]
            dv_ref[...] = dv_acc[slot].astype(BF16)
            dk_acc[slot] = jnp.zeros((BQ, CA_LANES), F32)
            dv_acc[slot] = jnp.zeros((BQ, CA_LANES), F32)

    return pl.pallas_call(
        body, name=name, grid=(WIDTH // CA_LANES, nq + 2),
        in_specs=[qspec, qspec] + _ca_window_specs(nq) + _ca_window_specs(nq, v_off) + [tspec],
        out_specs=[qspec, kout, kout, tspec],
        out_shape=[jax.ShapeDtypeStruct((S, WIDTH), F32), jax.ShapeDtypeStruct((S, WIDTH), F32),
                   jax.ShapeDtypeStruct((S, WIDTH), BF16), jax.ShapeDtypeStruct((N_HEADS, BQ, KWIN), F32)],
        scratch_shapes=[pltpu.VMEM((3, BQ, CA_LANES), F32)] * 2,
        compiler_params=_cparams(("parallel", "arbitrary")),
    )(qn, do, kn, kn, kn, v, v, v, tab)


def _sb_consts():
    r = lax.broadcasted_iota(jnp.int32, (BQ, BQ), 0)
    c = lax.broadcasted_iota(jnp.int32, (BQ, BQ), 1)
    from_s = jnp.where(r >= c, 1.0, 0.0).astype(BF16)
    causal = c < r
    return from_s, causal


def _suffix_sum(t, from_s):
    hi, lo = _split_bf16(t)
    return _dot(hi, from_s, NN) + _dot(lo, from_s, NN)


def _neg_abs(x):
    bits = lax.bitcast_convert_type(x, jnp.uint32) | jnp.uint32(0x80000000)
    return lax.bitcast_convert_type(bits, F32)


def _sb_log_keep(zn):
    return jnp.minimum(zn, 0.0) - jnp.log(1.0 + jnp.exp(_neg_abs(zn)))


SB_DEAD = 105.0


SB_QB = 2
SB_QB_FWD = 4


def _sb_walk(ip, tiles, keep_ref, qb=SB_QB):
    i0 = qb * ip

    @pl.when(ip == 0)
    def _():
        tiles([(0, [0], [True])] + [(a, [a, a - 1], [True, False]) for a in range(1, qb)])

    @pl.when(ip > 0)
    def _():
        tiles([(a, [i0 + a, i0 + a - 1], [True, False]) for a in range(qb)])

    for a in range(qb):
        def alive(a=a):
            return (jnp.max(keep_ref[2 * a:2 * a + 2]) > -SB_DEAD).astype(jnp.int32)

        def step(state, a=a, alive=alive):
            j, _ = state
            tiles([(a, [j], [False])])
            return j - 1, alive()

        lax.while_loop(lambda state: (state[0] >= 0) & (state[1] > 0), step, (i0 + a - 2, alive()))


def _sb_rows(j):
    return pl.ds(pl.multiple_of(j * BQ, BQ), BQ)


def _sb_chains(groups):
    chains = [(a, n, h) for a, js, _ in groups for n in range(len(js)) for h in range(2)]
    block = {(a, n): j for a, js, _ in groups for n, j in enumerate(js)}
    masked = [(a, n, h) for a, _, diags in groups for n, d in enumerate(diags) if d for h in range(2)]
    return chains, block, masked


def _sb_running(ref, vals, groups):
    before_chain = {}
    for a, js, _ in groups:
        for h in range(2):
            run = ref[2 * a + h]
            for n in range(len(js)):
                before_chain[(a, n, h)] = run
                run = run + jnp.sum(vals[(a, n, h)], axis=-1, keepdims=True)
            ref[2 * a + h] = run
    return before_chain


def _sb_specs(S, offs, qb=SB_QB):
    def qspec(off=0):
        return pl.BlockSpec((qb * BQ, PAIR), lambda p, i: (i, p + off))

    def kspec(off=0):
        return pl.BlockSpec((S, PAIR), lambda p, i: (0, p + off), pipeline_mode=pl.Buffered(1))

    return qspec, kspec, [qspec(offs[0]), kspec(offs[1]), kspec(offs[2])]


def _sb_fwd(q, k, v, name, offs=(0, 0, 0)):
    S = q.shape[0]
    qb = SB_QB_FWD
    steps = S // (qb * BQ)
    qspec, _, qkv_specs = _sb_specs(S, offs, qb)

    def body(q_ref, k_ref, v_ref, o_ref, of_ref, ot_ref, carry_ref, acc_ref):
        ip = pl.program_id(1)
        from_s, causal = _sb_consts()
        masks = _head_masks()
        qn = q_ref[...] * -(HEAD_DIM ** -0.5)
        qms = {(a, h): jnp.where(masks[h], qn[a * BQ:(a + 1) * BQ], 0) for a in range(qb) for h in range(2)}
        carry_ref[...] = jnp.zeros_like(carry_ref)
        acc_ref[...] = jnp.zeros_like(acc_ref)

        def tiles(groups):
            chains, block, masked = _sb_chains(groups)
            kbs = {an: k_ref[_sb_rows(j), :] for an, j in block.items()}
            vbs = {an: v_ref[_sb_rows(j), :] for an, j in block.items()}
            zn = {c: _dot(qms[(c[0], c[2])], kbs[c[:2]], NT) for c in chains}
            log_keep = {c: _sb_log_keep(zn[c]) for c in chains}
            for c in masked:
                log_keep[c] = jnp.where(causal, log_keep[c], 0.0)
            split = {c: _split_bf16(log_keep[c]) for c in chains}
            carry = _sb_running(carry_ref, log_keep, groups)
            suffix = {c: _dot(split[c][0], from_s, NN) + _dot(split[c][1], from_s, NN) for c in chains}
            w = {c: jnp.exp(carry[c] + suffix[c] - zn[c]) for c in chains}
            for c in masked:
                w[c] = jnp.where(causal, w[c], 0.0)
            for c in chains:
                acc_ref[2 * c[0] + c[2]] += _dot(w[c].astype(BF16), vbs[c[:2]], NN)

        _sb_walk(ip, tiles, carry_ref, qb)
        for a in range(qb):
            out = jnp.where(masks[0], acc_ref[2 * a], acc_ref[2 * a + 1])
            o_ref[a * BQ:(a + 1) * BQ, :] = out.astype(BF16)
            of_ref[a * BQ:(a + 1) * BQ, :] = out
            ot_ref[:, a * BQ:(a + 1) * BQ] = out.T.astype(BF16)

    return pl.pallas_call(
        body, name=name, grid=(WIDTH // PAIR, steps),
        in_specs=qkv_specs, out_specs=[qspec(), qspec(), pl.BlockSpec((PAIR, qb * BQ), lambda p, i: (p, i))],
        out_shape=[jax.ShapeDtypeStruct((S, WIDTH), BF16), jax.ShapeDtypeStruct((S, WIDTH), F32),
                   jax.ShapeDtypeStruct((WIDTH, S), BF16)],
        scratch_shapes=[pltpu.VMEM((2 * qb, BQ, 1), F32), pltpu.VMEM((2 * qb, BQ, PAIR), F32)],
        compiler_params=_cparams(("parallel", "arbitrary")),
    )(q, k, v)


def _sb_bwd(q, k, v, o, do, name, offs=(0, 0, 0), exchange=None):
    S = q.shape[0]
    steps = S // (SB_QB * BQ)
    pairs = WIDTH // PAIR
    qspec, kspec, qkv_specs = _sb_specs(S, offs)
    n_side = len(exchange.arrays) if exchange else 0

    def body(*refs):
        q_ref, o_ref, do_ref, k_ref, v_ref = refs[:5]
        dq_ref, dk_ref, dv_ref = refs[5 + n_side:8 + n_side]
        dk_acc, dv_acc, keep_ref, gsum_ref, dq_acc = refs[8 + 2 * n_side:13 + 2 * n_side]
        ip = pl.program_id(1)
        if exchange:
            start, finish = exchange.steps(refs[5:5 + n_side], refs[8 + n_side:8 + 2 * n_side], refs[13 + 2 * n_side:])
            pl.when((pl.program_id(0) == 0) & (ip == 0))(start)

        @pl.when(ip == 0)
        def _():
            dk_acc[...] = jnp.zeros_like(dk_acc)
            dv_acc[...] = jnp.zeros_like(dv_acc)

        from_s, causal = _sb_consts()
        masks = _head_masks()
        qn, dov = q_ref[...] * -(HEAD_DIM ** -0.5), do_ref[...]
        od = o_ref[...] * dov.astype(F32)
        lanes = [(a, h) for a in range(SB_QB) for h in range(2)]
        rows_of = {a: slice(a * BQ, (a + 1) * BQ) for a in range(SB_QB)}
        qms = {(a, h): jnp.where(masks[h], qn[rows_of[a]], 0) for a, h in lanes}
        doms = {(a, h): jnp.where(masks[h], dov[rows_of[a]], 0) for a, h in lanes}
        totals = {(a, h): jnp.sum(jnp.where(masks[h], od[rows_of[a]], 0.0), axis=-1, keepdims=True)
                  for a, h in lanes}
        for ref in (keep_ref, gsum_ref, dq_acc):
            ref[...] = jnp.zeros_like(ref)

        def tiles(groups):
            chains, block, masked = _sb_chains(groups)
            kbs = {an: k_ref[_sb_rows(j), :] for an, j in block.items()}
            vbs = {an: v_ref[_sb_rows(j), :] for an, j in block.items()}
            zn = {c: _dot(qms[(c[0], c[2])], kbs[c[:2]], NT) for c in chains}
            dw = {c: _dot(doms[(c[0], c[2])], vbs[c[:2]], NT) for c in chains}
            log_keep = {c: _sb_log_keep(zn[c]) for c in chains}
            for c in masked:
                log_keep[c] = jnp.where(causal, log_keep[c], 0.0)
            split = {c: _split_bf16(log_keep[c]) for c in chains}
            kept = _sb_running(keep_ref, log_keep, groups)
            suffix = {c: _dot(split[c][0], from_s, NN) + _dot(split[c][1], from_s, NN) for c in chains}
            w = {c: jnp.exp(kept[c] + suffix[c] - zn[c]) for c in chains}
            for c in masked:
                w[c] = jnp.where(causal, w[c], 0.0)
            wb = {c: w[c].astype(BF16) for c in chains}
            g = {c: wb[c].astype(F32) * dw[c] for c in chains}
            gsplit = {c: _split_bf16(g[c]) for c in chains}
            gsum = _sb_running(gsum_ref, g, groups)
            gsuffix = {c: _dot(gsplit[c][0], from_s, NN) + _dot(gsplit[c][1], from_s, NN) for c in chains}
            dzb = {}
            for c in chains:
                before = totals[(c[0], c[2])] - (gsum[c] + gsuffix[c])
                dz = (g[c] + before) * jnp.exp(log_keep[c]) - before
                if c in masked:
                    dz = jnp.where(causal, dz, 0.0)
                dzb[c] = dz.astype(BF16)
            for c in chains:
                rows = _sb_rows(block[c[:2]])
                dq_acc[2 * c[0] + c[2]] += _dot(dzb[c], kbs[c[:2]], NN)
                dk_acc[rows, :] -= _dot(dzb[c], qms[(c[0], c[2])], TN)
                dv_acc[rows, :] += _dot(wb[c], doms[(c[0], c[2])], TN)

        _sb_walk(ip, tiles, keep_ref)
        for a in range(SB_QB):
            dq = jnp.where(masks[0], dq_acc[2 * a], dq_acc[2 * a + 1])
            dq_ref[a * BQ:(a + 1) * BQ, :] = (dq * HEAD_DIM ** -0.5).astype(BF16)

        @pl.when(ip == steps - 1)
        def _():
            dk_ref[...] = dk_acc[...].astype(BF16)
            dv_ref[...] = dv_acc[...].astype(BF16)

        if exchange:
            pl.when((pl.program_id(0) == pairs - 1) & (ip == steps - 1))(finish)

    side = exchange.arrays if exchange else []
    return pl.pallas_call(
        body, name=name, grid=(pairs, steps),
        in_specs=[qkv_specs[0], qspec(), qspec(), qkv_specs[1], qkv_specs[2]] + [ANY] * n_side,
        out_specs=[qspec(), kspec(), kspec()] + [ANY] * n_side,
        out_shape=[jax.ShapeDtypeStruct((S, WIDTH), BF16)] * 3 + (exchange.out_shape if exchange else []),
        scratch_shapes=[pltpu.VMEM((S, PAIR), F32)] * 2 + [pltpu.VMEM((2 * SB_QB, BQ, 1), F32)] * 2
        + [pltpu.VMEM((2 * SB_QB, BQ, PAIR), F32)] + (exchange.scratch if exchange else []),
        compiler_params=_cparams(("arbitrary", "arbitrary") if exchange else ("parallel", "arbitrary")),
    )(q, o, do, k, v, *side)


ANY = pl.BlockSpec(memory_space=pl.ANY)


def _place():
    return lax.axis_index("x"), lax.axis_index("y"), lax.axis_index("c")


def _other_chips(x, y):
    return [(2 * px + py, (px, py)) for px, py in ((1 - x, y), (x, 1 - y), (1 - x, 1 - y))]


def _remote(src, dst, sems, k, to):
    return pltpu.make_async_remote_copy(src_ref=src, dst_ref=dst, send_sem=sems[0].at[k], recv_sem=sems[1].at[k],
                                        device_id=to, device_id_type=MESH)


class _Gather:
    def __init__(self, ws, extras=()):
        self.n, self.m = len(ws), len(extras)
        self.arrays = list(ws) + list(extras)
        self.n_copies = 6 * self.n + 3 * self.m
        self.out_shape = [jax.ShapeDtypeStruct((N_CHIPS,) + a.shape, a.dtype) for a in self.arrays]
        self.scratch = [pltpu.SemaphoreType.DMA((self.n_copies,)), pltpu.SemaphoreType.DMA((self.n_copies,))]

    def steps(self, in_refs, out_refs, sems):
        n = self.n
        x, y, c = _place()
        me = 2 * x + y
        chips = _other_chips(x, y)
        sibling = (x, y, 1 - c)

        def halves(ref):
            rh = ref.shape[-2] // 2
            return pl.ds(c * rh, rh), pl.ds((1 - c) * rh, rh)

        def first():
            cps = [_remote(w_ref.at[halves(w_ref)[0]], o_ref.at[me, halves(w_ref)[0]], sems, 6 * a + k, (*xy, c))
                   for a, (w_ref, o_ref) in enumerate(zip(in_refs[:n], out_refs[:n])) for k, (_, xy) in enumerate(chips)]
            return cps + [_remote(e_ref, eo_ref.at[me], sems, 6 * n + 3 * b + k, (*xy, c))
                          for b, (e_ref, eo_ref) in enumerate(zip(in_refs[n:], out_refs[n:]))
                          for k, (_, xy) in enumerate(chips)]

        def passed():
            return [_remote(o_ref.at[chip, halves(o_ref)[0]], o_ref.at[chip, halves(o_ref)[0]], sems, 6 * a + 3 + k, sibling)
                    for a, o_ref in enumerate(out_refs[:n]) for k, (chip, _) in enumerate(chips)]

        def start():
            for cp in first():
                cp.start()

        def forward():
            for a, o_ref in enumerate(out_refs[:n]):
                for k, (chip, xy) in enumerate(chips):
                    landed = o_ref.at[chip, halves(o_ref)[0]]
                    _remote(landed, landed, sems, 6 * a + k, (*xy, c)).wait_recv()
            for cp in passed():
                cp.start()

        def finish():
            for a, o_ref in enumerate(out_refs[:n]):
                for k, (chip, _) in enumerate(chips):
                    landed = o_ref.at[chip, halves(o_ref)[1]]
                    _remote(landed, landed, sems, 6 * a + 3 + k, sibling).wait_recv()
            for b, (e_ref, eo_ref) in enumerate(zip(in_refs[n:], out_refs[n:])):
                for k, (chip, xy) in enumerate(chips):
                    _remote(e_ref, eo_ref.at[chip], sems, 6 * n + 3 * b + k, (*xy, c)).wait_recv()
            for cp in first() + passed():
                cp.wait_send()

        return start, forward, finish


class _ChipExchange:
    def __init__(self, ps):
        self.arrays = list(ps)
        self.out_shape = [jax.ShapeDtypeStruct(p.shape, p.dtype) for p in ps]
        self.scratch = [pltpu.SemaphoreType.DMA((3 * len(ps),)), pltpu.SemaphoreType.DMA((3 * len(ps),))]

    def steps(self, p_refs, out_refs, sems):
        x, y, c = _place()
        me = 2 * x + y
        chips = _other_chips(x, y)

        def copies():
            return [_remote(p_ref.at[chip], o_ref.at[me], sems, 3 * a + k, (*xy, c))
                    for a, (p_ref, o_ref) in enumerate(zip(p_refs, out_refs)) for k, (chip, xy) in enumerate(chips)]

        def start():
            for cp in copies():
                cp.start()

        def finish():
            for a, (p_ref, o_ref) in enumerate(zip(p_refs, out_refs)):
                for k, (chip, xy) in enumerate(chips):
                    _remote(p_ref.at[chip], o_ref.at[chip], sems, 3 * a + k, (*xy, c)).wait_recv()
            for cp in copies():
                cp.wait_send()

        return start, finish


def _exchange_cores(gs, name, small=None):
    n = len(gs)
    m = 0 if small is None else 1

    def body(*refs):
        g_refs, sib_refs = refs[:n], refs[n + m:2 * n + m]
        sems = refs[2 * (n + m):]
        x, y, c = _place()
        me = 4 * x + 2 * y + c
        copies = []
        for a, (g_ref, sib_ref) in enumerate(zip(g_refs, sib_refs)):
            rh = g_ref.shape[1] // 2
            copies.append(_remote(g_ref.at[:, pl.ds((1 - c) * rh, rh), :], sib_ref, sems, a, (x, y, 1 - c)))
        if m:
            small_ref, all_ref = refs[n], refs[2 * n + m]
            k = n
            for fx in (0, 1):
                for fy in (0, 1):
                    for fc in (0, 1):
                        if fx or fy or fc:
                            to = (1 - x if fx else x, 1 - y if fy else y, 1 - c if fc else c)
                            copies.append(_remote(small_ref, all_ref.at[me], sems, k, to))
                            k += 1
        for cp in copies:
            cp.start()
        for cp in copies:
            cp.wait_recv()
        for cp in copies:
            cp.wait_send()

    n_copies = n + m * (N_DEV - 1)
    args = list(gs) + ([small] if m else [])
    return pl.pallas_call(
        body, name=name, in_specs=[ANY] * (n + m), out_specs=[ANY] * (n + m),
        out_shape=[jax.ShapeDtypeStruct((N_CHIPS, g.shape[1] // 2, g.shape[2]), F32) for g in gs]
        + ([jax.ShapeDtypeStruct((N_DEV,) + small.shape, F32)] if m else []),
        scratch_shapes=[pltpu.SemaphoreType.DMA((n_copies,)), pltpu.SemaphoreType.DMA((n_copies,))],
    )(*args)


def _share_halves(ghs, name):
    n = len(ghs)

    def body(*refs):
        gh_refs, out_refs, sems = refs[:n], refs[n:2 * n], refs[2 * n:]
        x, y, c = _place()
        copies = [_remote(gh_ref, o_ref, sems, a, (x, y, 1 - c)) for a, (gh_ref, o_ref) in enumerate(zip(gh_refs, out_refs))]
        for cp in copies:
            cp.start()
        for cp in copies:
            cp.wait_recv()
        for cp in copies:
            cp.wait_send()

    return pl.pallas_call(
        body, name=name, in_specs=[ANY] * n, out_specs=[ANY] * n,
        out_shape=[jax.ShapeDtypeStruct(g.shape, g.dtype) for g in ghs],
        scratch_shapes=[pltpu.SemaphoreType.DMA((n,)), pltpu.SemaphoreType.DMA((n,))],
    )(*ghs)


EW_BLOCK_BYTES = 2 * 1024 * 1024


def _row_block(rows, cols, mult=8):
    fits = [b for b in range(mult, rows + 1, mult) if rows % b == 0 and b * cols * 4 <= EW_BLOCK_BYTES]
    return max(fits) if fits else mult


def _add2(a, b, name):
    R, C = a.shape
    rows = _row_block(R, C, mult=16)
    spec = pl.BlockSpec((rows, C), lambda i: (i, 0))

    def body(a_ref, b_ref, o_ref):
        o_ref[...] = (a_ref[...] + b_ref[...]).astype(BF16)

    return pl.pallas_call(
        body, name=name, grid=(R // rows,), in_specs=[spec, spec], out_specs=spec,
        out_shape=jax.ShapeDtypeStruct(a.shape, BF16),
        compiler_params=_cparams(("parallel",)),
    )(a, b)


def _sum_leading(a, name):
    n, R, C = a.shape
    rows = _row_block(R, n * C, mult=16 if a.dtype == BF16 else 8)

    def body(a_ref, o_ref):
        acc = a_ref[0].astype(F32)
        for j in range(1, n):
            acc = acc + a_ref[j].astype(F32)
        o_ref[...] = acc

    return pl.pallas_call(
        body, name=name, grid=(R // rows,),
        in_specs=[pl.BlockSpec((n, rows, C), lambda i: (0, i, 0))],
        out_specs=pl.BlockSpec((rows, C), lambda i: (i, 0)),
        out_shape=jax.ShapeDtypeStruct((R, C), F32),
        compiler_params=_cparams(("parallel",)),
    )(a)


def _adamw(w, g, m, v, name):
    R, C = w.shape
    rows = _row_block(R, C)
    spec = pl.BlockSpec((rows, C), lambda i: (i, 0))

    def body(w_ref, g_ref, m_ref, v_ref, d_ref, mo_ref, vo_ref):
        gv = g_ref[...]
        mn = ADAM_B1 * m_ref[...] + (1.0 - ADAM_B1) * gv
        vn = ADAM_B2 * v_ref[...] + (1.0 - ADAM_B2) * (gv * gv)
        m_hat = mn / (1.0 - ADAM_B1 ** ADAM_STEP)
        v_hat = vn / (1.0 - ADAM_B2 ** ADAM_STEP)
        d_ref[...] = -ADAM_LR * (m_hat / (jnp.sqrt(v_hat) + ADAM_EPS) + ADAM_WD * w_ref[...])
        mo_ref[...] = mn
        vo_ref[...] = vn

    return pl.pallas_call(
        body, name=name, grid=(R // rows,), in_specs=[spec] * 4, out_specs=[spec] * 3,
        out_shape=[jax.ShapeDtypeStruct((R, C), F32)] * 3,
        compiler_params=_cparams(("parallel",)),
    )(w, g, m, v)


BIG = ("w_in", "w_branch_a", "w_branch_b", "w_out", "w_ffn_up", "w_ffn_down")
COL_SHARDED = {"w_in": True, "w_branch_a": True, "w_branch_b": True, "w_out": False, "w_ffn_up": True,
               "w_ffn_down": False}
CONV_W_COLS = 2 * D_FF // N_CHIPS
SMALL_REPLICATED = (("norm1_g", D_MODEL), ("q_norm_g", HEAD_DIM), ("k_norm_g", HEAD_DIM),
                    ("rel_bias", N_HEADS * N_REL), ("norm2_g", D_MODEL), ("ffn_conv_b", 2 * D_FF))
SMALL_GRADS = SMALL_REPLICATED + (("ffn_conv_w", 3 * 2 * D_FF),)
SMALL_OWN = SMALL_REPLICATED + (("ffn_conv_w", 3 * CONV_W_COLS),)
SMALL_GRAD_ROWS = 32
SMALL_OWN_ROWS = 16


def _whole(name, stacked):
    return jnp.concatenate(list(stacked), axis=1) if COL_SHARDED[name] else stacked.reshape(-1, stacked.shape[2])


def _pack_small(vals, sizes, rows):
    flat = jnp.concatenate([vals[n].reshape(-1) for n, _ in sizes])
    return jnp.pad(flat, (0, rows * PACK_COLS - flat.shape[0])).reshape(rows, PACK_COLS)


def _unpack_small(packed, sizes):
    flat, out, o = packed.reshape(-1), {}, 0
    for n, sz in sizes:
        out[n] = flat[o:o + sz]
        o += sz
    return out


def kernel(x, norm1_g, w_in, q_norm_g, k_norm_g, rel_bias, w_branch_a, w_branch_b, w_out, norm2_g, w_ffn_up, ffn_conv_w, ffn_conv_b, w_ffn_down, loss_target, m_norm1_g, m_w_in, m_q_norm_g, m_k_norm_g, m_rel_bias, m_w_branch_a, m_w_branch_b, m_w_out, m_norm2_g, m_w_ffn_up, m_ffn_conv_w, m_ffn_conv_b, m_w_ffn_down, v_norm1_g, v_w_in, v_q_norm_g, v_k_norm_g, v_rel_bias, v_w_branch_a, v_w_branch_b, v_w_out, v_norm2_g, v_w_ffn_up, v_ffn_conv_w, v_ffn_conv_b, v_w_ffn_down):
    w_big = {"w_in": w_in[0], "w_branch_a": w_branch_a[0], "w_branch_b": w_branch_b[0], "w_out": w_out[0],
             "w_ffn_up": w_ffn_up[0], "w_ffn_down": w_ffn_down[0]}
    m_big = {"w_in": m_w_in[0], "w_branch_a": m_w_branch_a[0], "w_branch_b": m_w_branch_b[0], "w_out": m_w_out[0],
             "w_ffn_up": m_w_ffn_up[0], "w_ffn_down": m_w_ffn_down[0]}
    v_big = {"w_in": v_w_in[0], "w_branch_a": v_w_branch_a[0], "w_branch_b": v_w_branch_b[0], "w_out": v_w_out[0],
             "w_ffn_up": v_w_ffn_up[0], "w_ffn_down": v_w_ffn_down[0]}
    xs, tgt = x[0], loss_target[0]

    xi, yi, ci = _place()
    chip = 2 * xi + yi

    def with_own(stacked, own):
        return lax.dynamic_update_slice(stacked, own[None], (chip,) + (0,) * own.ndim)

    shards_bf = {n: w_big[n].astype(BF16) for n in BIG}
    conv_own = jnp.pad(ffn_conv_w[0], ((0, 8 - ffn_conv_w.shape[1]), (0, 0)))
    later = [n for n in BIG if n != "w_in"]

    hn, hn_t, w_in_g = _rms_fwd(xs, norm1_g, "rms1", gather=_Gather([shards_bf["w_in"]]))
    w_in_f = _whole("w_in", with_own(w_in_g, shards_bf["w_in"]))
    w_in_t = w_in_f.T
    gq = jnp.tile(q_norm_g, (1, N_HEADS))
    gk = jnp.tile(k_norm_g, (1, N_HEADS))
    qk, qa, ka = _proj_qknorm_fwd(hn, w_in_f[:, :2 * WIDTH], gq, gk, "proj_qk_qknorm")
    vqkv = _matmul(hn, w_in_f[:, 2 * WIDTH:6 * WIDTH], BF16, "proj_vqkv")
    gates = _matmul(hn, w_in_f[:, 6 * WIDTH:], BF16, "proj_gates")
    per = WIDTH // PAIR
    b_offs = (per, 2 * per, 3 * per)
    tab = _bias_table(jnp.pad(rel_bias[0], ((0, 0), (0, REL_PAD - N_REL))), "bias_table")
    out_a, out_a_t, *gathered = _ca_fwd(qa, ka, vqkv, tab, "chunk_attn",
                                        gather=_Gather([shards_bf[n] for n in later], [conv_own]))
    full = {n: _whole(n, with_own(g, shards_bf[n])) for n, g in zip(later, gathered)}
    conv_w = jnp.concatenate(list(with_own(gathered[-1], conv_own)[:, :3]), axis=1)
    w_a, w_b, w_o, w_up, w_dn = (full[n] for n in later)
    w_a_t, w_b_t, w_o_t, w_up_t, w_dn_t = (w.T for w in (w_a, w_b, w_o, w_up, w_dn))
    out_b, out_b_f32, out_b_t = _sb_fwd(vqkv, vqkv, vqkv, "stick_attn", b_offs)
    y_a, y_b, mixed, mixed_t = _branch_mix_fwd(out_a, w_a, out_b, w_b, gates, "branch_mix")
    x2, hn2, hn2_t = _proj_rms_fwd(mixed, w_o, xs, norm2_g, "out_proj_rms2")
    hid = _matmul(hn2, w_up, BF16, "ffn_up")
    act, act_t = _convglu_fwd(hid, conv_w, ffn_conv_b, "convglu")
    dy, dyb, sq = _down_and_loss(act, w_dn, x2, tgt, "ffn_down_loss")
    loss = lax.psum(0.5 / D_MODEL * jnp.sum(sq), ("x", "y", "c"))

    dact = _matmul(dyb, w_dn_t, BF16, "d_act")
    d_w_dn = _matmul(act_t, dyb, F32, "d_w_down")
    dhg, dhu, dcwg, dcwu, dcbg, dcbu = _convglu_bwd(hid, dact, conv_w, ffn_conv_b, "convglu_bwd")
    half_chips = N_CHIPS // 2
    d_w_up = jnp.concatenate([_matmul(hn2_t, dhg, F32, "d_w_up_gate", slabs=half_chips),
                              _matmul(hn2_t, dhu, F32, "d_w_up_up", slabs=half_chips)], axis=0)
    dx2, dx2b, d_norm2 = _rms_bwd([(dhg, w_up_t[:D_FF]), (dhu, w_up_t[D_FF:])], x2, norm2_g, dy, "d_hn2_rms2_bwd")
    d_w_o = _matmul(mixed_t, dx2b, F32, "d_w_out")
    dgates, dya, dyb_b = _mix_bwd(dx2b, w_o_t, gates, y_a, y_b, "d_mixed_mix_bwd")
    d_w_a, d_w_b = (_matmul(o_t, d, F32, nm).reshape(WIDTH, N_CHIPS, -1).transpose(1, 0, 2)
                    for o_t, d, nm in ((out_a_t, dya, "d_w_branch_a"), (out_b_t, dyb_b, "d_w_branch_b")))
    do_a = _matmul(dya, w_a_t, BF16, "d_out_a")
    do_b = _matmul(dyb_b, w_b_t, BF16, "d_out_b")

    def core_sums(names, gs, sibs):
        out = {}
        for n, g, sib in zip(names, gs, sibs):
            rh, cols = sib.shape[1], sib.shape[2]
            mine = lax.dynamic_slice_in_dim(g, ci * rh, rh, axis=1)
            out[n] = _add2(mine.reshape(-1, cols), sib.reshape(-1, cols), "sum_cores_" + n).reshape(sib.shape)
        return out

    grads_full = {"w_branch_a": d_w_a, "w_branch_b": d_w_b, "w_ffn_up": d_w_up,
                  "w_out": d_w_o.reshape(N_CHIPS, -1, D_MODEL), "w_ffn_down": d_w_dn.reshape(N_CHIPS, -1, D_MODEL)}
    early = [grads_full[n] for n in later]
    chip_parts = core_sums(later, early, _exchange_cores(early, "exchange_cores_early"))
    dqb, dkb, dvb, *parts_early = _sb_bwd(vqkv, vqkv, vqkv, out_b_f32, do_b, "stick_attn_bwd", b_offs,
                                           exchange=_ChipExchange([chip_parts[n] for n in later]))
    parts = dict(zip(later, parts_early))
    dqa_n, dka_n, dva, dtab = _ca_bwd(qa, ka, vqkv, do_a, tab, "chunk_attn_bwd")
    d_rel = _bias_table_bwd(dtab, "bias_table_bwd")[:, :N_REL]
    dqa, dka, dgq, dgk = _qknorm_bwd(qk, gq, gk, dqa_n, dka_n, "qknorm_bwd")
    pieces = (("qa", dqa), ("ka", dka), ("va", dva), ("qb", dqb), ("kb", dkb), ("vb", dvb), ("gates", dgates))
    d_w_in = jnp.concatenate([_matmul(hn_t, d, F32, "d_w_in_" + nm) for nm, d in pieces], axis=1)
    d_w_in = d_w_in.reshape(D_MODEL, N_CHIPS, -1).transpose(1, 0, 2)
    chip_parts.update(core_sums(["w_in"], [d_w_in], _exchange_cores([d_w_in], "exchange_cores_w_in")))
    offsets = [sum(d.shape[1] for _, d in pieces[:k]) for k in range(len(pieces))]
    dx, _, d_norm1, parts["w_in"] = _rms_bwd([(d, w_in_t[o:o + d.shape[1]]) for (_, d), o in zip(pieces, offsets)],
                                             xs, norm1_g, dx2, "d_hn_rms1_bwd",
                                             exchange=_ChipExchange([chip_parts["w_in"]]))

    small_g = _pack_small({"norm1_g": d_norm1, "q_norm_g": dgq.reshape(N_HEADS, HEAD_DIM).sum(0),
                           "k_norm_g": dgk.reshape(N_HEADS, HEAD_DIM).sum(0), "rel_bias": d_rel,
                           "norm2_g": d_norm2, "ffn_conv_b": jnp.concatenate([dcbg, dcbu], axis=1),
                           "ffn_conv_w": jnp.concatenate([dcwg, dcwu], axis=1)}, SMALL_GRADS, SMALL_GRAD_ROWS)
    (small_all,) = _exchange_cores([], "exchange_small", small=small_g)
    g_halves = [_sum_leading(with_own(parts[n], lax.dynamic_index_in_dim(chip_parts[n], chip, 0, keepdims=False)),
                             "sum_chips_" + n) for n in BIG]
    g_others = _share_halves(g_halves, "share_halves")
    grads = {n: jnp.concatenate([jnp.where(ci == 0, mine, other), jnp.where(ci == 0, other, mine)], axis=0)
             for n, mine, other in zip(BIG, g_halves, g_others)}
    small_all = lax.dynamic_update_slice(small_all, small_g[None], (4 * xi + 2 * yi + ci, 0, 0))
    small_sum = _unpack_small(_sum_leading(small_all, "sum_small"), SMALL_GRADS)
    small_sum["ffn_conv_w"] = lax.dynamic_slice_in_dim(small_sum["ffn_conv_w"].reshape(3, 2 * D_FF),
                                                       chip * CONV_W_COLS, CONV_W_COLS, axis=1)

    deltas, new_m, new_v = {}, {}, {}
    for n in BIG:
        deltas[n], new_m[n], new_v[n] = _adamw(w_big[n], grads[n], m_big[n], v_big[n], "adamw_" + n)

    shapes = {"norm1_g": norm1_g.shape, "q_norm_g": q_norm_g.shape, "k_norm_g": k_norm_g.shape,
              "rel_bias": rel_bias.shape, "norm2_g": norm2_g.shape, "ffn_conv_b": ffn_conv_b.shape,
              "ffn_conv_w": ffn_conv_w.shape}
    small_w = {"norm1_g": norm1_g, "q_norm_g": q_norm_g, "k_norm_g": k_norm_g, "rel_bias": rel_bias,
               "norm2_g": norm2_g, "ffn_conv_b": ffn_conv_b, "ffn_conv_w": ffn_conv_w}
    small_m = {"norm1_g": m_norm1_g, "q_norm_g": m_q_norm_g, "k_norm_g": m_k_norm_g, "rel_bias": m_rel_bias,
               "norm2_g": m_norm2_g, "ffn_conv_b": m_ffn_conv_b, "ffn_conv_w": m_ffn_conv_w}
    small_v = {"norm1_g": v_norm1_g, "q_norm_g": v_q_norm_g, "k_norm_g": v_k_norm_g, "rel_bias": v_rel_bias,
               "norm2_g": v_norm2_g, "ffn_conv_b": v_ffn_conv_b, "ffn_conv_w": v_ffn_conv_w}
    ds, ms, vs = _adamw(*(_pack_small(t, SMALL_OWN, SMALL_OWN_ROWS) for t in (small_w, small_sum, small_m, small_v)),
                        "adamw_small")
    small_grads = small_sum
    ds, ms, vs = (_unpack_small(t, SMALL_OWN) for t in (ds, ms, vs))

    order = ("norm1_g", "w_in", "q_norm_g", "k_norm_g", "rel_bias", "w_branch_a", "w_branch_b", "w_out",
             "norm2_g", "w_ffn_up", "ffn_conv_w", "ffn_conv_b", "w_ffn_down")
    outs = [loss, dx[None]]
    for big, small in ((grads, small_grads), (deltas, ds), (new_m, ms), (new_v, vs)):
        for n in order:
            outs.append(big[n][None] if n in big else small[n].reshape(shapes[n]))
    return tuple(outs)
```

```python
import functools

import jax
import jax.numpy as jnp
from jax import lax
from jax.experimental import pallas as pl
from jax.experimental.pallas import tpu as pltpu

F32 = jnp.float32
BF16 = jnp.bfloat16
MESH = pl.DeviceIdType.MESH

D_MODEL = 1024
HEAD_DIM = 64
N_HEADS = 8
WIDTH = N_HEADS * HEAD_DIM
CHUNK = 64
LEFT_CHUNKS = 8
MAX_REL = 128
N_REL = 2 * MAX_REL + 1
D_FF = 2816
EPS = 1e-6
NEG = -1e30

ADAM_LR = 0.001
ADAM_B1 = 0.9
ADAM_B2 = 0.999
ADAM_EPS = 1e-08
ADAM_WD = 0.01
ADAM_STEP = 10

N_CHIPS = 4
N_DEV = 8
LANES = 128
PAIR = 2 * HEAD_DIM
BQ = 256
BAND = LEFT_CHUNKS * CHUNK
KWIN = BAND + BQ
VMEM_LIMIT = 56 * 1024 * 1024
PACK_COLS = 1024

NN = (((1,), (0,)), ((), ()))
NT = (((1,), (1,)), ((), ()))
TN = (((0,), (0,)), ((), ()))


def _cparams(sem=None):
    if sem is None:
        return pltpu.CompilerParams(vmem_limit_bytes=VMEM_LIMIT)
    return pltpu.CompilerParams(dimension_semantics=sem, vmem_limit_bytes=VMEM_LIMIT)


def _pick(n, cands):
    for c in cands:
        if n % c == 0:
            return c
    raise ValueError(f"no block for {n}")


def _dot(a, b, dn):
    return lax.dot_general(a, b, dn, preferred_element_type=F32)


def _sigmoid(x):
    return 0.5 * jnp.tanh(0.5 * x) + 0.5


def _split_bf16(x):
    hi = x.astype(BF16)
    lo = (x - hi.astype(F32)).astype(BF16)
    return hi, lo


MM_RESIDENT_BYTES = 12 * 1024 * 1024
MM_TILE_BYTES = 4 * 1024 * 1024


def _matmul(a, b, out_dtype, name, slabs=None):
    (M, K), N = a.shape, b.shape[1]
    out_bytes = jnp.dtype(out_dtype).itemsize
    if slabs is None and N <= D_FF and K * N * 2 <= MM_RESIDENT_BYTES:
        bk, bn = K, N
        bm = next(c for c in (1024, 512, 256, 128)
                  if M % c == 0 and c * K * 2 <= MM_TILE_BYTES and c * N * out_bytes <= MM_TILE_BYTES)
        b_spec = pl.BlockSpec((bk, bn), lambda i, j, k: (0, 0), pipeline_mode=pl.Buffered(1))
    elif slabs is None and K <= D_FF:
        bk, bm, bn = K, _pick(M, (1024, 512)), _pick(N, (D_FF // 2, 512, 256, 128))
        b_spec = pl.BlockSpec((bk, bn), lambda i, j, k: (k, j))
    else:
        bk = _pick(K, (2048, 1024, 512))
        bm = _pick(M, (D_FF // 2, 1024, 512, 256, 128))
        bn = N // slabs if slabs else _pick(N, (D_FF // 2, 1024, 512, 256, 128))
        b_spec = pl.BlockSpec((bk, bn), lambda i, j, k: (k, j))
    nk = K // bk
    dn = NN
    a_spec = pl.BlockSpec((bm, bk), lambda i, j, k: (i, k))
    if slabs:
        o_spec = pl.BlockSpec((None, bm, bn), lambda i, j, k: (j, i, 0))
        out_shape = jax.ShapeDtypeStruct((slabs, M, bn), out_dtype)
    else:
        o_spec = pl.BlockSpec((bm, bn), lambda i, j, k: (i, j))
        out_shape = jax.ShapeDtypeStruct((M, N), out_dtype)

    def body(a_ref, b_ref, o_ref, acc_ref):
        k = pl.program_id(2)
        part = _dot(a_ref[...], b_ref[...], dn)
        if nk == 1:
            o_ref[...] = part.astype(out_dtype)
        else:
            @pl.when(k == 0)
            def _():
                acc_ref[...] = part

            @pl.when(k > 0)
            def _():
                acc_ref[...] += part

            @pl.when(k == nk - 1)
            def _():
                o_ref[...] = acc_ref[...].astype(out_dtype)

    return pl.pallas_call(
        body, name=name, grid=(M // bm, N // bn, nk),
        in_specs=[a_spec, b_spec], out_specs=o_spec, out_shape=out_shape,
        scratch_shapes=[pltpu.VMEM((bm, bn) if nk > 1 else (8, LANES), F32)],
        compiler_params=_cparams(("parallel", "parallel", "arbitrary")),
    )(a, b)


ROWS = 512


def _row_spec(cols, bm=ROWS):
    return pl.BlockSpec((bm, cols), lambda i: (i, 0))


def _col_spec(rows, bn=ROWS):
    return pl.BlockSpec((rows, bn), lambda i: (0, i))


def _full_spec(shape):
    return pl.BlockSpec(shape, lambda i: (0,) * len(shape))


def _colsum8(t):
    return jnp.sum(t.reshape(t.shape[0] // 8, 8, t.shape[1]), axis=0)


def _proj_rms_fwd(a, w, res, g, name):
    (S, K), D = a.shape, w.shape[1]

    def body(a_ref, w_ref, res_ref, g_ref, x_ref, o_ref, ot_ref):
        xv = _dot(a_ref[...], w_ref[...], NN) + res_ref[...]
        x_ref[...] = xv
        r = lax.rsqrt(jnp.mean(xv * xv, axis=-1, keepdims=True) + EPS)
        y = xv * r * g_ref[...]
        o_ref[...] = y.astype(BF16)
        ot_ref[...] = y.T.astype(BF16)

    return pl.pallas_call(
        body, name=name, grid=(S // ROWS,),
        in_specs=[_row_spec(K), pl.BlockSpec(w.shape, lambda i: (0, 0), pipeline_mode=pl.Buffered(1)),
                  _row_spec(D), _full_spec((1, D))],
        out_specs=[_row_spec(D), _row_spec(D), _col_spec(D)],
        out_shape=[jax.ShapeDtypeStruct((S, D), F32), jax.ShapeDtypeStruct((S, D), BF16),
                   jax.ShapeDtypeStruct((D, S), BF16)],
        compiler_params=_cparams(("parallel",)),
    )(a, w, res, g)


def _rms_fwd(x, g, name, gather=None):
    S, D = x.shape
    nt = S // ROWS
    n_side = len(gather.arrays) if gather else 0

    def body(*refs):
        x_ref, g_ref = refs[:2]
        o_ref, ot_ref = refs[2 + n_side:4 + n_side]
        i = pl.program_id(0)
        if gather:
            start, forward, finish = gather.steps(refs[2:2 + n_side], refs[4 + n_side:4 + 2 * n_side], refs[4 + 2 * n_side:])
            pl.when(i == 0)(start)
            pl.when(i == 3 * nt // 4)(forward)
        xv = x_ref[...]
        r = lax.rsqrt(jnp.mean(xv * xv, axis=-1, keepdims=True) + EPS)
        y = xv * r * g_ref[...]
        o_ref[...] = y.astype(BF16)
        ot_ref[...] = y.T.astype(BF16)
        if gather:
            pl.when(i == nt - 1)(finish)

    side = gather.arrays if gather else []
    return pl.pallas_call(
        body, name=name, grid=(nt,),
        in_specs=[_row_spec(D), _full_spec((1, D))] + [ANY] * n_side,
        out_specs=[_row_spec(D), _col_spec(D)] + [ANY] * n_side,
        out_shape=[jax.ShapeDtypeStruct((S, D), BF16), jax.ShapeDtypeStruct((D, S), BF16)]
        + (gather.out_shape if gather else []),
        scratch_shapes=gather.scratch if gather else [],
        compiler_params=_cparams(("arbitrary",) if gather else ("parallel",)),
    )(x, g, *side)


RMS_BWD_ROWS = 256


def _rms_bwd(pairs, x, g, dres, name, exchange=None):
    S, D = x.shape
    bm = RMS_BWD_ROWS
    nt = S // bm
    n_pairs = len(pairs)
    n_side = len(exchange.arrays) if exchange else 0
    n_in = 2 * n_pairs + 3

    def body(*refs):
        x_ref, g_ref, dres_ref = refs[2 * n_pairs:n_in]
        dx_ref, dxb_ref, dg_ref = refs[n_in + n_side:n_in + n_side + 3]
        acc_ref = refs[n_in + 2 * n_side + 3]
        i = pl.program_id(0)
        if exchange:
            start, finish = exchange.steps(refs[n_in:n_in + n_side], refs[n_in + n_side + 3:n_in + 2 * n_side + 3],
                                           refs[n_in + 2 * n_side + 4:])
            pl.when(i == 0)(start)
        dyv = sum(_dot(refs[2 * p][...], refs[2 * p + 1][...], NN) for p in range(n_pairs))
        xv = x_ref[...]
        r = lax.rsqrt(jnp.mean(xv * xv, axis=-1, keepdims=True) + EPS)
        xr = xv * r
        u = dyv * g_ref[...]
        dx = r * u - xr * (r * r) * jnp.mean(xv * u, axis=-1, keepdims=True) + dres_ref[...]
        dx_ref[...] = dx
        dxb_ref[...] = dx.astype(BF16)
        part = _colsum8(dyv * xr)

        @pl.when(i == 0)
        def _():
            acc_ref[...] = part

        @pl.when(i > 0)
        def _():
            acc_ref[...] += part

        @pl.when(i == nt - 1)
        def _():
            dg_ref[...] = jnp.sum(acc_ref[...], axis=0, keepdims=True)

        if exchange:
            pl.when(i == nt - 1)(finish)

    rows = lambda cols: pl.BlockSpec((bm, cols), lambda i: (i, 0))
    in_specs = []
    for a, b in pairs:
        in_specs += [rows(a.shape[1]), pl.BlockSpec(b.shape, lambda i: (0, 0), pipeline_mode=pl.Buffered(1))]
    side = exchange.arrays if exchange else []
    return pl.pallas_call(
        body, name=name, grid=(nt,),
        in_specs=in_specs + [rows(D), _full_spec((1, D)), rows(D)] + [ANY] * n_side,
        out_specs=[rows(D), rows(D), _full_spec((1, D))] + [ANY] * n_side,
        out_shape=[jax.ShapeDtypeStruct((S, D), F32), jax.ShapeDtypeStruct((S, D), BF16),
                   jax.ShapeDtypeStruct((1, D), F32)] + (exchange.out_shape if exchange else []),
        scratch_shapes=[pltpu.VMEM((8, D), F32)] + (exchange.scratch if exchange else []),
        compiler_params=_cparams(("arbitrary",)),
    )(*[t for pair in pairs for t in pair], x, g, dres, *side)


def _head_mean(t, blockdiag):
    hi, lo = _split_bf16(t)
    return (_dot(hi, blockdiag, NN) + _dot(lo, blockdiag, NN)) * (1.0 / HEAD_DIM)


def _blockdiag():
    r = lax.broadcasted_iota(jnp.int32, (WIDTH, WIDTH), 0) // HEAD_DIM
    c = lax.broadcasted_iota(jnp.int32, (WIDTH, WIDTH), 1) // HEAD_DIM
    return jnp.where(r == c, 1.0, 0.0).astype(BF16)


def _proj_qknorm_fwd(hn, w, gq, gk, name):
    S, K = hn.shape

    def body(hn_ref, w_ref, gq_ref, gk_ref, qk_ref, q_ref, k_ref):
        bd = _blockdiag()
        qk = _dot(hn_ref[...], w_ref[...], NN)
        qk_ref[...] = qk
        for part, g_ref, o_ref, scale in ((0, gq_ref, q_ref, HEAD_DIM ** -0.5), (1, gk_ref, k_ref, 1.0)):
            t = qk[:, part * WIDTH:(part + 1) * WIDTH]
            r = lax.rsqrt(_head_mean(t * t, bd) + EPS)
            o_ref[...] = (t * r * g_ref[...] * scale).astype(BF16)

    return pl.pallas_call(
        body, name=name, grid=(S // ROWS,),
        in_specs=[_row_spec(K), pl.BlockSpec(w.shape, lambda i: (0, 0), pipeline_mode=pl.Buffered(1)),
                  _full_spec((1, WIDTH)), _full_spec((1, WIDTH))],
        out_specs=[_row_spec(2 * WIDTH), _row_spec(WIDTH), _row_spec(WIDTH)],
        out_shape=[jax.ShapeDtypeStruct((S, 2 * WIDTH), F32)] + [jax.ShapeDtypeStruct((S, WIDTH), BF16)] * 2,
        compiler_params=_cparams(("parallel",)),
    )(hn, w, gq, gk)


def _qknorm_bwd(qk, gq, gk, dqn, dkn, name):
    S = qk.shape[0]
    nt = S // ROWS

    def body(qk_ref, gq_ref, gk_ref, dqn_ref, dkn_ref, dq_ref, dk_ref, dgq_ref, dgk_ref, accq_ref, acck_ref):
        i = pl.program_id(0)
        bd = _blockdiag()
        for part, g_ref, dn_ref, o_ref, dg_ref, acc_ref, scale in (
                (0, gq_ref, dqn_ref, dq_ref, dgq_ref, accq_ref, HEAD_DIM ** -0.5),
                (1, gk_ref, dkn_ref, dk_ref, dgk_ref, acck_ref, 1.0)):
            t = qk_ref[:, part * WIDTH:(part + 1) * WIDTH]
            dn = dn_ref[...] * scale
            r = lax.rsqrt(_head_mean(t * t, bd) + EPS)
            u = dn * g_ref[...]
            dt = r * u - t * (r * r * r) * _head_mean(t * u, bd)
            o_ref[...] = dt.astype(BF16)
            psum = _colsum8(dn * t * r)

            @pl.when(i == 0)
            def _():
                acc_ref[...] = psum

            @pl.when(i > 0)
            def _():
                acc_ref[...] += psum

            @pl.when(i == nt - 1)
            def _():
                dg_ref[...] = jnp.sum(acc_ref[...], axis=0, keepdims=True)

    return pl.pallas_call(
        body, name=name, grid=(nt,),
        in_specs=[_row_spec(2 * WIDTH), _full_spec((1, WIDTH)), _full_spec((1, WIDTH)),
                  _row_spec(WIDTH), _row_spec(WIDTH)],
        out_specs=[_row_spec(WIDTH), _row_spec(WIDTH), _full_spec((1, WIDTH)), _full_spec((1, WIDTH))],
        out_shape=[jax.ShapeDtypeStruct((S, WIDTH), BF16)] * 2 + [jax.ShapeDtypeStruct((1, WIDTH), F32)] * 2,
        scratch_shapes=[pltpu.VMEM((8, WIDTH), F32)] * 2,
        compiler_params=_cparams(("arbitrary",)),
    )(qk, gq, gk, dqn, dkn)


def _gate_specs(D):
    return [pl.BlockSpec((ROWS, D), lambda i: (i, 0)), pl.BlockSpec((ROWS, D), lambda i: (i, 1))]


def _branch_mix_fwd(out_a, w_a, out_b, w_b, gates, name):
    (S, K), D = out_a.shape, w_a.shape[1]
    wspec = pl.BlockSpec(w_a.shape, lambda i: (0, 0), pipeline_mode=pl.Buffered(1))

    def body(oa_ref, wa_ref, ob_ref, wb_ref, ga_ref, gb_ref, ya_ref, yb_ref, o_ref, ot_ref):
        ya = _dot(oa_ref[...], wa_ref[...], NN)
        yb = _dot(ob_ref[...], wb_ref[...], NN)
        ya_ref[...] = ya.astype(BF16)
        yb_ref[...] = yb.astype(BF16)
        m = _sigmoid(ga_ref[...].astype(F32)) * ya + _sigmoid(gb_ref[...].astype(F32)) * yb
        o_ref[...] = m.astype(BF16)
        ot_ref[...] = m.T.astype(BF16)

    return pl.pallas_call(
        body, name=name, grid=(S // ROWS,),
        in_specs=[_row_spec(K), wspec, _row_spec(K), wspec] + _gate_specs(D),
        out_specs=[_row_spec(D), _row_spec(D), _row_spec(D), _col_spec(D)],
        out_shape=[jax.ShapeDtypeStruct((S, D), BF16)] * 3 + [jax.ShapeDtypeStruct((D, S), BF16)],
        compiler_params=_cparams(("parallel",)),
    )(out_a, w_a, out_b, w_b, gates, gates)


def _mix_bwd(dx, w_t, gates, ya, yb, name):
    S, D = ya.shape

    def body(dx_ref, w_ref, ga_ref, gb_ref, ya_ref, yb_ref, dg_ref, dya_ref, dyb_ref):
        dmv = _dot(dx_ref[...], w_ref[...], NN)
        for half, (g_ref, y_ref, dy_ref) in enumerate(((ga_ref, ya_ref, dya_ref), (gb_ref, yb_ref, dyb_ref))):
            s = _sigmoid(g_ref[...].astype(F32))
            dy_ref[...] = (dmv * s).astype(BF16)
            dg_ref[:, half * D:(half + 1) * D] = (dmv * y_ref[...].astype(F32) * s * (1.0 - s)).astype(BF16)

    return pl.pallas_call(
        body, name=name, grid=(S // ROWS,),
        in_specs=[_row_spec(D), pl.BlockSpec(w_t.shape, lambda i: (0, 0), pipeline_mode=pl.Buffered(1))]
        + _gate_specs(D) + [_row_spec(D)] * 2,
        out_specs=[_row_spec(2 * D), _row_spec(D), _row_spec(D)],
        out_shape=[jax.ShapeDtypeStruct((S, 2 * D), BF16)] + [jax.ShapeDtypeStruct((S, D), BF16)] * 2,
        compiler_params=_cparams(("parallel",)),
    )(dx, w_t, gates, gates, ya, yb)


def _down_and_loss(act, w, x2, target, name):
    (S, K), D = act.shape, w.shape[1]
    nt = S // ROWS

    def body(a_ref, w_ref, x_ref, t_ref, dy_ref, dyb_ref, p_ref):
        err = _dot(a_ref[...], w_ref[...], NN) + x_ref[...] - t_ref[...]
        dy = err * (1.0 / D)
        dy_ref[...] = dy
        dyb_ref[...] = dy.astype(BF16)
        sq = _colsum8(err * err)
        acc = sq[:, 0:LANES]
        for k in range(1, D // LANES):
            acc = acc + sq[:, k * LANES:(k + 1) * LANES]
        p_ref[...] = acc

    return pl.pallas_call(
        body, name=name, grid=(nt,),
        in_specs=[_row_spec(K), pl.BlockSpec((K, D), lambda i: (0, 0), pipeline_mode=pl.Buffered(1)),
                  _row_spec(D), _row_spec(D)],
        out_specs=[_row_spec(D), _row_spec(D), pl.BlockSpec((8, LANES), lambda i: (i, 0))],
        out_shape=[jax.ShapeDtypeStruct((S, D), F32), jax.ShapeDtypeStruct((S, D), BF16),
                   jax.ShapeDtypeStruct((nt * 8, LANES), F32)],
        compiler_params=_cparams(("parallel",)),
    )(act, w, x2, target)


CONV_COLS = D_FF // 2
HALO = 16
CONV_CHUNK = 64


def _aligned(start, multiple):
    return start if isinstance(start, int) else pl.multiple_of(start, multiple)


def _conv_taps(xe, cw, cb):
    taps = (pltpu.roll(xe, 2, 0), pltpu.roll(xe, 1, 0), xe)
    return taps, cw[0:1] * taps[0] + cw[1:2] * taps[1] + cw[2:3] * taps[2] + cb


def _conv_specs(nt):
    hb, nb = ROWS // HALO, D_FF // CONV_COLS
    specs = {}
    for part, off in (("gate", 0), ("up", nb)):
        specs[part] = dict(
            main=pl.BlockSpec((ROWS, CONV_COLS), functools.partial(lambda c, i, off: (i, c + off), off=off)),
            prev=pl.BlockSpec((HALO, CONV_COLS),
                              functools.partial(lambda c, i, off: (jnp.maximum(i * hb - 1, 0), c + off), off=off)),
            nxt=pl.BlockSpec((HALO, CONV_COLS),
                             functools.partial(lambda c, i, off: (jnp.minimum((i + 1) * hb, nt * hb - 1), c + off), off=off)),
            w=pl.BlockSpec((3, CONV_COLS), functools.partial(lambda c, i, off: (0, c + off), off=off)),
            b=pl.BlockSpec((1, CONV_COLS), functools.partial(lambda c, i, off: (0, c + off), off=off)))
    return specs


def _convglu_fwd(hid, cw, cb, name):
    S = hid.shape[0]
    sp = _conv_specs(S // ROWS)

    def body(hg_ref, hgp_ref, hu_ref, hup_ref, cwg_ref, cwu_ref, cbg_ref, cbu_ref, o_ref, ot_ref):
        i = pl.program_id(1)
        keep = (i > 0).astype(F32)

        def conv(h_ref, hp_ref, cw_ref, cb_ref):
            xe = jnp.concatenate([hp_ref[...].astype(F32) * keep, h_ref[...].astype(F32)], axis=0)
            return _conv_taps(xe, cw_ref[...], cb_ref[...])[1][HALO:, :]

        gate = conv(hg_ref, hgp_ref, cwg_ref, cbg_ref)
        up = conv(hu_ref, hup_ref, cwu_ref, cbu_ref)
        act = gate * _sigmoid(gate) * up
        o_ref[...] = act.astype(BF16)
        ot_ref[...] = act.T.astype(BF16)

    g, u = sp["gate"], sp["up"]
    return pl.pallas_call(
        body, name=name, grid=(D_FF // CONV_COLS, S // ROWS),
        in_specs=[g["main"], g["prev"], u["main"], u["prev"], g["w"], u["w"], g["b"], u["b"]],
        out_specs=[g["main"], pl.BlockSpec((CONV_COLS, ROWS), lambda c, i: (c, i))],
        out_shape=[jax.ShapeDtypeStruct((S, D_FF), BF16), jax.ShapeDtypeStruct((D_FF, S), BF16)],
        compiler_params=_cparams(("parallel", "parallel")),
    )(hid, hid, hid, hid, cw, cw, cb, cb)


def _convglu_bwd(hid, dact, cw, cb, name):
    S = hid.shape[0]
    nt = S // ROWS
    sp = _conv_specs(nt)

    n_chunks = ROWS // CONV_CHUNK

    def body(hg_ref, hgp_ref, hgn_ref, hu_ref, hup_ref, hun_ref, da_ref, dan_ref,
             cwg_ref, cwu_ref, cbg_ref, cbu_ref,
             dhg_ref, dhu_ref, dcwg_ref, dcwu_ref, dcbg_ref, dcbu_ref, xg_s, xu_s, da_s):
        i = pl.program_id(1)
        kp = (i > 0).astype(F32)
        kn = (i < nt - 1).astype(F32)
        for x_s, h_ref, hp_ref, hn_ref in ((xg_s, hg_ref, hgp_ref, hgn_ref), (xu_s, hu_ref, hup_ref, hun_ref)):
            x_s[0:HALO, :] = hp_ref[...].astype(F32) * kp
            x_s[HALO:HALO + ROWS, :] = h_ref[...].astype(F32)
            x_s[HALO + ROWS:, :] = hn_ref[...].astype(F32) * kn
        da_s[0:ROWS, :] = da_ref[...].astype(F32)
        da_s[ROWS:, :] = dan_ref[...].astype(F32) * kn

        @pl.when(i == 0)
        def _():
            for ref in (dcwg_ref, dcwu_ref, dcbg_ref, dcbu_ref):
                ref[...] = jnp.zeros_like(ref)

        def lane_group(grp, _):
            lanes = pl.ds(pl.multiple_of(grp * LANES, LANES), LANES)
            cwg, cwu, cbg, cbu = cwg_ref[:, lanes], cwu_ref[:, lanes], cbg_ref[:, lanes], cbu_ref[:, lanes]

            def grads(r0, n):
                rows = pl.ds(_aligned(r0 + HALO - 8, 8), n + 8)
                taps_g, gate = _conv_taps(xg_s[rows, lanes], cwg, cbg)
                taps_u, up = _conv_taps(xu_s[rows, lanes], cwu, cbu)
                gate, up = gate[8:], up[8:]
                da = da_s[pl.ds(_aligned(r0, 8), n), lanes]
                sg = _sigmoid(gate)
                return (da * up * sg * (1.0 + gate * (1.0 - sg)), da * gate * sg,
                        [t[8:] for t in taps_g], [t[8:] for t in taps_u])

            def chunk(step, carry):
                below_g, below_u, accs = carry
                r0 = (n_chunks - 1 - step) * CONV_CHUNK
                dg, du, taps_g, taps_u = grads(r0, CONV_CHUNK)
                new_accs = []
                for d, below, cwv, taps, dh_ref, acc in ((dg, below_g, cwg, taps_g, dhg_ref, accs[0]),
                                                        (du, below_u, cwu, taps_u, dhu_ref, accs[1])):
                    ext = jnp.concatenate([d, below], axis=0)
                    n_ext = CONV_CHUNK + 8
                    dh = (cwv[2:3] * d + cwv[1:2] * pltpu.roll(ext, n_ext - 1, 0)[:CONV_CHUNK]
                          + cwv[0:1] * pltpu.roll(ext, n_ext - 2, 0)[:CONV_CHUNK])
                    dh_ref[pl.ds(_aligned(r0, CONV_CHUNK), CONV_CHUNK), lanes] = dh.astype(BF16)
                    new_accs.append(tuple(a + _colsum8(d * tap) for a, tap in zip(acc[:3], taps))
                                    + (acc[3] + _colsum8(d),))
                return dg[0:8], du[0:8], tuple(new_accs)

            below_g, below_u, _, _ = grads(ROWS, 8)
            zero = jnp.zeros((8, LANES), F32)
            _, _, accs = lax.fori_loop(0, n_chunks, chunk, (below_g, below_u, ((zero,) * 4, (zero,) * 4)))
            for acc, dcw_ref, dcb_ref in ((accs[0], dcwg_ref, dcbg_ref), (accs[1], dcwu_ref, dcbu_ref)):
                for t in range(3):
                    dcw_ref[t:t + 1, lanes] += jnp.sum(acc[t], axis=0, keepdims=True)
                dcb_ref[:, lanes] += jnp.sum(acc[3], axis=0, keepdims=True)
            return 0

        lax.fori_loop(0, CONV_COLS // LANES, lane_group, 0)

    g, u = sp["gate"], sp["up"]
    return pl.pallas_call(
        body, name=name, grid=(D_FF // CONV_COLS, nt),
        in_specs=[g["main"], g["prev"], g["nxt"], u["main"], u["prev"], u["nxt"], g["main"], g["nxt"],
                  g["w"], u["w"], g["b"], u["b"]],
        out_specs=[g["main"], g["main"], g["w"], g["w"], g["b"], g["b"]],
        out_shape=[jax.ShapeDtypeStruct((S, D_FF), BF16)] * 2 + [jax.ShapeDtypeStruct((3, D_FF), F32)] * 2
        + [jax.ShapeDtypeStruct((1, D_FF), F32)] * 2,
        scratch_shapes=[pltpu.VMEM((ROWS + 2 * HALO, CONV_COLS), F32)] * 2 + [pltpu.VMEM((ROWS + HALO, CONV_COLS), F32)],
        compiler_params=_cparams(("parallel", "arbitrary")),
    )(hid, hid, hid, hid, hid, hid, dact, dact, cw, cw, cb, cb)


REL_PAD = 384
DIAG = 1024


def _band_valid():
    qc = lax.broadcasted_iota(jnp.int32, (BQ, KWIN), 0) // CHUNK
    kc = lax.broadcasted_iota(jnp.int32, (BQ, KWIN), 1) // CHUNK - LEFT_CHUNKS
    return (kc <= qc) & (kc >= qc - LEFT_CHUNKS)


def _rel_index(offset):
    return jnp.clip(BAND - offset, -MAX_REL, MAX_REL) + MAX_REL


def _split3(x):
    hi = x.astype(BF16)
    rest = x - hi.astype(F32)
    mid = rest.astype(BF16)
    return hi, mid, (rest - mid.astype(F32)).astype(BF16)


def _bias_table(rel_bias, name):
    def body(rb_ref, o_ref):
        t = lax.broadcasted_iota(jnp.int32, (REL_PAD, DIAG), 0)
        lane = lax.broadcasted_iota(jnp.int32, (REL_PAD, DIAG), 1)
        pick = jnp.where(t == _rel_index(lane - BQ), 1.0, 0.0).astype(BF16)
        base = sum(_dot(piece, pick, NN) for piece in _split3(rb_ref[...]))
        valid = _band_valid()
        for h in range(N_HEADS):
            rows = jnp.broadcast_to(base[h:h + 1], (BQ, DIAG))
            rolled = pltpu.roll(rows, 0, 1, stride=1, stride_axis=0)
            o_ref[h] = jnp.where(valid, rolled[:, BQ:], NEG)

    return pl.pallas_call(
        body, name=name,
        out_shape=jax.ShapeDtypeStruct((N_HEADS, BQ, KWIN), F32),
        compiler_params=_cparams(),
    )(rel_bias)


def _bias_table_bwd(dtab, name):
    def body(d_ref, o_ref, diag_ref):
        r = lax.broadcasted_iota(jnp.int32, (BQ, BQ), 0)
        c = lax.broadcasted_iota(jnp.int32, (BQ, BQ), 1)
        flip = jnp.where(r + c == BQ - 1, 1.0, 0.0).astype(BF16)
        for h in range(N_HEADS):
            flipped = sum(_dot(flip, piece, NN) for piece in _split3(d_ref[h]))
            padded = jnp.concatenate([flipped, jnp.zeros((BQ, DIAG - KWIN), F32)], axis=1)
            rolled = pltpu.roll(padded, DIAG - (BQ - 1), 1, stride=1, stride_axis=0)
            diag_ref[h:h + 1, :] = jnp.sum(rolled, axis=0, keepdims=True)
        lane = lax.broadcasted_iota(jnp.int32, (DIAG, REL_PAD), 0)
        t = lax.broadcasted_iota(jnp.int32, (DIAG, REL_PAD), 1)
        offset = jnp.where(lane < KWIN, lane, lane - DIAG)
        pick = jnp.where(t == _rel_index(offset), 1.0, 0.0).astype(BF16)
        o_ref[...] = sum(_dot(piece, pick, NN) for piece in _split3(diag_ref[...]))

    return pl.pallas_call(
        body, name=name,
        out_shape=jax.ShapeDtypeStruct((N_HEADS, REL_PAD), F32),
        scratch_shapes=[pltpu.VMEM((N_HEADS, DIAG), F32)],
        compiler_params=_cparams(),
    )(dtab)


def _head_masks(heads=2):
    lane = lax.broadcasted_iota(jnp.int32, (1, heads * HEAD_DIM), 1)
    return [lane // HEAD_DIM == h for h in range(heads)]


def _own_lanes(masks, vals):
    out = vals[-1]
    for m, val in zip(masks[-2::-1], vals[-2::-1]):
        out = jnp.where(m, val, out)
    return out


CA_HEADS = 4
CA_LANES = CA_HEADS * HEAD_DIM


def _ca_window_specs(nq, col_off=0):
    return [pl.BlockSpec((BQ, CA_LANES), functools.partial(
        lambda p, i, d: (jnp.clip(i - 2 + d, 0, nq - 1), p + col_off), d=d)) for d in range(3)]


def _softmax_rows(s):
    p = jnp.exp(s - jnp.max(s, axis=-1, keepdims=True))
    return p, jnp.sum(p, axis=-1, keepdims=True)


def _ca_scores(qm, kc, tab_h, i):
    col = lax.broadcasted_iota(jnp.int32, (1, KWIN), 1)
    in_seq = col + (i - 2) * BQ >= 0
    return jnp.where(in_seq, _dot(qm, kc, NT) + tab_h, NEG)


def _ca_fwd(qn, kn, v, tab, name, v_off=0, gather=None):
    S = qn.shape[0]
    nq = S // BQ
    groups = WIDTH // CA_LANES
    qspec = pl.BlockSpec((BQ, CA_LANES), lambda p, i: (i, p))
    tspec = pl.BlockSpec((CA_HEADS, BQ, KWIN), lambda p, i: (p, 0, 0))
    n_side = len(gather.arrays) if gather else 0

    def body(*refs):
        q_ref, k0, k1, k2, v0, v1, v2, tab_ref = refs[:8]
        o_ref, ot_ref = refs[8 + n_side:10 + n_side]
        p, i = pl.program_id(0), pl.program_id(1)
        if gather:
            start, forward, finish = gather.steps(refs[8:8 + n_side], refs[10 + n_side:10 + 2 * n_side],
                                                  refs[10 + 2 * n_side:])
            pl.when((p == 0) & (i == 0))(start)
            pl.when((p == groups - 1) & (i == nq // 2))(forward)
        kc = jnp.concatenate([k0[...], k1[...], k2[...]], axis=0)
        vc = jnp.concatenate([v0[...], v1[...], v2[...]], axis=0)
        qv = q_ref[...]
        masks = _head_masks(CA_HEADS)
        heads = range(CA_HEADS)
        s = [_ca_scores(jnp.where(masks[h], qv, 0), kc, tab_ref[h], i) for h in heads]
        soft = [_softmax_rows(s[h]) for h in heads]
        o = [_dot(soft[h][0].astype(BF16), vc, NN) / soft[h][1] for h in heads]
        out = _own_lanes(masks, o)
        o_ref[...] = out.astype(BF16)
        ot_ref[...] = out.T.astype(BF16)
        if gather:
            pl.when((p == groups - 1) & (i == nq - 1))(finish)

    side = gather.arrays if gather else []
    return pl.pallas_call(
        body, name=name, grid=(groups, nq),
        in_specs=[qspec] + _ca_window_specs(nq) + _ca_window_specs(nq, v_off) + [tspec] + [ANY] * n_side,
        out_specs=[qspec, pl.BlockSpec((CA_LANES, BQ), lambda p, i: (p, i))] + [ANY] * n_side,
        out_shape=[jax.ShapeDtypeStruct((S, WIDTH), BF16), jax.ShapeDtypeStruct((WIDTH, S), BF16)]
        + (gather.out_shape if gather else []),
        scratch_shapes=gather.scratch if gather else [],
        compiler_params=_cparams(("arbitrary", "arbitrary") if gather else ("parallel", "parallel")),
    )(qn, kn, kn, kn, v, v, v, tab, *side)


def _ca_bwd(qn, kn, v, do, tab, name, v_off=0, exchange=None):
    S = qn.shape[0]
    nq = S // BQ
    groups = WIDTH // CA_LANES
    qspec = pl.BlockSpec((BQ, CA_LANES), lambda p, i: (jnp.minimum(i, nq - 1), p))
    kout = pl.BlockSpec((BQ, CA_LANES), lambda p, i: (jnp.clip(i - 2, 0, nq - 1), p))
    tspec = pl.BlockSpec((CA_HEADS, BQ, KWIN), lambda p, i: (p, 0, 0))
    n_side = len(exchange.arrays) if exchange else 0

    def body(*refs):
        q_ref, do_ref, k0, k1, k2, v0, v1, v2, tab_ref = refs[:9]
        dq_ref, dk_ref, dv_ref, dtab_ref = refs[9 + n_side:13 + n_side]
        dk_acc, dv_acc = refs[13 + 2 * n_side:15 + 2 * n_side]
        i = pl.program_id(1)
        if exchange:
            start, finish = exchange.steps(refs[9:9 + n_side], refs[13 + n_side:13 + 2 * n_side], refs[15 + 2 * n_side:])
            pl.when((pl.program_id(0) == 0) & (i == 0))(start)

        @pl.when(i == 0)
        def _():
            dk_acc[...] = jnp.zeros_like(dk_acc)
            dv_acc[...] = jnp.zeros_like(dv_acc)
            dtab_ref[...] = jnp.zeros_like(dtab_ref)

        @pl.when(i < nq)
        def _():
            kc = jnp.concatenate([k0[...], k1[...], k2[...]], axis=0)
            vc = jnp.concatenate([v0[...], v1[...], v2[...]], axis=0)
            qv, dov = q_ref[...], do_ref[...]
            masks = _head_masks(CA_HEADS)
            heads = range(CA_HEADS)
            qm = [jnp.where(masks[h], qv, 0) for h in heads]
            dom = [jnp.where(masks[h], dov, 0) for h in heads]
            s = [_ca_scores(qm[h], kc, tab_ref[h], i) for h in heads]
            dp = [_dot(dom[h], vc, NT) for h in heads]
            soft = [_softmax_rows(s[h]) for h in heads]
            p = [soft[h][0] / soft[h][1] for h in heads]
            ds = [p[h] * (dp[h] - jnp.sum(p[h] * dp[h], axis=-1, keepdims=True)) for h in heads]
            for h in heads:
                dtab_ref[h] += ds[h]
            dsb = [ds[h].astype(BF16) for h in heads]
            pb = [p[h].astype(BF16) for h in heads]
            dq = [_dot(dsb[h], kc, NN) for h in heads]
            dq_ref[...] = _own_lanes(masks, dq)
            dkc = sum(_dot(dsb[h], qm[h], TN) for h in heads)
            dvc = sum(_dot(pb[h], dom[h], TN) for h in heads)
            for d in range(3):
                slot = (i + 1 + d) % 3
                dk_acc[slot] += dkc[d * BQ:(d + 1) * BQ]
                dv_acc[slot] += dvc[d * BQ:(d + 1) * BQ]

        @pl.when(i >= 2)
        def _():
            slot = (i + 1) % 3
            dk_ref[...] = dk_acc[slot]
            dv_ref[...] = dv_acc[slot].astype(BF16)
            dk_acc[slot] = jnp.zeros((BQ, CA_LANES), F32)
            dv_acc[slot] = jnp.zeros((BQ, CA_LANES), F32)

        if exchange:
            pl.when((pl.program_id(0) == groups - 1) & (i == nq + 1))(finish)

    side = exchange.arrays if exchange else []
    return pl.pallas_call(
        body, name=name, grid=(groups, nq + 2),
        in_specs=[qspec, qspec] + _ca_window_specs(nq) + _ca_window_specs(nq, v_off) + [tspec] + [ANY] * n_side,
        out_specs=[qspec, kout, kout, tspec] + [ANY] * n_side,
        out_shape=[jax.ShapeDtypeStruct((S, WIDTH), F32), jax.ShapeDtypeStruct((S, WIDTH), F32),
                   jax.ShapeDtypeStruct((S, WIDTH), BF16), jax.ShapeDtypeStruct((N_HEADS, BQ, KWIN), F32)]
        + (exchange.out_shape if exchange else []),
        scratch_shapes=[pltpu.VMEM((3, BQ, CA_LANES), F32)] * 2 + (exchange.scratch if exchange else []),
        compiler_params=_cparams(("arbitrary", "arbitrary") if exchange else ("parallel", "arbitrary")),
    )(qn, do, kn, kn, kn, v, v, v, tab, *side)


def _sb_consts():
    r = lax.broadcasted_iota(jnp.int32, (BQ, BQ), 0)
    c = lax.broadcasted_iota(jnp.int32, (BQ, BQ), 1)
    from_s = jnp.where(r >= c, 1.0, 0.0).astype(BF16)
    causal = c < r
    return from_s, causal


def _suffix_sum(t, from_s):
    hi, lo = _split_bf16(t)
    return _dot(hi, from_s, NN) + _dot(lo, from_s, NN)


def _neg_abs(x):
    bits = lax.bitcast_convert_type(x, jnp.uint32) | jnp.uint32(0x80000000)
    return lax.bitcast_convert_type(bits, F32)


def _sb_log_keep(zn):
    return jnp.minimum(zn, 0.0) - jnp.log(1.0 + jnp.exp(_neg_abs(zn)))


SB_DEAD = 105.0


SB_QB = 2
SB_QB_FWD = 4


def _sb_walk(ip, tiles, keep_ref, qb=SB_QB):
    i0 = qb * ip

    @pl.when(ip == 0)
    def _():
        tiles([(0, [0], [True])] + [(a, [a, a - 1], [True, False]) for a in range(1, qb)])

    @pl.when(ip > 0)
    def _():
        tiles([(a, [i0 + a, i0 + a - 1], [True, False]) for a in range(qb)])

    for a in range(qb):
        def alive(a=a):
            return (jnp.max(keep_ref[2 * a:2 * a + 2]) > -SB_DEAD).astype(jnp.int32)

        def step(state, a=a, alive=alive):
            j, _ = state
            tiles([(a, [j], [False])])
            return j - 1, alive()

        lax.while_loop(lambda state: (state[0] >= 0) & (state[1] > 0), step, (i0 + a - 2, alive()))


def _sb_rows(j):
    return pl.ds(pl.multiple_of(j * BQ, BQ), BQ)


def _sb_chains(groups):
    chains = [(a, n, h) for a, js, _ in groups for n in range(len(js)) for h in range(2)]
    block = {(a, n): j for a, js, _ in groups for n, j in enumerate(js)}
    masked = [(a, n, h) for a, _, diags in groups for n, d in enumerate(diags) if d for h in range(2)]
    return chains, block, masked


def _sb_running(ref, vals, groups):
    before_chain = {}
    for a, js, _ in groups:
        for h in range(2):
            run = ref[2 * a + h]
            for n in range(len(js)):
                before_chain[(a, n, h)] = run
                run = run + jnp.sum(vals[(a, n, h)], axis=-1, keepdims=True)
            ref[2 * a + h] = run
    return before_chain


def _sb_specs(S, offs, qb=SB_QB):
    def qspec(off=0):
        return pl.BlockSpec((qb * BQ, PAIR), lambda p, i: (i, p + off))

    def kspec(off=0):
        return pl.BlockSpec((S, PAIR), lambda p, i: (0, p + off), pipeline_mode=pl.Buffered(1))

    return qspec, kspec, [qspec(offs[0]), kspec(offs[1]), kspec(offs[2])]


def _sb_fwd(q, k, v, name, offs=(0, 0, 0)):
    S = q.shape[0]
    qb = SB_QB_FWD
    steps = S // (qb * BQ)
    qspec, _, qkv_specs = _sb_specs(S, offs, qb)

    def body(q_ref, k_ref, v_ref, o_ref, of_ref, ot_ref, carry_ref, acc_ref):
        ip = pl.program_id(1)
        from_s, causal = _sb_consts()
        masks = _head_masks()
        qn = q_ref[...] * -(HEAD_DIM ** -0.5)
        qms = {(a, h): jnp.where(masks[h], qn[a * BQ:(a + 1) * BQ], 0) for a in range(qb) for h in range(2)}
        carry_ref[...] = jnp.zeros_like(carry_ref)
        acc_ref[...] = jnp.zeros_like(acc_ref)

        def tiles(groups):
            chains, block, masked = _sb_chains(groups)
            kbs = {an: k_ref[_sb_rows(j), :] for an, j in block.items()}
            vbs = {an: v_ref[_sb_rows(j), :] for an, j in block.items()}
            zn = {c: _dot(qms[(c[0], c[2])], kbs[c[:2]], NT) for c in chains}
            log_keep = {c: _sb_log_keep(zn[c]) for c in chains}
            for c in masked:
                log_keep[c] = jnp.where(causal, log_keep[c], 0.0)
            split = {c: _split_bf16(log_keep[c]) for c in chains}
            carry = _sb_running(carry_ref, log_keep, groups)
            suffix = {c: _dot(split[c][0], from_s, NN) + _dot(split[c][1], from_s, NN) for c in chains}
            w = {c: jnp.exp(carry[c] + suffix[c] - zn[c]) for c in chains}
            for c in masked:
                w[c] = jnp.where(causal, w[c], 0.0)
            for c in chains:
                acc_ref[2 * c[0] + c[2]] += _dot(w[c].astype(BF16), vbs[c[:2]], NN)

        _sb_walk(ip, tiles, carry_ref, qb)
        for a in range(qb):
            out = jnp.where(masks[0], acc_ref[2 * a], acc_ref[2 * a + 1])
            o_ref[a * BQ:(a + 1) * BQ, :] = out.astype(BF16)
            of_ref[a * BQ:(a + 1) * BQ, :] = out
            ot_ref[:, a * BQ:(a + 1) * BQ] = out.T.astype(BF16)

    return pl.pallas_call(
        body, name=name, grid=(WIDTH // PAIR, steps),
        in_specs=qkv_specs, out_specs=[qspec(), qspec(), pl.BlockSpec((PAIR, qb * BQ), lambda p, i: (p, i))],
        out_shape=[jax.ShapeDtypeStruct((S, WIDTH), BF16), jax.ShapeDtypeStruct((S, WIDTH), F32),
                   jax.ShapeDtypeStruct((WIDTH, S), BF16)],
        scratch_shapes=[pltpu.VMEM((2 * qb, BQ, 1), F32), pltpu.VMEM((2 * qb, BQ, PAIR), F32)],
        compiler_params=_cparams(("parallel", "arbitrary")),
    )(q, k, v)


def _sb_bwd(q, k, v, o, do, name, offs=(0, 0, 0), exchange=None):
    S = q.shape[0]
    steps = S // (SB_QB * BQ)
    pairs = WIDTH // PAIR
    qspec, kspec, qkv_specs = _sb_specs(S, offs)
    n_side = len(exchange.arrays) if exchange else 0

    def body(*refs):
        q_ref, o_ref, do_ref, k_ref, v_ref = refs[:5]
        dq_ref, dk_ref, dv_ref = refs[5 + n_side:8 + n_side]
        dk_acc, dv_acc, keep_ref, gsum_ref, dq_acc = refs[8 + 2 * n_side:13 + 2 * n_side]
        ip = pl.program_id(1)
        if exchange:
            start, finish = exchange.steps(refs[5:5 + n_side], refs[8 + n_side:8 + 2 * n_side], refs[13 + 2 * n_side:])
            pl.when((pl.program_id(0) == 0) & (ip == 0))(start)

        @pl.when(ip == 0)
        def _():
            dk_acc[...] = jnp.zeros_like(dk_acc)
            dv_acc[...] = jnp.zeros_like(dv_acc)

        from_s, causal = _sb_consts()
        masks = _head_masks()
        qn, dov = q_ref[...] * -(HEAD_DIM ** -0.5), do_ref[...]
        od = o_ref[...] * dov.astype(F32)
        lanes = [(a, h) for a in range(SB_QB) for h in range(2)]
        rows_of = {a: slice(a * BQ, (a + 1) * BQ) for a in range(SB_QB)}
        qms = {(a, h): jnp.where(masks[h], qn[rows_of[a]], 0) for a, h in lanes}
        doms = {(a, h): jnp.where(masks[h], dov[rows_of[a]], 0) for a, h in lanes}
        totals = {(a, h): jnp.sum(jnp.where(masks[h], od[rows_of[a]], 0.0), axis=-1, keepdims=True)
                  for a, h in lanes}
        for ref in (keep_ref, gsum_ref, dq_acc):
            ref[...] = jnp.zeros_like(ref)

        def tiles(groups):
            chains, block, masked = _sb_chains(groups)
            kbs = {an: k_ref[_sb_rows(j), :] for an, j in block.items()}
            vbs = {an: v_ref[_sb_rows(j), :] for an, j in block.items()}
            zn = {c: _dot(qms[(c[0], c[2])], kbs[c[:2]], NT) for c in chains}
            dw = {c: _dot(doms[(c[0], c[2])], vbs[c[:2]], NT) for c in chains}
            log_keep = {c: _sb_log_keep(zn[c]) for c in chains}
            for c in masked:
                log_keep[c] = jnp.where(causal, log_keep[c], 0.0)
            split = {c: _split_bf16(log_keep[c]) for c in chains}
            kept = _sb_running(keep_ref, log_keep, groups)
            suffix = {c: _dot(split[c][0], from_s, NN) + _dot(split[c][1], from_s, NN) for c in chains}
            w = {c: jnp.exp(kept[c] + suffix[c] - zn[c]) for c in chains}
            for c in masked:
                w[c] = jnp.where(causal, w[c], 0.0)
            wb = {c: w[c].astype(BF16) for c in chains}
            g = {c: wb[c].astype(F32) * dw[c] for c in chains}
            gsplit = {c: _split_bf16(g[c]) for c in chains}
            gsum = _sb_running(gsum_ref, g, groups)
            gsuffix = {c: _dot(gsplit[c][0], from_s, NN) + _dot(gsplit[c][1], from_s, NN) for c in chains}
            dzb = {}
            for c in chains:
                before = totals[(c[0], c[2])] - (gsum[c] + gsuffix[c])
                dz = (g[c] + before) * jnp.exp(log_keep[c]) - before
                if c in masked:
                    dz = jnp.where(causal, dz, 0.0)
                dzb[c] = dz.astype(BF16)
            for c in chains:
                rows = _sb_rows(block[c[:2]])
                dq_acc[2 * c[0] + c[2]] += _dot(dzb[c], kbs[c[:2]], NN)
                dk_acc[rows, :] -= _dot(dzb[c], qms[(c[0], c[2])], TN)
                dv_acc[rows, :] += _dot(wb[c], doms[(c[0], c[2])], TN)

        _sb_walk(ip, tiles, keep_ref)
        for a in range(SB_QB):
            dq = jnp.where(masks[0], dq_acc[2 * a], dq_acc[2 * a + 1])
            dq_ref[a * BQ:(a + 1) * BQ, :] = (dq * HEAD_DIM ** -0.5).astype(BF16)

        @pl.when(ip == steps - 1)
        def _():
            dk_ref[...] = dk_acc[...].astype(BF16)
            dv_ref[...] = dv_acc[...].astype(BF16)

        if exchange:
            pl.when((pl.program_id(0) == pairs - 1) & (ip == steps - 1))(finish)

    side = exchange.arrays if exchange else []
    return pl.pallas_call(
        body, name=name, grid=(pairs, steps),
        in_specs=[qkv_specs[0], qspec(), qspec(), qkv_specs[1], qkv_specs[2]] + [ANY] * n_side,
        out_specs=[qspec(), kspec(), kspec()] + [ANY] * n_side,
        out_shape=[jax.ShapeDtypeStruct((S, WIDTH), BF16)] * 3 + (exchange.out_shape if exchange else []),
        scratch_shapes=[pltpu.VMEM((S, PAIR), F32)] * 2 + [pltpu.VMEM((2 * SB_QB, BQ, 1), F32)] * 2
        + [pltpu.VMEM((2 * SB_QB, BQ, PAIR), F32)] + (exchange.scratch if exchange else []),
        compiler_params=_cparams(("arbitrary", "arbitrary") if exchange else ("parallel", "arbitrary")),
    )(q, o, do, k, v, *side)


ANY = pl.BlockSpec(memory_space=pl.ANY)


def _place():
    return lax.axis_index("x"), lax.axis_index("y"), lax.axis_index("c")


def _other_chips(x, y):
    return [(2 * px + py, (px, py)) for px, py in ((1 - x, y), (x, 1 - y), (1 - x, 1 - y))]


def _remote(src, dst, sems, k, to):
    return pltpu.make_async_remote_copy(src_ref=src, dst_ref=dst, send_sem=sems[0].at[k], recv_sem=sems[1].at[k],
                                        device_id=to, device_id_type=MESH)


class _Gather:
    def __init__(self, ws, extras=()):
        self.n, self.m = len(ws), len(extras)
        self.arrays = list(ws) + list(extras)
        self.n_copies = 6 * self.n + 3 * self.m
        self.out_shape = [jax.ShapeDtypeStruct((N_CHIPS,) + a.shape, a.dtype) for a in self.arrays]
        self.scratch = [pltpu.SemaphoreType.DMA((self.n_copies,)), pltpu.SemaphoreType.DMA((self.n_copies,))]

    def steps(self, in_refs, out_refs, sems):
        n = self.n
        x, y, c = _place()
        me = 2 * x + y
        chips = _other_chips(x, y)
        sibling = (x, y, 1 - c)

        def halves(ref):
            rh = ref.shape[-2] // 2
            return pl.ds(c * rh, rh), pl.ds((1 - c) * rh, rh)

        def first():
            cps = [_remote(w_ref.at[halves(w_ref)[0]], o_ref.at[me, halves(w_ref)[0]], sems, 6 * a + k, (*xy, c))
                   for a, (w_ref, o_ref) in enumerate(zip(in_refs[:n], out_refs[:n])) for k, (_, xy) in enumerate(chips)]
            return cps + [_remote(e_ref, eo_ref.at[me], sems, 6 * n + 3 * b + k, (*xy, c))
                          for b, (e_ref, eo_ref) in enumerate(zip(in_refs[n:], out_refs[n:]))
                          for k, (_, xy) in enumerate(chips)]

        def passed():
            return [_remote(o_ref.at[chip, halves(o_ref)[0]], o_ref.at[chip, halves(o_ref)[0]], sems, 6 * a + 3 + k, sibling)
                    for a, o_ref in enumerate(out_refs[:n]) for k, (chip, _) in enumerate(chips)]

        def start():
            for cp in first():
                cp.start()

        def forward():
            for a, o_ref in enumerate(out_refs[:n]):
                for k, (chip, xy) in enumerate(chips):
                    landed = o_ref.at[chip, halves(o_ref)[0]]
                    _remote(landed, landed, sems, 6 * a + k, (*xy, c)).wait_recv()
            for cp in passed():
                cp.start()

        def finish():
            for a, o_ref in enumerate(out_refs[:n]):
                for k, (chip, _) in enumerate(chips):
                    landed = o_ref.at[chip, halves(o_ref)[1]]
                    _remote(landed, landed, sems, 6 * a + 3 + k, sibling).wait_recv()
            for b, (e_ref, eo_ref) in enumerate(zip(in_refs[n:], out_refs[n:])):
                for k, (chip, xy) in enumerate(chips):
                    _remote(e_ref, eo_ref.at[chip], sems, 6 * n + 3 * b + k, (*xy, c)).wait_recv()
            for cp in first() + passed():
                cp.wait_send()

        return start, forward, finish


class _CoreExchange:
    def __init__(self, gs):
        self.arrays = list(gs)
        self.out_shape = [jax.ShapeDtypeStruct((N_CHIPS, g.shape[1] // 2, g.shape[2]), F32) for g in gs]
        self.scratch = [pltpu.SemaphoreType.DMA((len(gs),)), pltpu.SemaphoreType.DMA((len(gs),))]

    def steps(self, g_refs, sib_refs, sems):
        x, y, c = _place()

        def copies():
            return [_remote(g_ref.at[:, pl.ds((1 - c) * (g_ref.shape[1] // 2), g_ref.shape[1] // 2), :], sib_ref, sems, a,
                            (x, y, 1 - c)) for a, (g_ref, sib_ref) in enumerate(zip(g_refs, sib_refs))]

        def start():
            for cp in copies():
                cp.start()

        def finish():
            for cp in copies():
                cp.wait_recv()
            for cp in copies():
                cp.wait_send()

        return start, finish


class _ChipExchange:
    def __init__(self, ps):
        self.arrays = list(ps)
        self.out_shape = [jax.ShapeDtypeStruct(p.shape, p.dtype) for p in ps]
        self.scratch = [pltpu.SemaphoreType.DMA((3 * len(ps),)), pltpu.SemaphoreType.DMA((3 * len(ps),))]

    def steps(self, p_refs, out_refs, sems):
        x, y, c = _place()
        me = 2 * x + y
        chips = _other_chips(x, y)

        def copies():
            return [_remote(p_ref.at[chip], o_ref.at[me], sems, 3 * a + k, (*xy, c))
                    for a, (p_ref, o_ref) in enumerate(zip(p_refs, out_refs)) for k, (chip, xy) in enumerate(chips)]

        def start():
            for cp in copies():
                cp.start()

        def finish():
            for a, (p_ref, o_ref) in enumerate(zip(p_refs, out_refs)):
                for k, (chip, xy) in enumerate(chips):
                    _remote(p_ref.at[chip], o_ref.at[chip], sems, 3 * a + k, (*xy, c)).wait_recv()
            for cp in copies():
                cp.wait_send()

        return start, finish


def _exchange_cores(gs, name, small=None):
    n = len(gs)
    m = 0 if small is None else 1

    def body(*refs):
        g_refs, sib_refs = refs[:n], refs[n + m:2 * n + m]
        sems = refs[2 * (n + m):]
        x, y, c = _place()
        me = 4 * x + 2 * y + c
        copies = []
        for a, (g_ref, sib_ref) in enumerate(zip(g_refs, sib_refs)):
            rh = g_ref.shape[1] // 2
            copies.append(_remote(g_ref.at[:, pl.ds((1 - c) * rh, rh), :], sib_ref, sems, a, (x, y, 1 - c)))
        if m:
            small_ref, all_ref = refs[n], refs[2 * n + m]
            k = n
            for fx in (0, 1):
                for fy in (0, 1):
                    for fc in (0, 1):
                        if fx or fy or fc:
                            to = (1 - x if fx else x, 1 - y if fy else y, 1 - c if fc else c)
                            copies.append(_remote(small_ref, all_ref.at[me], sems, k, to))
                            k += 1
        for cp in copies:
            cp.start()
        for cp in copies:
            cp.wait_recv()
        for cp in copies:
            cp.wait_send()

    n_copies = n + m * (N_DEV - 1)
    args = list(gs) + ([small] if m else [])
    return pl.pallas_call(
        body, name=name, in_specs=[ANY] * (n + m), out_specs=[ANY] * (n + m),
        out_shape=[jax.ShapeDtypeStruct((N_CHIPS, g.shape[1] // 2, g.shape[2]), F32) for g in gs]
        + ([jax.ShapeDtypeStruct((N_DEV,) + small.shape, F32)] if m else []),
        scratch_shapes=[pltpu.SemaphoreType.DMA((n_copies,)), pltpu.SemaphoreType.DMA((n_copies,))],
    )(*args)


def _share_halves(ghs, name):
    n = len(ghs)

    def body(*refs):
        gh_refs, out_refs, sems = refs[:n], refs[n:2 * n], refs[2 * n:]
        x, y, c = _place()
        copies = [_remote(gh_ref, o_ref, sems, a, (x, y, 1 - c)) for a, (gh_ref, o_ref) in enumerate(zip(gh_refs, out_refs))]
        for cp in copies:
            cp.start()
        for cp in copies:
            cp.wait_recv()
        for cp in copies:
            cp.wait_send()

    return pl.pallas_call(
        body, name=name, in_specs=[ANY] * n, out_specs=[ANY] * n,
        out_shape=[jax.ShapeDtypeStruct(g.shape, g.dtype) for g in ghs],
        scratch_shapes=[pltpu.SemaphoreType.DMA((n,)), pltpu.SemaphoreType.DMA((n,))],
    )(*ghs)


EW_BLOCK_BYTES = 2 * 1024 * 1024


def _row_block(rows, cols, mult=8):
    fits = [b for b in range(mult, rows + 1, mult) if rows % b == 0 and b * cols * 4 <= EW_BLOCK_BYTES]
    return max(fits) if fits else mult


def _add2(a, b, name):
    R, C = a.shape
    rows = _row_block(R, C, mult=16)
    spec = pl.BlockSpec((rows, C), lambda i: (i, 0))

    def body(a_ref, b_ref, o_ref):
        o_ref[...] = (a_ref[...] + b_ref[...]).astype(BF16)

    return pl.pallas_call(
        body, name=name, grid=(R // rows,), in_specs=[spec, spec], out_specs=spec,
        out_shape=jax.ShapeDtypeStruct(a.shape, BF16),
        compiler_params=_cparams(("parallel",)),
    )(a, b)


def _sum_leading(a, name):
    n, R, C = a.shape
    rows = _row_block(R, n * C, mult=16 if a.dtype == BF16 else 8)

    def body(a_ref, o_ref):
        acc = a_ref[0].astype(F32)
        for j in range(1, n):
            acc = acc + a_ref[j].astype(F32)
        o_ref[...] = acc

    return pl.pallas_call(
        body, name=name, grid=(R // rows,),
        in_specs=[pl.BlockSpec((n, rows, C), lambda i: (0, i, 0))],
        out_specs=pl.BlockSpec((rows, C), lambda i: (i, 0)),
        out_shape=jax.ShapeDtypeStruct((R, C), F32),
        compiler_params=_cparams(("parallel",)),
    )(a)


def _adamw(w, g, m, v, name):
    R, C = w.shape
    rows = _row_block(R, C)
    spec = pl.BlockSpec((rows, C), lambda i: (i, 0))

    def body(w_ref, g_ref, m_ref, v_ref, d_ref, mo_ref, vo_ref):
        gv = g_ref[...]
        mn = ADAM_B1 * m_ref[...] + (1.0 - ADAM_B1) * gv
        vn = ADAM_B2 * v_ref[...] + (1.0 - ADAM_B2) * (gv * gv)
        m_hat = mn / (1.0 - ADAM_B1 ** ADAM_STEP)
        v_hat = vn / (1.0 - ADAM_B2 ** ADAM_STEP)
        d_ref[...] = -ADAM_LR * (m_hat / (jnp.sqrt(v_hat) + ADAM_EPS) + ADAM_WD * w_ref[...])
        mo_ref[...] = mn
        vo_ref[...] = vn

    return pl.pallas_call(
        body, name=name, grid=(R // rows,), in_specs=[spec] * 4, out_specs=[spec] * 3,
        out_shape=[jax.ShapeDtypeStruct((R, C), F32)] * 3,
        compiler_params=_cparams(("parallel",)),
    )(w, g, m, v)


BIG = ("w_in", "w_branch_a", "w_branch_b", "w_out", "w_ffn_up", "w_ffn_down")
COL_SHARDED = {"w_in": True, "w_branch_a": True, "w_branch_b": True, "w_out": False, "w_ffn_up": True,
               "w_ffn_down": False}
CONV_W_COLS = 2 * D_FF // N_CHIPS
SMALL_REPLICATED = (("norm1_g", D_MODEL), ("q_norm_g", HEAD_DIM), ("k_norm_g", HEAD_DIM),
                    ("rel_bias", N_HEADS * N_REL), ("norm2_g", D_MODEL), ("ffn_conv_b", 2 * D_FF))
SMALL_GRADS = SMALL_REPLICATED + (("ffn_conv_w", 3 * 2 * D_FF),)
SMALL_OWN = SMALL_REPLICATED + (("ffn_conv_w", 3 * CONV_W_COLS),)
SMALL_GRAD_ROWS = 32
SMALL_OWN_ROWS = 16


def _whole(name, stacked):
    return jnp.concatenate(list(stacked), axis=1) if COL_SHARDED[name] else stacked.reshape(-1, stacked.shape[2])


def _pack_small(vals, sizes, rows):
    flat = jnp.concatenate([vals[n].reshape(-1) for n, _ in sizes])
    return jnp.pad(flat, (0, rows * PACK_COLS - flat.shape[0])).reshape(rows, PACK_COLS)


def _unpack_small(packed, sizes):
    flat, out, o = packed.reshape(-1), {}, 0
    for n, sz in sizes:
        out[n] = flat[o:o + sz]
        o += sz
    return out


def kernel(x, norm1_g, w_in, q_norm_g, k_norm_g, rel_bias, w_branch_a, w_branch_b, w_out, norm2_g, w_ffn_up, ffn_conv_w, ffn_conv_b, w_ffn_down, loss_target, m_norm1_g, m_w_in, m_q_norm_g, m_k_norm_g, m_rel_bias, m_w_branch_a, m_w_branch_b, m_w_out, m_norm2_g, m_w_ffn_up, m_ffn_conv_w, m_ffn_conv_b, m_w_ffn_down, v_norm1_g, v_w_in, v_q_norm_g, v_k_norm_g, v_rel_bias, v_w_branch_a, v_w_branch_b, v_w_out, v_norm2_g, v_w_ffn_up, v_ffn_conv_w, v_ffn_conv_b, v_w_ffn_down):
    w_big = {"w_in": w_in[0], "w_branch_a": w_branch_a[0], "w_branch_b": w_branch_b[0], "w_out": w_out[0],
             "w_ffn_up": w_ffn_up[0], "w_ffn_down": w_ffn_down[0]}
    m_big = {"w_in": m_w_in[0], "w_branch_a": m_w_branch_a[0], "w_branch_b": m_w_branch_b[0], "w_out": m_w_out[0],
             "w_ffn_up": m_w_ffn_up[0], "w_ffn_down": m_w_ffn_down[0]}
    v_big = {"w_in": v_w_in[0], "w_branch_a": v_w_branch_a[0], "w_branch_b": v_w_branch_b[0], "w_out": v_w_out[0],
             "w_ffn_up": v_w_ffn_up[0], "w_ffn_down": v_w_ffn_down[0]}
    xs, tgt = x[0], loss_target[0]

    xi, yi, ci = _place()
    chip = 2 * xi + yi

    def with_own(stacked, own):
        return lax.dynamic_update_slice(stacked, own[None], (chip,) + (0,) * own.ndim)

    shards_bf = {n: w_big[n].astype(BF16) for n in BIG}
    conv_own = jnp.pad(ffn_conv_w[0], ((0, 8 - ffn_conv_w.shape[1]), (0, 0)))
    later = [n for n in BIG if n != "w_in"]

    hn, hn_t, w_in_g = _rms_fwd(xs, norm1_g, "rms1", gather=_Gather([shards_bf["w_in"]]))
    w_in_f = _whole("w_in", with_own(w_in_g, shards_bf["w_in"]))
    w_in_t = w_in_f.T
    gq = jnp.tile(q_norm_g, (1, N_HEADS))
    gk = jnp.tile(k_norm_g, (1, N_HEADS))
    qk, qa, ka = _proj_qknorm_fwd(hn, w_in_f[:, :2 * WIDTH], gq, gk, "proj_qk_qknorm")
    vqkv = _matmul(hn, w_in_f[:, 2 * WIDTH:6 * WIDTH], BF16, "proj_vqkv")
    gates = _matmul(hn, w_in_f[:, 6 * WIDTH:], BF16, "proj_gates")
    per = WIDTH // PAIR
    b_offs = (per, 2 * per, 3 * per)
    tab = _bias_table(jnp.pad(rel_bias[0], ((0, 0), (0, REL_PAD - N_REL))), "bias_table")
    out_a, out_a_t, *gathered = _ca_fwd(qa, ka, vqkv, tab, "chunk_attn",
                                        gather=_Gather([shards_bf[n] for n in later], [conv_own]))
    full = {n: _whole(n, with_own(g, shards_bf[n])) for n, g in zip(later, gathered)}
    conv_w = jnp.concatenate(list(with_own(gathered[-1], conv_own)[:, :3]), axis=1)
    w_a, w_b, w_o, w_up, w_dn = (full[n] for n in later)
    w_a_t, w_b_t, w_o_t, w_up_t, w_dn_t = (w.T for w in (w_a, w_b, w_o, w_up, w_dn))
    out_b, out_b_f32, out_b_t = _sb_fwd(vqkv, vqkv, vqkv, "stick_attn", b_offs)
    y_a, y_b, mixed, mixed_t = _branch_mix_fwd(out_a, w_a, out_b, w_b, gates, "branch_mix")
    x2, hn2, hn2_t = _proj_rms_fwd(mixed, w_o, xs, norm2_g, "out_proj_rms2")
    hid = _matmul(hn2, w_up, BF16, "ffn_up")
    act, act_t = _convglu_fwd(hid, conv_w, ffn_conv_b, "convglu")
    dy, dyb, sq = _down_and_loss(act, w_dn, x2, tgt, "ffn_down_loss")
    loss = lax.psum(0.5 / D_MODEL * jnp.sum(sq), ("x", "y", "c"))

    dact = _matmul(dyb, w_dn_t, BF16, "d_act")
    d_w_dn = _matmul(act_t, dyb, F32, "d_w_down")
    dhg, dhu, dcwg, dcwu, dcbg, dcbu = _convglu_bwd(hid, dact, conv_w, ffn_conv_b, "convglu_bwd")
    half_chips = N_CHIPS // 2
    d_w_up = jnp.concatenate([_matmul(hn2_t, dhg, F32, "d_w_up_gate", slabs=half_chips),
                              _matmul(hn2_t, dhu, F32, "d_w_up_up", slabs=half_chips)], axis=0)
    dx2, dx2b, d_norm2 = _rms_bwd([(dhg, w_up_t[:D_FF]), (dhu, w_up_t[D_FF:])], x2, norm2_g, dy, "d_hn2_rms2_bwd")
    d_w_o = _matmul(mixed_t, dx2b, F32, "d_w_out")
    dgates, dya, dyb_b = _mix_bwd(dx2b, w_o_t, gates, y_a, y_b, "d_mixed_mix_bwd")
    d_w_a, d_w_b = (_matmul(o_t, d, F32, nm).reshape(WIDTH, N_CHIPS, -1).transpose(1, 0, 2)
                    for o_t, d, nm in ((out_a_t, dya, "d_w_branch_a"), (out_b_t, dyb_b, "d_w_branch_b")))
    do_a = _matmul(dya, w_a_t, BF16, "d_out_a")
    do_b = _matmul(dyb_b, w_b_t, BF16, "d_out_b")

    def core_sums(names, gs, sibs):
        out = {}
        for n, g, sib in zip(names, gs, sibs):
            rh, cols = sib.shape[1], sib.shape[2]
            mine = lax.dynamic_slice_in_dim(g, ci * rh, rh, axis=1)
            out[n] = _add2(mine.reshape(-1, cols), sib.reshape(-1, cols), "sum_cores_" + n).reshape(sib.shape)
        return out

    grads_full = {"w_branch_a": d_w_a, "w_branch_b": d_w_b, "w_ffn_up": d_w_up,
                  "w_out": d_w_o.reshape(N_CHIPS, -1, D_MODEL), "w_ffn_down": d_w_dn.reshape(N_CHIPS, -1, D_MODEL)}
    early = [grads_full[n] for n in later]
    dqb, dkb, dvb, *sibs = _sb_bwd(vqkv, vqkv, vqkv, out_b_f32, do_b, "stick_attn_bwd", b_offs,
                                   exchange=_CoreExchange(early))
    chip_parts = core_sums(later, early, sibs)
    dqa_n, dka_n, dva, dtab, *parts_early = _ca_bwd(qa, ka, vqkv, do_a, tab, "chunk_attn_bwd",
                                                    exchange=_ChipExchange([chip_parts[n] for n in later]))
    parts = dict(zip(later, parts_early))
    d_rel = _bias_table_bwd(dtab, "bias_table_bwd")[:, :N_REL]
    dqa, dka, dgq, dgk = _qknorm_bwd(qk, gq, gk, dqa_n, dka_n, "qknorm_bwd")
    pieces = (("qa", dqa), ("ka", dka), ("va", dva), ("qb", dqb), ("kb", dkb), ("vb", dvb), ("gates", dgates))
    d_w_in = jnp.concatenate([_matmul(hn_t, d, F32, "d_w_in_" + nm) for nm, d in pieces], axis=1)
    d_w_in = d_w_in.reshape(D_MODEL, N_CHIPS, -1).transpose(1, 0, 2)
    chip_parts.update(core_sums(["w_in"], [d_w_in], _exchange_cores([d_w_in], "exchange_cores_w_in")))
    offsets = [sum(d.shape[1] for _, d in pieces[:k]) for k in range(len(pieces))]
    dx, _, d_norm1, parts["w_in"] = _rms_bwd([(d, w_in_t[o:o + d.shape[1]]) for (_, d), o in zip(pieces, offsets)],
                                             xs, norm1_g, dx2, "d_hn_rms1_bwd",
                                             exchange=_ChipExchange([chip_parts["w_in"]]))

    small_g = _pack_small({"norm1_g": d_norm1, "q_norm_g": dgq.reshape(N_HEADS, HEAD_DIM).sum(0),
                           "k_norm_g": dgk.reshape(N_HEADS, HEAD_DIM).sum(0), "rel_bias": d_rel,
                           "norm2_g": d_norm2, "ffn_conv_b": jnp.concatenate([dcbg, dcbu], axis=1),
                           "ffn_conv_w": jnp.concatenate([dcwg, dcwu], axis=1)}, SMALL_GRADS, SMALL_GRAD_ROWS)
    (small_all,) = _exchange_cores([], "exchange_small", small=small_g)
    g_halves = [_sum_leading(with_own(parts[n], lax.dynamic_index_in_dim(chip_parts[n], chip, 0, keepdims=False)),
                             "sum_chips_" + n) for n in BIG]
    g_others = _share_halves(g_halves, "share_halves")
    grads = {n: jnp.concatenate([jnp.where(ci == 0, mine, other), jnp.where(ci == 0, other, mine)], axis=0)
             for n, mine, other in zip(BIG, g_halves, g_others)}
    small_all = lax.dynamic_update_slice(small_all, small_g[None], (4 * xi + 2 * yi + ci, 0, 0))
    small_sum = _unpack_small(_sum_leading(small_all, "sum_small"), SMALL_GRADS)
    small_sum["ffn_conv_w"] = lax.dynamic_slice_in_dim(small_sum["ffn_conv_w"].reshape(3, 2 * D_FF),
                                                       chip * CONV_W_COLS, CONV_W_COLS, axis=1)

    deltas, new_m, new_v = {}, {}, {}
    for n in BIG:
        deltas[n], new_m[n], new_v[n] = _adamw(w_big[n], grads[n], m_big[n], v_big[n], "adamw_" + n)

    shapes = {"norm1_g": norm1_g.shape, "q_norm_g": q_norm_g.shape, "k_norm_g": k_norm_g.shape,
              "rel_bias": rel_bias.shape, "norm2_g": norm2_g.shape, "ffn_conv_b": ffn_conv_b.shape,
              "ffn_conv_w": ffn_conv_w.shape}
    small_w = {"norm1_g": norm1_g, "q_norm_g": q_norm_g, "k_norm_g": k_norm_g, "rel_bias": rel_bias,
               "norm2_g": norm2_g, "ffn_conv_b": ffn_conv_b, "ffn_conv_w": ffn_conv_w}
    small_m = {"norm1_g": m_norm1_g, "q_norm_g": m_q_norm_g, "k_norm_g": m_k_norm_g, "rel_bias": m_rel_bias,
               "norm2_g": m_norm2_g, "ffn_conv_b": m_ffn_conv_b, "ffn_conv_w": m_ffn_conv_w}
    small_v = {"norm1_g": v_norm1_g, "q_norm_g": v_q_norm_g, "k_norm_g": v_k_norm_g, "rel_bias": v_rel_bias,
               "norm2_g": v_norm2_g, "ffn_conv_b": v_ffn_conv_b, "ffn_conv_w": v_ffn_conv_w}
    ds, ms, vs = _adamw(*(_pack_small(t, SMALL_OWN, SMALL_OWN_ROWS) for t in (small_w, small_sum, small_m, small_v)),
                        "adamw_small")
    small_grads = small_sum
    ds, ms, vs = (_unpack_small(t, SMALL_OWN) for t in (ds, ms, vs))

    order = ("norm1_g", "w_in", "q_norm_g", "k_norm_g", "rel_bias", "w_branch_a", "w_branch_b", "w_out",
             "norm2_g", "w_ffn_up", "ffn_conv_w", "ffn_conv_b", "w_ffn_down")
    outs = [loss, dx[None]]
    for big, small in ((grads, small_grads), (deltas, ds), (new_m, ms), (new_v, vs)):
        for n in order:
            outs.append(big[n][None] if n in big else small[n].reshape(shapes[n]))
    return tuple(outs)
```
